```python
import math
import jax, jax.numpy as jnp
from jax import lax
import numpy as np

D_MODEL = 1024
BATCH = 16
SEQ = 256
DEPTH = 2
DEC_BATCH = 4
DEC_SEQ = 2048
PAST_LEN = 256

GRID_W = 64
CHUNK = 128
NORM_EPS = 1e-6
SSD_HEADS = 16
SSD_P = 64
SSD_INNER = SSD_HEADS * SSD_P
SSD_GROUPS = 2
SSD_HPG = SSD_HEADS // SSD_GROUPS
SSD_N = 128
SSD_CONV = 3
SSD_XBC = SSD_INNER + 2 * SSD_GROUPS * SSD_N
SSD_IN = SSD_INNER + SSD_XBC + SSD_HEADS
RWKV_HEADS = 16
RWKV_N = 64
RWKV_DIM = RWKV_HEADS * RWKV_N
W_LORA = 64
A_LORA = 64
G_LORA = 128
RWKV_IN = 3 * RWKV_DIM + 2 * W_LORA + A_LORA + G_LORA
RWKV_GN_EPS = 64e-5
AB_IN = SSD_IN + RWKV_IN
AB_OUT = SSD_INNER + RWKV_DIM
RET_HEADS = 8
RET_DK = 128
RET_DV = 256
RET_QK = RET_HEADS * RET_DK
RET_V = RET_HEADS * RET_DV
RET_IN = 2 * RET_QK + 2 * RET_V
ROPE_BASE = 10000.0
N_EXPERTS = 16
EXPERT_FF = 1536
EC_CAPACITY = 2

kernel_name = 'hybrid_ssd_rwkv7_retention_ec_diffusion_step'


def rms_norm(x, g):
    xf = x.astype(jnp.float32)
    y = xf * lax.rsqrt(jnp.mean(xf * xf, -1, keepdims=True) + NORM_EPS)
    return (y * g).astype(x.dtype)


def group_rms_norm(x, g, groups):
    xf = x.astype(jnp.float32).reshape(x.shape[:-1] + (groups, -1))
    y = (xf * lax.rsqrt(jnp.mean(xf * xf, -1, keepdims=True) + NORM_EPS)).reshape(x.shape)
    return (y * g).astype(x.dtype)


def head_layer_norm(x, w, bias, eps):
    xf = x.astype(jnp.float32)
    mu = jnp.mean(xf, -1, keepdims=True)
    var = jnp.mean(jnp.square(xf - mu), -1, keepdims=True)
    h, n = x.shape[-2], x.shape[-1]
    y = (xf - mu) * lax.rsqrt(var + eps) * w.reshape(h, n) + bias.reshape(h, n)
    return y.astype(x.dtype)


def depthwise_conv_centred(x, w, bias):
    k = w.shape[0]
    y = lax.conv_general_dilated(x, w[:, None, :].astype(x.dtype), window_strides=(1,),
                                 padding=[(k // 2, k // 2)], dimension_numbers=('NWC', 'WIO', 'NWC'),
                                 feature_group_count=x.shape[-1])
    return y + bias.astype(x.dtype)


def token_shift_centred(x, mu_prev, mu_next):
    prev = jnp.pad(x, ((0, 0), (1, 0), (0, 0)))[:, :-1]
    nxt = jnp.pad(x, ((0, 0), (0, 1), (0, 0)))[:, 1:]
    return x + mu_prev * (prev - x) + mu_next * (nxt - x)


def rope_tables(n_tokens):
    rows = n_tokens // GRID_W
    row = jnp.repeat(jnp.arange(rows), GRID_W).astype(jnp.float32)
    col = jnp.tile(jnp.arange(GRID_W), rows).astype(jnp.float32)
    n_f = RET_DK // 4
    inv = ROPE_BASE ** (-jnp.arange(n_f, dtype=jnp.float32) / n_f)
    ang = jnp.concatenate([row[:, None] * inv, col[:, None] * inv], -1)
    return jnp.cos(ang), jnp.sin(ang)


def apply_rope(x, cos, sin):
    half = x.shape[-1] // 2
    x1, x2 = x[..., :half], x[..., half:]
    c, s = cos[None, :, None, :], sin[None, :, None, :]
    return jnp.concatenate([x1 * c - x2 * s, x1 * s + x2 * c], -1).astype(x.dtype)


def chunked_decay_scan(q, k, v, log_a, s0):
    b, l, g, n = q.shape
    j, p = v.shape[3], v.shape[4]
    nc = l // CHUNK
    f32 = jnp.float32
    qc = q.reshape(b, nc, CHUNK, g, n).astype(f32)
    kc = k.reshape(b, nc, CHUNK, g, n).astype(f32)
    vc = v.reshape(b, nc, CHUNK, g, j, p).astype(f32)
    acs = jnp.cumsum(log_a.reshape(b, nc, CHUNK, g, j).astype(f32), axis=2)
    causal = jnp.tril(jnp.ones((CHUNK, CHUNK), bool))[None, None, :, :, None, None]
    seg = acs[:, :, :, None] - acs[:, :, None, :]
    decay = jnp.exp(jnp.where(causal, seg, -jnp.inf))
    scores = jnp.einsum('bctgn,bcsgn->bctsg', qc, kc)
    y_diag = jnp.einsum('bctsgj,bcsgjp->bctgjp', scores[..., None] * decay, vc)
    to_end = jnp.exp(acs[:, :, -1:] - acs)
    chunk_states = jnp.einsum('bctgn,bctgjp->bcgjnp', kc, vc * to_end[..., None])
    chunk_decay = jnp.exp(acs[:, :, -1])

    def step(s, inp):
        cs, cd = inp
        return s * cd[..., None, None] + cs, s

    s_final, s_in = lax.scan(step, s0.astype(f32),
                             (jnp.moveaxis(chunk_states, 1, 0), jnp.moveaxis(chunk_decay, 1, 0)))
    s_in = jnp.moveaxis(s_in, 0, 1)
    y_off = jnp.einsum('bctgn,bcgjnp->bctgjp', qc, s_in) * jnp.exp(acs)[..., None]
    y = (y_diag + y_off).reshape(b, l, g, j, p)
    return y.astype(v.dtype), s_final.astype(v.dtype)


def directional_scan(q, k, v, log_a, s0, reverse):
    if reverse:
        q, k, v, log_a = (jnp.flip(t, axis=1) for t in (q, k, v, log_a))
    y, s_final = chunked_decay_scan(q, k, v, log_a, s0)
    if reverse:
        y = jnp.flip(y, axis=1)
    return y, s_final


def rwkv7_scan(r, w, k, v, kk, a, s0, reverse):
    f32 = jnp.float32
    seq = tuple(jnp.moveaxis(t.astype(f32), 1, 0) for t in (r, w, k, v, -kk, kk * a))

    def step(S, inp):
        r_t, w_t, k_t, v_t, a_t, b_t = inp
        sa = jnp.einsum('bhvk,bhk->bhv', S, a_t)
        S = S * w_t[:, :, None, :] + sa[..., None] * b_t[:, :, None, :] + v_t[..., None] * k_t[:, :, None, :]
        return S, jnp.einsum('bhvk,bhk->bhv', S, r_t)

    s_final, ys = lax.scan(step, s0.astype(f32), seq, reverse=reverse)
    return jnp.moveaxis(ys, 0, 1).astype(r.dtype), s_final.astype(r.dtype)


def ssd_rwkv_mixer(h, s_ssd, s_rwkv, e, p):
    b, l, _ = h.shape
    f32 = jnp.float32
    proj = h @ p['ab_w_in'][e]
    z, xbc, dt_raw, rw = jnp.split(proj, [SSD_INNER, SSD_INNER + SSD_XBC, SSD_IN], axis=-1)

    xbc = jax.nn.silu(depthwise_conv_centred(xbc, p['ssd_conv_w'][e], p['ssd_conv_b'][e]))
    xs, bm, cm = jnp.split(xbc, [SSD_INNER, SSD_INNER + SSD_GROUPS * SSD_N], axis=-1)
    xs = xs.reshape(b, l, SSD_GROUPS, SSD_HPG, SSD_P)
    bm = bm.reshape(b, l, SSD_GROUPS, SSD_N)
    cm = cm.reshape(b, l, SSD_GROUPS, SSD_N)
    a_neg = -jnp.exp(p['ssd_a_log'][e].astype(f32))
    y_ssd = xs * p['ssd_d'][e].reshape(SSD_GROUPS, SSD_HPG, 1).astype(xs.dtype)
    ssd_final = []
    for d in range(2):
        dt = jax.nn.softplus(dt_raw.astype(f32) + p['ssd_dt_bias'][e, d]).reshape(b, l, SSD_GROUPS, SSD_HPG)
        s0 = s_ssd[:, d].reshape(b, SSD_GROUPS, SSD_HPG, SSD_N, SSD_P)
        y_d, s_d = directional_scan(cm, bm, xs * dt[..., None].astype(xs.dtype),
                                    dt * a_neg[d].reshape(SSD_GROUPS, SSD_HPG), s0, reverse=(d == 1))
        y_ssd = y_ssd + y_d
        ssd_final.append(s_d.reshape(b, SSD_HEADS, SSD_N, SSD_P))
    y_ssd = group_rms_norm(y_ssd.reshape(b, l, SSD_INNER) * jax.nn.silu(z), p['ssd_norm_g'][e], SSD_GROUPS)

    rw = token_shift_centred(rw, p['rwkv_mu_prev'][e], p['rwkv_mu_next'][e])
    c = RWKV_DIM
    r, k, v, wl, al, gl = jnp.split(rw, [c, 2 * c, 3 * c, 3 * c + 2 * W_LORA, 3 * c + 2 * W_LORA + A_LORA], axis=-1)
    heads = (b, l, RWKV_HEADS, RWKV_N)
    gate = jax.nn.sigmoid(gl) @ p['rwkv_g2'][e]
    a_lora = al @ p['rwkv_a2'][e]
    kk = (k * p['rwkv_k_k'][e]).reshape(heads).astype(f32)
    kk = kk * lax.rsqrt(jnp.sum(kk * kk, -1, keepdims=True) + 1e-12)
    rh = r.reshape(heads)
    vh = v.reshape(heads)
    outs, bonuses, rwkv_final = [], [], []
    for d in range(2):
        w_log = -jax.nn.softplus(-(p['rwkv_w0'][e, d] + jnp.tanh(wl[..., d * W_LORA:(d + 1) * W_LORA]) @ p['rwkv_w2'][e, d])) - 0.5
        decay = jnp.exp(-jnp.exp(w_log.astype(f32)))
        a = jax.nn.sigmoid((p['rwkv_a0'][e, d] + a_lora).astype(f32))
        kd = (k.astype(f32) * (1.0 + (a - 1.0) * p['rwkv_k_a'][e])).reshape(heads)
        o_d, s_d = rwkv7_scan(rh, decay.reshape(heads), kd, vh, kk, a.reshape(heads), s_rwkv[:, d], reverse=(d == 1))
        outs.append(o_d)
        rwkv_final.append(s_d)
        bonuses.append(jnp.sum(rh * kd * p['rwkv_r_k'][e], -1, keepdims=True) * vh)
    o = head_layer_norm(outs[0] + outs[1], p['rwkv_ln_w'][e], p['rwkv_ln_b'][e], RWKV_GN_EPS)
    o = (o + bonuses[0] + bonuses[1]).reshape(b, l, RWKV_DIM) * gate

    out = jnp.concatenate([y_ssd, o.astype(y_ssd.dtype)], -1) @ p['ab_w_out'][e]
    return out, jnp.stack(ssd_final, 1), jnp.stack(rwkv_final, 1)


def retention_mixer(h, s_ret, rope, o, p):
    b, l, _ = h.shape
    proj = h @ p['ret_w_in'][o]
    q, k, v, g = jnp.split(proj, [RET_QK, 2 * RET_QK, 2 * RET_QK + RET_V], axis=-1)
    q = q.reshape(b, l, RET_HEADS, RET_DK)
    k = k.reshape(b, l, RET_HEADS, RET_DK)
    if rope is not None:
        q = apply_rope(q, *rope)
        k = apply_rope(k, *rope)
    k = k * (RET_DK ** -0.5)
    v = v.reshape(b, l, RET_HEADS, 1, RET_DV)
    log_gamma = jax.nn.log_sigmoid(p['ret_decay_logit'][o].astype(jnp.float32))
    outs = [directional_scan(q, k, v, jnp.broadcast_to(log_gamma[d][:, None], (b, l, RET_HEADS, 1)),
                             s_ret[:, d][:, :, None], reverse=(d == 1)) for d in range(2)]
    y = (outs[0][0] + outs[1][0]).reshape(b, l, RET_HEADS, RET_DV)
    y = head_layer_norm(y, p['ret_norm_w'][o], p['ret_norm_b'][o], 1e-5)
    out = (jax.nn.silu(g) * y.reshape(b, l, RET_V)) @ p['ret_w_out'][o]
    return out, jnp.stack([outs[0][1][:, :, 0], outs[1][1][:, :, 0]], axis=1)


def expert_choice_ffn(h, i, p):
    b, t, _ = h.shape
    cap = EC_CAPACITY * t // N_EXPERTS
    aff = jax.nn.softmax((h @ p['router_w'][i]).astype(jnp.float32), axis=-1)
    gate, idx = lax.top_k(jnp.swapaxes(aff, 1, 2), cap)
    bidx = jnp.arange(b)[:, None, None]
    xe = h[bidx, idx]
    hid = jax.nn.silu(jnp.einsum('becd,edf->becf', xe, p['exp_w_gate'][i])) * jnp.einsum('becd,edf->becf', xe, p['exp_w_up'][i])
    ye = jnp.einsum('becf,efd->becd', hid, p['exp_w_down'][i]) * gate[..., None].astype(h.dtype)
    return jnp.zeros_like(h).at[bidx, idx].add(ye.astype(h.dtype))


def trunk(x, cond, s_ssd, s_rwkv, s_ret, rope, p):
    new_ssd, new_rwkv, new_ret = [], [], []
    cond_act = jax.nn.silu(cond.astype(jnp.float32))
    for i in range(DEPTH):
        mod = (cond_act @ p['mod_w'][i] + p['mod_b'][i]).astype(x.dtype)[:, None, :]
        sh1, sc1, g1, sh2, sc2, g2 = jnp.split(mod, 6, axis=-1)
        hn = rms_norm(x, p['norm1_g'][i]) * (1 + sc1) + sh1
        if i % 2 == 0:
            y, st_ssd, st_rwkv = ssd_rwkv_mixer(hn, s_ssd[:, i // 2], s_rwkv[:, i // 2], i // 2, p)
            new_ssd.append(st_ssd)
            new_rwkv.append(st_rwkv)
        else:
            y, st_ret = retention_mixer(hn, s_ret[:, i // 2], rope, i // 2, p)
            new_ret.append(st_ret)
        x = x + g1 * y.astype(x.dtype)
        hn = rms_norm(x, p['norm2_g'][i]) * (1 + sc2) + sh2
        x = x + g2 * expert_choice_ffn(hn, i, p).astype(x.dtype)
    return rms_norm(x, p['final_norm_g']), jnp.stack(new_ssd, 1), jnp.stack(new_rwkv, 1), jnp.stack(new_ret, 1)


def setup_inputs(seed: int = 0) -> dict:
    key = jax.random.key(seed)
    ks = iter(jax.random.split(key, 48))
    f32 = jnp.float32
    n_even = (DEPTH + 1) // 2
    n_odd = DEPTH // 2
    D = D_MODEL

    def nrm(shape, scale):
        return jax.random.normal(next(ks), shape, f32) * scale

    def unif(shape, lo, hi):
        return jax.random.uniform(next(ks), shape, f32, lo, hi)

    dt0 = jnp.exp(unif((n_even, 2, SSD_HEADS), math.log(1e-3), math.log(1e-1)))
    gam = 1.0 - 2.0 ** (-5.0 - jnp.arange(RET_HEADS, dtype=f32))
    ret_logit = jnp.log(gam) - jnp.log1p(-gam)
    return {
        'x_prompt': nrm((BATCH, SEQ, D), 1.0),
        'x_sample': nrm((DEC_BATCH, DEC_SEQ, D), 1.0),
        'state_ssd': nrm((DEC_BATCH, n_even, 2, SSD_HEADS, SSD_N, SSD_P), 0.5),
        'state_rwkv': nrm((DEC_BATCH, n_even, 2, RWKV_HEADS, RWKV_N, RWKV_N), 0.3),
        'state_ret': nrm((DEC_BATCH, n_odd, 2, RET_HEADS, RET_DK, RET_DV), 0.5),
        'c': nrm((DEC_BATCH, D), 1.0),
        'c_ctx': nrm((D,), 1.0),
        'mod_w': nrm((DEPTH, D, 6 * D), 0.5 * D ** -0.5),
        'mod_b': nrm((DEPTH, 6 * D), 0.02),
        'norm1_g': 1.0 + nrm((DEPTH, D), 0.02),
        'norm2_g': 1.0 + nrm((DEPTH, D), 0.02),
        'router_w': nrm((DEPTH, D, N_EXPERTS), D ** -0.5),
        'exp_w_gate': nrm((DEPTH, N_EXPERTS, D, EXPERT_FF), D ** -0.5),
        'exp_w_up': nrm((DEPTH, N_EXPERTS, D, EXPERT_FF), D ** -0.5),
        'exp_w_down': nrm((DEPTH, N_EXPERTS, EXPERT_FF, D), EXPERT_FF ** -0.5),
        'ab_w_in': nrm((n_even, D, AB_IN), D ** -0.5),
        'ab_w_out': nrm((n_even, AB_OUT, D), AB_OUT ** -0.5),
        'ssd_conv_w': nrm((n_even, SSD_CONV, SSD_XBC), SSD_CONV ** -0.5),
        'ssd_conv_b': nrm((n_even, SSD_XBC), 0.02),
        'ssd_dt_bias': dt0 + jnp.log(-jnp.expm1(-dt0)),
        'ssd_a_log': jnp.log(unif((n_even, 2, SSD_HEADS), 1.0, 16.0)),
        'ssd_d': 1.0 + nrm((n_even, SSD_HEADS), 0.1),
        'ssd_norm_g': 1.0 + nrm((n_even, SSD_INNER), 0.02),
        'rwkv_mu_prev': unif((n_even, RWKV_IN), 0.0, 0.5),
        'rwkv_mu_next': unif((n_even, RWKV_IN), 0.0, 0.5),
        'rwkv_w0': unif((n_even, 2, RWKV_DIM), -6.0, -1.0),
        'rwkv_w2': nrm((n_even, 2, W_LORA, RWKV_DIM), 0.1 * W_LORA ** -0.5),
        'rwkv_a0': nrm((n_even, 2, RWKV_DIM), 0.1),
        'rwkv_a2': nrm((n_even, A_LORA, RWKV_DIM), 0.5 * A_LORA ** -0.5),
        'rwkv_g2': nrm((n_even, G_LORA, RWKV_DIM), G_LORA ** -0.5),
        'rwkv_k_k': 0.85 + nrm((n_even, RWKV_DIM), 0.02),
        'rwkv_k_a': 1.0 + nrm((n_even, RWKV_DIM), 0.02),
        'rwkv_r_k': nrm((n_even, RWKV_HEADS, RWKV_N), 0.1),
        'rwkv_ln_w': 1.0 + nrm((n_even, RWKV_DIM), 0.02),
        'rwkv_ln_b': nrm((n_even, RWKV_DIM), 0.02),
        'ret_w_in': nrm((n_odd, D, RET_IN), D ** -0.5),
        'ret_w_out': nrm((n_odd, RET_V, D), RET_V ** -0.5),
        'ret_decay_logit': ret_logit + nrm((n_odd, 2, RET_HEADS), 0.05),
        'ret_norm_w': 1.0 + nrm((n_odd, RET_V), 0.02),
        'ret_norm_b': nrm((n_odd, RET_V), 0.02),
        'final_norm_g': 1.0 + nrm((D,), 0.02),
    }


def reference(x_prompt, x_sample, state_ssd, state_rwkv, state_ret, c, c_ctx, mod_w, mod_b, norm1_g, norm2_g,
              router_w, exp_w_gate, exp_w_up, exp_w_down, ab_w_in, ab_w_out, ssd_conv_w, ssd_conv_b, ssd_dt_bias,
              ssd_a_log, ssd_d, ssd_norm_g, rwkv_mu_prev, rwkv_mu_next, rwkv_w0, rwkv_w2, rwkv_a0, rwkv_a2, rwkv_g2,
              rwkv_k_k, rwkv_k_a, rwkv_r_k, rwkv_ln_w, rwkv_ln_b, ret_w_in, ret_w_out, ret_decay_logit, ret_norm_w,
              ret_norm_b, final_norm_g):
    p = dict(mod_w=mod_w, mod_b=mod_b, norm1_g=norm1_g, norm2_g=norm2_g, router_w=router_w,
             exp_w_gate=exp_w_gate, exp_w_up=exp_w_up, exp_w_down=exp_w_down, ab_w_in=ab_w_in, ab_w_out=ab_w_out,
             ssd_conv_w=ssd_conv_w, ssd_conv_b=ssd_conv_b, ssd_dt_bias=ssd_dt_bias, ssd_a_log=ssd_a_log,
             ssd_d=ssd_d, ssd_norm_g=ssd_norm_g, rwkv_mu_prev=rwkv_mu_prev, rwkv_mu_next=rwkv_mu_next,
             rwkv_w0=rwkv_w0, rwkv_w2=rwkv_w2, rwkv_a0=rwkv_a0, rwkv_a2=rwkv_a2, rwkv_g2=rwkv_g2,
             rwkv_k_k=rwkv_k_k, rwkv_k_a=rwkv_k_a, rwkv_r_k=rwkv_r_k, rwkv_ln_w=rwkv_ln_w, rwkv_ln_b=rwkv_ln_b,
             ret_w_in=ret_w_in, ret_w_out=ret_w_out, ret_decay_logit=ret_decay_logit, ret_norm_w=ret_norm_w,
             ret_norm_b=ret_norm_b, final_norm_g=final_norm_g)
    b_ctx = x_prompt.shape[0]
    zero_ssd = jnp.zeros((b_ctx,) + state_ssd.shape[1:], x_prompt.dtype)
    zero_rwkv = jnp.zeros((b_ctx,) + state_rwkv.shape[1:], x_prompt.dtype)
    zero_ret = jnp.zeros((b_ctx,) + state_ret.shape[1:], x_prompt.dtype)
    y_prompt, new_state_ssd, new_state_rwkv, new_state_ret = trunk(
        x_prompt, c_ctx[None, :], zero_ssd, zero_rwkv, zero_ret, None, p)
    rope = rope_tables(x_sample.shape[1])
    y_sample, _, _, _ = trunk(x_sample, c, state_ssd, state_rwkv, state_ret, rope, p)
    return (y_prompt, y_sample, new_state_ssd, new_state_rwkv, new_state_ret)
```

```python
import functools
import math

import jax
import jax.numpy as jnp
from jax import lax
from jax.experimental import pallas as pl
from jax.experimental.pallas import tpu as pltpu

D_MODEL = 1024
DEPTH = 2
GRID_W = 64
CHUNK = 128
NORM_EPS = 1e-6
SSD_HEADS = 16
SSD_P = 64
SSD_INNER = SSD_HEADS * SSD_P
SSD_GROUPS = 2
SSD_HPG = SSD_HEADS // SSD_GROUPS
SSD_N = 128
SSD_XBC = SSD_INNER + 2 * SSD_GROUPS * SSD_N
SSD_IN = SSD_INNER + SSD_XBC + SSD_HEADS
RWKV_HEADS = 16
RWKV_N = 64
RWKV_DIM = RWKV_HEADS * RWKV_N
W_LORA = 64
A_LORA = 64
G_LORA = 128
RWKV_GN_EPS = 64e-5
RET_HEADS = 8
RET_DK = 128
RET_DV = 256
RET_QK = RET_HEADS * RET_DK
RET_V = RET_HEADS * RET_DV
ROPE_BASE = 10000.0
N_EXPERTS = 16
EC_CAPACITY = 2

RWKV_C = 64
RWKV_GH = 4
RWKV_GL = RWKV_GH * RWKV_N


def _split_bf16(x):
    hi = x.astype(jnp.bfloat16)
    lo = (x - hi.astype(jnp.float32)).astype(jnp.bfloat16)
    return hi, lo


def _dot(a, b, dims, passes):
    f = functools.partial(lax.dot_general, dimension_numbers=(dims, ((), ())),
                          preferred_element_type=jnp.float32)
    if passes == 1:
        return f(a.astype(jnp.bfloat16), b.astype(jnp.bfloat16))
    ah, al = _split_bf16(a)
    bh, bl = _split_bf16(b)
    return f(ah, bh) + (f(ah, bl) + f(al, bh))


def _rwkv_chunk_kernel(r_ref, v_ref, a_ref, lw_ref, k_ref, b_ref, s0_ref, y_ref, sf_ref, h_ref, *, passes):
    C, N, GH, GL = RWKV_C, RWKV_N, RWKV_GH, RWKV_GL
    d = pl.program_id(1)
    c = pl.program_id(2)
    nc = pl.num_programs(2)
    f32 = jnp.float32

    @pl.when(c == 0)
    def _():
        h_ref[...] = s0_ref[0, 0]

    sgn = 1 - 2 * d
    t_i = lax.broadcasted_iota(jnp.int32, (C, GL), 0)
    s_i = lax.broadcasted_iota(jnp.int32, (C, GL), 1) & (N - 1)
    rel = (s_i - t_i) * sgn
    strict = rel < 0
    incl = rel <= 0
    eye = (s_i == t_i).astype(f32)
    bh_r = lax.broadcasted_iota(jnp.int32, (GL, GL), 0) // N
    bh_c = lax.broadcasted_iota(jnp.int32, (GL, GL), 1) // N
    blk = bh_r == bh_c
    tt = lax.broadcasted_iota(jnp.int32, (C, C), 0)
    ss = lax.broadcasted_iota(jnp.int32, (C, C), 1)
    tri = ((ss - tt) * sgn <= 0).astype(jnp.bfloat16)

    def bd(x):
        return jnp.where(blk, jnp.concatenate([x] * GH, axis=0), 0.0)

    def mm(l, x):
        return _dot(l, bd(x), ((1,), (0,)), passes)

    def mmt(l, x):
        return _dot(l, bd(x), ((1,), (1,)), passes)

    lw = lw_ref[0, 0]
    lw_hi, lw_lo = _split_bf16(lw)
    cum = (jnp.dot(tri, lw_hi, preferred_element_type=f32)
           + jnp.dot(tri, lw_lo, preferred_element_type=f32))
    w_in = jnp.exp(cum)
    w_ex = jnp.exp(cum - lw)
    w_inv = jnp.exp(-cum)

    r_t = r_ref[0] * w_in
    a_t = a_ref[0] * w_ex
    b_t = b_ref[0, 0] * w_inv
    k_t = k_ref[0, 0] * w_inv
    v = v_ref[0]
    n_i = lax.broadcasted_iota(jnp.int32, (N, GL), 0)
    lane_i = lax.broadcasted_iota(jnp.int32, (N, GL), 1)
    diag_sel = (lane_i & (N - 1)) == n_i
    lane_head = lane_i // N
    ones_bd = blk.astype(f32)

    for g in range(RWKV_HEADS // GH):
        sl = slice(g * GL, (g + 1) * GL)
        rg, ag, bg, kg, vg = r_t[:, sl], a_t[:, sl], b_t[:, sl], k_t[:, sl], v[:, sl]
        h0 = h_ref[:, sl]
        a_ab = jnp.where(strict, mmt(ag, bg), 0.0)
        a_ak = jnp.where(strict, mmt(ag, kg), 0.0)
        a_rb = jnp.where(incl, mmt(rg, bg), 0.0)
        a_rk = jnp.where(incl, mmt(rg, kg), 0.0)
        tmat = eye + a_ab
        p = a_ab
        for _ in range(int(math.log2(C)) - 1):
            p = mm(p, p)
            tmat = tmat + mm(p, tmat)
        u = mm(tmat, mm(ag, h0) + mm(a_ak, vg))
        y_ref[0, 0, :, sl] = mm(rg, h0) + mm(a_rb, u) + mm(a_rk, vg)
        l2 = jnp.concatenate([bg, kg], axis=0)
        x2 = jnp.concatenate([u, vg], axis=0)
        full = _dot(l2, x2, ((0,), (0,)), passes)
        z = jnp.zeros((N, GL), f32)
        for hh in range(GH):
            z = z + jnp.where(lane_head == hh, full[hh * N:(hh + 1) * N], 0.0)
        w_tot = jnp.exp(jnp.sum(lw[:, sl], axis=0, keepdims=True))
        wsel = jnp.where(diag_sel, jnp.broadcast_to(w_tot, (N, GL)), 0.0)
        wcol = _dot(wsel, ones_bd, ((1,), (0,)), 3)
        h_ref[:, sl] = wcol * (h0 + z)

    @pl.when(c == nc - 1)
    def _():
        sf_ref[0, 0] = h_ref[...]


def rwkv_scan_pallas(r, v, a, lw, k, b, s0, passes=3):
    B, L, HN = r.shape
    C, N = RWKV_C, RWKV_N
    nc = L // C

    def cidx(d, c):
        return jnp.where(d == 0, c, nc - 1 - c)

    shared = pl.BlockSpec((1, C, HN), lambda bb, d, c: (bb, cidx(d, c), 0))
    perdir = pl.BlockSpec((1, 1, C, HN), lambda bb, d, c: (d, bb, cidx(d, c), 0))
    st = pl.BlockSpec((1, 1, N, HN), lambda bb, d, c: (bb, d, 0, 0))
    return pl.pallas_call(
        functools.partial(_rwkv_chunk_kernel, passes=passes),
        grid=(B, 2, nc),
        in_specs=[shared, shared, shared, perdir, perdir, perdir, st],
        out_specs=[perdir, st],
        out_shape=[jax.ShapeDtypeStruct((2, B, L, HN), jnp.float32),
                   jax.ShapeDtypeStruct((B, 2, N, HN), jnp.float32)],
        scratch_shapes=[pltpu.VMEM((N, HN), jnp.float32)],
        compiler_params=pltpu.CompilerParams(dimension_semantics=("arbitrary", "arbitrary", "arbitrary")),
        name="rwkv_scan",
    )(r, v, a, lw, k, b, s0)


def rms_norm(x, g):
    xf = x.astype(jnp.float32)
    y = xf * lax.rsqrt(jnp.mean(xf * xf, -1, keepdims=True) + NORM_EPS)
    return (y * g).astype(x.dtype)


def group_rms_norm(x, g, groups):
    xf = x.astype(jnp.float32).reshape(x.shape[:-1] + (groups, -1))
    y = (xf * lax.rsqrt(jnp.mean(xf * xf, -1, keepdims=True) + NORM_EPS)).reshape(x.shape)
    return (y * g).astype(x.dtype)


def head_layer_norm(x, w, bias, eps):
    xf = x.astype(jnp.float32)
    mu = jnp.mean(xf, -1, keepdims=True)
    var = jnp.mean(jnp.square(xf - mu), -1, keepdims=True)
    h, n = x.shape[-2], x.shape[-1]
    y = (xf - mu) * lax.rsqrt(var + eps) * w.reshape(h, n) + bias.reshape(h, n)
    return y.astype(x.dtype)


def depthwise_conv_centred(x, w, bias):
    k = w.shape[0]
    y = lax.conv_general_dilated(x, w[:, None, :].astype(x.dtype), window_strides=(1,),
                                 padding=[(k // 2, k // 2)], dimension_numbers=('NWC', 'WIO', 'NWC'),
                                 feature_group_count=x.shape[-1])
    return y + bias.astype(x.dtype)


def token_shift_centred(x, mu_prev, mu_next):
    prev = jnp.pad(x, ((0, 0), (1, 0), (0, 0)))[:, :-1]
    nxt = jnp.pad(x, ((0, 0), (0, 1), (0, 0)))[:, 1:]
    return x + mu_prev * (prev - x) + mu_next * (nxt - x)


def rope_tables(n_tokens):
    rows = n_tokens // GRID_W
    row = jnp.repeat(jnp.arange(rows), GRID_W).astype(jnp.float32)
    col = jnp.tile(jnp.arange(GRID_W), rows).astype(jnp.float32)
    n_f = RET_DK // 4
    inv = ROPE_BASE ** (-jnp.arange(n_f, dtype=jnp.float32) / n_f)
    ang = jnp.concatenate([row[:, None] * inv, col[:, None] * inv], -1)
    return jnp.cos(ang), jnp.sin(ang)


def apply_rope(x, cos, sin):
    half = x.shape[-1] // 2
    x1, x2 = x[..., :half], x[..., half:]
    c, s = cos[None, :, None, :], sin[None, :, None, :]
    return jnp.concatenate([x1 * c - x2 * s, x1 * s + x2 * c], -1).astype(x.dtype)


def chunked_decay_scan(q, k, v, log_a, s0):
    b, l, g, n = q.shape
    j, p = v.shape[3], v.shape[4]
    nc = l // CHUNK
    f32 = jnp.float32
    qc = q.reshape(b, nc, CHUNK, g, n).astype(f32)
    kc = k.reshape(b, nc, CHUNK, g, n).astype(f32)
    vc = v.reshape(b, nc, CHUNK, g, j, p).astype(f32)
    acs = jnp.cumsum(log_a.reshape(b, nc, CHUNK, g, j).astype(f32), axis=2)
    causal = jnp.tril(jnp.ones((CHUNK, CHUNK), bool))[None, None, :, :, None, None]
    seg = acs[:, :, :, None] - acs[:, :, None, :]
    decay = jnp.exp(jnp.where(causal, seg, -jnp.inf))
    scores = jnp.einsum('bctgn,bcsgn->bctsg', qc, kc)
    y_diag = jnp.einsum('bctsgj,bcsgjp->bctgjp', scores[..., None] * decay, vc)
    to_end = jnp.exp(acs[:, :, -1:] - acs)
    chunk_states = jnp.einsum('bctgn,bctgjp->bcgjnp', kc, vc * to_end[..., None])
    chunk_decay = jnp.exp(acs[:, :, -1])

    def step(s, inp):
        cs, cd = inp
        return s * cd[..., None, None] + cs, s

    s_final, s_in = lax.scan(step, s0.astype(f32),
                             (jnp.moveaxis(chunk_states, 1, 0), jnp.moveaxis(chunk_decay, 1, 0)))
    s_in = jnp.moveaxis(s_in, 0, 1)
    y_off = jnp.einsum('bctgn,bcgjnp->bctgjp', qc, s_in) * jnp.exp(acs)[..., None]
    y = (y_diag + y_off).reshape(b, l, g, j, p)
    return y.astype(v.dtype), s_final.astype(v.dtype)


def directional_scan(q, k, v, log_a, s0, reverse):
    if reverse:
        q, k, v, log_a = (jnp.flip(t, axis=1) for t in (q, k, v, log_a))
    y, s_final = chunked_decay_scan(q, k, v, log_a, s0)
    if reverse:
        y = jnp.flip(y, axis=1)
    return y, s_final


def ssd_rwkv_mixer(h, s_ssd, s_rwkv, e, p):
    b, l, _ = h.shape
    f32 = jnp.float32
    proj = h @ p['ab_w_in'][e]
    z, xbc, dt_raw, rw = jnp.split(proj, [SSD_INNER, SSD_INNER + SSD_XBC, SSD_IN], axis=-1)

    xbc = jax.nn.silu(depthwise_conv_centred(xbc, p['ssd_conv_w'][e], p['ssd_conv_b'][e]))
    xs, bm, cm = jnp.split(xbc, [SSD_INNER, SSD_INNER + SSD_GROUPS * SSD_N], axis=-1)
    xs = xs.reshape(b, l, SSD_GROUPS, SSD_HPG, SSD_P)
    bm = bm.reshape(b, l, SSD_GROUPS, SSD_N)
    cm = cm.reshape(b, l, SSD_GROUPS, SSD_N)
    a_neg = -jnp.exp(p['ssd_a_log'][e].astype(f32))
    y_ssd = xs * p['ssd_d'][e].reshape(SSD_GROUPS, SSD_HPG, 1).astype(xs.dtype)
    ssd_final = []
    for d in range(2):
        dt = jax.nn.softplus(dt_raw.astype(f32) + p['ssd_dt_bias'][e, d]).reshape(b, l, SSD_GROUPS, SSD_HPG)
        s0 = s_ssd[:, d].reshape(b, SSD_GROUPS, SSD_HPG, SSD_N, SSD_P)
        y_d, s_d = directional_scan(cm, bm, xs * dt[..., None].astype(xs.dtype),
                                    dt * a_neg[d].reshape(SSD_GROUPS, SSD_HPG), s0, reverse=(d == 1))
        y_ssd = y_ssd + y_d
        ssd_final.append(s_d.reshape(b, SSD_HEADS, SSD_N, SSD_P))
    y_ssd = group_rms_norm(y_ssd.reshape(b, l, SSD_INNER) * jax.nn.silu(z), p['ssd_norm_g'][e], SSD_GROUPS)

    rw = token_shift_centred(rw, p['rwkv_mu_prev'][e], p['rwkv_mu_next'][e])
    c = RWKV_DIM
    r, k, v, wl, al, gl = jnp.split(rw, [c, 2 * c, 3 * c, 3 * c + 2 * W_LORA, 3 * c + 2 * W_LORA + A_LORA], axis=-1)
    heads = (b, l, RWKV_HEADS, RWKV_N)
    gate = jax.nn.sigmoid(gl) @ p['rwkv_g2'][e]
    a_lora = al @ p['rwkv_a2'][e]
    kk = (k * p['rwkv_k_k'][e]).reshape(heads).astype(f32)
    kk = (kk * lax.rsqrt(jnp.sum(kk * kk, -1, keepdims=True) + 1e-12)).reshape(b, l, c)
    lws, kds, bvs = [], [], []
    for d in range(2):
        w_log = -jax.nn.softplus(-(p['rwkv_w0'][e, d] + jnp.tanh(wl[..., d * W_LORA:(d + 1) * W_LORA]) @ p['rwkv_w2'][e, d])) - 0.5
        lws.append(-jnp.exp(w_log.astype(f32)))
        a = jax.nn.sigmoid((p['rwkv_a0'][e, d] + a_lora).astype(f32))
        kds.append(k.astype(f32) * (1.0 + (a - 1.0) * p['rwkv_k_a'][e]))
        bvs.append(kk * a)
    s0k = jnp.transpose(s_rwkv, (0, 1, 4, 2, 3)).reshape(b, 2, RWKV_N, c)
    ys, sfk = rwkv_scan_pallas(r, v, -kk, jnp.stack(lws), jnp.stack(kds), jnp.stack(bvs), s0k)
    rwkv_final = jnp.transpose(sfk.reshape(b, 2, RWKV_N, RWKV_HEADS, RWKV_N), (0, 1, 3, 4, 2))
    rh = r.reshape(heads)
    vh = v.reshape(heads)
    bonus = 0.0
    for d in range(2):
        bonus = bonus + jnp.sum(rh * kds[d].reshape(heads) * p['rwkv_r_k'][e], -1, keepdims=True) * vh
    o = head_layer_norm((ys[0] + ys[1]).reshape(heads), p['rwkv_ln_w'][e], p['rwkv_ln_b'][e], RWKV_GN_EPS)
    o = (o + bonus).reshape(b, l, RWKV_DIM) * gate

    out = jnp.concatenate([y_ssd, o.astype(y_ssd.dtype)], -1) @ p['ab_w_out'][e]
    return out, jnp.stack(ssd_final, 1), rwkv_final


def retention_mixer(h, s_ret, rope, o, p):
    b, l, _ = h.shape
    proj = h @ p['ret_w_in'][o]
    q, k, v, g = jnp.split(proj, [RET_QK, 2 * RET_QK, 2 * RET_QK + RET_V], axis=-1)
    q = q.reshape(b, l, RET_HEADS, RET_DK)
    k = k.reshape(b, l, RET_HEADS, RET_DK)
    if rope is not None:
        q = apply_rope(q, *rope)
        k = apply_rope(k, *rope)
    k = k * (RET_DK ** -0.5)
    v = v.reshape(b, l, RET_HEADS, 1, RET_DV)
    log_gamma = jax.nn.log_sigmoid(p['ret_decay_logit'][o].astype(jnp.float32))
    outs = [directional_scan(q, k, v, jnp.broadcast_to(log_gamma[d][:, None], (b, l, RET_HEADS, 1)),
                             s_ret[:, d][:, :, None], reverse=(d == 1)) for d in range(2)]
    y = (outs[0][0] + outs[1][0]).reshape(b, l, RET_HEADS, RET_DV)
    y = head_layer_norm(y, p['ret_norm_w'][o], p['ret_norm_b'][o], 1e-5)
    out = (jax.nn.silu(g) * y.reshape(b, l, RET_V)) @ p['ret_w_out'][o]
    return out, jnp.stack([outs[0][1][:, :, 0], outs[1][1][:, :, 0]], axis=1)


def expert_choice_ffn(h, i, p):
    b, t, _ = h.shape
    cap = EC_CAPACITY * t // N_EXPERTS
    aff = jax.nn.softmax((h @ p['router_w'][i]).astype(jnp.float32), axis=-1)
    gate, idx = lax.top_k(jnp.swapaxes(aff, 1, 2), cap)
    bidx = jnp.arange(b)[:, None, None]
    xe = h[bidx, idx]
    hid = jax.nn.silu(jnp.einsum('becd,edf->becf', xe, p['exp_w_gate'][i])) * jnp.einsum('becd,edf->becf', xe, p['exp_w_up'][i])
    ye = jnp.einsum('becf,efd->becd', hid, p['exp_w_down'][i]) * gate[..., None].astype(h.dtype)
    return jnp.zeros_like(h).at[bidx, idx].add(ye.astype(h.dtype))


def trunk(x, cond, s_ssd, s_rwkv, s_ret, rope, p):
    new_ssd, new_rwkv, new_ret = [], [], []
    cond_act = jax.nn.silu(cond.astype(jnp.float32))
    for i in range(DEPTH):
        mod = (cond_act @ p['mod_w'][i] + p['mod_b'][i]).astype(x.dtype)[:, None, :]
        sh1, sc1, g1, sh2, sc2, g2 = jnp.split(mod, 6, axis=-1)
        hn = rms_norm(x, p['norm1_g'][i]) * (1 + sc1) + sh1
        if i % 2 == 0:
            y, st_ssd, st_rwkv = ssd_rwkv_mixer(hn, s_ssd[:, i // 2], s_rwkv[:, i // 2], i // 2, p)
            new_ssd.append(st_ssd)
            new_rwkv.append(st_rwkv)
        else:
            y, st_ret = retention_mixer(hn, s_ret[:, i // 2], rope, i // 2, p)
            new_ret.append(st_ret)
        x = x + g1 * y.astype(x.dtype)
        hn = rms_norm(x, p['norm2_g'][i]) * (1 + sc2) + sh2
        x = x + g2 * expert_choice_ffn(hn, i, p).astype(x.dtype)
    return rms_norm(x, p['final_norm_g']), jnp.stack(new_ssd, 1), jnp.stack(new_rwkv, 1), jnp.stack(new_ret, 1)


def kernel(x_prompt, x_sample, state_ssd, state_rwkv, state_ret, c, c_ctx, mod_w, mod_b, norm1_g, norm2_g,
           router_w, exp_w_gate, exp_w_up, exp_w_down, ab_w_in, ab_w_out, ssd_conv_w, ssd_conv_b, ssd_dt_bias,
           ssd_a_log, ssd_d, ssd_norm_g, rwkv_mu_prev, rwkv_mu_next, rwkv_w0, rwkv_w2, rwkv_a0, rwkv_a2, rwkv_g2,
           rwkv_k_k, rwkv_k_a, rwkv_r_k, rwkv_ln_w, rwkv_ln_b, ret_w_in, ret_w_out, ret_decay_logit, ret_norm_w,
           ret_norm_b, final_norm_g):
    p = dict(mod_w=mod_w, mod_b=mod_b, norm1_g=norm1_g, norm2_g=norm2_g, router_w=router_w,
             exp_w_gate=exp_w_gate, exp_w_up=exp_w_up, exp_w_down=exp_w_down, ab_w_in=ab_w_in, ab_w_out=ab_w_out,
             ssd_conv_w=ssd_conv_w, ssd_conv_b=ssd_conv_b, ssd_dt_bias=ssd_dt_bias, ssd_a_log=ssd_a_log,
             ssd_d=ssd_d, ssd_norm_g=ssd_norm_g, rwkv_mu_prev=rwkv_mu_prev, rwkv_mu_next=rwkv_mu_next,
             rwkv_w0=rwkv_w0, rwkv_w2=rwkv_w2, rwkv_a0=rwkv_a0, rwkv_a2=rwkv_a2, rwkv_g2=rwkv_g2,
             rwkv_k_k=rwkv_k_k, rwkv_k_a=rwkv_k_a, rwkv_r_k=rwkv_r_k, rwkv_ln_w=rwkv_ln_w, rwkv_ln_b=rwkv_ln_b,
             ret_w_in=ret_w_in, ret_w_out=ret_w_out, ret_decay_logit=ret_decay_logit, ret_norm_w=ret_norm_w,
             ret_norm_b=ret_norm_b, final_norm_g=final_norm_g)
    b_ctx = x_prompt.shape[0]
    zero_ssd = jnp.zeros((b_ctx,) + state_ssd.shape[1:], x_prompt.dtype)
    zero_rwkv = jnp.zeros((b_ctx,) + state_rwkv.shape[1:], x_prompt.dtype)
    zero_ret = jnp.zeros((b_ctx,) + state_ret.shape[1:], x_prompt.dtype)
    y_prompt, new_state_ssd, new_state_rwkv, new_state_ret = trunk(
        x_prompt, c_ctx[None, :], zero_ssd, zero_rwkv, zero_ret, None, p)
    rope = rope_tables(x_sample.shape[1])
    y_sample, _, _, _ = trunk(x_sample, c, state_ssd, state_rwkv, state_ret, rope, p)
    return (y_prompt, y_sample, new_state_ssd, new_state_rwkv, new_state_ret)
```

```python
import functools
import math

import jax
import jax.numpy as jnp
from jax import lax
from jax.experimental import pallas as pl
from jax.experimental.pallas import tpu as pltpu

D_MODEL = 1024
DEPTH = 2
GRID_W = 64
CHUNK = 128
NORM_EPS = 1e-6
SSD_HEADS = 16
SSD_P = 64
SSD_INNER = SSD_HEADS * SSD_P
SSD_GROUPS = 2
SSD_HPG = SSD_HEADS // SSD_GROUPS
SSD_N = 128
SSD_XBC = SSD_INNER + 2 * SSD_GROUPS * SSD_N
SSD_IN = SSD_INNER + SSD_XBC + SSD_HEADS
RWKV_HEADS = 16
RWKV_N = 64
RWKV_DIM = RWKV_HEADS * RWKV_N
W_LORA = 64
A_LORA = 64
G_LORA = 128
RWKV_GN_EPS = 64e-5
RET_HEADS = 8
RET_DK = 128
RET_DV = 256
RET_QK = RET_HEADS * RET_DK
RET_V = RET_HEADS * RET_DV
ROPE_BASE = 10000.0
N_EXPERTS = 16
EC_CAPACITY = 2

RWKV_C = 64
RWKV_GH = 4
RWKV_GL = RWKV_GH * RWKV_N


def _split_bf16(x):
    hi = x.astype(jnp.bfloat16)
    lo = (x - hi.astype(jnp.float32)).astype(jnp.bfloat16)
    return hi, lo


def _dot(a, b, dims, passes):
    f = functools.partial(lax.dot_general, dimension_numbers=(dims, ((), ())),
                          preferred_element_type=jnp.float32)
    if passes == 1:
        return f(a.astype(jnp.bfloat16), b.astype(jnp.bfloat16))
    ah, al = _split_bf16(a)
    bh, bl = _split_bf16(b)
    return f(ah, bh) + (f(ah, bl) + f(al, bh))


def _rwkv_chunk_kernel(tbl_ref, r_ref, v_ref, a_ref, lw_ref, k_ref, b_ref, s0_ref, y_ref, sf_ref, h_ref, *,
                       p_inv, p_oth):
    C, N, GH, GL = RWKV_C, RWKV_N, RWKV_GH, RWKV_GL
    i = pl.program_id(0)
    d = tbl_ref[1, i]
    f32 = jnp.float32

    @pl.when(tbl_ref[3, i] == 1)
    def _():
        h_ref[...] = s0_ref[0, 0]

    sgn = 1 - 2 * d
    t_i = lax.broadcasted_iota(jnp.int32, (C, GL), 0)
    s_i = lax.broadcasted_iota(jnp.int32, (C, GL), 1) & (N - 1)
    rel = (s_i - t_i) * sgn
    strict = rel < 0
    incl = rel <= 0
    eye = (s_i == t_i).astype(f32)
    bh_r = lax.broadcasted_iota(jnp.int32, (GL, GL), 0) // N
    bh_c = lax.broadcasted_iota(jnp.int32, (GL, GL), 1) // N
    blk = bh_r == bh_c
    tt = lax.broadcasted_iota(jnp.int32, (C, C), 0)
    ss = lax.broadcasted_iota(jnp.int32, (C, C), 1)
    tri = ((ss - tt) * sgn <= 0).astype(jnp.bfloat16)

    def bd(x, passes):
        pieces = []
        for _ in range(1 if passes == 1 else 2):
            hi = x.astype(jnp.bfloat16)
            x = x - hi.astype(f32)
            pieces.append(jnp.where(blk, jnp.concatenate([hi] * GH, axis=0), jnp.zeros((), jnp.bfloat16)))
        return pieces

    def mm(l, x, passes, dims=((1,), (0,))):
        f = functools.partial(lax.dot_general, dimension_numbers=(dims, ((), ())), preferred_element_type=f32)
        xs = bd(x, passes)
        lh = l.astype(jnp.bfloat16)
        if passes == 1:
            return f(lh, xs[0])
        ll = (l - lh.astype(f32)).astype(jnp.bfloat16)
        m = l.shape[0]
        both = f(jnp.concatenate([lh, ll], axis=0), xs[0])
        return both[:m] + (f(lh, xs[1]) + both[m:])

    nt = ((1,), (1,))
    lw = lw_ref[0]
    lw_hi, lw_lo = _split_bf16(lw)
    cum = (jnp.dot(tri, lw_hi, preferred_element_type=f32)
           + jnp.dot(tri, lw_lo, preferred_element_type=f32))
    w_in = jnp.exp(cum)
    w_ex = jnp.exp(cum - lw)
    w_inv = jnp.exp(-cum)

    r_t = r_ref[...] * w_in
    a_t = a_ref[...] * w_ex
    b_t = b_ref[0] * w_inv
    k_t = k_ref[0] * w_inv
    v = v_ref[...]
    n_i = lax.broadcasted_iota(jnp.int32, (N, GL), 0)
    lane_i = lax.broadcasted_iota(jnp.int32, (N, GL), 1)
    diag_sel = (lane_i & (N - 1)) == n_i
    lane_head = lane_i // N
    ones_bd = blk.astype(f32)
    strict2 = jnp.concatenate([strict, incl], axis=0)

    groups = range(RWKV_HEADS // GH)
    sls = [slice(g * GL, (g + 1) * GL) for g in groups]
    each = lambda fn: [fn(g) for g in groups]
    bg = each(lambda g: b_t[:, sls[g]])
    kg = each(lambda g: k_t[:, sls[g]])
    vg = each(lambda g: v[:, sls[g]])
    h0 = each(lambda g: h_ref[:, sls[g]])
    ar = each(lambda g: jnp.concatenate([a_t[:, sls[g]], r_t[:, sls[g]]], axis=0))
    m_b = each(lambda g: jnp.where(strict2, mm(ar[g], bg[g], p_oth, nt), 0.0))
    m_k = each(lambda g: jnp.where(strict2, mm(ar[g], kg[g], p_oth, nt), 0.0))
    p = each(lambda g: mm(m_b[g][:C], m_b[g][:C], p_inv))
    tmat = each(lambda g: eye + m_b[g][:C])
    for _ in range(int(math.log2(C)) - 2):
        pt = each(lambda g: mm(jnp.concatenate([p[g], tmat[g]], axis=0), p[g], p_inv))
        p = each(lambda g: pt[g][:C])
        tmat = each(lambda g: tmat[g] + pt[g][C:])
    tmat = each(lambda g: tmat[g] + mm(tmat[g], p[g], p_inv))
    ar_h = each(lambda g: mm(ar[g], h0[g], p_oth))
    mk_v = each(lambda g: mm(m_k[g], vg[g], p_oth))
    u = each(lambda g: mm(tmat[g], ar_h[g][:C] + mk_v[g][:C], p_oth))
    y = each(lambda g: ar_h[g][C:] + mm(m_b[g][C:], u[g], p_oth) + mk_v[g][C:])
    full = each(lambda g: _dot(jnp.concatenate([bg[g], kg[g]], axis=0), jnp.concatenate([u[g], vg[g]], axis=0),
                               ((0,), (0,)), p_oth))
    for g in groups:
        y_ref[0, :, sls[g]] = y[g]
        z = jnp.zeros((N, GL), f32)
        for hh in range(GH):
            z = z + jnp.where(lane_head == hh, full[g][hh * N:(hh + 1) * N], 0.0)
        w_tot = jnp.exp(jnp.sum(lw[:, sls[g]], axis=0, keepdims=True))
        wsel = jnp.where(diag_sel, jnp.broadcast_to(w_tot, (N, GL)), 0.0)
        wcol = _dot(wsel, ones_bd, ((1,), (0,)), 3)
        h_ref[:, sls[g]] = wcol * (h0[g] + z)

    @pl.when(tbl_ref[4, i] == 1)
    def _():
        sf_ref[0, 0] = h_ref[...]


def _rwkv_steps(n_ctx, n_lat, l_lat):
    C = RWKV_C
    rows = []
    seqs = [(s, s * BLK, BLK) for s in range(n_ctx)] + [(n_ctx + s, n_ctx * BLK + s * l_lat, l_lat) for s in range(n_lat)]
    for sid, row0, length in seqs:
        nc = length // C
        for d in range(2):
            for j in range(nc):
                cj = j if d == 0 else nc - 1 - j
                rows.append((row0 // C + cj, d, sid, int(j == 0), int(j == nc - 1)))
    return [list(col) for col in zip(*rows)]


def rwkv_scan_pallas(r, v, a, lw, k, b, s0, steps, p_inv=3, p_oth=3):
    R, HN = r.shape
    C, N = RWKV_C, RWKV_N
    tbl = jnp.asarray(steps, jnp.int32)
    shared = pl.BlockSpec((C, HN), lambda i, t: (t[0, i], 0))
    perdir = pl.BlockSpec((1, C, HN), lambda i, t: (t[1, i], t[0, i], 0))
    st = pl.BlockSpec((1, 1, N, HN), lambda i, t: (t[2, i], t[1, i], 0, 0))
    return pl.pallas_call(
        functools.partial(_rwkv_chunk_kernel, p_inv=p_inv, p_oth=p_oth),
        grid_spec=pltpu.PrefetchScalarGridSpec(
            num_scalar_prefetch=1,
            grid=(len(steps[0]),),
            in_specs=[shared, shared, shared, perdir, perdir, perdir, st],
            out_specs=[perdir, st],
            scratch_shapes=[pltpu.VMEM((N, HN), jnp.float32)]),
        out_shape=[jax.ShapeDtypeStruct((2, R, HN), jnp.float32),
                   jax.ShapeDtypeStruct(s0.shape, jnp.float32)],
        compiler_params=pltpu.CompilerParams(dimension_semantics=("arbitrary",)),
        name="rwkv_scan",
    )(tbl, r, v, a, lw, k, b, s0)


BLK = 256
FF_TILE = 512
SELECT_TILE = 512
GATHER_ROWS = 512
SELECT_MIN_EXP = -1100.0
SELECT_BINADE_STEPS = 11
SELECT_MANTISSA_STEPS = 40
MOE_FFN_VMEM_BYTES = 48 * 1024 * 1024


def _moe_select_kernel(aff_ref, slot_ref, *, cap):
    a = aff_ref[...]
    E, T = a.shape
    f32 = jnp.float32

    def enough(piv):
        return jnp.sum(jnp.where(a >= piv, 1.0, 0.0), axis=1, keepdims=True) >= cap

    def binade(_, lohi):
        e_lo, e_hi = lohi
        mid = jnp.floor((e_lo + e_hi) * 0.5)
        ok = enough(jnp.exp2(mid))
        return jnp.where(ok, mid, e_lo), jnp.where(ok, e_hi, mid)

    e_lo, e_hi = lax.fori_loop(0, SELECT_BINADE_STEPS, binade,
                               (jnp.full((E, 1), SELECT_MIN_EXP, f32), jnp.full((E, 1), 1.0, f32)))

    def inside(_, lohi):
        lo, hi = lohi
        mid = lo + (hi - lo) * 0.5
        ok = enough(mid)
        return jnp.where(ok, mid, lo), jnp.where(ok, hi, mid)

    thr, _ = lax.fori_loop(0, SELECT_MANTISSA_STEPS, inside, (jnp.exp2(e_lo), jnp.exp2(e_hi)))
    gt = a > thr
    eq = a == thr
    need = cap - jnp.sum(jnp.where(gt, 1.0, 0.0), axis=1, keepdims=True)
    tw = min(T, SELECT_TILE)

    def prefix_count(mask):
        m = jnp.where(mask, 1.0, 0.0).astype(jnp.bfloat16)
        outs = []
        for j in range(T // tw):
            s_i = lax.broadcasted_iota(jnp.int32, (T, tw), 0)
            t_i = lax.broadcasted_iota(jnp.int32, (T, tw), 1) + j * tw
            before = jnp.where(s_i < t_i, 1.0, 0.0).astype(jnp.bfloat16)
            outs.append(jnp.dot(m, before, preferred_element_type=f32))
        return outs[0] if len(outs) == 1 else jnp.concatenate(outs, axis=1)

    sel = gt | (eq & (prefix_count(eq) < need))
    slot_ref[...] = jnp.where(sel, prefix_count(sel).astype(jnp.int32), -1)


def _moe_select(affT, row0, n_seq, t):
    E = affT.shape[0]
    b0 = row0 // t
    return pl.pallas_call(
        functools.partial(_moe_select_kernel, cap=EC_CAPACITY * t // N_EXPERTS),
        grid=(n_seq,),
        in_specs=[pl.BlockSpec((E, t), lambda s: (0, b0 + s))],
        out_specs=pl.BlockSpec((E, t), lambda s: (0, s)),
        out_shape=jax.ShapeDtypeStruct((E, n_seq * t), jnp.int32),
        name="moe_select",
    )(affT)


def _moe_gather_kernel(slot_ref, aff_ref, hn_ref, xe_ref, gate_ref, *, cap):
    slot = slot_ref[...]
    eg, _, T = slot.shape
    p_i = lax.broadcasted_iota(jnp.int32, (eg, cap, T), 1)
    hit = slot == p_i
    onehot = jnp.where(hit, 1.0, 0.0).reshape(eg * cap, T).astype(jnp.bfloat16)
    xe = jnp.dot(onehot, hn_ref[...], preferred_element_type=jnp.float32)
    xe_ref[...] = xe.reshape(eg, cap, xe.shape[1]).astype(xe_ref.dtype)
    g = jnp.sum(jnp.where(hit, aff_ref[...], 0.0), axis=2, keepdims=True)
    gate_ref[...] = jnp.broadcast_to(g, (eg, cap, 128))


def _moe_gather(slot3, aff3, hn, row0, n_seq, t):
    E = slot3.shape[0]
    D = hn.shape[1]
    cap = EC_CAPACITY * t // N_EXPERTS
    eg = min(E, GATHER_ROWS // cap)
    b0 = row0 // t
    return pl.pallas_call(
        functools.partial(_moe_gather_kernel, cap=cap),
        grid=(n_seq, E // eg),
        in_specs=[pl.BlockSpec((eg, 1, t), lambda s, e: (e, 0, b0 + s)),
                  pl.BlockSpec((eg, 1, t), lambda s, e: (e, 0, b0 + s)),
                  pl.BlockSpec((t, D), lambda s, e: (b0 + s, 0))],
        out_specs=[pl.BlockSpec((eg, cap, D), lambda s, e: (e, s, 0)),
                   pl.BlockSpec((eg, cap, 128), lambda s, e: (e, s, 0))],
        out_shape=[jax.ShapeDtypeStruct((E, n_seq * cap, D), jnp.bfloat16),
                   jax.ShapeDtypeStruct((E, n_seq * cap, 128), jnp.float32)],
        compiler_params=pltpu.CompilerParams(dimension_semantics=("arbitrary", "arbitrary"),
                                             vmem_limit_bytes=MOE_FFN_VMEM_BYTES),
        name="moe_gather",
    )(slot3, aff3, hn)


def _moe_ffn_kernel(xc_ref, xl_ref, gc_ref, gl_ref, wg_ref, wu_ref, wd_ref, yc_ref, yl_ref, accc_ref, accl_ref):
    f = pl.program_id(1)
    nf = pl.num_programs(1)
    bf16 = jnp.bfloat16
    wg = wg_ref[0, 0].astype(bf16)
    wu = wu_ref[0, 0].astype(bf16)
    wd = wd_ref[0, 0].astype(bf16)

    def part(x_ref, acc_ref):
        x = x_ref[0]
        g = jnp.dot(x, wg, preferred_element_type=jnp.float32)
        u = jnp.dot(x, wu, preferred_element_type=jnp.float32)
        h = (g * jax.nn.sigmoid(g) * u).astype(bf16)
        y = jnp.dot(h, wd, preferred_element_type=jnp.float32)

        @pl.when(f == 0)
        def _():
            acc_ref[...] = y

        @pl.when(f != 0)
        def _():
            acc_ref[...] += y

    part(xc_ref, accc_ref)
    part(xl_ref, accl_ref)

    @pl.when(f == nf - 1)
    def _():
        for acc_ref, g_ref, y_ref in ((accc_ref, gc_ref, yc_ref), (accl_ref, gl_ref, yl_ref)):
            gate = jnp.concatenate([g_ref[0]] * (acc_ref.shape[1] // 128), axis=1)
            y_ref[0] = (acc_ref[...] * gate).astype(y_ref.dtype)


def _moe_ffn(xc, xl, gc, gl, wg, wu, wd, layer):
    E, nc_rows, D = xc.shape
    nl_rows = xl.shape[1]
    F = wg.shape[3]
    nf = F // FF_TILE
    return pl.pallas_call(
        _moe_ffn_kernel,
        grid=(E, nf),
        in_specs=[pl.BlockSpec((1, nc_rows, D), lambda e, f: (e, 0, 0)),
                  pl.BlockSpec((1, nl_rows, D), lambda e, f: (e, 0, 0)),
                  pl.BlockSpec((1, nc_rows, 128), lambda e, f: (e, 0, 0)),
                  pl.BlockSpec((1, nl_rows, 128), lambda e, f: (e, 0, 0)),
                  pl.BlockSpec((1, 1, D, FF_TILE), lambda e, f: (layer, e, 0, f)),
                  pl.BlockSpec((1, 1, D, FF_TILE), lambda e, f: (layer, e, 0, f)),
                  pl.BlockSpec((1, 1, FF_TILE, D), lambda e, f: (layer, e, f, 0))],
        out_specs=[pl.BlockSpec((1, nc_rows, D), lambda e, f: (e, 0, 0)),
                   pl.BlockSpec((1, nl_rows, D), lambda e, f: (e, 0, 0))],
        out_shape=[jax.ShapeDtypeStruct((E, nc_rows, D), jnp.bfloat16),
                   jax.ShapeDtypeStruct((E, nl_rows, D), jnp.bfloat16)],
        scratch_shapes=[pltpu.VMEM((nc_rows, D), jnp.float32), pltpu.VMEM((nl_rows, D), jnp.float32)],
        compiler_params=pltpu.CompilerParams(dimension_semantics=("arbitrary", "arbitrary"),
                                             vmem_limit_bytes=MOE_FFN_VMEM_BYTES),
        name="moe_ffn",
    )(xc, xl, gc, gl, wg, wu, wd)


def _moe_scatter_kernel(slot_ref, ye_ref, x_ref, g2_ref, *rest, cap, final):
    if final:
        fg_ref, o_ref = rest
    else:
        (o_ref,) = rest
    E = ye_ref.shape[0]
    D = ye_ref.shape[2]
    tb = x_ref.shape[0]
    slot = slot_ref[...]
    p_i = lax.broadcasted_iota(jnp.int32, (E, cap, tb), 1)
    onehot = jnp.where(slot == p_i, 1.0, 0.0).reshape(E * cap, tb).astype(jnp.bfloat16)
    ye = ye_ref[...].reshape(E * cap, D)
    moe = lax.dot_general(onehot, ye, (((0,), (0,)), ((), ())), preferred_element_type=jnp.float32)
    x = x_ref[...] + g2_ref[0] * moe
    if final:
        x = x * lax.rsqrt(jnp.mean(x * x, -1, keepdims=True) + NORM_EPS) * fg_ref[...]
    o_ref[...] = x


def _moe_scatter(slot3, ye, x, g2blk, row0, n_seq, t, final_g=None):
    E, _, D = ye.shape
    cap = EC_CAPACITY * t // N_EXPERTS
    nb = t // BLK
    b0 = row0 // BLK
    final = final_g is not None
    in_specs = [pl.BlockSpec((E, 1, BLK), lambda s, j: (0, 0, b0 + s * nb + j)),
                pl.BlockSpec((E, cap, D), lambda s, j: (0, s, 0)),
                pl.BlockSpec((BLK, D), lambda s, j: (b0 + s * nb + j, 0)),
                pl.BlockSpec((1, 1, D), lambda s, j: (b0 + s * nb + j, 0, 0))]
    args = [slot3, ye, x, g2blk]
    if final:
        in_specs.append(pl.BlockSpec((1, D), lambda s, j: (0, 0)))
        args.append(final_g.reshape(1, D))
        out_specs = pl.BlockSpec((BLK, D), lambda s, j: (s * nb + j, 0))
        out_shape = jax.ShapeDtypeStruct((n_seq * t, D), jnp.float32)
        aliases = {}
    else:
        out_specs = pl.BlockSpec((BLK, D), lambda s, j: (b0 + s * nb + j, 0))
        out_shape = jax.ShapeDtypeStruct(x.shape, jnp.float32)
        aliases = {2: 0}
    return pl.pallas_call(
        functools.partial(_moe_scatter_kernel, cap=cap, final=final),
        grid=(n_seq, nb),
        in_specs=in_specs, out_specs=out_specs, out_shape=out_shape,
        input_output_aliases=aliases,
        compiler_params=pltpu.CompilerParams(dimension_semantics=("arbitrary", "arbitrary"),
                                             vmem_limit_bytes=MOE_FFN_VMEM_BYTES),
        name="moe_scatter",
    )(*args)


def moe_layer(x, hn, affT, g2blk, wg, wu, wd, layer, n_ctx, l_lat, final_g=None):
    R = x.shape[0]
    r_ctx = n_ctx * BLK
    n_lat = (R - r_ctx) // l_lat
    slot = jnp.concatenate([_moe_select(affT, 0, n_ctx, BLK), _moe_select(affT, r_ctx, n_lat, l_lat)], axis=1)
    slot3 = slot[:, None, :]
    aff3 = affT[:, None, :]
    xc, gc = _moe_gather(slot3, aff3, hn, 0, n_ctx, BLK)
    xl, gl = _moe_gather(slot3, aff3, hn, r_ctx, n_lat, l_lat)
    yc, yl = _moe_ffn(xc, xl, gc, gl, wg, wu, wd, layer)
    if final_g is None:
        x = _moe_scatter(slot3, yc, x, g2blk, 0, n_ctx, BLK)
        return _moe_scatter(slot3, yl, x, g2blk, r_ctx, n_lat, l_lat)
    return (_moe_scatter(slot3, yc, x, g2blk, 0, n_ctx, BLK, final_g),
            _moe_scatter(slot3, yl, x, g2blk, r_ctx, n_lat, l_lat, final_g))


L0_Z = (0, 1024)
L0_XBC = (1024, 2560)
L0_SHIFT = (2560, 6016)
L0_DT = (6016, 6144)
L0_COLS = 6144
L0_VMEM_BYTES = 56 * 1024 * 1024


def _sum_split(x, m, n_split):
    acc = None
    for _ in range(n_split):
        hi = x.astype(jnp.bfloat16)
        x = x - hi.astype(jnp.float32)
        t = jnp.dot(hi, m, preferred_element_type=jnp.float32)
        acc = t if acc is None else acc + t
    return acc


def _sum_split_left(m, x, n_split):
    acc = None
    for _ in range(n_split):
        hi = x.astype(jnp.bfloat16)
        x = x - hi.astype(jnp.float32)
        t = jnp.dot(m, hi, preferred_element_type=jnp.float32)
        acc = t if acc is None else acc + t
    return acc


def _head_sum(x, e_ref, et_ref):
    return _sum_split(_sum_split(x, e_ref[...], 2), et_ref[...], 2)


def _adaln(x, g, sc, sh):
    y = x * lax.rsqrt(jnp.mean(x * x, -1, keepdims=True) + NORM_EPS) * g
    return y * (1.0 + sc) + sh


def _softplus(x):
    return jnp.maximum(x, 0.0) + jnp.log(1.0 + jnp.exp(-jnp.abs(x)))


def _l0_in_kernel(tbl_ref, x_ref, xp_ref, xn_ref, g_ref, sc_ref, sh_ref, w_ref, mup_ref, mun_ref, rwp_ref,
                  w2_ref, a2_ref, g2_ref, e_ref, et_ref,
                  z_ref, xbc_ref, dt_ref, r_ref, v_ref, an_ref, lw_ref, kd_ref, bv_ref, gate_ref, bonus_ref):
    i = pl.program_id(0)
    f32, bf16 = jnp.float32, jnp.bfloat16
    g, sc, sh = g_ref[...], sc_ref[0], sh_ref[0]
    hn = _adaln(x_ref[...], g, sc, sh).astype(bf16)
    halo = _adaln(jnp.concatenate([xp_ref[...], xn_ref[...]], axis=0), g, sc, sh).astype(bf16)
    z_ref[...] = jnp.dot(hn, w_ref[:, L0_Z[0]:L0_Z[1]], preferred_element_type=f32)
    xbc_ref[...] = jnp.dot(hn, w_ref[:, L0_XBC[0]:L0_XBC[1]], preferred_element_type=f32)
    dt_ref[...] = jnp.dot(hn, w_ref[:, L0_DT[0]:L0_DT[1]], preferred_element_type=f32)

    keep_prev = (1 - tbl_ref[0, i]).astype(f32)
    keep_next = (1 - tbl_ref[1, i]).astype(f32)
    row = lax.broadcasted_iota(jnp.int32, (BLK, 1), 0)

    def shifted(c0, c1):
        a, b = L0_SHIFT[0] + c0, L0_SHIFT[0] + c1
        cur = jnp.dot(hn, w_ref[:, a:b], preferred_element_type=f32)
        edge = jnp.dot(halo, w_ref[:, a:b], preferred_element_type=f32)
        prev = jnp.where(row == 0, edge[7:8] * keep_prev, pltpu.roll(cur, 1, 0))
        nxt = jnp.where(row == BLK - 1, edge[8:9] * keep_next, pltpu.roll(cur, BLK - 1, 0))
        return cur + mup_ref[:, c0:c1] * (prev - cur) + mun_ref[:, c0:c1] * (nxt - cur)

    c = RWKV_DIM
    r = shifted(0, c)
    k = shifted(c, 2 * c)
    v = shifted(2 * c, 3 * c)
    wl = shifted(3 * c, 3 * c + 2 * W_LORA)
    ag = shifted(3 * c + 2 * W_LORA, 3 * c + 2 * W_LORA + 256)
    k_k, k_a, r_k = rwp_ref[0:1], rwp_ref[1:2], rwp_ref[2:3]
    gate_ref[...] = jnp.dot(jax.nn.sigmoid(ag).astype(bf16), g2_ref[...], preferred_element_type=f32)
    a_lora = jnp.dot(ag.astype(bf16), a2_ref[...], preferred_element_type=f32)
    w_lin = jnp.dot(jnp.tanh(wl).astype(bf16), w2_ref[...], preferred_element_type=f32)
    kk = k * k_k
    kk = kk * lax.rsqrt(_head_sum(kk * kk, e_ref, et_ref) + 1e-12)
    r_ref[...] = r
    v_ref[...] = v
    an_ref[...] = -kk
    kd_sum = None
    for d in range(2):
        w_log = -_softplus(-(rwp_ref[3 + d:4 + d] + w_lin[:, d * c:(d + 1) * c])) - 0.5
        lw_ref[d] = -jnp.exp(w_log)
        a = jax.nn.sigmoid(rwp_ref[5 + d:6 + d] + a_lora)
        kd = k * (1.0 + (a - 1.0) * k_a)
        kd_ref[d] = kd
        bv_ref[d] = kk * a
        kd_sum = kd if kd_sum is None else kd_sum + kd
    bonus_ref[...] = _head_sum(r * kd_sum * r_k, e_ref, et_ref) * v


def _seq_tables(n_ctx, n_lat, l_lat):
    nb = l_lat // BLK
    cond = [0] * n_ctx + [1 + s for s in range(n_lat) for _ in range(nb)]
    first = [1] * n_ctx + [1 if j == 0 else 0 for _ in range(n_lat) for j in range(nb)]
    last = [1] * n_ctx + [1 if j == nb - 1 else 0 for _ in range(n_lat) for j in range(nb)]
    return cond, first, last


def l0_in(x, g1, scb, shb, w_packed, mup, mun, rwp, w2bd, a2p, g2p, e_ind, et_ind, first, last):
    R, D = x.shape
    nblk = R // BLK
    n8 = R // 8
    tbl = jnp.asarray([first, last], jnp.int32)
    c = RWKV_DIM
    row = lambda i, t: (i, 0)
    full = lambda shape: pl.BlockSpec(shape, lambda i, t: (0,) * len(shape))
    rows = lambda n: pl.BlockSpec((BLK, n), row)
    rows2 = lambda n: pl.BlockSpec((2, BLK, n), lambda i, t: (0, i, 0))
    f32 = jnp.float32
    sds = jax.ShapeDtypeStruct
    return pl.pallas_call(
        _l0_in_kernel,
        grid_spec=pltpu.PrefetchScalarGridSpec(
            num_scalar_prefetch=1,
            grid=(nblk,),
            in_specs=[rows(D),
                      pl.BlockSpec((8, D), lambda i, t: (jnp.maximum(i * (BLK // 8) - 1, 0), 0)),
                      pl.BlockSpec((8, D), lambda i, t: (jnp.minimum((i + 1) * (BLK // 8), n8 - 1), 0)),
                      full((1, D)),
                      pl.BlockSpec((1, 1, D), lambda i, t: (i, 0, 0)),
                      pl.BlockSpec((1, 1, D), lambda i, t: (i, 0, 0)),
                      full((D, L0_COLS)), full(mup.shape), full(mun.shape), full(rwp.shape),
                      full(w2bd.shape), full(a2p.shape), full(g2p.shape), full(e_ind.shape), full(et_ind.shape)],
            out_specs=[rows(c), rows(SSD_XBC), rows(128), rows(c), rows(c), rows(c),
                       rows2(c), rows2(c), rows2(c), rows(c), rows(c)]),
        out_shape=[sds((R, c), f32), sds((R, SSD_XBC), f32), sds((R, 128), f32), sds((R, c), f32), sds((R, c), f32),
                   sds((R, c), f32), sds((2, R, c), f32), sds((2, R, c), f32), sds((2, R, c), f32),
                   sds((R, c), f32), sds((R, c), f32)],
        compiler_params=pltpu.CompilerParams(dimension_semantics=("arbitrary",), vmem_limit_bytes=L0_VMEM_BYTES),
        name="l0_in",
    )(tbl, x, x, x, g1, scb, shb, w_packed, mup, mun, rwp, w2bd, a2p, g2p, e_ind, et_ind)


def l0_pack_weights(p, e):
    bf16 = jnp.bfloat16
    w = p['ab_w_in'][e]
    D = w.shape[0]
    c = RWKV_DIM
    rw0 = SSD_IN
    ag0 = rw0 + 3 * c + 2 * W_LORA
    w_packed = jnp.concatenate([
        w[:, :SSD_INNER + SSD_XBC], w[:, rw0:ag0], w[:, ag0:ag0 + A_LORA + G_LORA],
        jnp.zeros((D, 256 - A_LORA - G_LORA), w.dtype),
        w[:, SSD_INNER + SSD_XBC:SSD_IN], jnp.zeros((D, 128 - SSD_HEADS), w.dtype)], axis=1).astype(bf16)

    def pack_mu(mu):
        return jnp.concatenate([mu, jnp.zeros((256 - A_LORA - G_LORA,), mu.dtype)])[None, :]

    rwp = jnp.stack([p['rwkv_k_k'][e], p['rwkv_k_a'][e], p['rwkv_r_k'][e].reshape(-1), p['rwkv_w0'][e, 0],
                     p['rwkv_w0'][e, 1], p['rwkv_a0'][e, 0], p['rwkv_a0'][e, 1], jnp.zeros((c,), jnp.float32)])
    zw = jnp.zeros((W_LORA, c), jnp.float32)
    w2bd = jnp.concatenate([jnp.concatenate([p['rwkv_w2'][e, 0], zw], axis=1),
                            jnp.concatenate([zw, p['rwkv_w2'][e, 1]], axis=1)], axis=0).astype(bf16)
    a2p = jnp.concatenate([p['rwkv_a2'][e], jnp.zeros((256 - A_LORA, c), jnp.float32)], axis=0).astype(bf16)
    g2p = jnp.concatenate([jnp.zeros((A_LORA, c), jnp.float32), p['rwkv_g2'][e],
                           jnp.zeros((256 - A_LORA - G_LORA, c), jnp.float32)], axis=0).astype(bf16)
    head = jnp.arange(c) // RWKV_N
    e_ind = (head[:, None] == jnp.arange(128)[None, :]).astype(bf16)
    return w_packed, pack_mu(p['rwkv_mu_prev'][e]), pack_mu(p['rwkv_mu_next'][e]), rwp, w2bd, a2p, g2p, e_ind, e_ind.T


SSD_QH = 4
SSD_VMEM_BYTES = 48 * 1024 * 1024
NEG_BIG = -1e30


def _conv_silu(cur, prev_row, next_row, w_ref, b_ref):
    row = lax.broadcasted_iota(jnp.int32, (BLK, 1), 0)
    prev = jnp.where(row == 0, prev_row, pltpu.roll(cur, 1, 0))
    nxt = jnp.where(row == BLK - 1, next_row, pltpu.roll(cur, BLK - 1, 0))
    y = w_ref[0:1] * prev + w_ref[1:2] * cur + w_ref[2:3] * nxt + b_ref[...]
    return y * jax.nn.sigmoid(y)


def _ssd_kernel(tbl_ref, xs_ref, b_ref, c_ref, dt_ref, cwx_ref, cwb_ref, cwc_ref, cbx_ref, cbb_ref, cbc_ref,
                sel_ref, hp_ref, s0_ref, y_ref, fs_ref, xa_ref, ba_ref, ca_ref, sfx_ref, dtb_ref, st_ref):
    f32, bf16 = jnp.float32, jnp.bfloat16
    sb = pl.program_id(0)
    sbr = xs_ref.shape[0]
    nch = sbr // BLK
    P, QH = SSD_P, SSD_QH
    t_i = lax.broadcasted_iota(jnp.int32, (BLK, BLK), 0)
    s_i = lax.broadcasted_iota(jnp.int32, (BLK, BLK), 1)
    lower = s_i <= t_i
    upper = s_i >= t_i
    tri_lo = jnp.where(lower, 1.0, 0.0).astype(bf16)
    tri_up = jnp.where(upper, 1.0, 0.0).astype(bf16)
    hp = hp_ref[0]
    sel = sel_ref[0]

    def chunk_rows(c):
        return pl.ds(pl.multiple_of(c * BLK, BLK), BLK)

    def neighbours(ref, c, keep_prev, keep_next):
        lo = jnp.maximum(c * BLK - 1, 0)
        hi = jnp.minimum((c + 1) * BLK, sbr - 1)
        return ref[pl.ds(lo, 1), :] * keep_prev, ref[pl.ds(hi, 1), :] * keep_next

    def fwd(c, carry):
        blk = sb * nch + c
        first, last = tbl_ref[0, blk], tbl_ref[1, blk]
        kp, kn = (1 - first).astype(f32), (1 - last).astype(f32)
        rows = chunk_rows(c)

        @pl.when(first == 1)
        def _():
            st_ref[0] = s0_ref[0, 0]

        xa = _conv_silu(xs_ref[rows, :], *neighbours(xs_ref, c, kp, kn), cwx_ref, cbx_ref)
        bm = _conv_silu(b_ref[rows, :], *neighbours(b_ref, c, kp, kn), cwb_ref, cbb_ref)
        cm = _conv_silu(c_ref[rows, :], *neighbours(c_ref, c, kp, kn), cwc_ref, cbc_ref)
        xa_ref[rows, :] = xa
        ba_ref[rows, :] = bm
        ca_ref[rows, :] = cm
        dtq = _sum_split(dt_ref[rows, :], sel, 3)
        dtf = _softplus(dtq + hp[0:1])
        dtb = _softplus(dtq + hp[1:2])
        acs = _sum_split_left(tri_lo, dtf * hp[2:3], 3)
        sfx = _sum_split_left(tri_up, dtb * hp[3:4], 3)
        sfx_ref[rows, :] = sfx
        dtb_ref[rows, :] = dtb
        acs_t, sfx_t, dtf_t, dtb_t = acs.T, sfx.T, dtf.T, dtb.T
        g = lax.dot_general(cm.astype(bf16), bm.astype(bf16), (((1,), (1,)), ((), ())), preferred_element_type=f32)
        cmb = cm.astype(bf16)
        for j in range(QH):
            hs = slice(j * P, (j + 1) * P)
            xj = xa[:, hs]
            seg_f = jnp.where(lower, acs[:, j:j + 1] - acs_t[j:j + 1, :], NEG_BIG)
            seg_b = jnp.where(upper, sfx[:, j:j + 1] - sfx_t[j:j + 1, :], NEG_BIG)
            m = g * (jnp.exp(seg_f) * dtf_t[j:j + 1, :] + jnp.exp(seg_b) * dtb_t[j:j + 1, :])
            s_in = st_ref[0, j]
            y = (xj * hp[4:5, j:j + 1] + jnp.dot(m.astype(bf16), xj.astype(bf16), preferred_element_type=f32)
                 + jnp.exp(acs[:, j:j + 1]) * jnp.dot(cmb, s_in.astype(bf16), preferred_element_type=f32))
            y_ref[rows, hs] = y
            tot = acs[BLK - 1:BLK, j:j + 1]
            xw = xj * (jnp.exp(tot - acs[:, j:j + 1]) * dtf[:, j:j + 1])
            st_ref[0, j] = jnp.exp(tot) * s_in + lax.dot_general(
                bm.astype(bf16), xw.astype(bf16), (((0,), (0,)), ((), ())), preferred_element_type=f32)
        fs_ref[c, 0] = st_ref[0]
        return carry

    lax.fori_loop(0, nch, fwd, 0)

    def bwd(k, carry):
        c = nch - 1 - k
        blk = sb * nch + c
        rows = chunk_rows(c)

        @pl.when(tbl_ref[1, blk] == 1)
        def _():
            st_ref[1] = s0_ref[0, 1]

        sfx, dtb = sfx_ref[rows, :], dtb_ref[rows, :]
        bm = ba_ref[rows, :].astype(bf16)
        cmb = ca_ref[rows, :].astype(bf16)
        for j in range(QH):
            hs = slice(j * P, (j + 1) * P)
            xj = xa_ref[rows, hs]
            s_in = st_ref[1, j]
            y_ref[rows, hs] += jnp.exp(sfx[:, j:j + 1]) * jnp.dot(cmb, s_in.astype(bf16), preferred_element_type=f32)
            tot = sfx[0:1, j:j + 1]
            xw = xj * (jnp.exp(tot - sfx[:, j:j + 1]) * dtb[:, j:j + 1])
            st_ref[1, j] = jnp.exp(tot) * s_in + lax.dot_general(
                bm, xw.astype(bf16), (((0,), (0,)), ((), ())), preferred_element_type=f32)
        fs_ref[c, 1] = st_ref[1]
        return carry

    lax.fori_loop(0, nch, bwd, 0)


def ssd_scan(xbc, dt, conv_w, conv_b, sel, hp, s0, first, last, sb_rows):
    R = xbc.shape[0]
    n_sb = R // sb_rows
    nch = sb_rows // BLK
    nq = SSD_HEADS // SSD_QH
    qw = SSD_QH * SSD_P
    qpg = SSD_HPG // SSD_QH
    b_blk = SSD_INNER // SSD_N
    c_blk = b_blk + SSD_GROUPS
    tbl = jnp.asarray([first, last], jnp.int32)
    cw = conv_w
    cb = conv_b.reshape(1, -1)
    f32 = jnp.float32
    return pl.pallas_call(
        _ssd_kernel,
        grid_spec=pltpu.PrefetchScalarGridSpec(
            num_scalar_prefetch=1,
            grid=(n_sb, nq),
            in_specs=[pl.BlockSpec((sb_rows, qw), lambda s, q, t: (s, q)),
                      pl.BlockSpec((sb_rows, SSD_N), lambda s, q, t: (s, b_blk + q // qpg)),
                      pl.BlockSpec((sb_rows, SSD_N), lambda s, q, t: (s, c_blk + q // qpg)),
                      pl.BlockSpec((sb_rows, 128), lambda s, q, t: (s, 0)),
                      pl.BlockSpec((3, qw), lambda s, q, t: (0, q)),
                      pl.BlockSpec((3, SSD_N), lambda s, q, t: (0, b_blk + q // qpg)),
                      pl.BlockSpec((3, SSD_N), lambda s, q, t: (0, c_blk + q // qpg)),
                      pl.BlockSpec((1, qw), lambda s, q, t: (0, q)),
                      pl.BlockSpec((1, SSD_N), lambda s, q, t: (0, b_blk + q // qpg)),
                      pl.BlockSpec((1, SSD_N), lambda s, q, t: (0, c_blk + q // qpg)),
                      pl.BlockSpec((1, 128, 128), lambda s, q, t: (q, 0, 0)),
                      pl.BlockSpec((1, 8, 128), lambda s, q, t: (q, 0, 0)),
                      pl.BlockSpec((1, 2, SSD_QH, SSD_N, SSD_P), lambda s, q, t: (s, 0, q, 0, 0))],
            out_specs=[pl.BlockSpec((sb_rows, qw), lambda s, q, t: (s, q)),
                       pl.BlockSpec((nch, 2, SSD_QH, SSD_N, SSD_P), lambda s, q, t: (s, 0, q, 0, 0))],
            scratch_shapes=[pltpu.VMEM((sb_rows, qw), f32), pltpu.VMEM((sb_rows, SSD_N), f32),
                            pltpu.VMEM((sb_rows, SSD_N), f32), pltpu.VMEM((sb_rows, 128), f32),
                            pltpu.VMEM((sb_rows, 128), f32), pltpu.VMEM((2, SSD_QH, SSD_N, SSD_P), f32)]),
        out_shape=[jax.ShapeDtypeStruct((R, SSD_INNER), f32),
                   jax.ShapeDtypeStruct((R // BLK, 2, SSD_HEADS, SSD_N, SSD_P), f32)],
        compiler_params=pltpu.CompilerParams(dimension_semantics=("arbitrary", "arbitrary"),
                                             vmem_limit_bytes=SSD_VMEM_BYTES),
        name="ssd_scan",
    )(tbl, xbc, xbc, xbc, dt, cw, cw, cw, cb, cb, cb, sel, hp, s0)


def ssd_tables(p, e):
    nq = SSD_HEADS // SSD_QH
    lane = jnp.arange(128)
    sel = jnp.stack([(lane[:, None] == (q * SSD_QH + lane[None, :])) & (lane[None, :] < SSD_QH)
                     for q in range(nq)]).astype(jnp.bfloat16)
    a_neg = -jnp.exp(p['ssd_a_log'][e].astype(jnp.float32))
    rows = jnp.stack([p['ssd_dt_bias'][e, 0], p['ssd_dt_bias'][e, 1], a_neg[0], a_neg[1], p['ssd_d'][e]])
    hp = jnp.zeros((nq, 8, 128), jnp.float32)
    hp = hp.at[:, :5, :SSD_QH].set(jnp.transpose(rows.reshape(5, nq, SSD_QH), (1, 0, 2)))
    return sel, hp


MIX_VMEM_BYTES = 40 * 1024 * 1024
ROUTER_LANES = 128


def _residual_norm_router(x, out, g1, n2g, sc2, sh2, rw_ref, x_out_ref, hn_ref, aff_ref):
    x_new = x + g1 * out
    x_out_ref[...] = x_new
    hn = _adaln(x_new, n2g, sc2, sh2)
    hn_ref[...] = hn.astype(hn_ref.dtype)
    logits = _dot(hn, rw_ref[...], ((1,), (0,)), 3)
    lane = lax.broadcasted_iota(jnp.int32, logits.shape, 1)
    logits = jnp.where(lane < N_EXPERTS, logits, NEG_BIG)
    ex = jnp.exp(logits - jnp.max(logits, axis=-1, keepdims=True))
    aff = ex / jnp.sum(ex, axis=-1, keepdims=True)
    aff_ref[...] = aff.T[:N_EXPERTS]


def _l0_out_kernel(ys_ref, z_ref, yr_ref, bonus_ref, gate_ref, sg_ref, lnw_ref, lnb_ref, e_ref, et_ref, w_ref,
                   x_ref, g1_ref, n2g_ref, sc2_ref, sh2_ref, rw_ref, x_out_ref, hn_ref, aff_ref):
    f32, bf16 = jnp.float32, jnp.bfloat16
    z = z_ref[...]
    ys = ys_ref[...] * (z * jax.nn.sigmoid(z))
    gw = SSD_INNER // SSD_GROUPS
    parts = []
    for gi in range(SSD_GROUPS):
        yg = ys[:, gi * gw:(gi + 1) * gw]
        parts.append(yg * lax.rsqrt(jnp.mean(yg * yg, -1, keepdims=True) + NORM_EPS))
    a1 = jnp.concatenate(parts, axis=1) * sg_ref[...]
    o = yr_ref[0] + yr_ref[1]
    mu = _head_sum(o, e_ref, et_ref) * (1.0 / RWKV_N)
    oc = o - mu
    var = _head_sum(oc * oc, e_ref, et_ref) * (1.0 / RWKV_N)
    o = oc * lax.rsqrt(var + RWKV_GN_EPS) * lnw_ref[...] + lnb_ref[...]
    o = (o + bonus_ref[...]) * gate_ref[...]
    out = (jnp.dot(a1.astype(bf16), w_ref[:SSD_INNER], preferred_element_type=f32)
           + jnp.dot(o.astype(bf16), w_ref[SSD_INNER:], preferred_element_type=f32))
    _residual_norm_router(x_ref[...], out, g1_ref[0], n2g_ref[...], sc2_ref[0], sh2_ref[0], rw_ref,
                          x_out_ref, hn_ref, aff_ref)


def _l1_out_kernel(a_ref, w_ref, x_ref, g1_ref, n2g_ref, sc2_ref, sh2_ref, rw_ref, x_out_ref, hn_ref, aff_ref):
    out = jnp.dot(a_ref[...].astype(jnp.bfloat16), w_ref[...], preferred_element_type=jnp.float32)
    _residual_norm_router(x_ref[...], out, g1_ref[0], n2g_ref[...], sc2_ref[0], sh2_ref[0], rw_ref,
                          x_out_ref, hn_ref, aff_ref)


def _mix_out_call(kernel_fn, name, lead_args, lead_specs, w_out, x, g1b, n2g, sc2b, sh2b, router_w):
    R, D = x.shape
    nblk = R // BLK
    full = lambda a: pl.BlockSpec(a.shape, lambda i: (0,) * a.ndim)
    blkrow = pl.BlockSpec((1, 1, D), lambda i: (i, 0, 0))
    rw = jnp.zeros((D, ROUTER_LANES), jnp.float32).at[:, :N_EXPERTS].set(router_w)
    args = list(lead_args) + [w_out, x, g1b, n2g, sc2b, sh2b, rw]
    x_idx = len(lead_args) + 1
    in_specs = list(lead_specs) + [full(w_out), pl.BlockSpec((BLK, D), lambda i: (i, 0)), blkrow, full(n2g), blkrow,
                                   blkrow, full(rw)]
    return pl.pallas_call(
        kernel_fn,
        grid=(nblk,),
        in_specs=in_specs,
        out_specs=[pl.BlockSpec((BLK, D), lambda i: (i, 0)), pl.BlockSpec((BLK, D), lambda i: (i, 0)),
                   pl.BlockSpec((N_EXPERTS, BLK), lambda i: (0, i))],
        out_shape=[jax.ShapeDtypeStruct((R, D), jnp.float32), jax.ShapeDtypeStruct((R, D), jnp.bfloat16),
                   jax.ShapeDtypeStruct((N_EXPERTS, R), jnp.float32)],
        input_output_aliases={x_idx: 0},
        compiler_params=pltpu.CompilerParams(dimension_semantics=("arbitrary",), vmem_limit_bytes=MIX_VMEM_BYTES),
        name=name,
    )(*args)


def l0_out(ys, z, yr, bonus, gate, ssd_g, ln_w, ln_b, e_ind, et_ind, w_out, x, g1b, n2g, sc2b, sh2b, router_w):
    c = RWKV_DIM
    rows = lambda n: pl.BlockSpec((BLK, n), lambda i: (i, 0))
    full = lambda a: pl.BlockSpec(a.shape, lambda i: (0,) * a.ndim)
    lead = [ys, z, yr, bonus, gate, ssd_g, ln_w, ln_b, e_ind, et_ind]
    specs = [rows(SSD_INNER), rows(SSD_INNER), pl.BlockSpec((2, BLK, c), lambda i: (0, i, 0)), rows(c), rows(c),
             full(ssd_g), full(ln_w), full(ln_b), full(e_ind), full(et_ind)]
    return _mix_out_call(_l0_out_kernel, "l0_out", lead, specs, w_out, x, g1b, n2g, sc2b, sh2b, router_w)


def l1_out(a, w_out, x, g1b, n2g, sc2b, sh2b, router_w):
    specs = [pl.BlockSpec((BLK, a.shape[1]), lambda i: (i, 0))]
    return _mix_out_call(_l1_out_kernel, "l1_out", [a], specs, w_out, x, g1b, n2g, sc2b, sh2b, router_w)


def _l1_in_kernel(tbl_ref, x_ref, g_ref, sc_ref, sh_ref, w_ref, cos_ref, sin_ref, q_ref, k_ref, v_ref, gg_ref):
    f32 = jnp.float32
    hn = _adaln(x_ref[...], g_ref[...], sc_ref[0], sh_ref[0]).astype(jnp.bfloat16)
    cosf, sinf = cos_ref[0], sin_ref[0]

    def rope(x):
        parts = []
        for h in range(RET_HEADS):
            xh = x[:, h * RET_DK:(h + 1) * RET_DK]
            parts.append(xh * cosf + pltpu.roll(xh, RET_DK // 2, 1) * sinf)
        return jnp.concatenate(parts, axis=1)

    q_ref[...] = rope(jnp.dot(hn, w_ref[:, :RET_QK], preferred_element_type=f32))
    k_ref[...] = rope(jnp.dot(hn, w_ref[:, RET_QK:2 * RET_QK], preferred_element_type=f32)) * (RET_DK ** -0.5)
    v_ref[...] = jnp.dot(hn, w_ref[:, 2 * RET_QK:2 * RET_QK + RET_V], preferred_element_type=f32)
    gg_ref[...] = jnp.dot(hn, w_ref[:, 2 * RET_QK + RET_V:], preferred_element_type=f32)


def l1_in(x, g1, scb, shb, w_bf16, cos_t, sin_t, rope_blk):
    R, D = x.shape
    nblk = R // BLK
    tbl = jnp.asarray([rope_blk], jnp.int32)
    f32 = jnp.float32
    row = lambda n: pl.BlockSpec((BLK, n), lambda i, t: (i, 0))
    full = lambda a: pl.BlockSpec(a.shape, lambda i, t: (0,) * a.ndim)
    blkrow = pl.BlockSpec((1, 1, D), lambda i, t: (i, 0, 0))
    ropespec = pl.BlockSpec((1, BLK, RET_DK), lambda i, t: (t[0, i], 0, 0))
    return pl.pallas_call(
        _l1_in_kernel,
        grid_spec=pltpu.PrefetchScalarGridSpec(
            num_scalar_prefetch=1, grid=(nblk,),
            in_specs=[row(D), full(g1), blkrow, blkrow, full(w_bf16), ropespec, ropespec],
            out_specs=[row(RET_QK), row(RET_QK), row(RET_V), row(RET_V)]),
        out_shape=[jax.ShapeDtypeStruct((R, RET_QK), f32), jax.ShapeDtypeStruct((R, RET_QK), f32),
                   jax.ShapeDtypeStruct((R, RET_V), f32), jax.ShapeDtypeStruct((R, RET_V), f32)],
        compiler_params=pltpu.CompilerParams(dimension_semantics=("arbitrary",), vmem_limit_bytes=L0_VMEM_BYTES),
        name="l1_in",
    )(tbl, x, g1, scb, shb, w_bf16, cos_t, sin_t)


def _ret_kernel(tbl_ref, q_ref, k_ref, v_ref, g_ref, lg_ref, nw_ref, nb_ref, s0_ref, a_ref, fs_ref, st_ref):
    f32, bf16 = jnp.float32, jnp.bfloat16
    sb = pl.program_id(0)
    nch = q_ref.shape[0] // BLK
    lg = lg_ref[0]
    lgf, lgb = lg[0:1, 0:1], lg[1:2, 0:1]
    t_i = lax.broadcasted_iota(jnp.int32, (BLK, BLK), 0)
    s_i = lax.broadcasted_iota(jnp.int32, (BLK, BLK), 1)
    dist = (t_i - s_i).astype(f32)
    dm = (jnp.exp(jnp.where(s_i <= t_i, dist * lgf, NEG_BIG)) + jnp.exp(jnp.where(s_i >= t_i, -dist * lgb, NEG_BIG)))
    tcol = lax.broadcasted_iota(jnp.int32, (BLK, 1), 0).astype(f32)

    def chunk_rows(c):
        return pl.ds(pl.multiple_of(c * BLK, BLK), BLK)

    def fwd(c, carry):
        blk = sb * nch + c
        rows = chunk_rows(c)

        @pl.when(tbl_ref[0, blk] == 1)
        def _():
            st_ref[0] = s0_ref[0, 0, 0]

        q, k, v = q_ref[rows, :].astype(bf16), k_ref[rows, :], v_ref[rows, :].astype(bf16)
        g = lax.dot_general(q, k.astype(bf16), (((1,), (1,)), ((), ())), preferred_element_type=f32)
        s_in = st_ref[0]
        a_ref[rows, :] = (jnp.dot((g * dm).astype(bf16), v, preferred_element_type=f32)
                          + jnp.exp((tcol + 1.0) * lgf) * jnp.dot(q, s_in.astype(bf16), preferred_element_type=f32))
        kw = (k * jnp.exp((BLK - 1.0 - tcol) * lgf)).astype(bf16)
        st_ref[0] = jnp.exp(BLK * lgf) * s_in + lax.dot_general(kw, v, (((0,), (0,)), ((), ())),
                                                                preferred_element_type=f32)
        fs_ref[c, 0, 0] = st_ref[0]
        return carry

    lax.fori_loop(0, nch, fwd, 0)

    def bwd(j, carry):
        c = nch - 1 - j
        blk = sb * nch + c
        rows = chunk_rows(c)

        @pl.when(tbl_ref[1, blk] == 1)
        def _():
            st_ref[1] = s0_ref[0, 1, 0]

        q, k, v = q_ref[rows, :].astype(bf16), k_ref[rows, :], v_ref[rows, :].astype(bf16)
        s_in = st_ref[1]
        y = a_ref[rows, :] + jnp.exp((BLK - tcol) * lgb) * jnp.dot(q, s_in.astype(bf16), preferred_element_type=f32)
        kw = (k * jnp.exp(tcol * lgb)).astype(bf16)
        st_ref[1] = jnp.exp(BLK * lgb) * s_in + lax.dot_general(kw, v, (((0,), (0,)), ((), ())),
                                                                preferred_element_type=f32)
        fs_ref[c, 1, 0] = st_ref[1]
        mu = jnp.mean(y, -1, keepdims=True)
        yc = y - mu
        var = jnp.mean(yc * yc, -1, keepdims=True)
        gg = g_ref[rows, :]
        a_ref[rows, :] = (yc * lax.rsqrt(var + 1e-5) * nw_ref[...] + nb_ref[...]) * (gg * jax.nn.sigmoid(gg))
        return carry

    lax.fori_loop(0, nch, bwd, 0)


def ret_scan(q, k, v, g, lg_tab, norm_w, norm_b, s0, first, last, sb_rows):
    R = q.shape[0]
    n_sb = R // sb_rows
    nch = sb_rows // BLK
    tbl = jnp.asarray([first, last], jnp.int32)
    f32 = jnp.float32
    return pl.pallas_call(
        _ret_kernel,
        grid_spec=pltpu.PrefetchScalarGridSpec(
            num_scalar_prefetch=1, grid=(n_sb, RET_HEADS),
            in_specs=[pl.BlockSpec((sb_rows, RET_DK), lambda s, h, t: (s, h)),
                      pl.BlockSpec((sb_rows, RET_DK), lambda s, h, t: (s, h)),
                      pl.BlockSpec((sb_rows, RET_DV), lambda s, h, t: (s, h)),
                      pl.BlockSpec((sb_rows, RET_DV), lambda s, h, t: (s, h)),
                      pl.BlockSpec((1, 8, 128), lambda s, h, t: (h, 0, 0)),
                      pl.BlockSpec((1, RET_DV), lambda s, h, t: (0, h)),
                      pl.BlockSpec((1, RET_DV), lambda s, h, t: (0, h)),
                      pl.BlockSpec((1, 2, 1, RET_DK, RET_DV), lambda s, h, t: (s, 0, h, 0, 0))],
            out_specs=[pl.BlockSpec((sb_rows, RET_DV), lambda s, h, t: (s, h)),
                       pl.BlockSpec((nch, 2, 1, RET_DK, RET_DV), lambda s, h, t: (s, 0, h, 0, 0))],
            scratch_shapes=[pltpu.VMEM((2, RET_DK, RET_DV), f32)]),
        out_shape=[jax.ShapeDtypeStruct((R, RET_V), f32),
                   jax.ShapeDtypeStruct((R // BLK, 2, RET_HEADS, RET_DK, RET_DV), f32)],
        compiler_params=pltpu.CompilerParams(dimension_semantics=("arbitrary", "arbitrary"),
                                             vmem_limit_bytes=SSD_VMEM_BYTES),
        name="ret_scan",
    )(tbl, q, k, v, g, lg_tab, norm_w.reshape(1, -1), norm_b.reshape(1, -1), s0)


def _mod_kernel(c_ref, w_ref, b_ref, o_ref):
    c = c_ref[...]
    act = c * jax.nn.sigmoid(c)
    o_ref[0] = _dot(act, w_ref[0], ((1,), (0,)), 3) + b_ref[0]


def mod_vectors(conds, mod_w, mod_b):
    depth, D, n6 = mod_w.shape
    tn = D
    return pl.pallas_call(
        _mod_kernel,
        grid=(depth, n6 // tn),
        in_specs=[pl.BlockSpec(conds.shape, lambda i, j: (0, 0)),
                  pl.BlockSpec((1, D, tn), lambda i, j: (i, 0, j)),
                  pl.BlockSpec((1, 1, tn), lambda i, j: (i, 0, j))],
        out_specs=pl.BlockSpec((1, conds.shape[0], tn), lambda i, j: (i, 0, j)),
        out_shape=jax.ShapeDtypeStruct((depth, conds.shape[0], n6), jnp.float32),
        name="mod_vectors",
    )(conds, mod_w, mod_b.reshape(depth, 1, n6))


def rope_tables(n_tokens):
    rows = n_tokens // GRID_W
    row = jnp.repeat(jnp.arange(rows), GRID_W).astype(jnp.float32)
    col = jnp.tile(jnp.arange(GRID_W), rows).astype(jnp.float32)
    n_f = RET_DK // 4
    inv = ROPE_BASE ** (-jnp.arange(n_f, dtype=jnp.float32) / n_f)
    ang = jnp.concatenate([row[:, None] * inv, col[:, None] * inv], -1)
    return jnp.cos(ang), jnp.sin(ang)


def _rope_block_tables(n_ctx, n_lat, l_lat):
    nb = l_lat // BLK
    cos, sin = rope_tables(l_lat)
    cosf = jnp.concatenate([cos, cos], -1).reshape(nb, BLK, RET_DK)
    sinf = jnp.concatenate([-sin, sin], -1).reshape(nb, BLK, RET_DK)
    cos_t = jnp.concatenate([jnp.ones((1, BLK, RET_DK), jnp.float32), cosf])
    sin_t = jnp.concatenate([jnp.zeros((1, BLK, RET_DK), jnp.float32), sinf])
    rope_blk = [0] * n_ctx + [1 + j for _ in range(n_lat) for j in range(nb)]
    return cos_t, sin_t, rope_blk


def kernel(x_prompt, x_sample, state_ssd, state_rwkv, state_ret, c, c_ctx, mod_w, mod_b, norm1_g, norm2_g,
           router_w, exp_w_gate, exp_w_up, exp_w_down, ab_w_in, ab_w_out, ssd_conv_w, ssd_conv_b, ssd_dt_bias,
           ssd_a_log, ssd_d, ssd_norm_g, rwkv_mu_prev, rwkv_mu_next, rwkv_w0, rwkv_w2, rwkv_a0, rwkv_a2, rwkv_g2,
           rwkv_k_k, rwkv_k_a, rwkv_r_k, rwkv_ln_w, rwkv_ln_b, ret_w_in, ret_w_out, ret_decay_logit, ret_norm_w,
           ret_norm_b, final_norm_g):
    p = dict(mod_w=mod_w, mod_b=mod_b, norm1_g=norm1_g, norm2_g=norm2_g, router_w=router_w,
             exp_w_gate=exp_w_gate, exp_w_up=exp_w_up, exp_w_down=exp_w_down, ab_w_in=ab_w_in, ab_w_out=ab_w_out,
             ssd_conv_w=ssd_conv_w, ssd_conv_b=ssd_conv_b, ssd_dt_bias=ssd_dt_bias, ssd_a_log=ssd_a_log,
             ssd_d=ssd_d, ssd_norm_g=ssd_norm_g, rwkv_mu_prev=rwkv_mu_prev, rwkv_mu_next=rwkv_mu_next,
             rwkv_w0=rwkv_w0, rwkv_w2=rwkv_w2, rwkv_a0=rwkv_a0, rwkv_a2=rwkv_a2, rwkv_g2=rwkv_g2,
             rwkv_k_k=rwkv_k_k, rwkv_k_a=rwkv_k_a, rwkv_r_k=rwkv_r_k, rwkv_ln_w=rwkv_ln_w, rwkv_ln_b=rwkv_ln_b,
             ret_w_in=ret_w_in, ret_w_out=ret_w_out, ret_decay_logit=ret_decay_logit, ret_norm_w=ret_norm_w,
             ret_norm_b=ret_norm_b, final_norm_g=final_norm_g)
    f32, bf16 = jnp.float32, jnp.bfloat16
    n_ctx, l_ctx, D = x_prompt.shape
    n_lat, l_lat, _ = x_sample.shape
    assert l_ctx == BLK and l_lat % BLK == 0 and (n_ctx * BLK) % l_lat == 0
    n_sb_ctx = n_ctx * BLK // l_lat
    cond_id, first, last = _seq_tables(n_ctx, n_lat, l_lat)
    x = jnp.concatenate([x_prompt.reshape(-1, D), x_sample.reshape(-1, D)])

    conds = jnp.concatenate([c_ctx[None, :], c, jnp.zeros((8 - 1 - n_lat, D), f32)])
    mods = mod_vectors(conds, mod_w, mod_b)[:, jnp.asarray(cond_id)]
    mods = mods.reshape(DEPTH, len(cond_id), 6, 1, D)

    def with_ctx_zeros(state, n_zero):
        return jnp.concatenate([jnp.zeros((n_zero,) + state.shape[1:], f32), state])

    new_ssd, new_rwkv, new_ret = [], [], []
    out = None
    for i in range(DEPTH):
        sh1, sc1, g1, sh2, sc2, g2 = (mods[i, :, k] for k in range(6))
        e = i // 2
        if i % 2 == 0:
            w_packed, mup, mun, rwp, w2bd, a2p, g2p, e_ind, et_ind = l0_pack_weights(p, e)
            z, xbc, dt, r, v, an, lw, kd, bv, gate, bonus = l0_in(
                x, norm1_g[i][None], sc1, sh1, w_packed, mup, mun, rwp, w2bd, a2p, g2p, e_ind, et_ind, first, last)
            sel, hp = ssd_tables(p, e)
            ys, fs_ssd = ssd_scan(xbc, dt, ssd_conv_w[e], ssd_conv_b[e], sel, hp,
                                  with_ctx_zeros(state_ssd[:, e], n_sb_ctx), first, last, l_lat)
            new_ssd.append(fs_ssd[:n_ctx])
            s0_rwkv = jnp.transpose(state_rwkv[:, e], (0, 1, 4, 2, 3)).reshape(n_lat, 2, RWKV_N, RWKV_DIM)
            yr, sf_rwkv = rwkv_scan_pallas(r, v, an, lw, kd, bv, with_ctx_zeros(s0_rwkv, n_ctx),
                                           _rwkv_steps(n_ctx, n_lat, l_lat), p_inv=3, p_oth=1)
            new_rwkv.append(jnp.transpose(sf_rwkv[:n_ctx].reshape(n_ctx, 2, RWKV_N, RWKV_HEADS, RWKV_N),
                                          (0, 1, 3, 4, 2)))
            x, hn2, affT = l0_out(ys, z, yr, bonus, gate, ssd_norm_g[e][None], rwkv_ln_w[e][None], rwkv_ln_b[e][None],
                                  e_ind, et_ind, ab_w_out[e].astype(bf16), x, g1, norm2_g[i][None], sc2, sh2,
                                  router_w[i])
        else:
            cos_t, sin_t, rope_blk = _rope_block_tables(n_ctx, n_lat, l_lat)
            q, k, v, gg = l1_in(x, norm1_g[i][None], sc1, sh1, ret_w_in[e].astype(bf16), cos_t, sin_t, rope_blk)
            lg = jax.nn.log_sigmoid(ret_decay_logit[e].astype(f32))
            lg_tab = jnp.zeros((RET_HEADS, 8, 128), f32).at[:, :2, :].set(jnp.transpose(lg)[:, :, None])
            a, fs_ret = ret_scan(q, k, v, gg, lg_tab, ret_norm_w[e], ret_norm_b[e],
                                 with_ctx_zeros(state_ret[:, e], n_sb_ctx), first, last, l_lat)
            new_ret.append(fs_ret[:n_ctx])
            x, hn2, affT = l1_out(a, ret_w_out[e].astype(bf16), x, g1, norm2_g[i][None], sc2, sh2, router_w[i])
        fin = final_norm_g if i == DEPTH - 1 else None
        out = moe_layer(x, hn2, affT, g2, exp_w_gate, exp_w_up, exp_w_down, i, n_ctx, l_lat, final_g=fin)
        if fin is None:
            x = out
    y_ctx, y_lat = out
    return (y_ctx.reshape(n_ctx, l_ctx, D), y_lat.reshape(n_lat, l_lat, D),
            jnp.stack(new_ssd, 1), jnp.stack(new_rwkv, 1), jnp.stack(new_ret, 1))
```

```python
import functools
import math

import jax
import jax.numpy as jnp
from jax import lax
from jax.experimental import pallas as pl
from jax.experimental.pallas import tpu as pltpu

D_MODEL = 1024
DEPTH = 2
GRID_W = 64
CHUNK = 128
NORM_EPS = 1e-6
SSD_HEADS = 16
SSD_P = 64
SSD_INNER = SSD_HEADS * SSD_P
SSD_GROUPS = 2
SSD_HPG = SSD_HEADS // SSD_GROUPS
SSD_N = 128
SSD_XBC = SSD_INNER + 2 * SSD_GROUPS * SSD_N
SSD_IN = SSD_INNER + SSD_XBC + SSD_HEADS
RWKV_HEADS = 16
RWKV_N = 64
RWKV_DIM = RWKV_HEADS * RWKV_N
W_LORA = 64
A_LORA = 64
G_LORA = 128
RWKV_GN_EPS = 64e-5
RET_HEADS = 8
RET_DK = 128
RET_DV = 256
RET_QK = RET_HEADS * RET_DK
RET_V = RET_HEADS * RET_DV
ROPE_BASE = 10000.0
N_EXPERTS = 16
EC_CAPACITY = 2

ACT_DTYPE = jnp.bfloat16

RWKV_C = 64
RWKV_GH = 4
RWKV_GL = RWKV_GH * RWKV_N


def _split_bf16(x):
    hi = x.astype(jnp.bfloat16)
    lo = (x - hi.astype(jnp.float32)).astype(jnp.bfloat16)
    return hi, lo


def _dot(a, b, dims, passes):
    f = functools.partial(lax.dot_general, dimension_numbers=(dims, ((), ())),
                          preferred_element_type=jnp.float32)
    if passes == 1:
        return f(a.astype(jnp.bfloat16), b.astype(jnp.bfloat16))
    ah, al = _split_bf16(a)
    bh, bl = _split_bf16(b)
    return f(ah, bh) + (f(ah, bl) + f(al, bh))


def _rwkv_chunk_kernel(tbl_ref, r0_ref, v0_ref, a0_ref, r1_ref, v1_ref, a1_ref, lw0_ref, k0_ref, b0_ref,
                       lw1_ref, k1_ref, b1_ref, s0_ref, y0_ref, y1_ref, sf_ref, h_ref, *, p_inv, p_oth):
    C, N, GH, GL = RWKV_C, RWKV_N, RWKV_GH, RWKV_GL
    i = pl.program_id(0)
    f32, bf16 = jnp.float32, jnp.bfloat16

    @pl.when(tbl_ref[3, i] == 1)
    def _():
        h_ref[...] = s0_ref[0]

    t_i = lax.broadcasted_iota(jnp.int32, (C, GL), 0)
    s_i = lax.broadcasted_iota(jnp.int32, (C, GL), 1) & (N - 1)
    eye = (s_i == t_i).astype(f32)
    row2 = lax.broadcasted_iota(jnp.int32, (2 * C, GL), 0)
    rel2 = (lax.broadcasted_iota(jnp.int32, (2 * C, GL), 1) & (N - 1)) - (row2 & (C - 1))
    incl2 = row2 // C
    mask2 = [rel2 - incl2 < 0, -rel2 - incl2 < 0]
    bh_r = lax.broadcasted_iota(jnp.int32, (GL, GL), 0) // N
    bh_c = lax.broadcasted_iota(jnp.int32, (GL, GL), 1) // N
    blk = bh_r == bh_c
    tt = lax.broadcasted_iota(jnp.int32, (C, C), 0)
    ss = lax.broadcasted_iota(jnp.int32, (C, C), 1)
    tri = [(ss <= tt).astype(bf16), (ss >= tt).astype(bf16)]

    def bd(x, passes):
        pieces = []
        for _ in range(1 if passes == 1 else 2):
            hi = x.astype(bf16)
            x = x - hi.astype(f32)
            pieces.append(jnp.where(blk, jnp.concatenate([hi] * GH, axis=0), jnp.zeros((), bf16)))
        return pieces

    def mm(l, x, passes, dims=((1,), (0,))):
        f = functools.partial(lax.dot_general, dimension_numbers=(dims, ((), ())), preferred_element_type=f32)
        xs = bd(x, passes)
        lh = l.astype(bf16)
        if passes == 1:
            return f(lh, xs[0])
        ll = (l - lh.astype(f32)).astype(bf16)
        m = l.shape[0]
        both = f(jnp.concatenate([lh, ll], axis=0), xs[0])
        return both[:m] + (f(lh, xs[1]) + both[m:])

    nt = ((1,), (1,))
    refs = [(r0_ref, v0_ref, a0_ref, lw0_ref, k0_ref, b0_ref), (r1_ref, v1_ref, a1_ref, lw1_ref, k1_ref, b1_ref)]
    lw, r_t, a_t, b_t, k_t, v = [], [], [], [], [], []
    for d, (r_ref, v_ref, a_ref, lw_ref, k_ref, b_ref) in enumerate(refs):
        lwd = lw_ref[0]
        lw_hi, lw_lo = _split_bf16(lwd)
        cum = (jnp.dot(tri[d], lw_hi, preferred_element_type=f32) + jnp.dot(tri[d], lw_lo, preferred_element_type=f32))
        w_inv = jnp.exp(-cum)
        lw.append(lwd)
        r_t.append(r_ref[...].astype(f32) * jnp.exp(cum))
        a_t.append(a_ref[...].astype(f32) * jnp.exp(cum - lwd))
        b_t.append(b_ref[0].astype(f32) * w_inv)
        k_t.append(k_ref[0].astype(f32) * w_inv)
        v.append(v_ref[...].astype(f32))
    n_i = lax.broadcasted_iota(jnp.int32, (N, GL), 0)
    lane_i = lax.broadcasted_iota(jnp.int32, (N, GL), 1)
    diag_sel = (lane_i & (N - 1)) == n_i
    lane_head = lane_i // N
    ones_bd = blk.astype(f32)

    chains = [(d, slice(g * GL, (g + 1) * GL)) for d in range(2) for g in range(RWKV_HEADS // GH)]
    each = lambda fn: [fn(j, d, sl) for j, (d, sl) in enumerate(chains)]
    bg = each(lambda j, d, sl: b_t[d][:, sl])
    kg = each(lambda j, d, sl: k_t[d][:, sl])
    vg = each(lambda j, d, sl: v[d][:, sl])
    h0 = each(lambda j, d, sl: h_ref[d, :, sl])
    ar = each(lambda j, d, sl: jnp.concatenate([a_t[d][:, sl], r_t[d][:, sl]], axis=0))
    m_b = each(lambda j, d, sl: jnp.where(mask2[d], mm(ar[j], bg[j], p_oth, nt), 0.0))
    m_k = each(lambda j, d, sl: jnp.where(mask2[d], mm(ar[j], kg[j], p_oth, nt), 0.0))
    p = each(lambda j, d, sl: mm(m_b[j][:C], m_b[j][:C], p_inv))
    tmat = each(lambda j, d, sl: eye + m_b[j][:C])
    for _ in range(int(math.log2(C)) - 2):
        pt = each(lambda j, d, sl: mm(jnp.concatenate([p[j], tmat[j]], axis=0), p[j], p_inv))
        p = each(lambda j, d, sl: pt[j][:C])
        tmat = each(lambda j, d, sl: tmat[j] + pt[j][C:])
    tmat = each(lambda j, d, sl: tmat[j] + mm(tmat[j], p[j], p_inv))
    ar_h = each(lambda j, d, sl: mm(ar[j], h0[j], p_oth))
    mk_v = each(lambda j, d, sl: mm(m_k[j], vg[j], p_oth))
    u = each(lambda j, d, sl: mm(tmat[j], ar_h[j][:C] + mk_v[j][:C], p_oth))
    y = each(lambda j, d, sl: ar_h[j][C:] + mm(m_b[j][C:], u[j], p_oth) + mk_v[j][C:])
    full = each(lambda j, d, sl: _dot(jnp.concatenate([bg[j], kg[j]], axis=0), jnp.concatenate([u[j], vg[j]], axis=0),
                                      ((0,), (0,)), p_oth))
    y_refs = (y0_ref, y1_ref)
    for j, (d, sl) in enumerate(chains):
        y_refs[d][:, sl] = y[j].astype(y_refs[d].dtype)
        z = jnp.zeros((N, GL), f32)
        for hh in range(GH):
            z = z + jnp.where(lane_head == hh, full[j][hh * N:(hh + 1) * N], 0.0)
        w_tot = jnp.exp(jnp.sum(lw[d][:, sl], axis=0, keepdims=True))
        wsel = jnp.where(diag_sel, jnp.broadcast_to(w_tot, (N, GL)), 0.0)
        wcol = _dot(wsel, ones_bd, ((1,), (0,)), 3)
        h_ref[d, :, sl] = wcol * (h0[j] + z)

    @pl.when(tbl_ref[4, i] == 1)
    def _():
        sf_ref[0] = h_ref[...]


def _rwkv_steps(n_ctx, n_lat, l_lat):
    C = RWKV_C
    rows = []
    seqs = [(s, s * BLK, BLK) for s in range(n_ctx)] + [(n_ctx + s, n_ctx * BLK + s * l_lat, l_lat) for s in range(n_lat)]
    for sid, row0, length in seqs:
        nc = length // C
        for j in range(nc):
            rows.append((row0 // C + j, row0 // C + nc - 1 - j, sid, int(j == 0), int(j == nc - 1)))
    return [list(col) for col in zip(*rows)]


def rwkv_scan_pallas(r, v, a, lw, k, b, s0, steps, p_inv=3, p_oth=1):
    R, HN = r.shape
    C, N = RWKV_C, RWKV_N
    tbl = jnp.asarray(steps, jnp.int32)
    fwd = pl.BlockSpec((C, HN), lambda i, t: (t[0, i], 0))
    bwd = pl.BlockSpec((C, HN), lambda i, t: (t[1, i], 0))
    fwd_d = pl.BlockSpec((1, C, HN), lambda i, t: (0, t[0, i], 0))
    bwd_d = pl.BlockSpec((1, C, HN), lambda i, t: (1, t[1, i], 0))
    st = pl.BlockSpec((1, 2, N, HN), lambda i, t: (t[2, i], 0, 0, 0))
    return pl.pallas_call(
        functools.partial(_rwkv_chunk_kernel, p_inv=p_inv, p_oth=p_oth),
        grid_spec=pltpu.PrefetchScalarGridSpec(
            num_scalar_prefetch=1,
            grid=(len(steps[0]),),
            in_specs=[fwd, fwd, fwd, bwd, bwd, bwd, fwd_d, fwd_d, fwd_d, bwd_d, bwd_d, bwd_d, st],
            out_specs=[fwd, bwd, st],
            scratch_shapes=[pltpu.VMEM((2, N, HN), jnp.float32)]),
        out_shape=[jax.ShapeDtypeStruct((R, HN), ACT_DTYPE), jax.ShapeDtypeStruct((R, HN), ACT_DTYPE),
                   jax.ShapeDtypeStruct(s0.shape, jnp.float32)],
        compiler_params=pltpu.CompilerParams(dimension_semantics=("arbitrary",)),
        name="rwkv_scan",
    )(tbl, r, v, a, r, v, a, lw, k, b, lw, k, b, s0)


BLK = 256
FF_TILE = 512
SELECT_TILE = 512
GATHER_ROWS = 512
SELECT_MIN_EXP = -1100.0
SELECT_BINADE_STEPS = 11
SELECT_MANTISSA_STEPS = 40
MOE_FFN_VMEM_BYTES = 48 * 1024 * 1024


def _moe_select_kernel(aff_ref, slot_ref, *, cap):
    a = aff_ref[...]
    E, T = a.shape
    f32 = jnp.float32

    def enough(piv):
        return jnp.sum(jnp.where(a >= piv, 1.0, 0.0), axis=1, keepdims=True) >= cap

    def binade(_, lohi):
        e_lo, e_hi = lohi
        mid = jnp.floor((e_lo + e_hi) * 0.5)
        ok = enough(jnp.exp2(mid))
        return jnp.where(ok, mid, e_lo), jnp.where(ok, e_hi, mid)

    e_lo, e_hi = lax.fori_loop(0, SELECT_BINADE_STEPS, binade,
                               (jnp.full((E, 1), SELECT_MIN_EXP, f32), jnp.full((E, 1), 1.0, f32)))

    def inside(_, lohi):
        lo, hi = lohi
        mid = lo + (hi - lo) * 0.5
        ok = enough(mid)
        return jnp.where(ok, mid, lo), jnp.where(ok, hi, mid)

    thr, _ = lax.fori_loop(0, SELECT_MANTISSA_STEPS, inside, (jnp.exp2(e_lo), jnp.exp2(e_hi)))
    gt = a > thr
    eq = a == thr
    need = cap - jnp.sum(jnp.where(gt, 1.0, 0.0), axis=1, keepdims=True)
    tw = min(T, SELECT_TILE)

    def prefix_count(mask):
        m = jnp.where(mask, 1.0, 0.0).astype(jnp.bfloat16)
        outs = []
        for j in range(T // tw):
            s_i = lax.broadcasted_iota(jnp.int32, (T, tw), 0)
            t_i = lax.broadcasted_iota(jnp.int32, (T, tw), 1) + j * tw
            before = jnp.where(s_i < t_i, 1.0, 0.0).astype(jnp.bfloat16)
            outs.append(jnp.dot(m, before, preferred_element_type=f32))
        return outs[0] if len(outs) == 1 else jnp.concatenate(outs, axis=1)

    sel = gt | (eq & (prefix_count(eq) < need))
    slot_ref[...] = jnp.where(sel, prefix_count(sel).astype(jnp.int32), -1)


def _moe_select(affT, row0, n_seq, t):
    E = affT.shape[0]
    b0 = row0 // t
    return pl.pallas_call(
        functools.partial(_moe_select_kernel, cap=EC_CAPACITY * t // N_EXPERTS),
        grid=(n_seq,),
        in_specs=[pl.BlockSpec((E, t), lambda s: (0, b0 + s))],
        out_specs=pl.BlockSpec((E, t), lambda s: (0, s)),
        out_shape=jax.ShapeDtypeStruct((E, n_seq * t), jnp.int32),
        name="moe_select",
    )(affT)


def _moe_gather_kernel(slot_ref, aff_ref, hn_ref, xe_ref, gate_ref, *, cap):
    slot = slot_ref[...]
    eg, _, T = slot.shape
    p_i = lax.broadcasted_iota(jnp.int32, (eg, cap, T), 1)
    hit = slot == p_i
    onehot = jnp.where(hit, 1.0, 0.0).reshape(eg * cap, T).astype(jnp.bfloat16)
    xe = jnp.dot(onehot, hn_ref[...], preferred_element_type=jnp.float32)
    xe_ref[...] = xe.reshape(eg, cap, xe.shape[1]).astype(xe_ref.dtype)
    g = jnp.sum(jnp.where(hit, aff_ref[...], 0.0), axis=2, keepdims=True)
    gate_ref[...] = jnp.broadcast_to(g, (eg, cap, 128))


def _moe_gather(slot3, aff3, hn, row0, n_seq, t):
    E = slot3.shape[0]
    D = hn.shape[1]
    cap = EC_CAPACITY * t // N_EXPERTS
    eg = min(E, GATHER_ROWS // cap)
    b0 = row0 // t
    return pl.pallas_call(
        functools.partial(_moe_gather_kernel, cap=cap),
        grid=(n_seq, E // eg),
        in_specs=[pl.BlockSpec((eg, 1, t), lambda s, e: (e, 0, b0 + s)),
                  pl.BlockSpec((eg, 1, t), lambda s, e: (e, 0, b0 + s)),
                  pl.BlockSpec((t, D), lambda s, e: (b0 + s, 0))],
        out_specs=[pl.BlockSpec((eg, cap, D), lambda s, e: (e, s, 0)),
                   pl.BlockSpec((eg, cap, 128), lambda s, e: (e, s, 0))],
        out_shape=[jax.ShapeDtypeStruct((E, n_seq * cap, D), jnp.bfloat16),
                   jax.ShapeDtypeStruct((E, n_seq * cap, 128), jnp.float32)],
        compiler_params=pltpu.CompilerParams(dimension_semantics=("arbitrary", "arbitrary"),
                                             vmem_limit_bytes=MOE_FFN_VMEM_BYTES),
        name="moe_gather",
    )(slot3, aff3, hn)


def _moe_ffn_kernel(xc_ref, xl_ref, gc_ref, gl_ref, wg_ref, wu_ref, wd_ref, yc_ref, yl_ref, accc_ref, accl_ref):
    f = pl.program_id(1)
    nf = pl.num_programs(1)
    bf16 = jnp.bfloat16
    wg = wg_ref[0, 0].astype(bf16)
    wu = wu_ref[0, 0].astype(bf16)
    wd = wd_ref[0, 0].astype(bf16)

    def part(x_ref, acc_ref):
        x = x_ref[0]
        g = jnp.dot(x, wg, preferred_element_type=jnp.float32)
        u = jnp.dot(x, wu, preferred_element_type=jnp.float32)
        h = (g * jax.nn.sigmoid(g) * u).astype(bf16)
        y = jnp.dot(h, wd, preferred_element_type=jnp.float32)

        @pl.when(f == 0)
        def _():
            acc_ref[...] = y

        @pl.when(f != 0)
        def _():
            acc_ref[...] += y

    part(xc_ref, accc_ref)
    part(xl_ref, accl_ref)

    @pl.when(f == nf - 1)
    def _():
        for acc_ref, g_ref, y_ref in ((accc_ref, gc_ref, yc_ref), (accl_ref, gl_ref, yl_ref)):
            gate = jnp.concatenate([g_ref[0]] * (acc_ref.shape[1] // 128), axis=1)
            y_ref[0] = (acc_ref[...] * gate).astype(y_ref.dtype)


def _moe_ffn(xc, xl, gc, gl, wg, wu, wd, layer):
    E, nc_rows, D = xc.shape
    nl_rows = xl.shape[1]
    F = wg.shape[3]
    nf = F // FF_TILE
    return pl.pallas_call(
        _moe_ffn_kernel,
        grid=(E, nf),
        in_specs=[pl.BlockSpec((1, nc_rows, D), lambda e, f: (e, 0, 0)),
                  pl.BlockSpec((1, nl_rows, D), lambda e, f: (e, 0, 0)),
                  pl.BlockSpec((1, nc_rows, 128), lambda e, f: (e, 0, 0)),
                  pl.BlockSpec((1, nl_rows, 128), lambda e, f: (e, 0, 0)),
                  pl.BlockSpec((1, 1, D, FF_TILE), lambda e, f: (layer, e, 0, f)),
                  pl.BlockSpec((1, 1, D, FF_TILE), lambda e, f: (layer, e, 0, f)),
                  pl.BlockSpec((1, 1, FF_TILE, D), lambda e, f: (layer, e, f, 0))],
        out_specs=[pl.BlockSpec((1, nc_rows, D), lambda e, f: (e, 0, 0)),
                   pl.BlockSpec((1, nl_rows, D), lambda e, f: (e, 0, 0))],
        out_shape=[jax.ShapeDtypeStruct((E, nc_rows, D), jnp.bfloat16),
                   jax.ShapeDtypeStruct((E, nl_rows, D), jnp.bfloat16)],
        scratch_shapes=[pltpu.VMEM((nc_rows, D), jnp.float32), pltpu.VMEM((nl_rows, D), jnp.float32)],
        compiler_params=pltpu.CompilerParams(dimension_semantics=("arbitrary", "arbitrary"),
                                             vmem_limit_bytes=MOE_FFN_VMEM_BYTES),
        name="moe_ffn",
    )(xc, xl, gc, gl, wg, wu, wd)


def _moe_scatter_kernel(slot_ref, ye_ref, x_ref, g2_ref, *rest, cap, final):
    if final:
        fg_ref, o_ref = rest
    else:
        (o_ref,) = rest
    E = ye_ref.shape[0]
    D = ye_ref.shape[2]
    tb = x_ref.shape[0]
    slot = slot_ref[...]
    p_i = lax.broadcasted_iota(jnp.int32, (E, cap, tb), 1)
    onehot = jnp.where(slot == p_i, 1.0, 0.0).reshape(E * cap, tb).astype(jnp.bfloat16)
    ye = ye_ref[...].reshape(E * cap, D)
    moe = lax.dot_general(onehot, ye, (((0,), (0,)), ((), ())), preferred_element_type=jnp.float32)
    x = x_ref[...] + g2_ref[0] * moe
    if final:
        x = x * lax.rsqrt(jnp.mean(x * x, -1, keepdims=True) + NORM_EPS) * fg_ref[...]
    o_ref[...] = x


def _moe_scatter(slot3, ye, x, g2blk, row0, n_seq, t, final_g=None):
    E, _, D = ye.shape
    cap = EC_CAPACITY * t // N_EXPERTS
    nb = t // BLK
    b0 = row0 // BLK
    final = final_g is not None
    in_specs = [pl.BlockSpec((E, 1, BLK), lambda s, j: (0, 0, b0 + s * nb + j)),
                pl.BlockSpec((E, cap, D), lambda s, j: (0, s, 0)),
                pl.BlockSpec((BLK, D), lambda s, j: (b0 + s * nb + j, 0)),
                pl.BlockSpec((1, 1, D), lambda s, j: (b0 + s * nb + j, 0, 0))]
    args = [slot3, ye, x, g2blk]
    if final:
        in_specs.append(pl.BlockSpec((1, D), lambda s, j: (0, 0)))
        args.append(final_g.reshape(1, D))
        out_specs = pl.BlockSpec((BLK, D), lambda s, j: (s * nb + j, 0))
        out_shape = jax.ShapeDtypeStruct((n_seq * t, D), jnp.float32)
        aliases = {}
    else:
        out_specs = pl.BlockSpec((BLK, D), lambda s, j: (b0 + s * nb + j, 0))
        out_shape = jax.ShapeDtypeStruct(x.shape, jnp.float32)
        aliases = {2: 0}
    return pl.pallas_call(
        functools.partial(_moe_scatter_kernel, cap=cap, final=final),
        grid=(n_seq, nb),
        in_specs=in_specs, out_specs=out_specs, out_shape=out_shape,
        input_output_aliases=aliases,
        compiler_params=pltpu.CompilerParams(dimension_semantics=("arbitrary", "arbitrary"),
                                             vmem_limit_bytes=MOE_FFN_VMEM_BYTES),
        name="moe_scatter",
    )(*args)


def moe_layer(x, hn, affT, g2blk, wg, wu, wd, layer, n_ctx, l_lat, final_g=None):
    R = x.shape[0]
    r_ctx = n_ctx * BLK
    n_lat = (R - r_ctx) // l_lat
    slot = jnp.concatenate([_moe_select(affT, 0, n_ctx, BLK), _moe_select(affT, r_ctx, n_lat, l_lat)], axis=1)
    slot3 = slot[:, None, :]
    aff3 = affT[:, None, :]
    xc, gc = _moe_gather(slot3, aff3, hn, 0, n_ctx, BLK)
    xl, gl = _moe_gather(slot3, aff3, hn, r_ctx, n_lat, l_lat)
    yc, yl = _moe_ffn(xc, xl, gc, gl, wg, wu, wd, layer)
    if final_g is None:
        x = _moe_scatter(slot3, yc, x, g2blk, 0, n_ctx, BLK)
        return _moe_scatter(slot3, yl, x, g2blk, r_ctx, n_lat, l_lat)
    return (_moe_scatter(slot3, yc, x, g2blk, 0, n_ctx, BLK, final_g),
            _moe_scatter(slot3, yl, x, g2blk, r_ctx, n_lat, l_lat, final_g))


L0_Z = (0, 1024)
L0_XBC = (1024, 2560)
L0_SHIFT = (2560, 6016)
L0_DT = (6016, 6144)
L0_COLS = 6144
L0_VMEM_BYTES = 56 * 1024 * 1024


def _sum_split(x, m, n_split):
    acc = None
    for _ in range(n_split):
        hi = x.astype(jnp.bfloat16)
        x = x - hi.astype(jnp.float32)
        t = jnp.dot(hi, m, preferred_element_type=jnp.float32)
        acc = t if acc is None else acc + t
    return acc


def _sum_split_left(m, x, n_split):
    acc = None
    for _ in range(n_split):
        hi = x.astype(jnp.bfloat16)
        x = x - hi.astype(jnp.float32)
        t = jnp.dot(m, hi, preferred_element_type=jnp.float32)
        acc = t if acc is None else acc + t
    return acc


def _head_sum(x, e_ref, et_ref):
    return _sum_split(_sum_split(x, e_ref[...], 2), et_ref[...], 2)


def _adaln(x, g, sc, sh):
    y = x * lax.rsqrt(jnp.mean(x * x, -1, keepdims=True) + NORM_EPS) * g
    return y * (1.0 + sc) + sh


def _softplus(x):
    return jnp.maximum(x, 0.0) + jnp.log(1.0 + jnp.exp(-jnp.abs(x)))


def _l0_in_kernel(tbl_ref, x_ref, xp_ref, xn_ref, g_ref, sc_ref, sh_ref, w_ref, mup_ref, mun_ref, rwp_ref,
                  w2_ref, a2_ref, g2_ref, e_ref, et_ref,
                  z_ref, xbc_ref, dt_ref, r_ref, v_ref, an_ref, lw_ref, kd_ref, bv_ref, gate_ref, bonus_ref):
    i = pl.program_id(0)
    f32, bf16 = jnp.float32, jnp.bfloat16
    g, sc, sh = g_ref[...], sc_ref[0], sh_ref[0]
    hn = _adaln(x_ref[...], g, sc, sh).astype(bf16)
    halo = _adaln(jnp.concatenate([xp_ref[...], xn_ref[...]], axis=0), g, sc, sh).astype(bf16)
    hn_halo = jnp.concatenate([hn, halo], axis=0)
    z_ref[...] = jnp.dot(hn, w_ref[:, L0_Z[0]:L0_Z[1]], preferred_element_type=f32).astype(z_ref.dtype)
    xbc_ref[...] = jnp.dot(hn, w_ref[:, L0_XBC[0]:L0_XBC[1]], preferred_element_type=f32)
    dt_ref[...] = jnp.dot(hn, w_ref[:, L0_DT[0]:L0_DT[1]], preferred_element_type=f32)

    keep_prev = (1 - tbl_ref[0, i]).astype(f32)
    keep_next = (1 - tbl_ref[1, i]).astype(f32)
    row = lax.broadcasted_iota(jnp.int32, (BLK, 1), 0)

    def shifted(c0, c1):
        a, b = L0_SHIFT[0] + c0, L0_SHIFT[0] + c1
        both = jnp.dot(hn_halo, w_ref[:, a:b], preferred_element_type=f32)
        cur = both[:BLK]
        prev = jnp.where(row == 0, both[BLK + 7:BLK + 8] * keep_prev, pltpu.roll(cur, 1, 0))
        nxt = jnp.where(row == BLK - 1, both[BLK + 8:BLK + 9] * keep_next, pltpu.roll(cur, BLK - 1, 0))
        return cur + mup_ref[:, c0:c1] * (prev - cur) + mun_ref[:, c0:c1] * (nxt - cur)

    c = RWKV_DIM
    r = shifted(0, c)
    k = shifted(c, 2 * c)
    v = shifted(2 * c, 3 * c)
    wl = shifted(3 * c, 3 * c + 2 * W_LORA)
    ag = shifted(3 * c + 2 * W_LORA, 3 * c + 2 * W_LORA + 256)
    k_k, k_a, r_k = rwp_ref[0:1], rwp_ref[1:2], rwp_ref[2:3]
    gate_ref[...] = jnp.dot(jax.nn.sigmoid(ag).astype(bf16), g2_ref[...],
                            preferred_element_type=f32).astype(gate_ref.dtype)
    a_lora = jnp.dot(ag.astype(bf16), a2_ref[...], preferred_element_type=f32)
    w_lin = jnp.dot(jnp.tanh(wl).astype(bf16), w2_ref[...], preferred_element_type=f32)
    kk = k * k_k
    kk = kk * lax.rsqrt(_head_sum(kk * kk, e_ref, et_ref) + 1e-12)
    r_ref[...] = r.astype(r_ref.dtype)
    v_ref[...] = v.astype(v_ref.dtype)
    an_ref[...] = (-kk).astype(an_ref.dtype)
    kd_sum = None
    for d in range(2):
        w_log = -_softplus(-(rwp_ref[3 + d:4 + d] + w_lin[:, d * c:(d + 1) * c])) - 0.5
        lw_ref[d] = -jnp.exp(w_log)
        a = jax.nn.sigmoid(rwp_ref[5 + d:6 + d] + a_lora)
        kd = k * (1.0 + (a - 1.0) * k_a)
        kd_ref[d] = kd.astype(kd_ref.dtype)
        bv_ref[d] = (kk * a).astype(bv_ref.dtype)
        kd_sum = kd if kd_sum is None else kd_sum + kd
    bonus_ref[...] = (_head_sum(r * kd_sum * r_k, e_ref, et_ref) * v).astype(bonus_ref.dtype)


def _seq_tables(n_ctx, n_lat, l_lat):
    nb = l_lat // BLK
    cond = [0] * n_ctx + [1 + s for s in range(n_lat) for _ in range(nb)]
    first = [1] * n_ctx + [1 if j == 0 else 0 for _ in range(n_lat) for j in range(nb)]
    last = [1] * n_ctx + [1 if j == nb - 1 else 0 for _ in range(n_lat) for j in range(nb)]
    return cond, first, last


def l0_in(x, g1, scb, shb, w_packed, mup, mun, rwp, w2bd, a2p, g2p, e_ind, et_ind, first, last):
    R, D = x.shape
    nblk = R // BLK
    n8 = R // 8
    tbl = jnp.asarray([first, last], jnp.int32)
    c = RWKV_DIM
    row = lambda i, t: (i, 0)
    full = lambda shape: pl.BlockSpec(shape, lambda i, t: (0,) * len(shape))
    rows = lambda n: pl.BlockSpec((BLK, n), row)
    rows2 = lambda n: pl.BlockSpec((2, BLK, n), lambda i, t: (0, i, 0))
    f32, act = jnp.float32, ACT_DTYPE
    sds = jax.ShapeDtypeStruct
    return pl.pallas_call(
        _l0_in_kernel,
        grid_spec=pltpu.PrefetchScalarGridSpec(
            num_scalar_prefetch=1,
            grid=(nblk,),
            in_specs=[rows(D),
                      pl.BlockSpec((8, D), lambda i, t: (jnp.maximum(i * (BLK // 8) - 1, 0), 0)),
                      pl.BlockSpec((8, D), lambda i, t: (jnp.minimum((i + 1) * (BLK // 8), n8 - 1), 0)),
                      full((1, D)),
                      pl.BlockSpec((1, 1, D), lambda i, t: (i, 0, 0)),
                      pl.BlockSpec((1, 1, D), lambda i, t: (i, 0, 0)),
                      full((D, L0_COLS)), full(mup.shape), full(mun.shape), full(rwp.shape),
                      full(w2bd.shape), full(a2p.shape), full(g2p.shape), full(e_ind.shape), full(et_ind.shape)],
            out_specs=[rows(c), rows(SSD_XBC), rows(128), rows(c), rows(c), rows(c),
                       rows2(c), rows2(c), rows2(c), rows(c), rows(c)]),
        out_shape=[sds((R, c), act), sds((R, SSD_XBC), f32), sds((R, 128), f32), sds((R, c), act), sds((R, c), act),
                   sds((R, c), act), sds((2, R, c), f32), sds((2, R, c), act), sds((2, R, c), act),
                   sds((R, c), act), sds((R, c), act)],
        compiler_params=pltpu.CompilerParams(dimension_semantics=("arbitrary",), vmem_limit_bytes=L0_VMEM_BYTES),
        name="l0_in",
    )(tbl, x, x, x, g1, scb, shb, w_packed, mup, mun, rwp, w2bd, a2p, g2p, e_ind, et_ind)


def l0_pack_weights(p, e):
    bf16 = jnp.bfloat16
    w = p['ab_w_in'][e]
    D = w.shape[0]
    c = RWKV_DIM
    rw0 = SSD_IN
    ag0 = rw0 + 3 * c + 2 * W_LORA
    w_packed = jnp.concatenate([
        w[:, :SSD_INNER + SSD_XBC], w[:, rw0:ag0], w[:, ag0:ag0 + A_LORA + G_LORA],
        jnp.zeros((D, 256 - A_LORA - G_LORA), w.dtype),
        w[:, SSD_INNER + SSD_XBC:SSD_IN], jnp.zeros((D, 128 - SSD_HEADS), w.dtype)], axis=1).astype(bf16)

    def pack_mu(mu):
        return jnp.concatenate([mu, jnp.zeros((256 - A_LORA - G_LORA,), mu.dtype)])[None, :]

    rwp = jnp.stack([p['rwkv_k_k'][e], p['rwkv_k_a'][e], p['rwkv_r_k'][e].reshape(-1), p['rwkv_w0'][e, 0],
                     p['rwkv_w0'][e, 1], p['rwkv_a0'][e, 0], p['rwkv_a0'][e, 1], jnp.zeros((c,), jnp.float32)])
    zw = jnp.zeros((W_LORA, c), jnp.float32)
    w2bd = jnp.concatenate([jnp.concatenate([p['rwkv_w2'][e, 0], zw], axis=1),
                            jnp.concatenate([zw, p['rwkv_w2'][e, 1]], axis=1)], axis=0).astype(bf16)
    a2p = jnp.concatenate([p['rwkv_a2'][e], jnp.zeros((256 - A_LORA, c), jnp.float32)], axis=0).astype(bf16)
    g2p = jnp.concatenate([jnp.zeros((A_LORA, c), jnp.float32), p['rwkv_g2'][e],
                           jnp.zeros((256 - A_LORA - G_LORA, c), jnp.float32)], axis=0).astype(bf16)
    head = jnp.arange(c) // RWKV_N
    e_ind = (head[:, None] == jnp.arange(128)[None, :]).astype(bf16)
    return w_packed, pack_mu(p['rwkv_mu_prev'][e]), pack_mu(p['rwkv_mu_next'][e]), rwp, w2bd, a2p, g2p, e_ind, e_ind.T


SSD_QH = 4
SSD_VMEM_BYTES = 48 * 1024 * 1024
NEG_BIG = -1e30
LOG2E = 1.4426950408889634


def _conv_silu(cur, prev_row, next_row, w_ref, b_ref):
    row = lax.broadcasted_iota(jnp.int32, (BLK, 1), 0)
    prev = jnp.where(row == 0, prev_row, pltpu.roll(cur, 1, 0))
    nxt = jnp.where(row == BLK - 1, next_row, pltpu.roll(cur, BLK - 1, 0))
    y = w_ref[0:1] * prev + w_ref[1:2] * cur + w_ref[2:3] * nxt + b_ref[...]
    return y * jax.nn.sigmoid(y)


def _ssd_kernel(tbl_ref, xs_ref, b_ref, c_ref, dt_ref, cwx_ref, cwb_ref, cwc_ref, cbx_ref, cbb_ref, cbc_ref,
                sel_ref, hp_ref, s0_ref, y_ref, fs_ref, xa_ref, ba_ref, ca_ref, sfx_ref, ldb_ref, st_ref):
    f32, bf16 = jnp.float32, jnp.bfloat16
    sb = pl.program_id(0)
    sbr, qw = xs_ref.shape
    nch = sbr // BLK
    P, QH = SSD_P, SSD_QH
    t_i = lax.broadcasted_iota(jnp.int32, (BLK, BLK), 0)
    s_i = lax.broadcasted_iota(jnp.int32, (BLK, BLK), 1)
    lower = s_i <= t_i
    upper = s_i >= t_i
    tri_lo = jnp.where(lower, 1.0, 0.0).astype(bf16)
    tri_up = jnp.where(upper, 1.0, 0.0).astype(bf16)
    hp = hp_ref[0]
    sel = sel_ref[0]
    ind = jnp.where(lax.broadcasted_iota(jnp.int32, (128, qw), 1) // P == lax.broadcasted_iota(jnp.int32, (128, qw), 0),
                    1.0, 0.0).astype(bf16)
    head_of_lane = lax.broadcasted_iota(jnp.int32, (1, qw), 1) // P

    def expand(cols):
        return _sum_split(cols, ind, 3)

    d_row = expand(jnp.broadcast_to(hp[4:5], (8, 128)))[0:1]

    def chunk_rows(c):
        return pl.ds(pl.multiple_of(c * BLK, BLK), BLK)

    def neighbours(ref, c, keep_prev, keep_next):
        lo = jnp.maximum(c * BLK - 1, 0)
        hi = jnp.minimum((c + 1) * BLK, sbr - 1)
        return ref[pl.ds(lo, 1), :] * keep_prev, ref[pl.ds(hi, 1), :] * keep_next

    def fwd(c, carry):
        blk = sb * nch + c
        first, last = tbl_ref[0, blk], tbl_ref[1, blk]
        kp, kn = (1 - first).astype(f32), (1 - last).astype(f32)
        rows = chunk_rows(c)

        @pl.when(first == 1)
        def _():
            st_ref[0] = s0_ref[0, 0]

        xa = _conv_silu(xs_ref[rows, :], *neighbours(xs_ref, c, kp, kn), cwx_ref, cbx_ref)
        bm = _conv_silu(b_ref[rows, :], *neighbours(b_ref, c, kp, kn), cwb_ref, cbb_ref)
        cm = _conv_silu(c_ref[rows, :], *neighbours(c_ref, c, kp, kn), cwc_ref, cbc_ref)
        xb, bmb, cmb = xa.astype(bf16), bm.astype(bf16), cm.astype(bf16)
        xa_ref[rows, :] = xb
        ba_ref[rows, :] = bmb
        ca_ref[rows, :] = cmb
        dtq = _sum_split(dt_ref[rows, :], sel, 3)
        dtf = _softplus(dtq + hp[0:1])
        dtb = _softplus(dtq + hp[1:2])
        acs = _sum_split_left(tri_lo, dtf * hp[2:3], 3)
        sfx = _sum_split_left(tri_up, dtb * hp[3:4], 3)
        ldf, ldb = jnp.log(dtf), jnp.log(dtb)
        sfx_ref[rows, :] = sfx
        ldb_ref[rows, :] = ldb
        a2, s2 = acs * LOG2E, sfx * LOG2E
        a2r = (a2 - ldf * LOG2E).T
        s2r = (s2 - ldb * LOG2E).T
        g = lax.dot_general(cmb, bmb, (((1,), (1,)), ((), ())), preferred_element_type=f32)
        y_diag = None
        for j in range(QH):
            m = (g * (jnp.exp2(jnp.where(lower, a2[:, j:j + 1] - a2r[j:j + 1, :], NEG_BIG))
                      + jnp.exp2(jnp.where(upper, s2[:, j:j + 1] - s2r[j:j + 1, :], NEG_BIG)))).astype(bf16)
            xh = jnp.where(head_of_lane == j, xb, jnp.zeros((), bf16))
            t = jnp.dot(m, xh, preferred_element_type=f32)
            y_diag = t if y_diag is None else y_diag + t
        ea = jnp.exp(expand(acs))
        wf = jnp.exp(expand(acs[BLK - 1:BLK] - acs + ldf))
        s_in = st_ref[0]
        y_ref[rows, :] = xa * d_row + y_diag + ea * jnp.dot(cmb, s_in.astype(bf16), preferred_element_type=f32)
        st_ref[0] = ea[BLK - 1:BLK] * s_in + lax.dot_general(
            bmb, (xa * wf).astype(bf16), (((0,), (0,)), ((), ())), preferred_element_type=f32)
        fs_ref[c, 0] = st_ref[0]
        return carry

    lax.fori_loop(0, nch, fwd, 0)

    def bwd(k, carry):
        c = nch - 1 - k
        blk = sb * nch + c
        rows = chunk_rows(c)

        @pl.when(tbl_ref[1, blk] == 1)
        def _():
            st_ref[1] = s0_ref[0, 1]

        sfx = sfx_ref[rows, :]
        eb = jnp.exp(expand(sfx))
        wb = jnp.exp(expand(sfx[0:1] - sfx + ldb_ref[rows, :]))
        s_in = st_ref[1]
        y_ref[rows, :] += eb * jnp.dot(ca_ref[rows, :], s_in.astype(bf16), preferred_element_type=f32)
        st_ref[1] = eb[0:1] * s_in + lax.dot_general(
            ba_ref[rows, :], (xa_ref[rows, :].astype(f32) * wb).astype(bf16), (((0,), (0,)), ((), ())),
            preferred_element_type=f32)
        fs_ref[c, 1] = st_ref[1]
        return carry

    lax.fori_loop(0, nch, bwd, 0)


def ssd_scan(xbc, dt, conv_w, conv_b, sel, hp, s0, first, last, sb_rows):
    R = xbc.shape[0]
    n_sb = R // sb_rows
    nch = sb_rows // BLK
    nq = SSD_HEADS // SSD_QH
    qw = SSD_QH * SSD_P
    qpg = SSD_HPG // SSD_QH
    b_blk = SSD_INNER // SSD_N
    c_blk = b_blk + SSD_GROUPS
    tbl = jnp.asarray([first, last], jnp.int32)
    cw = conv_w
    cb = conv_b.reshape(1, -1)
    f32, bf16 = jnp.float32, jnp.bfloat16
    return pl.pallas_call(
        _ssd_kernel,
        grid_spec=pltpu.PrefetchScalarGridSpec(
            num_scalar_prefetch=1,
            grid=(n_sb, nq),
            in_specs=[pl.BlockSpec((sb_rows, qw), lambda s, q, t: (s, q)),
                      pl.BlockSpec((sb_rows, SSD_N), lambda s, q, t: (s, b_blk + q // qpg)),
                      pl.BlockSpec((sb_rows, SSD_N), lambda s, q, t: (s, c_blk + q // qpg)),
                      pl.BlockSpec((sb_rows, 128), lambda s, q, t: (s, 0)),
                      pl.BlockSpec((3, qw), lambda s, q, t: (0, q)),
                      pl.BlockSpec((3, SSD_N), lambda s, q, t: (0, b_blk + q // qpg)),
                      pl.BlockSpec((3, SSD_N), lambda s, q, t: (0, c_blk + q // qpg)),
                      pl.BlockSpec((1, qw), lambda s, q, t: (0, q)),
                      pl.BlockSpec((1, SSD_N), lambda s, q, t: (0, b_blk + q // qpg)),
                      pl.BlockSpec((1, SSD_N), lambda s, q, t: (0, c_blk + q // qpg)),
                      pl.BlockSpec((1, 128, 128), lambda s, q, t: (q, 0, 0)),
                      pl.BlockSpec((1, 8, 128), lambda s, q, t: (q, 0, 0)),
                      pl.BlockSpec((1, 2, SSD_N, qw), lambda s, q, t: (s, 0, 0, q))],
            out_specs=[pl.BlockSpec((sb_rows, qw), lambda s, q, t: (s, q)),
                       pl.BlockSpec((nch, 2, SSD_N, qw), lambda s, q, t: (s, 0, 0, q))],
            scratch_shapes=[pltpu.VMEM((sb_rows, qw), bf16), pltpu.VMEM((sb_rows, SSD_N), bf16),
                            pltpu.VMEM((sb_rows, SSD_N), bf16), pltpu.VMEM((sb_rows, 128), f32),
                            pltpu.VMEM((sb_rows, 128), f32), pltpu.VMEM((2, SSD_N, qw), f32)]),
        out_shape=[jax.ShapeDtypeStruct((R, SSD_INNER), f32),
                   jax.ShapeDtypeStruct((R // BLK, 2, SSD_N, SSD_INNER), f32)],
        compiler_params=pltpu.CompilerParams(dimension_semantics=("arbitrary", "arbitrary"),
                                             vmem_limit_bytes=SSD_VMEM_BYTES),
        name="ssd_scan",
    )(tbl, xbc, xbc, xbc, dt, cw, cw, cw, cb, cb, cb, sel, hp, s0)


def ssd_tables(p, e):
    nq = SSD_HEADS // SSD_QH
    lane = jnp.arange(128)
    sel = jnp.stack([(lane[:, None] == (q * SSD_QH + lane[None, :])) & (lane[None, :] < SSD_QH)
                     for q in range(nq)]).astype(jnp.bfloat16)
    a_neg = -jnp.exp(p['ssd_a_log'][e].astype(jnp.float32))
    rows = jnp.stack([p['ssd_dt_bias'][e, 0], p['ssd_dt_bias'][e, 1], a_neg[0], a_neg[1], p['ssd_d'][e]])
    hp = jnp.zeros((nq, 8, 128), jnp.float32)
    hp = hp.at[:, :5, :SSD_QH].set(jnp.transpose(rows.reshape(5, nq, SSD_QH), (1, 0, 2)))
    return sel, hp


MIX_VMEM_BYTES = 40 * 1024 * 1024
ROUTER_LANES = 128


def _residual_norm_router(x, out, g1, n2g, sc2, sh2, rw_ref, x_out_ref, hn_ref, aff_ref):
    x_new = x + g1 * out
    x_out_ref[...] = x_new
    hn = _adaln(x_new, n2g, sc2, sh2)
    hn_ref[...] = hn.astype(hn_ref.dtype)
    logits = _dot(hn, rw_ref[...], ((1,), (0,)), 3)
    lane = lax.broadcasted_iota(jnp.int32, logits.shape, 1)
    logits = jnp.where(lane < N_EXPERTS, logits, NEG_BIG)
    ex = jnp.exp(logits - jnp.max(logits, axis=-1, keepdims=True))
    aff = ex / jnp.sum(ex, axis=-1, keepdims=True)
    aff_ref[...] = aff.T[:N_EXPERTS]


def _l0_out_kernel(ys_ref, z_ref, yf_ref, yb_ref, bonus_ref, gate_ref, sg_ref, lnw_ref, lnb_ref, e_ref, et_ref, w_ref,
                   x_ref, g1_ref, n2g_ref, sc2_ref, sh2_ref, rw_ref, x_out_ref, hn_ref, aff_ref):
    f32, bf16 = jnp.float32, jnp.bfloat16
    z = z_ref[...].astype(f32)
    ys = ys_ref[...] * (z * jax.nn.sigmoid(z))
    gw = SSD_INNER // SSD_GROUPS
    parts = []
    for gi in range(SSD_GROUPS):
        yg = ys[:, gi * gw:(gi + 1) * gw]
        parts.append(yg * lax.rsqrt(jnp.mean(yg * yg, -1, keepdims=True) + NORM_EPS))
    a1 = jnp.concatenate(parts, axis=1) * sg_ref[...]
    o = yf_ref[...].astype(f32) + yb_ref[...].astype(f32)
    mu = _head_sum(o, e_ref, et_ref) * (1.0 / RWKV_N)
    oc = o - mu
    var = _head_sum(oc * oc, e_ref, et_ref) * (1.0 / RWKV_N)
    o = oc * lax.rsqrt(var + RWKV_GN_EPS) * lnw_ref[...] + lnb_ref[...]
    o = (o + bonus_ref[...].astype(f32)) * gate_ref[...].astype(f32)
    out = (jnp.dot(a1.astype(bf16), w_ref[:SSD_INNER], preferred_element_type=f32)
           + jnp.dot(o.astype(bf16), w_ref[SSD_INNER:], preferred_element_type=f32))
    _residual_norm_router(x_ref[...], out, g1_ref[0], n2g_ref[...], sc2_ref[0], sh2_ref[0], rw_ref,
                          x_out_ref, hn_ref, aff_ref)


def _l1_out_kernel(a_ref, w_ref, x_ref, g1_ref, n2g_ref, sc2_ref, sh2_ref, rw_ref, x_out_ref, hn_ref, aff_ref):
    out = jnp.dot(a_ref[...].astype(jnp.bfloat16), w_ref[...], preferred_element_type=jnp.float32)
    _residual_norm_router(x_ref[...], out, g1_ref[0], n2g_ref[...], sc2_ref[0], sh2_ref[0], rw_ref,
                          x_out_ref, hn_ref, aff_ref)


def _mix_out_call(kernel_fn, name, lead_args, lead_specs, w_out, x, g1b, n2g, sc2b, sh2b, router_w):
    R, D = x.shape
    nblk = R // BLK
    full = lambda a: pl.BlockSpec(a.shape, lambda i: (0,) * a.ndim)
    blkrow = pl.BlockSpec((1, 1, D), lambda i: (i, 0, 0))
    rw = jnp.zeros((D, ROUTER_LANES), jnp.float32).at[:, :N_EXPERTS].set(router_w)
    args = list(lead_args) + [w_out, x, g1b, n2g, sc2b, sh2b, rw]
    x_idx = len(lead_args) + 1
    in_specs = list(lead_specs) + [full(w_out), pl.BlockSpec((BLK, D), lambda i: (i, 0)), blkrow, full(n2g), blkrow,
                                   blkrow, full(rw)]
    return pl.pallas_call(
        kernel_fn,
        grid=(nblk,),
        in_specs=in_specs,
        out_specs=[pl.BlockSpec((BLK, D), lambda i: (i, 0)), pl.BlockSpec((BLK, D), lambda i: (i, 0)),
                   pl.BlockSpec((N_EXPERTS, BLK), lambda i: (0, i))],
        out_shape=[jax.ShapeDtypeStruct((R, D), jnp.float32), jax.ShapeDtypeStruct((R, D), jnp.bfloat16),
                   jax.ShapeDtypeStruct((N_EXPERTS, R), jnp.float32)],
        input_output_aliases={x_idx: 0},
        compiler_params=pltpu.CompilerParams(dimension_semantics=("arbitrary",), vmem_limit_bytes=MIX_VMEM_BYTES),
        name=name,
    )(*args)


def l0_out(ys, z, yf, yb, bonus, gate, ssd_g, ln_w, ln_b, e_ind, et_ind, w_out, x, g1b, n2g, sc2b, sh2b, router_w):
    c = RWKV_DIM
    rows = lambda n: pl.BlockSpec((BLK, n), lambda i: (i, 0))
    full = lambda a: pl.BlockSpec(a.shape, lambda i: (0,) * a.ndim)
    lead = [ys, z, yf, yb, bonus, gate, ssd_g, ln_w, ln_b, e_ind, et_ind]
    specs = [rows(SSD_INNER), rows(SSD_INNER), rows(c), rows(c), rows(c), rows(c),
             full(ssd_g), full(ln_w), full(ln_b), full(e_ind), full(et_ind)]
    return _mix_out_call(_l0_out_kernel, "l0_out", lead, specs, w_out, x, g1b, n2g, sc2b, sh2b, router_w)


def l1_out(a, w_out, x, g1b, n2g, sc2b, sh2b, router_w):
    specs = [pl.BlockSpec((BLK, a.shape[1]), lambda i: (i, 0))]
    return _mix_out_call(_l1_out_kernel, "l1_out", [a], specs, w_out, x, g1b, n2g, sc2b, sh2b, router_w)


RET_HPS = 2


def _l1_in_kernel(tbl_ref, x_ref, g_ref, sc_ref, sh_ref, w_ref, cos_ref, sin_ref, q_ref, k_ref, v_ref, gg_ref):
    f32 = jnp.float32
    hn = _adaln(x_ref[...], g_ref[...], sc_ref[0], sh_ref[0]).astype(jnp.bfloat16)
    cosf, sinf = cos_ref[0], sin_ref[0]

    def rope(x):
        parts = []
        for h in range(RET_HEADS):
            xh = x[:, h * RET_DK:(h + 1) * RET_DK]
            parts.append(xh * cosf + pltpu.roll(xh, RET_DK // 2, 1) * sinf)
        return jnp.concatenate(parts, axis=1)

    q_ref[...] = rope(jnp.dot(hn, w_ref[:, :RET_QK], preferred_element_type=f32)).astype(q_ref.dtype)
    k_ref[...] = (rope(jnp.dot(hn, w_ref[:, RET_QK:2 * RET_QK], preferred_element_type=f32))
                  * (RET_DK ** -0.5)).astype(k_ref.dtype)
    v_ref[...] = jnp.dot(hn, w_ref[:, 2 * RET_QK:2 * RET_QK + RET_V], preferred_element_type=f32).astype(v_ref.dtype)
    gg_ref[...] = jnp.dot(hn, w_ref[:, 2 * RET_QK + RET_V:], preferred_element_type=f32).astype(gg_ref.dtype)


def l1_in(x, g1, scb, shb, w_bf16, cos_t, sin_t, rope_blk):
    R, D = x.shape
    nblk = R // BLK
    tbl = jnp.asarray([rope_blk], jnp.int32)
    f32 = jnp.float32
    row = lambda n: pl.BlockSpec((BLK, n), lambda i, t: (i, 0))
    full = lambda a: pl.BlockSpec(a.shape, lambda i, t: (0,) * a.ndim)
    blkrow = pl.BlockSpec((1, 1, D), lambda i, t: (i, 0, 0))
    ropespec = pl.BlockSpec((1, BLK, RET_DK), lambda i, t: (t[0, i], 0, 0))
    return pl.pallas_call(
        _l1_in_kernel,
        grid_spec=pltpu.PrefetchScalarGridSpec(
            num_scalar_prefetch=1, grid=(nblk,),
            in_specs=[row(D), full(g1), blkrow, blkrow, full(w_bf16), ropespec, ropespec],
            out_specs=[row(RET_QK), row(RET_QK), row(RET_V), row(RET_V)]),
        out_shape=[jax.ShapeDtypeStruct((R, RET_QK), ACT_DTYPE), jax.ShapeDtypeStruct((R, RET_QK), ACT_DTYPE),
                   jax.ShapeDtypeStruct((R, RET_V), ACT_DTYPE), jax.ShapeDtypeStruct((R, RET_V), ACT_DTYPE)],
        compiler_params=pltpu.CompilerParams(dimension_semantics=("arbitrary",), vmem_limit_bytes=L0_VMEM_BYTES),
        name="l1_in",
    )(tbl, x, g1, scb, shb, w_bf16, cos_t, sin_t)


def _ret_kernel(tbl_ref, q_ref, k_ref, v_ref, g_ref, lg_ref, nw_ref, nb_ref, s0_ref, a_ref, fs_ref, st_ref):
    f32, bf16 = jnp.float32, jnp.bfloat16
    sb = pl.program_id(0)
    nch = q_ref.shape[0] // BLK
    heads = range(RET_HPS)
    ks = [slice(h * RET_DK, (h + 1) * RET_DK) for h in heads]
    vs = [slice(h * RET_DV, (h + 1) * RET_DV) for h in heads]
    lgf = [lg_ref[h, 0:1, 0:1] for h in heads]
    lgb = [lg_ref[h, 1:2, 0:1] for h in heads]
    t_i = lax.broadcasted_iota(jnp.int32, (BLK, BLK), 0)
    s_i = lax.broadcasted_iota(jnp.int32, (BLK, BLK), 1)
    dist = (t_i - s_i).astype(f32)
    dm = [jnp.exp(jnp.where(s_i <= t_i, dist * lgf[h], NEG_BIG)) + jnp.exp(jnp.where(s_i >= t_i, -dist * lgb[h], NEG_BIG))
          for h in heads]
    tcol = lax.broadcasted_iota(jnp.int32, (BLK, 1), 0).astype(f32)
    nt, tn = (((1,), (1,)), ((), ())), (((0,), (0,)), ((), ()))

    def chunk_rows(c):
        return pl.ds(pl.multiple_of(c * BLK, BLK), BLK)

    def fwd(c, carry):
        blk = sb * nch + c
        rows = chunk_rows(c)

        @pl.when(tbl_ref[0, blk] == 1)
        def _():
            st_ref[0] = s0_ref[0, 0]

        q = [q_ref[rows, ks[h]].astype(bf16) for h in heads]
        k = [k_ref[rows, ks[h]] for h in heads]
        v = [v_ref[rows, vs[h]].astype(bf16) for h in heads]
        s_in = [st_ref[0, h] for h in heads]
        g = [lax.dot_general(q[h], k[h].astype(bf16), nt, preferred_element_type=f32) for h in heads]
        y_diag = [jnp.dot((g[h] * dm[h]).astype(bf16), v[h], preferred_element_type=f32) for h in heads]
        y_off = [jnp.dot(q[h], s_in[h].astype(bf16), preferred_element_type=f32) for h in heads]
        kw = [(k[h].astype(f32) * jnp.exp((BLK - 1.0 - tcol) * lgf[h])).astype(bf16) for h in heads]
        upd = [lax.dot_general(kw[h], v[h], tn, preferred_element_type=f32) for h in heads]
        for h in heads:
            a_ref[rows, vs[h]] = y_diag[h] + jnp.exp((tcol + 1.0) * lgf[h]) * y_off[h]
            st_ref[0, h] = jnp.exp(BLK * lgf[h]) * s_in[h] + upd[h]
        fs_ref[c, 0] = st_ref[0]
        return carry

    lax.fori_loop(0, nch, fwd, 0)

    def bwd(j, carry):
        c = nch - 1 - j
        blk = sb * nch + c
        rows = chunk_rows(c)

        @pl.when(tbl_ref[1, blk] == 1)
        def _():
            st_ref[1] = s0_ref[0, 1]

        q = [q_ref[rows, ks[h]].astype(bf16) for h in heads]
        v = [v_ref[rows, vs[h]].astype(bf16) for h in heads]
        s_in = [st_ref[1, h] for h in heads]
        y_off = [jnp.dot(q[h], s_in[h].astype(bf16), preferred_element_type=f32) for h in heads]
        kw = [(k_ref[rows, ks[h]].astype(f32) * jnp.exp(tcol * lgb[h])).astype(bf16) for h in heads]
        upd = [lax.dot_general(kw[h], v[h], tn, preferred_element_type=f32) for h in heads]
        for h in heads:
            st_ref[1, h] = jnp.exp(BLK * lgb[h]) * s_in[h] + upd[h]
            y = a_ref[rows, vs[h]] + jnp.exp((BLK - tcol) * lgb[h]) * y_off[h]
            mu = jnp.mean(y, -1, keepdims=True)
            yc = y - mu
            var = jnp.mean(yc * yc, -1, keepdims=True)
            gg = g_ref[rows, vs[h]].astype(f32)
            a_ref[rows, vs[h]] = ((yc * lax.rsqrt(var + 1e-5) * nw_ref[:, vs[h]] + nb_ref[:, vs[h]])
                                  * (gg * jax.nn.sigmoid(gg)))
        fs_ref[c, 1] = st_ref[1]
        return carry

    lax.fori_loop(0, nch, bwd, 0)


def ret_scan(q, k, v, g, lg_tab, norm_w, norm_b, s0, first, last, sb_rows):
    R = q.shape[0]
    n_sb = R // sb_rows
    nch = sb_rows // BLK
    tbl = jnp.asarray([first, last], jnp.int32)
    f32 = jnp.float32
    hps = RET_HPS
    return pl.pallas_call(
        _ret_kernel,
        grid_spec=pltpu.PrefetchScalarGridSpec(
            num_scalar_prefetch=1, grid=(n_sb, RET_HEADS // hps),
            in_specs=[pl.BlockSpec((sb_rows, hps * RET_DK), lambda s, h, t: (s, h)),
                      pl.BlockSpec((sb_rows, hps * RET_DK), lambda s, h, t: (s, h)),
                      pl.BlockSpec((sb_rows, hps * RET_DV), lambda s, h, t: (s, h)),
                      pl.BlockSpec((sb_rows, hps * RET_DV), lambda s, h, t: (s, h)),
                      pl.BlockSpec((hps, 8, 128), lambda s, h, t: (h, 0, 0)),
                      pl.BlockSpec((1, hps * RET_DV), lambda s, h, t: (0, h)),
                      pl.BlockSpec((1, hps * RET_DV), lambda s, h, t: (0, h)),
                      pl.BlockSpec((1, 2, hps, RET_DK, RET_DV), lambda s, h, t: (s, 0, h, 0, 0))],
            out_specs=[pl.BlockSpec((sb_rows, hps * RET_DV), lambda s, h, t: (s, h)),
                       pl.BlockSpec((nch, 2, hps, RET_DK, RET_DV), lambda s, h, t: (s, 0, h, 0, 0))],
            scratch_shapes=[pltpu.VMEM((2, hps, RET_DK, RET_DV), f32)]),
        out_shape=[jax.ShapeDtypeStruct((R, RET_V), f32),
                   jax.ShapeDtypeStruct((R // BLK, 2, RET_HEADS, RET_DK, RET_DV), f32)],
        compiler_params=pltpu.CompilerParams(dimension_semantics=("arbitrary", "arbitrary"),
                                             vmem_limit_bytes=SSD_VMEM_BYTES),
        name="ret_scan",
    )(tbl, q, k, v, g, lg_tab, norm_w.reshape(1, -1), norm_b.reshape(1, -1), s0)


def _mod_kernel(c_ref, w_ref, b_ref, o_ref):
    c = c_ref[...]
    act = c * jax.nn.sigmoid(c)
    o_ref[0] = _dot(act, w_ref[0], ((1,), (0,)), 3) + b_ref[0]


def mod_vectors(conds, mod_w, mod_b):
    depth, D, n6 = mod_w.shape
    tn = D
    return pl.pallas_call(
        _mod_kernel,
        grid=(depth, n6 // tn),
        in_specs=[pl.BlockSpec(conds.shape, lambda i, j: (0, 0)),
                  pl.BlockSpec((1, D, tn), lambda i, j: (i, 0, j)),
                  pl.BlockSpec((1, 1, tn), lambda i, j: (i, 0, j))],
        out_specs=pl.BlockSpec((1, conds.shape[0], tn), lambda i, j: (i, 0, j)),
        out_shape=jax.ShapeDtypeStruct((depth, conds.shape[0], n6), jnp.float32),
        name="mod_vectors",
    )(conds, mod_w, mod_b.reshape(depth, 1, n6))


def rope_tables(n_tokens):
    rows = n_tokens // GRID_W
    row = jnp.repeat(jnp.arange(rows), GRID_W).astype(jnp.float32)
    col = jnp.tile(jnp.arange(GRID_W), rows).astype(jnp.float32)
    n_f = RET_DK // 4
    inv = ROPE_BASE ** (-jnp.arange(n_f, dtype=jnp.float32) / n_f)
    ang = jnp.concatenate([row[:, None] * inv, col[:, None] * inv], -1)
    return jnp.cos(ang), jnp.sin(ang)


def _rope_block_tables(n_ctx, n_lat, l_lat):
    nb = l_lat // BLK
    cos, sin = rope_tables(l_lat)
    cosf = jnp.concatenate([cos, cos], -1).reshape(nb, BLK, RET_DK)
    sinf = jnp.concatenate([-sin, sin], -1).reshape(nb, BLK, RET_DK)
    cos_t = jnp.concatenate([jnp.ones((1, BLK, RET_DK), jnp.float32), cosf])
    sin_t = jnp.concatenate([jnp.zeros((1, BLK, RET_DK), jnp.float32), sinf])
    rope_blk = [0] * n_ctx + [1 + j for _ in range(n_lat) for j in range(nb)]
    return cos_t, sin_t, rope_blk


def kernel(x_prompt, x_sample, state_ssd, state_rwkv, state_ret, c, c_ctx, mod_w, mod_b, norm1_g, norm2_g,
           router_w, exp_w_gate, exp_w_up, exp_w_down, ab_w_in, ab_w_out, ssd_conv_w, ssd_conv_b, ssd_dt_bias,
           ssd_a_log, ssd_d, ssd_norm_g, rwkv_mu_prev, rwkv_mu_next, rwkv_w0, rwkv_w2, rwkv_a0, rwkv_a2, rwkv_g2,
           rwkv_k_k, rwkv_k_a, rwkv_r_k, rwkv_ln_w, rwkv_ln_b, ret_w_in, ret_w_out, ret_decay_logit, ret_norm_w,
           ret_norm_b, final_norm_g):
    p = dict(mod_w=mod_w, mod_b=mod_b, norm1_g=norm1_g, norm2_g=norm2_g, router_w=router_w,
             exp_w_gate=exp_w_gate, exp_w_up=exp_w_up, exp_w_down=exp_w_down, ab_w_in=ab_w_in, ab_w_out=ab_w_out,
             ssd_conv_w=ssd_conv_w, ssd_conv_b=ssd_conv_b, ssd_dt_bias=ssd_dt_bias, ssd_a_log=ssd_a_log,
             ssd_d=ssd_d, ssd_norm_g=ssd_norm_g, rwkv_mu_prev=rwkv_mu_prev, rwkv_mu_next=rwkv_mu_next,
             rwkv_w0=rwkv_w0, rwkv_w2=rwkv_w2, rwkv_a0=rwkv_a0, rwkv_a2=rwkv_a2, rwkv_g2=rwkv_g2,
             rwkv_k_k=rwkv_k_k, rwkv_k_a=rwkv_k_a, rwkv_r_k=rwkv_r_k, rwkv_ln_w=rwkv_ln_w, rwkv_ln_b=rwkv_ln_b,
             ret_w_in=ret_w_in, ret_w_out=ret_w_out, ret_decay_logit=ret_decay_logit, ret_norm_w=ret_norm_w,
             ret_norm_b=ret_norm_b, final_norm_g=final_norm_g)
    f32, bf16 = jnp.float32, jnp.bfloat16
    n_ctx, l_ctx, D = x_prompt.shape
    n_lat, l_lat, _ = x_sample.shape
    assert l_ctx == BLK and l_lat % BLK == 0 and (n_ctx * BLK) % l_lat == 0
    n_sb_ctx = n_ctx * BLK // l_lat
    cond_id, first, last = _seq_tables(n_ctx, n_lat, l_lat)
    x = jnp.concatenate([x_prompt.reshape(-1, D), x_sample.reshape(-1, D)])

    conds = jnp.concatenate([c_ctx[None, :], c, jnp.zeros((8 - 1 - n_lat, D), f32)])
    mods = mod_vectors(conds, mod_w, mod_b)[:, jnp.asarray(cond_id)]
    mods = mods.reshape(DEPTH, len(cond_id), 6, 1, D)

    def with_ctx_zeros(state, n_zero):
        return jnp.concatenate([jnp.zeros((n_zero,) + state.shape[1:], f32), state])

    new_ssd, new_rwkv, new_ret = [], [], []
    out = None
    for i in range(DEPTH):
        sh1, sc1, g1, sh2, sc2, g2 = (mods[i, :, k] for k in range(6))
        e = i // 2
        if i % 2 == 0:
            w_packed, mup, mun, rwp, w2bd, a2p, g2p, e_ind, et_ind = l0_pack_weights(p, e)
            z, xbc, dt, r, v, an, lw, kd, bv, gate, bonus = l0_in(
                x, norm1_g[i][None], sc1, sh1, w_packed, mup, mun, rwp, w2bd, a2p, g2p, e_ind, et_ind, first, last)
            sel, hp = ssd_tables(p, e)
            s0_ssd = jnp.transpose(state_ssd[:, e], (0, 1, 3, 2, 4)).reshape(n_lat, 2, SSD_N, SSD_INNER)
            ys, fs_ssd = ssd_scan(xbc, dt, ssd_conv_w[e], ssd_conv_b[e], sel, hp,
                                  with_ctx_zeros(s0_ssd, n_sb_ctx), first, last, l_lat)
            new_ssd.append(jnp.transpose(fs_ssd[:n_ctx].reshape(n_ctx, 2, SSD_N, SSD_HEADS, SSD_P), (0, 1, 3, 2, 4)))
            s0_rwkv = jnp.transpose(state_rwkv[:, e], (0, 1, 4, 2, 3)).reshape(n_lat, 2, RWKV_N, RWKV_DIM)
            yf, yb, sf_rwkv = rwkv_scan_pallas(r, v, an, lw, kd, bv, with_ctx_zeros(s0_rwkv, n_ctx),
                                               _rwkv_steps(n_ctx, n_lat, l_lat))
            new_rwkv.append(jnp.transpose(sf_rwkv[:n_ctx].reshape(n_ctx, 2, RWKV_N, RWKV_HEADS, RWKV_N),
                                          (0, 1, 3, 4, 2)))
            x, hn2, affT = l0_out(ys, z, yf, yb, bonus, gate, ssd_norm_g[e][None], rwkv_ln_w[e][None], rwkv_ln_b[e][None],
                                  e_ind, et_ind, ab_w_out[e].astype(bf16), x, g1, norm2_g[i][None], sc2, sh2,
                                  router_w[i])
        else:
            cos_t, sin_t, rope_blk = _rope_block_tables(n_ctx, n_lat, l_lat)
            q, k, v, gg = l1_in(x, norm1_g[i][None], sc1, sh1, ret_w_in[e].astype(bf16), cos_t, sin_t, rope_blk)
            lg = jax.nn.log_sigmoid(ret_decay_logit[e].astype(f32))
            lg_tab = jnp.zeros((RET_HEADS, 8, 128), f32).at[:, :2, :].set(jnp.transpose(lg)[:, :, None])
            a, fs_ret = ret_scan(q, k, v, gg, lg_tab, ret_norm_w[e], ret_norm_b[e],
                                 with_ctx_zeros(state_ret[:, e], n_sb_ctx), first, last, l_lat)
            new_ret.append(fs_ret[:n_ctx])
            x, hn2, affT = l1_out(a, ret_w_out[e].astype(bf16), x, g1, norm2_g[i][None], sc2, sh2, router_w[i])
        fin = final_norm_g if i == DEPTH - 1 else None
        out = moe_layer(x, hn2, affT, g2, exp_w_gate, exp_w_up, exp_w_down, i, n_ctx, l_lat, final_g=fin)
        if fin is None:
            x = out
    y_ctx, y_lat = out
    return (y_ctx.reshape(n_ctx, l_ctx, D), y_lat.reshape(n_lat, l_lat, D),
            jnp.stack(new_ssd, 1), jnp.stack(new_rwkv, 1), jnp.stack(new_ret, 1))
```

```python
import functools
import math

import jax
import jax.numpy as jnp
from jax import lax
from jax.experimental import pallas as pl
from jax.experimental.pallas import tpu as pltpu

D_MODEL = 1024
DEPTH = 2
GRID_W = 64
CHUNK = 128
NORM_EPS = 1e-6
SSD_HEADS = 16
SSD_P = 64
SSD_INNER = SSD_HEADS * SSD_P
SSD_GROUPS = 2
SSD_HPG = SSD_HEADS // SSD_GROUPS
SSD_N = 128
SSD_XBC = SSD_INNER + 2 * SSD_GROUPS * SSD_N
SSD_IN = SSD_INNER + SSD_XBC + SSD_HEADS
RWKV_HEADS = 16
RWKV_N = 64
RWKV_DIM = RWKV_HEADS * RWKV_N
W_LORA = 64
A_LORA = 64
G_LORA = 128
RWKV_GN_EPS = 64e-5
RET_HEADS = 8
RET_DK = 128
RET_DV = 256
RET_QK = RET_HEADS * RET_DK
RET_V = RET_HEADS * RET_DV
ROPE_BASE = 10000.0
N_EXPERTS = 16
EC_CAPACITY = 2

ACT_DTYPE = jnp.bfloat16

RWKV_C = 64
RWKV_GH = 4
RWKV_GL = RWKV_GH * RWKV_N


def _split_bf16(x):
    hi = x.astype(jnp.bfloat16)
    lo = (x - hi.astype(jnp.float32)).astype(jnp.bfloat16)
    return hi, lo


def _dot(a, b, dims, passes):
    f = functools.partial(lax.dot_general, dimension_numbers=(dims, ((), ())),
                          preferred_element_type=jnp.float32)
    if passes == 1:
        return f(a.astype(jnp.bfloat16), b.astype(jnp.bfloat16))
    ah, al = _split_bf16(a)
    bh, bl = _split_bf16(b)
    return f(ah, bh) + (f(ah, bl) + f(al, bh))


def _rwkv_chunk_kernel(tbl_ref, r0_ref, v0_ref, a0_ref, r1_ref, v1_ref, a1_ref, lw0_ref, k0_ref, b0_ref,
                       lw1_ref, k1_ref, b1_ref, s0_ref, y0_ref, y1_ref, sf_ref, h_ref, *, p_inv, p_oth):
    C, N, GH, GL = RWKV_C, RWKV_N, RWKV_GH, RWKV_GL
    i = pl.program_id(0)
    f32, bf16 = jnp.float32, jnp.bfloat16

    @pl.when(tbl_ref[3, i] == 1)
    def _():
        h_ref[...] = s0_ref[0]

    t_i = lax.broadcasted_iota(jnp.int32, (C, GL), 0)
    s_i = lax.broadcasted_iota(jnp.int32, (C, GL), 1) & (N - 1)
    eye = (s_i == t_i).astype(f32)
    row2 = lax.broadcasted_iota(jnp.int32, (2 * C, GL), 0)
    rel2 = (lax.broadcasted_iota(jnp.int32, (2 * C, GL), 1) & (N - 1)) - (row2 & (C - 1))
    incl2 = row2 // C
    mask2 = [rel2 - incl2 < 0, -rel2 - incl2 < 0]
    bh_r = lax.broadcasted_iota(jnp.int32, (GL, GL), 0) // N
    bh_c = lax.broadcasted_iota(jnp.int32, (GL, GL), 1) // N
    blk = bh_r == bh_c
    tt = lax.broadcasted_iota(jnp.int32, (C, C), 0)
    ss = lax.broadcasted_iota(jnp.int32, (C, C), 1)
    tri = [(ss <= tt).astype(bf16), (ss >= tt).astype(bf16)]

    def bd(x, passes):
        pieces = []
        for _ in range(1 if passes == 1 else 2):
            hi = x.astype(bf16)
            x = x - hi.astype(f32)
            pieces.append(jnp.where(blk, jnp.concatenate([hi] * GH, axis=0), jnp.zeros((), bf16)))
        return pieces

    def mm(l, x, passes, dims=((1,), (0,))):
        f = functools.partial(lax.dot_general, dimension_numbers=(dims, ((), ())), preferred_element_type=f32)
        xs = bd(x, passes)
        lh = l.astype(bf16)
        if passes == 1:
            return f(lh, xs[0])
        ll = (l - lh.astype(f32)).astype(bf16)
        m = l.shape[0]
        both = f(jnp.concatenate([lh, ll], axis=0), xs[0])
        return both[:m] + (f(lh, xs[1]) + both[m:])

    nt = ((1,), (1,))
    refs = [(r0_ref, v0_ref, a0_ref, lw0_ref, k0_ref, b0_ref), (r1_ref, v1_ref, a1_ref, lw1_ref, k1_ref, b1_ref)]
    lw, r_t, a_t, b_t, k_t, v = [], [], [], [], [], []
    for d, (r_ref, v_ref, a_ref, lw_ref, k_ref, b_ref) in enumerate(refs):
        lwd = lw_ref[0]
        lw_hi, lw_lo = _split_bf16(lwd)
        cum = (jnp.dot(tri[d], lw_hi, preferred_element_type=f32) + jnp.dot(tri[d], lw_lo, preferred_element_type=f32))
        w_inv = jnp.exp(-cum)
        lw.append(lwd)
        r_t.append(r_ref[...].astype(f32) * jnp.exp(cum))
        a_t.append(a_ref[...].astype(f32) * jnp.exp(cum - lwd))
        b_t.append(b_ref[0].astype(f32) * w_inv)
        k_t.append(k_ref[0].astype(f32) * w_inv)
        v.append(v_ref[...].astype(f32))
    n_i = lax.broadcasted_iota(jnp.int32, (N, GL), 0)
    lane_i = lax.broadcasted_iota(jnp.int32, (N, GL), 1)
    diag_sel = (lane_i & (N - 1)) == n_i
    lane_head = lane_i // N
    ones_bd = blk.astype(f32)

    chains = [(d, slice(g * GL, (g + 1) * GL)) for d in range(2) for g in range(RWKV_HEADS // GH)]
    each = lambda fn: [fn(j, d, sl) for j, (d, sl) in enumerate(chains)]
    bg = each(lambda j, d, sl: b_t[d][:, sl])
    kg = each(lambda j, d, sl: k_t[d][:, sl])
    vg = each(lambda j, d, sl: v[d][:, sl])
    h0 = each(lambda j, d, sl: h_ref[d, :, sl])
    ar = each(lambda j, d, sl: jnp.concatenate([a_t[d][:, sl], r_t[d][:, sl]], axis=0))
    m_b = each(lambda j, d, sl: jnp.where(mask2[d], mm(ar[j], bg[j], p_oth, nt), 0.0))
    m_k = each(lambda j, d, sl: jnp.where(mask2[d], mm(ar[j], kg[j], p_oth, nt), 0.0))
    p = each(lambda j, d, sl: mm(m_b[j][:C], m_b[j][:C], p_inv))
    tmat = each(lambda j, d, sl: eye + m_b[j][:C])
    for _ in range(int(math.log2(C)) - 2):
        pt = each(lambda j, d, sl: mm(jnp.concatenate([p[j], tmat[j]], axis=0), p[j], p_inv))
        p = each(lambda j, d, sl: pt[j][:C])
        tmat = each(lambda j, d, sl: tmat[j] + pt[j][C:])
    tmat = each(lambda j, d, sl: tmat[j] + mm(tmat[j], p[j], p_inv))
    ar_h = each(lambda j, d, sl: mm(ar[j], h0[j], p_oth))
    mk_v = each(lambda j, d, sl: mm(m_k[j], vg[j], p_oth))
    u = each(lambda j, d, sl: mm(tmat[j], ar_h[j][:C] + mk_v[j][:C], p_oth))
    y = each(lambda j, d, sl: ar_h[j][C:] + mm(m_b[j][C:], u[j], p_oth) + mk_v[j][C:])
    full = each(lambda j, d, sl: _dot(jnp.concatenate([bg[j], kg[j]], axis=0), jnp.concatenate([u[j], vg[j]], axis=0),
                                      ((0,), (0,)), p_oth))
    y_refs = (y0_ref, y1_ref)
    for j, (d, sl) in enumerate(chains):
        y_refs[d][:, sl] = y[j].astype(y_refs[d].dtype)
        z = jnp.zeros((N, GL), f32)
        for hh in range(GH):
            z = z + jnp.where(lane_head == hh, full[j][hh * N:(hh + 1) * N], 0.0)
        w_tot = jnp.exp(jnp.sum(lw[d][:, sl], axis=0, keepdims=True))
        wsel = jnp.where(diag_sel, jnp.broadcast_to(w_tot, (N, GL)), 0.0)
        wcol = _dot(wsel, ones_bd, ((1,), (0,)), 3)
        h_ref[d, :, sl] = wcol * (h0[j] + z)

    @pl.when(tbl_ref[4, i] == 1)
    def _():
        sf_ref[0] = h_ref[...]


def _rwkv_steps(n_ctx, n_lat, l_lat):
    C = RWKV_C
    rows = []
    seqs = [(s, s * BLK, BLK) for s in range(n_ctx)] + [(n_ctx + s, n_ctx * BLK + s * l_lat, l_lat) for s in range(n_lat)]
    for sid, row0, length in seqs:
        nc = length // C
        for j in range(nc):
            rows.append((row0 // C + j, row0 // C + nc - 1 - j, sid, int(j == 0), int(j == nc - 1)))
    return [list(col) for col in zip(*rows)]


def rwkv_scan_pallas(r, v, a, lw, k, b, s0, steps, p_inv=3, p_oth=1):
    R, HN = r.shape
    C, N = RWKV_C, RWKV_N
    tbl = jnp.asarray(steps, jnp.int32)
    fwd = pl.BlockSpec((C, HN), lambda i, t: (t[0, i], 0))
    bwd = pl.BlockSpec((C, HN), lambda i, t: (t[1, i], 0))
    fwd_d = pl.BlockSpec((1, C, HN), lambda i, t: (0, t[0, i], 0))
    bwd_d = pl.BlockSpec((1, C, HN), lambda i, t: (1, t[1, i], 0))
    st = pl.BlockSpec((1, 2, N, HN), lambda i, t: (t[2, i], 0, 0, 0))
    return pl.pallas_call(
        functools.partial(_rwkv_chunk_kernel, p_inv=p_inv, p_oth=p_oth),
        grid_spec=pltpu.PrefetchScalarGridSpec(
            num_scalar_prefetch=1,
            grid=(len(steps[0]),),
            in_specs=[fwd, fwd, fwd, bwd, bwd, bwd, fwd_d, fwd_d, fwd_d, bwd_d, bwd_d, bwd_d, st],
            out_specs=[fwd, bwd, st],
            scratch_shapes=[pltpu.VMEM((2, N, HN), jnp.float32)]),
        out_shape=[jax.ShapeDtypeStruct((R, HN), ACT_DTYPE), jax.ShapeDtypeStruct((R, HN), ACT_DTYPE),
                   jax.ShapeDtypeStruct(s0.shape, jnp.float32)],
        compiler_params=pltpu.CompilerParams(dimension_semantics=("arbitrary",)),
        name="rwkv_scan",
    )(tbl, r, v, a, r, v, a, lw, k, b, lw, k, b, s0)


BLK = 256
FF_TILE = 768
SELECT_TILE = 512
GATHER_ROWS = 512
SELECT_MIN_EXP = -1100.0
SELECT_BINADE_STEPS = 11
SELECT_MANTISSA_STEPS = 40
MOE_FFN_VMEM_BYTES = 48 * 1024 * 1024


def _moe_select_kernel(aff_ref, slot_ref, *, cap):
    a = aff_ref[...]
    E, T = a.shape
    f32 = jnp.float32

    def enough(piv):
        return jnp.sum(jnp.where(a >= piv, 1.0, 0.0), axis=1, keepdims=True) >= cap

    def binade(_, lohi):
        e_lo, e_hi = lohi
        mid = jnp.floor((e_lo + e_hi) * 0.5)
        ok = enough(jnp.exp2(mid))
        return jnp.where(ok, mid, e_lo), jnp.where(ok, e_hi, mid)

    e_lo, e_hi = lax.fori_loop(0, SELECT_BINADE_STEPS, binade,
                               (jnp.full((E, 1), SELECT_MIN_EXP, f32), jnp.full((E, 1), 1.0, f32)))

    def inside(_, lohi):
        lo, hi = lohi
        mid = lo + (hi - lo) * 0.5
        ok = enough(mid)
        return jnp.where(ok, mid, lo), jnp.where(ok, hi, mid)

    thr, _ = lax.fori_loop(0, SELECT_MANTISSA_STEPS, inside, (jnp.exp2(e_lo), jnp.exp2(e_hi)))
    gt = a > thr
    eq = a == thr
    need = cap - jnp.sum(jnp.where(gt, 1.0, 0.0), axis=1, keepdims=True)
    tw = min(T, SELECT_TILE)

    def prefix_count(mask):
        m = jnp.where(mask, 1.0, 0.0).astype(jnp.bfloat16)
        outs = []
        for j in range(T // tw):
            s_i = lax.broadcasted_iota(jnp.int32, (T, tw), 0)
            t_i = lax.broadcasted_iota(jnp.int32, (T, tw), 1) + j * tw
            before = jnp.where(s_i < t_i, 1.0, 0.0).astype(jnp.bfloat16)
            outs.append(jnp.dot(m, before, preferred_element_type=f32))
        return outs[0] if len(outs) == 1 else jnp.concatenate(outs, axis=1)

    sel = gt | (eq & (prefix_count(eq) < need))
    slot_ref[...] = jnp.where(sel, prefix_count(sel).astype(jnp.int32), -1)


def _moe_select(affT, row0, n_seq, t):
    E = affT.shape[0]
    rows = jnp.transpose(affT[:, row0:row0 + n_seq * t].reshape(E, n_seq, t), (1, 0, 2)).reshape(n_seq * E, t)
    slot = pl.pallas_call(
        functools.partial(_moe_select_kernel, cap=EC_CAPACITY * t // N_EXPERTS),
        grid=(1,),
        in_specs=[pl.BlockSpec((n_seq * E, t), lambda s: (0, 0))],
        out_specs=pl.BlockSpec((n_seq * E, t), lambda s: (0, 0)),
        out_shape=jax.ShapeDtypeStruct((n_seq * E, t), jnp.int32),
        name="moe_select",
    )(rows)
    return jnp.transpose(slot.reshape(n_seq, E, t), (1, 0, 2)).reshape(E, n_seq * t)


def _moe_gather_kernel(slot_ref, aff_ref, hn_ref, xe_ref, gate_ref, *, cap):
    slot = slot_ref[...]
    eg, _, T = slot.shape
    p_i = lax.broadcasted_iota(jnp.int32, (eg, cap, T), 1)
    hit = slot == p_i
    onehot = jnp.where(hit, 1.0, 0.0).reshape(eg * cap, T).astype(jnp.bfloat16)
    xe = jnp.dot(onehot, hn_ref[...], preferred_element_type=jnp.float32)
    xe_ref[...] = xe.reshape(eg, cap, xe.shape[1]).astype(xe_ref.dtype)
    g = jnp.sum(jnp.where(hit, aff_ref[...], 0.0), axis=2, keepdims=True)
    gate_ref[...] = jnp.broadcast_to(g, (eg, cap, 128))


def _moe_gather(slot3, aff3, hn, row0, n_seq, t):
    E = slot3.shape[0]
    D = hn.shape[1]
    cap = EC_CAPACITY * t // N_EXPERTS
    eg = min(E, GATHER_ROWS // cap)
    b0 = row0 // t
    return pl.pallas_call(
        functools.partial(_moe_gather_kernel, cap=cap),
        grid=(n_seq, E // eg),
        in_specs=[pl.BlockSpec((eg, 1, t), lambda s, e: (e, 0, b0 + s)),
                  pl.BlockSpec((eg, 1, t), lambda s, e: (e, 0, b0 + s)),
                  pl.BlockSpec((t, D), lambda s, e: (b0 + s, 0))],
        out_specs=[pl.BlockSpec((eg, cap, D), lambda s, e: (e, s, 0)),
                   pl.BlockSpec((eg, cap, 128), lambda s, e: (e, s, 0))],
        out_shape=[jax.ShapeDtypeStruct((E, n_seq * cap, D), jnp.bfloat16),
                   jax.ShapeDtypeStruct((E, n_seq * cap, 128), jnp.float32)],
        compiler_params=pltpu.CompilerParams(dimension_semantics=("arbitrary", "arbitrary"),
                                             vmem_limit_bytes=MOE_FFN_VMEM_BYTES),
        name="moe_gather",
    )(slot3, aff3, hn)


def _moe_ffn_kernel(xc_ref, xl_ref, gc_ref, gl_ref, wg_ref, wu_ref, wd_ref, yc_ref, yl_ref, accc_ref, accl_ref):
    f = pl.program_id(1)
    nf = pl.num_programs(1)
    bf16 = jnp.bfloat16
    wg = wg_ref[0, 0].astype(bf16)
    wu = wu_ref[0, 0].astype(bf16)
    wd = wd_ref[0, 0].astype(bf16)

    def part(x_ref, acc_ref):
        x = x_ref[0]
        g = jnp.dot(x, wg, preferred_element_type=jnp.float32)
        u = jnp.dot(x, wu, preferred_element_type=jnp.float32)
        h = (g * jax.nn.sigmoid(g) * u).astype(bf16)
        y = jnp.dot(h, wd, preferred_element_type=jnp.float32)

        @pl.when(f == 0)
        def _():
            acc_ref[...] = y

        @pl.when(f != 0)
        def _():
            acc_ref[...] += y

    part(xc_ref, accc_ref)
    part(xl_ref, accl_ref)

    @pl.when(f == nf - 1)
    def _():
        for acc_ref, g_ref, y_ref in ((accc_ref, gc_ref, yc_ref), (accl_ref, gl_ref, yl_ref)):
            gate = jnp.concatenate([g_ref[0]] * (acc_ref.shape[1] // 128), axis=1)
            y_ref[0] = (acc_ref[...] * gate).astype(y_ref.dtype)


def _moe_ffn(xc, xl, gc, gl, wg, wu, wd, layer):
    E, nc_rows, D = xc.shape
    nl_rows = xl.shape[1]
    F = wg.shape[3]
    nf = F // FF_TILE
    return pl.pallas_call(
        _moe_ffn_kernel,
        grid=(E, nf),
        in_specs=[pl.BlockSpec((1, nc_rows, D), lambda e, f: (e, 0, 0)),
                  pl.BlockSpec((1, nl_rows, D), lambda e, f: (e, 0, 0)),
                  pl.BlockSpec((1, nc_rows, 128), lambda e, f: (e, 0, 0)),
                  pl.BlockSpec((1, nl_rows, 128), lambda e, f: (e, 0, 0)),
                  pl.BlockSpec((1, 1, D, FF_TILE), lambda e, f: (layer, e, 0, f)),
                  pl.BlockSpec((1, 1, D, FF_TILE), lambda e, f: (layer, e, 0, f)),
                  pl.BlockSpec((1, 1, FF_TILE, D), lambda e, f: (layer, e, f, 0))],
        out_specs=[pl.BlockSpec((1, nc_rows, D), lambda e, f: (e, 0, 0)),
                   pl.BlockSpec((1, nl_rows, D), lambda e, f: (e, 0, 0))],
        out_shape=[jax.ShapeDtypeStruct((E, nc_rows, D), jnp.bfloat16),
                   jax.ShapeDtypeStruct((E, nl_rows, D), jnp.bfloat16)],
        scratch_shapes=[pltpu.VMEM((nc_rows, D), jnp.float32), pltpu.VMEM((nl_rows, D), jnp.float32)],
        compiler_params=pltpu.CompilerParams(dimension_semantics=("arbitrary", "arbitrary"),
                                             vmem_limit_bytes=MOE_FFN_VMEM_BYTES),
        name="moe_ffn",
    )(xc, xl, gc, gl, wg, wu, wd)


def _moe_scatter_kernel(slot_ref, ye_ref, x_ref, g2_ref, *rest, cap, final):
    if final:
        fg_ref, o_ref = rest
    else:
        (o_ref,) = rest
    E = ye_ref.shape[0]
    D = ye_ref.shape[2]
    tb = x_ref.shape[0]
    slot = slot_ref[...]
    p_i = lax.broadcasted_iota(jnp.int32, (E, cap, tb), 1)
    onehot = jnp.where(slot == p_i, 1.0, 0.0).reshape(E * cap, tb).astype(jnp.bfloat16)
    ye = ye_ref[...].reshape(E * cap, D)
    moe = lax.dot_general(onehot, ye, (((0,), (0,)), ((), ())), preferred_element_type=jnp.float32)
    x = x_ref[...] + g2_ref[0] * moe
    if final:
        x = x * lax.rsqrt(jnp.mean(x * x, -1, keepdims=True) + NORM_EPS) * fg_ref[...]
    o_ref[...] = x


def _moe_scatter(slot3, ye, x, g2blk, row0, n_seq, t, final_g=None):
    E, _, D = ye.shape
    cap = EC_CAPACITY * t // N_EXPERTS
    nb = t // BLK
    b0 = row0 // BLK
    final = final_g is not None
    in_specs = [pl.BlockSpec((E, 1, BLK), lambda s, j: (0, 0, b0 + s * nb + j)),
                pl.BlockSpec((E, cap, D), lambda s, j: (0, s, 0)),
                pl.BlockSpec((BLK, D), lambda s, j: (b0 + s * nb + j, 0)),
                pl.BlockSpec((1, 1, D), lambda s, j: (b0 + s * nb + j, 0, 0))]
    args = [slot3, ye, x, g2blk]
    if final:
        in_specs.append(pl.BlockSpec((1, D), lambda s, j: (0, 0)))
        args.append(final_g.reshape(1, D))
        out_specs = pl.BlockSpec((BLK, D), lambda s, j: (s * nb + j, 0))
        out_shape = jax.ShapeDtypeStruct((n_seq * t, D), jnp.float32)
        aliases = {}
    else:
        out_specs = pl.BlockSpec((BLK, D), lambda s, j: (b0 + s * nb + j, 0))
        out_shape = jax.ShapeDtypeStruct(x.shape, jnp.float32)
        aliases = {2: 0}
    return pl.pallas_call(
        functools.partial(_moe_scatter_kernel, cap=cap, final=final),
        grid=(n_seq, nb),
        in_specs=in_specs, out_specs=out_specs, out_shape=out_shape,
        input_output_aliases=aliases,
        compiler_params=pltpu.CompilerParams(dimension_semantics=("arbitrary", "arbitrary"),
                                             vmem_limit_bytes=MOE_FFN_VMEM_BYTES),
        name="moe_scatter",
    )(*args)


def moe_layer(x, hn, affT, g2blk, wg, wu, wd, layer, n_ctx, l_lat, final_g=None):
    R = x.shape[0]
    r_ctx = n_ctx * BLK
    n_lat = (R - r_ctx) // l_lat
    slot = jnp.concatenate([_moe_select(affT, 0, n_ctx, BLK), _moe_select(affT, r_ctx, n_lat, l_lat)], axis=1)
    slot3 = slot[:, None, :]
    aff3 = affT[:, None, :]
    xc, gc = _moe_gather(slot3, aff3, hn, 0, n_ctx, BLK)
    xl, gl = _moe_gather(slot3, aff3, hn, r_ctx, n_lat, l_lat)
    yc, yl = _moe_ffn(xc, xl, gc, gl, wg, wu, wd, layer)
    if final_g is None:
        x = _moe_scatter(slot3, yc, x, g2blk, 0, n_ctx, BLK)
        return _moe_scatter(slot3, yl, x, g2blk, r_ctx, n_lat, l_lat)
    return (_moe_scatter(slot3, yc, x, g2blk, 0, n_ctx, BLK, final_g),
            _moe_scatter(slot3, yl, x, g2blk, r_ctx, n_lat, l_lat, final_g))


L0_Z = (0, 1024)
L0_XBC = (1024, 2560)
L0_SHIFT = (2560, 6016)
L0_DT = (6016, 6144)
L0_COLS = 6144
L0_VMEM_BYTES = 56 * 1024 * 1024


def _sum_split(x, m, n_split):
    acc = None
    for _ in range(n_split):
        hi = x.astype(jnp.bfloat16)
        x = x - hi.astype(jnp.float32)
        t = jnp.dot(hi, m, preferred_element_type=jnp.float32)
        acc = t if acc is None else acc + t
    return acc


def _sum_split_left(m, x, n_split):
    acc = None
    for _ in range(n_split):
        hi = x.astype(jnp.bfloat16)
        x = x - hi.astype(jnp.float32)
        t = jnp.dot(m, hi, preferred_element_type=jnp.float32)
        acc = t if acc is None else acc + t
    return acc


def _head_sum(x, e_ref, et_ref):
    return _sum_split(_sum_split(x, e_ref[...], 2), et_ref[...], 2)


def _adaln(x, g, sc, sh):
    y = x * lax.rsqrt(jnp.mean(x * x, -1, keepdims=True) + NORM_EPS) * g
    return y * (1.0 + sc) + sh


def _softplus(x):
    return jnp.maximum(x, 0.0) + jnp.log(1.0 + jnp.exp(-jnp.abs(x)))


def _l0_in_kernel(tbl_ref, x_ref, xp_ref, xn_ref, g_ref, sc_ref, sh_ref, w_ref, mup_ref, mun_ref, rwp_ref,
                  w2_ref, a2_ref, g2_ref, e_ref, et_ref,
                  z_ref, xbc_ref, dt_ref, r_ref, v_ref, an_ref, lw_ref, kd_ref, bv_ref, gate_ref, bonus_ref):
    i = pl.program_id(0)
    f32, bf16 = jnp.float32, jnp.bfloat16
    g, sc, sh = g_ref[...], sc_ref[0], sh_ref[0]
    hn = _adaln(x_ref[...], g, sc, sh).astype(bf16)
    halo = _adaln(jnp.concatenate([xp_ref[...], xn_ref[...]], axis=0), g, sc, sh).astype(bf16)
    hn_halo = jnp.concatenate([hn, halo], axis=0)

    keep_prev = (1 - tbl_ref[0, i]).astype(f32)
    keep_next = (1 - tbl_ref[1, i]).astype(f32)
    row = lax.broadcasted_iota(jnp.int32, (BLK, 1), 0)
    c = RWKV_DIM

    def proj(c0, c1):
        return jnp.dot(hn_halo, w_ref[:, L0_SHIFT[0] + c0:L0_SHIFT[0] + c1], preferred_element_type=f32)

    def shift(both, c0, c1):
        cur = both[:BLK]
        prev = jnp.where(row == 0, both[BLK + 7:BLK + 8] * keep_prev, pltpu.roll(cur, 1, 0))
        nxt = jnp.where(row == BLK - 1, both[BLK + 8:BLK + 9] * keep_next, pltpu.roll(cur, BLK - 1, 0))
        return cur + mup_ref[:, c0:c1] * (prev - cur) + mun_ref[:, c0:c1] * (nxt - cur)

    lo = 3 * c + 2 * W_LORA
    p_wl = proj(3 * c, lo)
    p_ag = proj(lo, lo + 256)
    p_k = proj(c, 2 * c)
    wl = shift(p_wl, 3 * c, lo)
    ag = shift(p_ag, lo, lo + 256)
    p_r = proj(0, c)
    k = shift(p_k, c, 2 * c)
    k_k, k_a, r_k = rwp_ref[0:1], rwp_ref[1:2], rwp_ref[2:3]
    w_lin = jnp.dot(jnp.tanh(wl).astype(bf16), w2_ref[...], preferred_element_type=f32)
    a_lora = jnp.dot(ag.astype(bf16), a2_ref[...], preferred_element_type=f32)
    gate_ref[...] = jnp.dot(jax.nn.sigmoid(ag).astype(bf16), g2_ref[...],
                            preferred_element_type=f32).astype(gate_ref.dtype)
    p_v = proj(2 * c, 3 * c)
    r = shift(p_r, 0, c)
    r_ref[...] = r.astype(r_ref.dtype)
    kk = k * k_k
    kk = kk * lax.rsqrt(_head_sum(kk * kk, e_ref, et_ref) + 1e-12)
    an_ref[...] = (-kk).astype(an_ref.dtype)
    z_ref[...] = jnp.dot(hn, w_ref[:, L0_Z[0]:L0_Z[1]], preferred_element_type=f32).astype(z_ref.dtype)
    v = shift(p_v, 2 * c, 3 * c)
    v_ref[...] = v.astype(v_ref.dtype)
    kd_sum = None
    xbc_cols = (L0_XBC[0], (L0_XBC[0] + L0_XBC[1]) // 2, L0_XBC[1])
    for d in range(2):
        xbc_ref[:, xbc_cols[d] - L0_XBC[0]:xbc_cols[d + 1] - L0_XBC[0]] = jnp.dot(
            hn, w_ref[:, xbc_cols[d]:xbc_cols[d + 1]], preferred_element_type=f32)
        w_log = -_softplus(-(rwp_ref[3 + d:4 + d] + w_lin[:, d * c:(d + 1) * c])) - 0.5
        lw_ref[d] = -jnp.exp(w_log)
        a = jax.nn.sigmoid(rwp_ref[5 + d:6 + d] + a_lora)
        kd = k * (1.0 + (a - 1.0) * k_a)
        kd_ref[d] = kd.astype(kd_ref.dtype)
        bv_ref[d] = (kk * a).astype(bv_ref.dtype)
        kd_sum = kd if kd_sum is None else kd_sum + kd
    dt_ref[...] = jnp.dot(hn, w_ref[:, L0_DT[0]:L0_DT[1]], preferred_element_type=f32)
    bonus_ref[...] = (_head_sum(r * kd_sum * r_k, e_ref, et_ref) * v).astype(bonus_ref.dtype)


def _seq_tables(n_ctx, n_lat, l_lat):
    nb = l_lat // BLK
    cond = [0] * n_ctx + [1 + s for s in range(n_lat) for _ in range(nb)]
    first = [1] * n_ctx + [1 if j == 0 else 0 for _ in range(n_lat) for j in range(nb)]
    last = [1] * n_ctx + [1 if j == nb - 1 else 0 for _ in range(n_lat) for j in range(nb)]
    return cond, first, last


def l0_in(x, g1, scb, shb, w_packed, mup, mun, rwp, w2bd, a2p, g2p, e_ind, et_ind, first, last):
    R, D = x.shape
    nblk = R // BLK
    n8 = R // 8
    tbl = jnp.asarray([first, last], jnp.int32)
    c = RWKV_DIM
    row = lambda i, t: (i, 0)
    full = lambda shape: pl.BlockSpec(shape, lambda i, t: (0,) * len(shape))
    rows = lambda n: pl.BlockSpec((BLK, n), row)
    rows2 = lambda n: pl.BlockSpec((2, BLK, n), lambda i, t: (0, i, 0))
    f32, act = jnp.float32, ACT_DTYPE
    sds = jax.ShapeDtypeStruct
    return pl.pallas_call(
        _l0_in_kernel,
        grid_spec=pltpu.PrefetchScalarGridSpec(
            num_scalar_prefetch=1,
            grid=(nblk,),
            in_specs=[rows(D),
                      pl.BlockSpec((8, D), lambda i, t: (jnp.maximum(i * (BLK // 8) - 1, 0), 0)),
                      pl.BlockSpec((8, D), lambda i, t: (jnp.minimum((i + 1) * (BLK // 8), n8 - 1), 0)),
                      full((1, D)),
                      pl.BlockSpec((1, 1, D), lambda i, t: (i, 0, 0)),
                      pl.BlockSpec((1, 1, D), lambda i, t: (i, 0, 0)),
                      full((D, L0_COLS)), full(mup.shape), full(mun.shape), full(rwp.shape),
                      full(w2bd.shape), full(a2p.shape), full(g2p.shape), full(e_ind.shape), full(et_ind.shape)],
            out_specs=[rows(c), rows(SSD_XBC), rows(128), rows(c), rows(c), rows(c),
                       rows2(c), rows2(c), rows2(c), rows(c), rows(c)]),
        out_shape=[sds((R, c), act), sds((R, SSD_XBC), f32), sds((R, 128), f32), sds((R, c), act), sds((R, c), act),
                   sds((R, c), act), sds((2, R, c), f32), sds((2, R, c), act), sds((2, R, c), act),
                   sds((R, c), act), sds((R, c), act)],
        compiler_params=pltpu.CompilerParams(dimension_semantics=("arbitrary",), vmem_limit_bytes=L0_VMEM_BYTES),
        name="l0_in",
    )(tbl, x, x, x, g1, scb, shb, w_packed, mup, mun, rwp, w2bd, a2p, g2p, e_ind, et_ind)


def l0_pack_weights(p, e):
    bf16 = jnp.bfloat16
    w = p['ab_w_in'][e]
    D = w.shape[0]
    c = RWKV_DIM
    rw0 = SSD_IN
    ag0 = rw0 + 3 * c + 2 * W_LORA
    w_packed = jnp.concatenate([
        w[:, :SSD_INNER + SSD_XBC], w[:, rw0:ag0], w[:, ag0:ag0 + A_LORA + G_LORA],
        jnp.zeros((D, 256 - A_LORA - G_LORA), w.dtype),
        w[:, SSD_INNER + SSD_XBC:SSD_IN], jnp.zeros((D, 128 - SSD_HEADS), w.dtype)], axis=1).astype(bf16)

    def pack_mu(mu):
        return jnp.concatenate([mu, jnp.zeros((256 - A_LORA - G_LORA,), mu.dtype)])[None, :]

    rwp = jnp.stack([p['rwkv_k_k'][e], p['rwkv_k_a'][e], p['rwkv_r_k'][e].reshape(-1), p['rwkv_w0'][e, 0],
                     p['rwkv_w0'][e, 1], p['rwkv_a0'][e, 0], p['rwkv_a0'][e, 1], jnp.zeros((c,), jnp.float32)])
    zw = jnp.zeros((W_LORA, c), jnp.float32)
    w2bd = jnp.concatenate([jnp.concatenate([p['rwkv_w2'][e, 0], zw], axis=1),
                            jnp.concatenate([zw, p['rwkv_w2'][e, 1]], axis=1)], axis=0).astype(bf16)
    a2p = jnp.concatenate([p['rwkv_a2'][e], jnp.zeros((256 - A_LORA, c), jnp.float32)], axis=0).astype(bf16)
    g2p = jnp.concatenate([jnp.zeros((A_LORA, c), jnp.float32), p['rwkv_g2'][e],
                           jnp.zeros((256 - A_LORA - G_LORA, c), jnp.float32)], axis=0).astype(bf16)
    head = jnp.arange(c) // RWKV_N
    e_ind = (head[:, None] == jnp.arange(128)[None, :]).astype(bf16)
    return w_packed, pack_mu(p['rwkv_mu_prev'][e]), pack_mu(p['rwkv_mu_next'][e]), rwp, w2bd, a2p, g2p, e_ind, e_ind.T


SSD_QH = 4
SSD_VMEM_BYTES = 48 * 1024 * 1024
NEG_BIG = -1e30
LOG2E = 1.4426950408889634


def _conv_silu(cur, prev_row, next_row, w_ref, b_ref):
    row = lax.broadcasted_iota(jnp.int32, (BLK, 1), 0)
    prev = jnp.where(row == 0, prev_row, pltpu.roll(cur, 1, 0))
    nxt = jnp.where(row == BLK - 1, next_row, pltpu.roll(cur, BLK - 1, 0))
    y = w_ref[0:1] * prev + w_ref[1:2] * cur + w_ref[2:3] * nxt + b_ref[...]
    return y * jax.nn.sigmoid(y)


def _ssd_kernel(tbl_ref, xs_ref, b_ref, c_ref, dt_ref, cwx_ref, cwb_ref, cwc_ref, cbx_ref, cbb_ref, cbc_ref,
                sel_ref, hp_ref, s0_ref, y_ref, fs_ref, xa_ref, ba_ref, ca_ref, sfx_ref, ldb_ref, st_ref):
    f32, bf16 = jnp.float32, jnp.bfloat16
    sb = pl.program_id(0)
    sbr, qw = xs_ref.shape
    nch = sbr // BLK
    P, QH = SSD_P, SSD_QH
    t_i = lax.broadcasted_iota(jnp.int32, (BLK, BLK), 0)
    s_i = lax.broadcasted_iota(jnp.int32, (BLK, BLK), 1)
    lower = s_i <= t_i
    upper = s_i >= t_i
    tri_lo = jnp.where(lower, 1.0, 0.0).astype(bf16)
    tri_up = jnp.where(upper, 1.0, 0.0).astype(bf16)
    hp = hp_ref[0]
    sel = sel_ref[0]
    ind = jnp.where(lax.broadcasted_iota(jnp.int32, (128, qw), 1) // P == lax.broadcasted_iota(jnp.int32, (128, qw), 0),
                    1.0, 0.0).astype(bf16)
    head_of_lane = lax.broadcasted_iota(jnp.int32, (1, qw), 1) // P

    def expand(cols):
        return _sum_split(cols, ind, 3)

    d_row = expand(jnp.broadcast_to(hp[4:5], (8, 128)))[0:1]

    def chunk_rows(c):
        return pl.ds(pl.multiple_of(c * BLK, BLK), BLK)

    def neighbours(ref, c, keep_prev, keep_next):
        lo = jnp.maximum(c * BLK - 1, 0)
        hi = jnp.minimum((c + 1) * BLK, sbr - 1)
        return ref[pl.ds(lo, 1), :] * keep_prev, ref[pl.ds(hi, 1), :] * keep_next

    def fwd(c, carry):
        blk = sb * nch + c
        first, last = tbl_ref[0, blk], tbl_ref[1, blk]
        kp, kn = (1 - first).astype(f32), (1 - last).astype(f32)
        rows = chunk_rows(c)

        @pl.when(first == 1)
        def _():
            st_ref[0] = s0_ref[0, 0]

        xa = _conv_silu(xs_ref[rows, :], *neighbours(xs_ref, c, kp, kn), cwx_ref, cbx_ref)
        bm = _conv_silu(b_ref[rows, :], *neighbours(b_ref, c, kp, kn), cwb_ref, cbb_ref)
        cm = _conv_silu(c_ref[rows, :], *neighbours(c_ref, c, kp, kn), cwc_ref, cbc_ref)
        xb, bmb, cmb = xa.astype(bf16), bm.astype(bf16), cm.astype(bf16)
        xa_ref[rows, :] = xb
        ba_ref[rows, :] = bmb
        ca_ref[rows, :] = cmb
        dtq = _sum_split(dt_ref[rows, :], sel, 3)
        dtf = _softplus(dtq + hp[0:1])
        dtb = _softplus(dtq + hp[1:2])
        acs = _sum_split_left(tri_lo, dtf * hp[2:3], 3)
        sfx = _sum_split_left(tri_up, dtb * hp[3:4], 3)
        ldf, ldb = jnp.log(dtf), jnp.log(dtb)
        sfx_ref[rows, :] = sfx
        ldb_ref[rows, :] = ldb
        a2, s2 = acs * LOG2E, sfx * LOG2E
        a2r = (a2 - ldf * LOG2E).T
        s2r = (s2 - ldb * LOG2E).T
        g = lax.dot_general(cmb, bmb, (((1,), (1,)), ((), ())), preferred_element_type=f32)
        y_diag = None
        for j in range(QH):
            m = (g * (jnp.exp2(jnp.where(lower, a2[:, j:j + 1] - a2r[j:j + 1, :], NEG_BIG))
                      + jnp.exp2(jnp.where(upper, s2[:, j:j + 1] - s2r[j:j + 1, :], NEG_BIG)))).astype(bf16)
            xh = jnp.where(head_of_lane == j, xb, jnp.zeros((), bf16))
            t = jnp.dot(m, xh, preferred_element_type=f32)
            y_diag = t if y_diag is None else y_diag + t
        ea = jnp.exp(expand(acs))
        wf = jnp.exp(expand(acs[BLK - 1:BLK] - acs + ldf))
        s_in = st_ref[0]
        y_ref[rows, :] = xa * d_row + y_diag + ea * jnp.dot(cmb, s_in.astype(bf16), preferred_element_type=f32)
        st_ref[0] = ea[BLK - 1:BLK] * s_in + lax.dot_general(
            bmb, (xa * wf).astype(bf16), (((0,), (0,)), ((), ())), preferred_element_type=f32)
        fs_ref[c, 0] = st_ref[0]
        return carry

    lax.fori_loop(0, nch, fwd, 0)

    def bwd(k, carry):
        c = nch - 1 - k
        blk = sb * nch + c
        rows = chunk_rows(c)

        @pl.when(tbl_ref[1, blk] == 1)
        def _():
            st_ref[1] = s0_ref[0, 1]

        sfx = sfx_ref[rows, :]
        eb = jnp.exp(expand(sfx))
        wb = jnp.exp(expand(sfx[0:1] - sfx + ldb_ref[rows, :]))
        s_in = st_ref[1]
        y_ref[rows, :] += eb * jnp.dot(ca_ref[rows, :], s_in.astype(bf16), preferred_element_type=f32)
        st_ref[1] = eb[0:1] * s_in + lax.dot_general(
            ba_ref[rows, :], (xa_ref[rows, :].astype(f32) * wb).astype(bf16), (((0,), (0,)), ((), ())),
            preferred_element_type=f32)
        fs_ref[c, 1] = st_ref[1]
        return carry

    lax.fori_loop(0, nch, bwd, 0)


def ssd_scan(xbc, dt, conv_w, conv_b, sel, hp, s0, first, last, sb_rows):
    R = xbc.shape[0]
    n_sb = R // sb_rows
    nch = sb_rows // BLK
    nq = SSD_HEADS // SSD_QH
    qw = SSD_QH * SSD_P
    qpg = SSD_HPG // SSD_QH
    b_blk = SSD_INNER // SSD_N
    c_blk = b_blk + SSD_GROUPS
    tbl = jnp.asarray([first, last], jnp.int32)
    cw = conv_w
    cb = conv_b.reshape(1, -1)
    f32, bf16 = jnp.float32, jnp.bfloat16
    return pl.pallas_call(
        _ssd_kernel,
        grid_spec=pltpu.PrefetchScalarGridSpec(
            num_scalar_prefetch=1,
            grid=(n_sb, nq),
            in_specs=[pl.BlockSpec((sb_rows, qw), lambda s, q, t: (s, q)),
                      pl.BlockSpec((sb_rows, SSD_N), lambda s, q, t: (s, b_blk + q // qpg)),
                      pl.BlockSpec((sb_rows, SSD_N), lambda s, q, t: (s, c_blk + q // qpg)),
                      pl.BlockSpec((sb_rows, 128), lambda s, q, t: (s, 0)),
                      pl.BlockSpec((3, qw), lambda s, q, t: (0, q)),
                      pl.BlockSpec((3, SSD_N), lambda s, q, t: (0, b_blk + q // qpg)),
                      pl.BlockSpec((3, SSD_N), lambda s, q, t: (0, c_blk + q // qpg)),
                      pl.BlockSpec((1, qw), lambda s, q, t: (0, q)),
                      pl.BlockSpec((1, SSD_N), lambda s, q, t: (0, b_blk + q // qpg)),
                      pl.BlockSpec((1, SSD_N), lambda s, q, t: (0, c_blk + q // qpg)),
                      pl.BlockSpec((1, 128, 128), lambda s, q, t: (q, 0, 0)),
                      pl.BlockSpec((1, 8, 128), lambda s, q, t: (q, 0, 0)),
                      pl.BlockSpec((1, 2, SSD_N, qw), lambda s, q, t: (s, 0, 0, q))],
            out_specs=[pl.BlockSpec((sb_rows, qw), lambda s, q, t: (s, q)),
                       pl.BlockSpec((nch, 2, SSD_N, qw), lambda s, q, t: (s, 0, 0, q))],
            scratch_shapes=[pltpu.VMEM((sb_rows, qw), bf16), pltpu.VMEM((sb_rows, SSD_N), bf16),
                            pltpu.VMEM((sb_rows, SSD_N), bf16), pltpu.VMEM((sb_rows, 128), f32),
                            pltpu.VMEM((sb_rows, 128), f32), pltpu.VMEM((2, SSD_N, qw), f32)]),
        out_shape=[jax.ShapeDtypeStruct((R, SSD_INNER), f32),
                   jax.ShapeDtypeStruct((R // BLK, 2, SSD_N, SSD_INNER), f32)],
        compiler_params=pltpu.CompilerParams(dimension_semantics=("arbitrary", "arbitrary"),
                                             vmem_limit_bytes=SSD_VMEM_BYTES),
        name="ssd_scan",
    )(tbl, xbc, xbc, xbc, dt, cw, cw, cw, cb, cb, cb, sel, hp, s0)


def ssd_tables(p, e):
    nq = SSD_HEADS // SSD_QH
    lane = jnp.arange(128)
    sel = jnp.stack([(lane[:, None] == (q * SSD_QH + lane[None, :])) & (lane[None, :] < SSD_QH)
                     for q in range(nq)]).astype(jnp.bfloat16)
    a_neg = -jnp.exp(p['ssd_a_log'][e].astype(jnp.float32))
    rows = jnp.stack([p['ssd_dt_bias'][e, 0], p['ssd_dt_bias'][e, 1], a_neg[0], a_neg[1], p['ssd_d'][e]])
    hp = jnp.zeros((nq, 8, 128), jnp.float32)
    hp = hp.at[:, :5, :SSD_QH].set(jnp.transpose(rows.reshape(5, nq, SSD_QH), (1, 0, 2)))
    return sel, hp


MIX_VMEM_BYTES = 40 * 1024 * 1024
ROUTER_LANES = 128


def _residual_norm_router(x, out, g1, n2g, sc2, sh2, rw_ref, x_out_ref, hn_ref, aff_ref):
    x_new = x + g1 * out
    x_out_ref[...] = x_new
    hn = _adaln(x_new, n2g, sc2, sh2)
    hn_ref[...] = hn.astype(hn_ref.dtype)
    logits = _dot(hn, rw_ref[...], ((1,), (0,)), 3)
    lane = lax.broadcasted_iota(jnp.int32, logits.shape, 1)
    logits = jnp.where(lane < N_EXPERTS, logits, NEG_BIG)
    ex = jnp.exp(logits - jnp.max(logits, axis=-1, keepdims=True))
    aff = ex / jnp.sum(ex, axis=-1, keepdims=True)
    aff_ref[...] = aff.T[:N_EXPERTS]


def _l0_out_kernel(ys_ref, z_ref, yf_ref, yb_ref, bonus_ref, gate_ref, sg_ref, lnw_ref, lnb_ref, e_ref, et_ref, w_ref,
                   x_ref, g1_ref, n2g_ref, sc2_ref, sh2_ref, rw_ref, x_out_ref, hn_ref, aff_ref):
    f32, bf16 = jnp.float32, jnp.bfloat16
    z = z_ref[...].astype(f32)
    ys = ys_ref[...] * (z * jax.nn.sigmoid(z))
    gw = SSD_INNER // SSD_GROUPS
    parts = []
    for gi in range(SSD_GROUPS):
        yg = ys[:, gi * gw:(gi + 1) * gw]
        parts.append(yg * lax.rsqrt(jnp.mean(yg * yg, -1, keepdims=True) + NORM_EPS))
    a1 = jnp.concatenate(parts, axis=1) * sg_ref[...]
    out_ssd = jnp.dot(a1.astype(bf16), w_ref[:SSD_INNER], preferred_element_type=f32)
    o = yf_ref[...].astype(f32) + yb_ref[...].astype(f32)
    mu = _head_sum(o, e_ref, et_ref) * (1.0 / RWKV_N)
    oc = o - mu
    var = _head_sum(oc * oc, e_ref, et_ref) * (1.0 / RWKV_N)
    o = oc * lax.rsqrt(var + RWKV_GN_EPS) * lnw_ref[...] + lnb_ref[...]
    o = (o + bonus_ref[...].astype(f32)) * gate_ref[...].astype(f32)
    out = out_ssd + jnp.dot(o.astype(bf16), w_ref[SSD_INNER:], preferred_element_type=f32)
    _residual_norm_router(x_ref[...], out, g1_ref[0], n2g_ref[...], sc2_ref[0], sh2_ref[0], rw_ref,
                          x_out_ref, hn_ref, aff_ref)


def _l1_out_kernel(a_ref, w_ref, x_ref, g1_ref, n2g_ref, sc2_ref, sh2_ref, rw_ref, x_out_ref, hn_ref, aff_ref):
    out = jnp.dot(a_ref[...].astype(jnp.bfloat16), w_ref[...], preferred_element_type=jnp.float32)
    _residual_norm_router(x_ref[...], out, g1_ref[0], n2g_ref[...], sc2_ref[0], sh2_ref[0], rw_ref,
                          x_out_ref, hn_ref, aff_ref)


def _mix_out_call(kernel_fn, name, lead_args, lead_specs, w_out, x, g1b, n2g, sc2b, sh2b, router_w):
    R, D = x.shape
    nblk = R // BLK
    full = lambda a: pl.BlockSpec(a.shape, lambda i: (0,) * a.ndim)
    blkrow = pl.BlockSpec((1, 1, D), lambda i: (i, 0, 0))
    rw = jnp.zeros((D, ROUTER_LANES), jnp.float32).at[:, :N_EXPERTS].set(router_w)
    args = list(lead_args) + [w_out, x, g1b, n2g, sc2b, sh2b, rw]
    x_idx = len(lead_args) + 1
    in_specs = list(lead_specs) + [full(w_out), pl.BlockSpec((BLK, D), lambda i: (i, 0)), blkrow, full(n2g), blkrow,
                                   blkrow, full(rw)]
    return pl.pallas_call(
        kernel_fn,
        grid=(nblk,),
        in_specs=in_specs,
        out_specs=[pl.BlockSpec((BLK, D), lambda i: (i, 0)), pl.BlockSpec((BLK, D), lambda i: (i, 0)),
                   pl.BlockSpec((N_EXPERTS, BLK), lambda i: (0, i))],
        out_shape=[jax.ShapeDtypeStruct((R, D), jnp.float32), jax.ShapeDtypeStruct((R, D), jnp.bfloat16),
                   jax.ShapeDtypeStruct((N_EXPERTS, R), jnp.float32)],
        input_output_aliases={x_idx: 0},
        compiler_params=pltpu.CompilerParams(dimension_semantics=("arbitrary",), vmem_limit_bytes=MIX_VMEM_BYTES),
        name=name,
    )(*args)


def l0_out(ys, z, yf, yb, bonus, gate, ssd_g, ln_w, ln_b, e_ind, et_ind, w_out, x, g1b, n2g, sc2b, sh2b, router_w):
    c = RWKV_DIM
    rows = lambda n: pl.BlockSpec((BLK, n), lambda i: (i, 0))
    full = lambda a: pl.BlockSpec(a.shape, lambda i: (0,) * a.ndim)
    lead = [ys, z, yf, yb, bonus, gate, ssd_g, ln_w, ln_b, e_ind, et_ind]
    specs = [rows(SSD_INNER), rows(SSD_INNER), rows(c), rows(c), rows(c), rows(c),
             full(ssd_g), full(ln_w), full(ln_b), full(e_ind), full(et_ind)]
    return _mix_out_call(_l0_out_kernel, "l0_out", lead, specs, w_out, x, g1b, n2g, sc2b, sh2b, router_w)


def l1_out(a, w_out, x, g1b, n2g, sc2b, sh2b, router_w):
    specs = [pl.BlockSpec((BLK, a.shape[1]), lambda i: (i, 0))]
    return _mix_out_call(_l1_out_kernel, "l1_out", [a], specs, w_out, x, g1b, n2g, sc2b, sh2b, router_w)


RET_HPS = 2


def _l1_in_kernel(tbl_ref, x_ref, g_ref, sc_ref, sh_ref, w_ref, cos_ref, sin_ref, q_ref, k_ref, v_ref, gg_ref):
    f32 = jnp.float32
    hn = _adaln(x_ref[...], g_ref[...], sc_ref[0], sh_ref[0]).astype(jnp.bfloat16)
    cosf, sinf = cos_ref[0], sin_ref[0]

    def rope(x):
        parts = []
        for h in range(RET_HEADS):
            xh = x[:, h * RET_DK:(h + 1) * RET_DK]
            parts.append(xh * cosf + pltpu.roll(xh, RET_DK // 2, 1) * sinf)
        return jnp.concatenate(parts, axis=1)

    q_ref[...] = rope(jnp.dot(hn, w_ref[:, :RET_QK], preferred_element_type=f32)).astype(q_ref.dtype)
    k_ref[...] = (rope(jnp.dot(hn, w_ref[:, RET_QK:2 * RET_QK], preferred_element_type=f32))
                  * (RET_DK ** -0.5)).astype(k_ref.dtype)
    v_ref[...] = jnp.dot(hn, w_ref[:, 2 * RET_QK:2 * RET_QK + RET_V], preferred_element_type=f32).astype(v_ref.dtype)
    gg_ref[...] = jnp.dot(hn, w_ref[:, 2 * RET_QK + RET_V:], preferred_element_type=f32).astype(gg_ref.dtype)


def l1_in(x, g1, scb, shb, w_bf16, cos_t, sin_t, rope_blk):
    R, D = x.shape
    nblk = R // BLK
    tbl = jnp.asarray([rope_blk], jnp.int32)
    f32 = jnp.float32
    row = lambda n: pl.BlockSpec((BLK, n), lambda i, t: (i, 0))
    full = lambda a: pl.BlockSpec(a.shape, lambda i, t: (0,) * a.ndim)
    blkrow = pl.BlockSpec((1, 1, D), lambda i, t: (i, 0, 0))
    ropespec = pl.BlockSpec((1, BLK, RET_DK), lambda i, t: (t[0, i], 0, 0))
    return pl.pallas_call(
        _l1_in_kernel,
        grid_spec=pltpu.PrefetchScalarGridSpec(
            num_scalar_prefetch=1, grid=(nblk,),
            in_specs=[row(D), full(g1), blkrow, blkrow, full(w_bf16), ropespec, ropespec],
            out_specs=[row(RET_QK), row(RET_QK), row(RET_V), row(RET_V)]),
        out_shape=[jax.ShapeDtypeStruct((R, RET_QK), ACT_DTYPE), jax.ShapeDtypeStruct((R, RET_QK), ACT_DTYPE),
                   jax.ShapeDtypeStruct((R, RET_V), ACT_DTYPE), jax.ShapeDtypeStruct((R, RET_V), ACT_DTYPE)],
        compiler_params=pltpu.CompilerParams(dimension_semantics=("arbitrary",), vmem_limit_bytes=L0_VMEM_BYTES),
        name="l1_in",
    )(tbl, x, g1, scb, shb, w_bf16, cos_t, sin_t)


def _ret_kernel(tbl_ref, q_ref, k_ref, v_ref, g_ref, lg_ref, nw_ref, nb_ref, s0_ref, a_ref, fs_ref, st_ref):
    f32, bf16 = jnp.float32, jnp.bfloat16
    sb = pl.program_id(0)
    nch = q_ref.shape[0] // BLK
    heads = range(RET_HPS)
    ks = [slice(h * RET_DK, (h + 1) * RET_DK) for h in heads]
    vs = [slice(h * RET_DV, (h + 1) * RET_DV) for h in heads]
    lgf = [lg_ref[h, 0:1, 0:1] for h in heads]
    lgb = [lg_ref[h, 1:2, 0:1] for h in heads]
    t_i = lax.broadcasted_iota(jnp.int32, (BLK, BLK), 0)
    s_i = lax.broadcasted_iota(jnp.int32, (BLK, BLK), 1)
    dist = (t_i - s_i).astype(f32)
    dm = [jnp.exp(jnp.where(s_i <= t_i, dist * lgf[h], NEG_BIG)) + jnp.exp(jnp.where(s_i >= t_i, -dist * lgb[h], NEG_BIG))
          for h in heads]
    tk = lax.broadcasted_iota(jnp.int32, (BLK, RET_DK), 0).astype(f32)
    tv = lax.broadcasted_iota(jnp.int32, (BLK, RET_DV), 0).astype(f32)
    k_to_end_f = [jnp.exp((BLK - 1.0 - tk) * lgf[h]) for h in heads]
    k_to_end_b = [jnp.exp(tk * lgb[h]) for h in heads]
    from_start_f = [jnp.exp((tv + 1.0) * lgf[h]) for h in heads]
    from_start_b = [jnp.exp((BLK - tv) * lgb[h]) for h in heads]
    nt, tn = (((1,), (1,)), ((), ())), (((0,), (0,)), ((), ()))

    def chunk_rows(c):
        return pl.ds(pl.multiple_of(c * BLK, BLK), BLK)

    def fwd(c, carry):
        blk = sb * nch + c
        rows = chunk_rows(c)

        @pl.when(tbl_ref[0, blk] == 1)
        def _():
            st_ref[0] = s0_ref[0, 0]

        q = [q_ref[rows, ks[h]].astype(bf16) for h in heads]
        k = [k_ref[rows, ks[h]] for h in heads]
        v = [v_ref[rows, vs[h]].astype(bf16) for h in heads]
        s_in = [st_ref[0, h] for h in heads]
        g = [lax.dot_general(q[h], k[h].astype(bf16), nt, preferred_element_type=f32) for h in heads]
        y_diag = [jnp.dot((g[h] * dm[h]).astype(bf16), v[h], preferred_element_type=f32) for h in heads]
        y_off = [jnp.dot(q[h], s_in[h].astype(bf16), preferred_element_type=f32) for h in heads]
        kw = [(k[h].astype(f32) * k_to_end_f[h]).astype(bf16) for h in heads]
        upd = [lax.dot_general(kw[h], v[h], tn, preferred_element_type=f32) for h in heads]
        for h in heads:
            a_ref[rows, vs[h]] = y_diag[h] + from_start_f[h] * y_off[h]
            st_ref[0, h] = jnp.exp(BLK * lgf[h]) * s_in[h] + upd[h]
        fs_ref[c, 0] = st_ref[0]
        return carry

    lax.fori_loop(0, nch, fwd, 0)

    def bwd(j, carry):
        c = nch - 1 - j
        blk = sb * nch + c
        rows = chunk_rows(c)

        @pl.when(tbl_ref[1, blk] == 1)
        def _():
            st_ref[1] = s0_ref[0, 1]

        q = [q_ref[rows, ks[h]].astype(bf16) for h in heads]
        v = [v_ref[rows, vs[h]].astype(bf16) for h in heads]
        s_in = [st_ref[1, h] for h in heads]
        y_off = [jnp.dot(q[h], s_in[h].astype(bf16), preferred_element_type=f32) for h in heads]
        kw = [(k_ref[rows, ks[h]].astype(f32) * k_to_end_b[h]).astype(bf16) for h in heads]
        upd = [lax.dot_general(kw[h], v[h], tn, preferred_element_type=f32) for h in heads]
        for h in heads:
            st_ref[1, h] = jnp.exp(BLK * lgb[h]) * s_in[h] + upd[h]
            y = a_ref[rows, vs[h]] + from_start_b[h] * y_off[h]
            mu = jnp.mean(y, -1, keepdims=True)
            yc = y - mu
            var = jnp.mean(yc * yc, -1, keepdims=True)
            gg = g_ref[rows, vs[h]].astype(f32)
            a_ref[rows, vs[h]] = ((yc * lax.rsqrt(var + 1e-5) * nw_ref[:, vs[h]] + nb_ref[:, vs[h]])
                                  * (gg * jax.nn.sigmoid(gg)))
        fs_ref[c, 1] = st_ref[1]
        return carry

    lax.fori_loop(0, nch, bwd, 0)


def ret_scan(q, k, v, g, lg_tab, norm_w, norm_b, s0, first, last, sb_rows):
    R = q.shape[0]
    n_sb = R // sb_rows
    nch = sb_rows // BLK
    tbl = jnp.asarray([first, last], jnp.int32)
    f32 = jnp.float32
    hps = RET_HPS
    return pl.pallas_call(
        _ret_kernel,
        grid_spec=pltpu.PrefetchScalarGridSpec(
            num_scalar_prefetch=1, grid=(n_sb, RET_HEADS // hps),
            in_specs=[pl.BlockSpec((sb_rows, hps * RET_DK), lambda s, h, t: (s, h)),
                      pl.BlockSpec((sb_rows, hps * RET_DK), lambda s, h, t: (s, h)),
                      pl.BlockSpec((sb_rows, hps * RET_DV), lambda s, h, t: (s, h)),
                      pl.BlockSpec((sb_rows, hps * RET_DV), lambda s, h, t: (s, h)),
                      pl.BlockSpec((hps, 8, 128), lambda s, h, t: (h, 0, 0)),
                      pl.BlockSpec((1, hps * RET_DV), lambda s, h, t: (0, h)),
                      pl.BlockSpec((1, hps * RET_DV), lambda s, h, t: (0, h)),
                      pl.BlockSpec((1, 2, hps, RET_DK, RET_DV), lambda s, h, t: (s, 0, h, 0, 0))],
            out_specs=[pl.BlockSpec((sb_rows, hps * RET_DV), lambda s, h, t: (s, h)),
                       pl.BlockSpec((nch, 2, hps, RET_DK, RET_DV), lambda s, h, t: (s, 0, h, 0, 0))],
            scratch_shapes=[pltpu.VMEM((2, hps, RET_DK, RET_DV), f32)]),
        out_shape=[jax.ShapeDtypeStruct((R, RET_V), f32),
                   jax.ShapeDtypeStruct((R // BLK, 2, RET_HEADS, RET_DK, RET_DV), f32)],
        compiler_params=pltpu.CompilerParams(dimension_semantics=("arbitrary", "arbitrary"),
                                             vmem_limit_bytes=SSD_VMEM_BYTES),
        name="ret_scan",
    )(tbl, q, k, v, g, lg_tab, norm_w.reshape(1, -1), norm_b.reshape(1, -1), s0)


def _mod_kernel(c_ref, w_ref, b_ref, o_ref):
    c = c_ref[...]
    act = c * jax.nn.sigmoid(c)
    o_ref[0] = _dot(act, w_ref[0], ((1,), (0,)), 3) + b_ref[0]


def mod_vectors(conds, mod_w, mod_b):
    depth, D, n6 = mod_w.shape
    tn = D
    return pl.pallas_call(
        _mod_kernel,
        grid=(depth, n6 // tn),
        in_specs=[pl.BlockSpec(conds.shape, lambda i, j: (0, 0)),
                  pl.BlockSpec((1, D, tn), lambda i, j: (i, 0, j)),
                  pl.BlockSpec((1, 1, tn), lambda i, j: (i, 0, j))],
        out_specs=pl.BlockSpec((1, conds.shape[0], tn), lambda i, j: (i, 0, j)),
        out_shape=jax.ShapeDtypeStruct((depth, conds.shape[0], n6), jnp.float32),
        name="mod_vectors",
    )(conds, mod_w, mod_b.reshape(depth, 1, n6))


def rope_tables(n_tokens):
    rows = n_tokens // GRID_W
    row = jnp.repeat(jnp.arange(rows), GRID_W).astype(jnp.float32)
    col = jnp.tile(jnp.arange(GRID_W), rows).astype(jnp.float32)
    n_f = RET_DK // 4
    inv = ROPE_BASE ** (-jnp.arange(n_f, dtype=jnp.float32) / n_f)
    ang = jnp.concatenate([row[:, None] * inv, col[:, None] * inv], -1)
    return jnp.cos(ang), jnp.sin(ang)


def _rope_block_tables(n_ctx, n_lat, l_lat):
    nb = l_lat // BLK
    cos, sin = rope_tables(l_lat)
    cosf = jnp.concatenate([cos, cos], -1).reshape(nb, BLK, RET_DK)
    sinf = jnp.concatenate([-sin, sin], -1).reshape(nb, BLK, RET_DK)
    cos_t = jnp.concatenate([jnp.ones((1, BLK, RET_DK), jnp.float32), cosf])
    sin_t = jnp.concatenate([jnp.zeros((1, BLK, RET_DK), jnp.float32), sinf])
    rope_blk = [0] * n_ctx + [1 + j for _ in range(n_lat) for j in range(nb)]
    return cos_t, sin_t, rope_blk


def kernel(x_prompt, x_sample, state_ssd, state_rwkv, state_ret, c, c_ctx, mod_w, mod_b, norm1_g, norm2_g,
           router_w, exp_w_gate, exp_w_up, exp_w_down, ab_w_in, ab_w_out, ssd_conv_w, ssd_conv_b, ssd_dt_bias,
           ssd_a_log, ssd_d, ssd_norm_g, rwkv_mu_prev, rwkv_mu_next, rwkv_w0, rwkv_w2, rwkv_a0, rwkv_a2, rwkv_g2,
           rwkv_k_k, rwkv_k_a, rwkv_r_k, rwkv_ln_w, rwkv_ln_b, ret_w_in, ret_w_out, ret_decay_logit, ret_norm_w,
           ret_norm_b, final_norm_g):
    p = dict(mod_w=mod_w, mod_b=mod_b, norm1_g=norm1_g, norm2_g=norm2_g, router_w=router_w,
             exp_w_gate=exp_w_gate, exp_w_up=exp_w_up, exp_w_down=exp_w_down, ab_w_in=ab_w_in, ab_w_out=ab_w_out,
             ssd_conv_w=ssd_conv_w, ssd_conv_b=ssd_conv_b, ssd_dt_bias=ssd_dt_bias, ssd_a_log=ssd_a_log,
             ssd_d=ssd_d, ssd_norm_g=ssd_norm_g, rwkv_mu_prev=rwkv_mu_prev, rwkv_mu_next=rwkv_mu_next,
             rwkv_w0=rwkv_w0, rwkv_w2=rwkv_w2, rwkv_a0=rwkv_a0, rwkv_a2=rwkv_a2, rwkv_g2=rwkv_g2,
             rwkv_k_k=rwkv_k_k, rwkv_k_a=rwkv_k_a, rwkv_r_k=rwkv_r_k, rwkv_ln_w=rwkv_ln_w, rwkv_ln_b=rwkv_ln_b,
             ret_w_in=ret_w_in, ret_w_out=ret_w_out, ret_decay_logit=ret_decay_logit, ret_norm_w=ret_norm_w,
             ret_norm_b=ret_norm_b, final_norm_g=final_norm_g)
    f32, bf16 = jnp.float32, jnp.bfloat16
    n_ctx, l_ctx, D = x_prompt.shape
    n_lat, l_lat, _ = x_sample.shape
    assert l_ctx == BLK and l_lat % BLK == 0 and (n_ctx * BLK) % l_lat == 0
    n_sb_ctx = n_ctx * BLK // l_lat
    cond_id, first, last = _seq_tables(n_ctx, n_lat, l_lat)
    x = jnp.concatenate([x_prompt.reshape(-1, D), x_sample.reshape(-1, D)])

    conds = jnp.concatenate([c_ctx[None, :], c, jnp.zeros((8 - 1 - n_lat, D), f32)])
    mods = mod_vectors(conds, mod_w, mod_b)[:, jnp.asarray(cond_id)]
    mods = mods.reshape(DEPTH, len(cond_id), 6, 1, D)

    def with_ctx_zeros(state, n_zero):
        return jnp.concatenate([jnp.zeros((n_zero,) + state.shape[1:], f32), state])

    new_ssd, new_rwkv, new_ret = [], [], []
    out = None
    for i in range(DEPTH):
        sh1, sc1, g1, sh2, sc2, g2 = (mods[i, :, k] for k in range(6))
        e = i // 2
        if i % 2 == 0:
            w_packed, mup, mun, rwp, w2bd, a2p, g2p, e_ind, et_ind = l0_pack_weights(p, e)
            z, xbc, dt, r, v, an, lw, kd, bv, gate, bonus = l0_in(
                x, norm1_g[i][None], sc1, sh1, w_packed, mup, mun, rwp, w2bd, a2p, g2p, e_ind, et_ind, first, last)
            sel, hp = ssd_tables(p, e)
            s0_ssd = jnp.transpose(state_ssd[:, e], (0, 1, 3, 2, 4)).reshape(n_lat, 2, SSD_N, SSD_INNER)
            ys, fs_ssd = ssd_scan(xbc, dt, ssd_conv_w[e], ssd_conv_b[e], sel, hp,
                                  with_ctx_zeros(s0_ssd, n_sb_ctx), first, last, l_lat)
            new_ssd.append(jnp.transpose(fs_ssd[:n_ctx].reshape(n_ctx, 2, SSD_N, SSD_HEADS, SSD_P), (0, 1, 3, 2, 4)))
            s0_rwkv = jnp.transpose(state_rwkv[:, e], (0, 1, 4, 2, 3)).reshape(n_lat, 2, RWKV_N, RWKV_DIM)
            yf, yb, sf_rwkv = rwkv_scan_pallas(r, v, an, lw, kd, bv, with_ctx_zeros(s0_rwkv, n_ctx),
                                               _rwkv_steps(n_ctx, n_lat, l_lat))
            new_rwkv.append(jnp.transpose(sf_rwkv[:n_ctx].reshape(n_ctx, 2, RWKV_N, RWKV_HEADS, RWKV_N),
                                          (0, 1, 3, 4, 2)))
            x, hn2, affT = l0_out(ys, z, yf, yb, bonus, gate, ssd_norm_g[e][None], rwkv_ln_w[e][None], rwkv_ln_b[e][None],
                                  e_ind, et_ind, ab_w_out[e].astype(bf16), x, g1, norm2_g[i][None], sc2, sh2,
                                  router_w[i])
        else:
            cos_t, sin_t, rope_blk = _rope_block_tables(n_ctx, n_lat, l_lat)
            q, k, v, gg = l1_in(x, norm1_g[i][None], sc1, sh1, ret_w_in[e].astype(bf16), cos_t, sin_t, rope_blk)
            lg = jax.nn.log_sigmoid(ret_decay_logit[e].astype(f32))
            lg_tab = jnp.zeros((RET_HEADS, 8, 128), f32).at[:, :2, :].set(jnp.transpose(lg)[:, :, None])
            a, fs_ret = ret_scan(q, k, v, gg, lg_tab, ret_norm_w[e], ret_norm_b[e],
                                 with_ctx_zeros(state_ret[:, e], n_sb_ctx), first, last, l_lat)
            new_ret.append(fs_ret[:n_ctx])
            x, hn2, affT = l1_out(a, ret_w_out[e].astype(bf16), x, g1, norm2_g[i][None], sc2, sh2, router_w[i])
        fin = final_norm_g if i == DEPTH - 1 else None
        out = moe_layer(x, hn2, affT, g2, exp_w_gate, exp_w_up, exp_w_down, i, n_ctx, l_lat, final_g=fin)
        if fin is None:
            x = out
    y_ctx, y_lat = out
    return (y_ctx.reshape(n_ctx, l_ctx, D), y_lat.reshape(n_lat, l_lat, D),
            jnp.stack(new_ssd, 1), jnp.stack(new_rwkv, 1), jnp.stack(new_ret, 1))
```

```python
import functools
import math

import jax
import jax.numpy as jnp
from jax import lax
from jax.experimental import pallas as pl
from jax.experimental.pallas import tpu as pltpu

D_MODEL = 1024
DEPTH = 2
GRID_W = 64
CHUNK = 128
NORM_EPS = 1e-6
SSD_HEADS = 16
SSD_P = 64
SSD_INNER = SSD_HEADS * SSD_P
SSD_GROUPS = 2
SSD_HPG = SSD_HEADS // SSD_GROUPS
SSD_N = 128
SSD_XBC = SSD_INNER + 2 * SSD_GROUPS * SSD_N
SSD_IN = SSD_INNER + SSD_XBC + SSD_HEADS
RWKV_HEADS = 16
RWKV_N = 64
RWKV_DIM = RWKV_HEADS * RWKV_N
W_LORA = 64
A_LORA = 64
G_LORA = 128
RWKV_GN_EPS = 64e-5
RET_HEADS = 8
RET_DK = 128
RET_DV = 256
RET_QK = RET_HEADS * RET_DK
RET_V = RET_HEADS * RET_DV
ROPE_BASE = 10000.0
N_EXPERTS = 16
EC_CAPACITY = 2

ACT_DTYPE = jnp.bfloat16

RWKV_C = 64
RWKV_GH = 4
RWKV_GL = RWKV_GH * RWKV_N
RWKV_DOUBLING_PASSES = (3, 3, 3, 2, 1, 1)


def _split_bf16(x):
    hi = x.astype(jnp.bfloat16)
    lo = (x - hi.astype(jnp.float32)).astype(jnp.bfloat16)
    return hi, lo


def _dot(a, b, dims, passes):
    f = functools.partial(lax.dot_general, dimension_numbers=(dims, ((), ())),
                          preferred_element_type=jnp.float32)
    if passes == 1:
        return f(a.astype(jnp.bfloat16), b.astype(jnp.bfloat16))
    ah, al = _split_bf16(a)
    bh, bl = _split_bf16(b)
    return f(ah, bh) + (f(ah, bl) + f(al, bh))


def _rwkv_chunk_kernel(tbl_ref, r0_ref, v0_ref, a0_ref, r1_ref, v1_ref, a1_ref, lw0_ref, k0_ref, b0_ref,
                       lw1_ref, k1_ref, b1_ref, s0_ref, y0_ref, y1_ref, sf_ref, h_ref, *, p_inv, p_oth, n_zero):
    C, N, GH, GL = RWKV_C, RWKV_N, RWKV_GH, RWKV_GL
    i = pl.program_id(0)
    f32, bf16 = jnp.float32, jnp.bfloat16

    @pl.when(tbl_ref[3, i] == 1)
    def _():
        h_ref[...] = jnp.where(tbl_ref[2, i] >= n_zero, s0_ref[0], 0.0)

    t_i = lax.broadcasted_iota(jnp.int32, (C, GL), 0)
    s_i = lax.broadcasted_iota(jnp.int32, (C, GL), 1) & (N - 1)
    eye = (s_i == t_i).astype(f32)
    row2 = lax.broadcasted_iota(jnp.int32, (2 * C, GL), 0)
    rel2 = (lax.broadcasted_iota(jnp.int32, (2 * C, GL), 1) & (N - 1)) - (row2 & (C - 1))
    incl2 = row2 // C
    mask2 = [rel2 - incl2 < 0, -rel2 - incl2 < 0]
    bh_r = lax.broadcasted_iota(jnp.int32, (GL, GL), 0) // N
    bh_c = lax.broadcasted_iota(jnp.int32, (GL, GL), 1) // N
    blk = bh_r == bh_c
    tt = lax.broadcasted_iota(jnp.int32, (C, C), 0)
    ss = lax.broadcasted_iota(jnp.int32, (C, C), 1)
    tri = [(ss <= tt).astype(bf16), (ss >= tt).astype(bf16)]

    def bd(x, passes):
        pieces = []
        for _ in range(2 if passes == 3 else 1):
            hi = x.astype(bf16)
            x = x - hi.astype(f32)
            pieces.append(jnp.where(blk, jnp.concatenate([hi] * GH, axis=0), jnp.zeros((), bf16)))
        return pieces

    def mm(l, x, passes, dims=((1,), (0,))):
        f = functools.partial(lax.dot_general, dimension_numbers=(dims, ((), ())), preferred_element_type=f32)
        xs = bd(x, passes)
        lh = l.astype(bf16)
        if passes == 1:
            return f(lh, xs[0])
        ll = (l - lh.astype(f32)).astype(bf16)
        m = l.shape[0]
        both = f(jnp.concatenate([lh, ll], axis=0), xs[0])
        if passes == 2:
            return both[:m] + both[m:]
        return both[:m] + (f(lh, xs[1]) + both[m:])

    nt = ((1,), (1,))
    refs = [(r0_ref, v0_ref, a0_ref, lw0_ref, k0_ref, b0_ref), (r1_ref, v1_ref, a1_ref, lw1_ref, k1_ref, b1_ref)]
    lw, r_t, a_t, b_t, k_t, v = [], [], [], [], [], []
    for d, (r_ref, v_ref, a_ref, lw_ref, k_ref, b_ref) in enumerate(refs):
        lwd = lw_ref[0]
        lw_hi, lw_lo = _split_bf16(lwd)
        cum = (jnp.dot(tri[d], lw_hi, preferred_element_type=f32) + jnp.dot(tri[d], lw_lo, preferred_element_type=f32))
        w_inv = jnp.exp(-cum)
        lw.append(lwd)
        r_t.append(r_ref[...].astype(f32) * jnp.exp(cum))
        a_t.append(a_ref[...].astype(f32) * jnp.exp(cum - lwd))
        b_t.append(b_ref[0].astype(f32) * w_inv)
        k_t.append(k_ref[0].astype(f32) * w_inv)
        v.append(v_ref[...].astype(f32))
    lane_head = lax.broadcasted_iota(jnp.int32, (N, GL), 1) // N

    chains = [(d, slice(g * GL, (g + 1) * GL)) for d in range(2) for g in range(RWKV_HEADS // GH)]
    each = lambda fn: [fn(j, d, sl) for j, (d, sl) in enumerate(chains)]
    bg = each(lambda j, d, sl: b_t[d][:, sl])
    kg = each(lambda j, d, sl: k_t[d][:, sl])
    vg = each(lambda j, d, sl: v[d][:, sl])
    h0 = each(lambda j, d, sl: h_ref[d, :, sl])
    ar = each(lambda j, d, sl: jnp.concatenate([a_t[d][:, sl], r_t[d][:, sl]], axis=0))
    m_b = each(lambda j, d, sl: jnp.where(mask2[d], mm(ar[j], bg[j], p_oth, nt), 0.0))
    m_k = each(lambda j, d, sl: jnp.where(mask2[d], mm(ar[j], kg[j], p_oth, nt), 0.0))
    p = each(lambda j, d, sl: mm(m_b[j][:C], m_b[j][:C], p_inv[0]))
    tmat = each(lambda j, d, sl: eye + m_b[j][:C])
    for lv in range(int(math.log2(C)) - 2):
        pt = each(lambda j, d, sl: mm(jnp.concatenate([p[j], tmat[j]], axis=0), p[j], p_inv[1 + lv]))
        p = each(lambda j, d, sl: pt[j][:C])
        tmat = each(lambda j, d, sl: tmat[j] + pt[j][C:])
    tmat = each(lambda j, d, sl: tmat[j] + mm(tmat[j], p[j], p_inv[-1]))
    ar_h = each(lambda j, d, sl: mm(ar[j], h0[j], p_oth, nt))
    mk_v = each(lambda j, d, sl: mm(m_k[j], vg[j], p_oth))
    u = each(lambda j, d, sl: mm(tmat[j], ar_h[j][:C] + mk_v[j][:C], p_oth))
    y = each(lambda j, d, sl: ar_h[j][C:] + mm(m_b[j][C:], u[j], p_oth) + mk_v[j][C:])
    full = each(lambda j, d, sl: _dot(jnp.concatenate([u[j], vg[j]], axis=0), jnp.concatenate([bg[j], kg[j]], axis=0),
                                      ((0,), (0,)), p_oth))
    y_refs = (y0_ref, y1_ref)
    for j, (d, sl) in enumerate(chains):
        y_refs[d][:, sl] = y[j].astype(y_refs[d].dtype)
        z = jnp.zeros((N, GL), f32)
        for hh in range(GH):
            z = z + jnp.where(lane_head == hh, full[j][hh * N:(hh + 1) * N], 0.0)
        w_tot = jnp.exp(jnp.sum(lw[d][:, sl], axis=0, keepdims=True))
        h_ref[d, :, sl] = w_tot * (h0[j] + z)

    @pl.when(tbl_ref[4, i] == 1)
    def _():
        sf_ref[0] = h_ref[...]


def _rwkv_steps(n_ctx, n_lat, l_lat):
    C = RWKV_C
    rows = []
    seqs = [(s, s * BLK, BLK) for s in range(n_ctx)] + [(n_ctx + s, n_ctx * BLK + s * l_lat, l_lat) for s in range(n_lat)]
    for sid, row0, length in seqs:
        nc = length // C
        for j in range(nc):
            rows.append((row0 // C + j, row0 // C + nc - 1 - j, sid, int(j == 0), int(j == nc - 1)))
    return [list(col) for col in zip(*rows)]


def rwkv_scan_pallas(r, v, a, lw, k, b, s0, n_zero, steps, p_inv=RWKV_DOUBLING_PASSES, p_oth=1):
    R, HN = r.shape
    C, N = RWKV_C, RWKV_N
    tbl = jnp.asarray(steps, jnp.int32)
    fwd = pl.BlockSpec((C, HN), lambda i, t: (t[0, i], 0))
    bwd = pl.BlockSpec((C, HN), lambda i, t: (t[1, i], 0))
    fwd_d = pl.BlockSpec((1, C, HN), lambda i, t: (0, t[0, i], 0))
    bwd_d = pl.BlockSpec((1, C, HN), lambda i, t: (1, t[1, i], 0))
    n_seq = max(steps[2]) + 1
    st_in = pl.BlockSpec((1, 2, N, HN), lambda i, t: (jnp.maximum(t[2, i] - n_zero, 0), 0, 0, 0))
    st = pl.BlockSpec((1, 2, N, HN), lambda i, t: (t[2, i], 0, 0, 0))
    return pl.pallas_call(
        functools.partial(_rwkv_chunk_kernel, p_inv=p_inv, p_oth=p_oth, n_zero=n_zero),
        grid_spec=pltpu.PrefetchScalarGridSpec(
            num_scalar_prefetch=1,
            grid=(len(steps[0]),),
            in_specs=[fwd, fwd, fwd, bwd, bwd, bwd, fwd_d, fwd_d, fwd_d, bwd_d, bwd_d, bwd_d, st_in],
            out_specs=[fwd, bwd, st],
            scratch_shapes=[pltpu.VMEM((2, N, HN), jnp.float32)]),
        out_shape=[jax.ShapeDtypeStruct((R, HN), ACT_DTYPE), jax.ShapeDtypeStruct((R, HN), ACT_DTYPE),
                   jax.ShapeDtypeStruct((n_seq, 2, N, HN), jnp.float32)],
        compiler_params=pltpu.CompilerParams(dimension_semantics=("arbitrary",)),
        name="rwkv_scan",
    )(tbl, r, v, a, r, v, a, lw, k, b, lw, k, b, s0)


BLK = 256
FF_TILE = 768
SELECT_TILE = 512
GATHER_ROWS = 512
SELECT_MIN_EXP = -1100.0
SELECT_BINADE_STEPS = 11
SELECT_MANTISSA_STEPS = 40
MOE_FFN_VMEM_BYTES = 48 * 1024 * 1024


def _moe_select_kernel(aff_ref, slot_ref, *, cap):
    a = aff_ref[...]
    E, T = a.shape
    f32 = jnp.float32

    def enough(piv):
        return jnp.sum(jnp.where(a >= piv, 1.0, 0.0), axis=1, keepdims=True) >= cap

    def binade(_, lohi):
        e_lo, e_hi = lohi
        mid = jnp.floor((e_lo + e_hi) * 0.5)
        ok = enough(jnp.exp2(mid))
        return jnp.where(ok, mid, e_lo), jnp.where(ok, e_hi, mid)

    e_lo, e_hi = lax.fori_loop(0, SELECT_BINADE_STEPS, binade,
                               (jnp.full((E, 1), SELECT_MIN_EXP, f32), jnp.full((E, 1), 1.0, f32)))

    def inside(_, lohi):
        lo, hi = lohi
        mid = lo + (hi - lo) * 0.5
        ok = enough(mid)
        return jnp.where(ok, mid, lo), jnp.where(ok, hi, mid)

    thr, _ = lax.fori_loop(0, SELECT_MANTISSA_STEPS, inside, (jnp.exp2(e_lo), jnp.exp2(e_hi)))
    gt = a > thr
    eq = a == thr
    need = cap - jnp.sum(jnp.where(gt, 1.0, 0.0), axis=1, keepdims=True)
    tw = min(T, SELECT_TILE)

    def prefix_count(mask):
        m = jnp.where(mask, 1.0, 0.0).astype(jnp.bfloat16)
        outs = []
        for j in range(T // tw):
            s_i = lax.broadcasted_iota(jnp.int32, (T, tw), 0)
            t_i = lax.broadcasted_iota(jnp.int32, (T, tw), 1) + j * tw
            before = jnp.where(s_i < t_i, 1.0, 0.0).astype(jnp.bfloat16)
            outs.append(jnp.dot(m, before, preferred_element_type=f32))
        return outs[0] if len(outs) == 1 else jnp.concatenate(outs, axis=1)

    sel = gt | (eq & (prefix_count(eq) < need))
    slot_ref[...] = jnp.where(sel, prefix_count(sel).astype(jnp.int32), -1)


def _moe_select(affT, row0, n_seq, t):
    E = affT.shape[0]
    rows = jnp.transpose(affT[:, row0:row0 + n_seq * t].reshape(E, n_seq, t), (1, 0, 2)).reshape(n_seq * E, t)
    slot = pl.pallas_call(
        functools.partial(_moe_select_kernel, cap=EC_CAPACITY * t // N_EXPERTS),
        grid=(1,),
        in_specs=[pl.BlockSpec((n_seq * E, t), lambda s: (0, 0))],
        out_specs=pl.BlockSpec((n_seq * E, t), lambda s: (0, 0)),
        out_shape=jax.ShapeDtypeStruct((n_seq * E, t), jnp.int32),
        name="moe_select",
    )(rows)
    return jnp.transpose(slot.reshape(n_seq, E, t), (1, 0, 2)).reshape(E, n_seq * t)


def _moe_gather_kernel(slot_ref, aff_ref, hn_ref, xe_ref, gate_ref, *, cap):
    slot = slot_ref[...]
    eg, _, T = slot.shape
    p_i = lax.broadcasted_iota(jnp.int32, (eg, cap, T), 1)
    hit = slot == p_i
    onehot = jnp.where(hit, 1.0, 0.0).reshape(eg * cap, T).astype(jnp.bfloat16)
    xe = jnp.dot(onehot, hn_ref[...], preferred_element_type=jnp.float32)
    xe_ref[...] = xe.reshape(eg, cap, xe.shape[1]).astype(xe_ref.dtype)
    g = jnp.sum(jnp.where(hit, aff_ref[...], 0.0), axis=2, keepdims=True)
    gate_ref[...] = jnp.broadcast_to(g, (eg, cap, 128))


def _moe_gather(slot3, aff3, hn, row0, n_seq, t):
    E = slot3.shape[0]
    D = hn.shape[1]
    cap = EC_CAPACITY * t // N_EXPERTS
    eg = min(E, GATHER_ROWS // cap)
    b0 = row0 // t
    return pl.pallas_call(
        functools.partial(_moe_gather_kernel, cap=cap),
        grid=(n_seq, E // eg),
        in_specs=[pl.BlockSpec((eg, 1, t), lambda s, e: (e, 0, b0 + s)),
                  pl.BlockSpec((eg, 1, t), lambda s, e: (e, 0, b0 + s)),
                  pl.BlockSpec((t, D), lambda s, e: (b0 + s, 0))],
        out_specs=[pl.BlockSpec((eg, cap, D), lambda s, e: (e, s, 0)),
                   pl.BlockSpec((eg, cap, 128), lambda s, e: (e, s, 0))],
        out_shape=[jax.ShapeDtypeStruct((E, n_seq * cap, D), jnp.bfloat16),
                   jax.ShapeDtypeStruct((E, n_seq * cap, 128), jnp.float32)],
        compiler_params=pltpu.CompilerParams(dimension_semantics=("arbitrary", "arbitrary"),
                                             vmem_limit_bytes=MOE_FFN_VMEM_BYTES),
        name="moe_gather",
    )(slot3, aff3, hn)


def _moe_ffn_kernel(xc_ref, xl_ref, gc_ref, gl_ref, wg_ref, wu_ref, wd_ref, yc_ref, yl_ref, accc_ref, accl_ref):
    f = pl.program_id(1)
    nf = pl.num_programs(1)
    bf16 = jnp.bfloat16
    wg = wg_ref[0, 0].astype(bf16)
    wu = wu_ref[0, 0].astype(bf16)
    wd = wd_ref[0, 0].astype(bf16)

    def part(x_ref, acc_ref):
        x = x_ref[0]
        g = jnp.dot(x, wg, preferred_element_type=jnp.float32)
        u = jnp.dot(x, wu, preferred_element_type=jnp.float32)
        h = (g * jax.nn.sigmoid(g) * u).astype(bf16)
        y = jnp.dot(h, wd, preferred_element_type=jnp.float32)

        @pl.when(f == 0)
        def _():
            acc_ref[...] = y

        @pl.when(f != 0)
        def _():
            acc_ref[...] += y

    part(xc_ref, accc_ref)
    part(xl_ref, accl_ref)

    @pl.when(f == nf - 1)
    def _():
        for acc_ref, g_ref, y_ref in ((accc_ref, gc_ref, yc_ref), (accl_ref, gl_ref, yl_ref)):
            gate = jnp.concatenate([g_ref[0]] * (acc_ref.shape[1] // 128), axis=1)
            y_ref[0] = (acc_ref[...] * gate).astype(y_ref.dtype)


def _moe_ffn(xc, xl, gc, gl, wg, wu, wd, layer):
    E, nc_rows, D = xc.shape
    nl_rows = xl.shape[1]
    F = wg.shape[3]
    nf = F // FF_TILE
    return pl.pallas_call(
        _moe_ffn_kernel,
        grid=(E, nf),
        in_specs=[pl.BlockSpec((1, nc_rows, D), lambda e, f: (e, 0, 0)),
                  pl.BlockSpec((1, nl_rows, D), lambda e, f: (e, 0, 0)),
                  pl.BlockSpec((1, nc_rows, 128), lambda e, f: (e, 0, 0)),
                  pl.BlockSpec((1, nl_rows, 128), lambda e, f: (e, 0, 0)),
                  pl.BlockSpec((1, 1, D, FF_TILE), lambda e, f: (layer, e, 0, f)),
                  pl.BlockSpec((1, 1, D, FF_TILE), lambda e, f: (layer, e, 0, f)),
                  pl.BlockSpec((1, 1, FF_TILE, D), lambda e, f: (layer, e, f, 0))],
        out_specs=[pl.BlockSpec((1, nc_rows, D), lambda e, f: (e, 0, 0)),
                   pl.BlockSpec((1, nl_rows, D), lambda e, f: (e, 0, 0))],
        out_shape=[jax.ShapeDtypeStruct((E, nc_rows, D), jnp.bfloat16),
                   jax.ShapeDtypeStruct((E, nl_rows, D), jnp.bfloat16)],
        scratch_shapes=[pltpu.VMEM((nc_rows, D), jnp.float32), pltpu.VMEM((nl_rows, D), jnp.float32)],
        compiler_params=pltpu.CompilerParams(dimension_semantics=("arbitrary", "arbitrary"),
                                             vmem_limit_bytes=MOE_FFN_VMEM_BYTES),
        name="moe_ffn",
    )(xc, xl, gc, gl, wg, wu, wd)


def _moe_scatter_kernel(slot_ref, ye_ref, x_ref, g2_ref, *rest, cap, final):
    if final:
        fg_ref, o_ref = rest
    else:
        (o_ref,) = rest
    E = ye_ref.shape[0]
    D = ye_ref.shape[2]
    tb = x_ref.shape[0]
    slot = slot_ref[...]
    p_i = lax.broadcasted_iota(jnp.int32, (E, cap, tb), 1)
    onehot = jnp.where(slot == p_i, 1.0, 0.0).reshape(E * cap, tb).astype(jnp.bfloat16)
    ye = ye_ref[...].reshape(E * cap, D)
    moe = lax.dot_general(onehot, ye, (((0,), (0,)), ((), ())), preferred_element_type=jnp.float32)
    x = x_ref[...] + g2_ref[0] * moe
    if final:
        x = x * lax.rsqrt(jnp.mean(x * x, -1, keepdims=True) + NORM_EPS) * fg_ref[...]
    o_ref[...] = x


def _moe_scatter(slot3, ye, x, g2blk, row0, n_seq, t, final_g=None):
    E, _, D = ye.shape
    cap = EC_CAPACITY * t // N_EXPERTS
    nb = t // BLK
    b0 = row0 // BLK
    final = final_g is not None
    in_specs = [pl.BlockSpec((E, 1, BLK), lambda s, j: (0, 0, b0 + s * nb + j)),
                pl.BlockSpec((E, cap, D), lambda s, j: (0, s, 0)),
                pl.BlockSpec((BLK, D), lambda s, j: (b0 + s * nb + j, 0)),
                pl.BlockSpec((1, 1, D), lambda s, j: (b0 + s * nb + j, 0, 0))]
    args = [slot3, ye, x, g2blk]
    if final:
        in_specs.append(pl.BlockSpec((1, D), lambda s, j: (0, 0)))
        args.append(final_g.reshape(1, D))
        out_specs = pl.BlockSpec((BLK, D), lambda s, j: (s * nb + j, 0))
        out_shape = jax.ShapeDtypeStruct((n_seq * t, D), jnp.float32)
        aliases = {}
    else:
        out_specs = pl.BlockSpec((BLK, D), lambda s, j: (b0 + s * nb + j, 0))
        out_shape = jax.ShapeDtypeStruct(x.shape, jnp.float32)
        aliases = {2: 0}
    return pl.pallas_call(
        functools.partial(_moe_scatter_kernel, cap=cap, final=final),
        grid=(n_seq, nb),
        in_specs=in_specs, out_specs=out_specs, out_shape=out_shape,
        input_output_aliases=aliases,
        compiler_params=pltpu.CompilerParams(dimension_semantics=("arbitrary", "arbitrary"),
                                             vmem_limit_bytes=MOE_FFN_VMEM_BYTES),
        name="moe_scatter",
    )(*args)


def moe_layer(x, hn, affT, g2blk, wg, wu, wd, layer, n_ctx, l_lat, final_g=None):
    R = x.shape[0]
    r_ctx = n_ctx * BLK
    n_lat = (R - r_ctx) // l_lat
    slot = jnp.concatenate([_moe_select(affT, 0, n_ctx, BLK), _moe_select(affT, r_ctx, n_lat, l_lat)], axis=1)
    slot3 = slot[:, None, :]
    aff3 = affT[:, None, :]
    xc, gc = _moe_gather(slot3, aff3, hn, 0, n_ctx, BLK)
    xl, gl = _moe_gather(slot3, aff3, hn, r_ctx, n_lat, l_lat)
    yc, yl = _moe_ffn(xc, xl, gc, gl, wg, wu, wd, layer)
    if final_g is None:
        x = _moe_scatter(slot3, yc, x, g2blk, 0, n_ctx, BLK)
        return _moe_scatter(slot3, yl, x, g2blk, r_ctx, n_lat, l_lat)
    return (_moe_scatter(slot3, yc, x, g2blk, 0, n_ctx, BLK, final_g),
            _moe_scatter(slot3, yl, x, g2blk, r_ctx, n_lat, l_lat, final_g))


L0_Z = (0, 1024)
L0_XBC = (1024, 2560)
L0_SHIFT = (2560, 6016)
L0_DT = (6016, 6144)
L0_COLS = 6144
L0_VMEM_BYTES = 56 * 1024 * 1024


def _sum_split(x, m, n_split):
    acc = None
    for _ in range(n_split):
        hi = x.astype(jnp.bfloat16)
        x = x - hi.astype(jnp.float32)
        t = jnp.dot(hi, m, preferred_element_type=jnp.float32)
        acc = t if acc is None else acc + t
    return acc


def _sum_split_left(m, x, n_split):
    acc = None
    for _ in range(n_split):
        hi = x.astype(jnp.bfloat16)
        x = x - hi.astype(jnp.float32)
        t = jnp.dot(m, hi, preferred_element_type=jnp.float32)
        acc = t if acc is None else acc + t
    return acc


def _head_sum(x, e_ref, et_ref):
    return _sum_split(_sum_split(x, e_ref[...], 2), et_ref[...], 2)


def _adaln(x, g, sc, sh):
    y = x * lax.rsqrt(jnp.mean(x * x, -1, keepdims=True) + NORM_EPS) * g
    return y * (1.0 + sc) + sh


def _softplus(x):
    return jnp.maximum(x, 0.0) + jnp.log(1.0 + jnp.exp(-jnp.abs(x)))


def _l0_in_kernel(tbl_ref, x_ref, xp_ref, xn_ref, g_ref, sc_ref, sh_ref, w_ref, mup_ref, mun_ref, rwp_ref,
                  w2_ref, a2_ref, g2_ref, e_ref, et_ref,
                  z_ref, xbc_ref, dt_ref, r_ref, v_ref, an_ref, lw_ref, kd_ref, bv_ref, gate_ref, bonus_ref):
    i = pl.program_id(0)
    f32, bf16 = jnp.float32, jnp.bfloat16
    g, sc, sh = g_ref[...], sc_ref[0], sh_ref[0]
    hn = _adaln(x_ref[...], g, sc, sh).astype(bf16)
    halo = _adaln(jnp.concatenate([xp_ref[...], xn_ref[...]], axis=0), g, sc, sh).astype(bf16)
    hn_halo = jnp.concatenate([hn, halo], axis=0)

    keep_prev = (1 - tbl_ref[0, i]).astype(f32)
    keep_next = (1 - tbl_ref[1, i]).astype(f32)
    row = lax.broadcasted_iota(jnp.int32, (BLK, 1), 0)
    c = RWKV_DIM

    def proj(c0, c1):
        return jnp.dot(hn_halo, w_ref[:, L0_SHIFT[0] + c0:L0_SHIFT[0] + c1], preferred_element_type=f32)

    def shift(both, c0, c1):
        cur = both[:BLK]
        prev = jnp.where(row == 0, both[BLK + 7:BLK + 8] * keep_prev, pltpu.roll(cur, 1, 0))
        nxt = jnp.where(row == BLK - 1, both[BLK + 8:BLK + 9] * keep_next, pltpu.roll(cur, BLK - 1, 0))
        return cur + mup_ref[:, c0:c1] * (prev - cur) + mun_ref[:, c0:c1] * (nxt - cur)

    lo = 3 * c + 2 * W_LORA
    p_wl = proj(3 * c, lo)
    p_ag = proj(lo, lo + 256)
    p_k = proj(c, 2 * c)
    wl = shift(p_wl, 3 * c, lo)
    ag = shift(p_ag, lo, lo + 256)
    p_r = proj(0, c)
    k = shift(p_k, c, 2 * c)
    k_k, k_a, r_k = rwp_ref[0:1], rwp_ref[1:2], rwp_ref[2:3]
    w_lin = jnp.dot(jnp.tanh(wl).astype(bf16), w2_ref[...], preferred_element_type=f32)
    a_lora = jnp.dot(ag.astype(bf16), a2_ref[...], preferred_element_type=f32)
    gate_ref[...] = jnp.dot(jax.nn.sigmoid(ag).astype(bf16), g2_ref[...],
                            preferred_element_type=f32).astype(gate_ref.dtype)
    p_v = proj(2 * c, 3 * c)
    r = shift(p_r, 0, c)
    r_ref[...] = r.astype(r_ref.dtype)
    kk = k * k_k
    kk = kk * lax.rsqrt(_head_sum(kk * kk, e_ref, et_ref) + 1e-12)
    an_ref[...] = (-kk).astype(an_ref.dtype)
    z_ref[...] = jnp.dot(hn, w_ref[:, L0_Z[0]:L0_Z[1]], preferred_element_type=f32).astype(z_ref.dtype)
    v = shift(p_v, 2 * c, 3 * c)
    v_ref[...] = v.astype(v_ref.dtype)
    kd_sum = None
    xbc_cols = (L0_XBC[0], (L0_XBC[0] + L0_XBC[1]) // 2, L0_XBC[1])
    for d in range(2):
        xbc_ref[:, xbc_cols[d] - L0_XBC[0]:xbc_cols[d + 1] - L0_XBC[0]] = jnp.dot(
            hn, w_ref[:, xbc_cols[d]:xbc_cols[d + 1]], preferred_element_type=f32)
        w_log = -_softplus(-(rwp_ref[3 + d:4 + d] + w_lin[:, d * c:(d + 1) * c])) - 0.5
        lw_ref[d] = -jnp.exp(w_log)
        a = jax.nn.sigmoid(rwp_ref[5 + d:6 + d] + a_lora)
        kd = k * (1.0 + (a - 1.0) * k_a)
        kd_ref[d] = kd.astype(kd_ref.dtype)
        bv_ref[d] = (kk * a).astype(bv_ref.dtype)
        kd_sum = kd if kd_sum is None else kd_sum + kd
    dt_ref[...] = jnp.dot(hn, w_ref[:, L0_DT[0]:L0_DT[1]], preferred_element_type=f32)
    bonus_ref[...] = (_head_sum(r * kd_sum * r_k, e_ref, et_ref) * v).astype(bonus_ref.dtype)


def _seq_tables(n_ctx, n_lat, l_lat):
    nb = l_lat // BLK
    cond = [0] * n_ctx + [1 + s for s in range(n_lat) for _ in range(nb)]
    first = [1] * n_ctx + [1 if j == 0 else 0 for _ in range(n_lat) for j in range(nb)]
    last = [1] * n_ctx + [1 if j == nb - 1 else 0 for _ in range(n_lat) for j in range(nb)]
    return cond, first, last


def l0_in(x, g1, scb, shb, w_packed, mup, mun, rwp, w2bd, a2p, g2p, e_ind, et_ind, first, last):
    R, D = x.shape
    nblk = R // BLK
    n8 = R // 8
    tbl = jnp.asarray([first, last], jnp.int32)
    c = RWKV_DIM
    row = lambda i, t: (i, 0)
    full = lambda shape: pl.BlockSpec(shape, lambda i, t: (0,) * len(shape))
    rows = lambda n: pl.BlockSpec((BLK, n), row)
    rows2 = lambda n: pl.BlockSpec((2, BLK, n), lambda i, t: (0, i, 0))
    f32, act = jnp.float32, ACT_DTYPE
    sds = jax.ShapeDtypeStruct
    return pl.pallas_call(
        _l0_in_kernel,
        grid_spec=pltpu.PrefetchScalarGridSpec(
            num_scalar_prefetch=1,
            grid=(nblk,),
            in_specs=[rows(D),
                      pl.BlockSpec((8, D), lambda i, t: (jnp.maximum(i * (BLK // 8) - 1, 0), 0)),
                      pl.BlockSpec((8, D), lambda i, t: (jnp.minimum((i + 1) * (BLK // 8), n8 - 1), 0)),
                      full((1, D)),
                      pl.BlockSpec((1, 1, D), lambda i, t: (i, 0, 0)),
                      pl.BlockSpec((1, 1, D), lambda i, t: (i, 0, 0)),
                      full((D, L0_COLS)), full(mup.shape), full(mun.shape), full(rwp.shape),
                      full(w2bd.shape), full(a2p.shape), full(g2p.shape), full(e_ind.shape), full(et_ind.shape)],
            out_specs=[rows(c), rows(SSD_XBC), rows(128), rows(c), rows(c), rows(c),
                       rows2(c), rows2(c), rows2(c), rows(c), rows(c)]),
        out_shape=[sds((R, c), act), sds((R, SSD_XBC), f32), sds((R, 128), f32), sds((R, c), act), sds((R, c), act),
                   sds((R, c), act), sds((2, R, c), f32), sds((2, R, c), act), sds((2, R, c), act),
                   sds((R, c), act), sds((R, c), act)],
        compiler_params=pltpu.CompilerParams(dimension_semantics=("arbitrary",), vmem_limit_bytes=L0_VMEM_BYTES),
        name="l0_in",
    )(tbl, x, x, x, g1, scb, shb, w_packed, mup, mun, rwp, w2bd, a2p, g2p, e_ind, et_ind)


def l0_pack_weights(p, e):
    bf16 = jnp.bfloat16
    w = p['ab_w_in'][e]
    D = w.shape[0]
    c = RWKV_DIM
    rw0 = SSD_IN
    ag0 = rw0 + 3 * c + 2 * W_LORA
    w_packed = jnp.concatenate([
        w[:, :SSD_INNER + SSD_XBC], w[:, rw0:ag0], w[:, ag0:ag0 + A_LORA + G_LORA],
        jnp.zeros((D, 256 - A_LORA - G_LORA), w.dtype),
        w[:, SSD_INNER + SSD_XBC:SSD_IN], jnp.zeros((D, 128 - SSD_HEADS), w.dtype)], axis=1).astype(bf16)

    def pack_mu(mu):
        return jnp.concatenate([mu, jnp.zeros((256 - A_LORA - G_LORA,), mu.dtype)])[None, :]

    rwp = jnp.stack([p['rwkv_k_k'][e], p['rwkv_k_a'][e], p['rwkv_r_k'][e].reshape(-1), p['rwkv_w0'][e, 0],
                     p['rwkv_w0'][e, 1], p['rwkv_a0'][e, 0], p['rwkv_a0'][e, 1], jnp.zeros((c,), jnp.float32)])
    zw = jnp.zeros((W_LORA, c), jnp.float32)
    w2bd = jnp.concatenate([jnp.concatenate([p['rwkv_w2'][e, 0], zw], axis=1),
                            jnp.concatenate([zw, p['rwkv_w2'][e, 1]], axis=1)], axis=0).astype(bf16)
    a2p = jnp.concatenate([p['rwkv_a2'][e], jnp.zeros((256 - A_LORA, c), jnp.float32)], axis=0).astype(bf16)
    g2p = jnp.concatenate([jnp.zeros((A_LORA, c), jnp.float32), p['rwkv_g2'][e],
                           jnp.zeros((256 - A_LORA - G_LORA, c), jnp.float32)], axis=0).astype(bf16)
    head = jnp.arange(c) // RWKV_N
    e_ind = (head[:, None] == jnp.arange(128)[None, :]).astype(bf16)
    return w_packed, pack_mu(p['rwkv_mu_prev'][e]), pack_mu(p['rwkv_mu_next'][e]), rwp, w2bd, a2p, g2p, e_ind, e_ind.T


SSD_QH = 4
SSD_VMEM_BYTES = 48 * 1024 * 1024
NEG_BIG = -1e30
LOG2E = 1.4426950408889634


def _conv_silu(cur, prev_row, next_row, w_ref, b_ref):
    row = lax.broadcasted_iota(jnp.int32, (BLK, 1), 0)
    prev = jnp.where(row == 0, prev_row, pltpu.roll(cur, 1, 0))
    nxt = jnp.where(row == BLK - 1, next_row, pltpu.roll(cur, BLK - 1, 0))
    y = w_ref[0:1] * prev + w_ref[1:2] * cur + w_ref[2:3] * nxt + b_ref[...]
    return y * jax.nn.sigmoid(y)


def _ssd_kernel(tbl_ref, xs_ref, b_ref, c_ref, dt_ref, cwx_ref, cwb_ref, cwc_ref, cbx_ref, cbb_ref, cbc_ref,
                sel_ref, hp_ref, s0_ref, y_ref, fs_ref, xa_ref, ba_ref, ca_ref, sfx_ref, ldb_ref, st_ref, *, n_zero):
    f32, bf16 = jnp.float32, jnp.bfloat16
    sb = pl.program_id(0)
    sbr, qw = xs_ref.shape
    nch = sbr // BLK
    P, QH = SSD_P, SSD_QH
    t_i = lax.broadcasted_iota(jnp.int32, (BLK, BLK), 0)
    s_i = lax.broadcasted_iota(jnp.int32, (BLK, BLK), 1)
    lower = s_i <= t_i
    upper = s_i >= t_i
    tri_lo = jnp.where(lower, 1.0, 0.0).astype(bf16)
    tri_up = jnp.where(upper, 1.0, 0.0).astype(bf16)
    hp = hp_ref[0]
    sel = sel_ref[0]
    ind = jnp.where(lax.broadcasted_iota(jnp.int32, (128, qw), 1) // P == lax.broadcasted_iota(jnp.int32, (128, qw), 0),
                    1.0, 0.0).astype(bf16)
    head_of_lane = lax.broadcasted_iota(jnp.int32, (1, qw), 1) // P

    def expand(cols):
        return _sum_split(cols, ind, 2)

    d_row = expand(jnp.broadcast_to(hp[4:5], (8, 128)))[0:1]

    def chunk_rows(c):
        return pl.ds(pl.multiple_of(c * BLK, BLK), BLK)

    def neighbours(ref, c, keep_prev, keep_next):
        lo = jnp.maximum(c * BLK - 1, 0)
        hi = jnp.minimum((c + 1) * BLK, sbr - 1)
        return ref[pl.ds(lo, 1), :] * keep_prev, ref[pl.ds(hi, 1), :] * keep_next

    def fwd(c, carry):
        blk = sb * nch + c
        first, last = tbl_ref[0, blk], tbl_ref[1, blk]
        kp, kn = (1 - first).astype(f32), (1 - last).astype(f32)
        rows = chunk_rows(c)

        @pl.when(first == 1)
        def _():
            st_ref[0] = jnp.where(sb >= n_zero, s0_ref[0, 0], 0.0)

        xa = _conv_silu(xs_ref[rows, :], *neighbours(xs_ref, c, kp, kn), cwx_ref, cbx_ref)
        bm = _conv_silu(b_ref[rows, :], *neighbours(b_ref, c, kp, kn), cwb_ref, cbb_ref)
        cm = _conv_silu(c_ref[rows, :], *neighbours(c_ref, c, kp, kn), cwc_ref, cbc_ref)
        xb, bmb, cmb = xa.astype(bf16), bm.astype(bf16), cm.astype(bf16)
        xa_ref[rows, :] = xb
        ba_ref[rows, :] = bmb
        ca_ref[rows, :] = cmb
        dtq = _sum_split(dt_ref[rows, :], sel, 2)
        dtf = _softplus(dtq + hp[0:1])
        dtb = _softplus(dtq + hp[1:2])
        acs = _sum_split_left(tri_lo, dtf * hp[2:3], 3)
        sfx = _sum_split_left(tri_up, dtb * hp[3:4], 3)
        ldf, ldb = jnp.log(dtf), jnp.log(dtb)
        sfx_ref[rows, :] = sfx
        ldb_ref[rows, :] = ldb
        a2, s2 = acs * LOG2E, sfx * LOG2E
        a2r = (a2 - ldf * LOG2E).T
        s2r = (s2 - ldb * LOG2E).T
        g = lax.dot_general(cmb, bmb, (((1,), (1,)), ((), ())), preferred_element_type=f32)
        y_diag = None
        for j in range(QH):
            m = (g * (jnp.exp2(jnp.where(lower, a2[:, j:j + 1] - a2r[j:j + 1, :], NEG_BIG))
                      + jnp.exp2(jnp.where(upper, s2[:, j:j + 1] - s2r[j:j + 1, :], NEG_BIG)))).astype(bf16)
            xh = jnp.where(head_of_lane == j, xb, jnp.zeros((), bf16))
            t = jnp.dot(m, xh, preferred_element_type=f32)
            y_diag = t if y_diag is None else y_diag + t
        ea = jnp.exp(expand(acs))
        wf = jnp.exp(expand(acs[BLK - 1:BLK] - acs + ldf))
        s_in = st_ref[0]
        y_ref[rows, :] = xa * d_row + y_diag + ea * jnp.dot(cmb, s_in.astype(bf16), preferred_element_type=f32)
        st_ref[0] = ea[BLK - 1:BLK] * s_in + lax.dot_general(
            bmb, (xa * wf).astype(bf16), (((0,), (0,)), ((), ())), preferred_element_type=f32)
        fs_ref[c, 0] = st_ref[0]
        return carry

    lax.fori_loop(0, nch, fwd, 0)

    def bwd(k, carry):
        c = nch - 1 - k
        blk = sb * nch + c
        rows = chunk_rows(c)

        @pl.when(tbl_ref[1, blk] == 1)
        def _():
            st_ref[1] = jnp.where(sb >= n_zero, s0_ref[0, 1], 0.0)

        sfx = sfx_ref[rows, :]
        eb = jnp.exp(expand(sfx))
        wb = jnp.exp(expand(sfx[0:1] - sfx + ldb_ref[rows, :]))
        s_in = st_ref[1]
        y_ref[rows, :] += eb * jnp.dot(ca_ref[rows, :], s_in.astype(bf16), preferred_element_type=f32)
        st_ref[1] = eb[0:1] * s_in + lax.dot_general(
            ba_ref[rows, :], (xa_ref[rows, :].astype(f32) * wb).astype(bf16), (((0,), (0,)), ((), ())),
            preferred_element_type=f32)
        fs_ref[c, 1] = st_ref[1]
        return carry

    lax.fori_loop(0, nch, bwd, 0)


def ssd_scan(xbc, dt, conv_w, conv_b, sel, hp, s0, n_zero, first, last, sb_rows):
    R = xbc.shape[0]
    n_sb = R // sb_rows
    nch = sb_rows // BLK
    nq = SSD_HEADS // SSD_QH
    qw = SSD_QH * SSD_P
    qpg = SSD_HPG // SSD_QH
    b_blk = SSD_INNER // SSD_N
    c_blk = b_blk + SSD_GROUPS
    tbl = jnp.asarray([first, last], jnp.int32)
    cw = conv_w
    cb = conv_b.reshape(1, -1)
    f32, bf16 = jnp.float32, jnp.bfloat16
    return pl.pallas_call(
        functools.partial(_ssd_kernel, n_zero=n_zero),
        grid_spec=pltpu.PrefetchScalarGridSpec(
            num_scalar_prefetch=1,
            grid=(n_sb, nq),
            in_specs=[pl.BlockSpec((sb_rows, qw), lambda s, q, t: (s, q)),
                      pl.BlockSpec((sb_rows, SSD_N), lambda s, q, t: (s, b_blk + q // qpg)),
                      pl.BlockSpec((sb_rows, SSD_N), lambda s, q, t: (s, c_blk + q // qpg)),
                      pl.BlockSpec((sb_rows, 128), lambda s, q, t: (s, 0)),
                      pl.BlockSpec((3, qw), lambda s, q, t: (0, q)),
                      pl.BlockSpec((3, SSD_N), lambda s, q, t: (0, b_blk + q // qpg)),
                      pl.BlockSpec((3, SSD_N), lambda s, q, t: (0, c_blk + q // qpg)),
                      pl.BlockSpec((1, qw), lambda s, q, t: (0, q)),
                      pl.BlockSpec((1, SSD_N), lambda s, q, t: (0, b_blk + q // qpg)),
                      pl.BlockSpec((1, SSD_N), lambda s, q, t: (0, c_blk + q // qpg)),
                      pl.BlockSpec((1, 128, 128), lambda s, q, t: (q, 0, 0)),
                      pl.BlockSpec((1, 8, 128), lambda s, q, t: (q, 0, 0)),
                      pl.BlockSpec((1, 2, SSD_N, qw), lambda s, q, t: (jnp.maximum(s - n_zero, 0), 0, 0, q))],
            out_specs=[pl.BlockSpec((sb_rows, qw), lambda s, q, t: (s, q)),
                       pl.BlockSpec((nch, 2, SSD_N, qw), lambda s, q, t: (s, 0, 0, q))],
            scratch_shapes=[pltpu.VMEM((sb_rows, qw), bf16), pltpu.VMEM((sb_rows, SSD_N), bf16),
                            pltpu.VMEM((sb_rows, SSD_N), bf16), pltpu.VMEM((sb_rows, 128), f32),
                            pltpu.VMEM((sb_rows, 128), f32), pltpu.VMEM((2, SSD_N, qw), f32)]),
        out_shape=[jax.ShapeDtypeStruct((R, SSD_INNER), f32),
                   jax.ShapeDtypeStruct((R // BLK, 2, SSD_N, SSD_INNER), f32)],
        compiler_params=pltpu.CompilerParams(dimension_semantics=("arbitrary", "arbitrary"),
                                             vmem_limit_bytes=SSD_VMEM_BYTES),
        name="ssd_scan",
    )(tbl, xbc, xbc, xbc, dt, cw, cw, cw, cb, cb, cb, sel, hp, s0)


def ssd_tables(p, e):
    nq = SSD_HEADS // SSD_QH
    lane = jnp.arange(128)
    sel = jnp.stack([(lane[:, None] == (q * SSD_QH + lane[None, :])) & (lane[None, :] < SSD_QH)
                     for q in range(nq)]).astype(jnp.bfloat16)
    a_neg = -jnp.exp(p['ssd_a_log'][e].astype(jnp.float32))
    rows = jnp.stack([p['ssd_dt_bias'][e, 0], p['ssd_dt_bias'][e, 1], a_neg[0], a_neg[1], p['ssd_d'][e]])
    hp = jnp.zeros((nq, 8, 128), jnp.float32)
    hp = hp.at[:, :5, :SSD_QH].set(jnp.transpose(rows.reshape(5, nq, SSD_QH), (1, 0, 2)))
    return sel, hp


MIX_VMEM_BYTES = 48 * 1024 * 1024
MIX_BLOCKS = 1
ROUTER_LANES = 128


def _residual_norm_router(x, out, g1, n2g, sc2, sh2, rw_ref, x_out_ref, hn_ref, aff_ref):
    rows, d = x.shape
    nb = g1.shape[0]
    x_new = x.reshape(nb, rows // nb, d) + g1 * out.reshape(nb, rows // nb, d)
    x_out_ref[...] = x_new.reshape(rows, d)
    hn = _adaln(x_new, n2g, sc2, sh2).reshape(rows, d)
    hn_ref[...] = hn.astype(hn_ref.dtype)
    logits = _dot(hn, rw_ref[...], ((1,), (0,)), 3)
    lane = lax.broadcasted_iota(jnp.int32, logits.shape, 1)
    logits = jnp.where(lane < N_EXPERTS, logits, NEG_BIG)
    ex = jnp.exp(logits - jnp.max(logits, axis=-1, keepdims=True))
    aff = ex / jnp.sum(ex, axis=-1, keepdims=True)
    aff_ref[...] = aff.T[:N_EXPERTS]


def _l0_out_kernel(ys_ref, z_ref, yf_ref, yb_ref, bonus_ref, gate_ref, sg_ref, lnw_ref, lnb_ref, e_ref, et_ref, w_ref,
                   x_ref, g1_ref, n2g_ref, sc2_ref, sh2_ref, rw_ref, x_out_ref, hn_ref, aff_ref):
    f32, bf16 = jnp.float32, jnp.bfloat16
    z = z_ref[...].astype(f32)
    ys = ys_ref[...] * (z * jax.nn.sigmoid(z))
    gw = SSD_INNER // SSD_GROUPS
    parts = []
    for gi in range(SSD_GROUPS):
        yg = ys[:, gi * gw:(gi + 1) * gw]
        parts.append(yg * lax.rsqrt(jnp.mean(yg * yg, -1, keepdims=True) + NORM_EPS))
    a1 = jnp.concatenate(parts, axis=1) * sg_ref[...]
    o = yf_ref[...].astype(f32) + yb_ref[...].astype(f32)
    mu = _head_sum(o, e_ref, et_ref) * (1.0 / RWKV_N)
    oc = o - mu
    var = _head_sum(oc * oc, e_ref, et_ref) * (1.0 / RWKV_N)
    o = oc * lax.rsqrt(var + RWKV_GN_EPS) * lnw_ref[...] + lnb_ref[...]
    o = (o + bonus_ref[...].astype(f32)) * gate_ref[...].astype(f32)
    out = (jnp.dot(a1.astype(bf16), w_ref[:SSD_INNER], preferred_element_type=f32)
           + jnp.dot(o.astype(bf16), w_ref[SSD_INNER:], preferred_element_type=f32))
    _residual_norm_router(x_ref[...], out, g1_ref[...], n2g_ref[...], sc2_ref[...], sh2_ref[...], rw_ref,
                          x_out_ref, hn_ref, aff_ref)


def _l1_out_kernel(a_ref, w_ref, x_ref, g1_ref, n2g_ref, sc2_ref, sh2_ref, rw_ref, x_out_ref, hn_ref, aff_ref):
    out = jnp.dot(a_ref[...].astype(jnp.bfloat16), w_ref[...], preferred_element_type=jnp.float32)
    _residual_norm_router(x_ref[...], out, g1_ref[...], n2g_ref[...], sc2_ref[...], sh2_ref[...], rw_ref,
                          x_out_ref, hn_ref, aff_ref)


def _mix_out_call(kernel_fn, name, lead_args, lead_specs, w_out, x, g1b, n2g, sc2b, sh2b, router_w):
    R, D = x.shape
    mb = MIX_BLOCKS
    assert (R // BLK) % mb == 0
    full = lambda a: pl.BlockSpec(a.shape, lambda i: (0,) * a.ndim)
    blkrow = pl.BlockSpec((mb, 1, D), lambda i: (i, 0, 0))
    rw = jnp.zeros((D, ROUTER_LANES), jnp.float32).at[:, :N_EXPERTS].set(router_w)
    args = list(lead_args) + [w_out, x, g1b, n2g, sc2b, sh2b, rw]
    x_idx = len(lead_args) + 1
    in_specs = list(lead_specs) + [full(w_out), pl.BlockSpec((mb * BLK, D), lambda i: (i, 0)), blkrow, full(n2g), blkrow,
                                   blkrow, full(rw)]
    return pl.pallas_call(
        kernel_fn,
        grid=(R // (mb * BLK),),
        in_specs=in_specs,
        out_specs=[pl.BlockSpec((mb * BLK, D), lambda i: (i, 0)), pl.BlockSpec((mb * BLK, D), lambda i: (i, 0)),
                   pl.BlockSpec((N_EXPERTS, mb * BLK), lambda i: (0, i))],
        out_shape=[jax.ShapeDtypeStruct((R, D), jnp.float32), jax.ShapeDtypeStruct((R, D), jnp.bfloat16),
                   jax.ShapeDtypeStruct((N_EXPERTS, R), jnp.float32)],
        input_output_aliases={x_idx: 0},
        compiler_params=pltpu.CompilerParams(dimension_semantics=("arbitrary",), vmem_limit_bytes=MIX_VMEM_BYTES),
        name=name,
    )(*args)


def l0_out(ys, z, yf, yb, bonus, gate, ssd_g, ln_w, ln_b, e_ind, et_ind, w_out, x, g1b, n2g, sc2b, sh2b, router_w):
    c = RWKV_DIM
    rows = lambda n: pl.BlockSpec((MIX_BLOCKS * BLK, n), lambda i: (i, 0))
    full = lambda a: pl.BlockSpec(a.shape, lambda i: (0,) * a.ndim)
    lead = [ys, z, yf, yb, bonus, gate, ssd_g, ln_w, ln_b, e_ind, et_ind]
    specs = [rows(SSD_INNER), rows(SSD_INNER), rows(c), rows(c), rows(c), rows(c),
             full(ssd_g), full(ln_w), full(ln_b), full(e_ind), full(et_ind)]
    return _mix_out_call(_l0_out_kernel, "l0_out", lead, specs, w_out, x, g1b, n2g, sc2b, sh2b, router_w)


def l1_out(a, w_out, x, g1b, n2g, sc2b, sh2b, router_w):
    specs = [pl.BlockSpec((MIX_BLOCKS * BLK, a.shape[1]), lambda i: (i, 0))]
    return _mix_out_call(_l1_out_kernel, "l1_out", [a], specs, w_out, x, g1b, n2g, sc2b, sh2b, router_w)


RET_HPS = 2


def _l1_in_kernel(tbl_ref, x_ref, g_ref, sc_ref, sh_ref, w_ref, cos_ref, sin_ref, q_ref, k_ref, v_ref, gg_ref):
    f32 = jnp.float32
    hn = _adaln(x_ref[...], g_ref[...], sc_ref[0], sh_ref[0]).astype(jnp.bfloat16)
    cosf, sinf = cos_ref[0], sin_ref[0]

    def rope(x):
        parts = []
        for h in range(RET_HEADS):
            xh = x[:, h * RET_DK:(h + 1) * RET_DK]
            parts.append(xh * cosf + pltpu.roll(xh, RET_DK // 2, 1) * sinf)
        return jnp.concatenate(parts, axis=1)

    q_ref[...] = rope(jnp.dot(hn, w_ref[:, :RET_QK], preferred_element_type=f32)).astype(q_ref.dtype)
    k_ref[...] = (rope(jnp.dot(hn, w_ref[:, RET_QK:2 * RET_QK], preferred_element_type=f32))
                  * (RET_DK ** -0.5)).astype(k_ref.dtype)
    v_ref[...] = jnp.dot(hn, w_ref[:, 2 * RET_QK:2 * RET_QK + RET_V], preferred_element_type=f32).astype(v_ref.dtype)
    gg_ref[...] = jnp.dot(hn, w_ref[:, 2 * RET_QK + RET_V:], preferred_element_type=f32).astype(gg_ref.dtype)


def l1_in(x, g1, scb, shb, w_bf16, cos_t, sin_t, rope_blk):
    R, D = x.shape
    nblk = R // BLK
    tbl = jnp.asarray([rope_blk], jnp.int32)
    f32 = jnp.float32
    row = lambda n: pl.BlockSpec((BLK, n), lambda i, t: (i, 0))
    full = lambda a: pl.BlockSpec(a.shape, lambda i, t: (0,) * a.ndim)
    blkrow = pl.BlockSpec((1, 1, D), lambda i, t: (i, 0, 0))
    ropespec = pl.BlockSpec((1, BLK, RET_DK), lambda i, t: (t[0, i], 0, 0))
    return pl.pallas_call(
        _l1_in_kernel,
        grid_spec=pltpu.PrefetchScalarGridSpec(
            num_scalar_prefetch=1, grid=(nblk,),
            in_specs=[row(D), full(g1), blkrow, blkrow, full(w_bf16), ropespec, ropespec],
            out_specs=[row(RET_QK), row(RET_QK), row(RET_V), row(RET_V)]),
        out_shape=[jax.ShapeDtypeStruct((R, RET_QK), ACT_DTYPE), jax.ShapeDtypeStruct((R, RET_QK), ACT_DTYPE),
                   jax.ShapeDtypeStruct((R, RET_V), ACT_DTYPE), jax.ShapeDtypeStruct((R, RET_V), ACT_DTYPE)],
        compiler_params=pltpu.CompilerParams(dimension_semantics=("arbitrary",), vmem_limit_bytes=L0_VMEM_BYTES),
        name="l1_in",
    )(tbl, x, g1, scb, shb, w_bf16, cos_t, sin_t)


def _ret_kernel(tbl_ref, q_ref, k_ref, v_ref, g_ref, lg_ref, nw_ref, nb_ref, s0_ref, a_ref, fs_ref, st_ref, *,
                n_zero):
    f32, bf16 = jnp.float32, jnp.bfloat16
    sb = pl.program_id(0)
    nch = q_ref.shape[0] // BLK
    heads = range(RET_HPS)
    ks = [slice(h * RET_DK, (h + 1) * RET_DK) for h in heads]
    vs = [slice(h * RET_DV, (h + 1) * RET_DV) for h in heads]
    lgf = [lg_ref[h, 0:1, 0:1] for h in heads]
    lgb = [lg_ref[h, 1:2, 0:1] for h in heads]
    t_i = lax.broadcasted_iota(jnp.int32, (BLK, BLK), 0)
    s_i = lax.broadcasted_iota(jnp.int32, (BLK, BLK), 1)
    dist = (t_i - s_i).astype(f32)
    dm = [jnp.exp(jnp.where(s_i <= t_i, dist * lgf[h], NEG_BIG)) + jnp.exp(jnp.where(s_i >= t_i, -dist * lgb[h], NEG_BIG))
          for h in heads]
    tk = lax.broadcasted_iota(jnp.int32, (BLK, RET_DK), 0).astype(f32)
    tv = lax.broadcasted_iota(jnp.int32, (BLK, RET_DV), 0).astype(f32)
    k_to_end_f = [jnp.exp((BLK - 1.0 - tk) * lgf[h]) for h in heads]
    k_to_end_b = [jnp.exp(tk * lgb[h]) for h in heads]
    from_start_f = [jnp.exp((tv + 1.0) * lgf[h]) for h in heads]
    from_start_b = [jnp.exp((BLK - tv) * lgb[h]) for h in heads]
    nt, tn = (((1,), (1,)), ((), ())), (((0,), (0,)), ((), ()))

    def chunk_rows(c):
        return pl.ds(pl.multiple_of(c * BLK, BLK), BLK)

    def fwd(c, carry):
        blk = sb * nch + c
        rows = chunk_rows(c)

        @pl.when(tbl_ref[0, blk] == 1)
        def _():
            st_ref[0] = jnp.where(sb >= n_zero, s0_ref[0, 0], 0.0)

        q = [q_ref[rows, ks[h]].astype(bf16) for h in heads]
        k = [k_ref[rows, ks[h]] for h in heads]
        v = [v_ref[rows, vs[h]].astype(bf16) for h in heads]
        s_in = [st_ref[0, h] for h in heads]
        g = [lax.dot_general(q[h], k[h].astype(bf16), nt, preferred_element_type=f32) for h in heads]
        y_diag = [jnp.dot((g[h] * dm[h]).astype(bf16), v[h], preferred_element_type=f32) for h in heads]
        y_off = [jnp.dot(q[h], s_in[h].astype(bf16), preferred_element_type=f32) for h in heads]
        kw = [(k[h].astype(f32) * k_to_end_f[h]).astype(bf16) for h in heads]
        upd = [lax.dot_general(kw[h], v[h], tn, preferred_element_type=f32) for h in heads]
        for h in heads:
            a_ref[rows, vs[h]] = y_diag[h] + from_start_f[h] * y_off[h]
            st_ref[0, h] = jnp.exp(BLK * lgf[h]) * s_in[h] + upd[h]
        fs_ref[c, 0] = st_ref[0]
        return carry

    lax.fori_loop(0, nch, fwd, 0)

    def bwd(j, carry):
        c = nch - 1 - j
        blk = sb * nch + c
        rows = chunk_rows(c)

        @pl.when(tbl_ref[1, blk] == 1)
        def _():
            st_ref[1] = jnp.where(sb >= n_zero, s0_ref[0, 1], 0.0)

        q = [q_ref[rows, ks[h]].astype(bf16) for h in heads]
        v = [v_ref[rows, vs[h]].astype(bf16) for h in heads]
        s_in = [st_ref[1, h] for h in heads]
        y_off = [jnp.dot(q[h], s_in[h].astype(bf16), preferred_element_type=f32) for h in heads]
        kw = [(k_ref[rows, ks[h]].astype(f32) * k_to_end_b[h]).astype(bf16) for h in heads]
        upd = [lax.dot_general(kw[h], v[h], tn, preferred_element_type=f32) for h in heads]
        for h in heads:
            st_ref[1, h] = jnp.exp(BLK * lgb[h]) * s_in[h] + upd[h]
            y = a_ref[rows, vs[h]] + from_start_b[h] * y_off[h]
            mu = jnp.mean(y, -1, keepdims=True)
            yc = y - mu
            var = jnp.mean(yc * yc, -1, keepdims=True)
            gg = g_ref[rows, vs[h]].astype(f32)
            a_ref[rows, vs[h]] = ((yc * lax.rsqrt(var + 1e-5) * nw_ref[:, vs[h]] + nb_ref[:, vs[h]])
                                  * (gg * jax.nn.sigmoid(gg)))
        fs_ref[c, 1] = st_ref[1]
        return carry

    lax.fori_loop(0, nch, bwd, 0)


def ret_scan(q, k, v, g, lg_tab, norm_w, norm_b, s0, n_zero, first, last, sb_rows):
    R = q.shape[0]
    n_sb = R // sb_rows
    nch = sb_rows // BLK
    tbl = jnp.asarray([first, last], jnp.int32)
    f32 = jnp.float32
    hps = RET_HPS
    return pl.pallas_call(
        functools.partial(_ret_kernel, n_zero=n_zero),
        grid_spec=pltpu.PrefetchScalarGridSpec(
            num_scalar_prefetch=1, grid=(n_sb, RET_HEADS // hps),
            in_specs=[pl.BlockSpec((sb_rows, hps * RET_DK), lambda s, h, t: (s, h)),
                      pl.BlockSpec((sb_rows, hps * RET_DK), lambda s, h, t: (s, h)),
                      pl.BlockSpec((sb_rows, hps * RET_DV), lambda s, h, t: (s, h)),
                      pl.BlockSpec((sb_rows, hps * RET_DV), lambda s, h, t: (s, h)),
                      pl.BlockSpec((hps, 8, 128), lambda s, h, t: (h, 0, 0)),
                      pl.BlockSpec((1, hps * RET_DV), lambda s, h, t: (0, h)),
                      pl.BlockSpec((1, hps * RET_DV), lambda s, h, t: (0, h)),
                      pl.BlockSpec((1, 2, hps, RET_DK, RET_DV), lambda s, h, t: (jnp.maximum(s - n_zero, 0), 0, h, 0, 0))],
            out_specs=[pl.BlockSpec((sb_rows, hps * RET_DV), lambda s, h, t: (s, h)),
                       pl.BlockSpec((nch, 2, hps, RET_DK, RET_DV), lambda s, h, t: (s, 0, h, 0, 0))],
            scratch_shapes=[pltpu.VMEM((2, hps, RET_DK, RET_DV), f32)]),
        out_shape=[jax.ShapeDtypeStruct((R, RET_V), f32),
                   jax.ShapeDtypeStruct((R // BLK, 2, RET_HEADS, RET_DK, RET_DV), f32)],
        compiler_params=pltpu.CompilerParams(dimension_semantics=("arbitrary", "arbitrary"),
                                             vmem_limit_bytes=SSD_VMEM_BYTES),
        name="ret_scan",
    )(tbl, q, k, v, g, lg_tab, norm_w.reshape(1, -1), norm_b.reshape(1, -1), s0)


def _mod_kernel(c_ref, w_ref, b_ref, o_ref):
    c = c_ref[...]
    act = c * jax.nn.sigmoid(c)
    o_ref[0] = _dot(act, w_ref[0], ((1,), (0,)), 3) + b_ref[0]


def mod_vectors(conds, mod_w, mod_b):
    depth, D, n6 = mod_w.shape
    tn = D
    return pl.pallas_call(
        _mod_kernel,
        grid=(depth, n6 // tn),
        in_specs=[pl.BlockSpec(conds.shape, lambda i, j: (0, 0)),
                  pl.BlockSpec((1, D, tn), lambda i, j: (i, 0, j)),
                  pl.BlockSpec((1, 1, tn), lambda i, j: (i, 0, j))],
        out_specs=pl.BlockSpec((1, conds.shape[0], tn), lambda i, j: (i, 0, j)),
        out_shape=jax.ShapeDtypeStruct((depth, conds.shape[0], n6), jnp.float32),
        name="mod_vectors",
    )(conds, mod_w, mod_b.reshape(depth, 1, n6))


def rope_tables(n_tokens):
    rows = n_tokens // GRID_W
    row = jnp.repeat(jnp.arange(rows), GRID_W).astype(jnp.float32)
    col = jnp.tile(jnp.arange(GRID_W), rows).astype(jnp.float32)
    n_f = RET_DK // 4
    inv = ROPE_BASE ** (-jnp.arange(n_f, dtype=jnp.float32) / n_f)
    ang = jnp.concatenate([row[:, None] * inv, col[:, None] * inv], -1)
    return jnp.cos(ang), jnp.sin(ang)


def _rope_block_tables(n_ctx, n_lat, l_lat):
    nb = l_lat // BLK
    cos, sin = rope_tables(l_lat)
    cosf = jnp.concatenate([cos, cos], -1).reshape(nb, BLK, RET_DK)
    sinf = jnp.concatenate([-sin, sin], -1).reshape(nb, BLK, RET_DK)
    cos_t = jnp.concatenate([jnp.ones((1, BLK, RET_DK), jnp.float32), cosf])
    sin_t = jnp.concatenate([jnp.zeros((1, BLK, RET_DK), jnp.float32), sinf])
    rope_blk = [0] * n_ctx + [1 + j for _ in range(n_lat) for j in range(nb)]
    return cos_t, sin_t, rope_blk


def kernel(x_prompt, x_sample, state_ssd, state_rwkv, state_ret, c, c_ctx, mod_w, mod_b, norm1_g, norm2_g,
           router_w, exp_w_gate, exp_w_up, exp_w_down, ab_w_in, ab_w_out, ssd_conv_w, ssd_conv_b, ssd_dt_bias,
           ssd_a_log, ssd_d, ssd_norm_g, rwkv_mu_prev, rwkv_mu_next, rwkv_w0, rwkv_w2, rwkv_a0, rwkv_a2, rwkv_g2,
           rwkv_k_k, rwkv_k_a, rwkv_r_k, rwkv_ln_w, rwkv_ln_b, ret_w_in, ret_w_out, ret_decay_logit, ret_norm_w,
           ret_norm_b, final_norm_g):
    p = dict(mod_w=mod_w, mod_b=mod_b, norm1_g=norm1_g, norm2_g=norm2_g, router_w=router_w,
             exp_w_gate=exp_w_gate, exp_w_up=exp_w_up, exp_w_down=exp_w_down, ab_w_in=ab_w_in, ab_w_out=ab_w_out,
             ssd_conv_w=ssd_conv_w, ssd_conv_b=ssd_conv_b, ssd_dt_bias=ssd_dt_bias, ssd_a_log=ssd_a_log,
             ssd_d=ssd_d, ssd_norm_g=ssd_norm_g, rwkv_mu_prev=rwkv_mu_prev, rwkv_mu_next=rwkv_mu_next,
             rwkv_w0=rwkv_w0, rwkv_w2=rwkv_w2, rwkv_a0=rwkv_a0, rwkv_a2=rwkv_a2, rwkv_g2=rwkv_g2,
             rwkv_k_k=rwkv_k_k, rwkv_k_a=rwkv_k_a, rwkv_r_k=rwkv_r_k, rwkv_ln_w=rwkv_ln_w, rwkv_ln_b=rwkv_ln_b,
             ret_w_in=ret_w_in, ret_w_out=ret_w_out, ret_decay_logit=ret_decay_logit, ret_norm_w=ret_norm_w,
             ret_norm_b=ret_norm_b, final_norm_g=final_norm_g)
    f32, bf16 = jnp.float32, jnp.bfloat16
    n_ctx, l_ctx, D = x_prompt.shape
    n_lat, l_lat, _ = x_sample.shape
    assert l_ctx == BLK and l_lat % BLK == 0 and (n_ctx * BLK) % l_lat == 0
    n_sb_ctx = n_ctx * BLK // l_lat
    cond_id, first, last = _seq_tables(n_ctx, n_lat, l_lat)
    x = jnp.concatenate([x_prompt.reshape(-1, D), x_sample.reshape(-1, D)])

    conds = jnp.concatenate([c_ctx[None, :], c, jnp.zeros((8 - 1 - n_lat, D), f32)])
    mods = mod_vectors(conds, mod_w, mod_b)[:, jnp.asarray(cond_id)]
    mods = mods.reshape(DEPTH, len(cond_id), 6, 1, D)

    new_ssd, new_rwkv, new_ret = [], [], []
    out = None
    for i in range(DEPTH):
        sh1, sc1, g1, sh2, sc2, g2 = (mods[i, :, k] for k in range(6))
        e = i // 2
        if i % 2 == 0:
            w_packed, mup, mun, rwp, w2bd, a2p, g2p, e_ind, et_ind = l0_pack_weights(p, e)
            z, xbc, dt, r, v, an, lw, kd, bv, gate, bonus = l0_in(
                x, norm1_g[i][None], sc1, sh1, w_packed, mup, mun, rwp, w2bd, a2p, g2p, e_ind, et_ind, first, last)
            sel, hp = ssd_tables(p, e)
            s0_ssd = jnp.transpose(state_ssd[:, e], (0, 1, 3, 2, 4)).reshape(n_lat, 2, SSD_N, SSD_INNER)
            ys, fs_ssd = ssd_scan(xbc, dt, ssd_conv_w[e], ssd_conv_b[e], sel, hp,
                                  s0_ssd, n_sb_ctx, first, last, l_lat)
            new_ssd.append(jnp.transpose(fs_ssd[:n_ctx].reshape(n_ctx, 2, SSD_N, SSD_HEADS, SSD_P), (0, 1, 3, 2, 4)))
            s0_rwkv = jnp.transpose(state_rwkv[:, e], (0, 1, 3, 2, 4)).reshape(n_lat, 2, RWKV_N, RWKV_DIM)
            yf, yb, sf_rwkv = rwkv_scan_pallas(r, v, an, lw, kd, bv, s0_rwkv, n_ctx,
                                               _rwkv_steps(n_ctx, n_lat, l_lat))
            new_rwkv.append(jnp.transpose(sf_rwkv[:n_ctx].reshape(n_ctx, 2, RWKV_N, RWKV_HEADS, RWKV_N),
                                          (0, 1, 3, 2, 4)))
            x, hn2, affT = l0_out(ys, z, yf, yb, bonus, gate, ssd_norm_g[e][None], rwkv_ln_w[e][None], rwkv_ln_b[e][None],
                                  e_ind, et_ind, ab_w_out[e].astype(bf16), x, g1, norm2_g[i][None], sc2, sh2,
                                  router_w[i])
        else:
            cos_t, sin_t, rope_blk = _rope_block_tables(n_ctx, n_lat, l_lat)
            q, k, v, gg = l1_in(x, norm1_g[i][None], sc1, sh1, ret_w_in[e].astype(bf16), cos_t, sin_t, rope_blk)
            lg = jax.nn.log_sigmoid(ret_decay_logit[e].astype(f32))
            lg_tab = jnp.zeros((RET_HEADS, 8, 128), f32).at[:, :2, :].set(jnp.transpose(lg)[:, :, None])
            a, fs_ret = ret_scan(q, k, v, gg, lg_tab, ret_norm_w[e], ret_norm_b[e],
                                 state_ret[:, e], n_sb_ctx, first, last, l_lat)
            new_ret.append(fs_ret[:n_ctx])
            x, hn2, affT = l1_out(a, ret_w_out[e].astype(bf16), x, g1, norm2_g[i][None], sc2, sh2, router_w[i])
        fin = final_norm_g if i == DEPTH - 1 else None
        out = moe_layer(x, hn2, affT, g2, exp_w_gate, exp_w_up, exp_w_down, i, n_ctx, l_lat, final_g=fin)
        if fin is None:
            x = out
    y_ctx, y_lat = out
    return (y_ctx.reshape(n_ctx, l_ctx, D), y_lat.reshape(n_lat, l_lat, D),
            jnp.stack(new_ssd, 1), jnp.stack(new_rwkv, 1), jnp.stack(new_ret, 1))
```

```python
import functools
import math

import jax
import jax.numpy as jnp
import numpy as np
from jax import lax
from jax.experimental import pallas as pl
from jax.experimental.pallas import tpu as pltpu

D_MODEL = 1024
DEPTH = 2
GRID_W = 64
CHUNK = 128
NORM_EPS = 1e-6
SSD_HEADS = 16
SSD_P = 64
SSD_INNER = SSD_HEADS * SSD_P
SSD_GROUPS = 2
SSD_HPG = SSD_HEADS // SSD_GROUPS
SSD_N = 128
SSD_XBC = SSD_INNER + 2 * SSD_GROUPS * SSD_N
SSD_IN = SSD_INNER + SSD_XBC + SSD_HEADS
RWKV_HEADS = 16
RWKV_N = 64
RWKV_DIM = RWKV_HEADS * RWKV_N
W_LORA = 64
A_LORA = 64
G_LORA = 128
RWKV_GN_EPS = 64e-5
RET_HEADS = 8
RET_DK = 128
RET_DV = 256
RET_QK = RET_HEADS * RET_DK
RET_V = RET_HEADS * RET_DV
ROPE_BASE = 10000.0
N_EXPERTS = 16
EC_CAPACITY = 2

ACT_DTYPE = jnp.bfloat16

RWKV_C = 64
RWKV_GH = 4
RWKV_GL = RWKV_GH * RWKV_N
RWKV_DOUBLING_PASSES = (3, 3, 3, 3, 1, 1)


def _split_bf16(x):
    hi = x.astype(jnp.bfloat16)
    lo = (x - hi.astype(jnp.float32)).astype(jnp.bfloat16)
    return hi, lo


def _dot(a, b, dims, passes):
    f = functools.partial(lax.dot_general, dimension_numbers=(dims, ((), ())),
                          preferred_element_type=jnp.float32)
    if passes == 1:
        return f(a.astype(jnp.bfloat16), b.astype(jnp.bfloat16))
    ah, al = _split_bf16(a)
    bh, bl = _split_bf16(b)
    return f(ah, bh) + (f(ah, bl) + f(al, bh))


def _rwkv_chunk_kernel(tbl_ref, r0_ref, v0_ref, a0_ref, r1_ref, v1_ref, a1_ref, lw0_ref, k0_ref, b0_ref,
                       lw1_ref, k1_ref, b1_ref, s0_ref, y0_ref, y1_ref, sf_ref, h_ref, *, p_inv, p_oth, n_zero):
    C, N, GH, GL = RWKV_C, RWKV_N, RWKV_GH, RWKV_GL
    i = pl.program_id(0)
    f32, bf16 = jnp.float32, jnp.bfloat16

    @pl.when(tbl_ref[3, i] == 1)
    def _():
        h_ref[...] = jnp.where(tbl_ref[2, i] >= n_zero, s0_ref[0], 0.0)

    t_i = lax.broadcasted_iota(jnp.int32, (C, GL), 0)
    s_i = lax.broadcasted_iota(jnp.int32, (C, GL), 1) & (N - 1)
    eye = (s_i == t_i).astype(f32)
    row2 = lax.broadcasted_iota(jnp.int32, (2 * C, GL), 0)
    rel2 = (lax.broadcasted_iota(jnp.int32, (2 * C, GL), 1) & (N - 1)) - (row2 & (C - 1))
    incl2 = row2 // C
    mask2 = [rel2 - incl2 < 0, -rel2 - incl2 < 0]
    bh_r = lax.broadcasted_iota(jnp.int32, (GL, GL), 0) // N
    bh_c = lax.broadcasted_iota(jnp.int32, (GL, GL), 1) // N
    blk = bh_r == bh_c
    tt = lax.broadcasted_iota(jnp.int32, (C, C), 0)
    ss = lax.broadcasted_iota(jnp.int32, (C, C), 1)
    tri = [(ss <= tt).astype(bf16), (ss >= tt).astype(bf16)]

    def bd(x, passes):
        pieces = []
        for _ in range(2 if passes == 3 else 1):
            hi = x.astype(bf16)
            x = x - hi.astype(f32)
            pieces.append(jnp.where(blk, jnp.concatenate([hi] * GH, axis=0), jnp.zeros((), bf16)))
        return pieces

    def mm(l, x, passes, dims=((1,), (0,))):
        f = functools.partial(lax.dot_general, dimension_numbers=(dims, ((), ())), preferred_element_type=f32)
        xs = bd(x, passes)
        lh = l.astype(bf16)
        if passes == 1:
            return f(lh, xs[0])
        ll = (l - lh.astype(f32)).astype(bf16)
        m = l.shape[0]
        both = f(jnp.concatenate([lh, ll], axis=0), xs[0])
        if passes == 2:
            return both[:m] + both[m:]
        return both[:m] + (f(lh, xs[1]) + both[m:])

    nt = ((1,), (1,))
    refs = [(r0_ref, v0_ref, a0_ref, lw0_ref, k0_ref, b0_ref), (r1_ref, v1_ref, a1_ref, lw1_ref, k1_ref, b1_ref)]
    lw, r_t, a_t, b_t, k_t, v = [], [], [], [], [], []
    for d, (r_ref, v_ref, a_ref, lw_ref, k_ref, b_ref) in enumerate(refs):
        lwd = lw_ref[0]
        lw_hi, lw_lo = _split_bf16(lwd)
        cum = (jnp.dot(tri[d], lw_hi, preferred_element_type=f32) + jnp.dot(tri[d], lw_lo, preferred_element_type=f32))
        w_inv = jnp.exp(-cum)
        lw.append(lwd)
        r_t.append(r_ref[...].astype(f32) * jnp.exp(cum))
        a_t.append(a_ref[...].astype(f32) * jnp.exp(cum - lwd))
        b_t.append(b_ref[0].astype(f32) * w_inv)
        k_t.append(k_ref[0].astype(f32) * w_inv)
        v.append(v_ref[...].astype(f32))
    lane_head = lax.broadcasted_iota(jnp.int32, (N, GL), 1) // N

    chains = [(d, slice(g * GL, (g + 1) * GL)) for d in range(2) for g in range(RWKV_HEADS // GH)]
    each = lambda fn: [fn(j, d, sl) for j, (d, sl) in enumerate(chains)]
    bg = each(lambda j, d, sl: b_t[d][:, sl])
    kg = each(lambda j, d, sl: k_t[d][:, sl])
    vg = each(lambda j, d, sl: v[d][:, sl])
    h0 = each(lambda j, d, sl: h_ref[d, :, sl])
    ar = each(lambda j, d, sl: jnp.concatenate([a_t[d][:, sl], r_t[d][:, sl]], axis=0))
    m_b = each(lambda j, d, sl: jnp.where(mask2[d], mm(ar[j], bg[j], p_oth, nt), 0.0))
    m_k = each(lambda j, d, sl: jnp.where(mask2[d], mm(ar[j], kg[j], p_oth, nt), 0.0))
    p = each(lambda j, d, sl: mm(m_b[j][:C], m_b[j][:C], p_inv[0]))
    tmat = each(lambda j, d, sl: eye + m_b[j][:C])
    for lv in range(int(math.log2(C)) - 2):
        pt = each(lambda j, d, sl: mm(jnp.concatenate([p[j], tmat[j]], axis=0), p[j], p_inv[1 + lv]))
        p = each(lambda j, d, sl: pt[j][:C])
        tmat = each(lambda j, d, sl: tmat[j] + pt[j][C:])
    tmat = each(lambda j, d, sl: tmat[j] + mm(tmat[j], p[j], p_inv[-1]))
    ar_h = each(lambda j, d, sl: mm(ar[j], h0[j], p_oth, nt))
    mk_v = each(lambda j, d, sl: mm(m_k[j], vg[j], p_oth))
    u = each(lambda j, d, sl: mm(tmat[j], ar_h[j][:C] + mk_v[j][:C], p_oth))
    y = each(lambda j, d, sl: ar_h[j][C:] + mm(m_b[j][C:], u[j], p_oth) + mk_v[j][C:])
    full = each(lambda j, d, sl: _dot(jnp.concatenate([u[j], vg[j]], axis=0), jnp.concatenate([bg[j], kg[j]], axis=0),
                                      ((0,), (0,)), p_oth))
    y_refs = (y0_ref, y1_ref)
    for j, (d, sl) in enumerate(chains):
        y_refs[d][:, sl] = y[j].astype(y_refs[d].dtype)
        z = jnp.zeros((N, GL), f32)
        for hh in range(GH):
            z = z + jnp.where(lane_head == hh, full[j][hh * N:(hh + 1) * N], 0.0)
        w_tot = jnp.exp(jnp.sum(lw[d][:, sl], axis=0, keepdims=True))
        h_ref[d, :, sl] = w_tot * (h0[j] + z)

    @pl.when(tbl_ref[4, i] == 1)
    def _():
        sf_ref[0] = h_ref[...]


def _rwkv_steps(n_ctx, n_lat, l_lat):
    C = RWKV_C
    rows = []
    seqs = [(s, s * BLK, BLK) for s in range(n_ctx)] + [(n_ctx + s, n_ctx * BLK + s * l_lat, l_lat) for s in range(n_lat)]
    for sid, row0, length in seqs:
        nc = length // C
        for j in range(nc):
            rows.append((row0 // C + j, row0 // C + nc - 1 - j, sid, int(j == 0), int(j == nc - 1)))
    return [list(col) for col in zip(*rows)]


def rwkv_scan_pallas(r, v, a, lw, k, b, s0, n_zero, steps, p_inv=RWKV_DOUBLING_PASSES, p_oth=1):
    R, HN = r.shape
    C, N = RWKV_C, RWKV_N
    tbl = jnp.asarray(steps, jnp.int32)
    fwd = pl.BlockSpec((C, HN), lambda i, t: (t[0, i], 0))
    bwd = pl.BlockSpec((C, HN), lambda i, t: (t[1, i], 0))
    fwd_d = pl.BlockSpec((1, C, HN), lambda i, t: (0, t[0, i], 0))
    bwd_d = pl.BlockSpec((1, C, HN), lambda i, t: (1, t[1, i], 0))
    n_seq = max(steps[2]) + 1
    st_in = pl.BlockSpec((1, 2, N, HN), lambda i, t: (jnp.maximum(t[2, i] - n_zero, 0), 0, 0, 0))
    st = pl.BlockSpec((1, 2, N, HN), lambda i, t: (t[2, i], 0, 0, 0))
    return pl.pallas_call(
        functools.partial(_rwkv_chunk_kernel, p_inv=p_inv, p_oth=p_oth, n_zero=n_zero),
        grid_spec=pltpu.PrefetchScalarGridSpec(
            num_scalar_prefetch=1,
            grid=(len(steps[0]),),
            in_specs=[fwd, fwd, fwd, bwd, bwd, bwd, fwd_d, fwd_d, fwd_d, bwd_d, bwd_d, bwd_d, st_in],
            out_specs=[fwd, bwd, st],
            scratch_shapes=[pltpu.VMEM((2, N, HN), jnp.float32)]),
        out_shape=[jax.ShapeDtypeStruct((R, HN), ACT_DTYPE), jax.ShapeDtypeStruct((R, HN), ACT_DTYPE),
                   jax.ShapeDtypeStruct((n_seq, 2, N, HN), jnp.float32)],
        compiler_params=pltpu.CompilerParams(dimension_semantics=("arbitrary",)),
        name="rwkv_scan",
    )(tbl, r, v, a, r, v, a, lw, k, b, lw, k, b, s0)


BLK = 256
FF_TILE = 768
SELECT_TILE = 512
GATHER_ROWS = 512
SELECT_MIN_EXP = -1100.0
SELECT_BINADE_STEPS = 11
SELECT_MANTISSA_STEPS = 40
MOE_FFN_VMEM_BYTES = 48 * 1024 * 1024


def _moe_select_kernel(aff_ref, slot_ref, *, cap):
    a = aff_ref[...]
    E, T = a.shape
    f32 = jnp.float32

    def enough(piv):
        return jnp.sum(jnp.where(a >= piv, 1.0, 0.0), axis=1, keepdims=True) >= cap

    def binade(_, lohi):
        e_lo, e_hi = lohi
        mid = jnp.floor((e_lo + e_hi) * 0.5)
        ok = enough(jnp.exp2(mid))
        return jnp.where(ok, mid, e_lo), jnp.where(ok, e_hi, mid)

    e_lo, e_hi = lax.fori_loop(0, SELECT_BINADE_STEPS, binade,
                               (jnp.full((E, 1), SELECT_MIN_EXP, f32), jnp.full((E, 1), 1.0, f32)))

    def inside(_, lohi):
        lo, hi = lohi
        mid = lo + (hi - lo) * 0.5
        ok = enough(mid)
        return jnp.where(ok, mid, lo), jnp.where(ok, hi, mid)

    thr, _ = lax.fori_loop(0, SELECT_MANTISSA_STEPS, inside, (jnp.exp2(e_lo), jnp.exp2(e_hi)))
    gt = a > thr
    eq = a == thr
    need = cap - jnp.sum(jnp.where(gt, 1.0, 0.0), axis=1, keepdims=True)
    tw = min(T, SELECT_TILE)

    def prefix_count(mask):
        m = jnp.where(mask, 1.0, 0.0).astype(jnp.bfloat16)
        outs = []
        for j in range(T // tw):
            s_i = lax.broadcasted_iota(jnp.int32, (T, tw), 0)
            t_i = lax.broadcasted_iota(jnp.int32, (T, tw), 1) + j * tw
            before = jnp.where(s_i < t_i, 1.0, 0.0).astype(jnp.bfloat16)
            outs.append(jnp.dot(m, before, preferred_element_type=f32))
        return outs[0] if len(outs) == 1 else jnp.concatenate(outs, axis=1)

    sel = gt | (eq & (prefix_count(eq) < need))
    slot_ref[...] = jnp.where(sel, prefix_count(sel).astype(jnp.int32), -1)


def _moe_select(affT, row0, n_seq, t):
    E = affT.shape[0]
    rows = jnp.transpose(affT[:, row0:row0 + n_seq * t].reshape(E, n_seq, t), (1, 0, 2)).reshape(n_seq * E, t)
    slot = pl.pallas_call(
        functools.partial(_moe_select_kernel, cap=EC_CAPACITY * t // N_EXPERTS),
        grid=(1,),
        in_specs=[pl.BlockSpec((n_seq * E, t), lambda s: (0, 0))],
        out_specs=pl.BlockSpec((n_seq * E, t), lambda s: (0, 0)),
        out_shape=jax.ShapeDtypeStruct((n_seq * E, t), jnp.int32),
        name="moe_select",
    )(rows)
    return jnp.transpose(slot.reshape(n_seq, E, t), (1, 0, 2)).reshape(E, n_seq * t)


def _moe_gather_kernel(slot_ref, aff_ref, hn_ref, xe_ref, gate_ref, *, cap):
    slot = slot_ref[...]
    eg, _, T = slot.shape
    p_i = lax.broadcasted_iota(jnp.int32, (eg, cap, T), 1)
    hit = slot == p_i
    onehot = jnp.where(hit, 1.0, 0.0).reshape(eg * cap, T).astype(jnp.bfloat16)
    xe = jnp.dot(onehot, hn_ref[...], preferred_element_type=jnp.float32)
    xe_ref[...] = xe.reshape(eg, cap, xe.shape[1]).astype(xe_ref.dtype)
    g = jnp.sum(jnp.where(hit, aff_ref[...], 0.0), axis=2, keepdims=True)
    gate_ref[...] = jnp.broadcast_to(g, (eg, cap, 128))


def _moe_gather(slot3, aff3, hn, row0, n_seq, t):
    E = slot3.shape[0]
    D = hn.shape[1]
    cap = EC_CAPACITY * t // N_EXPERTS
    eg = min(E, GATHER_ROWS // cap)
    b0 = row0 // t
    return pl.pallas_call(
        functools.partial(_moe_gather_kernel, cap=cap),
        grid=(n_seq, E // eg),
        in_specs=[pl.BlockSpec((eg, 1, t), lambda s, e: (e, 0, b0 + s)),
                  pl.BlockSpec((eg, 1, t), lambda s, e: (e, 0, b0 + s)),
                  pl.BlockSpec((t, D), lambda s, e: (b0 + s, 0))],
        out_specs=[pl.BlockSpec((eg, cap, D), lambda s, e: (e, s, 0)),
                   pl.BlockSpec((eg, cap, 128), lambda s, e: (e, s, 0))],
        out_shape=[jax.ShapeDtypeStruct((E, n_seq * cap, D), jnp.bfloat16),
                   jax.ShapeDtypeStruct((E, n_seq * cap, 128), jnp.float32)],
        compiler_params=pltpu.CompilerParams(dimension_semantics=("arbitrary", "arbitrary"),
                                             vmem_limit_bytes=MOE_FFN_VMEM_BYTES),
        name="moe_gather",
    )(slot3, aff3, hn)


def _moe_ffn_kernel(xc_ref, xl_ref, gc_ref, gl_ref, wg_ref, wu_ref, wd_ref, yc_ref, yl_ref, accc_ref, accl_ref):
    f = pl.program_id(1)
    nf = pl.num_programs(1)
    bf16 = jnp.bfloat16
    wg = wg_ref[0, 0].astype(bf16)
    wu = wu_ref[0, 0].astype(bf16)
    wd = wd_ref[0, 0].astype(bf16)

    def part(x_ref, acc_ref):
        x = x_ref[0]
        g = jnp.dot(x, wg, preferred_element_type=jnp.float32)
        u = jnp.dot(x, wu, preferred_element_type=jnp.float32)
        h = (g * jax.nn.sigmoid(g) * u).astype(bf16)
        y = jnp.dot(h, wd, preferred_element_type=jnp.float32)

        @pl.when(f == 0)
        def _():
            acc_ref[...] = y

        @pl.when(f != 0)
        def _():
            acc_ref[...] += y

    part(xc_ref, accc_ref)
    part(xl_ref, accl_ref)

    @pl.when(f == nf - 1)
    def _():
        for acc_ref, g_ref, y_ref in ((accc_ref, gc_ref, yc_ref), (accl_ref, gl_ref, yl_ref)):
            gate = jnp.concatenate([g_ref[0]] * (acc_ref.shape[1] // 128), axis=1)
            y_ref[0] = (acc_ref[...] * gate).astype(y_ref.dtype)


def _moe_ffn(xc, xl, gc, gl, wg, wu, wd, layer):
    E, nc_rows, D = xc.shape
    nl_rows = xl.shape[1]
    F = wg.shape[3]
    nf = F // FF_TILE
    return pl.pallas_call(
        _moe_ffn_kernel,
        grid=(E, nf),
        in_specs=[pl.BlockSpec((1, nc_rows, D), lambda e, f: (e, 0, 0)),
                  pl.BlockSpec((1, nl_rows, D), lambda e, f: (e, 0, 0)),
                  pl.BlockSpec((1, nc_rows, 128), lambda e, f: (e, 0, 0)),
                  pl.BlockSpec((1, nl_rows, 128), lambda e, f: (e, 0, 0)),
                  pl.BlockSpec((1, 1, D, FF_TILE), lambda e, f: (layer, e, 0, f)),
                  pl.BlockSpec((1, 1, D, FF_TILE), lambda e, f: (layer, e, 0, f)),
                  pl.BlockSpec((1, 1, FF_TILE, D), lambda e, f: (layer, e, f, 0))],
        out_specs=[pl.BlockSpec((1, nc_rows, D), lambda e, f: (e, 0, 0)),
                   pl.BlockSpec((1, nl_rows, D), lambda e, f: (e, 0, 0))],
        out_shape=[jax.ShapeDtypeStruct((E, nc_rows, D), jnp.bfloat16),
                   jax.ShapeDtypeStruct((E, nl_rows, D), jnp.bfloat16)],
        scratch_shapes=[pltpu.VMEM((nc_rows, D), jnp.float32), pltpu.VMEM((nl_rows, D), jnp.float32)],
        compiler_params=pltpu.CompilerParams(dimension_semantics=("arbitrary", "arbitrary"),
                                             vmem_limit_bytes=MOE_FFN_VMEM_BYTES),
        name="moe_ffn",
    )(xc, xl, gc, gl, wg, wu, wd)


def _moe_scatter_kernel(slot_ref, ye_ref, x_ref, g2_ref, *rest, cap, final):
    if final:
        fg_ref, o_ref = rest
    else:
        (o_ref,) = rest
    E = ye_ref.shape[0]
    D = ye_ref.shape[2]
    tb = x_ref.shape[0]
    slot = slot_ref[...]
    p_i = lax.broadcasted_iota(jnp.int32, (E, cap, tb), 1)
    onehot = jnp.where(slot == p_i, 1.0, 0.0).reshape(E * cap, tb).astype(jnp.bfloat16)
    ye = ye_ref[...].reshape(E * cap, D)
    moe = lax.dot_general(onehot, ye, (((0,), (0,)), ((), ())), preferred_element_type=jnp.float32)
    x = x_ref[...] + g2_ref[0] * moe
    if final:
        x = x * lax.rsqrt(jnp.mean(x * x, -1, keepdims=True) + NORM_EPS) * fg_ref[...]
    o_ref[...] = x


def _moe_scatter(slot3, ye, x, g2blk, row0, n_seq, t, final_g=None):
    E, _, D = ye.shape
    cap = EC_CAPACITY * t // N_EXPERTS
    nb = t // BLK
    b0 = row0 // BLK
    final = final_g is not None
    in_specs = [pl.BlockSpec((E, 1, BLK), lambda s, j: (0, 0, b0 + s * nb + j)),
                pl.BlockSpec((E, cap, D), lambda s, j: (0, s, 0)),
                pl.BlockSpec((BLK, D), lambda s, j: (b0 + s * nb + j, 0)),
                pl.BlockSpec((1, 1, D), lambda s, j: (b0 + s * nb + j, 0, 0))]
    args = [slot3, ye, x, g2blk]
    if final:
        in_specs.append(pl.BlockSpec((1, D), lambda s, j: (0, 0)))
        args.append(final_g.reshape(1, D))
        out_specs = pl.BlockSpec((BLK, D), lambda s, j: (s * nb + j, 0))
        out_shape = jax.ShapeDtypeStruct((n_seq * t, D), jnp.float32)
        aliases = {}
    else:
        out_specs = pl.BlockSpec((BLK, D), lambda s, j: (b0 + s * nb + j, 0))
        out_shape = jax.ShapeDtypeStruct(x.shape, jnp.float32)
        aliases = {2: 0}
    return pl.pallas_call(
        functools.partial(_moe_scatter_kernel, cap=cap, final=final),
        grid=(n_seq, nb),
        in_specs=in_specs, out_specs=out_specs, out_shape=out_shape,
        input_output_aliases=aliases,
        compiler_params=pltpu.CompilerParams(dimension_semantics=("arbitrary", "arbitrary"),
                                             vmem_limit_bytes=MOE_FFN_VMEM_BYTES),
        name="moe_scatter",
    )(*args)


def moe_layer(x, hn, affT, g2blk, wg, wu, wd, layer, n_ctx, l_lat, final_g=None):
    R = x.shape[0]
    r_ctx = n_ctx * BLK
    n_lat = (R - r_ctx) // l_lat
    slot = jnp.concatenate([_moe_select(affT, 0, n_ctx, BLK), _moe_select(affT, r_ctx, n_lat, l_lat)], axis=1)
    slot3 = slot[:, None, :]
    aff3 = affT[:, None, :]
    xc, gc = _moe_gather(slot3, aff3, hn, 0, n_ctx, BLK)
    xl, gl = _moe_gather(slot3, aff3, hn, r_ctx, n_lat, l_lat)
    yc, yl = _moe_ffn(xc, xl, gc, gl, wg, wu, wd, layer)
    if final_g is None:
        x = _moe_scatter(slot3, yc, x, g2blk, 0, n_ctx, BLK)
        return _moe_scatter(slot3, yl, x, g2blk, r_ctx, n_lat, l_lat)
    return (_moe_scatter(slot3, yc, x, g2blk, 0, n_ctx, BLK, final_g),
            _moe_scatter(slot3, yl, x, g2blk, r_ctx, n_lat, l_lat, final_g))


L0_Z = (0, 1024)
L0_XBC = (1024, 2560)
L0_SHIFT = (2560, 6016)
L0_DT = (6016, 6144)
L0_COLS = 6144
L0_VMEM_BYTES = 56 * 1024 * 1024


def _sum_split(x, m, n_split):
    acc = None
    for _ in range(n_split):
        hi = x.astype(jnp.bfloat16)
        x = x - hi.astype(jnp.float32)
        t = jnp.dot(hi, m, preferred_element_type=jnp.float32)
        acc = t if acc is None else acc + t
    return acc


def _sum_split_left(m, x, n_split):
    acc = None
    for _ in range(n_split):
        hi = x.astype(jnp.bfloat16)
        x = x - hi.astype(jnp.float32)
        t = jnp.dot(m, hi, preferred_element_type=jnp.float32)
        acc = t if acc is None else acc + t
    return acc


def _head_sum(x, e_ref, et_ref):
    return _sum_split(_sum_split(x, e_ref[...], 2), et_ref[...], 2)


def _adaln(x, g, sc, sh):
    y = x * lax.rsqrt(jnp.mean(x * x, -1, keepdims=True) + NORM_EPS) * g
    return y * (1.0 + sc) + sh


def _softplus(x):
    return jnp.maximum(x, 0.0) + jnp.log(1.0 + jnp.exp(-jnp.abs(x)))


def _l0_in_kernel(tbl_ref, x_ref, xp_ref, xn_ref, g_ref, sc_ref, sh_ref, w_ref, mup_ref, mun_ref, rwp_ref,
                  w2_ref, a2_ref, g2_ref, e_ref, et_ref,
                  z_ref, xbc_ref, dt_ref, r_ref, v_ref, an_ref, lw_ref, kd_ref, bv_ref, gate_ref, bonus_ref):
    i = pl.program_id(0)
    f32, bf16 = jnp.float32, jnp.bfloat16
    g, sc, sh = g_ref[...], sc_ref[0], sh_ref[0]
    hn = _adaln(x_ref[...], g, sc, sh).astype(bf16)
    halo = _adaln(jnp.concatenate([xp_ref[...], xn_ref[...]], axis=0), g, sc, sh).astype(bf16)
    hn_halo = jnp.concatenate([hn, halo], axis=0)

    keep_prev = (1 - tbl_ref[0, i]).astype(f32)
    keep_next = (1 - tbl_ref[1, i]).astype(f32)
    row = lax.broadcasted_iota(jnp.int32, (BLK, 1), 0)
    c = RWKV_DIM

    def plain(c0, c1):
        return jnp.dot(hn, w_ref[:, c0:c1], preferred_element_type=f32)

    def proj(c0, c1):
        return jnp.dot(hn_halo, w_ref[:, L0_SHIFT[0] + c0:L0_SHIFT[0] + c1], preferred_element_type=f32)

    def shift(both, c0, c1):
        cur = both[:BLK]
        prev = jnp.where(row == 0, both[BLK + 7:BLK + 8] * keep_prev, pltpu.roll(cur, 1, 0))
        nxt = jnp.where(row == BLK - 1, both[BLK + 8:BLK + 9] * keep_next, pltpu.roll(cur, BLK - 1, 0))
        return cur + mup_ref[:, c0:c1] * (prev - cur) + mun_ref[:, c0:c1] * (nxt - cur)

    lo = 3 * c + 2 * W_LORA
    p_wl = proj(3 * c, lo)
    p_ag = proj(lo, lo + 256)
    p_k = proj(c, 2 * c)
    wl = shift(p_wl, 3 * c, lo)
    ag = shift(p_ag, lo, lo + 256)
    p_r = proj(0, c)
    k = shift(p_k, c, 2 * c)
    k_k, k_a, r_k = rwp_ref[0:1], rwp_ref[1:2], rwp_ref[2:3]
    w_lin = jnp.dot(jnp.tanh(wl).astype(bf16), w2_ref[...], preferred_element_type=f32)
    a_lora = jnp.dot(ag.astype(bf16), a2_ref[...], preferred_element_type=f32)
    gate_ref[...] = jnp.dot(jax.nn.sigmoid(ag).astype(bf16), g2_ref[...],
                            preferred_element_type=f32).astype(gate_ref.dtype)
    p_v = proj(2 * c, 3 * c)
    r = shift(p_r, 0, c)
    r_ref[...] = r.astype(r_ref.dtype)
    kk = k * k_k
    kk = kk * lax.rsqrt(_head_sum(kk * kk, e_ref, et_ref) + 1e-12)
    an_ref[...] = (-kk).astype(an_ref.dtype)
    z_ref[...] = plain(L0_Z[0], L0_Z[1]).astype(z_ref.dtype)
    v = shift(p_v, 2 * c, 3 * c)
    v_ref[...] = v.astype(v_ref.dtype)
    kd_sum = None
    xbc_cols = (L0_XBC[0], (L0_XBC[0] + L0_XBC[1]) // 2, L0_XBC[1])
    for d in range(2):
        xbc_ref[:, xbc_cols[d] - L0_XBC[0]:xbc_cols[d + 1] - L0_XBC[0]] = plain(xbc_cols[d], xbc_cols[d + 1])
        w_log = -_softplus(-(rwp_ref[3 + d:4 + d] + w_lin[:, d * c:(d + 1) * c])) - 0.5
        lw_ref[d] = -jnp.exp(w_log)
        a = jax.nn.sigmoid(rwp_ref[5 + d:6 + d] + a_lora)
        kd = k * (1.0 + (a - 1.0) * k_a)
        kd_ref[d] = kd.astype(kd_ref.dtype)
        bv_ref[d] = (kk * a).astype(bv_ref.dtype)
        kd_sum = kd if kd_sum is None else kd_sum + kd
    dt_ref[...] = plain(L0_DT[0], L0_DT[1])
    bonus_ref[...] = (_head_sum(r * kd_sum * r_k, e_ref, et_ref) * v).astype(bonus_ref.dtype)


def _seq_tables(n_ctx, n_lat, l_lat):
    nb = l_lat // BLK
    cond = [0] * n_ctx + [1 + s for s in range(n_lat) for _ in range(nb)]
    first = [1] * n_ctx + [1 if j == 0 else 0 for _ in range(n_lat) for j in range(nb)]
    last = [1] * n_ctx + [1 if j == nb - 1 else 0 for _ in range(n_lat) for j in range(nb)]
    return cond, first, last


def l0_in(x, g1, scb, shb, w_packed, mup, mun, rwp, w2bd, a2p, g2p, e_ind, et_ind, first, last):
    R, D = x.shape
    nblk = R // BLK
    n8 = R // 8
    tbl = jnp.asarray([first, last], jnp.int32)
    c = RWKV_DIM
    row = lambda i, t: (i, 0)
    full = lambda shape: pl.BlockSpec(shape, lambda i, t: (0,) * len(shape))
    rows = lambda n: pl.BlockSpec((BLK, n), row)
    rows2 = lambda n: pl.BlockSpec((2, BLK, n), lambda i, t: (0, i, 0))
    f32, act = jnp.float32, ACT_DTYPE
    sds = jax.ShapeDtypeStruct
    return pl.pallas_call(
        _l0_in_kernel,
        grid_spec=pltpu.PrefetchScalarGridSpec(
            num_scalar_prefetch=1,
            grid=(nblk,),
            in_specs=[rows(D),
                      pl.BlockSpec((8, D), lambda i, t: (jnp.maximum(i * (BLK // 8) - 1, 0), 0)),
                      pl.BlockSpec((8, D), lambda i, t: (jnp.minimum((i + 1) * (BLK // 8), n8 - 1), 0)),
                      full((1, D)),
                      pl.BlockSpec((1, 1, D), lambda i, t: (i, 0, 0)),
                      pl.BlockSpec((1, 1, D), lambda i, t: (i, 0, 0)),
                      full((D, L0_COLS)), full(mup.shape), full(mun.shape), full(rwp.shape),
                      full(w2bd.shape), full(a2p.shape), full(g2p.shape), full(e_ind.shape), full(et_ind.shape)],
            out_specs=[rows(c), rows(SSD_XBC), rows(128), rows(c), rows(c), rows(c),
                       rows2(c), rows2(c), rows2(c), rows(c), rows(c)]),
        out_shape=[sds((R, c), act), sds((R, SSD_XBC), f32), sds((R, 128), f32), sds((R, c), act), sds((R, c), act),
                   sds((R, c), act), sds((2, R, c), f32), sds((2, R, c), act), sds((2, R, c), act),
                   sds((R, c), act), sds((R, c), act)],
        compiler_params=pltpu.CompilerParams(dimension_semantics=("arbitrary",), vmem_limit_bytes=L0_VMEM_BYTES),
        name="l0_in",
    )(tbl, x, x, x, g1, scb, shb, w_packed, mup, mun, rwp, w2bd, a2p, g2p, e_ind, et_ind)


def l0_pack_weights(p, e):
    bf16 = jnp.bfloat16
    w = p['ab_w_in'][e]
    D = w.shape[0]
    c = RWKV_DIM
    rw0 = SSD_IN
    ag0 = rw0 + 3 * c + 2 * W_LORA
    w_packed = jnp.concatenate([
        w[:, :SSD_INNER + SSD_XBC], w[:, rw0:ag0], w[:, ag0:ag0 + A_LORA + G_LORA],
        jnp.zeros((D, 256 - A_LORA - G_LORA), w.dtype),
        w[:, SSD_INNER + SSD_XBC:SSD_IN], jnp.zeros((D, 128 - SSD_HEADS), w.dtype)], axis=1).astype(bf16)

    def pack_mu(mu):
        return jnp.concatenate([mu, jnp.zeros((256 - A_LORA - G_LORA,), mu.dtype)])[None, :]

    rwp = jnp.stack([p['rwkv_k_k'][e], p['rwkv_k_a'][e], p['rwkv_r_k'][e].reshape(-1), p['rwkv_w0'][e, 0],
                     p['rwkv_w0'][e, 1], p['rwkv_a0'][e, 0], p['rwkv_a0'][e, 1], jnp.zeros((c,), jnp.float32)])
    zw = jnp.zeros((W_LORA, c), jnp.float32)
    w2bd = jnp.concatenate([jnp.concatenate([p['rwkv_w2'][e, 0], zw], axis=1),
                            jnp.concatenate([zw, p['rwkv_w2'][e, 1]], axis=1)], axis=0).astype(bf16)
    a2p = jnp.concatenate([p['rwkv_a2'][e], jnp.zeros((256 - A_LORA, c), jnp.float32)], axis=0).astype(bf16)
    g2p = jnp.concatenate([jnp.zeros((A_LORA, c), jnp.float32), p['rwkv_g2'][e],
                           jnp.zeros((256 - A_LORA - G_LORA, c), jnp.float32)], axis=0).astype(bf16)
    head = jnp.arange(c) // RWKV_N
    e_ind = (head[:, None] == jnp.arange(128)[None, :]).astype(bf16)
    return w_packed, pack_mu(p['rwkv_mu_prev'][e]), pack_mu(p['rwkv_mu_next'][e]), rwp, w2bd, a2p, g2p, e_ind, e_ind.T


SSD_QH = 4
SSD_VMEM_BYTES = 48 * 1024 * 1024
NEG_BIG = -1e30
LOG2E = 1.4426950408889634


def _conv_silu(cur, prev_row, next_row, w_ref, b_ref):
    row = lax.broadcasted_iota(jnp.int32, (BLK, 1), 0)
    prev = jnp.where(row == 0, prev_row, pltpu.roll(cur, 1, 0))
    nxt = jnp.where(row == BLK - 1, next_row, pltpu.roll(cur, BLK - 1, 0))
    y = w_ref[0:1] * prev + w_ref[1:2] * cur + w_ref[2:3] * nxt + b_ref[...]
    return y * jax.nn.sigmoid(y)


def _ssd_kernel(tbl_ref, xs_ref, b_ref, c_ref, dt_ref, cwx_ref, cwb_ref, cwc_ref, cbx_ref, cbb_ref, cbc_ref,
                sel_ref, hp_ref, s0_ref, y_ref, fs_ref, xa_ref, ba_ref, ca_ref, sfx_ref, ldb_ref, st_ref, *, n_zero):
    f32, bf16 = jnp.float32, jnp.bfloat16
    sb = pl.program_id(0)
    sbr, qw = xs_ref.shape
    nch = sbr // BLK
    P, QH = SSD_P, SSD_QH
    t_i = lax.broadcasted_iota(jnp.int32, (BLK, BLK), 0)
    s_i = lax.broadcasted_iota(jnp.int32, (BLK, BLK), 1)
    lower = s_i <= t_i
    upper = s_i >= t_i
    tri_lo = jnp.where(lower, 1.0, 0.0).astype(bf16)
    tri_up = jnp.where(upper, 1.0, 0.0).astype(bf16)
    hp = hp_ref[0]
    sel = sel_ref[0]
    ind = jnp.where(lax.broadcasted_iota(jnp.int32, (128, qw), 1) // P == lax.broadcasted_iota(jnp.int32, (128, qw), 0),
                    1.0, 0.0).astype(bf16)
    head_of_lane = lax.broadcasted_iota(jnp.int32, (1, qw), 1) // P

    def expand(cols):
        return _sum_split(cols, ind, 2)

    d_row = expand(jnp.broadcast_to(hp[4:5], (8, 128)))[0:1]

    def chunk_rows(c):
        return pl.ds(pl.multiple_of(c * BLK, BLK), BLK)

    def neighbours(ref, c, keep_prev, keep_next):
        lo = jnp.maximum(c * BLK - 1, 0)
        hi = jnp.minimum((c + 1) * BLK, sbr - 1)
        return ref[pl.ds(lo, 1), :] * keep_prev, ref[pl.ds(hi, 1), :] * keep_next

    def fwd(c, carry):
        blk = sb * nch + c
        first, last = tbl_ref[0, blk], tbl_ref[1, blk]
        kp, kn = (1 - first).astype(f32), (1 - last).astype(f32)
        rows = chunk_rows(c)

        @pl.when(first == 1)
        def _():
            st_ref[0] = jnp.where(sb >= n_zero, s0_ref[0, 0], 0.0)

        xa = _conv_silu(xs_ref[rows, :], *neighbours(xs_ref, c, kp, kn), cwx_ref, cbx_ref)
        bm = _conv_silu(b_ref[rows, :], *neighbours(b_ref, c, kp, kn), cwb_ref, cbb_ref)
        cm = _conv_silu(c_ref[rows, :], *neighbours(c_ref, c, kp, kn), cwc_ref, cbc_ref)
        xb, bmb, cmb = xa.astype(bf16), bm.astype(bf16), cm.astype(bf16)
        xa_ref[rows, :] = xb
        ba_ref[rows, :] = bmb
        ca_ref[rows, :] = cmb
        dtq = _sum_split(dt_ref[rows, :], sel, 2)
        dtf = _softplus(dtq + hp[0:1])
        dtb = _softplus(dtq + hp[1:2])
        acs = _sum_split_left(tri_lo, dtf * hp[2:3], 3)
        sfx = _sum_split_left(tri_up, dtb * hp[3:4], 3)
        ldf, ldb = jnp.log(dtf), jnp.log(dtb)
        sfx_ref[rows, :] = sfx
        ldb_ref[rows, :] = ldb
        a2, s2 = acs * LOG2E, sfx * LOG2E
        a2r = (a2 - ldf * LOG2E).T
        s2r = (s2 - ldb * LOG2E).T
        g = lax.dot_general(cmb, bmb, (((1,), (1,)), ((), ())), preferred_element_type=f32)
        y_diag = None
        for j in range(QH):
            m = (g * (jnp.exp2(jnp.where(lower, a2[:, j:j + 1] - a2r[j:j + 1, :], NEG_BIG))
                      + jnp.exp2(jnp.where(upper, s2[:, j:j + 1] - s2r[j:j + 1, :], NEG_BIG)))).astype(bf16)
            xh = jnp.where(head_of_lane == j, xb, jnp.zeros((), bf16))
            t = jnp.dot(m, xh, preferred_element_type=f32)
            y_diag = t if y_diag is None else y_diag + t
        ea = jnp.exp(expand(acs))
        wf = jnp.exp(expand(acs[BLK - 1:BLK] - acs + ldf))
        s_in = st_ref[0]
        y_ref[rows, :] = xa * d_row + y_diag + ea * jnp.dot(cmb, s_in.astype(bf16), preferred_element_type=f32)
        st_ref[0] = ea[BLK - 1:BLK] * s_in + lax.dot_general(
            bmb, (xa * wf).astype(bf16), (((0,), (0,)), ((), ())), preferred_element_type=f32)
        fs_ref[c, 0] = st_ref[0]
        return carry

    lax.fori_loop(0, nch, fwd, 0)

    def bwd(k, carry):
        c = nch - 1 - k
        blk = sb * nch + c
        rows = chunk_rows(c)

        @pl.when(tbl_ref[1, blk] == 1)
        def _():
            st_ref[1] = jnp.where(sb >= n_zero, s0_ref[0, 1], 0.0)

        sfx = sfx_ref[rows, :]
        eb = jnp.exp(expand(sfx))
        wb = jnp.exp(expand(sfx[0:1] - sfx + ldb_ref[rows, :]))
        s_in = st_ref[1]
        y_ref[rows, :] += eb * jnp.dot(ca_ref[rows, :], s_in.astype(bf16), preferred_element_type=f32)
        st_ref[1] = eb[0:1] * s_in + lax.dot_general(
            ba_ref[rows, :], (xa_ref[rows, :].astype(f32) * wb).astype(bf16), (((0,), (0,)), ((), ())),
            preferred_element_type=f32)
        fs_ref[c, 1] = st_ref[1]
        return carry

    lax.fori_loop(0, nch, bwd, 0)


def ssd_scan(xbc, dt, conv_w, conv_b, sel, hp, s0, n_zero, first, last, sb_rows):
    R = xbc.shape[0]
    n_sb = R // sb_rows
    nch = sb_rows // BLK
    nq = SSD_HEADS // SSD_QH
    qw = SSD_QH * SSD_P
    qpg = SSD_HPG // SSD_QH
    b_blk = SSD_INNER // SSD_N
    c_blk = b_blk + SSD_GROUPS
    tbl = jnp.asarray([first, last], jnp.int32)
    cw = conv_w
    cb = conv_b.reshape(1, -1)
    f32, bf16 = jnp.float32, jnp.bfloat16
    return pl.pallas_call(
        functools.partial(_ssd_kernel, n_zero=n_zero),
        grid_spec=pltpu.PrefetchScalarGridSpec(
            num_scalar_prefetch=1,
            grid=(n_sb, nq),
            in_specs=[pl.BlockSpec((sb_rows, qw), lambda s, q, t: (s, q)),
                      pl.BlockSpec((sb_rows, SSD_N), lambda s, q, t: (s, b_blk + q // qpg)),
                      pl.BlockSpec((sb_rows, SSD_N), lambda s, q, t: (s, c_blk + q // qpg)),
                      pl.BlockSpec((sb_rows, 128), lambda s, q, t: (s, 0)),
                      pl.BlockSpec((3, qw), lambda s, q, t: (0, q)),
                      pl.BlockSpec((3, SSD_N), lambda s, q, t: (0, b_blk + q // qpg)),
                      pl.BlockSpec((3, SSD_N), lambda s, q, t: (0, c_blk + q // qpg)),
                      pl.BlockSpec((1, qw), lambda s, q, t: (0, q)),
                      pl.BlockSpec((1, SSD_N), lambda s, q, t: (0, b_blk + q // qpg)),
                      pl.BlockSpec((1, SSD_N), lambda s, q, t: (0, c_blk + q // qpg)),
                      pl.BlockSpec((1, 128, 128), lambda s, q, t: (q, 0, 0)),
                      pl.BlockSpec((1, 8, 128), lambda s, q, t: (q, 0, 0)),
                      pl.BlockSpec((1, 2, SSD_N, qw), lambda s, q, t: (jnp.maximum(s - n_zero, 0), 0, 0, q))],
            out_specs=[pl.BlockSpec((sb_rows, qw), lambda s, q, t: (s, q)),
                       pl.BlockSpec((nch, 2, SSD_N, qw), lambda s, q, t: (s, 0, 0, q))],
            scratch_shapes=[pltpu.VMEM((sb_rows, qw), bf16), pltpu.VMEM((sb_rows, SSD_N), bf16),
                            pltpu.VMEM((sb_rows, SSD_N), bf16), pltpu.VMEM((sb_rows, 128), f32),
                            pltpu.VMEM((sb_rows, 128), f32), pltpu.VMEM((2, SSD_N, qw), f32)]),
        out_shape=[jax.ShapeDtypeStruct((R, SSD_INNER), f32),
                   jax.ShapeDtypeStruct((R // BLK, 2, SSD_N, SSD_INNER), f32)],
        compiler_params=pltpu.CompilerParams(dimension_semantics=("arbitrary", "arbitrary"),
                                             vmem_limit_bytes=SSD_VMEM_BYTES),
        name="ssd_scan",
    )(tbl, xbc, xbc, xbc, dt, cw, cw, cw, cb, cb, cb, sel, hp, s0)


def ssd_tables(p, e):
    nq = SSD_HEADS // SSD_QH
    lane = jnp.arange(128)
    sel = jnp.stack([(lane[:, None] == (q * SSD_QH + lane[None, :])) & (lane[None, :] < SSD_QH)
                     for q in range(nq)]).astype(jnp.bfloat16)
    a_neg = -jnp.exp(p['ssd_a_log'][e].astype(jnp.float32))
    rows = jnp.stack([p['ssd_dt_bias'][e, 0], p['ssd_dt_bias'][e, 1], a_neg[0], a_neg[1], p['ssd_d'][e]])
    hp = jnp.zeros((nq, 8, 128), jnp.float32)
    hp = hp.at[:, :5, :SSD_QH].set(jnp.transpose(rows.reshape(5, nq, SSD_QH), (1, 0, 2)))
    return sel, hp


MIX_VMEM_BYTES = 48 * 1024 * 1024
MIX_BLOCKS = 1
ROUTER_LANES = 128


def _residual_norm_router(x, out, g1, n2g, sc2, sh2, rw_ref, x_out_ref, hn_ref, aff_ref):
    rows, d = x.shape
    nb = g1.shape[0]
    x_new = x.reshape(nb, rows // nb, d) + g1 * out.reshape(nb, rows // nb, d)
    x_out_ref[...] = x_new.reshape(rows, d)
    hn = _adaln(x_new, n2g, sc2, sh2).reshape(rows, d)
    hn_ref[...] = hn.astype(hn_ref.dtype)
    logits = _dot(hn, rw_ref[...], ((1,), (0,)), 3)
    lane = lax.broadcasted_iota(jnp.int32, logits.shape, 1)
    logits = jnp.where(lane < N_EXPERTS, logits, NEG_BIG)
    ex = jnp.exp(logits - jnp.max(logits, axis=-1, keepdims=True))
    aff = ex / jnp.sum(ex, axis=-1, keepdims=True)
    aff_ref[...] = aff.T[:N_EXPERTS]


def _l0_out_kernel(ys_ref, z_ref, yf_ref, yb_ref, bonus_ref, gate_ref, sg_ref, lnw_ref, lnb_ref, e_ref, et_ref, w_ref,
                   x_ref, g1_ref, n2g_ref, sc2_ref, sh2_ref, rw_ref, x_out_ref, hn_ref, aff_ref):
    f32, bf16 = jnp.float32, jnp.bfloat16
    z = z_ref[...].astype(f32)
    ys = ys_ref[...] * (z * jax.nn.sigmoid(z))
    gw = SSD_INNER // SSD_GROUPS
    parts = []
    for gi in range(SSD_GROUPS):
        yg = ys[:, gi * gw:(gi + 1) * gw]
        parts.append(yg * lax.rsqrt(jnp.mean(yg * yg, -1, keepdims=True) + NORM_EPS))
    a1 = jnp.concatenate(parts, axis=1) * sg_ref[...]
    o = yf_ref[...].astype(f32) + yb_ref[...].astype(f32)
    mu = _head_sum(o, e_ref, et_ref) * (1.0 / RWKV_N)
    oc = o - mu
    var = _head_sum(oc * oc, e_ref, et_ref) * (1.0 / RWKV_N)
    o = oc * lax.rsqrt(var + RWKV_GN_EPS) * lnw_ref[...] + lnb_ref[...]
    o = (o + bonus_ref[...].astype(f32)) * gate_ref[...].astype(f32)
    out = (jnp.dot(a1.astype(bf16), w_ref[:SSD_INNER], preferred_element_type=f32)
           + jnp.dot(o.astype(bf16), w_ref[SSD_INNER:], preferred_element_type=f32))
    _residual_norm_router(x_ref[...], out, g1_ref[...], n2g_ref[...], sc2_ref[...], sh2_ref[...], rw_ref,
                          x_out_ref, hn_ref, aff_ref)


def _l1_out_kernel(a_ref, w_ref, x_ref, g1_ref, n2g_ref, sc2_ref, sh2_ref, rw_ref, x_out_ref, hn_ref, aff_ref):
    out = jnp.dot(a_ref[...].astype(jnp.bfloat16), w_ref[...], preferred_element_type=jnp.float32)
    _residual_norm_router(x_ref[...], out, g1_ref[...], n2g_ref[...], sc2_ref[...], sh2_ref[...], rw_ref,
                          x_out_ref, hn_ref, aff_ref)


def _mix_out_call(kernel_fn, name, lead_args, lead_specs, w_out, x, g1b, n2g, sc2b, sh2b, router_w):
    R, D = x.shape
    mb = MIX_BLOCKS
    assert (R // BLK) % mb == 0
    full = lambda a: pl.BlockSpec(a.shape, lambda i: (0,) * a.ndim)
    blkrow = pl.BlockSpec((mb, 1, D), lambda i: (i, 0, 0))
    rw = jnp.zeros((D, ROUTER_LANES), jnp.float32).at[:, :N_EXPERTS].set(router_w)
    args = list(lead_args) + [w_out, x, g1b, n2g, sc2b, sh2b, rw]
    x_idx = len(lead_args) + 1
    in_specs = list(lead_specs) + [full(w_out), pl.BlockSpec((mb * BLK, D), lambda i: (i, 0)), blkrow, full(n2g), blkrow,
                                   blkrow, full(rw)]
    return pl.pallas_call(
        kernel_fn,
        grid=(R // (mb * BLK),),
        in_specs=in_specs,
        out_specs=[pl.BlockSpec((mb * BLK, D), lambda i: (i, 0)), pl.BlockSpec((mb * BLK, D), lambda i: (i, 0)),
                   pl.BlockSpec((N_EXPERTS, mb * BLK), lambda i: (0, i))],
        out_shape=[jax.ShapeDtypeStruct((R, D), jnp.float32), jax.ShapeDtypeStruct((R, D), jnp.bfloat16),
                   jax.ShapeDtypeStruct((N_EXPERTS, R), jnp.float32)],
        input_output_aliases={x_idx: 0},
        compiler_params=pltpu.CompilerParams(dimension_semantics=("arbitrary",), vmem_limit_bytes=MIX_VMEM_BYTES),
        name=name,
    )(*args)


def l0_out(ys, z, yf, yb, bonus, gate, ssd_g, ln_w, ln_b, e_ind, et_ind, w_out, x, g1b, n2g, sc2b, sh2b, router_w):
    c = RWKV_DIM
    rows = lambda n: pl.BlockSpec((MIX_BLOCKS * BLK, n), lambda i: (i, 0))
    full = lambda a: pl.BlockSpec(a.shape, lambda i: (0,) * a.ndim)
    lead = [ys, z, yf, yb, bonus, gate, ssd_g, ln_w, ln_b, e_ind, et_ind]
    specs = [rows(SSD_INNER), rows(SSD_INNER), rows(c), rows(c), rows(c), rows(c),
             full(ssd_g), full(ln_w), full(ln_b), full(e_ind), full(et_ind)]
    return _mix_out_call(_l0_out_kernel, "l0_out", lead, specs, w_out, x, g1b, n2g, sc2b, sh2b, router_w)


def l1_out(a, w_out, x, g1b, n2g, sc2b, sh2b, router_w):
    specs = [pl.BlockSpec((MIX_BLOCKS * BLK, a.shape[1]), lambda i: (i, 0))]
    return _mix_out_call(_l1_out_kernel, "l1_out", [a], specs, w_out, x, g1b, n2g, sc2b, sh2b, router_w)


RET_HPS = 2


def _l1_in_kernel(tbl_ref, x_ref, g_ref, sc_ref, sh_ref, w_ref, cos_ref, sin_ref, q_ref, k_ref, v_ref, gg_ref):
    f32 = jnp.float32
    hn = _adaln(x_ref[...], g_ref[...], sc_ref[0], sh_ref[0]).astype(jnp.bfloat16)
    cosf, sinf = cos_ref[0], sin_ref[0]

    def rope(x):
        parts = []
        for h in range(RET_HEADS):
            xh = x[:, h * RET_DK:(h + 1) * RET_DK]
            parts.append(xh * cosf + pltpu.roll(xh, RET_DK // 2, 1) * sinf)
        return jnp.concatenate(parts, axis=1)

    q_ref[...] = rope(jnp.dot(hn, w_ref[:, :RET_QK], preferred_element_type=f32)).astype(q_ref.dtype)
    k_ref[...] = (rope(jnp.dot(hn, w_ref[:, RET_QK:2 * RET_QK], preferred_element_type=f32))
                  * (RET_DK ** -0.5)).astype(k_ref.dtype)
    v_ref[...] = jnp.dot(hn, w_ref[:, 2 * RET_QK:2 * RET_QK + RET_V], preferred_element_type=f32).astype(v_ref.dtype)
    gg_ref[...] = jnp.dot(hn, w_ref[:, 2 * RET_QK + RET_V:], preferred_element_type=f32).astype(gg_ref.dtype)


def l1_in(x, g1, scb, shb, w_bf16, cos_t, sin_t, rope_blk):
    R, D = x.shape
    nblk = R // BLK
    tbl = jnp.asarray([rope_blk], jnp.int32)
    f32 = jnp.float32
    row = lambda n: pl.BlockSpec((BLK, n), lambda i, t: (i, 0))
    full = lambda a: pl.BlockSpec(a.shape, lambda i, t: (0,) * a.ndim)
    blkrow = pl.BlockSpec((1, 1, D), lambda i, t: (i, 0, 0))
    ropespec = pl.BlockSpec((1, BLK, RET_DK), lambda i, t: (t[0, i], 0, 0))
    return pl.pallas_call(
        _l1_in_kernel,
        grid_spec=pltpu.PrefetchScalarGridSpec(
            num_scalar_prefetch=1, grid=(nblk,),
            in_specs=[row(D), full(g1), blkrow, blkrow, full(w_bf16), ropespec, ropespec],
            out_specs=[row(RET_QK), row(RET_QK), row(RET_V), row(RET_V)]),
        out_shape=[jax.ShapeDtypeStruct((R, RET_QK), ACT_DTYPE), jax.ShapeDtypeStruct((R, RET_QK), ACT_DTYPE),
                   jax.ShapeDtypeStruct((R, RET_V), ACT_DTYPE), jax.ShapeDtypeStruct((R, RET_V), ACT_DTYPE)],
        compiler_params=pltpu.CompilerParams(dimension_semantics=("arbitrary",), vmem_limit_bytes=L0_VMEM_BYTES),
        name="l1_in",
    )(tbl, x, g1, scb, shb, w_bf16, cos_t, sin_t)


def _ret_kernel(tbl_ref, q_ref, k_ref, v_ref, g_ref, lg_ref, nw_ref, nb_ref, s0_ref, a_ref, fs_ref, st_ref, *,
                n_zero):
    f32, bf16 = jnp.float32, jnp.bfloat16
    sb = pl.program_id(0)
    nch = q_ref.shape[0] // BLK
    heads = range(RET_HPS)
    ks = [slice(h * RET_DK, (h + 1) * RET_DK) for h in heads]
    vs = [slice(h * RET_DV, (h + 1) * RET_DV) for h in heads]
    lgf = [lg_ref[h, 0:1, 0:1] for h in heads]
    lgb = [lg_ref[h, 1:2, 0:1] for h in heads]
    t_i = lax.broadcasted_iota(jnp.int32, (BLK, BLK), 0)
    s_i = lax.broadcasted_iota(jnp.int32, (BLK, BLK), 1)
    dist = (t_i - s_i).astype(f32)
    dm = [jnp.exp(jnp.where(s_i <= t_i, dist * lgf[h], NEG_BIG)) + jnp.exp(jnp.where(s_i >= t_i, -dist * lgb[h], NEG_BIG))
          for h in heads]
    tk = lax.broadcasted_iota(jnp.int32, (BLK, RET_DK), 0).astype(f32)
    tv = lax.broadcasted_iota(jnp.int32, (BLK, RET_DV), 0).astype(f32)
    k_to_end_f = [jnp.exp((BLK - 1.0 - tk) * lgf[h]) for h in heads]
    k_to_end_b = [jnp.exp(tk * lgb[h]) for h in heads]
    from_start_f = [jnp.exp((tv + 1.0) * lgf[h]) for h in heads]
    from_start_b = [jnp.exp((BLK - tv) * lgb[h]) for h in heads]
    nt, tn = (((1,), (1,)), ((), ())), (((0,), (0,)), ((), ()))

    def chunk_rows(c):
        return pl.ds(pl.multiple_of(c * BLK, BLK), BLK)

    def fwd(c, carry):
        blk = sb * nch + c
        rows = chunk_rows(c)

        @pl.when(tbl_ref[0, blk] == 1)
        def _():
            st_ref[0] = jnp.where(sb >= n_zero, s0_ref[0, 0], 0.0)

        q = [q_ref[rows, ks[h]].astype(bf16) for h in heads]
        k = [k_ref[rows, ks[h]] for h in heads]
        v = [v_ref[rows, vs[h]].astype(bf16) for h in heads]
        s_in = [st_ref[0, h] for h in heads]
        g = [lax.dot_general(q[h], k[h].astype(bf16), nt, preferred_element_type=f32) for h in heads]
        y_diag = [jnp.dot((g[h] * dm[h]).astype(bf16), v[h], preferred_element_type=f32) for h in heads]
        y_off = [jnp.dot(q[h], s_in[h].astype(bf16), preferred_element_type=f32) for h in heads]
        kw = [(k[h].astype(f32) * k_to_end_f[h]).astype(bf16) for h in heads]
        upd = [lax.dot_general(kw[h], v[h], tn, preferred_element_type=f32) for h in heads]
        for h in heads:
            a_ref[rows, vs[h]] = y_diag[h] + from_start_f[h] * y_off[h]
            st_ref[0, h] = jnp.exp(BLK * lgf[h]) * s_in[h] + upd[h]
        fs_ref[c, 0] = st_ref[0]
        return carry

    lax.fori_loop(0, nch, fwd, 0)

    def bwd(j, carry):
        c = nch - 1 - j
        blk = sb * nch + c
        rows = chunk_rows(c)

        @pl.when(tbl_ref[1, blk] == 1)
        def _():
            st_ref[1] = jnp.where(sb >= n_zero, s0_ref[0, 1], 0.0)

        q = [q_ref[rows, ks[h]].astype(bf16) for h in heads]
        v = [v_ref[rows, vs[h]].astype(bf16) for h in heads]
        s_in = [st_ref[1, h] for h in heads]
        y_off = [jnp.dot(q[h], s_in[h].astype(bf16), preferred_element_type=f32) for h in heads]
        kw = [(k_ref[rows, ks[h]].astype(f32) * k_to_end_b[h]).astype(bf16) for h in heads]
        upd = [lax.dot_general(kw[h], v[h], tn, preferred_element_type=f32) for h in heads]
        for h in heads:
            st_ref[1, h] = jnp.exp(BLK * lgb[h]) * s_in[h] + upd[h]
            y = a_ref[rows, vs[h]] + from_start_b[h] * y_off[h]
            mu = jnp.mean(y, -1, keepdims=True)
            yc = y - mu
            var = jnp.mean(yc * yc, -1, keepdims=True)
            gg = g_ref[rows, vs[h]].astype(f32)
            a_ref[rows, vs[h]] = ((yc * lax.rsqrt(var + 1e-5) * nw_ref[:, vs[h]] + nb_ref[:, vs[h]])
                                  * (gg * jax.nn.sigmoid(gg)))
        fs_ref[c, 1] = st_ref[1]
        return carry

    lax.fori_loop(0, nch, bwd, 0)


def ret_scan(q, k, v, g, lg_tab, norm_w, norm_b, s0, n_zero, first, last, sb_rows):
    R = q.shape[0]
    n_sb = R // sb_rows
    nch = sb_rows // BLK
    tbl = jnp.asarray([first, last], jnp.int32)
    f32 = jnp.float32
    hps = RET_HPS
    return pl.pallas_call(
        functools.partial(_ret_kernel, n_zero=n_zero),
        grid_spec=pltpu.PrefetchScalarGridSpec(
            num_scalar_prefetch=1, grid=(n_sb, RET_HEADS // hps),
            in_specs=[pl.BlockSpec((sb_rows, hps * RET_DK), lambda s, h, t: (s, h)),
                      pl.BlockSpec((sb_rows, hps * RET_DK), lambda s, h, t: (s, h)),
                      pl.BlockSpec((sb_rows, hps * RET_DV), lambda s, h, t: (s, h)),
                      pl.BlockSpec((sb_rows, hps * RET_DV), lambda s, h, t: (s, h)),
                      pl.BlockSpec((hps, 8, 128), lambda s, h, t: (h, 0, 0)),
                      pl.BlockSpec((1, hps * RET_DV), lambda s, h, t: (0, h)),
                      pl.BlockSpec((1, hps * RET_DV), lambda s, h, t: (0, h)),
                      pl.BlockSpec((1, 2, hps, RET_DK, RET_DV), lambda s, h, t: (jnp.maximum(s - n_zero, 0), 0, h, 0, 0))],
            out_specs=[pl.BlockSpec((sb_rows, hps * RET_DV), lambda s, h, t: (s, h)),
                       pl.BlockSpec((nch, 2, hps, RET_DK, RET_DV), lambda s, h, t: (s, 0, h, 0, 0))],
            scratch_shapes=[pltpu.VMEM((2, hps, RET_DK, RET_DV), f32)]),
        out_shape=[jax.ShapeDtypeStruct((R, RET_V), f32),
                   jax.ShapeDtypeStruct((R // BLK, 2, RET_HEADS, RET_DK, RET_DV), f32)],
        compiler_params=pltpu.CompilerParams(dimension_semantics=("arbitrary", "arbitrary"),
                                             vmem_limit_bytes=SSD_VMEM_BYTES),
        name="ret_scan",
    )(tbl, q, k, v, g, lg_tab, norm_w.reshape(1, -1), norm_b.reshape(1, -1), s0)


def _mod_kernel(c_ref, w_ref, b_ref, o_ref):
    c = c_ref[...]
    act = c * jax.nn.sigmoid(c)
    o_ref[0] = _dot(act, w_ref[0], ((1,), (0,)), 3) + b_ref[0]


def mod_vectors(conds, mod_w, mod_b):
    depth, D, n6 = mod_w.shape
    tn = D
    return pl.pallas_call(
        _mod_kernel,
        grid=(depth, n6 // tn),
        in_specs=[pl.BlockSpec(conds.shape, lambda i, j: (0, 0)),
                  pl.BlockSpec((1, D, tn), lambda i, j: (i, 0, j)),
                  pl.BlockSpec((1, 1, tn), lambda i, j: (i, 0, j))],
        out_specs=pl.BlockSpec((1, conds.shape[0], tn), lambda i, j: (i, 0, j)),
        out_shape=jax.ShapeDtypeStruct((depth, conds.shape[0], n6), jnp.float32),
        name="mod_vectors",
    )(conds, mod_w, mod_b.reshape(depth, 1, n6))


def rope_tables(n_tokens):
    rows = n_tokens // GRID_W
    row = np.repeat(np.arange(rows), GRID_W).astype(np.float64)
    col = np.tile(np.arange(GRID_W), rows).astype(np.float64)
    n_f = RET_DK // 4
    inv = ROPE_BASE ** (-np.arange(n_f, dtype=np.float64) / n_f)
    ang = np.concatenate([row[:, None] * inv, col[:, None] * inv], -1)
    return np.cos(ang), np.sin(ang)


def _rope_block_tables(n_ctx, n_lat, l_lat):
    nb = l_lat // BLK
    cos, sin = rope_tables(l_lat)
    cosf = np.concatenate([cos, cos], -1).reshape(nb, BLK, RET_DK)
    sinf = np.concatenate([-sin, sin], -1).reshape(nb, BLK, RET_DK)
    cos_t = jnp.asarray(np.concatenate([np.ones((1, BLK, RET_DK)), cosf]), jnp.float32)
    sin_t = jnp.asarray(np.concatenate([np.zeros((1, BLK, RET_DK)), sinf]), jnp.float32)
    rope_blk = [0] * n_ctx + [1 + j for _ in range(n_lat) for j in range(nb)]
    return cos_t, sin_t, rope_blk


def kernel(x_prompt, x_sample, state_ssd, state_rwkv, state_ret, c, c_ctx, mod_w, mod_b, norm1_g, norm2_g,
           router_w, exp_w_gate, exp_w_up, exp_w_down, ab_w_in, ab_w_out, ssd_conv_w, ssd_conv_b, ssd_dt_bias,
           ssd_a_log, ssd_d, ssd_norm_g, rwkv_mu_prev, rwkv_mu_next, rwkv_w0, rwkv_w2, rwkv_a0, rwkv_a2, rwkv_g2,
           rwkv_k_k, rwkv_k_a, rwkv_r_k, rwkv_ln_w, rwkv_ln_b, ret_w_in, ret_w_out, ret_decay_logit, ret_norm_w,
           ret_norm_b, final_norm_g):
    p = dict(mod_w=mod_w, mod_b=mod_b, norm1_g=norm1_g, norm2_g=norm2_g, router_w=router_w,
             exp_w_gate=exp_w_gate, exp_w_up=exp_w_up, exp_w_down=exp_w_down, ab_w_in=ab_w_in, ab_w_out=ab_w_out,
             ssd_conv_w=ssd_conv_w, ssd_conv_b=ssd_conv_b, ssd_dt_bias=ssd_dt_bias, ssd_a_log=ssd_a_log,
             ssd_d=ssd_d, ssd_norm_g=ssd_norm_g, rwkv_mu_prev=rwkv_mu_prev, rwkv_mu_next=rwkv_mu_next,
             rwkv_w0=rwkv_w0, rwkv_w2=rwkv_w2, rwkv_a0=rwkv_a0, rwkv_a2=rwkv_a2, rwkv_g2=rwkv_g2,
             rwkv_k_k=rwkv_k_k, rwkv_k_a=rwkv_k_a, rwkv_r_k=rwkv_r_k, rwkv_ln_w=rwkv_ln_w, rwkv_ln_b=rwkv_ln_b,
             ret_w_in=ret_w_in, ret_w_out=ret_w_out, ret_decay_logit=ret_decay_logit, ret_norm_w=ret_norm_w,
             ret_norm_b=ret_norm_b, final_norm_g=final_norm_g)
    f32, bf16 = jnp.float32, jnp.bfloat16
    n_ctx, l_ctx, D = x_prompt.shape
    n_lat, l_lat, _ = x_sample.shape
    assert l_ctx == BLK and l_lat % BLK == 0 and (n_ctx * BLK) % l_lat == 0
    n_sb_ctx = n_ctx * BLK // l_lat
    cond_id, first, last = _seq_tables(n_ctx, n_lat, l_lat)
    x = jnp.concatenate([x_prompt.reshape(-1, D), x_sample.reshape(-1, D)])

    conds = jnp.concatenate([c_ctx[None, :], c, jnp.zeros((8 - 1 - n_lat, D), f32)])
    mods = mod_vectors(conds, mod_w, mod_b)[:, jnp.asarray(cond_id)]
    mods = mods.reshape(DEPTH, len(cond_id), 6, 1, D)

    new_ssd, new_rwkv, new_ret = [], [], []
    out = None
    for i in range(DEPTH):
        sh1, sc1, g1, sh2, sc2, g2 = (mods[i, :, k] for k in range(6))
        e = i // 2
        if i % 2 == 0:
            w_packed, mup, mun, rwp, w2bd, a2p, g2p, e_ind, et_ind = l0_pack_weights(p, e)
            z, xbc, dt, r, v, an, lw, kd, bv, gate, bonus = l0_in(
                x, norm1_g[i][None], sc1, sh1, w_packed, mup, mun, rwp, w2bd, a2p, g2p, e_ind, et_ind, first, last)
            sel, hp = ssd_tables(p, e)
            s0_ssd = jnp.transpose(state_ssd[:, e], (0, 1, 3, 2, 4)).reshape(n_lat, 2, SSD_N, SSD_INNER)
            ys, fs_ssd = ssd_scan(xbc, dt, ssd_conv_w[e], ssd_conv_b[e], sel, hp,
                                  s0_ssd, n_sb_ctx, first, last, l_lat)
            new_ssd.append(jnp.transpose(fs_ssd[:n_ctx].reshape(n_ctx, 2, SSD_N, SSD_HEADS, SSD_P), (0, 1, 3, 2, 4)))
            s0_rwkv = jnp.transpose(state_rwkv[:, e], (0, 1, 3, 2, 4)).reshape(n_lat, 2, RWKV_N, RWKV_DIM)
            yf, yb, sf_rwkv = rwkv_scan_pallas(r, v, an, lw, kd, bv, s0_rwkv, n_ctx,
                                               _rwkv_steps(n_ctx, n_lat, l_lat))
            new_rwkv.append(jnp.transpose(sf_rwkv[:n_ctx].reshape(n_ctx, 2, RWKV_N, RWKV_HEADS, RWKV_N),
                                          (0, 1, 3, 2, 4)))
            x, hn2, affT = l0_out(ys, z, yf, yb, bonus, gate, ssd_norm_g[e][None], rwkv_ln_w[e][None], rwkv_ln_b[e][None],
                                  e_ind, et_ind, ab_w_out[e].astype(bf16), x, g1, norm2_g[i][None], sc2, sh2,
                                  router_w[i])
        else:
            cos_t, sin_t, rope_blk = _rope_block_tables(n_ctx, n_lat, l_lat)
            q, k, v, gg = l1_in(x, norm1_g[i][None], sc1, sh1, ret_w_in[e].astype(bf16), cos_t, sin_t, rope_blk)
            lg = jax.nn.log_sigmoid(ret_decay_logit[e].astype(f32))
            lg_tab = jnp.zeros((RET_HEADS, 8, 128), f32).at[:, :2, :].set(jnp.transpose(lg)[:, :, None])
            a, fs_ret = ret_scan(q, k, v, gg, lg_tab, ret_norm_w[e], ret_norm_b[e],
                                 state_ret[:, e], n_sb_ctx, first, last, l_lat)
            new_ret.append(fs_ret[:n_ctx])
            x, hn2, affT = l1_out(a, ret_w_out[e].astype(bf16), x, g1, norm2_g[i][None], sc2, sh2, router_w[i])
        fin = final_norm_g if i == DEPTH - 1 else None
        out = moe_layer(x, hn2, affT, g2, exp_w_gate, exp_w_up, exp_w_down, i, n_ctx, l_lat, final_g=fin)
        if fin is None:
            x = out
    y_ctx, y_lat = out
    return (y_ctx.reshape(n_ctx, l_ctx, D), y_lat.reshape(n_lat, l_lat, D),
            jnp.stack(new_ssd, 1), jnp.stack(new_rwkv, 1), jnp.stack(new_ret, 1))
```

```python
import functools
import math

import jax
import jax.numpy as jnp
import numpy as np
from jax import lax
from jax.experimental import pallas as pl
from jax.experimental.pallas import tpu as pltpu

D_MODEL = 1024
DEPTH = 2
GRID_W = 64
CHUNK = 128
NORM_EPS = 1e-6
SSD_HEADS = 16
SSD_P = 64
SSD_INNER = SSD_HEADS * SSD_P
SSD_GROUPS = 2
SSD_HPG = SSD_HEADS // SSD_GROUPS
SSD_N = 128
SSD_XBC = SSD_INNER + 2 * SSD_GROUPS * SSD_N
SSD_IN = SSD_INNER + SSD_XBC + SSD_HEADS
RWKV_HEADS = 16
RWKV_N = 64
RWKV_DIM = RWKV_HEADS * RWKV_N
W_LORA = 64
A_LORA = 64
G_LORA = 128
RWKV_GN_EPS = 64e-5
RET_HEADS = 8
RET_DK = 128
RET_DV = 256
RET_QK = RET_HEADS * RET_DK
RET_V = RET_HEADS * RET_DV
ROPE_BASE = 10000.0
N_EXPERTS = 16
EC_CAPACITY = 2

ACT_DTYPE = jnp.bfloat16

RWKV_C = 64
RWKV_GH = 4
RWKV_GL = RWKV_GH * RWKV_N
RWKV_DOUBLING_PASSES = (3, 3, 3, 3, 1, 1)


def _split_bf16(x):
    hi = x.astype(jnp.bfloat16)
    lo = (x - hi.astype(jnp.float32)).astype(jnp.bfloat16)
    return hi, lo


def _dot(a, b, dims, passes):
    f = functools.partial(lax.dot_general, dimension_numbers=(dims, ((), ())),
                          preferred_element_type=jnp.float32)
    if passes == 1:
        return f(a.astype(jnp.bfloat16), b.astype(jnp.bfloat16))
    ah, al = _split_bf16(a)
    bh, bl = _split_bf16(b)
    return f(ah, bh) + (f(ah, bl) + f(al, bh))


def _rwkv_chunk_kernel(tbl_ref, r0_ref, v0_ref, a0_ref, r1_ref, v1_ref, a1_ref, lw0_ref, k0_ref, b0_ref,
                       lw1_ref, k1_ref, b1_ref, s0_ref, y0_ref, y1_ref, sf_ref, h_ref, *, p_inv, p_oth, n_zero):
    C, N, GH, GL = RWKV_C, RWKV_N, RWKV_GH, RWKV_GL
    i = pl.program_id(0)
    f32, bf16 = jnp.float32, jnp.bfloat16

    @pl.when(tbl_ref[3, i] == 1)
    def _():
        h_ref[...] = jnp.where(tbl_ref[2, i] >= n_zero, s0_ref[0], 0.0)

    t_i = lax.broadcasted_iota(jnp.int32, (C, GL), 0)
    s_i = lax.broadcasted_iota(jnp.int32, (C, GL), 1) & (N - 1)
    eye = (s_i == t_i).astype(f32)
    row2 = lax.broadcasted_iota(jnp.int32, (2 * C, GL), 0)
    rel2 = (lax.broadcasted_iota(jnp.int32, (2 * C, GL), 1) & (N - 1)) - (row2 & (C - 1))
    incl2 = row2 // C
    mask2 = [rel2 - incl2 < 0, -rel2 - incl2 < 0]
    bh_r = lax.broadcasted_iota(jnp.int32, (GL, GL), 0) // N
    bh_c = lax.broadcasted_iota(jnp.int32, (GL, GL), 1) // N
    blk = bh_r == bh_c
    tt = lax.broadcasted_iota(jnp.int32, (C, C), 0)
    ss = lax.broadcasted_iota(jnp.int32, (C, C), 1)
    tri = [(ss <= tt).astype(bf16), (ss >= tt).astype(bf16)]

    def bd(x, passes):
        pieces = []
        for _ in range(2 if passes == 3 else 1):
            hi = x.astype(bf16)
            x = x - hi.astype(f32)
            pieces.append(jnp.where(blk, jnp.concatenate([hi] * GH, axis=0), jnp.zeros((), bf16)))
        return pieces

    def mm(l, x, passes, dims=((1,), (0,))):
        f = functools.partial(lax.dot_general, dimension_numbers=(dims, ((), ())), preferred_element_type=f32)
        xs = bd(x, passes)
        lh = l.astype(bf16)
        if passes == 1:
            return f(lh, xs[0])
        ll = (l - lh.astype(f32)).astype(bf16)
        m = l.shape[0]
        both = f(jnp.concatenate([lh, ll], axis=0), xs[0])
        if passes == 2:
            return both[:m] + both[m:]
        return both[:m] + (f(lh, xs[1]) + both[m:])

    nt = ((1,), (1,))
    refs = [(r0_ref, v0_ref, a0_ref, lw0_ref, k0_ref, b0_ref), (r1_ref, v1_ref, a1_ref, lw1_ref, k1_ref, b1_ref)]
    lw, r_t, a_t, b_t, k_t, v = [], [], [], [], [], []
    for d, (r_ref, v_ref, a_ref, lw_ref, k_ref, b_ref) in enumerate(refs):
        lwd = lw_ref[0]
        lw_hi, lw_lo = _split_bf16(lwd)
        cum = (jnp.dot(tri[d], lw_hi, preferred_element_type=f32) + jnp.dot(tri[d], lw_lo, preferred_element_type=f32))
        w_inv = jnp.exp(-cum)
        lw.append(lwd)
        r_t.append(r_ref[...].astype(f32) * jnp.exp(cum))
        a_t.append(a_ref[...].astype(f32) * jnp.exp(cum - lwd))
        b_t.append(b_ref[0].astype(f32) * w_inv)
        k_t.append(k_ref[0].astype(f32) * w_inv)
        v.append(v_ref[...].astype(f32))
    lane_head = lax.broadcasted_iota(jnp.int32, (N, GL), 1) // N

    chains = [(d, slice(g * GL, (g + 1) * GL)) for d in range(2) for g in range(RWKV_HEADS // GH)]
    each = lambda fn: [fn(j, d, sl) for j, (d, sl) in enumerate(chains)]
    bg = each(lambda j, d, sl: b_t[d][:, sl])
    kg = each(lambda j, d, sl: k_t[d][:, sl])
    vg = each(lambda j, d, sl: v[d][:, sl])
    h0 = each(lambda j, d, sl: h_ref[d, :, sl])
    ar = each(lambda j, d, sl: jnp.concatenate([a_t[d][:, sl], r_t[d][:, sl]], axis=0))
    m_b = each(lambda j, d, sl: jnp.where(mask2[d], mm(ar[j], bg[j], p_oth, nt), 0.0))
    m_k = each(lambda j, d, sl: jnp.where(mask2[d], mm(ar[j], kg[j], p_oth, nt), 0.0))
    p = each(lambda j, d, sl: mm(m_b[j][:C], m_b[j][:C], p_inv[0]))
    tmat = each(lambda j, d, sl: eye + m_b[j][:C])
    for lv in range(int(math.log2(C)) - 2):
        pt = each(lambda j, d, sl: mm(jnp.concatenate([p[j], tmat[j]], axis=0), p[j], p_inv[1 + lv]))
        p = each(lambda j, d, sl: pt[j][:C])
        tmat = each(lambda j, d, sl: tmat[j] + pt[j][C:])
    tmat = each(lambda j, d, sl: tmat[j] + mm(tmat[j], p[j], p_inv[-1]))
    ar_h = each(lambda j, d, sl: mm(ar[j], h0[j], p_oth, nt))
    mk_v = each(lambda j, d, sl: mm(m_k[j], vg[j], p_oth))
    u = each(lambda j, d, sl: mm(tmat[j], ar_h[j][:C] + mk_v[j][:C], p_oth))
    y = each(lambda j, d, sl: ar_h[j][C:] + mm(m_b[j][C:], u[j], p_oth) + mk_v[j][C:])
    full = each(lambda j, d, sl: _dot(jnp.concatenate([u[j], vg[j]], axis=0), jnp.concatenate([bg[j], kg[j]], axis=0),
                                      ((0,), (0,)), p_oth))
    y_refs = (y0_ref, y1_ref)
    for j, (d, sl) in enumerate(chains):
        y_refs[d][:, sl] = y[j].astype(y_refs[d].dtype)
        z = jnp.zeros((N, GL), f32)
        for hh in range(GH):
            z = z + jnp.where(lane_head == hh, full[j][hh * N:(hh + 1) * N], 0.0)
        w_tot = jnp.exp(jnp.sum(lw[d][:, sl], axis=0, keepdims=True))
        h_ref[d, :, sl] = w_tot * (h0[j] + z)

    @pl.when(tbl_ref[4, i] == 1)
    def _():
        sf_ref[0] = h_ref[...]


def _rwkv_steps(n_ctx, n_lat, l_lat):
    C = RWKV_C
    rows = []
    seqs = [(s, s * BLK, BLK) for s in range(n_ctx)] + [(n_ctx + s, n_ctx * BLK + s * l_lat, l_lat) for s in range(n_lat)]
    for sid, row0, length in seqs:
        nc = length // C
        for j in range(nc):
            rows.append((row0 // C + j, row0 // C + nc - 1 - j, sid, int(j == 0), int(j == nc - 1)))
    return [list(col) for col in zip(*rows)]


def rwkv_scan_pallas(r, v, a, lw, k, b, s0, n_zero, steps, p_inv=RWKV_DOUBLING_PASSES, p_oth=1):
    R, HN = r.shape
    C, N = RWKV_C, RWKV_N
    tbl = jnp.asarray(steps, jnp.int32)
    fwd = pl.BlockSpec((C, HN), lambda i, t: (t[0, i], 0))
    bwd = pl.BlockSpec((C, HN), lambda i, t: (t[1, i], 0))
    fwd_d = pl.BlockSpec((1, C, HN), lambda i, t: (0, t[0, i], 0))
    bwd_d = pl.BlockSpec((1, C, HN), lambda i, t: (1, t[1, i], 0))
    n_seq = max(steps[2]) + 1
    st_in = pl.BlockSpec((1, 2, N, HN), lambda i, t: (jnp.maximum(t[2, i] - n_zero, 0), 0, 0, 0))
    st = pl.BlockSpec((1, 2, N, HN), lambda i, t: (t[2, i], 0, 0, 0))
    return pl.pallas_call(
        functools.partial(_rwkv_chunk_kernel, p_inv=p_inv, p_oth=p_oth, n_zero=n_zero),
        grid_spec=pltpu.PrefetchScalarGridSpec(
            num_scalar_prefetch=1,
            grid=(len(steps[0]),),
            in_specs=[fwd, fwd, fwd, bwd, bwd, bwd, fwd_d, fwd_d, fwd_d, bwd_d, bwd_d, bwd_d, st_in],
            out_specs=[fwd, bwd, st],
            scratch_shapes=[pltpu.VMEM((2, N, HN), jnp.float32)]),
        out_shape=[jax.ShapeDtypeStruct((R, HN), ACT_DTYPE), jax.ShapeDtypeStruct((R, HN), ACT_DTYPE),
                   jax.ShapeDtypeStruct((n_seq, 2, N, HN), jnp.float32)],
        compiler_params=pltpu.CompilerParams(dimension_semantics=("arbitrary",)),
        name="rwkv_scan",
    )(tbl, r, v, a, r, v, a, lw, k, b, lw, k, b, s0)


BLK = 256
FF_TILE = 768
SELECT_TILE = 512
SCATTER_WINDOW = 96
SLOT_ALIGN = 16
SELECT_MIN_EXP = -1100.0
SELECT_BINADE_STEPS = 11
SELECT_MANTISSA_STEPS = 40
MOE_FFN_VMEM_BYTES = 48 * 1024 * 1024


def _moe_select_kernel(aff_ref, slot_ref, *, cap):
    a = aff_ref[...]
    E, T = a.shape
    f32 = jnp.float32

    def enough(piv):
        return jnp.sum(jnp.where(a >= piv, 1.0, 0.0), axis=1, keepdims=True) >= cap

    def binade(_, lohi):
        e_lo, e_hi = lohi
        mid = jnp.floor((e_lo + e_hi) * 0.5)
        ok = enough(jnp.exp2(mid))
        return jnp.where(ok, mid, e_lo), jnp.where(ok, e_hi, mid)

    e_lo, e_hi = lax.fori_loop(0, SELECT_BINADE_STEPS, binade,
                               (jnp.full((E, 1), SELECT_MIN_EXP, f32), jnp.full((E, 1), 1.0, f32)))

    def inside(_, lohi):
        lo, hi = lohi
        mid = lo + (hi - lo) * 0.5
        ok = enough(mid)
        return jnp.where(ok, mid, lo), jnp.where(ok, hi, mid)

    thr, _ = lax.fori_loop(0, SELECT_MANTISSA_STEPS, inside, (jnp.exp2(e_lo), jnp.exp2(e_hi)))
    gt = a > thr
    eq = a == thr
    need = cap - jnp.sum(jnp.where(gt, 1.0, 0.0), axis=1, keepdims=True)
    tw = min(T, SELECT_TILE)

    def prefix_count(mask):
        m = jnp.where(mask, 1.0, 0.0).astype(jnp.bfloat16)
        outs = []
        for j in range(T // tw):
            s_i = lax.broadcasted_iota(jnp.int32, (T, tw), 0)
            t_i = lax.broadcasted_iota(jnp.int32, (T, tw), 1) + j * tw
            before = jnp.where(s_i < t_i, 1.0, 0.0).astype(jnp.bfloat16)
            outs.append(jnp.dot(m, before, preferred_element_type=f32))
        return outs[0] if len(outs) == 1 else jnp.concatenate(outs, axis=1)

    sel = gt | (eq & (prefix_count(eq) < need))
    slot_ref[...] = jnp.where(sel, prefix_count(sel).astype(jnp.int32), -1)


def _moe_select(affT, row0, n_seq, t):
    E = affT.shape[0]
    rows = jnp.transpose(affT[:, row0:row0 + n_seq * t].reshape(E, n_seq, t), (1, 0, 2)).reshape(n_seq * E, t)
    slot = pl.pallas_call(
        functools.partial(_moe_select_kernel, cap=EC_CAPACITY * t // N_EXPERTS),
        grid=(1,),
        in_specs=[pl.BlockSpec((n_seq * E, t), lambda s: (0, 0))],
        out_specs=pl.BlockSpec((n_seq * E, t), lambda s: (0, 0)),
        out_shape=jax.ShapeDtypeStruct((n_seq * E, t), jnp.int32),
        name="moe_select",
    )(rows)
    return jnp.transpose(slot.reshape(n_seq, E, t), (1, 0, 2)).reshape(E, n_seq * t)


def _moe_gather_kernel(tbl_ref, slot_ref, aff_ref, hn_ref, xe_ref, gate_ref, *, cap, win, nb):
    E = slot_ref.shape[0]
    s = pl.program_id(0)
    w_i = lax.broadcasted_iota(jnp.int32, (win, BLK), 0)
    xe_ref[...] = jnp.zeros_like(xe_ref)
    gate_ref[...] = jnp.zeros_like(gate_ref)
    for j in range(nb):
        blk = s * nb + j
        toks = slice(j * BLK, (j + 1) * BLK)
        hn_blk = hn_ref[toks, :]

        def window(w, carry):
            hits, starts = [], []
            for e in range(E):
                lo = tbl_ref[e, blk] + w * win
                start = pl.multiple_of(jnp.minimum(lo, cap - win), SLOT_ALIGN)
                slot = slot_ref[e, :, toks]
                slot = jnp.where(slot >= lo, slot, -1)
                hits.append(slot == w_i + start)
                starts.append(start)
            onehot = jnp.concatenate([jnp.where(h, 1.0, 0.0) for h in hits], axis=0).astype(jnp.bfloat16)
            rows = jnp.dot(onehot, hn_blk, preferred_element_type=jnp.float32)
            for e in range(E):
                dst = pl.ds(starts[e], win)
                xe_ref[e, dst, :] += rows[e * win:(e + 1) * win].astype(xe_ref.dtype)
                g = jnp.sum(jnp.where(hits[e], aff_ref[e, :, toks], 0.0), axis=1, keepdims=True)
                gate_ref[e, dst, :] += jnp.broadcast_to(g, (win, 128))
            return carry

        lax.fori_loop(0, tbl_ref[E, blk], window, 0)


def _moe_gather(slot, slot3, aff3, hn, row0, n_seq, t):
    E = slot3.shape[0]
    D = hn.shape[1]
    cap = EC_CAPACITY * t // N_EXPERTS
    win = min(cap, SCATTER_WINDOW)
    nb = t // BLK
    b0 = row0 // t
    tbl = _scatter_windows(slot, row0, n_seq, t, win)
    return pl.pallas_call(
        functools.partial(_moe_gather_kernel, cap=cap, win=win, nb=nb),
        grid_spec=pltpu.PrefetchScalarGridSpec(
            num_scalar_prefetch=1, grid=(n_seq,),
            in_specs=[pl.BlockSpec((E, 1, t), lambda s, tb_: (0, 0, b0 + s)),
                      pl.BlockSpec((E, 1, t), lambda s, tb_: (0, 0, b0 + s)),
                      pl.BlockSpec((t, D), lambda s, tb_: (b0 + s, 0))],
            out_specs=[pl.BlockSpec((E, cap, D), lambda s, tb_: (0, s, 0)),
                       pl.BlockSpec((E, cap, 128), lambda s, tb_: (0, s, 0))]),
        out_shape=[jax.ShapeDtypeStruct((E, n_seq * cap, D), jnp.bfloat16),
                   jax.ShapeDtypeStruct((E, n_seq * cap, 128), jnp.float32)],
        compiler_params=pltpu.CompilerParams(dimension_semantics=("arbitrary",),
                                             vmem_limit_bytes=MOE_FFN_VMEM_BYTES),
        name="moe_gather",
    )(tbl, slot3, aff3, hn)


def _moe_ffn_kernel(xc_ref, xl_ref, gc_ref, gl_ref, wg_ref, wu_ref, wd_ref, yc_ref, yl_ref, accc_ref, accl_ref):
    f = pl.program_id(1)
    nf = pl.num_programs(1)
    bf16 = jnp.bfloat16
    wg = wg_ref[0, 0].astype(bf16)
    wu = wu_ref[0, 0].astype(bf16)
    wd = wd_ref[0, 0].astype(bf16)

    def part(x_ref, acc_ref):
        x = x_ref[0]
        g = jnp.dot(x, wg, preferred_element_type=jnp.float32)
        u = jnp.dot(x, wu, preferred_element_type=jnp.float32)
        h = (g * jax.nn.sigmoid(g) * u).astype(bf16)
        y = jnp.dot(h, wd, preferred_element_type=jnp.float32)

        @pl.when(f == 0)
        def _():
            acc_ref[...] = y

        @pl.when(f != 0)
        def _():
            acc_ref[...] += y

    part(xc_ref, accc_ref)
    part(xl_ref, accl_ref)

    @pl.when(f == nf - 1)
    def _():
        for acc_ref, g_ref, y_ref in ((accc_ref, gc_ref, yc_ref), (accl_ref, gl_ref, yl_ref)):
            gate = jnp.concatenate([g_ref[0]] * (acc_ref.shape[1] // 128), axis=1)
            y_ref[0] = (acc_ref[...] * gate).astype(y_ref.dtype)


def _moe_ffn(xc, xl, gc, gl, wg, wu, wd, layer):
    E, nc_rows, D = xc.shape
    nl_rows = xl.shape[1]
    F = wg.shape[3]
    nf = F // FF_TILE
    return pl.pallas_call(
        _moe_ffn_kernel,
        grid=(E, nf),
        in_specs=[pl.BlockSpec((1, nc_rows, D), lambda e, f: (e, 0, 0)),
                  pl.BlockSpec((1, nl_rows, D), lambda e, f: (e, 0, 0)),
                  pl.BlockSpec((1, nc_rows, 128), lambda e, f: (e, 0, 0)),
                  pl.BlockSpec((1, nl_rows, 128), lambda e, f: (e, 0, 0)),
                  pl.BlockSpec((1, 1, D, FF_TILE), lambda e, f: (layer, e, 0, f)),
                  pl.BlockSpec((1, 1, D, FF_TILE), lambda e, f: (layer, e, 0, f)),
                  pl.BlockSpec((1, 1, FF_TILE, D), lambda e, f: (layer, e, f, 0))],
        out_specs=[pl.BlockSpec((1, nc_rows, D), lambda e, f: (e, 0, 0)),
                   pl.BlockSpec((1, nl_rows, D), lambda e, f: (e, 0, 0))],
        out_shape=[jax.ShapeDtypeStruct((E, nc_rows, D), jnp.bfloat16),
                   jax.ShapeDtypeStruct((E, nl_rows, D), jnp.bfloat16)],
        scratch_shapes=[pltpu.VMEM((nc_rows, D), jnp.float32), pltpu.VMEM((nl_rows, D), jnp.float32)],
        compiler_params=pltpu.CompilerParams(dimension_semantics=("arbitrary", "arbitrary"),
                                             vmem_limit_bytes=MOE_FFN_VMEM_BYTES),
        name="moe_ffn",
    )(xc, xl, gc, gl, wg, wu, wd)


def _moe_scatter_kernel(tbl_ref, slot_ref, ye_ref, x_ref, g2_ref, *rest, cap, win, nb, final):
    if final:
        fg_ref, o_ref, acc_ref = rest
    else:
        o_ref, acc_ref = rest
    E = ye_ref.shape[0]
    tb = x_ref.shape[0]
    blk = pl.program_id(0) * nb + pl.program_id(1)
    w_i = lax.broadcasted_iota(jnp.int32, (win, tb), 0)
    acc_ref[...] = jnp.zeros_like(acc_ref)

    def window(w, carry):
        hot, rows = [], []
        for e in range(E):
            lo = tbl_ref[e, blk] + w * win
            start = pl.multiple_of(jnp.minimum(lo, cap - win), SLOT_ALIGN)
            slot = slot_ref[e]
            slot = jnp.where(slot >= lo, slot, -1)
            hot.append(jnp.where(slot == w_i + start, 1.0, 0.0))
            rows.append(ye_ref[e, pl.ds(start, win), :])
        onehot = jnp.concatenate(hot, axis=0).astype(jnp.bfloat16)
        acc_ref[...] += lax.dot_general(onehot, jnp.concatenate(rows, axis=0), (((0,), (0,)), ((), ())),
                                        preferred_element_type=jnp.float32)
        return carry

    lax.fori_loop(0, tbl_ref[E, blk], window, 0)
    x = x_ref[...] + g2_ref[0] * acc_ref[...]
    if final:
        x = x * lax.rsqrt(jnp.mean(x * x, -1, keepdims=True) + NORM_EPS) * fg_ref[...]
    o_ref[...] = x


def _scatter_windows(slot, row0, n_seq, t, win):
    E = slot.shape[0]
    cap = EC_CAPACITY * t // N_EXPERTS
    nb = t // BLK
    cnt = jnp.sum(slot[:, row0:row0 + n_seq * t].reshape(E, n_seq, nb, BLK) >= 0, axis=3)
    first = jnp.cumsum(cnt, axis=2) - cnt
    start = jnp.minimum(first // SLOT_ALIGN * SLOT_ALIGN, cap - win)
    n_win = jnp.maximum(jnp.max((first + cnt - start + win - 1) // win, axis=0), 1)
    return jnp.concatenate([start.reshape(E, n_seq * nb), n_win.reshape(1, n_seq * nb)]).astype(jnp.int32)


def _moe_scatter(slot, slot3, ye, x, g2blk, row0, n_seq, t, final_g=None):
    E, _, D = ye.shape
    cap = EC_CAPACITY * t // N_EXPERTS
    win = min(cap, SCATTER_WINDOW)
    nb = t // BLK
    b0 = row0 // BLK
    final = final_g is not None
    tbl = _scatter_windows(slot, row0, n_seq, t, win)
    in_specs = [pl.BlockSpec((E, 1, BLK), lambda s, j, tb_: (0, 0, b0 + s * nb + j)),
                pl.BlockSpec((E, cap, D), lambda s, j, tb_: (0, s, 0)),
                pl.BlockSpec((BLK, D), lambda s, j, tb_: (b0 + s * nb + j, 0)),
                pl.BlockSpec((1, 1, D), lambda s, j, tb_: (b0 + s * nb + j, 0, 0))]
    args = [tbl, slot3, ye, x, g2blk]
    if final:
        in_specs.append(pl.BlockSpec((1, D), lambda s, j, tb_: (0, 0)))
        args.append(final_g.reshape(1, D))
        out_specs = pl.BlockSpec((BLK, D), lambda s, j, tb_: (s * nb + j, 0))
        out_shape = jax.ShapeDtypeStruct((n_seq * t, D), jnp.float32)
        aliases = {}
    else:
        out_specs = pl.BlockSpec((BLK, D), lambda s, j, tb_: (b0 + s * nb + j, 0))
        out_shape = jax.ShapeDtypeStruct(x.shape, jnp.float32)
        aliases = {3: 0}
    return pl.pallas_call(
        functools.partial(_moe_scatter_kernel, cap=cap, win=win, nb=nb, final=final),
        grid_spec=pltpu.PrefetchScalarGridSpec(
            num_scalar_prefetch=1, grid=(n_seq, nb), in_specs=in_specs, out_specs=out_specs,
            scratch_shapes=[pltpu.VMEM((BLK, D), jnp.float32)]),
        out_shape=out_shape,
        input_output_aliases=aliases,
        compiler_params=pltpu.CompilerParams(dimension_semantics=("arbitrary", "arbitrary"),
                                             vmem_limit_bytes=MOE_FFN_VMEM_BYTES),
        name="moe_scatter",
    )(*args)


def moe_layer(x, hn, affT, g2blk, wg, wu, wd, layer, n_ctx, l_lat, final_g=None):
    R = x.shape[0]
    r_ctx = n_ctx * BLK
    n_lat = (R - r_ctx) // l_lat
    slot = jnp.concatenate([_moe_select(affT, 0, n_ctx, BLK), _moe_select(affT, r_ctx, n_lat, l_lat)], axis=1)
    slot3 = slot[:, None, :]
    aff3 = affT[:, None, :]
    xc, gc = _moe_gather(slot, slot3, aff3, hn, 0, n_ctx, BLK)
    xl, gl = _moe_gather(slot, slot3, aff3, hn, r_ctx, n_lat, l_lat)
    yc, yl = _moe_ffn(xc, xl, gc, gl, wg, wu, wd, layer)
    if final_g is None:
        x = _moe_scatter(slot, slot3, yc, x, g2blk, 0, n_ctx, BLK)
        return _moe_scatter(slot, slot3, yl, x, g2blk, r_ctx, n_lat, l_lat)
    return (_moe_scatter(slot, slot3, yc, x, g2blk, 0, n_ctx, BLK, final_g),
            _moe_scatter(slot, slot3, yl, x, g2blk, r_ctx, n_lat, l_lat, final_g))


L0_Z = (0, 1024)
L0_XBC = (1024, 2560)
L0_SHIFT = (2560, 6016)
L0_DT = (6016, 6144)
L0_COLS = 6144
L0_VMEM_BYTES = 56 * 1024 * 1024


def _sum_split(x, m, n_split):
    acc = None
    for _ in range(n_split):
        hi = x.astype(jnp.bfloat16)
        x = x - hi.astype(jnp.float32)
        t = jnp.dot(hi, m, preferred_element_type=jnp.float32)
        acc = t if acc is None else acc + t
    return acc


def _sum_split_left(m, x, n_split):
    acc = None
    for _ in range(n_split):
        hi = x.astype(jnp.bfloat16)
        x = x - hi.astype(jnp.float32)
        t = jnp.dot(m, hi, preferred_element_type=jnp.float32)
        acc = t if acc is None else acc + t
    return acc


def _head_sum(x, e_ref, et_ref):
    return _sum_split(_sum_split(x, e_ref[...], 2), et_ref[...], 2)


def _adaln(x, g, sc, sh):
    y = x * lax.rsqrt(jnp.mean(x * x, -1, keepdims=True) + NORM_EPS) * g
    return y * (1.0 + sc) + sh


def _softplus(x):
    return jnp.maximum(x, 0.0) + jnp.log(1.0 + jnp.exp(-jnp.abs(x)))


def _l0_in_kernel(tbl_ref, x_ref, xp_ref, xn_ref, g_ref, sc_ref, sh_ref, w_ref, mup_ref, mun_ref, rwp_ref,
                  w2_ref, a2_ref, g2_ref, e_ref, et_ref,
                  z_ref, xbc_ref, dt_ref, r_ref, v_ref, an_ref, lw_ref, kd_ref, bv_ref, gate_ref, bonus_ref):
    i = pl.program_id(0)
    f32, bf16 = jnp.float32, jnp.bfloat16
    g, sc, sh = g_ref[...], sc_ref[0], sh_ref[0]
    hn = _adaln(x_ref[...], g, sc, sh).astype(bf16)
    halo = _adaln(jnp.concatenate([xp_ref[...], xn_ref[...]], axis=0), g, sc, sh).astype(bf16)
    hn_halo = jnp.concatenate([hn, halo], axis=0)

    keep_prev = (1 - tbl_ref[0, i]).astype(f32)
    keep_next = (1 - tbl_ref[1, i]).astype(f32)
    row = lax.broadcasted_iota(jnp.int32, (BLK, 1), 0)
    c = RWKV_DIM

    def plain(c0, c1):
        return jnp.dot(hn, w_ref[:, c0:c1], preferred_element_type=f32)

    def proj(c0, c1):
        return jnp.dot(hn_halo, w_ref[:, L0_SHIFT[0] + c0:L0_SHIFT[0] + c1], preferred_element_type=f32)

    def shift(both, c0, c1):
        cur = both[:BLK]
        prev = jnp.where(row == 0, both[BLK + 7:BLK + 8] * keep_prev, pltpu.roll(cur, 1, 0))
        nxt = jnp.where(row == BLK - 1, both[BLK + 8:BLK + 9] * keep_next, pltpu.roll(cur, BLK - 1, 0))
        return cur + mup_ref[:, c0:c1] * (prev - cur) + mun_ref[:, c0:c1] * (nxt - cur)

    lo = 3 * c + 2 * W_LORA
    p_wl = proj(3 * c, lo)
    p_ag = proj(lo, lo + 256)
    p_k = proj(c, 2 * c)
    wl = shift(p_wl, 3 * c, lo)
    ag = shift(p_ag, lo, lo + 256)
    p_r = proj(0, c)
    k = shift(p_k, c, 2 * c)
    k_k, k_a, r_k = rwp_ref[0:1], rwp_ref[1:2], rwp_ref[2:3]
    w_lin = jnp.dot(jnp.tanh(wl).astype(bf16), w2_ref[...], preferred_element_type=f32)
    a_lora = jnp.dot(ag.astype(bf16), a2_ref[...], preferred_element_type=f32)
    gate_ref[...] = jnp.dot(jax.nn.sigmoid(ag).astype(bf16), g2_ref[...],
                            preferred_element_type=f32).astype(gate_ref.dtype)
    p_v = proj(2 * c, 3 * c)
    r = shift(p_r, 0, c)
    r_ref[...] = r.astype(r_ref.dtype)
    kk = k * k_k
    kk = kk * lax.rsqrt(_head_sum(kk * kk, e_ref, et_ref) + 1e-12)
    an_ref[...] = (-kk).astype(an_ref.dtype)
    z_ref[...] = plain(L0_Z[0], L0_Z[1]).astype(z_ref.dtype)
    v = shift(p_v, 2 * c, 3 * c)
    v_ref[...] = v.astype(v_ref.dtype)
    kd_sum = None
    xbc_cols = (L0_XBC[0], (L0_XBC[0] + L0_XBC[1]) // 2, L0_XBC[1])
    for d in range(2):
        xbc_ref[:, xbc_cols[d] - L0_XBC[0]:xbc_cols[d + 1] - L0_XBC[0]] = plain(xbc_cols[d], xbc_cols[d + 1])
        w_log = -_softplus(-(rwp_ref[3 + d:4 + d] + w_lin[:, d * c:(d + 1) * c])) - 0.5
        lw_ref[d] = -jnp.exp(w_log)
        a = jax.nn.sigmoid(rwp_ref[5 + d:6 + d] + a_lora)
        kd = k * (1.0 + (a - 1.0) * k_a)
        kd_ref[d] = kd.astype(kd_ref.dtype)
        bv_ref[d] = (kk * a).astype(bv_ref.dtype)
        kd_sum = kd if kd_sum is None else kd_sum + kd
    dt_ref[...] = plain(L0_DT[0], L0_DT[1])
    bonus_ref[...] = (_head_sum(r * kd_sum * r_k, e_ref, et_ref) * v).astype(bonus_ref.dtype)


def _seq_tables(n_ctx, n_lat, l_lat):
    nb = l_lat // BLK
    cond = [0] * n_ctx + [1 + s for s in range(n_lat) for _ in range(nb)]
    first = [1] * n_ctx + [1 if j == 0 else 0 for _ in range(n_lat) for j in range(nb)]
    last = [1] * n_ctx + [1 if j == nb - 1 else 0 for _ in range(n_lat) for j in range(nb)]
    return cond, first, last


def l0_in(x, g1, scb, shb, w_packed, mup, mun, rwp, w2bd, a2p, g2p, e_ind, et_ind, first, last):
    R, D = x.shape
    nblk = R // BLK
    n8 = R // 8
    tbl = jnp.asarray([first, last], jnp.int32)
    c = RWKV_DIM
    row = lambda i, t: (i, 0)
    full = lambda shape: pl.BlockSpec(shape, lambda i, t: (0,) * len(shape))
    rows = lambda n: pl.BlockSpec((BLK, n), row)
    rows2 = lambda n: pl.BlockSpec((2, BLK, n), lambda i, t: (0, i, 0))
    f32, act = jnp.float32, ACT_DTYPE
    sds = jax.ShapeDtypeStruct
    return pl.pallas_call(
        _l0_in_kernel,
        grid_spec=pltpu.PrefetchScalarGridSpec(
            num_scalar_prefetch=1,
            grid=(nblk,),
            in_specs=[rows(D),
                      pl.BlockSpec((8, D), lambda i, t: (jnp.maximum(i * (BLK // 8) - 1, 0), 0)),
                      pl.BlockSpec((8, D), lambda i, t: (jnp.minimum((i + 1) * (BLK // 8), n8 - 1), 0)),
                      full((1, D)),
                      pl.BlockSpec((1, 1, D), lambda i, t: (i, 0, 0)),
                      pl.BlockSpec((1, 1, D), lambda i, t: (i, 0, 0)),
                      full((D, L0_COLS)), full(mup.shape), full(mun.shape), full(rwp.shape),
                      full(w2bd.shape), full(a2p.shape), full(g2p.shape), full(e_ind.shape), full(et_ind.shape)],
            out_specs=[rows(c), rows(SSD_XBC), rows(128), rows(c), rows(c), rows(c),
                       rows2(c), rows2(c), rows2(c), rows(c), rows(c)]),
        out_shape=[sds((R, c), act), sds((R, SSD_XBC), f32), sds((R, 128), f32), sds((R, c), act), sds((R, c), act),
                   sds((R, c), act), sds((2, R, c), f32), sds((2, R, c), act), sds((2, R, c), act),
                   sds((R, c), act), sds((R, c), act)],
        compiler_params=pltpu.CompilerParams(dimension_semantics=("arbitrary",), vmem_limit_bytes=L0_VMEM_BYTES),
        name="l0_in",
    )(tbl, x, x, x, g1, scb, shb, w_packed, mup, mun, rwp, w2bd, a2p, g2p, e_ind, et_ind)


def l0_pack_weights(p, e):
    bf16 = jnp.bfloat16
    w = p['ab_w_in'][e]
    D = w.shape[0]
    c = RWKV_DIM
    rw0 = SSD_IN
    ag0 = rw0 + 3 * c + 2 * W_LORA
    w_packed = jnp.concatenate([
        w[:, :SSD_INNER + SSD_XBC], w[:, rw0:ag0], w[:, ag0:ag0 + A_LORA + G_LORA],
        jnp.zeros((D, 256 - A_LORA - G_LORA), w.dtype),
        w[:, SSD_INNER + SSD_XBC:SSD_IN], jnp.zeros((D, 128 - SSD_HEADS), w.dtype)], axis=1).astype(bf16)

    def pack_mu(mu):
        return jnp.concatenate([mu, jnp.zeros((256 - A_LORA - G_LORA,), mu.dtype)])[None, :]

    rwp = jnp.stack([p['rwkv_k_k'][e], p['rwkv_k_a'][e], p['rwkv_r_k'][e].reshape(-1), p['rwkv_w0'][e, 0],
                     p['rwkv_w0'][e, 1], p['rwkv_a0'][e, 0], p['rwkv_a0'][e, 1], jnp.zeros((c,), jnp.float32)])
    zw = jnp.zeros((W_LORA, c), jnp.float32)
    w2bd = jnp.concatenate([jnp.concatenate([p['rwkv_w2'][e, 0], zw], axis=1),
                            jnp.concatenate([zw, p['rwkv_w2'][e, 1]], axis=1)], axis=0).astype(bf16)
    a2p = jnp.concatenate([p['rwkv_a2'][e], jnp.zeros((256 - A_LORA, c), jnp.float32)], axis=0).astype(bf16)
    g2p = jnp.concatenate([jnp.zeros((A_LORA, c), jnp.float32), p['rwkv_g2'][e],
                           jnp.zeros((256 - A_LORA - G_LORA, c), jnp.float32)], axis=0).astype(bf16)
    head = jnp.arange(c) // RWKV_N
    e_ind = (head[:, None] == jnp.arange(128)[None, :]).astype(bf16)
    return w_packed, pack_mu(p['rwkv_mu_prev'][e]), pack_mu(p['rwkv_mu_next'][e]), rwp, w2bd, a2p, g2p, e_ind, e_ind.T


SSD_QH = 4
SSD_VMEM_BYTES = 48 * 1024 * 1024
NEG_BIG = -1e30
LOG2E = 1.4426950408889634


def _conv_silu(cur, prev_row, next_row, w_ref, b_ref):
    row = lax.broadcasted_iota(jnp.int32, (BLK, 1), 0)
    prev = jnp.where(row == 0, prev_row, pltpu.roll(cur, 1, 0))
    nxt = jnp.where(row == BLK - 1, next_row, pltpu.roll(cur, BLK - 1, 0))
    y = w_ref[0:1] * prev + w_ref[1:2] * cur + w_ref[2:3] * nxt + b_ref[...]
    return y * jax.nn.sigmoid(y)


def _ssd_kernel(tbl_ref, xs_ref, b_ref, c_ref, dt_ref, cwx_ref, cwb_ref, cwc_ref, cbx_ref, cbb_ref, cbc_ref,
                sel_ref, hp_ref, s0_ref, y_ref, fs_ref, xa_ref, ba_ref, ca_ref, sfx_ref, ldb_ref, st_ref, *, n_zero):
    f32, bf16 = jnp.float32, jnp.bfloat16
    sb = pl.program_id(0)
    sbr, qw = xs_ref.shape
    nch = sbr // BLK
    P, QH = SSD_P, SSD_QH
    t_i = lax.broadcasted_iota(jnp.int32, (BLK, BLK), 0)
    s_i = lax.broadcasted_iota(jnp.int32, (BLK, BLK), 1)
    lower = s_i <= t_i
    upper = s_i >= t_i
    tri_lo = jnp.where(lower, 1.0, 0.0).astype(bf16)
    tri_up = jnp.where(upper, 1.0, 0.0).astype(bf16)
    hp = hp_ref[0]
    sel = sel_ref[0]
    ind = jnp.where(lax.broadcasted_iota(jnp.int32, (128, qw), 1) // P == lax.broadcasted_iota(jnp.int32, (128, qw), 0),
                    1.0, 0.0).astype(bf16)
    head_of_lane = lax.broadcasted_iota(jnp.int32, (1, qw), 1) // P

    def expand(cols):
        return _sum_split(cols, ind, 2)

    d_row = expand(jnp.broadcast_to(hp[4:5], (8, 128)))[0:1]

    def chunk_rows(c):
        return pl.ds(pl.multiple_of(c * BLK, BLK), BLK)

    def neighbours(ref, c, keep_prev, keep_next):
        lo = jnp.maximum(c * BLK - 1, 0)
        hi = jnp.minimum((c + 1) * BLK, sbr - 1)
        return ref[pl.ds(lo, 1), :] * keep_prev, ref[pl.ds(hi, 1), :] * keep_next

    def fwd(c, carry):
        blk = sb * nch + c
        first, last = tbl_ref[0, blk], tbl_ref[1, blk]
        kp, kn = (1 - first).astype(f32), (1 - last).astype(f32)
        rows = chunk_rows(c)

        @pl.when(first == 1)
        def _():
            st_ref[0] = jnp.where(sb >= n_zero, s0_ref[0, 0], 0.0)

        xa = _conv_silu(xs_ref[rows, :], *neighbours(xs_ref, c, kp, kn), cwx_ref, cbx_ref)
        bm = _conv_silu(b_ref[rows, :], *neighbours(b_ref, c, kp, kn), cwb_ref, cbb_ref)
        cm = _conv_silu(c_ref[rows, :], *neighbours(c_ref, c, kp, kn), cwc_ref, cbc_ref)
        xb, bmb, cmb = xa.astype(bf16), bm.astype(bf16), cm.astype(bf16)
        xa_ref[rows, :] = xb
        ba_ref[rows, :] = bmb
        ca_ref[rows, :] = cmb
        dtq = _sum_split(dt_ref[rows, :], sel, 2)
        dtf = _softplus(dtq + hp[0:1])
        dtb = _softplus(dtq + hp[1:2])
        acs = _sum_split_left(tri_lo, dtf * hp[2:3], 3)
        sfx = _sum_split_left(tri_up, dtb * hp[3:4], 3)
        ldf, ldb = jnp.log(dtf), jnp.log(dtb)
        sfx_ref[rows, :] = sfx
        ldb_ref[rows, :] = ldb
        a2, s2 = acs * LOG2E, sfx * LOG2E
        a2r = (a2 - ldf * LOG2E).T
        s2r = (s2 - ldb * LOG2E).T
        g = lax.dot_general(cmb, bmb, (((1,), (1,)), ((), ())), preferred_element_type=f32)
        y_diag = None
        for j in range(QH):
            m = (g * (jnp.exp2(jnp.where(lower, a2[:, j:j + 1] - a2r[j:j + 1, :], NEG_BIG))
                      + jnp.exp2(jnp.where(upper, s2[:, j:j + 1] - s2r[j:j + 1, :], NEG_BIG)))).astype(bf16)
            xh = jnp.where(head_of_lane == j, xb, jnp.zeros((), bf16))
            t = jnp.dot(m, xh, preferred_element_type=f32)
            y_diag = t if y_diag is None else y_diag + t
        ea = jnp.exp(expand(acs))
        wf = jnp.exp(expand(acs[BLK - 1:BLK] - acs + ldf))
        s_in = st_ref[0]
        y_ref[rows, :] = xa * d_row + y_diag + ea * jnp.dot(cmb, s_in.astype(bf16), preferred_element_type=f32)
        st_ref[0] = ea[BLK - 1:BLK] * s_in + lax.dot_general(
            bmb, (xa * wf).astype(bf16), (((0,), (0,)), ((), ())), preferred_element_type=f32)
        fs_ref[c, 0] = st_ref[0]
        return carry

    lax.fori_loop(0, nch, fwd, 0)

    def bwd(k, carry):
        c = nch - 1 - k
        blk = sb * nch + c
        rows = chunk_rows(c)

        @pl.when(tbl_ref[1, blk] == 1)
        def _():
            st_ref[1] = jnp.where(sb >= n_zero, s0_ref[0, 1], 0.0)

        sfx = sfx_ref[rows, :]
        eb = jnp.exp(expand(sfx))
        wb = jnp.exp(expand(sfx[0:1] - sfx + ldb_ref[rows, :]))
        s_in = st_ref[1]
        y_ref[rows, :] += eb * jnp.dot(ca_ref[rows, :], s_in.astype(bf16), preferred_element_type=f32)
        st_ref[1] = eb[0:1] * s_in + lax.dot_general(
            ba_ref[rows, :], (xa_ref[rows, :].astype(f32) * wb).astype(bf16), (((0,), (0,)), ((), ())),
            preferred_element_type=f32)
        fs_ref[c, 1] = st_ref[1]
        return carry

    lax.fori_loop(0, nch, bwd, 0)


def ssd_scan(xbc, dt, conv_w, conv_b, sel, hp, s0, n_zero, first, last, sb_rows):
    R = xbc.shape[0]
    n_sb = R // sb_rows
    nch = sb_rows // BLK
    nq = SSD_HEADS // SSD_QH
    qw = SSD_QH * SSD_P
    qpg = SSD_HPG // SSD_QH
    b_blk = SSD_INNER // SSD_N
    c_blk = b_blk + SSD_GROUPS
    tbl = jnp.asarray([first, last], jnp.int32)
    cw = conv_w
    cb = conv_b.reshape(1, -1)
    f32, bf16 = jnp.float32, jnp.bfloat16
    return pl.pallas_call(
        functools.partial(_ssd_kernel, n_zero=n_zero),
        grid_spec=pltpu.PrefetchScalarGridSpec(
            num_scalar_prefetch=1,
            grid=(n_sb, nq),
            in_specs=[pl.BlockSpec((sb_rows, qw), lambda s, q, t: (s, q)),
                      pl.BlockSpec((sb_rows, SSD_N), lambda s, q, t: (s, b_blk + q // qpg)),
                      pl.BlockSpec((sb_rows, SSD_N), lambda s, q, t: (s, c_blk + q // qpg)),
                      pl.BlockSpec((sb_rows, 128), lambda s, q, t: (s, 0)),
                      pl.BlockSpec((3, qw), lambda s, q, t: (0, q)),
                      pl.BlockSpec((3, SSD_N), lambda s, q, t: (0, b_blk + q // qpg)),
                      pl.BlockSpec((3, SSD_N), lambda s, q, t: (0, c_blk + q // qpg)),
                      pl.BlockSpec((1, qw), lambda s, q, t: (0, q)),
                      pl.BlockSpec((1, SSD_N), lambda s, q, t: (0, b_blk + q // qpg)),
                      pl.BlockSpec((1, SSD_N), lambda s, q, t: (0, c_blk + q // qpg)),
                      pl.BlockSpec((1, 128, 128), lambda s, q, t: (q, 0, 0)),
                      pl.BlockSpec((1, 8, 128), lambda s, q, t: (q, 0, 0)),
                      pl.BlockSpec((1, 2, SSD_N, qw), lambda s, q, t: (jnp.maximum(s - n_zero, 0), 0, 0, q))],
            out_specs=[pl.BlockSpec((sb_rows, qw), lambda s, q, t: (s, q)),
                       pl.BlockSpec((nch, 2, SSD_N, qw), lambda s, q, t: (s, 0, 0, q))],
            scratch_shapes=[pltpu.VMEM((sb_rows, qw), bf16), pltpu.VMEM((sb_rows, SSD_N), bf16),
                            pltpu.VMEM((sb_rows, SSD_N), bf16), pltpu.VMEM((sb_rows, 128), f32),
                            pltpu.VMEM((sb_rows, 128), f32), pltpu.VMEM((2, SSD_N, qw), f32)]),
        out_shape=[jax.ShapeDtypeStruct((R, SSD_INNER), f32),
                   jax.ShapeDtypeStruct((R // BLK, 2, SSD_N, SSD_INNER), f32)],
        compiler_params=pltpu.CompilerParams(dimension_semantics=("arbitrary", "arbitrary"),
                                             vmem_limit_bytes=SSD_VMEM_BYTES),
        name="ssd_scan",
    )(tbl, xbc, xbc, xbc, dt, cw, cw, cw, cb, cb, cb, sel, hp, s0)


def ssd_tables(p, e):
    nq = SSD_HEADS // SSD_QH
    lane = jnp.arange(128)
    sel = jnp.stack([(lane[:, None] == (q * SSD_QH + lane[None, :])) & (lane[None, :] < SSD_QH)
                     for q in range(nq)]).astype(jnp.bfloat16)
    a_neg = -jnp.exp(p['ssd_a_log'][e].astype(jnp.float32))
    rows = jnp.stack([p['ssd_dt_bias'][e, 0], p['ssd_dt_bias'][e, 1], a_neg[0], a_neg[1], p['ssd_d'][e]])
    hp = jnp.zeros((nq, 8, 128), jnp.float32)
    hp = hp.at[:, :5, :SSD_QH].set(jnp.transpose(rows.reshape(5, nq, SSD_QH), (1, 0, 2)))
    return sel, hp


MIX_VMEM_BYTES = 48 * 1024 * 1024
MIX_BLOCKS = 1
ROUTER_LANES = 128


def _residual_norm_router(x, out, g1, n2g, sc2, sh2, rw_ref, x_out_ref, hn_ref, aff_ref):
    rows, d = x.shape
    nb = g1.shape[0]
    x_new = x.reshape(nb, rows // nb, d) + g1 * out.reshape(nb, rows // nb, d)
    x_out_ref[...] = x_new.reshape(rows, d)
    hn = _adaln(x_new, n2g, sc2, sh2).reshape(rows, d)
    hn_ref[...] = hn.astype(hn_ref.dtype)
    logits = _dot(hn, rw_ref[...], ((1,), (0,)), 3)
    lane = lax.broadcasted_iota(jnp.int32, logits.shape, 1)
    logits = jnp.where(lane < N_EXPERTS, logits, NEG_BIG)
    ex = jnp.exp(logits - jnp.max(logits, axis=-1, keepdims=True))
    aff = ex / jnp.sum(ex, axis=-1, keepdims=True)
    aff_ref[...] = aff.T[:N_EXPERTS]


def _l0_out_kernel(ys_ref, z_ref, yf_ref, yb_ref, bonus_ref, gate_ref, sg_ref, lnw_ref, lnb_ref, e_ref, et_ref, w_ref,
                   x_ref, g1_ref, n2g_ref, sc2_ref, sh2_ref, rw_ref, x_out_ref, hn_ref, aff_ref):
    f32, bf16 = jnp.float32, jnp.bfloat16
    z = z_ref[...].astype(f32)
    ys = ys_ref[...] * (z * jax.nn.sigmoid(z))
    gw = SSD_INNER // SSD_GROUPS
    parts = []
    for gi in range(SSD_GROUPS):
        yg = ys[:, gi * gw:(gi + 1) * gw]
        parts.append(yg * lax.rsqrt(jnp.mean(yg * yg, -1, keepdims=True) + NORM_EPS))
    a1 = jnp.concatenate(parts, axis=1) * sg_ref[...]
    o = yf_ref[...].astype(f32) + yb_ref[...].astype(f32)
    mu = _head_sum(o, e_ref, et_ref) * (1.0 / RWKV_N)
    oc = o - mu
    var = _head_sum(oc * oc, e_ref, et_ref) * (1.0 / RWKV_N)
    o = oc * lax.rsqrt(var + RWKV_GN_EPS) * lnw_ref[...] + lnb_ref[...]
    o = (o + bonus_ref[...].astype(f32)) * gate_ref[...].astype(f32)
    out = (jnp.dot(a1.astype(bf16), w_ref[:SSD_INNER], preferred_element_type=f32)
           + jnp.dot(o.astype(bf16), w_ref[SSD_INNER:], preferred_element_type=f32))
    _residual_norm_router(x_ref[...], out, g1_ref[...], n2g_ref[...], sc2_ref[...], sh2_ref[...], rw_ref,
                          x_out_ref, hn_ref, aff_ref)


def _l1_out_kernel(a_ref, w_ref, x_ref, g1_ref, n2g_ref, sc2_ref, sh2_ref, rw_ref, x_out_ref, hn_ref, aff_ref):
    out = jnp.dot(a_ref[...].astype(jnp.bfloat16), w_ref[...], preferred_element_type=jnp.float32)
    _residual_norm_router(x_ref[...], out, g1_ref[...], n2g_ref[...], sc2_ref[...], sh2_ref[...], rw_ref,
                          x_out_ref, hn_ref, aff_ref)


def _mix_out_call(kernel_fn, name, lead_args, lead_specs, w_out, x, g1b, n2g, sc2b, sh2b, router_w):
    R, D = x.shape
    mb = MIX_BLOCKS
    assert (R // BLK) % mb == 0
    full = lambda a: pl.BlockSpec(a.shape, lambda i: (0,) * a.ndim)
    blkrow = pl.BlockSpec((mb, 1, D), lambda i: (i, 0, 0))
    rw = jnp.zeros((D, ROUTER_LANES), jnp.float32).at[:, :N_EXPERTS].set(router_w)
    args = list(lead_args) + [w_out, x, g1b, n2g, sc2b, sh2b, rw]
    x_idx = len(lead_args) + 1
    in_specs = list(lead_specs) + [full(w_out), pl.BlockSpec((mb * BLK, D), lambda i: (i, 0)), blkrow, full(n2g), blkrow,
                                   blkrow, full(rw)]
    return pl.pallas_call(
        kernel_fn,
        grid=(R // (mb * BLK),),
        in_specs=in_specs,
        out_specs=[pl.BlockSpec((mb * BLK, D), lambda i: (i, 0)), pl.BlockSpec((mb * BLK, D), lambda i: (i, 0)),
                   pl.BlockSpec((N_EXPERTS, mb * BLK), lambda i: (0, i))],
        out_shape=[jax.ShapeDtypeStruct((R, D), jnp.float32), jax.ShapeDtypeStruct((R, D), jnp.bfloat16),
                   jax.ShapeDtypeStruct((N_EXPERTS, R), jnp.float32)],
        input_output_aliases={x_idx: 0},
        compiler_params=pltpu.CompilerParams(dimension_semantics=("arbitrary",), vmem_limit_bytes=MIX_VMEM_BYTES),
        name=name,
    )(*args)


def l0_out(ys, z, yf, yb, bonus, gate, ssd_g, ln_w, ln_b, e_ind, et_ind, w_out, x, g1b, n2g, sc2b, sh2b, router_w):
    c = RWKV_DIM
    rows = lambda n: pl.BlockSpec((MIX_BLOCKS * BLK, n), lambda i: (i, 0))
    full = lambda a: pl.BlockSpec(a.shape, lambda i: (0,) * a.ndim)
    lead = [ys, z, yf, yb, bonus, gate, ssd_g, ln_w, ln_b, e_ind, et_ind]
    specs = [rows(SSD_INNER), rows(SSD_INNER), rows(c), rows(c), rows(c), rows(c),
             full(ssd_g), full(ln_w), full(ln_b), full(e_ind), full(et_ind)]
    return _mix_out_call(_l0_out_kernel, "l0_out", lead, specs, w_out, x, g1b, n2g, sc2b, sh2b, router_w)


def l1_out(a, w_out, x, g1b, n2g, sc2b, sh2b, router_w):
    specs = [pl.BlockSpec((MIX_BLOCKS * BLK, a.shape[1]), lambda i: (i, 0))]
    return _mix_out_call(_l1_out_kernel, "l1_out", [a], specs, w_out, x, g1b, n2g, sc2b, sh2b, router_w)


RET_HPS = 2


def _l1_in_kernel(tbl_ref, x_ref, g_ref, sc_ref, sh_ref, w_ref, cos_ref, sin_ref, q_ref, k_ref, v_ref, gg_ref):
    f32 = jnp.float32
    hn = _adaln(x_ref[...], g_ref[...], sc_ref[0], sh_ref[0]).astype(jnp.bfloat16)
    cosf, sinf = cos_ref[0], sin_ref[0]

    def rope(x):
        parts = []
        for h in range(RET_HEADS):
            xh = x[:, h * RET_DK:(h + 1) * RET_DK]
            parts.append(xh * cosf + pltpu.roll(xh, RET_DK // 2, 1) * sinf)
        return jnp.concatenate(parts, axis=1)

    q_ref[...] = rope(jnp.dot(hn, w_ref[:, :RET_QK], preferred_element_type=f32)).astype(q_ref.dtype)
    k_ref[...] = (rope(jnp.dot(hn, w_ref[:, RET_QK:2 * RET_QK], preferred_element_type=f32))
                  * (RET_DK ** -0.5)).astype(k_ref.dtype)
    v_ref[...] = jnp.dot(hn, w_ref[:, 2 * RET_QK:2 * RET_QK + RET_V], preferred_element_type=f32).astype(v_ref.dtype)
    gg_ref[...] = jnp.dot(hn, w_ref[:, 2 * RET_QK + RET_V:], preferred_element_type=f32).astype(gg_ref.dtype)


def l1_in(x, g1, scb, shb, w_bf16, cos_t, sin_t, rope_blk):
    R, D = x.shape
    nblk = R // BLK
    tbl = jnp.asarray([rope_blk], jnp.int32)
    f32 = jnp.float32
    row = lambda n: pl.BlockSpec((BLK, n), lambda i, t: (i, 0))
    full = lambda a: pl.BlockSpec(a.shape, lambda i, t: (0,) * a.ndim)
    blkrow = pl.BlockSpec((1, 1, D), lambda i, t: (i, 0, 0))
    ropespec = pl.BlockSpec((1, BLK, RET_DK), lambda i, t: (t[0, i], 0, 0))
    return pl.pallas_call(
        _l1_in_kernel,
        grid_spec=pltpu.PrefetchScalarGridSpec(
            num_scalar_prefetch=1, grid=(nblk,),
            in_specs=[row(D), full(g1), blkrow, blkrow, full(w_bf16), ropespec, ropespec],
            out_specs=[row(RET_QK), row(RET_QK), row(RET_V), row(RET_V)]),
        out_shape=[jax.ShapeDtypeStruct((R, RET_QK), ACT_DTYPE), jax.ShapeDtypeStruct((R, RET_QK), ACT_DTYPE),
                   jax.ShapeDtypeStruct((R, RET_V), ACT_DTYPE), jax.ShapeDtypeStruct((R, RET_V), ACT_DTYPE)],
        compiler_params=pltpu.CompilerParams(dimension_semantics=("arbitrary",), vmem_limit_bytes=L0_VMEM_BYTES),
        name="l1_in",
    )(tbl, x, g1, scb, shb, w_bf16, cos_t, sin_t)


def _ret_kernel(tbl_ref, q_ref, k_ref, v_ref, g_ref, lg_ref, nw_ref, nb_ref, s0_ref, a_ref, fs_ref, st_ref, *,
                n_zero):
    f32, bf16 = jnp.float32, jnp.bfloat16
    sb = pl.program_id(0)
    nch = q_ref.shape[0] // BLK
    heads = range(RET_HPS)
    ks = [slice(h * RET_DK, (h + 1) * RET_DK) for h in heads]
    vs = [slice(h * RET_DV, (h + 1) * RET_DV) for h in heads]
    lgf = [lg_ref[h, 0:1, 0:1] for h in heads]
    lgb = [lg_ref[h, 1:2, 0:1] for h in heads]
    t_i = lax.broadcasted_iota(jnp.int32, (BLK, BLK), 0)
    s_i = lax.broadcasted_iota(jnp.int32, (BLK, BLK), 1)
    dist = (t_i - s_i).astype(f32)
    dm = [jnp.exp(jnp.where(s_i <= t_i, dist * lgf[h], NEG_BIG)) + jnp.exp(jnp.where(s_i >= t_i, -dist * lgb[h], NEG_BIG))
          for h in heads]
    tk = lax.broadcasted_iota(jnp.int32, (BLK, RET_DK), 0).astype(f32)
    tv = lax.broadcasted_iota(jnp.int32, (BLK, RET_DV), 0).astype(f32)
    k_to_end_f = [jnp.exp((BLK - 1.0 - tk) * lgf[h]) for h in heads]
    k_to_end_b = [jnp.exp(tk * lgb[h]) for h in heads]
    from_start_f = [jnp.exp((tv + 1.0) * lgf[h]) for h in heads]
    from_start_b = [jnp.exp((BLK - tv) * lgb[h]) for h in heads]
    nt, tn = (((1,), (1,)), ((), ())), (((0,), (0,)), ((), ()))

    def chunk_rows(c):
        return pl.ds(pl.multiple_of(c * BLK, BLK), BLK)

    def fwd(c, carry):
        blk = sb * nch + c
        rows = chunk_rows(c)

        @pl.when(tbl_ref[0, blk] == 1)
        def _():
            st_ref[0] = jnp.where(sb >= n_zero, s0_ref[0, 0], 0.0)

        q = [q_ref[rows, ks[h]].astype(bf16) for h in heads]
        k = [k_ref[rows, ks[h]] for h in heads]
        v = [v_ref[rows, vs[h]].astype(bf16) for h in heads]
        s_in = [st_ref[0, h] for h in heads]
        g = [lax.dot_general(q[h], k[h].astype(bf16), nt, preferred_element_type=f32) for h in heads]
        y_diag = [jnp.dot((g[h] * dm[h]).astype(bf16), v[h], preferred_element_type=f32) for h in heads]
        y_off = [jnp.dot(q[h], s_in[h].astype(bf16), preferred_element_type=f32) for h in heads]
        kw = [(k[h].astype(f32) * k_to_end_f[h]).astype(bf16) for h in heads]
        upd = [lax.dot_general(kw[h], v[h], tn, preferred_element_type=f32) for h in heads]
        for h in heads:
            a_ref[rows, vs[h]] = y_diag[h] + from_start_f[h] * y_off[h]
            st_ref[0, h] = jnp.exp(BLK * lgf[h]) * s_in[h] + upd[h]
        fs_ref[c, 0] = st_ref[0]
        return carry

    lax.fori_loop(0, nch, fwd, 0)

    def bwd(j, carry):
        c = nch - 1 - j
        blk = sb * nch + c
        rows = chunk_rows(c)

        @pl.when(tbl_ref[1, blk] == 1)
        def _():
            st_ref[1] = jnp.where(sb >= n_zero, s0_ref[0, 1], 0.0)

        q = [q_ref[rows, ks[h]].astype(bf16) for h in heads]
        v = [v_ref[rows, vs[h]].astype(bf16) for h in heads]
        s_in = [st_ref[1, h] for h in heads]
        y_off = [jnp.dot(q[h], s_in[h].astype(bf16), preferred_element_type=f32) for h in heads]
        kw = [(k_ref[rows, ks[h]].astype(f32) * k_to_end_b[h]).astype(bf16) for h in heads]
        upd = [lax.dot_general(kw[h], v[h], tn, preferred_element_type=f32) for h in heads]
        for h in heads:
            st_ref[1, h] = jnp.exp(BLK * lgb[h]) * s_in[h] + upd[h]
            y = a_ref[rows, vs[h]] + from_start_b[h] * y_off[h]
            mu = jnp.mean(y, -1, keepdims=True)
            yc = y - mu
            var = jnp.mean(yc * yc, -1, keepdims=True)
            gg = g_ref[rows, vs[h]].astype(f32)
            a_ref[rows, vs[h]] = ((yc * lax.rsqrt(var + 1e-5) * nw_ref[:, vs[h]] + nb_ref[:, vs[h]])
                                  * (gg * jax.nn.sigmoid(gg)))
        fs_ref[c, 1] = st_ref[1]
        return carry

    lax.fori_loop(0, nch, bwd, 0)


def ret_scan(q, k, v, g, lg_tab, norm_w, norm_b, s0, n_zero, first, last, sb_rows):
    R = q.shape[0]
    n_sb = R // sb_rows
    nch = sb_rows // BLK
    tbl = jnp.asarray([first, last], jnp.int32)
    f32 = jnp.float32
    hps = RET_HPS
    return pl.pallas_call(
        functools.partial(_ret_kernel, n_zero=n_zero),
        grid_spec=pltpu.PrefetchScalarGridSpec(
            num_scalar_prefetch=1, grid=(n_sb, RET_HEADS // hps),
            in_specs=[pl.BlockSpec((sb_rows, hps * RET_DK), lambda s, h, t: (s, h)),
                      pl.BlockSpec((sb_rows, hps * RET_DK), lambda s, h, t: (s, h)),
                      pl.BlockSpec((sb_rows, hps * RET_DV), lambda s, h, t: (s, h)),
                      pl.BlockSpec((sb_rows, hps * RET_DV), lambda s, h, t: (s, h)),
                      pl.BlockSpec((hps, 8, 128), lambda s, h, t: (h, 0, 0)),
                      pl.BlockSpec((1, hps * RET_DV), lambda s, h, t: (0, h)),
                      pl.BlockSpec((1, hps * RET_DV), lambda s, h, t: (0, h)),
                      pl.BlockSpec((1, 2, hps, RET_DK, RET_DV), lambda s, h, t: (jnp.maximum(s - n_zero, 0), 0, h, 0, 0))],
            out_specs=[pl.BlockSpec((sb_rows, hps * RET_DV), lambda s, h, t: (s, h)),
                       pl.BlockSpec((nch, 2, hps, RET_DK, RET_DV), lambda s, h, t: (s, 0, h, 0, 0))],
            scratch_shapes=[pltpu.VMEM((2, hps, RET_DK, RET_DV), f32)]),
        out_shape=[jax.ShapeDtypeStruct((R, RET_V), f32),
                   jax.ShapeDtypeStruct((R // BLK, 2, RET_HEADS, RET_DK, RET_DV), f32)],
        compiler_params=pltpu.CompilerParams(dimension_semantics=("arbitrary", "arbitrary"),
                                             vmem_limit_bytes=SSD_VMEM_BYTES),
        name="ret_scan",
    )(tbl, q, k, v, g, lg_tab, norm_w.reshape(1, -1), norm_b.reshape(1, -1), s0)


def _mod_kernel(c_ref, w_ref, b_ref, o_ref):
    c = c_ref[...]
    act = c * jax.nn.sigmoid(c)
    o_ref[0] = _dot(act, w_ref[0], ((1,), (0,)), 3) + b_ref[0]


def mod_vectors(conds, mod_w, mod_b):
    depth, D, n6 = mod_w.shape
    tn = D
    return pl.pallas_call(
        _mod_kernel,
        grid=(depth, n6 // tn),
        in_specs=[pl.BlockSpec(conds.shape, lambda i, j: (0, 0)),
                  pl.BlockSpec((1, D, tn), lambda i, j: (i, 0, j)),
                  pl.BlockSpec((1, 1, tn), lambda i, j: (i, 0, j))],
        out_specs=pl.BlockSpec((1, conds.shape[0], tn), lambda i, j: (i, 0, j)),
        out_shape=jax.ShapeDtypeStruct((depth, conds.shape[0], n6), jnp.float32),
        name="mod_vectors",
    )(conds, mod_w, mod_b.reshape(depth, 1, n6))


def rope_tables(n_tokens):
    rows = n_tokens // GRID_W
    row = np.repeat(np.arange(rows), GRID_W).astype(np.float64)
    col = np.tile(np.arange(GRID_W), rows).astype(np.float64)
    n_f = RET_DK // 4
    inv = ROPE_BASE ** (-np.arange(n_f, dtype=np.float64) / n_f)
    ang = np.concatenate([row[:, None] * inv, col[:, None] * inv], -1)
    return np.cos(ang), np.sin(ang)


def _rope_block_tables(n_ctx, n_lat, l_lat):
    nb = l_lat // BLK
    cos, sin = rope_tables(l_lat)
    cosf = np.concatenate([cos, cos], -1).reshape(nb, BLK, RET_DK)
    sinf = np.concatenate([-sin, sin], -1).reshape(nb, BLK, RET_DK)
    cos_t = jnp.asarray(np.concatenate([np.ones((1, BLK, RET_DK)), cosf]), jnp.float32)
    sin_t = jnp.asarray(np.concatenate([np.zeros((1, BLK, RET_DK)), sinf]), jnp.float32)
    rope_blk = [0] * n_ctx + [1 + j for _ in range(n_lat) for j in range(nb)]
    return cos_t, sin_t, rope_blk


def kernel(x_prompt, x_sample, state_ssd, state_rwkv, state_ret, c, c_ctx, mod_w, mod_b, norm1_g, norm2_g,
           router_w, exp_w_gate, exp_w_up, exp_w_down, ab_w_in, ab_w_out, ssd_conv_w, ssd_conv_b, ssd_dt_bias,
           ssd_a_log, ssd_d, ssd_norm_g, rwkv_mu_prev, rwkv_mu_next, rwkv_w0, rwkv_w2, rwkv_a0, rwkv_a2, rwkv_g2,
           rwkv_k_k, rwkv_k_a, rwkv_r_k, rwkv_ln_w, rwkv_ln_b, ret_w_in, ret_w_out, ret_decay_logit, ret_norm_w,
           ret_norm_b, final_norm_g):
    p = dict(mod_w=mod_w, mod_b=mod_b, norm1_g=norm1_g, norm2_g=norm2_g, router_w=router_w,
             exp_w_gate=exp_w_gate, exp_w_up=exp_w_up, exp_w_down=exp_w_down, ab_w_in=ab_w_in, ab_w_out=ab_w_out,
             ssd_conv_w=ssd_conv_w, ssd_conv_b=ssd_conv_b, ssd_dt_bias=ssd_dt_bias, ssd_a_log=ssd_a_log,
             ssd_d=ssd_d, ssd_norm_g=ssd_norm_g, rwkv_mu_prev=rwkv_mu_prev, rwkv_mu_next=rwkv_mu_next,
             rwkv_w0=rwkv_w0, rwkv_w2=rwkv_w2, rwkv_a0=rwkv_a0, rwkv_a2=rwkv_a2, rwkv_g2=rwkv_g2,
             rwkv_k_k=rwkv_k_k, rwkv_k_a=rwkv_k_a, rwkv_r_k=rwkv_r_k, rwkv_ln_w=rwkv_ln_w, rwkv_ln_b=rwkv_ln_b,
             ret_w_in=ret_w_in, ret_w_out=ret_w_out, ret_decay_logit=ret_decay_logit, ret_norm_w=ret_norm_w,
             ret_norm_b=ret_norm_b, final_norm_g=final_norm_g)
    f32, bf16 = jnp.float32, jnp.bfloat16
    n_ctx, l_ctx, D = x_prompt.shape
    n_lat, l_lat, _ = x_sample.shape
    assert l_ctx == BLK and l_lat % BLK == 0 and (n_ctx * BLK) % l_lat == 0
    n_sb_ctx = n_ctx * BLK // l_lat
    cond_id, first, last = _seq_tables(n_ctx, n_lat, l_lat)
    x = jnp.concatenate([x_prompt.reshape(-1, D), x_sample.reshape(-1, D)])

    conds = jnp.concatenate([c_ctx[None, :], c, jnp.zeros((8 - 1 - n_lat, D), f32)])
    mods = mod_vectors(conds, mod_w, mod_b)[:, jnp.asarray(cond_id)]
    mods = mods.reshape(DEPTH, len(cond_id), 6, 1, D)

    new_ssd, new_rwkv, new_ret = [], [], []
    out = None
    for i in range(DEPTH):
        sh1, sc1, g1, sh2, sc2, g2 = (mods[i, :, k] for k in range(6))
        e = i // 2
        if i % 2 == 0:
            w_packed, mup, mun, rwp, w2bd, a2p, g2p, e_ind, et_ind = l0_pack_weights(p, e)
            z, xbc, dt, r, v, an, lw, kd, bv, gate, bonus = l0_in(
                x, norm1_g[i][None], sc1, sh1, w_packed, mup, mun, rwp, w2bd, a2p, g2p, e_ind, et_ind, first, last)
            sel, hp = ssd_tables(p, e)
            s0_ssd = jnp.transpose(state_ssd[:, e], (0, 1, 3, 2, 4)).reshape(n_lat, 2, SSD_N, SSD_INNER)
            ys, fs_ssd = ssd_scan(xbc, dt, ssd_conv_w[e], ssd_conv_b[e], sel, hp,
                                  s0_ssd, n_sb_ctx, first, last, l_lat)
            new_ssd.append(jnp.transpose(fs_ssd[:n_ctx].reshape(n_ctx, 2, SSD_N, SSD_HEADS, SSD_P), (0, 1, 3, 2, 4)))
            s0_rwkv = jnp.transpose(state_rwkv[:, e], (0, 1, 3, 2, 4)).reshape(n_lat, 2, RWKV_N, RWKV_DIM)
            yf, yb, sf_rwkv = rwkv_scan_pallas(r, v, an, lw, kd, bv, s0_rwkv, n_ctx,
                                               _rwkv_steps(n_ctx, n_lat, l_lat))
            new_rwkv.append(jnp.transpose(sf_rwkv[:n_ctx].reshape(n_ctx, 2, RWKV_N, RWKV_HEADS, RWKV_N),
                                          (0, 1, 3, 2, 4)))
            x, hn2, affT = l0_out(ys, z, yf, yb, bonus, gate, ssd_norm_g[e][None], rwkv_ln_w[e][None], rwkv_ln_b[e][None],
                                  e_ind, et_ind, ab_w_out[e].astype(bf16), x, g1, norm2_g[i][None], sc2, sh2,
                                  router_w[i])
        else:
            cos_t, sin_t, rope_blk = _rope_block_tables(n_ctx, n_lat, l_lat)
            q, k, v, gg = l1_in(x, norm1_g[i][None], sc1, sh1, ret_w_in[e].astype(bf16), cos_t, sin_t, rope_blk)
            lg = jax.nn.log_sigmoid(ret_decay_logit[e].astype(f32))
            lg_tab = jnp.zeros((RET_HEADS, 8, 128), f32).at[:, :2, :].set(jnp.transpose(lg)[:, :, None])
            a, fs_ret = ret_scan(q, k, v, gg, lg_tab, ret_norm_w[e], ret_norm_b[e],
                                 state_ret[:, e], n_sb_ctx, first, last, l_lat)
            new_ret.append(fs_ret[:n_ctx])
            x, hn2, affT = l1_out(a, ret_w_out[e].astype(bf16), x, g1, norm2_g[i][None], sc2, sh2, router_w[i])
        fin = final_norm_g if i == DEPTH - 1 else None
        out = moe_layer(x, hn2, affT, g2, exp_w_gate, exp_w_up, exp_w_down, i, n_ctx, l_lat, final_g=fin)
        if fin is None:
            x = out
    y_ctx, y_lat = out
    return (y_ctx.reshape(n_ctx, l_ctx, D), y_lat.reshape(n_lat, l_lat, D),
            jnp.stack(new_ssd, 1), jnp.stack(new_rwkv, 1), jnp.stack(new_ret, 1))
```

```python
import functools
import math

import jax
import jax.numpy as jnp
import numpy as np
from jax import lax
from jax.experimental import pallas as pl
from jax.experimental.pallas import tpu as pltpu

D_MODEL = 1024
DEPTH = 2
GRID_W = 64
CHUNK = 128
NORM_EPS = 1e-6
SSD_HEADS = 16
SSD_P = 64
SSD_INNER = SSD_HEADS * SSD_P
SSD_GROUPS = 2
SSD_HPG = SSD_HEADS // SSD_GROUPS
SSD_N = 128
SSD_XBC = SSD_INNER + 2 * SSD_GROUPS * SSD_N
SSD_IN = SSD_INNER + SSD_XBC + SSD_HEADS
RWKV_HEADS = 16
RWKV_N = 64
RWKV_DIM = RWKV_HEADS * RWKV_N
W_LORA = 64
A_LORA = 64
G_LORA = 128
RWKV_GN_EPS = 64e-5
RET_HEADS = 8
RET_DK = 128
RET_DV = 256
RET_QK = RET_HEADS * RET_DK
RET_V = RET_HEADS * RET_DV
ROPE_BASE = 10000.0
N_EXPERTS = 16
EC_CAPACITY = 2

ACT_DTYPE = jnp.bfloat16

RWKV_C = 64
RWKV_GH = 4
RWKV_GL = RWKV_GH * RWKV_N
RWKV_DOUBLING_PASSES = (3, 3, 3, 3, 1, 1)


def _split_bf16(x):
    hi = x.astype(jnp.bfloat16)
    lo = (x - hi.astype(jnp.float32)).astype(jnp.bfloat16)
    return hi, lo


def _dot(a, b, dims, passes):
    f = functools.partial(lax.dot_general, dimension_numbers=(dims, ((), ())),
                          preferred_element_type=jnp.float32)
    if passes == 1:
        return f(a.astype(jnp.bfloat16), b.astype(jnp.bfloat16))
    ah, al = _split_bf16(a)
    bh, bl = _split_bf16(b)
    return f(ah, bh) + (f(ah, bl) + f(al, bh))


def _rwkv_chunk_kernel(tbl_ref, r0_ref, v0_ref, a0_ref, r1_ref, v1_ref, a1_ref, lw0_ref, k0_ref, b0_ref,
                       lw1_ref, k1_ref, b1_ref, s0_ref, y0_ref, y1_ref, sf_ref, h_ref, *, p_inv, p_oth, n_zero):
    C, N, GH, GL = RWKV_C, RWKV_N, RWKV_GH, RWKV_GL
    i = pl.program_id(0)
    f32, bf16 = jnp.float32, jnp.bfloat16

    @pl.when(tbl_ref[3, i] == 1)
    def _():
        h_ref[...] = jnp.where(tbl_ref[2, i] >= n_zero, s0_ref[0], 0.0)

    t_i = lax.broadcasted_iota(jnp.int32, (C, GL), 0)
    s_i = lax.broadcasted_iota(jnp.int32, (C, GL), 1) & (N - 1)
    eye = (s_i == t_i).astype(f32)
    row2 = lax.broadcasted_iota(jnp.int32, (2 * C, GL), 0)
    rel2 = (lax.broadcasted_iota(jnp.int32, (2 * C, GL), 1) & (N - 1)) - (row2 & (C - 1))
    incl2 = row2 // C
    mask2 = [rel2 - incl2 < 0, -rel2 - incl2 < 0]
    bh_r = lax.broadcasted_iota(jnp.int32, (GL, GL), 0) // N
    bh_c = lax.broadcasted_iota(jnp.int32, (GL, GL), 1) // N
    blk = bh_r == bh_c
    tt = lax.broadcasted_iota(jnp.int32, (C, C), 0)
    ss = lax.broadcasted_iota(jnp.int32, (C, C), 1)
    tri = [(ss <= tt).astype(bf16), (ss >= tt).astype(bf16)]

    def bd(x, passes):
        pieces = []
        for _ in range(2 if passes == 3 else 1):
            hi = x.astype(bf16)
            x = x - hi.astype(f32)
            pieces.append(jnp.where(blk, jnp.concatenate([hi] * GH, axis=0), jnp.zeros((), bf16)))
        return pieces

    def mm(l, x, passes, dims=((1,), (0,))):
        f = functools.partial(lax.dot_general, dimension_numbers=(dims, ((), ())), preferred_element_type=f32)
        xs = bd(x, passes)
        lh = l.astype(bf16)
        if passes == 1:
            return f(lh, xs[0])
        ll = (l - lh.astype(f32)).astype(bf16)
        m = l.shape[0]
        both = f(jnp.concatenate([lh, ll], axis=0), xs[0])
        if passes == 2:
            return both[:m] + both[m:]
        return both[:m] + (f(lh, xs[1]) + both[m:])

    nt = ((1,), (1,))
    refs = [(r0_ref, v0_ref, a0_ref, lw0_ref, k0_ref, b0_ref), (r1_ref, v1_ref, a1_ref, lw1_ref, k1_ref, b1_ref)]
    lw, r_t, a_t, b_t, k_t, v = [], [], [], [], [], []
    for d, (r_ref, v_ref, a_ref, lw_ref, k_ref, b_ref) in enumerate(refs):
        lwd = lw_ref[0]
        lw_hi, lw_lo = _split_bf16(lwd)
        cum = (jnp.dot(tri[d], lw_hi, preferred_element_type=f32) + jnp.dot(tri[d], lw_lo, preferred_element_type=f32))
        w_inv = jnp.exp(-cum)
        lw.append(lwd)
        r_t.append(r_ref[...].astype(f32) * jnp.exp(cum))
        a_t.append(a_ref[...].astype(f32) * jnp.exp(cum - lwd))
        b_t.append(b_ref[0].astype(f32) * w_inv)
        k_t.append(k_ref[0].astype(f32) * w_inv)
        v.append(v_ref[...].astype(f32))
    lane_head = lax.broadcasted_iota(jnp.int32, (N, GL), 1) // N

    chains = [(d, slice(g * GL, (g + 1) * GL)) for d in range(2) for g in range(RWKV_HEADS // GH)]
    each = lambda fn: [fn(j, d, sl) for j, (d, sl) in enumerate(chains)]
    bg = each(lambda j, d, sl: b_t[d][:, sl])
    kg = each(lambda j, d, sl: k_t[d][:, sl])
    vg = each(lambda j, d, sl: v[d][:, sl])
    h0 = each(lambda j, d, sl: h_ref[d, :, sl])
    ar = each(lambda j, d, sl: jnp.concatenate([a_t[d][:, sl], r_t[d][:, sl]], axis=0))
    m_b = each(lambda j, d, sl: jnp.where(mask2[d], mm(ar[j], bg[j], p_oth, nt), 0.0))
    m_k = each(lambda j, d, sl: jnp.where(mask2[d], mm(ar[j], kg[j], p_oth, nt), 0.0))
    p = each(lambda j, d, sl: mm(m_b[j][:C], m_b[j][:C], p_inv[0]))
    tmat = each(lambda j, d, sl: eye + m_b[j][:C])
    for lv in range(int(math.log2(C)) - 2):
        pt = each(lambda j, d, sl: mm(jnp.concatenate([p[j], tmat[j]], axis=0), p[j], p_inv[1 + lv]))
        p = each(lambda j, d, sl: pt[j][:C])
        tmat = each(lambda j, d, sl: tmat[j] + pt[j][C:])
    tmat = each(lambda j, d, sl: tmat[j] + mm(tmat[j], p[j], p_inv[-1]))
    ar_h = each(lambda j, d, sl: mm(ar[j], h0[j], p_oth, nt))
    mk_v = each(lambda j, d, sl: mm(m_k[j], vg[j], p_oth))
    u = each(lambda j, d, sl: mm(tmat[j], ar_h[j][:C] + mk_v[j][:C], p_oth))
    y = each(lambda j, d, sl: ar_h[j][C:] + mm(m_b[j][C:], u[j], p_oth) + mk_v[j][C:])
    full = each(lambda j, d, sl: _dot(jnp.concatenate([u[j], vg[j]], axis=0), jnp.concatenate([bg[j], kg[j]], axis=0),
                                      ((0,), (0,)), p_oth))
    y_refs = (y0_ref, y1_ref)
    for j, (d, sl) in enumerate(chains):
        y_refs[d][:, sl] = y[j].astype(y_refs[d].dtype)
        z = jnp.zeros((N, GL), f32)
        for hh in range(GH):
            z = z + jnp.where(lane_head == hh, full[j][hh * N:(hh + 1) * N], 0.0)
        w_tot = jnp.exp(jnp.sum(lw[d][:, sl], axis=0, keepdims=True))
        h_ref[d, :, sl] = w_tot * (h0[j] + z)

    @pl.when(tbl_ref[4, i] == 1)
    def _():
        sf_ref[0] = h_ref[...]


def _rwkv_steps(n_ctx, n_lat, l_lat):
    C = RWKV_C
    rows = []
    seqs = [(s, s * BLK, BLK) for s in range(n_ctx)] + [(n_ctx + s, n_ctx * BLK + s * l_lat, l_lat) for s in range(n_lat)]
    for sid, row0, length in seqs:
        nc = length // C
        for j in range(nc):
            rows.append((row0 // C + j, row0 // C + nc - 1 - j, sid, int(j == 0), int(j == nc - 1)))
    return [list(col) for col in zip(*rows)]


def rwkv_scan_pallas(r, v, a, lw, k, b, s0, n_zero, steps, p_inv=RWKV_DOUBLING_PASSES, p_oth=1):
    R, HN = r.shape
    C, N = RWKV_C, RWKV_N
    tbl = jnp.asarray(steps, jnp.int32)
    fwd = pl.BlockSpec((C, HN), lambda i, t: (t[0, i], 0))
    bwd = pl.BlockSpec((C, HN), lambda i, t: (t[1, i], 0))
    fwd_d = pl.BlockSpec((1, C, HN), lambda i, t: (0, t[0, i], 0))
    bwd_d = pl.BlockSpec((1, C, HN), lambda i, t: (1, t[1, i], 0))
    n_seq = max(steps[2]) + 1
    st_in = pl.BlockSpec((1, 2, N, HN), lambda i, t: (jnp.maximum(t[2, i] - n_zero, 0), 0, 0, 0))
    st = pl.BlockSpec((1, 2, N, HN), lambda i, t: (t[2, i], 0, 0, 0))
    return pl.pallas_call(
        functools.partial(_rwkv_chunk_kernel, p_inv=p_inv, p_oth=p_oth, n_zero=n_zero),
        grid_spec=pltpu.PrefetchScalarGridSpec(
            num_scalar_prefetch=1,
            grid=(len(steps[0]),),
            in_specs=[fwd, fwd, fwd, bwd, bwd, bwd, fwd_d, fwd_d, fwd_d, bwd_d, bwd_d, bwd_d, st_in],
            out_specs=[fwd, bwd, st],
            scratch_shapes=[pltpu.VMEM((2, N, HN), jnp.float32)]),
        out_shape=[jax.ShapeDtypeStruct((R, HN), ACT_DTYPE), jax.ShapeDtypeStruct((R, HN), ACT_DTYPE),
                   jax.ShapeDtypeStruct((n_seq, 2, N, HN), jnp.float32)],
        compiler_params=pltpu.CompilerParams(dimension_semantics=("arbitrary",)),
        name="rwkv_scan",
    )(tbl, r, v, a, r, v, a, lw, k, b, lw, k, b, s0)


BLK = 256
FF_TILE = 768
SELECT_TILE = 512
SCATTER_WINDOW = 96
SLOT_ALIGN = 16
SELECT_MIN_EXP = -1100.0
SELECT_BINADE_STEPS = 11
SELECT_MANTISSA_STEPS = 40
MOE_FFN_VMEM_BYTES = 48 * 1024 * 1024


def _moe_select_kernel(aff_ref, slot_ref, *, cap):
    a = aff_ref[...]
    E, T = a.shape
    f32 = jnp.float32

    def enough(piv):
        return jnp.sum(jnp.where(a >= piv, 1.0, 0.0), axis=1, keepdims=True) >= cap

    def binade(_, lohi):
        e_lo, e_hi = lohi
        mid = jnp.floor((e_lo + e_hi) * 0.5)
        ok = enough(jnp.exp2(mid))
        return jnp.where(ok, mid, e_lo), jnp.where(ok, e_hi, mid)

    e_lo, e_hi = lax.fori_loop(0, SELECT_BINADE_STEPS, binade,
                               (jnp.full((E, 1), SELECT_MIN_EXP, f32), jnp.full((E, 1), 1.0, f32)))

    def inside(_, lohi):
        lo, hi = lohi
        mid = lo + (hi - lo) * 0.5
        ok = enough(mid)
        return jnp.where(ok, mid, lo), jnp.where(ok, hi, mid)

    thr, _ = lax.fori_loop(0, SELECT_MANTISSA_STEPS, inside, (jnp.exp2(e_lo), jnp.exp2(e_hi)))
    gt = a > thr
    eq = a == thr
    need = cap - jnp.sum(jnp.where(gt, 1.0, 0.0), axis=1, keepdims=True)
    tw = min(T, SELECT_TILE)

    def prefix_count(mask):
        m = jnp.where(mask, 1.0, 0.0).astype(jnp.bfloat16)
        outs = []
        for j in range(T // tw):
            s_i = lax.broadcasted_iota(jnp.int32, (T, tw), 0)
            t_i = lax.broadcasted_iota(jnp.int32, (T, tw), 1) + j * tw
            before = jnp.where(s_i < t_i, 1.0, 0.0).astype(jnp.bfloat16)
            outs.append(jnp.dot(m, before, preferred_element_type=f32))
        return outs[0] if len(outs) == 1 else jnp.concatenate(outs, axis=1)

    sel = gt | (eq & (prefix_count(eq) < need))
    slot_ref[...] = jnp.where(sel, prefix_count(sel).astype(jnp.int32), -1)


def _moe_select(affT, row0, n_seq, t):
    E = affT.shape[0]
    rows = jnp.transpose(affT[:, row0:row0 + n_seq * t].reshape(E, n_seq, t), (1, 0, 2)).reshape(n_seq * E, t)
    slot = pl.pallas_call(
        functools.partial(_moe_select_kernel, cap=EC_CAPACITY * t // N_EXPERTS),
        grid=(1,),
        in_specs=[pl.BlockSpec((n_seq * E, t), lambda s: (0, 0))],
        out_specs=pl.BlockSpec((n_seq * E, t), lambda s: (0, 0)),
        out_shape=jax.ShapeDtypeStruct((n_seq * E, t), jnp.int32),
        name="moe_select",
    )(rows)
    return jnp.transpose(slot.reshape(n_seq, E, t), (1, 0, 2)).reshape(E, n_seq * t)


def _moe_gather_kernel(tbl_ref, slot_ref, aff_ref, hn_ref, xe_ref, gate_ref, *, cap, win, nb):
    E = slot_ref.shape[0]
    s = pl.program_id(0)
    w_i = lax.broadcasted_iota(jnp.int32, (win, BLK), 0)
    xe_ref[...] = jnp.zeros_like(xe_ref)
    gate_ref[...] = jnp.zeros_like(gate_ref)
    for j in range(nb):
        blk = s * nb + j
        toks = slice(j * BLK, (j + 1) * BLK)
        hn_blk = hn_ref[toks, :]

        def window(w, carry):
            hits, starts = [], []
            for e in range(E):
                lo = tbl_ref[e, blk] + w * win
                start = pl.multiple_of(jnp.minimum(lo, cap - win), SLOT_ALIGN)
                slot = slot_ref[e, :, toks]
                slot = jnp.where(slot >= lo, slot, -1)
                hits.append(slot == w_i + start)
                starts.append(start)
            onehot = jnp.concatenate([jnp.where(h, 1.0, 0.0) for h in hits], axis=0).astype(jnp.bfloat16)
            rows = jnp.dot(onehot, hn_blk, preferred_element_type=jnp.float32)
            for e in range(E):
                dst = pl.ds(starts[e], win)
                xe_ref[e, dst, :] += rows[e * win:(e + 1) * win].astype(xe_ref.dtype)
                g = jnp.sum(jnp.where(hits[e], aff_ref[e, :, toks], 0.0), axis=1, keepdims=True)
                gate_ref[e, dst, :] += jnp.broadcast_to(g, (win, 128))
            return carry

        lax.fori_loop(0, tbl_ref[E, blk], window, 0)


def _moe_gather(slot, slot3, aff3, hn, row0, n_seq, t):
    E = slot3.shape[0]
    D = hn.shape[1]
    cap = EC_CAPACITY * t // N_EXPERTS
    win = min(cap, SCATTER_WINDOW)
    nb = t // BLK
    b0 = row0 // t
    tbl = _scatter_windows(slot, row0, n_seq, t, win)
    return pl.pallas_call(
        functools.partial(_moe_gather_kernel, cap=cap, win=win, nb=nb),
        grid_spec=pltpu.PrefetchScalarGridSpec(
            num_scalar_prefetch=1, grid=(n_seq,),
            in_specs=[pl.BlockSpec((E, 1, t), lambda s, tb_: (0, 0, b0 + s)),
                      pl.BlockSpec((E, 1, t), lambda s, tb_: (0, 0, b0 + s)),
                      pl.BlockSpec((t, D), lambda s, tb_: (b0 + s, 0))],
            out_specs=[pl.BlockSpec((E, cap, D), lambda s, tb_: (0, s, 0)),
                       pl.BlockSpec((E, cap, 128), lambda s, tb_: (0, s, 0))]),
        out_shape=[jax.ShapeDtypeStruct((E, n_seq * cap, D), jnp.bfloat16),
                   jax.ShapeDtypeStruct((E, n_seq * cap, 128), jnp.float32)],
        compiler_params=pltpu.CompilerParams(dimension_semantics=("arbitrary",),
                                             vmem_limit_bytes=MOE_FFN_VMEM_BYTES),
        name="moe_gather",
    )(tbl, slot3, aff3, hn)


def _moe_ffn_kernel(xc_ref, xl_ref, gc_ref, gl_ref, wg_ref, wu_ref, wd_ref, yc_ref, yl_ref, accc_ref, accl_ref):
    f = pl.program_id(1)
    nf = pl.num_programs(1)
    bf16 = jnp.bfloat16
    wg = wg_ref[0, 0].astype(bf16)
    wu = wu_ref[0, 0].astype(bf16)
    wd = wd_ref[0, 0].astype(bf16)

    def part(x_ref, acc_ref):
        x = x_ref[0]
        g = jnp.dot(x, wg, preferred_element_type=jnp.float32)
        u = jnp.dot(x, wu, preferred_element_type=jnp.float32)
        h = (g * jax.nn.sigmoid(g) * u).astype(bf16)
        y = jnp.dot(h, wd, preferred_element_type=jnp.float32)

        @pl.when(f == 0)
        def _():
            acc_ref[...] = y

        @pl.when(f != 0)
        def _():
            acc_ref[...] += y

    part(xc_ref, accc_ref)
    part(xl_ref, accl_ref)

    @pl.when(f == nf - 1)
    def _():
        for acc_ref, g_ref, y_ref in ((accc_ref, gc_ref, yc_ref), (accl_ref, gl_ref, yl_ref)):
            gate = jnp.concatenate([g_ref[0]] * (acc_ref.shape[1] // 128), axis=1)
            y_ref[0] = (acc_ref[...] * gate).astype(y_ref.dtype)


def _moe_ffn(xc, xl, gc, gl, wg, wu, wd, layer):
    E, nc_rows, D = xc.shape
    nl_rows = xl.shape[1]
    F = wg.shape[3]
    nf = F // FF_TILE
    return pl.pallas_call(
        _moe_ffn_kernel,
        grid=(E, nf),
        in_specs=[pl.BlockSpec((1, nc_rows, D), lambda e, f: (e, 0, 0)),
                  pl.BlockSpec((1, nl_rows, D), lambda e, f: (e, 0, 0)),
                  pl.BlockSpec((1, nc_rows, 128), lambda e, f: (e, 0, 0)),
                  pl.BlockSpec((1, nl_rows, 128), lambda e, f: (e, 0, 0)),
                  pl.BlockSpec((1, 1, D, FF_TILE), lambda e, f: (layer, e, 0, f)),
                  pl.BlockSpec((1, 1, D, FF_TILE), lambda e, f: (layer, e, 0, f)),
                  pl.BlockSpec((1, 1, FF_TILE, D), lambda e, f: (layer, e, f, 0))],
        out_specs=[pl.BlockSpec((1, nc_rows, D), lambda e, f: (e, 0, 0)),
                   pl.BlockSpec((1, nl_rows, D), lambda e, f: (e, 0, 0))],
        out_shape=[jax.ShapeDtypeStruct((E, nc_rows, D), jnp.bfloat16),
                   jax.ShapeDtypeStruct((E, nl_rows, D), jnp.bfloat16)],
        scratch_shapes=[pltpu.VMEM((nc_rows, D), jnp.float32), pltpu.VMEM((nl_rows, D), jnp.float32)],
        compiler_params=pltpu.CompilerParams(dimension_semantics=("arbitrary", "arbitrary"),
                                             vmem_limit_bytes=MOE_FFN_VMEM_BYTES),
        name="moe_ffn",
    )(xc, xl, gc, gl, wg, wu, wd)


def _moe_scatter_kernel(tbl_ref, slot_ref, ye_ref, x_ref, g2_ref, *rest, cap, win, nb, final):
    if final:
        fg_ref, o_ref, acc_ref = rest
    else:
        o_ref, acc_ref = rest
    E = ye_ref.shape[0]
    tb = x_ref.shape[0]
    blk = pl.program_id(0) * nb + pl.program_id(1)
    w_i = lax.broadcasted_iota(jnp.int32, (win, tb), 0)
    acc_ref[...] = jnp.zeros_like(acc_ref)

    def window(w, carry):
        hot, rows = [], []
        for e in range(E):
            lo = tbl_ref[e, blk] + w * win
            start = pl.multiple_of(jnp.minimum(lo, cap - win), SLOT_ALIGN)
            slot = slot_ref[e]
            slot = jnp.where(slot >= lo, slot, -1)
            hot.append(jnp.where(slot == w_i + start, 1.0, 0.0))
            rows.append(ye_ref[e, pl.ds(start, win), :])
        onehot = jnp.concatenate(hot, axis=0).astype(jnp.bfloat16)
        acc_ref[...] += lax.dot_general(onehot, jnp.concatenate(rows, axis=0), (((0,), (0,)), ((), ())),
                                        preferred_element_type=jnp.float32)
        return carry

    lax.fori_loop(0, tbl_ref[E, blk], window, 0)
    x = x_ref[...] + g2_ref[0] * acc_ref[...]
    if final:
        x = x * lax.rsqrt(jnp.mean(x * x, -1, keepdims=True) + NORM_EPS) * fg_ref[...]
    o_ref[...] = x


def _scatter_windows(slot, row0, n_seq, t, win):
    E = slot.shape[0]
    cap = EC_CAPACITY * t // N_EXPERTS
    nb = t // BLK
    cnt = jnp.sum(slot[:, row0:row0 + n_seq * t].reshape(E, n_seq, nb, BLK) >= 0, axis=3)
    first = jnp.cumsum(cnt, axis=2) - cnt
    start = jnp.minimum(first // SLOT_ALIGN * SLOT_ALIGN, cap - win)
    n_win = jnp.maximum(jnp.max((first + cnt - start + win - 1) // win, axis=0), 1)
    return jnp.concatenate([start.reshape(E, n_seq * nb), n_win.reshape(1, n_seq * nb)]).astype(jnp.int32)


def _moe_scatter(slot, slot3, ye, x, g2blk, row0, n_seq, t, final_g=None):
    E, _, D = ye.shape
    cap = EC_CAPACITY * t // N_EXPERTS
    win = min(cap, SCATTER_WINDOW)
    nb = t // BLK
    b0 = row0 // BLK
    final = final_g is not None
    tbl = _scatter_windows(slot, row0, n_seq, t, win)
    in_specs = [pl.BlockSpec((E, 1, BLK), lambda s, j, tb_: (0, 0, b0 + s * nb + j)),
                pl.BlockSpec((E, cap, D), lambda s, j, tb_: (0, s, 0)),
                pl.BlockSpec((BLK, D), lambda s, j, tb_: (b0 + s * nb + j, 0)),
                pl.BlockSpec((1, 1, D), lambda s, j, tb_: (b0 + s * nb + j, 0, 0))]
    args = [tbl, slot3, ye, x, g2blk]
    if final:
        in_specs.append(pl.BlockSpec((1, D), lambda s, j, tb_: (0, 0)))
        args.append(final_g.reshape(1, D))
        out_specs = pl.BlockSpec((BLK, D), lambda s, j, tb_: (s * nb + j, 0))
        out_shape = jax.ShapeDtypeStruct((n_seq * t, D), jnp.float32)
        aliases = {}
    else:
        out_specs = pl.BlockSpec((BLK, D), lambda s, j, tb_: (b0 + s * nb + j, 0))
        out_shape = jax.ShapeDtypeStruct(x.shape, jnp.float32)
        aliases = {3: 0}
    return pl.pallas_call(
        functools.partial(_moe_scatter_kernel, cap=cap, win=win, nb=nb, final=final),
        grid_spec=pltpu.PrefetchScalarGridSpec(
            num_scalar_prefetch=1, grid=(n_seq, nb), in_specs=in_specs, out_specs=out_specs,
            scratch_shapes=[pltpu.VMEM((BLK, D), jnp.float32)]),
        out_shape=out_shape,
        input_output_aliases=aliases,
        compiler_params=pltpu.CompilerParams(dimension_semantics=("arbitrary", "arbitrary"),
                                             vmem_limit_bytes=MOE_FFN_VMEM_BYTES),
        name="moe_scatter",
    )(*args)


def moe_layer(x, hn, affT, g2blk, wg, wu, wd, layer, n_ctx, l_lat, final_g=None):
    R = x.shape[0]
    r_ctx = n_ctx * BLK
    n_lat = (R - r_ctx) // l_lat
    slot = jnp.concatenate([_moe_select(affT, 0, n_ctx, BLK), _moe_select(affT, r_ctx, n_lat, l_lat)], axis=1)
    slot3 = slot[:, None, :]
    aff3 = affT[:, None, :]
    xc, gc = _moe_gather(slot, slot3, aff3, hn, 0, n_ctx, BLK)
    xl, gl = _moe_gather(slot, slot3, aff3, hn, r_ctx, n_lat, l_lat)
    yc, yl = _moe_ffn(xc, xl, gc, gl, wg, wu, wd, layer)
    if final_g is None:
        x = _moe_scatter(slot, slot3, yc, x, g2blk, 0, n_ctx, BLK)
        return _moe_scatter(slot, slot3, yl, x, g2blk, r_ctx, n_lat, l_lat)
    return (_moe_scatter(slot, slot3, yc, x, g2blk, 0, n_ctx, BLK, final_g),
            _moe_scatter(slot, slot3, yl, x, g2blk, r_ctx, n_lat, l_lat, final_g))


L0_Z = (0, 1024)
L0_XBC = (1024, 2560)
L0_SHIFT = (2560, 6016)
L0_DT = (6016, 6144)
L0_COLS = 6144
L0_VMEM_BYTES = 56 * 1024 * 1024


def _sum_split(x, m, n_split):
    acc = None
    for _ in range(n_split):
        hi = x.astype(jnp.bfloat16)
        x = x - hi.astype(jnp.float32)
        t = jnp.dot(hi, m, preferred_element_type=jnp.float32)
        acc = t if acc is None else acc + t
    return acc


def _sum_split_left(m, x, n_split):
    acc = None
    for _ in range(n_split):
        hi = x.astype(jnp.bfloat16)
        x = x - hi.astype(jnp.float32)
        t = jnp.dot(m, hi, preferred_element_type=jnp.float32)
        acc = t if acc is None else acc + t
    return acc


def _head_sum(x, e_ref, et_ref):
    return _sum_split(_sum_split(x, e_ref[...], 2), et_ref[...], 2)


def _adaln(x, g, sc, sh):
    y = x * lax.rsqrt(jnp.mean(x * x, -1, keepdims=True) + NORM_EPS) * g
    return y * (1.0 + sc) + sh


def _softplus(x):
    return jnp.maximum(x, 0.0) + jnp.log(1.0 + jnp.exp(-jnp.abs(x)))


def _l0_in_kernel(tbl_ref, *refs, n_parts, n_a):
    x_refs, xp_refs, xn_refs = refs[:n_parts], refs[n_parts:2 * n_parts], refs[2 * n_parts:3 * n_parts]
    (g_ref, sc_ref, sh_ref, w_ref, mup_ref, mun_ref, rwp_ref, w2_ref, a2_ref, g2_ref, e_ref, et_ref,
     z_ref, xbc_ref, dt_ref, r_ref, v_ref, an_ref, lw_ref, kd_ref, bv_ref, gate_ref, bonus_ref) = refs[3 * n_parts:]
    i = pl.program_id(0)
    f32, bf16 = jnp.float32, jnp.bfloat16
    g, sc, sh = g_ref[...], sc_ref[0], sh_ref[0]
    hn = _adaln(_residual_rows(x_refs, n_a), g, sc, sh).astype(bf16)
    halo = _adaln(jnp.concatenate([_residual_rows(xp_refs, n_a), _residual_rows(xn_refs, n_a)], axis=0),
                  g, sc, sh).astype(bf16)
    hn_halo = jnp.concatenate([hn, halo], axis=0)

    keep_prev = (1 - tbl_ref[0, i]).astype(f32)
    keep_next = (1 - tbl_ref[1, i]).astype(f32)
    row = lax.broadcasted_iota(jnp.int32, (BLK, 1), 0)
    c = RWKV_DIM

    def plain(c0, c1):
        return jnp.dot(hn, w_ref[:, c0:c1], preferred_element_type=f32)

    def proj(c0, c1):
        return jnp.dot(hn_halo, w_ref[:, L0_SHIFT[0] + c0:L0_SHIFT[0] + c1], preferred_element_type=f32)

    def shift(both, c0, c1):
        cur = both[:BLK]
        prev = jnp.where(row == 0, both[BLK + 7:BLK + 8] * keep_prev, pltpu.roll(cur, 1, 0))
        nxt = jnp.where(row == BLK - 1, both[BLK + 8:BLK + 9] * keep_next, pltpu.roll(cur, BLK - 1, 0))
        return cur + mup_ref[:, c0:c1] * (prev - cur) + mun_ref[:, c0:c1] * (nxt - cur)

    lo = 3 * c + 2 * W_LORA
    p_wl = proj(3 * c, lo)
    p_ag = proj(lo, lo + 256)
    p_k = proj(c, 2 * c)
    wl = shift(p_wl, 3 * c, lo)
    ag = shift(p_ag, lo, lo + 256)
    p_r = proj(0, c)
    k = shift(p_k, c, 2 * c)
    k_k, k_a, r_k = rwp_ref[0:1], rwp_ref[1:2], rwp_ref[2:3]
    w_lin = jnp.dot(jnp.tanh(wl).astype(bf16), w2_ref[...], preferred_element_type=f32)
    a_lora = jnp.dot(ag.astype(bf16), a2_ref[...], preferred_element_type=f32)
    gate_ref[...] = jnp.dot(jax.nn.sigmoid(ag).astype(bf16), g2_ref[...],
                            preferred_element_type=f32).astype(gate_ref.dtype)
    p_v = proj(2 * c, 3 * c)
    r = shift(p_r, 0, c)
    r_ref[...] = r.astype(r_ref.dtype)
    kk = k * k_k
    kk = kk * lax.rsqrt(_head_sum(kk * kk, e_ref, et_ref) + 1e-12)
    an_ref[...] = (-kk).astype(an_ref.dtype)
    z_ref[...] = plain(L0_Z[0], L0_Z[1]).astype(z_ref.dtype)
    v = shift(p_v, 2 * c, 3 * c)
    v_ref[...] = v.astype(v_ref.dtype)
    kd_sum = None
    xbc_cols = (L0_XBC[0], (L0_XBC[0] + L0_XBC[1]) // 2, L0_XBC[1])
    for d in range(2):
        xbc_ref[:, xbc_cols[d] - L0_XBC[0]:xbc_cols[d + 1] - L0_XBC[0]] = plain(xbc_cols[d], xbc_cols[d + 1])
        w_log = -_softplus(-(rwp_ref[3 + d:4 + d] + w_lin[:, d * c:(d + 1) * c])) - 0.5
        lw_ref[d] = -jnp.exp(w_log)
        a = jax.nn.sigmoid(rwp_ref[5 + d:6 + d] + a_lora)
        kd = k * (1.0 + (a - 1.0) * k_a)
        kd_ref[d] = kd.astype(kd_ref.dtype)
        bv_ref[d] = (kk * a).astype(bv_ref.dtype)
        kd_sum = kd if kd_sum is None else kd_sum + kd
    dt_ref[...] = plain(L0_DT[0], L0_DT[1])
    bonus_ref[...] = (_head_sum(r * kd_sum * r_k, e_ref, et_ref) * v).astype(bonus_ref.dtype)


def _seq_tables(n_ctx, n_lat, l_lat):
    nb = l_lat // BLK
    cond = [0] * n_ctx + [1 + s for s in range(n_lat) for _ in range(nb)]
    first = [1] * n_ctx + [1 if j == 0 else 0 for _ in range(n_lat) for j in range(nb)]
    last = [1] * n_ctx + [1 if j == nb - 1 else 0 for _ in range(n_lat) for j in range(nb)]
    return cond, first, last


def l0_in(x, g1, scb, shb, w_packed, mup, mun, rwp, w2bd, a2p, g2p, e_ind, et_ind, first, last):
    xs = x if isinstance(x, (tuple, list)) else (x,)
    D = xs[0].shape[1]
    R = sum(a.shape[0] for a in xs)
    nblk = R // BLK
    n_a = xs[0].shape[0] // BLK
    tbl = jnp.asarray([first, last], jnp.int32)
    c = RWKV_DIM
    row = lambda i, t: (i, 0)
    full = lambda shape: pl.BlockSpec(shape, lambda i, t: (0,) * len(shape))
    rows = lambda n: pl.BlockSpec((BLK, n), row)
    rows2 = lambda n: pl.BlockSpec((2, BLK, n), lambda i, t: (0, i, 0))
    f32, act = jnp.float32, ACT_DTYPE
    sds = jax.ShapeDtypeStruct
    h8 = BLK // 8

    def part_specs(k):
        nb_k, b0 = xs[k].shape[0] // BLK, (0 if k == 0 else n_a)
        local = lambda i: jnp.clip(i - b0, 0, nb_k - 1)
        return (pl.BlockSpec((BLK, D), lambda i, t: (local(i), 0)),
                pl.BlockSpec((8, D), lambda i, t: (jnp.maximum(local(i) * h8 - 1, 0), 0)),
                pl.BlockSpec((8, D), lambda i, t: (jnp.minimum((local(i) + 1) * h8, nb_k * h8 - 1), 0)))

    specs = [part_specs(k) for k in range(len(xs))]
    x_specs = [s[j] for j in range(3) for s in specs]
    return pl.pallas_call(
        functools.partial(_l0_in_kernel, n_parts=len(xs), n_a=n_a),
        grid_spec=pltpu.PrefetchScalarGridSpec(
            num_scalar_prefetch=1,
            grid=(nblk,),
            in_specs=[*x_specs,
                      full((1, D)),
                      pl.BlockSpec((1, 1, D), lambda i, t: (i, 0, 0)),
                      pl.BlockSpec((1, 1, D), lambda i, t: (i, 0, 0)),
                      full((D, L0_COLS)), full(mup.shape), full(mun.shape), full(rwp.shape),
                      full(w2bd.shape), full(a2p.shape), full(g2p.shape), full(e_ind.shape), full(et_ind.shape)],
            out_specs=[rows(c), rows(SSD_XBC), rows(128), rows(c), rows(c), rows(c),
                       rows2(c), rows2(c), rows2(c), rows(c), rows(c)]),
        out_shape=[sds((R, c), act), sds((R, SSD_XBC), f32), sds((R, 128), f32), sds((R, c), act), sds((R, c), act),
                   sds((R, c), act), sds((2, R, c), f32), sds((2, R, c), act), sds((2, R, c), act),
                   sds((R, c), act), sds((R, c), act)],
        compiler_params=pltpu.CompilerParams(dimension_semantics=("arbitrary",), vmem_limit_bytes=L0_VMEM_BYTES),
        name="l0_in",
    )(tbl, *xs, *xs, *xs, g1, scb, shb, w_packed, mup, mun, rwp, w2bd, a2p, g2p, e_ind, et_ind)


def l0_pack_weights(p, e):
    bf16 = jnp.bfloat16
    w = p['ab_w_in'][e]
    D = w.shape[0]
    c = RWKV_DIM
    rw0 = SSD_IN
    ag0 = rw0 + 3 * c + 2 * W_LORA
    w_packed = jnp.concatenate([
        w[:, :SSD_INNER + SSD_XBC], w[:, rw0:ag0], w[:, ag0:ag0 + A_LORA + G_LORA],
        jnp.zeros((D, 256 - A_LORA - G_LORA), w.dtype),
        w[:, SSD_INNER + SSD_XBC:SSD_IN], jnp.zeros((D, 128 - SSD_HEADS), w.dtype)], axis=1).astype(bf16)

    def pack_mu(mu):
        return jnp.concatenate([mu, jnp.zeros((256 - A_LORA - G_LORA,), mu.dtype)])[None, :]

    rwp = jnp.stack([p['rwkv_k_k'][e], p['rwkv_k_a'][e], p['rwkv_r_k'][e].reshape(-1), p['rwkv_w0'][e, 0],
                     p['rwkv_w0'][e, 1], p['rwkv_a0'][e, 0], p['rwkv_a0'][e, 1], jnp.zeros((c,), jnp.float32)])
    zw = jnp.zeros((W_LORA, c), jnp.float32)
    w2bd = jnp.concatenate([jnp.concatenate([p['rwkv_w2'][e, 0], zw], axis=1),
                            jnp.concatenate([zw, p['rwkv_w2'][e, 1]], axis=1)], axis=0).astype(bf16)
    a2p = jnp.concatenate([p['rwkv_a2'][e], jnp.zeros((256 - A_LORA, c), jnp.float32)], axis=0).astype(bf16)
    g2p = jnp.concatenate([jnp.zeros((A_LORA, c), jnp.float32), p['rwkv_g2'][e],
                           jnp.zeros((256 - A_LORA - G_LORA, c), jnp.float32)], axis=0).astype(bf16)
    head = jnp.arange(c) // RWKV_N
    e_ind = (head[:, None] == jnp.arange(128)[None, :]).astype(bf16)
    return w_packed, pack_mu(p['rwkv_mu_prev'][e]), pack_mu(p['rwkv_mu_next'][e]), rwp, w2bd, a2p, g2p, e_ind, e_ind.T


SSD_QH = 4
SSD_VMEM_BYTES = 48 * 1024 * 1024
NEG_BIG = -1e30
LOG2E = 1.4426950408889634


def _conv_silu(cur, prev_row, next_row, w_ref, b_ref):
    row = lax.broadcasted_iota(jnp.int32, (BLK, 1), 0)
    prev = jnp.where(row == 0, prev_row, pltpu.roll(cur, 1, 0))
    nxt = jnp.where(row == BLK - 1, next_row, pltpu.roll(cur, BLK - 1, 0))
    y = w_ref[0:1] * prev + w_ref[1:2] * cur + w_ref[2:3] * nxt + b_ref[...]
    return y * jax.nn.sigmoid(y)


def _ssd_kernel(tbl_ref, xs_ref, b_ref, c_ref, dt_ref, cwx_ref, cwb_ref, cwc_ref, cbx_ref, cbb_ref, cbc_ref,
                sel_ref, hp_ref, s0_ref, y_ref, fs_ref, xa_ref, ba_ref, ca_ref, sfx_ref, ldb_ref, st_ref, *, n_zero):
    f32, bf16 = jnp.float32, jnp.bfloat16
    sb = pl.program_id(0)
    sbr, qw = xs_ref.shape
    nch = sbr // BLK
    P, QH = SSD_P, SSD_QH
    t_i = lax.broadcasted_iota(jnp.int32, (BLK, BLK), 0)
    s_i = lax.broadcasted_iota(jnp.int32, (BLK, BLK), 1)
    lower = s_i <= t_i
    upper = s_i >= t_i
    tri_lo = jnp.where(lower, 1.0, 0.0).astype(bf16)
    tri_up = jnp.where(upper, 1.0, 0.0).astype(bf16)
    hp = hp_ref[0]
    sel = sel_ref[0]
    ind = jnp.where(lax.broadcasted_iota(jnp.int32, (128, qw), 1) // P == lax.broadcasted_iota(jnp.int32, (128, qw), 0),
                    1.0, 0.0).astype(bf16)
    head_of_lane = lax.broadcasted_iota(jnp.int32, (1, qw), 1) // P

    def expand(cols):
        return _sum_split(cols, ind, 2)

    d_row = expand(jnp.broadcast_to(hp[4:5], (8, 128)))[0:1]

    def chunk_rows(c):
        return pl.ds(pl.multiple_of(c * BLK, BLK), BLK)

    def neighbours(ref, c, keep_prev, keep_next):
        lo = jnp.maximum(c * BLK - 1, 0)
        hi = jnp.minimum((c + 1) * BLK, sbr - 1)
        return ref[pl.ds(lo, 1), :] * keep_prev, ref[pl.ds(hi, 1), :] * keep_next

    def fwd(c, carry):
        blk = sb * nch + c
        first, last = tbl_ref[0, blk], tbl_ref[1, blk]
        kp, kn = (1 - first).astype(f32), (1 - last).astype(f32)
        rows = chunk_rows(c)

        @pl.when(first == 1)
        def _():
            st_ref[0] = jnp.where(sb >= n_zero, s0_ref[0, 0], 0.0)

        xa = _conv_silu(xs_ref[rows, :], *neighbours(xs_ref, c, kp, kn), cwx_ref, cbx_ref)
        bm = _conv_silu(b_ref[rows, :], *neighbours(b_ref, c, kp, kn), cwb_ref, cbb_ref)
        cm = _conv_silu(c_ref[rows, :], *neighbours(c_ref, c, kp, kn), cwc_ref, cbc_ref)
        xb, bmb, cmb = xa.astype(bf16), bm.astype(bf16), cm.astype(bf16)
        xa_ref[rows, :] = xb
        ba_ref[rows, :] = bmb
        ca_ref[rows, :] = cmb
        dtq = _sum_split(dt_ref[rows, :], sel, 2)
        dtf = _softplus(dtq + hp[0:1])
        dtb = _softplus(dtq + hp[1:2])
        acs = _sum_split_left(tri_lo, dtf * hp[2:3], 3)
        sfx = _sum_split_left(tri_up, dtb * hp[3:4], 3)
        ldf, ldb = jnp.log(dtf), jnp.log(dtb)
        sfx_ref[rows, :] = sfx
        ldb_ref[rows, :] = ldb
        a2, s2 = acs * LOG2E, sfx * LOG2E
        a2r = (a2 - ldf * LOG2E).T
        s2r = (s2 - ldb * LOG2E).T
        g = lax.dot_general(cmb, bmb, (((1,), (1,)), ((), ())), preferred_element_type=f32)
        y_diag = None
        for j in range(QH):
            m = (g * (jnp.exp2(jnp.where(lower, a2[:, j:j + 1] - a2r[j:j + 1, :], NEG_BIG))
                      + jnp.exp2(jnp.where(upper, s2[:, j:j + 1] - s2r[j:j + 1, :], NEG_BIG)))).astype(bf16)
            xh = jnp.where(head_of_lane == j, xb, jnp.zeros((), bf16))
            t = jnp.dot(m, xh, preferred_element_type=f32)
            y_diag = t if y_diag is None else y_diag + t
        ea = jnp.exp(expand(acs))
        wf = jnp.exp(expand(acs[BLK - 1:BLK] - acs + ldf))
        s_in = st_ref[0]
        y_ref[rows, :] = xa * d_row + y_diag + ea * jnp.dot(cmb, s_in.astype(bf16), preferred_element_type=f32)
        st_ref[0] = ea[BLK - 1:BLK] * s_in + lax.dot_general(
            bmb, (xa * wf).astype(bf16), (((0,), (0,)), ((), ())), preferred_element_type=f32)
        fs_ref[c, 0] = st_ref[0]
        return carry

    lax.fori_loop(0, nch, fwd, 0)

    def bwd(k, carry):
        c = nch - 1 - k
        blk = sb * nch + c
        rows = chunk_rows(c)

        @pl.when(tbl_ref[1, blk] == 1)
        def _():
            st_ref[1] = jnp.where(sb >= n_zero, s0_ref[0, 1], 0.0)

        sfx = sfx_ref[rows, :]
        eb = jnp.exp(expand(sfx))
        wb = jnp.exp(expand(sfx[0:1] - sfx + ldb_ref[rows, :]))
        s_in = st_ref[1]
        y_ref[rows, :] += eb * jnp.dot(ca_ref[rows, :], s_in.astype(bf16), preferred_element_type=f32)
        st_ref[1] = eb[0:1] * s_in + lax.dot_general(
            ba_ref[rows, :], (xa_ref[rows, :].astype(f32) * wb).astype(bf16), (((0,), (0,)), ((), ())),
            preferred_element_type=f32)
        fs_ref[c, 1] = st_ref[1]
        return carry

    lax.fori_loop(0, nch, bwd, 0)


def ssd_scan(xbc, dt, conv_w, conv_b, sel, hp, s0, n_zero, first, last, sb_rows):
    R = xbc.shape[0]
    n_sb = R // sb_rows
    nch = sb_rows // BLK
    nq = SSD_HEADS // SSD_QH
    qw = SSD_QH * SSD_P
    qpg = SSD_HPG // SSD_QH
    b_blk = SSD_INNER // SSD_N
    c_blk = b_blk + SSD_GROUPS
    tbl = jnp.asarray([first, last], jnp.int32)
    cw = conv_w
    cb = conv_b.reshape(1, -1)
    f32, bf16 = jnp.float32, jnp.bfloat16
    return pl.pallas_call(
        functools.partial(_ssd_kernel, n_zero=n_zero),
        grid_spec=pltpu.PrefetchScalarGridSpec(
            num_scalar_prefetch=1,
            grid=(n_sb, nq),
            in_specs=[pl.BlockSpec((sb_rows, qw), lambda s, q, t: (s, q)),
                      pl.BlockSpec((sb_rows, SSD_N), lambda s, q, t: (s, b_blk + q // qpg)),
                      pl.BlockSpec((sb_rows, SSD_N), lambda s, q, t: (s, c_blk + q // qpg)),
                      pl.BlockSpec((sb_rows, 128), lambda s, q, t: (s, 0)),
                      pl.BlockSpec((3, qw), lambda s, q, t: (0, q)),
                      pl.BlockSpec((3, SSD_N), lambda s, q, t: (0, b_blk + q // qpg)),
                      pl.BlockSpec((3, SSD_N), lambda s, q, t: (0, c_blk + q // qpg)),
                      pl.BlockSpec((1, qw), lambda s, q, t: (0, q)),
                      pl.BlockSpec((1, SSD_N), lambda s, q, t: (0, b_blk + q // qpg)),
                      pl.BlockSpec((1, SSD_N), lambda s, q, t: (0, c_blk + q // qpg)),
                      pl.BlockSpec((1, 128, 128), lambda s, q, t: (q, 0, 0)),
                      pl.BlockSpec((1, 8, 128), lambda s, q, t: (q, 0, 0)),
                      pl.BlockSpec((1, 2, SSD_N, qw), lambda s, q, t: (jnp.maximum(s - n_zero, 0), 0, 0, q))],
            out_specs=[pl.BlockSpec((sb_rows, qw), lambda s, q, t: (s, q)),
                       pl.BlockSpec((nch, 2, SSD_N, qw), lambda s, q, t: (s, 0, 0, q))],
            scratch_shapes=[pltpu.VMEM((sb_rows, qw), bf16), pltpu.VMEM((sb_rows, SSD_N), bf16),
                            pltpu.VMEM((sb_rows, SSD_N), bf16), pltpu.VMEM((sb_rows, 128), f32),
                            pltpu.VMEM((sb_rows, 128), f32), pltpu.VMEM((2, SSD_N, qw), f32)]),
        out_shape=[jax.ShapeDtypeStruct((R, SSD_INNER), f32),
                   jax.ShapeDtypeStruct((R // BLK, 2, SSD_N, SSD_INNER), f32)],
        compiler_params=pltpu.CompilerParams(dimension_semantics=("arbitrary", "arbitrary"),
                                             vmem_limit_bytes=SSD_VMEM_BYTES),
        name="ssd_scan",
    )(tbl, xbc, xbc, xbc, dt, cw, cw, cw, cb, cb, cb, sel, hp, s0)


def ssd_tables(p, e):
    nq = SSD_HEADS // SSD_QH
    lane = jnp.arange(128)
    sel = jnp.stack([(lane[:, None] == (q * SSD_QH + lane[None, :])) & (lane[None, :] < SSD_QH)
                     for q in range(nq)]).astype(jnp.bfloat16)
    a_neg = -jnp.exp(p['ssd_a_log'][e].astype(jnp.float32))
    rows = jnp.stack([p['ssd_dt_bias'][e, 0], p['ssd_dt_bias'][e, 1], a_neg[0], a_neg[1], p['ssd_d'][e]])
    hp = jnp.zeros((nq, 8, 128), jnp.float32)
    hp = hp.at[:, :5, :SSD_QH].set(jnp.transpose(rows.reshape(5, nq, SSD_QH), (1, 0, 2)))
    return sel, hp


MIX_VMEM_BYTES = 48 * 1024 * 1024
MIX_BLOCKS = 1
ROUTER_LANES = 128


def _residual_norm_router(x, out, g1, n2g, sc2, sh2, rw_ref, x_out_ref, hn_ref, aff_ref):
    rows, d = x.shape
    nb = g1.shape[0]
    x_new = x.reshape(nb, rows // nb, d) + g1 * out.reshape(nb, rows // nb, d)
    x_out_ref[...] = x_new.reshape(rows, d)
    hn = _adaln(x_new, n2g, sc2, sh2).reshape(rows, d)
    hn_ref[...] = hn.astype(hn_ref.dtype)
    logits = _dot(hn, rw_ref[...], ((1,), (0,)), 3)
    lane = lax.broadcasted_iota(jnp.int32, logits.shape, 1)
    logits = jnp.where(lane < N_EXPERTS, logits, NEG_BIG)
    ex = jnp.exp(logits - jnp.max(logits, axis=-1, keepdims=True))
    aff = ex / jnp.sum(ex, axis=-1, keepdims=True)
    aff_ref[...] = aff.T[:N_EXPERTS]


def _residual_rows(x_refs, n_a):
    if len(x_refs) == 1:
        return x_refs[0][...]
    return jnp.where(pl.program_id(0) < n_a, x_refs[0][...], x_refs[1][...])


def _l0_out_kernel(ys_ref, z_ref, yf_ref, yb_ref, bonus_ref, gate_ref, sg_ref, lnw_ref, lnb_ref, e_ref, et_ref, w_ref,
                   *rest, n_a):
    *x_refs, g1_ref, n2g_ref, sc2_ref, sh2_ref, rw_ref, x_out_ref, hn_ref, aff_ref = rest
    f32, bf16 = jnp.float32, jnp.bfloat16
    z = z_ref[...].astype(f32)
    ys = ys_ref[...] * (z * jax.nn.sigmoid(z))
    gw = SSD_INNER // SSD_GROUPS
    parts = []
    for gi in range(SSD_GROUPS):
        yg = ys[:, gi * gw:(gi + 1) * gw]
        parts.append(yg * lax.rsqrt(jnp.mean(yg * yg, -1, keepdims=True) + NORM_EPS))
    a1 = jnp.concatenate(parts, axis=1) * sg_ref[...]
    o = yf_ref[...].astype(f32) + yb_ref[...].astype(f32)
    mu = _head_sum(o, e_ref, et_ref) * (1.0 / RWKV_N)
    oc = o - mu
    var = _head_sum(oc * oc, e_ref, et_ref) * (1.0 / RWKV_N)
    o = oc * lax.rsqrt(var + RWKV_GN_EPS) * lnw_ref[...] + lnb_ref[...]
    o = (o + bonus_ref[...].astype(f32)) * gate_ref[...].astype(f32)
    out = (jnp.dot(a1.astype(bf16), w_ref[:SSD_INNER], preferred_element_type=f32)
           + jnp.dot(o.astype(bf16), w_ref[SSD_INNER:], preferred_element_type=f32))
    _residual_norm_router(_residual_rows(x_refs, n_a), out, g1_ref[...], n2g_ref[...], sc2_ref[...], sh2_ref[...],
                          rw_ref, x_out_ref, hn_ref, aff_ref)


def _l1_out_kernel(a_ref, w_ref, *rest, n_a):
    *x_refs, g1_ref, n2g_ref, sc2_ref, sh2_ref, rw_ref, x_out_ref, hn_ref, aff_ref = rest
    out = jnp.dot(a_ref[...].astype(jnp.bfloat16), w_ref[...], preferred_element_type=jnp.float32)
    _residual_norm_router(_residual_rows(x_refs, n_a), out, g1_ref[...], n2g_ref[...], sc2_ref[...], sh2_ref[...],
                          rw_ref, x_out_ref, hn_ref, aff_ref)


def _mix_out_call(kernel_fn, name, lead_args, lead_specs, w_out, x, g1b, n2g, sc2b, sh2b, router_w):
    mb = MIX_BLOCKS
    xs = x if isinstance(x, (tuple, list)) else (x,)
    D = xs[0].shape[1]
    R = sum(a.shape[0] for a in xs)
    n_a = xs[0].shape[0] // (mb * BLK)
    assert (R // BLK) % mb == 0 and xs[0].shape[0] % (mb * BLK) == 0
    full = lambda a: pl.BlockSpec(a.shape, lambda i: (0,) * a.ndim)
    blkrow = pl.BlockSpec((mb, 1, D), lambda i: (i, 0, 0))
    rw = jnp.zeros((D, ROUTER_LANES), jnp.float32).at[:, :N_EXPERTS].set(router_w)
    args = list(lead_args) + [w_out, *xs, g1b, n2g, sc2b, sh2b, rw]
    if len(xs) == 1:
        x_specs = [pl.BlockSpec((mb * BLK, D), lambda i: (i, 0))]
        aliases = {len(lead_args) + 1: 0}
    else:
        x_specs = [pl.BlockSpec((mb * BLK, D), lambda i: (jnp.minimum(i, n_a - 1), 0)),
                   pl.BlockSpec((mb * BLK, D), lambda i: (jnp.maximum(i - n_a, 0), 0))]
        aliases = {}
    in_specs = list(lead_specs) + [full(w_out), *x_specs, blkrow, full(n2g), blkrow, blkrow, full(rw)]
    return pl.pallas_call(
        functools.partial(kernel_fn, n_a=n_a),
        grid=(R // (mb * BLK),),
        in_specs=in_specs,
        out_specs=[pl.BlockSpec((mb * BLK, D), lambda i: (i, 0)), pl.BlockSpec((mb * BLK, D), lambda i: (i, 0)),
                   pl.BlockSpec((N_EXPERTS, mb * BLK), lambda i: (0, i))],
        out_shape=[jax.ShapeDtypeStruct((R, D), jnp.float32), jax.ShapeDtypeStruct((R, D), jnp.bfloat16),
                   jax.ShapeDtypeStruct((N_EXPERTS, R), jnp.float32)],
        input_output_aliases=aliases,
        compiler_params=pltpu.CompilerParams(dimension_semantics=("arbitrary",), vmem_limit_bytes=MIX_VMEM_BYTES),
        name=name,
    )(*args)


def l0_out(ys, z, yf, yb, bonus, gate, ssd_g, ln_w, ln_b, e_ind, et_ind, w_out, x, g1b, n2g, sc2b, sh2b, router_w):
    c = RWKV_DIM
    rows = lambda n: pl.BlockSpec((MIX_BLOCKS * BLK, n), lambda i: (i, 0))
    full = lambda a: pl.BlockSpec(a.shape, lambda i: (0,) * a.ndim)
    lead = [ys, z, yf, yb, bonus, gate, ssd_g, ln_w, ln_b, e_ind, et_ind]
    specs = [rows(SSD_INNER), rows(SSD_INNER), rows(c), rows(c), rows(c), rows(c),
             full(ssd_g), full(ln_w), full(ln_b), full(e_ind), full(et_ind)]
    return _mix_out_call(_l0_out_kernel, "l0_out", lead, specs, w_out, x, g1b, n2g, sc2b, sh2b, router_w)


def l1_out(a, w_out, x, g1b, n2g, sc2b, sh2b, router_w):
    specs = [pl.BlockSpec((MIX_BLOCKS * BLK, a.shape[1]), lambda i: (i, 0))]
    return _mix_out_call(_l1_out_kernel, "l1_out", [a], specs, w_out, x, g1b, n2g, sc2b, sh2b, router_w)


RET_HPS = 2


def _l1_in_kernel(tbl_ref, x_ref, g_ref, sc_ref, sh_ref, w_ref, cos_ref, sin_ref, q_ref, k_ref, v_ref, gg_ref):
    f32 = jnp.float32
    hn = _adaln(x_ref[...], g_ref[...], sc_ref[0], sh_ref[0]).astype(jnp.bfloat16)
    cosf, sinf = cos_ref[0], sin_ref[0]

    def rope(x):
        parts = []
        for h in range(RET_HEADS):
            xh = x[:, h * RET_DK:(h + 1) * RET_DK]
            parts.append(xh * cosf + pltpu.roll(xh, RET_DK // 2, 1) * sinf)
        return jnp.concatenate(parts, axis=1)

    q_ref[...] = rope(jnp.dot(hn, w_ref[:, :RET_QK], preferred_element_type=f32)).astype(q_ref.dtype)
    k_ref[...] = (rope(jnp.dot(hn, w_ref[:, RET_QK:2 * RET_QK], preferred_element_type=f32))
                  * (RET_DK ** -0.5)).astype(k_ref.dtype)
    v_ref[...] = jnp.dot(hn, w_ref[:, 2 * RET_QK:2 * RET_QK + RET_V], preferred_element_type=f32).astype(v_ref.dtype)
    gg_ref[...] = jnp.dot(hn, w_ref[:, 2 * RET_QK + RET_V:], preferred_element_type=f32).astype(gg_ref.dtype)


def l1_in(x, g1, scb, shb, w_bf16, cos_t, sin_t, rope_blk):
    R, D = x.shape
    nblk = R // BLK
    tbl = jnp.asarray([rope_blk], jnp.int32)
    f32 = jnp.float32
    row = lambda n: pl.BlockSpec((BLK, n), lambda i, t: (i, 0))
    full = lambda a: pl.BlockSpec(a.shape, lambda i, t: (0,) * a.ndim)
    blkrow = pl.BlockSpec((1, 1, D), lambda i, t: (i, 0, 0))
    ropespec = pl.BlockSpec((1, BLK, RET_DK), lambda i, t: (t[0, i], 0, 0))
    return pl.pallas_call(
        _l1_in_kernel,
        grid_spec=pltpu.PrefetchScalarGridSpec(
            num_scalar_prefetch=1, grid=(nblk,),
            in_specs=[row(D), full(g1), blkrow, blkrow, full(w_bf16), ropespec, ropespec],
            out_specs=[row(RET_QK), row(RET_QK), row(RET_V), row(RET_V)]),
        out_shape=[jax.ShapeDtypeStruct((R, RET_QK), ACT_DTYPE), jax.ShapeDtypeStruct((R, RET_QK), ACT_DTYPE),
                   jax.ShapeDtypeStruct((R, RET_V), ACT_DTYPE), jax.ShapeDtypeStruct((R, RET_V), ACT_DTYPE)],
        compiler_params=pltpu.CompilerParams(dimension_semantics=("arbitrary",), vmem_limit_bytes=L0_VMEM_BYTES),
        name="l1_in",
    )(tbl, x, g1, scb, shb, w_bf16, cos_t, sin_t)


def _ret_kernel(tbl_ref, q_ref, k_ref, v_ref, g_ref, lg_ref, nw_ref, nb_ref, s0_ref, a_ref, fs_ref, st_ref, *,
                n_zero):
    f32, bf16 = jnp.float32, jnp.bfloat16
    sb = pl.program_id(0)
    nch = q_ref.shape[0] // BLK
    heads = range(RET_HPS)
    ks = [slice(h * RET_DK, (h + 1) * RET_DK) for h in heads]
    vs = [slice(h * RET_DV, (h + 1) * RET_DV) for h in heads]
    lgf = [lg_ref[h, 0:1, 0:1] for h in heads]
    lgb = [lg_ref[h, 1:2, 0:1] for h in heads]
    t_i = lax.broadcasted_iota(jnp.int32, (BLK, BLK), 0)
    s_i = lax.broadcasted_iota(jnp.int32, (BLK, BLK), 1)
    dist = (t_i - s_i).astype(f32)
    dm = [jnp.exp(jnp.where(s_i <= t_i, dist * lgf[h], NEG_BIG)) + jnp.exp(jnp.where(s_i >= t_i, -dist * lgb[h], NEG_BIG))
          for h in heads]
    tk = lax.broadcasted_iota(jnp.int32, (BLK, RET_DK), 0).astype(f32)
    tv = lax.broadcasted_iota(jnp.int32, (BLK, RET_DV), 0).astype(f32)
    k_to_end_f = [jnp.exp((BLK - 1.0 - tk) * lgf[h]) for h in heads]
    k_to_end_b = [jnp.exp(tk * lgb[h]) for h in heads]
    from_start_f = [jnp.exp((tv + 1.0) * lgf[h]) for h in heads]
    from_start_b = [jnp.exp((BLK - tv) * lgb[h]) for h in heads]
    nt, tn = (((1,), (1,)), ((), ())), (((0,), (0,)), ((), ()))

    def chunk_rows(c):
        return pl.ds(pl.multiple_of(c * BLK, BLK), BLK)

    def fwd(c, carry):
        blk = sb * nch + c
        rows = chunk_rows(c)

        @pl.when(tbl_ref[0, blk] == 1)
        def _():
            st_ref[0] = jnp.where(sb >= n_zero, s0_ref[0, 0], 0.0)

        q = [q_ref[rows, ks[h]].astype(bf16) for h in heads]
        k = [k_ref[rows, ks[h]] for h in heads]
        v = [v_ref[rows, vs[h]].astype(bf16) for h in heads]
        s_in = [st_ref[0, h] for h in heads]
        g = [lax.dot_general(q[h], k[h].astype(bf16), nt, preferred_element_type=f32) for h in heads]
        y_diag = [jnp.dot((g[h] * dm[h]).astype(bf16), v[h], preferred_element_type=f32) for h in heads]
        y_off = [jnp.dot(q[h], s_in[h].astype(bf16), preferred_element_type=f32) for h in heads]
        kw = [(k[h].astype(f32) * k_to_end_f[h]).astype(bf16) for h in heads]
        upd = [lax.dot_general(kw[h], v[h], tn, preferred_element_type=f32) for h in heads]
        for h in heads:
            a_ref[rows, vs[h]] = y_diag[h] + from_start_f[h] * y_off[h]
            st_ref[0, h] = jnp.exp(BLK * lgf[h]) * s_in[h] + upd[h]
        fs_ref[c, 0] = st_ref[0]
        return carry

    lax.fori_loop(0, nch, fwd, 0)

    def bwd(j, carry):
        c = nch - 1 - j
        blk = sb * nch + c
        rows = chunk_rows(c)

        @pl.when(tbl_ref[1, blk] == 1)
        def _():
            st_ref[1] = jnp.where(sb >= n_zero, s0_ref[0, 1], 0.0)

        q = [q_ref[rows, ks[h]].astype(bf16) for h in heads]
        v = [v_ref[rows, vs[h]].astype(bf16) for h in heads]
        s_in = [st_ref[1, h] for h in heads]
        y_off = [jnp.dot(q[h], s_in[h].astype(bf16), preferred_element_type=f32) for h in heads]
        kw = [(k_ref[rows, ks[h]].astype(f32) * k_to_end_b[h]).astype(bf16) for h in heads]
        upd = [lax.dot_general(kw[h], v[h], tn, preferred_element_type=f32) for h in heads]
        for h in heads:
            st_ref[1, h] = jnp.exp(BLK * lgb[h]) * s_in[h] + upd[h]
            y = a_ref[rows, vs[h]] + from_start_b[h] * y_off[h]
            mu = jnp.mean(y, -1, keepdims=True)
            yc = y - mu
            var = jnp.mean(yc * yc, -1, keepdims=True)
            gg = g_ref[rows, vs[h]].astype(f32)
            a_ref[rows, vs[h]] = ((yc * lax.rsqrt(var + 1e-5) * nw_ref[:, vs[h]] + nb_ref[:, vs[h]])
                                  * (gg * jax.nn.sigmoid(gg)))
        fs_ref[c, 1] = st_ref[1]
        return carry

    lax.fori_loop(0, nch, bwd, 0)


def ret_scan(q, k, v, g, lg_tab, norm_w, norm_b, s0, n_zero, first, last, sb_rows):
    R = q.shape[0]
    n_sb = R // sb_rows
    nch = sb_rows // BLK
    tbl = jnp.asarray([first, last], jnp.int32)
    f32 = jnp.float32
    hps = RET_HPS
    return pl.pallas_call(
        functools.partial(_ret_kernel, n_zero=n_zero),
        grid_spec=pltpu.PrefetchScalarGridSpec(
            num_scalar_prefetch=1, grid=(n_sb, RET_HEADS // hps),
            in_specs=[pl.BlockSpec((sb_rows, hps * RET_DK), lambda s, h, t: (s, h)),
                      pl.BlockSpec((sb_rows, hps * RET_DK), lambda s, h, t: (s, h)),
                      pl.BlockSpec((sb_rows, hps * RET_DV), lambda s, h, t: (s, h)),
                      pl.BlockSpec((sb_rows, hps * RET_DV), lambda s, h, t: (s, h)),
                      pl.BlockSpec((hps, 8, 128), lambda s, h, t: (h, 0, 0)),
                      pl.BlockSpec((1, hps * RET_DV), lambda s, h, t: (0, h)),
                      pl.BlockSpec((1, hps * RET_DV), lambda s, h, t: (0, h)),
                      pl.BlockSpec((1, 2, hps, RET_DK, RET_DV), lambda s, h, t: (jnp.maximum(s - n_zero, 0), 0, h, 0, 0))],
            out_specs=[pl.BlockSpec((sb_rows, hps * RET_DV), lambda s, h, t: (s, h)),
                       pl.BlockSpec((nch, 2, hps, RET_DK, RET_DV), lambda s, h, t: (s, 0, h, 0, 0))],
            scratch_shapes=[pltpu.VMEM((2, hps, RET_DK, RET_DV), f32)]),
        out_shape=[jax.ShapeDtypeStruct((R, RET_V), f32),
                   jax.ShapeDtypeStruct((R // BLK, 2, RET_HEADS, RET_DK, RET_DV), f32)],
        compiler_params=pltpu.CompilerParams(dimension_semantics=("arbitrary", "arbitrary"),
                                             vmem_limit_bytes=SSD_VMEM_BYTES),
        name="ret_scan",
    )(tbl, q, k, v, g, lg_tab, norm_w.reshape(1, -1), norm_b.reshape(1, -1), s0)


def _mod_kernel(c_ref, w_ref, b_ref, o_ref):
    c = c_ref[...]
    act = c * jax.nn.sigmoid(c)
    o_ref[0] = _dot(act, w_ref[0], ((1,), (0,)), 3) + b_ref[0]


def mod_vectors(conds, mod_w, mod_b):
    depth, D, n6 = mod_w.shape
    tn = D
    return pl.pallas_call(
        _mod_kernel,
        grid=(depth, n6 // tn),
        in_specs=[pl.BlockSpec(conds.shape, lambda i, j: (0, 0)),
                  pl.BlockSpec((1, D, tn), lambda i, j: (i, 0, j)),
                  pl.BlockSpec((1, 1, tn), lambda i, j: (i, 0, j))],
        out_specs=pl.BlockSpec((1, conds.shape[0], tn), lambda i, j: (i, 0, j)),
        out_shape=jax.ShapeDtypeStruct((depth, conds.shape[0], n6), jnp.float32),
        name="mod_vectors",
    )(conds, mod_w, mod_b.reshape(depth, 1, n6))


def rope_tables(n_tokens):
    rows = n_tokens // GRID_W
    row = np.repeat(np.arange(rows), GRID_W).astype(np.float64)
    col = np.tile(np.arange(GRID_W), rows).astype(np.float64)
    n_f = RET_DK // 4
    inv = ROPE_BASE ** (-np.arange(n_f, dtype=np.float64) / n_f)
    ang = np.concatenate([row[:, None] * inv, col[:, None] * inv], -1)
    return np.cos(ang), np.sin(ang)


def _rope_block_tables(n_ctx, n_lat, l_lat):
    nb = l_lat // BLK
    cos, sin = rope_tables(l_lat)
    cosf = np.concatenate([cos, cos], -1).reshape(nb, BLK, RET_DK)
    sinf = np.concatenate([-sin, sin], -1).reshape(nb, BLK, RET_DK)
    cos_t = jnp.asarray(np.concatenate([np.ones((1, BLK, RET_DK)), cosf]), jnp.float32)
    sin_t = jnp.asarray(np.concatenate([np.zeros((1, BLK, RET_DK)), sinf]), jnp.float32)
    rope_blk = [0] * n_ctx + [1 + j for _ in range(n_lat) for j in range(nb)]
    return cos_t, sin_t, rope_blk


def kernel(x_prompt, x_sample, state_ssd, state_rwkv, state_ret, c, c_ctx, mod_w, mod_b, norm1_g, norm2_g,
           router_w, exp_w_gate, exp_w_up, exp_w_down, ab_w_in, ab_w_out, ssd_conv_w, ssd_conv_b, ssd_dt_bias,
           ssd_a_log, ssd_d, ssd_norm_g, rwkv_mu_prev, rwkv_mu_next, rwkv_w0, rwkv_w2, rwkv_a0, rwkv_a2, rwkv_g2,
           rwkv_k_k, rwkv_k_a, rwkv_r_k, rwkv_ln_w, rwkv_ln_b, ret_w_in, ret_w_out, ret_decay_logit, ret_norm_w,
           ret_norm_b, final_norm_g):
    p = dict(mod_w=mod_w, mod_b=mod_b, norm1_g=norm1_g, norm2_g=norm2_g, router_w=router_w,
             exp_w_gate=exp_w_gate, exp_w_up=exp_w_up, exp_w_down=exp_w_down, ab_w_in=ab_w_in, ab_w_out=ab_w_out,
             ssd_conv_w=ssd_conv_w, ssd_conv_b=ssd_conv_b, ssd_dt_bias=ssd_dt_bias, ssd_a_log=ssd_a_log,
             ssd_d=ssd_d, ssd_norm_g=ssd_norm_g, rwkv_mu_prev=rwkv_mu_prev, rwkv_mu_next=rwkv_mu_next,
             rwkv_w0=rwkv_w0, rwkv_w2=rwkv_w2, rwkv_a0=rwkv_a0, rwkv_a2=rwkv_a2, rwkv_g2=rwkv_g2,
             rwkv_k_k=rwkv_k_k, rwkv_k_a=rwkv_k_a, rwkv_r_k=rwkv_r_k, rwkv_ln_w=rwkv_ln_w, rwkv_ln_b=rwkv_ln_b,
             ret_w_in=ret_w_in, ret_w_out=ret_w_out, ret_decay_logit=ret_decay_logit, ret_norm_w=ret_norm_w,
             ret_norm_b=ret_norm_b, final_norm_g=final_norm_g)
    f32, bf16 = jnp.float32, jnp.bfloat16
    n_ctx, l_ctx, D = x_prompt.shape
    n_lat, l_lat, _ = x_sample.shape
    assert l_ctx == BLK and l_lat % BLK == 0 and (n_ctx * BLK) % l_lat == 0
    n_sb_ctx = n_ctx * BLK // l_lat
    cond_id, first, last = _seq_tables(n_ctx, n_lat, l_lat)
    x = (x_prompt.reshape(-1, D), x_sample.reshape(-1, D))

    conds = jnp.concatenate([c_ctx[None, :], c, jnp.zeros((8 - 1 - n_lat, D), f32)])
    mods = mod_vectors(conds, mod_w, mod_b)[:, jnp.asarray(cond_id)]
    mods = mods.reshape(DEPTH, len(cond_id), 6, 1, D)

    new_ssd, new_rwkv, new_ret = [], [], []
    out = None
    for i in range(DEPTH):
        sh1, sc1, g1, sh2, sc2, g2 = (mods[i, :, k] for k in range(6))
        e = i // 2
        if i % 2 == 0:
            w_packed, mup, mun, rwp, w2bd, a2p, g2p, e_ind, et_ind = l0_pack_weights(p, e)
            z, xbc, dt, r, v, an, lw, kd, bv, gate, bonus = l0_in(
                x, norm1_g[i][None], sc1, sh1, w_packed, mup, mun, rwp, w2bd, a2p, g2p, e_ind, et_ind, first, last)
            sel, hp = ssd_tables(p, e)
            s0_ssd = jnp.transpose(state_ssd[:, e], (0, 1, 3, 2, 4)).reshape(n_lat, 2, SSD_N, SSD_INNER)
            ys, fs_ssd = ssd_scan(xbc, dt, ssd_conv_w[e], ssd_conv_b[e], sel, hp,
                                  s0_ssd, n_sb_ctx, first, last, l_lat)
            new_ssd.append(jnp.transpose(fs_ssd[:n_ctx].reshape(n_ctx, 2, SSD_N, SSD_HEADS, SSD_P), (0, 1, 3, 2, 4)))
            s0_rwkv = jnp.transpose(state_rwkv[:, e], (0, 1, 3, 2, 4)).reshape(n_lat, 2, RWKV_N, RWKV_DIM)
            yf, yb, sf_rwkv = rwkv_scan_pallas(r, v, an, lw, kd, bv, s0_rwkv, n_ctx,
                                               _rwkv_steps(n_ctx, n_lat, l_lat))
            new_rwkv.append(jnp.transpose(sf_rwkv[:n_ctx].reshape(n_ctx, 2, RWKV_N, RWKV_HEADS, RWKV_N),
                                          (0, 1, 3, 2, 4)))
            x, hn2, affT = l0_out(ys, z, yf, yb, bonus, gate, ssd_norm_g[e][None], rwkv_ln_w[e][None], rwkv_ln_b[e][None],
                                  e_ind, et_ind, ab_w_out[e].astype(bf16), x, g1, norm2_g[i][None], sc2, sh2,
                                  router_w[i])
        else:
            cos_t, sin_t, rope_blk = _rope_block_tables(n_ctx, n_lat, l_lat)
            q, k, v, gg = l1_in(x, norm1_g[i][None], sc1, sh1, ret_w_in[e].astype(bf16), cos_t, sin_t, rope_blk)
            lg = jax.nn.log_sigmoid(ret_decay_logit[e].astype(f32))
            lg_tab = jnp.zeros((RET_HEADS, 8, 128), f32).at[:, :2, :].set(jnp.transpose(lg)[:, :, None])
            a, fs_ret = ret_scan(q, k, v, gg, lg_tab, ret_norm_w[e], ret_norm_b[e],
                                 state_ret[:, e], n_sb_ctx, first, last, l_lat)
            new_ret.append(fs_ret[:n_ctx])
            x, hn2, affT = l1_out(a, ret_w_out[e].astype(bf16), x, g1, norm2_g[i][None], sc2, sh2, router_w[i])
        fin = final_norm_g if i == DEPTH - 1 else None
        out = moe_layer(x, hn2, affT, g2, exp_w_gate, exp_w_up, exp_w_down, i, n_ctx, l_lat, final_g=fin)
        if fin is None:
            x = out
    y_ctx, y_lat = out
    return (y_ctx.reshape(n_ctx, l_ctx, D), y_lat.reshape(n_lat, l_lat, D),
            jnp.stack(new_ssd, 1), jnp.stack(new_rwkv, 1), jnp.stack(new_ret, 1))
```

```python
import functools
import math

import jax
import jax.numpy as jnp
import numpy as np
from jax import lax
from jax.experimental import pallas as pl
from jax.experimental.pallas import tpu as pltpu

DEPTH = 2
GRID_W = 64
NORM_EPS = 1e-6
SSD_HEADS = 16
SSD_P = 64
SSD_INNER = SSD_HEADS * SSD_P
SSD_GROUPS = 2
SSD_HPG = SSD_HEADS // SSD_GROUPS
SSD_N = 128
SSD_XBC = SSD_INNER + 2 * SSD_GROUPS * SSD_N
SSD_IN = SSD_INNER + SSD_XBC + SSD_HEADS
RWKV_HEADS = 16
RWKV_N = 64
RWKV_DIM = RWKV_HEADS * RWKV_N
W_LORA = 64
A_LORA = 64
G_LORA = 128
RWKV_GN_EPS = 64e-5
RET_HEADS = 8
RET_DK = 128
RET_DV = 256
RET_QK = RET_HEADS * RET_DK
RET_V = RET_HEADS * RET_DV
ROPE_BASE = 10000.0
N_EXPERTS = 16
EC_CAPACITY = 2

ACT_DTYPE = jnp.bfloat16

RWKV_C = 64
RWKV_GH = 4
RWKV_GL = RWKV_GH * RWKV_N
RWKV_DOUBLING_PASSES = (3, 3, 3, 3, 1, 1)


def _split_bf16(x):
    hi = x.astype(jnp.bfloat16)
    lo = (x - hi.astype(jnp.float32)).astype(jnp.bfloat16)
    return hi, lo


def _dot(a, b, dims, passes):
    f = functools.partial(lax.dot_general, dimension_numbers=(dims, ((), ())),
                          preferred_element_type=jnp.float32)
    if passes == 1:
        return f(a.astype(jnp.bfloat16), b.astype(jnp.bfloat16))
    ah, al = _split_bf16(a)
    bh, bl = _split_bf16(b)
    return f(ah, bh) + (f(ah, bl) + f(al, bh))


def _rwkv_chunk_kernel(tbl_ref, r0_ref, v0_ref, a0_ref, r1_ref, v1_ref, a1_ref, lw0_ref, k0_ref, b0_ref,
                       lw1_ref, k1_ref, b1_ref, s0_ref, y0_ref, y1_ref, sf_ref, h_ref, *, p_inv, p_oth, n_zero):
    C, N, GH, GL = RWKV_C, RWKV_N, RWKV_GH, RWKV_GL
    i = pl.program_id(0)
    f32, bf16 = jnp.float32, jnp.bfloat16

    @pl.when(tbl_ref[3, i] == 1)
    def _():
        h_ref[...] = jnp.where(tbl_ref[2, i] >= n_zero, s0_ref[0], 0.0)

    t_i = lax.broadcasted_iota(jnp.int32, (C, GL), 0)
    s_i = lax.broadcasted_iota(jnp.int32, (C, GL), 1) & (N - 1)
    eye = (s_i == t_i).astype(f32)
    row2 = lax.broadcasted_iota(jnp.int32, (2 * C, GL), 0)
    rel2 = (lax.broadcasted_iota(jnp.int32, (2 * C, GL), 1) & (N - 1)) - (row2 & (C - 1))
    incl2 = row2 // C
    mask2 = [rel2 - incl2 < 0, -rel2 - incl2 < 0]
    bh_r = lax.broadcasted_iota(jnp.int32, (GL, GL), 0) // N
    bh_c = lax.broadcasted_iota(jnp.int32, (GL, GL), 1) // N
    blk = bh_r == bh_c
    tt = lax.broadcasted_iota(jnp.int32, (C, C), 0)
    ss = lax.broadcasted_iota(jnp.int32, (C, C), 1)
    tri = [(ss <= tt).astype(bf16), (ss >= tt).astype(bf16)]

    def bd(x, passes):
        pieces = []
        for _ in range(2 if passes == 3 else 1):
            hi = x.astype(bf16)
            x = x - hi.astype(f32)
            pieces.append(jnp.where(blk, jnp.concatenate([hi] * GH, axis=0), jnp.zeros((), bf16)))
        return pieces

    def mm(l, x, passes, dims=((1,), (0,))):
        f = functools.partial(lax.dot_general, dimension_numbers=(dims, ((), ())), preferred_element_type=f32)
        xs = bd(x, passes)
        lh = l.astype(bf16)
        if passes == 1:
            return f(lh, xs[0])
        ll = (l - lh.astype(f32)).astype(bf16)
        m = l.shape[0]
        both = f(jnp.concatenate([lh, ll], axis=0), xs[0])
        if passes == 2:
            return both[:m] + both[m:]
        return both[:m] + (f(lh, xs[1]) + both[m:])

    nt = ((1,), (1,))
    refs = [(r0_ref, v0_ref, a0_ref, lw0_ref, k0_ref, b0_ref), (r1_ref, v1_ref, a1_ref, lw1_ref, k1_ref, b1_ref)]
    lw, r_t, a_t, b_t, k_t, v = [], [], [], [], [], []
    for d, (r_ref, v_ref, a_ref, lw_ref, k_ref, b_ref) in enumerate(refs):
        lwd = lw_ref[0]
        lw_hi, lw_lo = _split_bf16(lwd)
        cum = (jnp.dot(tri[d], lw_hi, preferred_element_type=f32) + jnp.dot(tri[d], lw_lo, preferred_element_type=f32))
        w_inv = jnp.exp(-cum)
        lw.append(lwd)
        r_t.append(r_ref[...].astype(f32) * jnp.exp(cum))
        a_t.append(a_ref[...].astype(f32) * jnp.exp(cum - lwd))
        b_t.append(b_ref[0].astype(f32) * w_inv)
        k_t.append(k_ref[0].astype(f32) * w_inv)
        v.append(v_ref[...].astype(f32))
    lane_head = lax.broadcasted_iota(jnp.int32, (N, GL), 1) // N

    chains = [(d, slice(g * GL, (g + 1) * GL)) for d in range(2) for g in range(RWKV_HEADS // GH)]
    each = lambda fn: [fn(j, d, sl) for j, (d, sl) in enumerate(chains)]
    bg = each(lambda j, d, sl: b_t[d][:, sl])
    kg = each(lambda j, d, sl: k_t[d][:, sl])
    vg = each(lambda j, d, sl: v[d][:, sl])
    h0 = each(lambda j, d, sl: h_ref[d, :, sl])
    ar = each(lambda j, d, sl: jnp.concatenate([a_t[d][:, sl], r_t[d][:, sl]], axis=0))
    m_b = each(lambda j, d, sl: jnp.where(mask2[d], mm(ar[j], bg[j], p_oth, nt), 0.0))
    m_k = each(lambda j, d, sl: jnp.where(mask2[d], mm(ar[j], kg[j], p_oth, nt), 0.0))
    p = each(lambda j, d, sl: mm(m_b[j][:C], m_b[j][:C], p_inv[0]))
    tmat = each(lambda j, d, sl: eye + m_b[j][:C])
    for lv in range(int(math.log2(C)) - 2):
        pt = each(lambda j, d, sl: mm(jnp.concatenate([p[j], tmat[j]], axis=0), p[j], p_inv[1 + lv]))
        p = each(lambda j, d, sl: pt[j][:C])
        tmat = each(lambda j, d, sl: tmat[j] + pt[j][C:])
    tmat = each(lambda j, d, sl: tmat[j] + mm(tmat[j], p[j], p_inv[-1]))
    ar_h = each(lambda j, d, sl: mm(ar[j], h0[j], p_oth, nt))
    mk_v = each(lambda j, d, sl: mm(m_k[j], vg[j], p_oth))
    u = each(lambda j, d, sl: mm(tmat[j], ar_h[j][:C] + mk_v[j][:C], p_oth))
    y = each(lambda j, d, sl: ar_h[j][C:] + mm(m_b[j][C:], u[j], p_oth) + mk_v[j][C:])
    full = each(lambda j, d, sl: _dot(jnp.concatenate([u[j], vg[j]], axis=0), jnp.concatenate([bg[j], kg[j]], axis=0),
                                      ((0,), (0,)), p_oth))
    y_refs = (y0_ref, y1_ref)
    for j, (d, sl) in enumerate(chains):
        y_refs[d][:, sl] = y[j].astype(y_refs[d].dtype)
        z = jnp.zeros((N, GL), f32)
        for hh in range(GH):
            z = z + jnp.where(lane_head == hh, full[j][hh * N:(hh + 1) * N], 0.0)
        w_tot = jnp.exp(jnp.sum(lw[d][:, sl], axis=0, keepdims=True))
        h_ref[d, :, sl] = w_tot * (h0[j] + z)

    @pl.when(tbl_ref[4, i] == 1)
    def _():
        sf_ref[0] = h_ref[...]


def _rwkv_steps(n_ctx, n_lat, l_lat):
    C = RWKV_C
    rows = []
    seqs = [(s, s * BLK, BLK) for s in range(n_ctx)] + [(n_ctx + s, n_ctx * BLK + s * l_lat, l_lat) for s in range(n_lat)]
    for sid, row0, length in seqs:
        nc = length // C
        for j in range(nc):
            rows.append((row0 // C + j, row0 // C + nc - 1 - j, sid, int(j == 0), int(j == nc - 1)))
    return [list(col) for col in zip(*rows)]


def rwkv_scan_pallas(r, v, a, lw, k, b, s0, n_zero, steps, p_inv=RWKV_DOUBLING_PASSES, p_oth=1):
    R, HN = r.shape
    C, N = RWKV_C, RWKV_N
    tbl = jnp.asarray(steps, jnp.int32)
    fwd = pl.BlockSpec((C, HN), lambda i, t: (t[0, i], 0))
    bwd = pl.BlockSpec((C, HN), lambda i, t: (t[1, i], 0))
    fwd_d = pl.BlockSpec((1, C, HN), lambda i, t: (0, t[0, i], 0))
    bwd_d = pl.BlockSpec((1, C, HN), lambda i, t: (1, t[1, i], 0))
    n_seq = max(steps[2]) + 1
    st_in = pl.BlockSpec((1, 2, N, HN), lambda i, t: (jnp.maximum(t[2, i] - n_zero, 0), 0, 0, 0))
    st = pl.BlockSpec((1, 2, N, HN), lambda i, t: (t[2, i], 0, 0, 0))
    return pl.pallas_call(
        functools.partial(_rwkv_chunk_kernel, p_inv=p_inv, p_oth=p_oth, n_zero=n_zero),
        grid_spec=pltpu.PrefetchScalarGridSpec(
            num_scalar_prefetch=1,
            grid=(len(steps[0]),),
            in_specs=[fwd, fwd, fwd, bwd, bwd, bwd, fwd_d, fwd_d, fwd_d, bwd_d, bwd_d, bwd_d, st_in],
            out_specs=[fwd, bwd, st],
            scratch_shapes=[pltpu.VMEM((2, N, HN), jnp.float32)]),
        out_shape=[jax.ShapeDtypeStruct((R, HN), ACT_DTYPE), jax.ShapeDtypeStruct((R, HN), ACT_DTYPE),
                   jax.ShapeDtypeStruct((n_seq, 2, N, HN), jnp.float32)],
        compiler_params=pltpu.CompilerParams(dimension_semantics=("arbitrary",)),
        name="rwkv_scan",
    )(tbl, r, v, a, r, v, a, lw, k, b, lw, k, b, s0)


BLK = 256
FF_TILE = 768
SELECT_TILE = 512
SCATTER_WINDOW = 96
SLOT_ALIGN = 16
SELECT_MIN_EXP = -1100.0
SELECT_BINADE_STEPS = 11
SELECT_MANTISSA_STEPS = 40
MOE_FFN_VMEM_BYTES = 48 * 1024 * 1024


def _moe_select_kernel(aff_ref, slot_ref, *, cap):
    a = aff_ref[...]
    E, T = a.shape
    f32 = jnp.float32

    def enough(piv):
        return jnp.sum(jnp.where(a >= piv, 1.0, 0.0), axis=1, keepdims=True) >= cap

    def binade(_, lohi):
        e_lo, e_hi = lohi
        mid = jnp.floor((e_lo + e_hi) * 0.5)
        ok = enough(jnp.exp2(mid))
        return jnp.where(ok, mid, e_lo), jnp.where(ok, e_hi, mid)

    e_lo, e_hi = lax.fori_loop(0, SELECT_BINADE_STEPS, binade,
                               (jnp.full((E, 1), SELECT_MIN_EXP, f32), jnp.full((E, 1), 1.0, f32)))

    def inside(_, lohi):
        lo, hi = lohi
        mid = lo + (hi - lo) * 0.5
        ok = enough(mid)
        return jnp.where(ok, mid, lo), jnp.where(ok, hi, mid)

    thr, _ = lax.fori_loop(0, SELECT_MANTISSA_STEPS, inside, (jnp.exp2(e_lo), jnp.exp2(e_hi)))
    gt = a > thr
    eq = a == thr
    need = cap - jnp.sum(jnp.where(gt, 1.0, 0.0), axis=1, keepdims=True)
    tw = min(T, SELECT_TILE)

    def prefix_count(mask):
        m = jnp.where(mask, 1.0, 0.0).astype(jnp.bfloat16)
        outs = []
        for j in range(T // tw):
            s_i = lax.broadcasted_iota(jnp.int32, (T, tw), 0)
            t_i = lax.broadcasted_iota(jnp.int32, (T, tw), 1) + j * tw
            before = jnp.where(s_i < t_i, 1.0, 0.0).astype(jnp.bfloat16)
            outs.append(jnp.dot(m, before, preferred_element_type=f32))
        return outs[0] if len(outs) == 1 else jnp.concatenate(outs, axis=1)

    sel = gt | (eq & (prefix_count(eq) < need))
    slot_ref[...] = jnp.where(sel, prefix_count(sel).astype(jnp.int32), -1)


def _moe_select(affT, row0, n_seq, t):
    E = affT.shape[0]
    rows = jnp.transpose(affT[:, row0:row0 + n_seq * t].reshape(E, n_seq, t), (1, 0, 2)).reshape(n_seq * E, t)
    slot = pl.pallas_call(
        functools.partial(_moe_select_kernel, cap=EC_CAPACITY * t // N_EXPERTS),
        grid=(1,),
        in_specs=[pl.BlockSpec((n_seq * E, t), lambda s: (0, 0))],
        out_specs=pl.BlockSpec((n_seq * E, t), lambda s: (0, 0)),
        out_shape=jax.ShapeDtypeStruct((n_seq * E, t), jnp.int32),
        name="moe_select",
    )(rows)
    return jnp.transpose(slot.reshape(n_seq, E, t), (1, 0, 2)).reshape(E, n_seq * t)


def _moe_gather_kernel(tbl_ref, slot_ref, aff_ref, hn_ref, xe_ref, gate_ref, *, cap, win, nb):
    E = slot_ref.shape[0]
    s = pl.program_id(0)
    w_i = lax.broadcasted_iota(jnp.int32, (win, BLK), 0)
    xe_ref[...] = jnp.zeros_like(xe_ref)
    gate_ref[...] = jnp.zeros_like(gate_ref)
    for j in range(nb):
        blk = s * nb + j
        toks = slice(j * BLK, (j + 1) * BLK)
        hn_blk = hn_ref[toks, :]

        def window(w, carry):
            hits, starts = [], []
            for e in range(E):
                lo = tbl_ref[e, blk] + w * win
                start = pl.multiple_of(jnp.minimum(lo, cap - win), SLOT_ALIGN)
                slot = slot_ref[e, :, toks]
                slot = jnp.where(slot >= lo, slot, -1)
                hits.append(slot == w_i + start)
                starts.append(start)
            onehot = jnp.concatenate([jnp.where(h, 1.0, 0.0) for h in hits], axis=0).astype(jnp.bfloat16)
            rows = jnp.dot(onehot, hn_blk, preferred_element_type=jnp.float32)
            for e in range(E):
                dst = pl.ds(starts[e], win)
                xe_ref[e, dst, :] += rows[e * win:(e + 1) * win].astype(xe_ref.dtype)
                g = jnp.sum(jnp.where(hits[e], aff_ref[e, :, toks], 0.0), axis=1, keepdims=True)
                gate_ref[e, dst, :] += jnp.broadcast_to(g, (win, 128))
            return carry

        lax.fori_loop(0, tbl_ref[E, blk], window, 0)


def _moe_gather(slot, slot3, aff3, hn, row0, n_seq, t):
    E = slot3.shape[0]
    D = hn.shape[1]
    cap = EC_CAPACITY * t // N_EXPERTS
    win = min(cap, SCATTER_WINDOW)
    nb = t // BLK
    b0 = row0 // t
    tbl = _scatter_windows(slot, row0, n_seq, t, win)
    return pl.pallas_call(
        functools.partial(_moe_gather_kernel, cap=cap, win=win, nb=nb),
        grid_spec=pltpu.PrefetchScalarGridSpec(
            num_scalar_prefetch=1, grid=(n_seq,),
            in_specs=[pl.BlockSpec((E, 1, t), lambda s, tb_: (0, 0, b0 + s)),
                      pl.BlockSpec((E, 1, t), lambda s, tb_: (0, 0, b0 + s)),
                      pl.BlockSpec((t, D), lambda s, tb_: (b0 + s, 0))],
            out_specs=[pl.BlockSpec((E, cap, D), lambda s, tb_: (0, s, 0)),
                       pl.BlockSpec((E, cap, 128), lambda s, tb_: (0, s, 0))]),
        out_shape=[jax.ShapeDtypeStruct((E, n_seq * cap, D), jnp.bfloat16),
                   jax.ShapeDtypeStruct((E, n_seq * cap, 128), jnp.float32)],
        compiler_params=pltpu.CompilerParams(dimension_semantics=("arbitrary",),
                                             vmem_limit_bytes=MOE_FFN_VMEM_BYTES),
        name="moe_gather",
    )(tbl, slot3, aff3, hn)


def _moe_ffn_kernel(xc_ref, xl_ref, gc_ref, gl_ref, wg_ref, wu_ref, wd_ref, yc_ref, yl_ref, accc_ref, accl_ref, *,
                    nf):
    f = pl.program_id(1)
    bf16 = jnp.bfloat16
    wg = wg_ref[0, 0].astype(bf16)
    wu = wu_ref[0, 0].astype(bf16)
    wd = wd_ref[0, 0].astype(bf16)

    def part(x_ref, acc_ref, g_ref, y_ref):
        x = x_ref[0]
        g = jnp.dot(x, wg, preferred_element_type=jnp.float32)
        u = jnp.dot(x, wu, preferred_element_type=jnp.float32)
        h = (g * jax.nn.sigmoid(g) * u).astype(bf16)
        y = jnp.dot(h, wd, preferred_element_type=jnp.float32)

        if nf > 1:
            @pl.when(f == 0)
            def _():
                acc_ref[...] = y

            @pl.when((f != 0) & (f != nf - 1))
            def _():
                acc_ref[...] += y

        @pl.when(f == nf - 1)
        def _():
            total = y if nf == 1 else acc_ref[...] + y
            gate = jnp.concatenate([g_ref[0]] * (acc_ref.shape[1] // 128), axis=1)
            y_ref[0] = (total * gate).astype(y_ref.dtype)

    part(xc_ref, accc_ref, gc_ref, yc_ref)
    part(xl_ref, accl_ref, gl_ref, yl_ref)


def _moe_ffn(xc, xl, gc, gl, wg, wu, wd, layer):
    E, nc_rows, D = xc.shape
    nl_rows = xl.shape[1]
    F = wg.shape[3]
    nf = F // FF_TILE
    return pl.pallas_call(
        functools.partial(_moe_ffn_kernel, nf=nf),
        grid=(E, nf),
        in_specs=[pl.BlockSpec((1, nc_rows, D), lambda e, f: (e, 0, 0)),
                  pl.BlockSpec((1, nl_rows, D), lambda e, f: (e, 0, 0)),
                  pl.BlockSpec((1, nc_rows, 128), lambda e, f: (e, 0, 0)),
                  pl.BlockSpec((1, nl_rows, 128), lambda e, f: (e, 0, 0)),
                  pl.BlockSpec((1, 1, D, FF_TILE), lambda e, f: (layer, e, 0, f)),
                  pl.BlockSpec((1, 1, D, FF_TILE), lambda e, f: (layer, e, 0, f)),
                  pl.BlockSpec((1, 1, FF_TILE, D), lambda e, f: (layer, e, f, 0))],
        out_specs=[pl.BlockSpec((1, nc_rows, D), lambda e, f: (e, 0, 0)),
                   pl.BlockSpec((1, nl_rows, D), lambda e, f: (e, 0, 0))],
        out_shape=[jax.ShapeDtypeStruct((E, nc_rows, D), jnp.bfloat16),
                   jax.ShapeDtypeStruct((E, nl_rows, D), jnp.bfloat16)],
        scratch_shapes=[pltpu.VMEM((nc_rows, D), jnp.float32), pltpu.VMEM((nl_rows, D), jnp.float32)],
        compiler_params=pltpu.CompilerParams(dimension_semantics=("arbitrary", "arbitrary"),
                                             vmem_limit_bytes=MOE_FFN_VMEM_BYTES),
        name="moe_ffn",
    )(xc, xl, gc, gl, wg, wu, wd)


def _moe_scatter_kernel(tbl_ref, slot_ref, ye_ref, x_ref, g2_ref, *rest, cap, win, nb, final):
    if final:
        fg_ref, o_ref, acc_ref = rest
    else:
        o_ref, acc_ref = rest
    E = ye_ref.shape[0]
    tb = x_ref.shape[0]
    blk = pl.program_id(0) * nb + pl.program_id(1)
    w_i = lax.broadcasted_iota(jnp.int32, (win, tb), 0)
    acc_ref[...] = jnp.zeros_like(acc_ref)

    def window(w, carry):
        hot, rows = [], []
        for e in range(E):
            lo = tbl_ref[e, blk] + w * win
            start = pl.multiple_of(jnp.minimum(lo, cap - win), SLOT_ALIGN)
            slot = slot_ref[e]
            slot = jnp.where(slot >= lo, slot, -1)
            hot.append(jnp.where(slot == w_i + start, 1.0, 0.0))
            rows.append(ye_ref[e, pl.ds(start, win), :])
        onehot = jnp.concatenate(hot, axis=0).astype(jnp.bfloat16)
        acc_ref[...] += lax.dot_general(onehot, jnp.concatenate(rows, axis=0), (((0,), (0,)), ((), ())),
                                        preferred_element_type=jnp.float32)
        return carry

    lax.fori_loop(0, tbl_ref[E, blk], window, 0)
    x = x_ref[...] + g2_ref[0] * acc_ref[...]
    if final:
        x = x * lax.rsqrt(jnp.mean(x * x, -1, keepdims=True) + NORM_EPS) * fg_ref[...]
    o_ref[...] = x


def _scatter_windows(slot, row0, n_seq, t, win):
    E = slot.shape[0]
    cap = EC_CAPACITY * t // N_EXPERTS
    nb = t // BLK
    cnt = jnp.sum(slot[:, row0:row0 + n_seq * t].reshape(E, n_seq, nb, BLK) >= 0, axis=3)
    first = jnp.cumsum(cnt, axis=2) - cnt
    start = jnp.minimum(first // SLOT_ALIGN * SLOT_ALIGN, cap - win)
    n_win = jnp.maximum(jnp.max((first + cnt - start + win - 1) // win, axis=0), 1)
    return jnp.concatenate([start.reshape(E, n_seq * nb), n_win.reshape(1, n_seq * nb)]).astype(jnp.int32)


def _moe_scatter(slot, slot3, ye, x, g2blk, row0, n_seq, t, final_g=None):
    E, _, D = ye.shape
    cap = EC_CAPACITY * t // N_EXPERTS
    win = min(cap, SCATTER_WINDOW)
    nb = t // BLK
    b0 = row0 // BLK
    final = final_g is not None
    tbl = _scatter_windows(slot, row0, n_seq, t, win)
    in_specs = [pl.BlockSpec((E, 1, BLK), lambda s, j, tb_: (0, 0, b0 + s * nb + j)),
                pl.BlockSpec((E, cap, D), lambda s, j, tb_: (0, s, 0)),
                pl.BlockSpec((BLK, D), lambda s, j, tb_: (b0 + s * nb + j, 0)),
                pl.BlockSpec((1, 1, D), lambda s, j, tb_: (b0 + s * nb + j, 0, 0))]
    args = [tbl, slot3, ye, x, g2blk]
    if final:
        in_specs.append(pl.BlockSpec((1, D), lambda s, j, tb_: (0, 0)))
        args.append(final_g.reshape(1, D))
        out_specs = pl.BlockSpec((BLK, D), lambda s, j, tb_: (s * nb + j, 0))
        out_shape = jax.ShapeDtypeStruct((n_seq * t, D), jnp.float32)
        aliases = {}
    else:
        out_specs = pl.BlockSpec((BLK, D), lambda s, j, tb_: (b0 + s * nb + j, 0))
        out_shape = jax.ShapeDtypeStruct(x.shape, jnp.float32)
        aliases = {3: 0}
    return pl.pallas_call(
        functools.partial(_moe_scatter_kernel, cap=cap, win=win, nb=nb, final=final),
        grid_spec=pltpu.PrefetchScalarGridSpec(
            num_scalar_prefetch=1, grid=(n_seq, nb), in_specs=in_specs, out_specs=out_specs,
            scratch_shapes=[pltpu.VMEM((BLK, D), jnp.float32)]),
        out_shape=out_shape,
        input_output_aliases=aliases,
        compiler_params=pltpu.CompilerParams(dimension_semantics=("arbitrary", "arbitrary"),
                                             vmem_limit_bytes=MOE_FFN_VMEM_BYTES),
        name="moe_scatter",
    )(*args)


def moe_layer(x, hn, affT, g2blk, wg, wu, wd, layer, n_ctx, l_lat, final_g=None):
    R = x.shape[0]
    r_ctx = n_ctx * BLK
    n_lat = (R - r_ctx) // l_lat
    slot = jnp.concatenate([_moe_select(affT, 0, n_ctx, BLK), _moe_select(affT, r_ctx, n_lat, l_lat)], axis=1)
    slot3 = slot[:, None, :]
    aff3 = affT[:, None, :]
    xc, gc = _moe_gather(slot, slot3, aff3, hn, 0, n_ctx, BLK)
    xl, gl = _moe_gather(slot, slot3, aff3, hn, r_ctx, n_lat, l_lat)
    yc, yl = _moe_ffn(xc, xl, gc, gl, wg, wu, wd, layer)
    if final_g is None:
        x = _moe_scatter(slot, slot3, yc, x, g2blk, 0, n_ctx, BLK)
        return _moe_scatter(slot, slot3, yl, x, g2blk, r_ctx, n_lat, l_lat)
    return (_moe_scatter(slot, slot3, yc, x, g2blk, 0, n_ctx, BLK, final_g),
            _moe_scatter(slot, slot3, yl, x, g2blk, r_ctx, n_lat, l_lat, final_g))


L0_Z = (0, 1024)
L0_XBC = (1024, 2560)
L0_SHIFT = (2560, 6016)
L0_DT = (6016, 6144)
L0_COLS = 6144
L0_VMEM_BYTES = 56 * 1024 * 1024


def _sum_split(x, m, n_split):
    acc = None
    for _ in range(n_split):
        hi = x.astype(jnp.bfloat16)
        x = x - hi.astype(jnp.float32)
        t = jnp.dot(hi, m, preferred_element_type=jnp.float32)
        acc = t if acc is None else acc + t
    return acc


def _sum_split_left(m, x, n_split):
    acc = None
    for _ in range(n_split):
        hi = x.astype(jnp.bfloat16)
        x = x - hi.astype(jnp.float32)
        t = jnp.dot(m, hi, preferred_element_type=jnp.float32)
        acc = t if acc is None else acc + t
    return acc


def _head_sum(x, e_ref, et_ref):
    return _sum_split(_sum_split(x, e_ref[...], 2), et_ref[...], 2)


def _adaln(x, g, sc, sh):
    y = x * lax.rsqrt(jnp.mean(x * x, -1, keepdims=True) + NORM_EPS) * g
    return y * (1.0 + sc) + sh


def _softplus(x):
    return jnp.maximum(x, 0.0) + jnp.log(1.0 + jnp.exp(-jnp.abs(x)))


def _l0_in_kernel(tbl_ref, *refs, n_parts, n_a):
    x_refs, xp_refs, xn_refs = refs[:n_parts], refs[n_parts:2 * n_parts], refs[2 * n_parts:3 * n_parts]
    (g_ref, sc_ref, sh_ref, w_ref, mup_ref, mun_ref, rwp_ref, w2_ref, a2_ref, g2_ref, e_ref, et_ref,
     z_ref, xbc_ref, dt_ref, r_ref, v_ref, an_ref, lw_ref, kd_ref, bv_ref, gate_ref, bonus_ref) = refs[3 * n_parts:]
    i = pl.program_id(0)
    f32, bf16 = jnp.float32, jnp.bfloat16
    g, sc, sh = g_ref[...], sc_ref[0], sh_ref[0]
    hn = _adaln(_residual_rows(x_refs, n_a), g, sc, sh).astype(bf16)
    halo = _adaln(jnp.concatenate([_residual_rows(xp_refs, n_a), _residual_rows(xn_refs, n_a)], axis=0),
                  g, sc, sh).astype(bf16)
    hn_halo = jnp.concatenate([hn, halo], axis=0)

    keep_prev = (1 - tbl_ref[0, i]).astype(f32)
    keep_next = (1 - tbl_ref[1, i]).astype(f32)
    row = lax.broadcasted_iota(jnp.int32, (BLK, 1), 0)
    c = RWKV_DIM

    def plain(c0, c1):
        return jnp.dot(hn, w_ref[:, c0:c1], preferred_element_type=f32)

    def proj(c0, c1):
        return jnp.dot(hn_halo, w_ref[:, L0_SHIFT[0] + c0:L0_SHIFT[0] + c1], preferred_element_type=f32)

    def shift(both, c0, c1):
        cur = both[:BLK]
        prev = jnp.where(row == 0, both[BLK + 7:BLK + 8] * keep_prev, pltpu.roll(cur, 1, 0))
        nxt = jnp.where(row == BLK - 1, both[BLK + 8:BLK + 9] * keep_next, pltpu.roll(cur, BLK - 1, 0))
        return cur + mup_ref[:, c0:c1] * (prev - cur) + mun_ref[:, c0:c1] * (nxt - cur)

    lo = 3 * c + 2 * W_LORA
    p_wl = proj(3 * c, lo)
    p_ag = proj(lo, lo + 256)
    p_k = proj(c, 2 * c)
    wl = shift(p_wl, 3 * c, lo)
    ag = shift(p_ag, lo, lo + 256)
    p_r = proj(0, c)
    k = shift(p_k, c, 2 * c)
    k_k, k_a, r_k = rwp_ref[0:1], rwp_ref[1:2], rwp_ref[2:3]
    w_lin = jnp.dot(jnp.tanh(wl).astype(bf16), w2_ref[...], preferred_element_type=f32)
    a_lora = jnp.dot(ag.astype(bf16), a2_ref[...], preferred_element_type=f32)
    gate_ref[...] = jnp.dot(jax.nn.sigmoid(ag).astype(bf16), g2_ref[...],
                            preferred_element_type=f32).astype(gate_ref.dtype)
    p_v = proj(2 * c, 3 * c)
    r = shift(p_r, 0, c)
    r_ref[...] = r.astype(r_ref.dtype)
    kk = k * k_k
    kk = kk * lax.rsqrt(_head_sum(kk * kk, e_ref, et_ref) + 1e-12)
    an_ref[...] = (-kk).astype(an_ref.dtype)
    z_ref[...] = plain(L0_Z[0], L0_Z[1]).astype(z_ref.dtype)
    v = shift(p_v, 2 * c, 3 * c)
    v_ref[...] = v.astype(v_ref.dtype)
    kd_sum = None
    xbc_cols = (L0_XBC[0], (L0_XBC[0] + L0_XBC[1]) // 2, L0_XBC[1])
    for d in range(2):
        xbc_ref[:, xbc_cols[d] - L0_XBC[0]:xbc_cols[d + 1] - L0_XBC[0]] = plain(xbc_cols[d], xbc_cols[d + 1])
        w_log = -_softplus(-(rwp_ref[3 + d:4 + d] + w_lin[:, d * c:(d + 1) * c])) - 0.5
        lw_ref[d] = -jnp.exp(w_log)
        a = jax.nn.sigmoid(rwp_ref[5 + d:6 + d] + a_lora)
        kd = k * (1.0 + (a - 1.0) * k_a)
        kd_ref[d] = kd.astype(kd_ref.dtype)
        bv_ref[d] = (kk * a).astype(bv_ref.dtype)
        kd_sum = kd if kd_sum is None else kd_sum + kd
    dt_ref[...] = plain(L0_DT[0], L0_DT[1])
    bonus_ref[...] = (_head_sum(r * kd_sum * r_k, e_ref, et_ref) * v).astype(bonus_ref.dtype)


def _seq_tables(n_ctx, n_lat, l_lat):
    nb = l_lat // BLK
    cond = [0] * n_ctx + [1 + s for s in range(n_lat) for _ in range(nb)]
    first = [1] * n_ctx + [1 if j == 0 else 0 for _ in range(n_lat) for j in range(nb)]
    last = [1] * n_ctx + [1 if j == nb - 1 else 0 for _ in range(n_lat) for j in range(nb)]
    return cond, first, last


def l0_in(x, g1, scb, shb, w_packed, mup, mun, rwp, w2bd, a2p, g2p, e_ind, et_ind, first, last):
    xs = x if isinstance(x, (tuple, list)) else (x,)
    D = xs[0].shape[1]
    R = sum(a.shape[0] for a in xs)
    nblk = R // BLK
    n_a = xs[0].shape[0] // BLK
    tbl = jnp.asarray([first, last], jnp.int32)
    c = RWKV_DIM
    row = lambda i, t: (i, 0)
    full = lambda shape: pl.BlockSpec(shape, lambda i, t: (0,) * len(shape))
    rows = lambda n: pl.BlockSpec((BLK, n), row)
    rows2 = lambda n: pl.BlockSpec((2, BLK, n), lambda i, t: (0, i, 0))
    f32, act = jnp.float32, ACT_DTYPE
    sds = jax.ShapeDtypeStruct
    h8 = BLK // 8

    def part_specs(k):
        nb_k, b0 = xs[k].shape[0] // BLK, (0 if k == 0 else n_a)
        local = lambda i: jnp.clip(i - b0, 0, nb_k - 1)
        return (pl.BlockSpec((BLK, D), lambda i, t: (local(i), 0)),
                pl.BlockSpec((8, D), lambda i, t: (jnp.maximum(local(i) * h8 - 1, 0), 0)),
                pl.BlockSpec((8, D), lambda i, t: (jnp.minimum((local(i) + 1) * h8, nb_k * h8 - 1), 0)))

    specs = [part_specs(k) for k in range(len(xs))]
    x_specs = [s[j] for j in range(3) for s in specs]
    return pl.pallas_call(
        functools.partial(_l0_in_kernel, n_parts=len(xs), n_a=n_a),
        grid_spec=pltpu.PrefetchScalarGridSpec(
            num_scalar_prefetch=1,
            grid=(nblk,),
            in_specs=[*x_specs,
                      full((1, D)),
                      pl.BlockSpec((1, 1, D), lambda i, t: (i, 0, 0)),
                      pl.BlockSpec((1, 1, D), lambda i, t: (i, 0, 0)),
                      full((D, L0_COLS)), full(mup.shape), full(mun.shape), full(rwp.shape),
                      full(w2bd.shape), full(a2p.shape), full(g2p.shape), full(e_ind.shape), full(et_ind.shape)],
            out_specs=[rows(c), rows(SSD_XBC), rows(128), rows(c), rows(c), rows(c),
                       rows2(c), rows2(c), rows2(c), rows(c), rows(c)]),
        out_shape=[sds((R, c), act), sds((R, SSD_XBC), f32), sds((R, 128), f32), sds((R, c), act), sds((R, c), act),
                   sds((R, c), act), sds((2, R, c), f32), sds((2, R, c), act), sds((2, R, c), act),
                   sds((R, c), act), sds((R, c), act)],
        compiler_params=pltpu.CompilerParams(dimension_semantics=("arbitrary",), vmem_limit_bytes=L0_VMEM_BYTES),
        name="l0_in",
    )(tbl, *xs, *xs, *xs, g1, scb, shb, w_packed, mup, mun, rwp, w2bd, a2p, g2p, e_ind, et_ind)


def l0_pack_weights(p, e):
    bf16 = jnp.bfloat16
    w = p['ab_w_in'][e]
    D = w.shape[0]
    c = RWKV_DIM
    rw0 = SSD_IN
    ag0 = rw0 + 3 * c + 2 * W_LORA
    w_packed = jnp.concatenate([
        w[:, :SSD_INNER + SSD_XBC], w[:, rw0:ag0], w[:, ag0:ag0 + A_LORA + G_LORA],
        jnp.zeros((D, 256 - A_LORA - G_LORA), w.dtype),
        w[:, SSD_INNER + SSD_XBC:SSD_IN], jnp.zeros((D, 128 - SSD_HEADS), w.dtype)], axis=1).astype(bf16)

    def pack_mu(mu):
        return jnp.concatenate([mu, jnp.zeros((256 - A_LORA - G_LORA,), mu.dtype)])[None, :]

    rwp = jnp.stack([p['rwkv_k_k'][e], p['rwkv_k_a'][e], p['rwkv_r_k'][e].reshape(-1), p['rwkv_w0'][e, 0],
                     p['rwkv_w0'][e, 1], p['rwkv_a0'][e, 0], p['rwkv_a0'][e, 1], jnp.zeros((c,), jnp.float32)])
    zw = jnp.zeros((W_LORA, c), jnp.float32)
    w2bd = jnp.concatenate([jnp.concatenate([p['rwkv_w2'][e, 0], zw], axis=1),
                            jnp.concatenate([zw, p['rwkv_w2'][e, 1]], axis=1)], axis=0).astype(bf16)
    a2p = jnp.concatenate([p['rwkv_a2'][e], jnp.zeros((256 - A_LORA, c), jnp.float32)], axis=0).astype(bf16)
    g2p = jnp.concatenate([jnp.zeros((A_LORA, c), jnp.float32), p['rwkv_g2'][e],
                           jnp.zeros((256 - A_LORA - G_LORA, c), jnp.float32)], axis=0).astype(bf16)
    head = jnp.arange(c) // RWKV_N
    e_ind = (head[:, None] == jnp.arange(128)[None, :]).astype(bf16)
    return w_packed, pack_mu(p['rwkv_mu_prev'][e]), pack_mu(p['rwkv_mu_next'][e]), rwp, w2bd, a2p, g2p, e_ind, e_ind.T


SSD_QH = 4
SSD_VMEM_BYTES = 48 * 1024 * 1024
NEG_BIG = -1e30
LOG2E = 1.4426950408889634


def _conv_silu(cur, prev_row, next_row, w_ref, b_ref):
    row = lax.broadcasted_iota(jnp.int32, (BLK, 1), 0)
    prev = jnp.where(row == 0, prev_row, pltpu.roll(cur, 1, 0))
    nxt = jnp.where(row == BLK - 1, next_row, pltpu.roll(cur, BLK - 1, 0))
    y = w_ref[0:1] * prev + w_ref[1:2] * cur + w_ref[2:3] * nxt + b_ref[...]
    return y * jax.nn.sigmoid(y)


def _ssd_kernel(tbl_ref, xs_ref, b_ref, c_ref, dt_ref, cwx_ref, cwb_ref, cwc_ref, cbx_ref, cbb_ref, cbc_ref,
                sel_ref, hp_ref, s0_ref, y_ref, fs_ref, xa_ref, ba_ref, ca_ref, sfx_ref, ldb_ref, st_ref, *, n_zero):
    f32, bf16 = jnp.float32, jnp.bfloat16
    sb = pl.program_id(0)
    sbr, qw = xs_ref.shape
    nch = sbr // BLK
    P, QH = SSD_P, SSD_QH
    t_i = lax.broadcasted_iota(jnp.int32, (BLK, BLK), 0)
    s_i = lax.broadcasted_iota(jnp.int32, (BLK, BLK), 1)
    lower = s_i <= t_i
    upper = s_i >= t_i
    tri_lo = jnp.where(lower, 1.0, 0.0).astype(bf16)
    tri_up = jnp.where(upper, 1.0, 0.0).astype(bf16)
    hp = hp_ref[0]
    sel = sel_ref[0]
    ind = jnp.where(lax.broadcasted_iota(jnp.int32, (128, qw), 1) // P == lax.broadcasted_iota(jnp.int32, (128, qw), 0),
                    1.0, 0.0).astype(bf16)
    head_of_lane = lax.broadcasted_iota(jnp.int32, (1, qw), 1) // P

    def expand(cols):
        return _sum_split(cols, ind, 2)

    d_row = expand(jnp.broadcast_to(hp[4:5], (8, 128)))[0:1]

    def chunk_rows(c):
        return pl.ds(pl.multiple_of(c * BLK, BLK), BLK)

    def neighbours(ref, c, keep_prev, keep_next):
        lo = jnp.maximum(c * BLK - 1, 0)
        hi = jnp.minimum((c + 1) * BLK, sbr - 1)
        return ref[pl.ds(lo, 1), :] * keep_prev, ref[pl.ds(hi, 1), :] * keep_next

    def fwd(c, carry):
        blk = sb * nch + c
        first, last = tbl_ref[0, blk], tbl_ref[1, blk]
        kp, kn = (1 - first).astype(f32), (1 - last).astype(f32)
        rows = chunk_rows(c)

        @pl.when(first == 1)
        def _():
            st_ref[0] = jnp.where(sb >= n_zero, s0_ref[0, 0], 0.0)

        xa = _conv_silu(xs_ref[rows, :], *neighbours(xs_ref, c, kp, kn), cwx_ref, cbx_ref)
        bm = _conv_silu(b_ref[rows, :], *neighbours(b_ref, c, kp, kn), cwb_ref, cbb_ref)
        cm = _conv_silu(c_ref[rows, :], *neighbours(c_ref, c, kp, kn), cwc_ref, cbc_ref)
        xb, bmb, cmb = xa.astype(bf16), bm.astype(bf16), cm.astype(bf16)
        xa_ref[rows, :] = xb
        ba_ref[rows, :] = bmb
        ca_ref[rows, :] = cmb
        dtq = _sum_split(dt_ref[rows, :], sel, 2)
        dtf = _softplus(dtq + hp[0:1])
        dtb = _softplus(dtq + hp[1:2])
        acs = _sum_split_left(tri_lo, dtf * hp[2:3], 3)
        sfx = _sum_split_left(tri_up, dtb * hp[3:4], 3)
        ldf, ldb = jnp.log(dtf), jnp.log(dtb)
        sfx_ref[rows, :] = sfx
        ldb_ref[rows, :] = ldb
        a2, s2 = acs * LOG2E, sfx * LOG2E
        a2r = (a2 - ldf * LOG2E).T
        s2r = (s2 - ldb * LOG2E).T
        g = lax.dot_general(cmb, bmb, (((1,), (1,)), ((), ())), preferred_element_type=f32)
        y_diag = None
        for j in range(QH):
            m = (g * (jnp.exp2(jnp.where(lower, a2[:, j:j + 1] - a2r[j:j + 1, :], NEG_BIG))
                      + jnp.exp2(jnp.where(upper, s2[:, j:j + 1] - s2r[j:j + 1, :], NEG_BIG)))).astype(bf16)
            xh = jnp.where(head_of_lane == j, xb, jnp.zeros((), bf16))
            t = jnp.dot(m, xh, preferred_element_type=f32)
            y_diag = t if y_diag is None else y_diag + t
        ea = jnp.exp(expand(acs))
        wf = jnp.exp(expand(acs[BLK - 1:BLK] - acs + ldf))
        s_in = st_ref[0]
        y_ref[rows, :] = xa * d_row + y_diag + ea * jnp.dot(cmb, s_in.astype(bf16), preferred_element_type=f32)
        st_ref[0] = ea[BLK - 1:BLK] * s_in + lax.dot_general(
            bmb, (xa * wf).astype(bf16), (((0,), (0,)), ((), ())), preferred_element_type=f32)
        fs_ref[c, 0] = st_ref[0]
        return carry

    lax.fori_loop(0, nch, fwd, 0)

    def bwd(k, carry):
        c = nch - 1 - k
        blk = sb * nch + c
        rows = chunk_rows(c)

        @pl.when(tbl_ref[1, blk] == 1)
        def _():
            st_ref[1] = jnp.where(sb >= n_zero, s0_ref[0, 1], 0.0)

        sfx = sfx_ref[rows, :]
        eb = jnp.exp(expand(sfx))
        wb = jnp.exp(expand(sfx[0:1] - sfx + ldb_ref[rows, :]))
        s_in = st_ref[1]
        y_ref[rows, :] += eb * jnp.dot(ca_ref[rows, :], s_in.astype(bf16), preferred_element_type=f32)
        st_ref[1] = eb[0:1] * s_in + lax.dot_general(
            ba_ref[rows, :], (xa_ref[rows, :].astype(f32) * wb).astype(bf16), (((0,), (0,)), ((), ())),
            preferred_element_type=f32)
        fs_ref[c, 1] = st_ref[1]
        return carry

    lax.fori_loop(0, nch, bwd, 0)


def ssd_scan(xbc, dt, conv_w, conv_b, sel, hp, s0, n_zero, first, last, sb_rows):
    R = xbc.shape[0]
    n_sb = R // sb_rows
    nch = sb_rows // BLK
    nq = SSD_HEADS // SSD_QH
    qw = SSD_QH * SSD_P
    qpg = SSD_HPG // SSD_QH
    b_blk = SSD_INNER // SSD_N
    c_blk = b_blk + SSD_GROUPS
    tbl = jnp.asarray([first, last], jnp.int32)
    cw = conv_w
    cb = conv_b.reshape(1, -1)
    f32, bf16 = jnp.float32, jnp.bfloat16
    return pl.pallas_call(
        functools.partial(_ssd_kernel, n_zero=n_zero),
        grid_spec=pltpu.PrefetchScalarGridSpec(
            num_scalar_prefetch=1,
            grid=(n_sb, nq),
            in_specs=[pl.BlockSpec((sb_rows, qw), lambda s, q, t: (s, q)),
                      pl.BlockSpec((sb_rows, SSD_N), lambda s, q, t: (s, b_blk + q // qpg)),
                      pl.BlockSpec((sb_rows, SSD_N), lambda s, q, t: (s, c_blk + q // qpg)),
                      pl.BlockSpec((sb_rows, 128), lambda s, q, t: (s, 0)),
                      pl.BlockSpec((3, qw), lambda s, q, t: (0, q)),
                      pl.BlockSpec((3, SSD_N), lambda s, q, t: (0, b_blk + q // qpg)),
                      pl.BlockSpec((3, SSD_N), lambda s, q, t: (0, c_blk + q // qpg)),
                      pl.BlockSpec((1, qw), lambda s, q, t: (0, q)),
                      pl.BlockSpec((1, SSD_N), lambda s, q, t: (0, b_blk + q // qpg)),
                      pl.BlockSpec((1, SSD_N), lambda s, q, t: (0, c_blk + q // qpg)),
                      pl.BlockSpec((1, 128, 128), lambda s, q, t: (q, 0, 0)),
                      pl.BlockSpec((1, 8, 128), lambda s, q, t: (q, 0, 0)),
                      pl.BlockSpec((1, 2, SSD_N, qw), lambda s, q, t: (jnp.maximum(s - n_zero, 0), 0, 0, q))],
            out_specs=[pl.BlockSpec((sb_rows, qw), lambda s, q, t: (s, q)),
                       pl.BlockSpec((nch, 2, SSD_N, qw), lambda s, q, t: (s, 0, 0, q))],
            scratch_shapes=[pltpu.VMEM((sb_rows, qw), bf16), pltpu.VMEM((sb_rows, SSD_N), bf16),
                            pltpu.VMEM((sb_rows, SSD_N), bf16), pltpu.VMEM((sb_rows, 128), f32),
                            pltpu.VMEM((sb_rows, 128), f32), pltpu.VMEM((2, SSD_N, qw), f32)]),
        out_shape=[jax.ShapeDtypeStruct((R, SSD_INNER), f32),
                   jax.ShapeDtypeStruct((R // BLK, 2, SSD_N, SSD_INNER), f32)],
        compiler_params=pltpu.CompilerParams(dimension_semantics=("arbitrary", "arbitrary"),
                                             vmem_limit_bytes=SSD_VMEM_BYTES),
        name="ssd_scan",
    )(tbl, xbc, xbc, xbc, dt, cw, cw, cw, cb, cb, cb, sel, hp, s0)


def ssd_tables(p, e):
    nq = SSD_HEADS // SSD_QH
    lane = jnp.arange(128)
    sel = jnp.stack([(lane[:, None] == (q * SSD_QH + lane[None, :])) & (lane[None, :] < SSD_QH)
                     for q in range(nq)]).astype(jnp.bfloat16)
    a_neg = -jnp.exp(p['ssd_a_log'][e].astype(jnp.float32))
    rows = jnp.stack([p['ssd_dt_bias'][e, 0], p['ssd_dt_bias'][e, 1], a_neg[0], a_neg[1], p['ssd_d'][e]])
    hp = jnp.zeros((nq, 8, 128), jnp.float32)
    hp = hp.at[:, :5, :SSD_QH].set(jnp.transpose(rows.reshape(5, nq, SSD_QH), (1, 0, 2)))
    return sel, hp


MIX_VMEM_BYTES = 48 * 1024 * 1024
MIX_BLOCKS = 1
ROUTER_LANES = 128


def _residual_norm_router(x, out, g1, n2g, sc2, sh2, rw_ref, x_out_ref, hn_ref, aff_ref):
    rows, d = x.shape
    nb = g1.shape[0]
    x_new = x.reshape(nb, rows // nb, d) + g1 * out.reshape(nb, rows // nb, d)
    x_out_ref[...] = x_new.reshape(rows, d)
    hn = _adaln(x_new, n2g, sc2, sh2).reshape(rows, d)
    hn_ref[...] = hn.astype(hn_ref.dtype)
    logits = _dot(hn, rw_ref[...], ((1,), (0,)), 3)
    lane = lax.broadcasted_iota(jnp.int32, logits.shape, 1)
    logits = jnp.where(lane < N_EXPERTS, logits, NEG_BIG)
    ex = jnp.exp(logits - jnp.max(logits, axis=-1, keepdims=True))
    aff = ex / jnp.sum(ex, axis=-1, keepdims=True)
    aff_ref[...] = aff.T[:N_EXPERTS]


def _residual_rows(x_refs, n_a):
    if len(x_refs) == 1:
        return x_refs[0][...]
    return jnp.where(pl.program_id(0) < n_a, x_refs[0][...], x_refs[1][...])


def _l0_out_kernel(ys_ref, z_ref, yf_ref, yb_ref, bonus_ref, gate_ref, sg_ref, lnw_ref, lnb_ref, e_ref, et_ref, w_ref,
                   *rest, n_a):
    *x_refs, g1_ref, n2g_ref, sc2_ref, sh2_ref, rw_ref, x_out_ref, hn_ref, aff_ref = rest
    f32, bf16 = jnp.float32, jnp.bfloat16
    z = z_ref[...].astype(f32)
    ys = ys_ref[...] * (z * jax.nn.sigmoid(z))
    gw = SSD_INNER // SSD_GROUPS
    parts = []
    for gi in range(SSD_GROUPS):
        yg = ys[:, gi * gw:(gi + 1) * gw]
        parts.append(yg * lax.rsqrt(jnp.mean(yg * yg, -1, keepdims=True) + NORM_EPS))
    a1 = jnp.concatenate(parts, axis=1) * sg_ref[...]
    o = yf_ref[...].astype(f32) + yb_ref[...].astype(f32)
    mu = _head_sum(o, e_ref, et_ref) * (1.0 / RWKV_N)
    oc = o - mu
    var = _head_sum(oc * oc, e_ref, et_ref) * (1.0 / RWKV_N)
    o = oc * lax.rsqrt(var + RWKV_GN_EPS) * lnw_ref[...] + lnb_ref[...]
    o = (o + bonus_ref[...].astype(f32)) * gate_ref[...].astype(f32)
    out = (jnp.dot(a1.astype(bf16), w_ref[:SSD_INNER], preferred_element_type=f32)
           + jnp.dot(o.astype(bf16), w_ref[SSD_INNER:], preferred_element_type=f32))
    _residual_norm_router(_residual_rows(x_refs, n_a), out, g1_ref[...], n2g_ref[...], sc2_ref[...], sh2_ref[...],
                          rw_ref, x_out_ref, hn_ref, aff_ref)


def _l1_out_kernel(a_ref, w_ref, *rest, n_a):
    *x_refs, g1_ref, n2g_ref, sc2_ref, sh2_ref, rw_ref, x_out_ref, hn_ref, aff_ref = rest
    out = jnp.dot(a_ref[...].astype(jnp.bfloat16), w_ref[...], preferred_element_type=jnp.float32)
    _residual_norm_router(_residual_rows(x_refs, n_a), out, g1_ref[...], n2g_ref[...], sc2_ref[...], sh2_ref[...],
                          rw_ref, x_out_ref, hn_ref, aff_ref)


def _mix_out_call(kernel_fn, name, lead_args, lead_specs, w_out, x, g1b, n2g, sc2b, sh2b, router_w):
    mb = MIX_BLOCKS
    xs = x if isinstance(x, (tuple, list)) else (x,)
    D = xs[0].shape[1]
    R = sum(a.shape[0] for a in xs)
    n_a = xs[0].shape[0] // (mb * BLK)
    assert (R // BLK) % mb == 0 and xs[0].shape[0] % (mb * BLK) == 0
    full = lambda a: pl.BlockSpec(a.shape, lambda i: (0,) * a.ndim)
    blkrow = pl.BlockSpec((mb, 1, D), lambda i: (i, 0, 0))
    rw = jnp.zeros((D, ROUTER_LANES), jnp.float32).at[:, :N_EXPERTS].set(router_w)
    args = list(lead_args) + [w_out, *xs, g1b, n2g, sc2b, sh2b, rw]
    if len(xs) == 1:
        x_specs = [pl.BlockSpec((mb * BLK, D), lambda i: (i, 0))]
        aliases = {len(lead_args) + 1: 0}
    else:
        x_specs = [pl.BlockSpec((mb * BLK, D), lambda i: (jnp.minimum(i, n_a - 1), 0)),
                   pl.BlockSpec((mb * BLK, D), lambda i: (jnp.maximum(i - n_a, 0), 0))]
        aliases = {}
    in_specs = list(lead_specs) + [full(w_out), *x_specs, blkrow, full(n2g), blkrow, blkrow, full(rw)]
    return pl.pallas_call(
        functools.partial(kernel_fn, n_a=n_a),
        grid=(R // (mb * BLK),),
        in_specs=in_specs,
        out_specs=[pl.BlockSpec((mb * BLK, D), lambda i: (i, 0)), pl.BlockSpec((mb * BLK, D), lambda i: (i, 0)),
                   pl.BlockSpec((N_EXPERTS, mb * BLK), lambda i: (0, i))],
        out_shape=[jax.ShapeDtypeStruct((R, D), jnp.float32), jax.ShapeDtypeStruct((R, D), jnp.bfloat16),
                   jax.ShapeDtypeStruct((N_EXPERTS, R), jnp.float32)],
        input_output_aliases=aliases,
        compiler_params=pltpu.CompilerParams(dimension_semantics=("arbitrary",), vmem_limit_bytes=MIX_VMEM_BYTES),
        name=name,
    )(*args)


def l0_out(ys, z, yf, yb, bonus, gate, ssd_g, ln_w, ln_b, e_ind, et_ind, w_out, x, g1b, n2g, sc2b, sh2b, router_w):
    c = RWKV_DIM
    rows = lambda n: pl.BlockSpec((MIX_BLOCKS * BLK, n), lambda i: (i, 0))
    full = lambda a: pl.BlockSpec(a.shape, lambda i: (0,) * a.ndim)
    lead = [ys, z, yf, yb, bonus, gate, ssd_g, ln_w, ln_b, e_ind, et_ind]
    specs = [rows(SSD_INNER), rows(SSD_INNER), rows(c), rows(c), rows(c), rows(c),
             full(ssd_g), full(ln_w), full(ln_b), full(e_ind), full(et_ind)]
    return _mix_out_call(_l0_out_kernel, "l0_out", lead, specs, w_out, x, g1b, n2g, sc2b, sh2b, router_w)


def l1_out(a, w_out, x, g1b, n2g, sc2b, sh2b, router_w):
    specs = [pl.BlockSpec((MIX_BLOCKS * BLK, a.shape[1]), lambda i: (i, 0))]
    return _mix_out_call(_l1_out_kernel, "l1_out", [a], specs, w_out, x, g1b, n2g, sc2b, sh2b, router_w)


RET_HPS = 2


def _l1_in_kernel(tbl_ref, x_ref, g_ref, sc_ref, sh_ref, w_ref, cos_ref, sin_ref, q_ref, k_ref, v_ref, gg_ref):
    f32 = jnp.float32
    hn = _adaln(x_ref[...], g_ref[...], sc_ref[0], sh_ref[0]).astype(jnp.bfloat16)
    cosf, sinf = cos_ref[0], sin_ref[0]

    def rope(x):
        parts = []
        for h in range(RET_HEADS):
            xh = x[:, h * RET_DK:(h + 1) * RET_DK]
            parts.append(xh * cosf + pltpu.roll(xh, RET_DK // 2, 1) * sinf)
        return jnp.concatenate(parts, axis=1)

    q_ref[...] = rope(jnp.dot(hn, w_ref[:, :RET_QK], preferred_element_type=f32)).astype(q_ref.dtype)
    k_ref[...] = (rope(jnp.dot(hn, w_ref[:, RET_QK:2 * RET_QK], preferred_element_type=f32))
                  * (RET_DK ** -0.5)).astype(k_ref.dtype)
    v_ref[...] = jnp.dot(hn, w_ref[:, 2 * RET_QK:2 * RET_QK + RET_V], preferred_element_type=f32).astype(v_ref.dtype)
    gg_ref[...] = jnp.dot(hn, w_ref[:, 2 * RET_QK + RET_V:], preferred_element_type=f32).astype(gg_ref.dtype)


def l1_in(x, g1, scb, shb, w_bf16, cos_t, sin_t, rope_blk):
    R, D = x.shape
    nblk = R // BLK
    tbl = jnp.asarray([rope_blk], jnp.int32)
    f32 = jnp.float32
    row = lambda n: pl.BlockSpec((BLK, n), lambda i, t: (i, 0))
    full = lambda a: pl.BlockSpec(a.shape, lambda i, t: (0,) * a.ndim)
    blkrow = pl.BlockSpec((1, 1, D), lambda i, t: (i, 0, 0))
    ropespec = pl.BlockSpec((1, BLK, RET_DK), lambda i, t: (t[0, i], 0, 0))
    return pl.pallas_call(
        _l1_in_kernel,
        grid_spec=pltpu.PrefetchScalarGridSpec(
            num_scalar_prefetch=1, grid=(nblk,),
            in_specs=[row(D), full(g1), blkrow, blkrow, full(w_bf16), ropespec, ropespec],
            out_specs=[row(RET_QK), row(RET_QK), row(RET_V), row(RET_V)]),
        out_shape=[jax.ShapeDtypeStruct((R, RET_QK), ACT_DTYPE), jax.ShapeDtypeStruct((R, RET_QK), ACT_DTYPE),
                   jax.ShapeDtypeStruct((R, RET_V), ACT_DTYPE), jax.ShapeDtypeStruct((R, RET_V), ACT_DTYPE)],
        compiler_params=pltpu.CompilerParams(dimension_semantics=("arbitrary",), vmem_limit_bytes=L0_VMEM_BYTES),
        name="l1_in",
    )(tbl, x, g1, scb, shb, w_bf16, cos_t, sin_t)


def _ret_kernel(tbl_ref, q_ref, k_ref, v_ref, g_ref, lg_ref, nw_ref, nb_ref, s0_ref, a_ref, fs_ref, st_ref, *,
                n_zero):
    f32, bf16 = jnp.float32, jnp.bfloat16
    sb = pl.program_id(0)
    nch = q_ref.shape[0] // BLK
    heads = range(RET_HPS)
    ks = [slice(h * RET_DK, (h + 1) * RET_DK) for h in heads]
    vs = [slice(h * RET_DV, (h + 1) * RET_DV) for h in heads]
    lgf = [lg_ref[h, 0:1, 0:1] for h in heads]
    lgb = [lg_ref[h, 1:2, 0:1] for h in heads]
    t_i = lax.broadcasted_iota(jnp.int32, (BLK, BLK), 0)
    s_i = lax.broadcasted_iota(jnp.int32, (BLK, BLK), 1)
    dist = (t_i - s_i).astype(f32)
    dm = [jnp.exp(jnp.where(s_i <= t_i, dist * lgf[h], NEG_BIG)) + jnp.exp(jnp.where(s_i >= t_i, -dist * lgb[h], NEG_BIG))
          for h in heads]
    tk = lax.broadcasted_iota(jnp.int32, (BLK, RET_DK), 0).astype(f32)
    tv = lax.broadcasted_iota(jnp.int32, (BLK, RET_DV), 0).astype(f32)
    k_to_end_f = [jnp.exp((BLK - 1.0 - tk) * lgf[h]) for h in heads]
    k_to_end_b = [jnp.exp(tk * lgb[h]) for h in heads]
    from_start_f = [jnp.exp((tv + 1.0) * lgf[h]) for h in heads]
    from_start_b = [jnp.exp((BLK - tv) * lgb[h]) for h in heads]
    nt, tn = (((1,), (1,)), ((), ())), (((0,), (0,)), ((), ()))

    def chunk_rows(c):
        return pl.ds(pl.multiple_of(c * BLK, BLK), BLK)

    def fwd(c, carry):
        blk = sb * nch + c
        rows = chunk_rows(c)

        @pl.when(tbl_ref[0, blk] == 1)
        def _():
            st_ref[0] = jnp.where(sb >= n_zero, s0_ref[0, 0], 0.0)

        q = [q_ref[rows, ks[h]].astype(bf16) for h in heads]
        k = [k_ref[rows, ks[h]] for h in heads]
        v = [v_ref[rows, vs[h]].astype(bf16) for h in heads]
        s_in = [st_ref[0, h] for h in heads]
        g = [lax.dot_general(q[h], k[h].astype(bf16), nt, preferred_element_type=f32) for h in heads]
        y_diag = [jnp.dot((g[h] * dm[h]).astype(bf16), v[h], preferred_element_type=f32) for h in heads]
        y_off = [jnp.dot(q[h], s_in[h].astype(bf16), preferred_element_type=f32) for h in heads]
        kw = [(k[h].astype(f32) * k_to_end_f[h]).astype(bf16) for h in heads]
        upd = [lax.dot_general(kw[h], v[h], tn, preferred_element_type=f32) for h in heads]
        for h in heads:
            a_ref[rows, vs[h]] = y_diag[h] + from_start_f[h] * y_off[h]
            st_ref[0, h] = jnp.exp(BLK * lgf[h]) * s_in[h] + upd[h]
        fs_ref[c, 0] = st_ref[0]
        return carry

    lax.fori_loop(0, nch, fwd, 0)

    def bwd(j, carry):
        c = nch - 1 - j
        blk = sb * nch + c
        rows = chunk_rows(c)

        @pl.when(tbl_ref[1, blk] == 1)
        def _():
            st_ref[1] = jnp.where(sb >= n_zero, s0_ref[0, 1], 0.0)

        q = [q_ref[rows, ks[h]].astype(bf16) for h in heads]
        v = [v_ref[rows, vs[h]].astype(bf16) for h in heads]
        s_in = [st_ref[1, h] for h in heads]
        y_off = [jnp.dot(q[h], s_in[h].astype(bf16), preferred_element_type=f32) for h in heads]
        kw = [(k_ref[rows, ks[h]].astype(f32) * k_to_end_b[h]).astype(bf16) for h in heads]
        upd = [lax.dot_general(kw[h], v[h], tn, preferred_element_type=f32) for h in heads]
        for h in heads:
            st_ref[1, h] = jnp.exp(BLK * lgb[h]) * s_in[h] + upd[h]
            y = a_ref[rows, vs[h]] + from_start_b[h] * y_off[h]
            mu = jnp.mean(y, -1, keepdims=True)
            yc = y - mu
            var = jnp.mean(yc * yc, -1, keepdims=True)
            gg = g_ref[rows, vs[h]].astype(f32)
            a_ref[rows, vs[h]] = ((yc * lax.rsqrt(var + 1e-5) * nw_ref[:, vs[h]] + nb_ref[:, vs[h]])
                                  * (gg * jax.nn.sigmoid(gg)))
        fs_ref[c, 1] = st_ref[1]
        return carry

    lax.fori_loop(0, nch, bwd, 0)


def ret_scan(q, k, v, g, lg_tab, norm_w, norm_b, s0, n_zero, first, last, sb_rows):
    R = q.shape[0]
    n_sb = R // sb_rows
    nch = sb_rows // BLK
    tbl = jnp.asarray([first, last], jnp.int32)
    f32 = jnp.float32
    hps = RET_HPS
    return pl.pallas_call(
        functools.partial(_ret_kernel, n_zero=n_zero),
        grid_spec=pltpu.PrefetchScalarGridSpec(
            num_scalar_prefetch=1, grid=(n_sb, RET_HEADS // hps),
            in_specs=[pl.BlockSpec((sb_rows, hps * RET_DK), lambda s, h, t: (s, h)),
                      pl.BlockSpec((sb_rows, hps * RET_DK), lambda s, h, t: (s, h)),
                      pl.BlockSpec((sb_rows, hps * RET_DV), lambda s, h, t: (s, h)),
                      pl.BlockSpec((sb_rows, hps * RET_DV), lambda s, h, t: (s, h)),
                      pl.BlockSpec((hps, 8, 128), lambda s, h, t: (h, 0, 0)),
                      pl.BlockSpec((1, hps * RET_DV), lambda s, h, t: (0, h)),
                      pl.BlockSpec((1, hps * RET_DV), lambda s, h, t: (0, h)),
                      pl.BlockSpec((1, 2, hps, RET_DK, RET_DV), lambda s, h, t: (jnp.maximum(s - n_zero, 0), 0, h, 0, 0))],
            out_specs=[pl.BlockSpec((sb_rows, hps * RET_DV), lambda s, h, t: (s, h)),
                       pl.BlockSpec((nch, 2, hps, RET_DK, RET_DV), lambda s, h, t: (s, 0, h, 0, 0))],
            scratch_shapes=[pltpu.VMEM((2, hps, RET_DK, RET_DV), f32)]),
        out_shape=[jax.ShapeDtypeStruct((R, RET_V), f32),
                   jax.ShapeDtypeStruct((R // BLK, 2, RET_HEADS, RET_DK, RET_DV), f32)],
        compiler_params=pltpu.CompilerParams(dimension_semantics=("arbitrary", "arbitrary"),
                                             vmem_limit_bytes=SSD_VMEM_BYTES),
        name="ret_scan",
    )(tbl, q, k, v, g, lg_tab, norm_w.reshape(1, -1), norm_b.reshape(1, -1), s0)


def _mod_kernel(c_ref, w_ref, b_ref, o_ref):
    c = c_ref[...]
    act = c * jax.nn.sigmoid(c)
    o_ref[0] = _dot(act, w_ref[0], ((1,), (0,)), 3) + b_ref[0]


def mod_vectors(conds, mod_w, mod_b):
    depth, D, n6 = mod_w.shape
    tn = D
    return pl.pallas_call(
        _mod_kernel,
        grid=(depth, n6 // tn),
        in_specs=[pl.BlockSpec(conds.shape, lambda i, j: (0, 0)),
                  pl.BlockSpec((1, D, tn), lambda i, j: (i, 0, j)),
                  pl.BlockSpec((1, 1, tn), lambda i, j: (i, 0, j))],
        out_specs=pl.BlockSpec((1, conds.shape[0], tn), lambda i, j: (i, 0, j)),
        out_shape=jax.ShapeDtypeStruct((depth, conds.shape[0], n6), jnp.float32),
        name="mod_vectors",
    )(conds, mod_w, mod_b.reshape(depth, 1, n6))


def rope_tables(n_tokens):
    rows = n_tokens // GRID_W
    row = np.repeat(np.arange(rows), GRID_W).astype(np.float64)
    col = np.tile(np.arange(GRID_W), rows).astype(np.float64)
    n_f = RET_DK // 4
    inv = ROPE_BASE ** (-np.arange(n_f, dtype=np.float64) / n_f)
    ang = np.concatenate([row[:, None] * inv, col[:, None] * inv], -1)
    return np.cos(ang), np.sin(ang)


def _rope_block_tables(n_ctx, n_lat, l_lat):
    nb = l_lat // BLK
    cos, sin = rope_tables(l_lat)
    cosf = np.concatenate([cos, cos], -1).reshape(nb, BLK, RET_DK)
    sinf = np.concatenate([-sin, sin], -1).reshape(nb, BLK, RET_DK)
    cos_t = jnp.asarray(np.concatenate([np.ones((1, BLK, RET_DK)), cosf]), jnp.float32)
    sin_t = jnp.asarray(np.concatenate([np.zeros((1, BLK, RET_DK)), sinf]), jnp.float32)
    rope_blk = [0] * n_ctx + [1 + j for _ in range(n_lat) for j in range(nb)]
    return cos_t, sin_t, rope_blk


def kernel(x_prompt, x_sample, state_ssd, state_rwkv, state_ret, c, c_ctx, mod_w, mod_b, norm1_g, norm2_g,
           router_w, exp_w_gate, exp_w_up, exp_w_down, ab_w_in, ab_w_out, ssd_conv_w, ssd_conv_b, ssd_dt_bias,
           ssd_a_log, ssd_d, ssd_norm_g, rwkv_mu_prev, rwkv_mu_next, rwkv_w0, rwkv_w2, rwkv_a0, rwkv_a2, rwkv_g2,
           rwkv_k_k, rwkv_k_a, rwkv_r_k, rwkv_ln_w, rwkv_ln_b, ret_w_in, ret_w_out, ret_decay_logit, ret_norm_w,
           ret_norm_b, final_norm_g):
    p = dict(mod_w=mod_w, mod_b=mod_b, norm1_g=norm1_g, norm2_g=norm2_g, router_w=router_w,
             exp_w_gate=exp_w_gate, exp_w_up=exp_w_up, exp_w_down=exp_w_down, ab_w_in=ab_w_in, ab_w_out=ab_w_out,
             ssd_conv_w=ssd_conv_w, ssd_conv_b=ssd_conv_b, ssd_dt_bias=ssd_dt_bias, ssd_a_log=ssd_a_log,
             ssd_d=ssd_d, ssd_norm_g=ssd_norm_g, rwkv_mu_prev=rwkv_mu_prev, rwkv_mu_next=rwkv_mu_next,
             rwkv_w0=rwkv_w0, rwkv_w2=rwkv_w2, rwkv_a0=rwkv_a0, rwkv_a2=rwkv_a2, rwkv_g2=rwkv_g2,
             rwkv_k_k=rwkv_k_k, rwkv_k_a=rwkv_k_a, rwkv_r_k=rwkv_r_k, rwkv_ln_w=rwkv_ln_w, rwkv_ln_b=rwkv_ln_b,
             ret_w_in=ret_w_in, ret_w_out=ret_w_out, ret_decay_logit=ret_decay_logit, ret_norm_w=ret_norm_w,
             ret_norm_b=ret_norm_b, final_norm_g=final_norm_g)
    f32, bf16 = jnp.float32, jnp.bfloat16
    n_ctx, l_ctx, D = x_prompt.shape
    n_lat, l_lat, _ = x_sample.shape
    assert l_ctx == BLK and l_lat % BLK == 0 and (n_ctx * BLK) % l_lat == 0
    n_sb_ctx = n_ctx * BLK // l_lat
    cond_id, first, last = _seq_tables(n_ctx, n_lat, l_lat)
    x = (x_prompt.reshape(-1, D), x_sample.reshape(-1, D))

    conds = jnp.concatenate([c_ctx[None, :], c, jnp.zeros((8 - 1 - n_lat, D), f32)])
    mods = mod_vectors(conds, mod_w, mod_b)[:, jnp.asarray(cond_id)]
    mods = mods.reshape(DEPTH, len(cond_id), 6, 1, D)

    new_ssd, new_rwkv, new_ret = [], [], []
    out = None
    for i in range(DEPTH):
        sh1, sc1, g1, sh2, sc2, g2 = (mods[i, :, k] for k in range(6))
        e = i // 2
        if i % 2 == 0:
            w_packed, mup, mun, rwp, w2bd, a2p, g2p, e_ind, et_ind = l0_pack_weights(p, e)
            z, xbc, dt, r, v, an, lw, kd, bv, gate, bonus = l0_in(
                x, norm1_g[i][None], sc1, sh1, w_packed, mup, mun, rwp, w2bd, a2p, g2p, e_ind, et_ind, first, last)
            sel, hp = ssd_tables(p, e)
            s0_ssd = jnp.transpose(state_ssd[:, e], (0, 1, 3, 2, 4)).reshape(n_lat, 2, SSD_N, SSD_INNER)
            ys, fs_ssd = ssd_scan(xbc, dt, ssd_conv_w[e], ssd_conv_b[e], sel, hp,
                                  s0_ssd, n_sb_ctx, first, last, l_lat)
            new_ssd.append(jnp.transpose(fs_ssd[:n_ctx].reshape(n_ctx, 2, SSD_N, SSD_HEADS, SSD_P), (0, 1, 3, 2, 4)))
            s0_rwkv = jnp.transpose(state_rwkv[:, e], (0, 1, 3, 2, 4)).reshape(n_lat, 2, RWKV_N, RWKV_DIM)
            yf, yb, sf_rwkv = rwkv_scan_pallas(r, v, an, lw, kd, bv, s0_rwkv, n_ctx,
                                               _rwkv_steps(n_ctx, n_lat, l_lat))
            new_rwkv.append(jnp.transpose(sf_rwkv[:n_ctx].reshape(n_ctx, 2, RWKV_N, RWKV_HEADS, RWKV_N),
                                          (0, 1, 3, 2, 4)))
            x, hn2, affT = l0_out(ys, z, yf, yb, bonus, gate, ssd_norm_g[e][None], rwkv_ln_w[e][None], rwkv_ln_b[e][None],
                                  e_ind, et_ind, ab_w_out[e].astype(bf16), x, g1, norm2_g[i][None], sc2, sh2,
                                  router_w[i])
        else:
            cos_t, sin_t, rope_blk = _rope_block_tables(n_ctx, n_lat, l_lat)
            q, k, v, gg = l1_in(x, norm1_g[i][None], sc1, sh1, ret_w_in[e].astype(bf16), cos_t, sin_t, rope_blk)
            lg = jax.nn.log_sigmoid(ret_decay_logit[e].astype(f32))
            lg_tab = jnp.zeros((RET_HEADS, 8, 128), f32).at[:, :2, :].set(jnp.transpose(lg)[:, :, None])
            a, fs_ret = ret_scan(q, k, v, gg, lg_tab, ret_norm_w[e], ret_norm_b[e],
                                 state_ret[:, e], n_sb_ctx, first, last, l_lat)
            new_ret.append(fs_ret[:n_ctx])
            x, hn2, affT = l1_out(a, ret_w_out[e].astype(bf16), x, g1, norm2_g[i][None], sc2, sh2, router_w[i])
        fin = final_norm_g if i == DEPTH - 1 else None
        out = moe_layer(x, hn2, affT, g2, exp_w_gate, exp_w_up, exp_w_down, i, n_ctx, l_lat, final_g=fin)
        if fin is None:
            x = out
    y_ctx, y_lat = out
    return (y_ctx.reshape(n_ctx, l_ctx, D), y_lat.reshape(n_lat, l_lat, D),
            jnp.stack(new_ssd, 1), jnp.stack(new_rwkv, 1), jnp.stack(new_ret, 1))
```

```python
import functools
import math

import jax
import jax.numpy as jnp
import numpy as np
from jax import lax
from jax.experimental import pallas as pl
from jax.experimental.pallas import tpu as pltpu

DEPTH = 2
GRID_W = 64
NORM_EPS = 1e-6
SSD_HEADS = 16
SSD_P = 64
SSD_INNER = SSD_HEADS * SSD_P
SSD_GROUPS = 2
SSD_HPG = SSD_HEADS // SSD_GROUPS
SSD_N = 128
SSD_XBC = SSD_INNER + 2 * SSD_GROUPS * SSD_N
SSD_IN = SSD_INNER + SSD_XBC + SSD_HEADS
RWKV_HEADS = 16
RWKV_N = 64
RWKV_DIM = RWKV_HEADS * RWKV_N
W_LORA = 64
A_LORA = 64
G_LORA = 128
RWKV_GN_EPS = 64e-5
RET_HEADS = 8
RET_DK = 128
RET_DV = 256
RET_QK = RET_HEADS * RET_DK
RET_V = RET_HEADS * RET_DV
ROPE_BASE = 10000.0
N_EXPERTS = 16
EC_CAPACITY = 2

ACT_DTYPE = jnp.bfloat16

RWKV_C = 64
RWKV_GH = 4
RWKV_GL = RWKV_GH * RWKV_N
RWKV_DOUBLING_PASSES = (3, 3, 3, 3, 1, 1)


def _split_bf16(x):
    hi = x.astype(jnp.bfloat16)
    lo = (x - hi.astype(jnp.float32)).astype(jnp.bfloat16)
    return hi, lo


def _dot(a, b, dims, passes):
    f = functools.partial(lax.dot_general, dimension_numbers=(dims, ((), ())),
                          preferred_element_type=jnp.float32)
    if passes == 1:
        return f(a.astype(jnp.bfloat16), b.astype(jnp.bfloat16))
    ah, al = _split_bf16(a)
    bh, bl = _split_bf16(b)
    return f(ah, bh) + (f(ah, bl) + f(al, bh))


def _rwkv_chunk_kernel(tbl_ref, r0_ref, v0_ref, a0_ref, r1_ref, v1_ref, a1_ref, lw0_ref, k0_ref, b0_ref,
                       lw1_ref, k1_ref, b1_ref, s0_ref, y0_ref, y1_ref, sf_ref, h_ref, *, p_inv, p_oth, n_zero):
    C, N, GH, GL = RWKV_C, RWKV_N, RWKV_GH, RWKV_GL
    i = pl.program_id(0)
    f32, bf16 = jnp.float32, jnp.bfloat16

    @pl.when(tbl_ref[3, i] == 1)
    def _():
        h_ref[...] = jnp.where(tbl_ref[2, i] >= n_zero, s0_ref[0], 0.0)

    t_i = lax.broadcasted_iota(jnp.int32, (C, GL), 0)
    s_i = lax.broadcasted_iota(jnp.int32, (C, GL), 1) & (N - 1)
    eye = (s_i == t_i).astype(f32)
    row2 = lax.broadcasted_iota(jnp.int32, (2 * C, GL), 0)
    rel2 = (lax.broadcasted_iota(jnp.int32, (2 * C, GL), 1) & (N - 1)) - (row2 & (C - 1))
    incl2 = row2 // C
    mask2 = [rel2 - incl2 < 0, -rel2 - incl2 < 0]
    bh_r = lax.broadcasted_iota(jnp.int32, (GL, GL), 0) // N
    bh_c = lax.broadcasted_iota(jnp.int32, (GL, GL), 1) // N
    blk = bh_r == bh_c
    tt = lax.broadcasted_iota(jnp.int32, (C, C), 0)
    ss = lax.broadcasted_iota(jnp.int32, (C, C), 1)
    tri = [(ss <= tt).astype(bf16), (ss >= tt).astype(bf16)]

    def bd(x, passes):
        pieces = []
        for _ in range(2 if passes == 3 else 1):
            hi = x.astype(bf16)
            x = x - hi.astype(f32)
            pieces.append(jnp.where(blk, jnp.concatenate([hi] * GH, axis=0), jnp.zeros((), bf16)))
        return pieces

    def mm(l, x, passes, dims=((1,), (0,))):
        f = functools.partial(lax.dot_general, dimension_numbers=(dims, ((), ())), preferred_element_type=f32)
        xs = bd(x, passes)
        lh = l.astype(bf16)
        if passes == 1:
            return f(lh, xs[0])
        ll = (l - lh.astype(f32)).astype(bf16)
        m = l.shape[0]
        both = f(jnp.concatenate([lh, ll], axis=0), xs[0])
        if passes == 2:
            return both[:m] + both[m:]
        return both[:m] + (f(lh, xs[1]) + both[m:])

    nt = ((1,), (1,))
    refs = [(r0_ref, v0_ref, a0_ref, lw0_ref, k0_ref, b0_ref), (r1_ref, v1_ref, a1_ref, lw1_ref, k1_ref, b1_ref)]
    lw, r_t, a_t, b_t, k_t, v = [], [], [], [], [], []
    for d, (r_ref, v_ref, a_ref, lw_ref, k_ref, b_ref) in enumerate(refs):
        lwd = lw_ref[0]
        lw_hi, lw_lo = _split_bf16(lwd)
        cum = (jnp.dot(tri[d], lw_hi, preferred_element_type=f32) + jnp.dot(tri[d], lw_lo, preferred_element_type=f32))
        w_inv = jnp.exp(-cum)
        lw.append(lwd)
        r_t.append(r_ref[...].astype(f32) * jnp.exp(cum))
        a_t.append(a_ref[...].astype(f32) * jnp.exp(cum - lwd))
        b_t.append(b_ref[0].astype(f32) * w_inv)
        k_t.append(k_ref[0].astype(f32) * w_inv)
        v.append(v_ref[...].astype(f32))
    lane_head = lax.broadcasted_iota(jnp.int32, (N, GL), 1) // N

    chains = [(d, slice(g * GL, (g + 1) * GL)) for d in range(2) for g in range(RWKV_HEADS // GH)]
    each = lambda fn: [fn(j, d, sl) for j, (d, sl) in enumerate(chains)]
    bg = each(lambda j, d, sl: b_t[d][:, sl])
    kg = each(lambda j, d, sl: k_t[d][:, sl])
    vg = each(lambda j, d, sl: v[d][:, sl])
    h0 = each(lambda j, d, sl: h_ref[d, :, sl])
    ar = each(lambda j, d, sl: jnp.concatenate([a_t[d][:, sl], r_t[d][:, sl]], axis=0))
    m_b = each(lambda j, d, sl: jnp.where(mask2[d], mm(ar[j], bg[j], p_oth, nt), 0.0))
    m_k = each(lambda j, d, sl: jnp.where(mask2[d], mm(ar[j], kg[j], p_oth, nt), 0.0))
    p = each(lambda j, d, sl: mm(m_b[j][:C], m_b[j][:C], p_inv[0]))
    tmat = each(lambda j, d, sl: eye + m_b[j][:C])
    for lv in range(int(math.log2(C)) - 2):
        pt = each(lambda j, d, sl: mm(jnp.concatenate([p[j], tmat[j]], axis=0), p[j], p_inv[1 + lv]))
        p = each(lambda j, d, sl: pt[j][:C])
        tmat = each(lambda j, d, sl: tmat[j] + pt[j][C:])
    tmat = each(lambda j, d, sl: tmat[j] + mm(tmat[j], p[j], p_inv[-1]))
    ar_h = each(lambda j, d, sl: mm(ar[j], h0[j], p_oth, nt))
    mk_v = each(lambda j, d, sl: mm(m_k[j], vg[j], p_oth))
    u = each(lambda j, d, sl: mm(tmat[j], ar_h[j][:C] + mk_v[j][:C], p_oth))
    y = each(lambda j, d, sl: ar_h[j][C:] + mm(m_b[j][C:], u[j], p_oth) + mk_v[j][C:])
    full = each(lambda j, d, sl: _dot(jnp.concatenate([u[j], vg[j]], axis=0), jnp.concatenate([bg[j], kg[j]], axis=0),
                                      ((0,), (0,)), p_oth))
    y_refs = (y0_ref, y1_ref)
    for j, (d, sl) in enumerate(chains):
        y_refs[d][:, sl] = y[j].astype(y_refs[d].dtype)
        z = jnp.zeros((N, GL), f32)
        for hh in range(GH):
            z = z + jnp.where(lane_head == hh, full[j][hh * N:(hh + 1) * N], 0.0)
        w_tot = jnp.exp(jnp.sum(lw[d][:, sl], axis=0, keepdims=True))
        h_ref[d, :, sl] = w_tot * (h0[j] + z)

    @pl.when(tbl_ref[4, i] == 1)
    def _():
        sf_ref[0] = h_ref[...]


def _rwkv_steps(n_ctx, n_lat, l_lat):
    C = RWKV_C
    rows = []
    seqs = [(s, s * BLK, BLK) for s in range(n_ctx)] + [(n_ctx + s, n_ctx * BLK + s * l_lat, l_lat) for s in range(n_lat)]
    for sid, row0, length in seqs:
        nc = length // C
        for j in range(nc):
            rows.append((row0 // C + j, row0 // C + nc - 1 - j, sid, int(j == 0), int(j == nc - 1)))
    return [list(col) for col in zip(*rows)]


def rwkv_scan_pallas(r, v, a, lw, k, b, s0, n_zero, steps, p_inv=RWKV_DOUBLING_PASSES, p_oth=1):
    R, HN = r.shape
    C, N = RWKV_C, RWKV_N
    tbl = jnp.asarray(steps, jnp.int32)
    fwd = pl.BlockSpec((C, HN), lambda i, t: (t[0, i], 0))
    bwd = pl.BlockSpec((C, HN), lambda i, t: (t[1, i], 0))
    fwd_d = pl.BlockSpec((1, C, HN), lambda i, t: (0, t[0, i], 0))
    bwd_d = pl.BlockSpec((1, C, HN), lambda i, t: (1, t[1, i], 0))
    n_seq = max(steps[2]) + 1
    st_in = pl.BlockSpec((1, 2, N, HN), lambda i, t: (jnp.maximum(t[2, i] - n_zero, 0), 0, 0, 0))
    st = pl.BlockSpec((1, 2, N, HN), lambda i, t: (t[2, i], 0, 0, 0))
    return pl.pallas_call(
        functools.partial(_rwkv_chunk_kernel, p_inv=p_inv, p_oth=p_oth, n_zero=n_zero),
        grid_spec=pltpu.PrefetchScalarGridSpec(
            num_scalar_prefetch=1,
            grid=(len(steps[0]),),
            in_specs=[fwd, fwd, fwd, bwd, bwd, bwd, fwd_d, fwd_d, fwd_d, bwd_d, bwd_d, bwd_d, st_in],
            out_specs=[fwd, bwd, st],
            scratch_shapes=[pltpu.VMEM((2, N, HN), jnp.float32)]),
        out_shape=[jax.ShapeDtypeStruct((R, HN), ACT_DTYPE), jax.ShapeDtypeStruct((R, HN), ACT_DTYPE),
                   jax.ShapeDtypeStruct((n_seq, 2, N, HN), jnp.float32)],
        compiler_params=pltpu.CompilerParams(dimension_semantics=("arbitrary",)),
        name="rwkv_scan",
    )(tbl, r, v, a, r, v, a, lw, k, b, lw, k, b, s0)


BLK = 256
FF_TILE = 768
SELECT_TILE = 512
SCATTER_WINDOW = 96
SLOT_ALIGN = 16
SELECT_MIN_EXP = -1100.0
SELECT_BINADE_STEPS = 11
SELECT_MANTISSA_STEPS = 40
MOE_FFN_VMEM_BYTES = 48 * 1024 * 1024


def _moe_select_kernel(aff_ref, slot_ref, *, cap):
    a = aff_ref[...]
    E, T = a.shape
    f32 = jnp.float32

    def enough(piv):
        return jnp.sum(jnp.where(a >= piv, 1.0, 0.0), axis=1, keepdims=True) >= cap

    def binade(_, lohi):
        e_lo, e_hi = lohi
        mid = jnp.floor((e_lo + e_hi) * 0.5)
        ok = enough(jnp.exp2(mid))
        return jnp.where(ok, mid, e_lo), jnp.where(ok, e_hi, mid)

    e_lo, e_hi = lax.fori_loop(0, SELECT_BINADE_STEPS, binade,
                               (jnp.full((E, 1), SELECT_MIN_EXP, f32), jnp.full((E, 1), 1.0, f32)))

    def inside(_, lohi):
        lo, hi = lohi
        mid = lo + (hi - lo) * 0.5
        ok = enough(mid)
        return jnp.where(ok, mid, lo), jnp.where(ok, hi, mid)

    thr, _ = lax.fori_loop(0, SELECT_MANTISSA_STEPS, inside, (jnp.exp2(e_lo), jnp.exp2(e_hi)))
    gt = a > thr
    eq = a == thr
    need = cap - jnp.sum(jnp.where(gt, 1.0, 0.0), axis=1, keepdims=True)
    tw = min(T, SELECT_TILE)

    def prefix_count(mask):
        m = jnp.where(mask, 1.0, 0.0).astype(jnp.bfloat16)
        outs = []
        for j in range(T // tw):
            s_i = lax.broadcasted_iota(jnp.int32, (T, tw), 0)
            t_i = lax.broadcasted_iota(jnp.int32, (T, tw), 1) + j * tw
            before = jnp.where(s_i < t_i, 1.0, 0.0).astype(jnp.bfloat16)
            outs.append(jnp.dot(m, before, preferred_element_type=f32))
        return outs[0] if len(outs) == 1 else jnp.concatenate(outs, axis=1)

    sel = gt | (eq & (prefix_count(eq) < need))
    slot_ref[...] = jnp.where(sel, prefix_count(sel).astype(jnp.int32), -1)


def _moe_select(affT, row0, n_seq, t):
    E = affT.shape[0]
    rows = jnp.transpose(affT[:, row0:row0 + n_seq * t].reshape(E, n_seq, t), (1, 0, 2)).reshape(n_seq * E, t)
    slot = pl.pallas_call(
        functools.partial(_moe_select_kernel, cap=EC_CAPACITY * t // N_EXPERTS),
        grid=(1,),
        in_specs=[pl.BlockSpec((n_seq * E, t), lambda s: (0, 0))],
        out_specs=pl.BlockSpec((n_seq * E, t), lambda s: (0, 0)),
        out_shape=jax.ShapeDtypeStruct((n_seq * E, t), jnp.int32),
        name="moe_select",
    )(rows)
    return jnp.transpose(slot.reshape(n_seq, E, t), (1, 0, 2)).reshape(E, n_seq * t)


def _moe_gather_kernel(tbl_ref, slot_ref, aff_ref, hn_ref, xe_ref, gate_ref, *, cap, win, nb):
    E = slot_ref.shape[0]
    s = pl.program_id(0)
    w_i = lax.broadcasted_iota(jnp.int32, (win, BLK), 0)
    xe_ref[...] = jnp.zeros_like(xe_ref)
    gate_ref[...] = jnp.zeros_like(gate_ref)
    for j in range(nb):
        blk = s * nb + j
        toks = slice(j * BLK, (j + 1) * BLK)
        hn_blk = hn_ref[toks, :]

        def window(w, carry):
            hits, starts = [], []
            for e in range(E):
                lo = tbl_ref[e, blk] + w * win
                start = pl.multiple_of(jnp.minimum(lo, cap - win), SLOT_ALIGN)
                slot = slot_ref[e, :, toks]
                slot = jnp.where(slot >= lo, slot, -1)
                hits.append(slot == w_i + start)
                starts.append(start)
            onehot = jnp.concatenate([jnp.where(h, 1.0, 0.0) for h in hits], axis=0).astype(jnp.bfloat16)
            rows = jnp.dot(onehot, hn_blk, preferred_element_type=jnp.float32)
            for e in range(E):
                dst = pl.ds(starts[e], win)
                xe_ref[e, dst, :] += rows[e * win:(e + 1) * win].astype(xe_ref.dtype)
                g = jnp.sum(jnp.where(hits[e], aff_ref[e, :, toks], 0.0), axis=1, keepdims=True)
                gate_ref[e, dst, :] += jnp.broadcast_to(g, (win, 128))
            return carry

        lax.fori_loop(0, tbl_ref[E, blk], window, 0)


def _moe_gather(slot, slot3, aff3, hn, row0, n_seq, t):
    E = slot3.shape[0]
    D = hn.shape[1]
    cap = EC_CAPACITY * t // N_EXPERTS
    win = min(cap, SCATTER_WINDOW)
    nb = t // BLK
    b0 = row0 // t
    tbl = _scatter_windows(slot, row0, n_seq, t, win)
    return pl.pallas_call(
        functools.partial(_moe_gather_kernel, cap=cap, win=win, nb=nb),
        grid_spec=pltpu.PrefetchScalarGridSpec(
            num_scalar_prefetch=1, grid=(n_seq,),
            in_specs=[pl.BlockSpec((E, 1, t), lambda s, tb_: (0, 0, b0 + s)),
                      pl.BlockSpec((E, 1, t), lambda s, tb_: (0, 0, b0 + s)),
                      pl.BlockSpec((t, D), lambda s, tb_: (b0 + s, 0))],
            out_specs=[pl.BlockSpec((E, cap, D), lambda s, tb_: (0, s, 0)),
                       pl.BlockSpec((E, cap, 128), lambda s, tb_: (0, s, 0))]),
        out_shape=[jax.ShapeDtypeStruct((E, n_seq * cap, D), jnp.bfloat16),
                   jax.ShapeDtypeStruct((E, n_seq * cap, 128), jnp.float32)],
        compiler_params=pltpu.CompilerParams(dimension_semantics=("arbitrary",),
                                             vmem_limit_bytes=MOE_FFN_VMEM_BYTES),
        name="moe_gather",
    )(tbl, slot3, aff3, hn)


def _moe_ffn_kernel(xc_ref, xl_ref, gc_ref, gl_ref, wg_ref, wu_ref, wd_ref, yc_ref, yl_ref, acc_ref, *, nf):
    f = pl.program_id(1)
    bf16 = jnp.bfloat16
    wg = wg_ref[0, 0].astype(bf16)
    wu = wu_ref[0, 0].astype(bf16)
    wd = wd_ref[0, 0].astype(bf16)
    nc = xc_ref.shape[1]
    x = jnp.concatenate([xc_ref[0], xl_ref[0]], axis=0)
    g = jnp.dot(x, wg, preferred_element_type=jnp.float32)
    u = jnp.dot(x, wu, preferred_element_type=jnp.float32)
    h = (g * jax.nn.sigmoid(g) * u).astype(bf16)
    y = jnp.dot(h, wd, preferred_element_type=jnp.float32)

    if nf > 1:
        @pl.when(f == 0)
        def _():
            acc_ref[...] = y

        @pl.when((f != 0) & (f != nf - 1))
        def _():
            acc_ref[...] += y

    @pl.when(f == nf - 1)
    def _():
        total = y if nf == 1 else acc_ref[...] + y
        gate = jnp.concatenate([gc_ref[0], gl_ref[0]], axis=0)
        out = total * jnp.concatenate([gate] * (total.shape[1] // 128), axis=1)
        yc_ref[0] = out[:nc].astype(yc_ref.dtype)
        yl_ref[0] = out[nc:].astype(yl_ref.dtype)


def _moe_ffn(xc, xl, gc, gl, wg, wu, wd, layer):
    E, nc_rows, D = xc.shape
    nl_rows = xl.shape[1]
    F = wg.shape[3]
    nf = F // FF_TILE
    return pl.pallas_call(
        functools.partial(_moe_ffn_kernel, nf=nf),
        grid=(E, nf),
        in_specs=[pl.BlockSpec((1, nc_rows, D), lambda e, f: (e, 0, 0)),
                  pl.BlockSpec((1, nl_rows, D), lambda e, f: (e, 0, 0)),
                  pl.BlockSpec((1, nc_rows, 128), lambda e, f: (e, 0, 0)),
                  pl.BlockSpec((1, nl_rows, 128), lambda e, f: (e, 0, 0)),
                  pl.BlockSpec((1, 1, D, FF_TILE), lambda e, f: (layer, e, 0, f)),
                  pl.BlockSpec((1, 1, D, FF_TILE), lambda e, f: (layer, e, 0, f)),
                  pl.BlockSpec((1, 1, FF_TILE, D), lambda e, f: (layer, e, f, 0))],
        out_specs=[pl.BlockSpec((1, nc_rows, D), lambda e, f: (e, 0, 0)),
                   pl.BlockSpec((1, nl_rows, D), lambda e, f: (e, 0, 0))],
        out_shape=[jax.ShapeDtypeStruct((E, nc_rows, D), jnp.bfloat16),
                   jax.ShapeDtypeStruct((E, nl_rows, D), jnp.bfloat16)],
        scratch_shapes=[pltpu.VMEM((nc_rows + nl_rows, D), jnp.float32)],
        compiler_params=pltpu.CompilerParams(dimension_semantics=("arbitrary", "arbitrary"),
                                             vmem_limit_bytes=MOE_FFN_VMEM_BYTES),
        name="moe_ffn",
    )(xc, xl, gc, gl, wg, wu, wd)


def _moe_scatter_kernel(tbl_ref, slot_ref, ye_ref, x_ref, g2_ref, *rest, cap, win, nb, final):
    if final:
        fg_ref, o_ref, acc_ref = rest
    else:
        o_ref, acc_ref = rest
    E = ye_ref.shape[0]
    tb = x_ref.shape[0]
    blk = pl.program_id(0) * nb + pl.program_id(1)
    w_i = lax.broadcasted_iota(jnp.int32, (win, tb), 0)
    acc_ref[...] = jnp.zeros_like(acc_ref)

    def window(w, carry):
        hot, rows = [], []
        for e in range(E):
            lo = tbl_ref[e, blk] + w * win
            start = pl.multiple_of(jnp.minimum(lo, cap - win), SLOT_ALIGN)
            slot = slot_ref[e]
            slot = jnp.where(slot >= lo, slot, -1)
            hot.append(jnp.where(slot == w_i + start, 1.0, 0.0))
            rows.append(ye_ref[e, pl.ds(start, win), :])
        onehot = jnp.concatenate(hot, axis=0).astype(jnp.bfloat16)
        acc_ref[...] += lax.dot_general(onehot, jnp.concatenate(rows, axis=0), (((0,), (0,)), ((), ())),
                                        preferred_element_type=jnp.float32)
        return carry

    lax.fori_loop(0, tbl_ref[E, blk], window, 0)
    x = x_ref[...] + g2_ref[0] * acc_ref[...]
    if final:
        x = x * lax.rsqrt(jnp.mean(x * x, -1, keepdims=True) + NORM_EPS) * fg_ref[...]
    o_ref[...] = x


def _scatter_windows(slot, row0, n_seq, t, win):
    E = slot.shape[0]
    cap = EC_CAPACITY * t // N_EXPERTS
    nb = t // BLK
    cnt = jnp.sum(slot[:, row0:row0 + n_seq * t].reshape(E, n_seq, nb, BLK) >= 0, axis=3)
    first = jnp.cumsum(cnt, axis=2) - cnt
    start = jnp.minimum(first // SLOT_ALIGN * SLOT_ALIGN, cap - win)
    n_win = jnp.maximum(jnp.max((first + cnt - start + win - 1) // win, axis=0), 1)
    return jnp.concatenate([start.reshape(E, n_seq * nb), n_win.reshape(1, n_seq * nb)]).astype(jnp.int32)


def _moe_scatter(slot, slot3, ye, x, g2blk, row0, n_seq, t, final_g=None):
    E, _, D = ye.shape
    cap = EC_CAPACITY * t // N_EXPERTS
    win = min(cap, SCATTER_WINDOW)
    nb = t // BLK
    b0 = row0 // BLK
    final = final_g is not None
    tbl = _scatter_windows(slot, row0, n_seq, t, win)
    in_specs = [pl.BlockSpec((E, 1, BLK), lambda s, j, tb_: (0, 0, b0 + s * nb + j)),
                pl.BlockSpec((E, cap, D), lambda s, j, tb_: (0, s, 0)),
                pl.BlockSpec((BLK, D), lambda s, j, tb_: (b0 + s * nb + j, 0)),
                pl.BlockSpec((1, 1, D), lambda s, j, tb_: (b0 + s * nb + j, 0, 0))]
    args = [tbl, slot3, ye, x, g2blk]
    if final:
        in_specs.append(pl.BlockSpec((1, D), lambda s, j, tb_: (0, 0)))
        args.append(final_g.reshape(1, D))
        out_specs = pl.BlockSpec((BLK, D), lambda s, j, tb_: (s * nb + j, 0))
        out_shape = jax.ShapeDtypeStruct((n_seq * t, D), jnp.float32)
        aliases = {}
    else:
        out_specs = pl.BlockSpec((BLK, D), lambda s, j, tb_: (b0 + s * nb + j, 0))
        out_shape = jax.ShapeDtypeStruct(x.shape, jnp.float32)
        aliases = {3: 0}
    return pl.pallas_call(
        functools.partial(_moe_scatter_kernel, cap=cap, win=win, nb=nb, final=final),
        grid_spec=pltpu.PrefetchScalarGridSpec(
            num_scalar_prefetch=1, grid=(n_seq, nb), in_specs=in_specs, out_specs=out_specs,
            scratch_shapes=[pltpu.VMEM((BLK, D), jnp.float32)]),
        out_shape=out_shape,
        input_output_aliases=aliases,
        compiler_params=pltpu.CompilerParams(dimension_semantics=("arbitrary", "arbitrary"),
                                             vmem_limit_bytes=MOE_FFN_VMEM_BYTES),
        name="moe_scatter",
    )(*args)


def moe_layer(x, hn, affT, g2blk, wg, wu, wd, layer, n_ctx, l_lat, final_g=None):
    R = x.shape[0]
    r_ctx = n_ctx * BLK
    n_lat = (R - r_ctx) // l_lat
    slot = jnp.concatenate([_moe_select(affT, 0, n_ctx, BLK), _moe_select(affT, r_ctx, n_lat, l_lat)], axis=1)
    slot3 = slot[:, None, :]
    aff3 = affT[:, None, :]
    xc, gc = _moe_gather(slot, slot3, aff3, hn, 0, n_ctx, BLK)
    xl, gl = _moe_gather(slot, slot3, aff3, hn, r_ctx, n_lat, l_lat)
    yc, yl = _moe_ffn(xc, xl, gc, gl, wg, wu, wd, layer)
    if final_g is None:
        x = _moe_scatter(slot, slot3, yc, x, g2blk, 0, n_ctx, BLK)
        return _moe_scatter(slot, slot3, yl, x, g2blk, r_ctx, n_lat, l_lat)
    return (_moe_scatter(slot, slot3, yc, x, g2blk, 0, n_ctx, BLK, final_g),
            _moe_scatter(slot, slot3, yl, x, g2blk, r_ctx, n_lat, l_lat, final_g))


L0_Z = (0, 1024)
L0_XBC = (1024, 2560)
L0_SHIFT = (2560, 6016)
L0_DT = (6016, 6144)
L0_COLS = 6144
L0_VMEM_BYTES = 56 * 1024 * 1024


def _sum_split(x, m, n_split):
    acc = None
    for _ in range(n_split):
        hi = x.astype(jnp.bfloat16)
        x = x - hi.astype(jnp.float32)
        t = jnp.dot(hi, m, preferred_element_type=jnp.float32)
        acc = t if acc is None else acc + t
    return acc


def _sum_split_left(m, x, n_split):
    acc = None
    for _ in range(n_split):
        hi = x.astype(jnp.bfloat16)
        x = x - hi.astype(jnp.float32)
        t = jnp.dot(m, hi, preferred_element_type=jnp.float32)
        acc = t if acc is None else acc + t
    return acc


def _head_sum(x, e_ref, et_ref):
    return _sum_split(_sum_split(x, e_ref[...], 2), et_ref[...], 2)


def _adaln(x, g, sc, sh):
    y = x * lax.rsqrt(jnp.mean(x * x, -1, keepdims=True) + NORM_EPS) * g
    return y * (1.0 + sc) + sh


def _softplus(x):
    return jnp.maximum(x, 0.0) + jnp.log(1.0 + jnp.exp(-jnp.abs(x)))


def _l0_in_kernel(tbl_ref, *refs, n_parts, n_a):
    x_refs, xp_refs, xn_refs = refs[:n_parts], refs[n_parts:2 * n_parts], refs[2 * n_parts:3 * n_parts]
    (g_ref, sc_ref, sh_ref, w_ref, mup_ref, mun_ref, rwp_ref, w2_ref, a2_ref, g2_ref, e_ref, et_ref,
     z_ref, xbc_ref, dt_ref, r_ref, v_ref, an_ref, lw_ref, kd_ref, bv_ref, gate_ref, bonus_ref) = refs[3 * n_parts:]
    i = pl.program_id(0)
    f32, bf16 = jnp.float32, jnp.bfloat16
    g, sc, sh = g_ref[...], sc_ref[0], sh_ref[0]
    hn = _adaln(_residual_rows(x_refs, n_a), g, sc, sh).astype(bf16)
    halo = _adaln(jnp.concatenate([_residual_rows(xp_refs, n_a), _residual_rows(xn_refs, n_a)], axis=0),
                  g, sc, sh).astype(bf16)
    hn_halo = jnp.concatenate([hn, halo], axis=0)

    keep_prev = (1 - tbl_ref[0, i]).astype(f32)
    keep_next = (1 - tbl_ref[1, i]).astype(f32)
    row = lax.broadcasted_iota(jnp.int32, (BLK, 1), 0)
    c = RWKV_DIM

    def plain(c0, c1):
        return jnp.dot(hn, w_ref[:, c0:c1], preferred_element_type=f32)

    def proj(c0, c1):
        return jnp.dot(hn_halo, w_ref[:, L0_SHIFT[0] + c0:L0_SHIFT[0] + c1], preferred_element_type=f32)

    def shift(both, c0, c1):
        cur = both[:BLK]
        prev = jnp.where(row == 0, both[BLK + 7:BLK + 8] * keep_prev, pltpu.roll(cur, 1, 0))
        nxt = jnp.where(row == BLK - 1, both[BLK + 8:BLK + 9] * keep_next, pltpu.roll(cur, BLK - 1, 0))
        return cur + mup_ref[:, c0:c1] * (prev - cur) + mun_ref[:, c0:c1] * (nxt - cur)

    lo = 3 * c + 2 * W_LORA
    p_wl = proj(3 * c, lo)
    p_ag = proj(lo, lo + 256)
    p_k = proj(c, 2 * c)
    wl = shift(p_wl, 3 * c, lo)
    ag = shift(p_ag, lo, lo + 256)
    p_r = proj(0, c)
    k = shift(p_k, c, 2 * c)
    k_k, k_a, r_k = rwp_ref[0:1], rwp_ref[1:2], rwp_ref[2:3]
    w_lin = jnp.dot(jnp.tanh(wl).astype(bf16), w2_ref[...], preferred_element_type=f32)
    a_lora = jnp.dot(ag.astype(bf16), a2_ref[...], preferred_element_type=f32)
    gate_ref[...] = jnp.dot(jax.nn.sigmoid(ag).astype(bf16), g2_ref[...],
                            preferred_element_type=f32).astype(gate_ref.dtype)
    p_v = proj(2 * c, 3 * c)
    r = shift(p_r, 0, c)
    r_ref[...] = r.astype(r_ref.dtype)
    kk = k * k_k
    kk = kk * lax.rsqrt(_head_sum(kk * kk, e_ref, et_ref) + 1e-12)
    an_ref[...] = (-kk).astype(an_ref.dtype)
    z_ref[...] = plain(L0_Z[0], L0_Z[1]).astype(z_ref.dtype)
    v = shift(p_v, 2 * c, 3 * c)
    v_ref[...] = v.astype(v_ref.dtype)
    kd_sum = None
    xbc_cols = (L0_XBC[0], (L0_XBC[0] + L0_XBC[1]) // 2, L0_XBC[1])
    for d in range(2):
        xbc_ref[:, xbc_cols[d] - L0_XBC[0]:xbc_cols[d + 1] - L0_XBC[0]] = plain(xbc_cols[d], xbc_cols[d + 1])
        w_log = -_softplus(-(rwp_ref[3 + d:4 + d] + w_lin[:, d * c:(d + 1) * c])) - 0.5
        lw_ref[d] = -jnp.exp(w_log)
        a = jax.nn.sigmoid(rwp_ref[5 + d:6 + d] + a_lora)
        kd = k * (1.0 + (a - 1.0) * k_a)
        kd_ref[d] = kd.astype(kd_ref.dtype)
        bv_ref[d] = (kk * a).astype(bv_ref.dtype)
        kd_sum = kd if kd_sum is None else kd_sum + kd
    dt_ref[...] = plain(L0_DT[0], L0_DT[1])
    bonus_ref[...] = (_head_sum(r * kd_sum * r_k, e_ref, et_ref) * v).astype(bonus_ref.dtype)


def _seq_tables(n_ctx, n_lat, l_lat):
    nb = l_lat // BLK
    cond = [0] * n_ctx + [1 + s for s in range(n_lat) for _ in range(nb)]
    first = [1] * n_ctx + [1 if j == 0 else 0 for _ in range(n_lat) for j in range(nb)]
    last = [1] * n_ctx + [1 if j == nb - 1 else 0 for _ in range(n_lat) for j in range(nb)]
    return cond, first, last


def l0_in(x, g1, scb, shb, w_packed, mup, mun, rwp, w2bd, a2p, g2p, e_ind, et_ind, first, last):
    xs = x if isinstance(x, (tuple, list)) else (x,)
    D = xs[0].shape[1]
    R = sum(a.shape[0] for a in xs)
    nblk = R // BLK
    n_a = xs[0].shape[0] // BLK
    tbl = jnp.asarray([first, last], jnp.int32)
    c = RWKV_DIM
    row = lambda i, t: (i, 0)
    full = lambda shape: pl.BlockSpec(shape, lambda i, t: (0,) * len(shape))
    rows = lambda n: pl.BlockSpec((BLK, n), row)
    rows2 = lambda n: pl.BlockSpec((2, BLK, n), lambda i, t: (0, i, 0))
    f32, act = jnp.float32, ACT_DTYPE
    sds = jax.ShapeDtypeStruct
    h8 = BLK // 8

    def part_specs(k):
        nb_k, b0 = xs[k].shape[0] // BLK, (0 if k == 0 else n_a)
        local = lambda i: jnp.clip(i - b0, 0, nb_k - 1)
        return (pl.BlockSpec((BLK, D), lambda i, t: (local(i), 0)),
                pl.BlockSpec((8, D), lambda i, t: (jnp.maximum(local(i) * h8 - 1, 0), 0)),
                pl.BlockSpec((8, D), lambda i, t: (jnp.minimum((local(i) + 1) * h8, nb_k * h8 - 1), 0)))

    specs = [part_specs(k) for k in range(len(xs))]
    x_specs = [s[j] for j in range(3) for s in specs]
    return pl.pallas_call(
        functools.partial(_l0_in_kernel, n_parts=len(xs), n_a=n_a),
        grid_spec=pltpu.PrefetchScalarGridSpec(
            num_scalar_prefetch=1,
            grid=(nblk,),
            in_specs=[*x_specs,
                      full((1, D)),
                      pl.BlockSpec((1, 1, D), lambda i, t: (i, 0, 0)),
                      pl.BlockSpec((1, 1, D), lambda i, t: (i, 0, 0)),
                      full((D, L0_COLS)), full(mup.shape), full(mun.shape), full(rwp.shape),
                      full(w2bd.shape), full(a2p.shape), full(g2p.shape), full(e_ind.shape), full(et_ind.shape)],
            out_specs=[rows(c), rows(SSD_XBC), rows(128), rows(c), rows(c), rows(c),
                       rows2(c), rows2(c), rows2(c), rows(c), rows(c)]),
        out_shape=[sds((R, c), act), sds((R, SSD_XBC), f32), sds((R, 128), f32), sds((R, c), act), sds((R, c), act),
                   sds((R, c), act), sds((2, R, c), f32), sds((2, R, c), act), sds((2, R, c), act),
                   sds((R, c), act), sds((R, c), act)],
        compiler_params=pltpu.CompilerParams(dimension_semantics=("arbitrary",), vmem_limit_bytes=L0_VMEM_BYTES),
        name="l0_in",
    )(tbl, *xs, *xs, *xs, g1, scb, shb, w_packed, mup, mun, rwp, w2bd, a2p, g2p, e_ind, et_ind)


def l0_pack_weights(p, e):
    bf16 = jnp.bfloat16
    w = p['ab_w_in'][e]
    D = w.shape[0]
    c = RWKV_DIM
    rw0 = SSD_IN
    ag0 = rw0 + 3 * c + 2 * W_LORA
    w_packed = jnp.concatenate([
        w[:, :SSD_INNER + SSD_XBC], w[:, rw0:ag0], w[:, ag0:ag0 + A_LORA + G_LORA],
        jnp.zeros((D, 256 - A_LORA - G_LORA), w.dtype),
        w[:, SSD_INNER + SSD_XBC:SSD_IN], jnp.zeros((D, 128 - SSD_HEADS), w.dtype)], axis=1).astype(bf16)

    def pack_mu(mu):
        return jnp.concatenate([mu, jnp.zeros((256 - A_LORA - G_LORA,), mu.dtype)])[None, :]

    rwp = jnp.stack([p['rwkv_k_k'][e], p['rwkv_k_a'][e], p['rwkv_r_k'][e].reshape(-1), p['rwkv_w0'][e, 0],
                     p['rwkv_w0'][e, 1], p['rwkv_a0'][e, 0], p['rwkv_a0'][e, 1], jnp.zeros((c,), jnp.float32)])
    zw = jnp.zeros((W_LORA, c), jnp.float32)
    w2bd = jnp.concatenate([jnp.concatenate([p['rwkv_w2'][e, 0], zw], axis=1),
                            jnp.concatenate([zw, p['rwkv_w2'][e, 1]], axis=1)], axis=0).astype(bf16)
    a2p = jnp.concatenate([p['rwkv_a2'][e], jnp.zeros((256 - A_LORA, c), jnp.float32)], axis=0).astype(bf16)
    g2p = jnp.concatenate([jnp.zeros((A_LORA, c), jnp.float32), p['rwkv_g2'][e],
                           jnp.zeros((256 - A_LORA - G_LORA, c), jnp.float32)], axis=0).astype(bf16)
    head = jnp.arange(c) // RWKV_N
    e_ind = (head[:, None] == jnp.arange(128)[None, :]).astype(bf16)
    return w_packed, pack_mu(p['rwkv_mu_prev'][e]), pack_mu(p['rwkv_mu_next'][e]), rwp, w2bd, a2p, g2p, e_ind, e_ind.T


SSD_QH = 4
SSD_VMEM_BYTES = 48 * 1024 * 1024
NEG_BIG = -1e30
LOG2E = 1.4426950408889634


def _conv_silu(cur, prev_row, next_row, w_ref, b_ref):
    row = lax.broadcasted_iota(jnp.int32, (BLK, 1), 0)
    prev = jnp.where(row == 0, prev_row, pltpu.roll(cur, 1, 0))
    nxt = jnp.where(row == BLK - 1, next_row, pltpu.roll(cur, BLK - 1, 0))
    y = w_ref[0:1] * prev + w_ref[1:2] * cur + w_ref[2:3] * nxt + b_ref[...]
    return y * jax.nn.sigmoid(y)


def _ssd_kernel(tbl_ref, xs_ref, b_ref, c_ref, dt_ref, cwx_ref, cwb_ref, cwc_ref, cbx_ref, cbb_ref, cbc_ref,
                sel_ref, hp_ref, s0_ref, y_ref, fs_ref, xa_ref, ba_ref, ca_ref, sfx_ref, ldb_ref, st_ref, *, n_zero):
    f32, bf16 = jnp.float32, jnp.bfloat16
    sb = pl.program_id(0)
    sbr, qw = xs_ref.shape
    nch = sbr // BLK
    P, QH = SSD_P, SSD_QH
    t_i = lax.broadcasted_iota(jnp.int32, (BLK, BLK), 0)
    s_i = lax.broadcasted_iota(jnp.int32, (BLK, BLK), 1)
    lower = s_i <= t_i
    upper = s_i >= t_i
    tri_lo = jnp.where(lower, 1.0, 0.0).astype(bf16)
    tri_up = jnp.where(upper, 1.0, 0.0).astype(bf16)
    hp = hp_ref[0]
    sel = sel_ref[0]
    ind = jnp.where(lax.broadcasted_iota(jnp.int32, (128, qw), 1) // P == lax.broadcasted_iota(jnp.int32, (128, qw), 0),
                    1.0, 0.0).astype(bf16)
    head_of_lane = lax.broadcasted_iota(jnp.int32, (1, qw), 1) // P

    def expand(cols):
        return _sum_split(cols, ind, 2)

    d_row = expand(jnp.broadcast_to(hp[4:5], (8, 128)))[0:1]

    def chunk_rows(c):
        return pl.ds(pl.multiple_of(c * BLK, BLK), BLK)

    def neighbours(ref, c, keep_prev, keep_next):
        lo = jnp.maximum(c * BLK - 1, 0)
        hi = jnp.minimum((c + 1) * BLK, sbr - 1)
        return ref[pl.ds(lo, 1), :] * keep_prev, ref[pl.ds(hi, 1), :] * keep_next

    def fwd(c, carry):
        blk = sb * nch + c
        first, last = tbl_ref[0, blk], tbl_ref[1, blk]
        kp, kn = (1 - first).astype(f32), (1 - last).astype(f32)
        rows = chunk_rows(c)

        @pl.when(first == 1)
        def _():
            st_ref[0] = jnp.where(sb >= n_zero, s0_ref[0, 0], 0.0)

        xa = _conv_silu(xs_ref[rows, :], *neighbours(xs_ref, c, kp, kn), cwx_ref, cbx_ref)
        bm = _conv_silu(b_ref[rows, :], *neighbours(b_ref, c, kp, kn), cwb_ref, cbb_ref)
        cm = _conv_silu(c_ref[rows, :], *neighbours(c_ref, c, kp, kn), cwc_ref, cbc_ref)
        xb, bmb, cmb = xa.astype(bf16), bm.astype(bf16), cm.astype(bf16)
        xa_ref[rows, :] = xb
        ba_ref[rows, :] = bmb
        ca_ref[rows, :] = cmb
        dtq = _sum_split(dt_ref[rows, :], sel, 2)
        dtf = _softplus(dtq + hp[0:1])
        dtb = _softplus(dtq + hp[1:2])
        acs = _sum_split_left(tri_lo, dtf * hp[2:3], 3)
        sfx = _sum_split_left(tri_up, dtb * hp[3:4], 3)
        ldf, ldb = jnp.log(dtf), jnp.log(dtb)
        sfx_ref[rows, :] = sfx
        ldb_ref[rows, :] = ldb
        a2, s2 = acs * LOG2E, sfx * LOG2E
        a2r = (a2 - ldf * LOG2E).T
        s2r = (s2 - ldb * LOG2E).T
        g = lax.dot_general(cmb, bmb, (((1,), (1,)), ((), ())), preferred_element_type=f32)
        y_diag = None
        for j in range(QH):
            m = (g * (jnp.exp2(jnp.where(lower, a2[:, j:j + 1] - a2r[j:j + 1, :], NEG_BIG))
                      + jnp.exp2(jnp.where(upper, s2[:, j:j + 1] - s2r[j:j + 1, :], NEG_BIG)))).astype(bf16)
            xh = jnp.where(head_of_lane == j, xb, jnp.zeros((), bf16))
            t = jnp.dot(m, xh, preferred_element_type=f32)
            y_diag = t if y_diag is None else y_diag + t
        ea = jnp.exp(expand(acs))
        wf = jnp.exp(expand(acs[BLK - 1:BLK] - acs + ldf))
        s_in = st_ref[0]
        y_ref[rows, :] = xa * d_row + y_diag + ea * jnp.dot(cmb, s_in.astype(bf16), preferred_element_type=f32)
        st_ref[0] = ea[BLK - 1:BLK] * s_in + lax.dot_general(
            bmb, (xa * wf).astype(bf16), (((0,), (0,)), ((), ())), preferred_element_type=f32)
        fs_ref[c, 0] = st_ref[0]
        return carry

    lax.fori_loop(0, nch, fwd, 0)

    def bwd(k, carry):
        c = nch - 1 - k
        blk = sb * nch + c
        rows = chunk_rows(c)

        @pl.when(tbl_ref[1, blk] == 1)
        def _():
            st_ref[1] = jnp.where(sb >= n_zero, s0_ref[0, 1], 0.0)

        sfx = sfx_ref[rows, :]
        eb = jnp.exp(expand(sfx))
        wb = jnp.exp(expand(sfx[0:1] - sfx + ldb_ref[rows, :]))
        s_in = st_ref[1]
        y_ref[rows, :] += eb * jnp.dot(ca_ref[rows, :], s_in.astype(bf16), preferred_element_type=f32)
        st_ref[1] = eb[0:1] * s_in + lax.dot_general(
            ba_ref[rows, :], (xa_ref[rows, :].astype(f32) * wb).astype(bf16), (((0,), (0,)), ((), ())),
            preferred_element_type=f32)
        fs_ref[c, 1] = st_ref[1]
        return carry

    lax.fori_loop(0, nch, bwd, 0)


def ssd_scan(xbc, dt, conv_w, conv_b, sel, hp, s0, n_zero, first, last, sb_rows):
    R = xbc.shape[0]
    n_sb = R // sb_rows
    nch = sb_rows // BLK
    nq = SSD_HEADS // SSD_QH
    qw = SSD_QH * SSD_P
    qpg = SSD_HPG // SSD_QH
    b_blk = SSD_INNER // SSD_N
    c_blk = b_blk + SSD_GROUPS
    tbl = jnp.asarray([first, last], jnp.int32)
    cw = conv_w
    cb = conv_b.reshape(1, -1)
    f32, bf16 = jnp.float32, jnp.bfloat16
    return pl.pallas_call(
        functools.partial(_ssd_kernel, n_zero=n_zero),
        grid_spec=pltpu.PrefetchScalarGridSpec(
            num_scalar_prefetch=1,
            grid=(n_sb, nq),
            in_specs=[pl.BlockSpec((sb_rows, qw), lambda s, q, t: (s, q)),
                      pl.BlockSpec((sb_rows, SSD_N), lambda s, q, t: (s, b_blk + q // qpg)),
                      pl.BlockSpec((sb_rows, SSD_N), lambda s, q, t: (s, c_blk + q // qpg)),
                      pl.BlockSpec((sb_rows, 128), lambda s, q, t: (s, 0)),
                      pl.BlockSpec((3, qw), lambda s, q, t: (0, q)),
                      pl.BlockSpec((3, SSD_N), lambda s, q, t: (0, b_blk + q // qpg)),
                      pl.BlockSpec((3, SSD_N), lambda s, q, t: (0, c_blk + q // qpg)),
                      pl.BlockSpec((1, qw), lambda s, q, t: (0, q)),
                      pl.BlockSpec((1, SSD_N), lambda s, q, t: (0, b_blk + q // qpg)),
                      pl.BlockSpec((1, SSD_N), lambda s, q, t: (0, c_blk + q // qpg)),
                      pl.BlockSpec((1, 128, 128), lambda s, q, t: (q, 0, 0)),
                      pl.BlockSpec((1, 8, 128), lambda s, q, t: (q, 0, 0)),
                      pl.BlockSpec((1, 2, SSD_N, qw), lambda s, q, t: (jnp.maximum(s - n_zero, 0), 0, 0, q))],
            out_specs=[pl.BlockSpec((sb_rows, qw), lambda s, q, t: (s, q)),
                       pl.BlockSpec((nch, 2, SSD_N, qw), lambda s, q, t: (s, 0, 0, q))],
            scratch_shapes=[pltpu.VMEM((sb_rows, qw), bf16), pltpu.VMEM((sb_rows, SSD_N), bf16),
                            pltpu.VMEM((sb_rows, SSD_N), bf16), pltpu.VMEM((sb_rows, 128), f32),
                            pltpu.VMEM((sb_rows, 128), f32), pltpu.VMEM((2, SSD_N, qw), f32)]),
        out_shape=[jax.ShapeDtypeStruct((R, SSD_INNER), f32),
                   jax.ShapeDtypeStruct((R // BLK, 2, SSD_N, SSD_INNER), f32)],
        compiler_params=pltpu.CompilerParams(dimension_semantics=("arbitrary", "arbitrary"),
                                             vmem_limit_bytes=SSD_VMEM_BYTES),
        name="ssd_scan",
    )(tbl, xbc, xbc, xbc, dt, cw, cw, cw, cb, cb, cb, sel, hp, s0)


def ssd_tables(p, e):
    nq = SSD_HEADS // SSD_QH
    lane = jnp.arange(128)
    sel = jnp.stack([(lane[:, None] == (q * SSD_QH + lane[None, :])) & (lane[None, :] < SSD_QH)
                     for q in range(nq)]).astype(jnp.bfloat16)
    a_neg = -jnp.exp(p['ssd_a_log'][e].astype(jnp.float32))
    rows = jnp.stack([p['ssd_dt_bias'][e, 0], p['ssd_dt_bias'][e, 1], a_neg[0], a_neg[1], p['ssd_d'][e]])
    hp = jnp.zeros((nq, 8, 128), jnp.float32)
    hp = hp.at[:, :5, :SSD_QH].set(jnp.transpose(rows.reshape(5, nq, SSD_QH), (1, 0, 2)))
    return sel, hp


MIX_VMEM_BYTES = 48 * 1024 * 1024
MIX_BLOCKS = 1
ROUTER_LANES = 128


def _residual_norm_router(x, out, g1, n2g, sc2, sh2, rw_ref, x_out_ref, hn_ref, aff_ref):
    rows, d = x.shape
    nb = g1.shape[0]
    x_new = x.reshape(nb, rows // nb, d) + g1 * out.reshape(nb, rows // nb, d)
    x_out_ref[...] = x_new.reshape(rows, d)
    hn = _adaln(x_new, n2g, sc2, sh2).reshape(rows, d)
    hn_ref[...] = hn.astype(hn_ref.dtype)
    logits = _dot(hn, rw_ref[...], ((1,), (0,)), 3)
    lane = lax.broadcasted_iota(jnp.int32, logits.shape, 1)
    logits = jnp.where(lane < N_EXPERTS, logits, NEG_BIG)
    ex = jnp.exp(logits - jnp.max(logits, axis=-1, keepdims=True))
    aff = ex / jnp.sum(ex, axis=-1, keepdims=True)
    aff_ref[...] = aff.T[:N_EXPERTS]


def _residual_rows(x_refs, n_a):
    if len(x_refs) == 1:
        return x_refs[0][...]
    return jnp.where(pl.program_id(0) < n_a, x_refs[0][...], x_refs[1][...])


def _l0_out_kernel(ys_ref, z_ref, yf_ref, yb_ref, bonus_ref, gate_ref, sg_ref, lnw_ref, lnb_ref, e_ref, et_ref, w_ref,
                   *rest, n_a):
    *x_refs, g1_ref, n2g_ref, sc2_ref, sh2_ref, rw_ref, x_out_ref, hn_ref, aff_ref = rest
    f32, bf16 = jnp.float32, jnp.bfloat16
    z = z_ref[...].astype(f32)
    ys = ys_ref[...] * (z * jax.nn.sigmoid(z))
    gw = SSD_INNER // SSD_GROUPS
    parts = []
    for gi in range(SSD_GROUPS):
        yg = ys[:, gi * gw:(gi + 1) * gw]
        parts.append(yg * lax.rsqrt(jnp.mean(yg * yg, -1, keepdims=True) + NORM_EPS))
    a1 = jnp.concatenate(parts, axis=1) * sg_ref[...]
    o = yf_ref[...].astype(f32) + yb_ref[...].astype(f32)
    mu = _head_sum(o, e_ref, et_ref) * (1.0 / RWKV_N)
    oc = o - mu
    var = _head_sum(oc * oc, e_ref, et_ref) * (1.0 / RWKV_N)
    o = oc * lax.rsqrt(var + RWKV_GN_EPS) * lnw_ref[...] + lnb_ref[...]
    o = (o + bonus_ref[...].astype(f32)) * gate_ref[...].astype(f32)
    out = (jnp.dot(a1.astype(bf16), w_ref[:SSD_INNER], preferred_element_type=f32)
           + jnp.dot(o.astype(bf16), w_ref[SSD_INNER:], preferred_element_type=f32))
    _residual_norm_router(_residual_rows(x_refs, n_a), out, g1_ref[...], n2g_ref[...], sc2_ref[...], sh2_ref[...],
                          rw_ref, x_out_ref, hn_ref, aff_ref)


def _l1_out_kernel(a_ref, w_ref, *rest, n_a):
    *x_refs, g1_ref, n2g_ref, sc2_ref, sh2_ref, rw_ref, x_out_ref, hn_ref, aff_ref = rest
    out = jnp.dot(a_ref[...].astype(jnp.bfloat16), w_ref[...], preferred_element_type=jnp.float32)
    _residual_norm_router(_residual_rows(x_refs, n_a), out, g1_ref[...], n2g_ref[...], sc2_ref[...], sh2_ref[...],
                          rw_ref, x_out_ref, hn_ref, aff_ref)


def _mix_out_call(kernel_fn, name, lead_args, lead_specs, w_out, x, g1b, n2g, sc2b, sh2b, router_w):
    mb = MIX_BLOCKS
    xs = x if isinstance(x, (tuple, list)) else (x,)
    D = xs[0].shape[1]
    R = sum(a.shape[0] for a in xs)
    n_a = xs[0].shape[0] // (mb * BLK)
    assert (R // BLK) % mb == 0 and xs[0].shape[0] % (mb * BLK) == 0
    full = lambda a: pl.BlockSpec(a.shape, lambda i: (0,) * a.ndim)
    blkrow = pl.BlockSpec((mb, 1, D), lambda i: (i, 0, 0))
    rw = jnp.zeros((D, ROUTER_LANES), jnp.float32).at[:, :N_EXPERTS].set(router_w)
    args = list(lead_args) + [w_out, *xs, g1b, n2g, sc2b, sh2b, rw]
    if len(xs) == 1:
        x_specs = [pl.BlockSpec((mb * BLK, D), lambda i: (i, 0))]
        aliases = {len(lead_args) + 1: 0}
    else:
        x_specs = [pl.BlockSpec((mb * BLK, D), lambda i: (jnp.minimum(i, n_a - 1), 0)),
                   pl.BlockSpec((mb * BLK, D), lambda i: (jnp.maximum(i - n_a, 0), 0))]
        aliases = {}
    in_specs = list(lead_specs) + [full(w_out), *x_specs, blkrow, full(n2g), blkrow, blkrow, full(rw)]
    return pl.pallas_call(
        functools.partial(kernel_fn, n_a=n_a),
        grid=(R // (mb * BLK),),
        in_specs=in_specs,
        out_specs=[pl.BlockSpec((mb * BLK, D), lambda i: (i, 0)), pl.BlockSpec((mb * BLK, D), lambda i: (i, 0)),
                   pl.BlockSpec((N_EXPERTS, mb * BLK), lambda i: (0, i))],
        out_shape=[jax.ShapeDtypeStruct((R, D), jnp.float32), jax.ShapeDtypeStruct((R, D), jnp.bfloat16),
                   jax.ShapeDtypeStruct((N_EXPERTS, R), jnp.float32)],
        input_output_aliases=aliases,
        compiler_params=pltpu.CompilerParams(dimension_semantics=("arbitrary",), vmem_limit_bytes=MIX_VMEM_BYTES),
        name=name,
    )(*args)


def l0_out(ys, z, yf, yb, bonus, gate, ssd_g, ln_w, ln_b, e_ind, et_ind, w_out, x, g1b, n2g, sc2b, sh2b, router_w):
    c = RWKV_DIM
    rows = lambda n: pl.BlockSpec((MIX_BLOCKS * BLK, n), lambda i: (i, 0))
    full = lambda a: pl.BlockSpec(a.shape, lambda i: (0,) * a.ndim)
    lead = [ys, z, yf, yb, bonus, gate, ssd_g, ln_w, ln_b, e_ind, et_ind]
    specs = [rows(SSD_INNER), rows(SSD_INNER), rows(c), rows(c), rows(c), rows(c),
             full(ssd_g), full(ln_w), full(ln_b), full(e_ind), full(et_ind)]
    return _mix_out_call(_l0_out_kernel, "l0_out", lead, specs, w_out, x, g1b, n2g, sc2b, sh2b, router_w)


def l1_out(a, w_out, x, g1b, n2g, sc2b, sh2b, router_w):
    specs = [pl.BlockSpec((MIX_BLOCKS * BLK, a.shape[1]), lambda i: (i, 0))]
    return _mix_out_call(_l1_out_kernel, "l1_out", [a], specs, w_out, x, g1b, n2g, sc2b, sh2b, router_w)


RET_HPS = 2


def _l1_in_kernel(tbl_ref, x_ref, g_ref, sc_ref, sh_ref, w_ref, cos_ref, sin_ref, q_ref, k_ref, v_ref, gg_ref):
    f32 = jnp.float32
    hn = _adaln(x_ref[...], g_ref[...], sc_ref[0], sh_ref[0]).astype(jnp.bfloat16)
    cosf, sinf = cos_ref[0], sin_ref[0]

    def rope(x):
        parts = []
        for h in range(RET_HEADS):
            xh = x[:, h * RET_DK:(h + 1) * RET_DK]
            parts.append(xh * cosf + pltpu.roll(xh, RET_DK // 2, 1) * sinf)
        return jnp.concatenate(parts, axis=1)

    q_ref[...] = rope(jnp.dot(hn, w_ref[:, :RET_QK], preferred_element_type=f32)).astype(q_ref.dtype)
    k_ref[...] = (rope(jnp.dot(hn, w_ref[:, RET_QK:2 * RET_QK], preferred_element_type=f32))
                  * (RET_DK ** -0.5)).astype(k_ref.dtype)
    v_ref[...] = jnp.dot(hn, w_ref[:, 2 * RET_QK:2 * RET_QK + RET_V], preferred_element_type=f32).astype(v_ref.dtype)
    gg_ref[...] = jnp.dot(hn, w_ref[:, 2 * RET_QK + RET_V:], preferred_element_type=f32).astype(gg_ref.dtype)


def l1_in(x, g1, scb, shb, w_bf16, cos_t, sin_t, rope_blk):
    R, D = x.shape
    nblk = R // BLK
    tbl = jnp.asarray([rope_blk], jnp.int32)
    f32 = jnp.float32
    row = lambda n: pl.BlockSpec((BLK, n), lambda i, t: (i, 0))
    full = lambda a: pl.BlockSpec(a.shape, lambda i, t: (0,) * a.ndim)
    blkrow = pl.BlockSpec((1, 1, D), lambda i, t: (i, 0, 0))
    ropespec = pl.BlockSpec((1, BLK, RET_DK), lambda i, t: (t[0, i], 0, 0))
    return pl.pallas_call(
        _l1_in_kernel,
        grid_spec=pltpu.PrefetchScalarGridSpec(
            num_scalar_prefetch=1, grid=(nblk,),
            in_specs=[row(D), full(g1), blkrow, blkrow, full(w_bf16), ropespec, ropespec],
            out_specs=[row(RET_QK), row(RET_QK), row(RET_V), row(RET_V)]),
        out_shape=[jax.ShapeDtypeStruct((R, RET_QK), ACT_DTYPE), jax.ShapeDtypeStruct((R, RET_QK), ACT_DTYPE),
                   jax.ShapeDtypeStruct((R, RET_V), ACT_DTYPE), jax.ShapeDtypeStruct((R, RET_V), ACT_DTYPE)],
        compiler_params=pltpu.CompilerParams(dimension_semantics=("arbitrary",), vmem_limit_bytes=L0_VMEM_BYTES),
        name="l1_in",
    )(tbl, x, g1, scb, shb, w_bf16, cos_t, sin_t)


def _ret_kernel(tbl_ref, q_ref, k_ref, v_ref, g_ref, lg_ref, nw_ref, nb_ref, s0_ref, a_ref, fs_ref, st_ref, *,
                n_zero):
    f32, bf16 = jnp.float32, jnp.bfloat16
    sb = pl.program_id(0)
    nch = q_ref.shape[0] // BLK
    heads = range(RET_HPS)
    ks = [slice(h * RET_DK, (h + 1) * RET_DK) for h in heads]
    vs = [slice(h * RET_DV, (h + 1) * RET_DV) for h in heads]
    lgf = [lg_ref[h, 0:1, 0:1] for h in heads]
    lgb = [lg_ref[h, 1:2, 0:1] for h in heads]
    t_i = lax.broadcasted_iota(jnp.int32, (BLK, BLK), 0)
    s_i = lax.broadcasted_iota(jnp.int32, (BLK, BLK), 1)
    dist = (t_i - s_i).astype(f32)
    dm = [jnp.exp(jnp.where(s_i <= t_i, dist * lgf[h], NEG_BIG)) + jnp.exp(jnp.where(s_i >= t_i, -dist * lgb[h], NEG_BIG))
          for h in heads]
    tk = lax.broadcasted_iota(jnp.int32, (BLK, RET_DK), 0).astype(f32)
    tv = lax.broadcasted_iota(jnp.int32, (BLK, RET_DV), 0).astype(f32)
    k_to_end_f = [jnp.exp((BLK - 1.0 - tk) * lgf[h]) for h in heads]
    k_to_end_b = [jnp.exp(tk * lgb[h]) for h in heads]
    from_start_f = [jnp.exp((tv + 1.0) * lgf[h]) for h in heads]
    from_start_b = [jnp.exp((BLK - tv) * lgb[h]) for h in heads]
    nt, tn = (((1,), (1,)), ((), ())), (((0,), (0,)), ((), ()))

    def chunk_rows(c):
        return pl.ds(pl.multiple_of(c * BLK, BLK), BLK)

    def fwd(c, carry):
        blk = sb * nch + c
        rows = chunk_rows(c)

        @pl.when(tbl_ref[0, blk] == 1)
        def _():
            st_ref[0] = jnp.where(sb >= n_zero, s0_ref[0, 0], 0.0)

        q = [q_ref[rows, ks[h]].astype(bf16) for h in heads]
        k = [k_ref[rows, ks[h]] for h in heads]
        v = [v_ref[rows, vs[h]].astype(bf16) for h in heads]
        s_in = [st_ref[0, h] for h in heads]
        g = [lax.dot_general(q[h], k[h].astype(bf16), nt, preferred_element_type=f32) for h in heads]
        y_diag = [jnp.dot((g[h] * dm[h]).astype(bf16), v[h], preferred_element_type=f32) for h in heads]
        y_off = [jnp.dot(q[h], s_in[h].astype(bf16), preferred_element_type=f32) for h in heads]
        kw = [(k[h].astype(f32) * k_to_end_f[h]).astype(bf16) for h in heads]
        upd = [lax.dot_general(kw[h], v[h], tn, preferred_element_type=f32) for h in heads]
        for h in heads:
            a_ref[rows, vs[h]] = y_diag[h] + from_start_f[h] * y_off[h]
            st_ref[0, h] = jnp.exp(BLK * lgf[h]) * s_in[h] + upd[h]
        fs_ref[c, 0] = st_ref[0]
        return carry

    lax.fori_loop(0, nch, fwd, 0)

    def bwd(j, carry):
        c = nch - 1 - j
        blk = sb * nch + c
        rows = chunk_rows(c)

        @pl.when(tbl_ref[1, blk] == 1)
        def _():
            st_ref[1] = jnp.where(sb >= n_zero, s0_ref[0, 1], 0.0)

        q = [q_ref[rows, ks[h]].astype(bf16) for h in heads]
        v = [v_ref[rows, vs[h]].astype(bf16) for h in heads]
        s_in = [st_ref[1, h] for h in heads]
        y_off = [jnp.dot(q[h], s_in[h].astype(bf16), preferred_element_type=f32) for h in heads]
        kw = [(k_ref[rows, ks[h]].astype(f32) * k_to_end_b[h]).astype(bf16) for h in heads]
        upd = [lax.dot_general(kw[h], v[h], tn, preferred_element_type=f32) for h in heads]
        for h in heads:
            st_ref[1, h] = jnp.exp(BLK * lgb[h]) * s_in[h] + upd[h]
            y = a_ref[rows, vs[h]] + from_start_b[h] * y_off[h]
            mu = jnp.mean(y, -1, keepdims=True)
            yc = y - mu
            var = jnp.mean(yc * yc, -1, keepdims=True)
            gg = g_ref[rows, vs[h]].astype(f32)
            a_ref[rows, vs[h]] = ((yc * lax.rsqrt(var + 1e-5) * nw_ref[:, vs[h]] + nb_ref[:, vs[h]])
                                  * (gg * jax.nn.sigmoid(gg)))
        fs_ref[c, 1] = st_ref[1]
        return carry

    lax.fori_loop(0, nch, bwd, 0)


def ret_scan(q, k, v, g, lg_tab, norm_w, norm_b, s0, n_zero, first, last, sb_rows):
    R = q.shape[0]
    n_sb = R // sb_rows
    nch = sb_rows // BLK
    tbl = jnp.asarray([first, last], jnp.int32)
    f32 = jnp.float32
    hps = RET_HPS
    return pl.pallas_call(
        functools.partial(_ret_kernel, n_zero=n_zero),
        grid_spec=pltpu.PrefetchScalarGridSpec(
            num_scalar_prefetch=1, grid=(n_sb, RET_HEADS // hps),
            in_specs=[pl.BlockSpec((sb_rows, hps * RET_DK), lambda s, h, t: (s, h)),
                      pl.BlockSpec((sb_rows, hps * RET_DK), lambda s, h, t: (s, h)),
                      pl.BlockSpec((sb_rows, hps * RET_DV), lambda s, h, t: (s, h)),
                      pl.BlockSpec((sb_rows, hps * RET_DV), lambda s, h, t: (s, h)),
                      pl.BlockSpec((hps, 8, 128), lambda s, h, t: (h, 0, 0)),
                      pl.BlockSpec((1, hps * RET_DV), lambda s, h, t: (0, h)),
                      pl.BlockSpec((1, hps * RET_DV), lambda s, h, t: (0, h)),
                      pl.BlockSpec((1, 2, hps, RET_DK, RET_DV), lambda s, h, t: (jnp.maximum(s - n_zero, 0), 0, h, 0, 0))],
            out_specs=[pl.BlockSpec((sb_rows, hps * RET_DV), lambda s, h, t: (s, h)),
                       pl.BlockSpec((nch, 2, hps, RET_DK, RET_DV), lambda s, h, t: (s, 0, h, 0, 0))],
            scratch_shapes=[pltpu.VMEM((2, hps, RET_DK, RET_DV), f32)]),
        out_shape=[jax.ShapeDtypeStruct((R, RET_V), f32),
                   jax.ShapeDtypeStruct((R // BLK, 2, RET_HEADS, RET_DK, RET_DV), f32)],
        compiler_params=pltpu.CompilerParams(dimension_semantics=("arbitrary", "arbitrary"),
                                             vmem_limit_bytes=SSD_VMEM_BYTES),
        name="ret_scan",
    )(tbl, q, k, v, g, lg_tab, norm_w.reshape(1, -1), norm_b.reshape(1, -1), s0)


def _mod_kernel(c_ref, w_ref, b_ref, o_ref):
    c = c_ref[...]
    act = c * jax.nn.sigmoid(c)
    o_ref[0] = _dot(act, w_ref[0], ((1,), (0,)), 3) + b_ref[0]


def mod_vectors(conds, mod_w, mod_b):
    depth, D, n6 = mod_w.shape
    tn = D
    return pl.pallas_call(
        _mod_kernel,
        grid=(depth, n6 // tn),
        in_specs=[pl.BlockSpec(conds.shape, lambda i, j: (0, 0)),
                  pl.BlockSpec((1, D, tn), lambda i, j: (i, 0, j)),
                  pl.BlockSpec((1, 1, tn), lambda i, j: (i, 0, j))],
        out_specs=pl.BlockSpec((1, conds.shape[0], tn), lambda i, j: (i, 0, j)),
        out_shape=jax.ShapeDtypeStruct((depth, conds.shape[0], n6), jnp.float32),
        name="mod_vectors",
    )(conds, mod_w, mod_b.reshape(depth, 1, n6))


def rope_tables(n_tokens):
    rows = n_tokens // GRID_W
    row = np.repeat(np.arange(rows), GRID_W).astype(np.float64)
    col = np.tile(np.arange(GRID_W), rows).astype(np.float64)
    n_f = RET_DK // 4
    inv = ROPE_BASE ** (-np.arange(n_f, dtype=np.float64) / n_f)
    ang = np.concatenate([row[:, None] * inv, col[:, None] * inv], -1)
    return np.cos(ang), np.sin(ang)


def _rope_block_tables(n_ctx, n_lat, l_lat):
    nb = l_lat // BLK
    cos, sin = rope_tables(l_lat)
    cosf = np.concatenate([cos, cos], -1).reshape(nb, BLK, RET_DK)
    sinf = np.concatenate([-sin, sin], -1).reshape(nb, BLK, RET_DK)
    cos_t = jnp.asarray(np.concatenate([np.ones((1, BLK, RET_DK)), cosf]), jnp.float32)
    sin_t = jnp.asarray(np.concatenate([np.zeros((1, BLK, RET_DK)), sinf]), jnp.float32)
    rope_blk = [0] * n_ctx + [1 + j for _ in range(n_lat) for j in range(nb)]
    return cos_t, sin_t, rope_blk


def kernel(x_prompt, x_sample, state_ssd, state_rwkv, state_ret, c, c_ctx, mod_w, mod_b, norm1_g, norm2_g,
           router_w, exp_w_gate, exp_w_up, exp_w_down, ab_w_in, ab_w_out, ssd_conv_w, ssd_conv_b, ssd_dt_bias,
           ssd_a_log, ssd_d, ssd_norm_g, rwkv_mu_prev, rwkv_mu_next, rwkv_w0, rwkv_w2, rwkv_a0, rwkv_a2, rwkv_g2,
           rwkv_k_k, rwkv_k_a, rwkv_r_k, rwkv_ln_w, rwkv_ln_b, ret_w_in, ret_w_out, ret_decay_logit, ret_norm_w,
           ret_norm_b, final_norm_g):
    p = dict(mod_w=mod_w, mod_b=mod_b, norm1_g=norm1_g, norm2_g=norm2_g, router_w=router_w,
             exp_w_gate=exp_w_gate, exp_w_up=exp_w_up, exp_w_down=exp_w_down, ab_w_in=ab_w_in, ab_w_out=ab_w_out,
             ssd_conv_w=ssd_conv_w, ssd_conv_b=ssd_conv_b, ssd_dt_bias=ssd_dt_bias, ssd_a_log=ssd_a_log,
             ssd_d=ssd_d, ssd_norm_g=ssd_norm_g, rwkv_mu_prev=rwkv_mu_prev, rwkv_mu_next=rwkv_mu_next,
             rwkv_w0=rwkv_w0, rwkv_w2=rwkv_w2, rwkv_a0=rwkv_a0, rwkv_a2=rwkv_a2, rwkv_g2=rwkv_g2,
             rwkv_k_k=rwkv_k_k, rwkv_k_a=rwkv_k_a, rwkv_r_k=rwkv_r_k, rwkv_ln_w=rwkv_ln_w, rwkv_ln_b=rwkv_ln_b,
             ret_w_in=ret_w_in, ret_w_out=ret_w_out, ret_decay_logit=ret_decay_logit, ret_norm_w=ret_norm_w,
             ret_norm_b=ret_norm_b, final_norm_g=final_norm_g)
    f32, bf16 = jnp.float32, jnp.bfloat16
    n_ctx, l_ctx, D = x_prompt.shape
    n_lat, l_lat, _ = x_sample.shape
    assert l_ctx == BLK and l_lat % BLK == 0 and (n_ctx * BLK) % l_lat == 0
    n_sb_ctx = n_ctx * BLK // l_lat
    cond_id, first, last = _seq_tables(n_ctx, n_lat, l_lat)
    x = (x_prompt.reshape(-1, D), x_sample.reshape(-1, D))

    conds = jnp.concatenate([c_ctx[None, :], c, jnp.zeros((8 - 1 - n_lat, D), f32)])
    mods = mod_vectors(conds, mod_w, mod_b)[:, jnp.asarray(cond_id)]
    mods = mods.reshape(DEPTH, len(cond_id), 6, 1, D)

    new_ssd, new_rwkv, new_ret = [], [], []
    out = None
    for i in range(DEPTH):
        sh1, sc1, g1, sh2, sc2, g2 = (mods[i, :, k] for k in range(6))
        e = i // 2
        if i % 2 == 0:
            w_packed, mup, mun, rwp, w2bd, a2p, g2p, e_ind, et_ind = l0_pack_weights(p, e)
            z, xbc, dt, r, v, an, lw, kd, bv, gate, bonus = l0_in(
                x, norm1_g[i][None], sc1, sh1, w_packed, mup, mun, rwp, w2bd, a2p, g2p, e_ind, et_ind, first, last)
            sel, hp = ssd_tables(p, e)
            s0_ssd = jnp.transpose(state_ssd[:, e], (0, 1, 3, 2, 4)).reshape(n_lat, 2, SSD_N, SSD_INNER)
            ys, fs_ssd = ssd_scan(xbc, dt, ssd_conv_w[e], ssd_conv_b[e], sel, hp,
                                  s0_ssd, n_sb_ctx, first, last, l_lat)
            new_ssd.append(jnp.transpose(fs_ssd[:n_ctx].reshape(n_ctx, 2, SSD_N, SSD_HEADS, SSD_P), (0, 1, 3, 2, 4)))
            s0_rwkv = jnp.transpose(state_rwkv[:, e], (0, 1, 3, 2, 4)).reshape(n_lat, 2, RWKV_N, RWKV_DIM)
            yf, yb, sf_rwkv = rwkv_scan_pallas(r, v, an, lw, kd, bv, s0_rwkv, n_ctx,
                                               _rwkv_steps(n_ctx, n_lat, l_lat))
            new_rwkv.append(jnp.transpose(sf_rwkv[:n_ctx].reshape(n_ctx, 2, RWKV_N, RWKV_HEADS, RWKV_N),
                                          (0, 1, 3, 2, 4)))
            x, hn2, affT = l0_out(ys, z, yf, yb, bonus, gate, ssd_norm_g[e][None], rwkv_ln_w[e][None], rwkv_ln_b[e][None],
                                  e_ind, et_ind, ab_w_out[e].astype(bf16), x, g1, norm2_g[i][None], sc2, sh2,
                                  router_w[i])
        else:
            cos_t, sin_t, rope_blk = _rope_block_tables(n_ctx, n_lat, l_lat)
            q, k, v, gg = l1_in(x, norm1_g[i][None], sc1, sh1, ret_w_in[e].astype(bf16), cos_t, sin_t, rope_blk)
            lg = jax.nn.log_sigmoid(ret_decay_logit[e].astype(f32))
            lg_tab = jnp.zeros((RET_HEADS, 8, 128), f32).at[:, :2, :].set(jnp.transpose(lg)[:, :, None])
            a, fs_ret = ret_scan(q, k, v, gg, lg_tab, ret_norm_w[e], ret_norm_b[e],
                                 state_ret[:, e], n_sb_ctx, first, last, l_lat)
            new_ret.append(fs_ret[:n_ctx])
            x, hn2, affT = l1_out(a, ret_w_out[e].astype(bf16), x, g1, norm2_g[i][None], sc2, sh2, router_w[i])
        fin = final_norm_g if i == DEPTH - 1 else None
        out = moe_layer(x, hn2, affT, g2, exp_w_gate, exp_w_up, exp_w_down, i, n_ctx, l_lat, final_g=fin)
        if fin is None:
            x = out
    y_ctx, y_lat = out
    return (y_ctx.reshape(n_ctx, l_ctx, D), y_lat.reshape(n_lat, l_lat, D),
            jnp.stack(new_ssd, 1), jnp.stack(new_rwkv, 1), jnp.stack(new_ret, 1))
```

```python
import functools
import math

import jax
import jax.numpy as jnp
import numpy as np
from jax import lax
from jax.experimental import pallas as pl
from jax.experimental.pallas import tpu as pltpu

DEPTH = 2
GRID_W = 64
NORM_EPS = 1e-6
SSD_HEADS = 16
SSD_P = 64
SSD_INNER = SSD_HEADS * SSD_P
SSD_GROUPS = 2
SSD_HPG = SSD_HEADS // SSD_GROUPS
SSD_N = 128
SSD_XBC = SSD_INNER + 2 * SSD_GROUPS * SSD_N
SSD_IN = SSD_INNER + SSD_XBC + SSD_HEADS
RWKV_HEADS = 16
RWKV_N = 64
RWKV_DIM = RWKV_HEADS * RWKV_N
W_LORA = 64
A_LORA = 64
G_LORA = 128
RWKV_GN_EPS = 64e-5
RET_HEADS = 8
RET_DK = 128
RET_DV = 256
RET_QK = RET_HEADS * RET_DK
RET_V = RET_HEADS * RET_DV
ROPE_BASE = 10000.0
N_EXPERTS = 16
EC_CAPACITY = 2

ACT_DTYPE = jnp.bfloat16

RWKV_C = 64
RWKV_GH = 4
RWKV_GL = RWKV_GH * RWKV_N
RWKV_DOUBLING_PASSES = (3, 3, 3, 3, 1, 1)


def _split_bf16(x):
    hi = x.astype(jnp.bfloat16)
    lo = (x - hi.astype(jnp.float32)).astype(jnp.bfloat16)
    return hi, lo


def _dot(a, b, dims, passes):
    f = functools.partial(lax.dot_general, dimension_numbers=(dims, ((), ())),
                          preferred_element_type=jnp.float32)
    if passes == 1:
        return f(a.astype(jnp.bfloat16), b.astype(jnp.bfloat16))
    ah, al = _split_bf16(a)
    bh, bl = _split_bf16(b)
    return f(ah, bh) + (f(ah, bl) + f(al, bh))


def _rwkv_chunk_kernel(tbl_ref, r0_ref, v0_ref, a0_ref, r1_ref, v1_ref, a1_ref, lw0_ref, k0_ref, b0_ref,
                       lw1_ref, k1_ref, b1_ref, s0_ref, y0_ref, y1_ref, sf_ref, h_ref, *, p_inv, p_oth, n_zero):
    C, N, GH, GL = RWKV_C, RWKV_N, RWKV_GH, RWKV_GL
    i = pl.program_id(0)
    f32, bf16 = jnp.float32, jnp.bfloat16

    @pl.when(tbl_ref[3, i] == 1)
    def _():
        h_ref[...] = jnp.where(tbl_ref[2, i] >= n_zero, s0_ref[0], 0.0)

    t_i = lax.broadcasted_iota(jnp.int32, (C, GL), 0)
    s_i = lax.broadcasted_iota(jnp.int32, (C, GL), 1) & (N - 1)
    eye = (s_i == t_i).astype(f32)
    row2 = lax.broadcasted_iota(jnp.int32, (2 * C, GL), 0)
    rel2 = (lax.broadcasted_iota(jnp.int32, (2 * C, GL), 1) & (N - 1)) - (row2 & (C - 1))
    incl2 = row2 // C
    mask2 = [rel2 - incl2 < 0, -rel2 - incl2 < 0]
    bh_r = lax.broadcasted_iota(jnp.int32, (GL, GL), 0) // N
    bh_c = lax.broadcasted_iota(jnp.int32, (GL, GL), 1) // N
    blk = bh_r == bh_c
    tt = lax.broadcasted_iota(jnp.int32, (C, C), 0)
    ss = lax.broadcasted_iota(jnp.int32, (C, C), 1)
    tri = [(ss <= tt).astype(bf16), (ss >= tt).astype(bf16)]

    def bd(x, passes):
        pieces = []
        for _ in range(2 if passes == 3 else 1):
            hi = x.astype(bf16)
            x = x - hi.astype(f32)
            pieces.append(jnp.where(blk, jnp.concatenate([hi] * GH, axis=0), jnp.zeros((), bf16)))
        return pieces

    def mm(l, x, passes, dims=((1,), (0,))):
        f = functools.partial(lax.dot_general, dimension_numbers=(dims, ((), ())), preferred_element_type=f32)
        xs = bd(x, passes)
        lh = l.astype(bf16)
        if passes == 1:
            return f(lh, xs[0])
        ll = (l - lh.astype(f32)).astype(bf16)
        m = l.shape[0]
        both = f(jnp.concatenate([lh, ll], axis=0), xs[0])
        if passes == 2:
            return both[:m] + both[m:]
        return both[:m] + (f(lh, xs[1]) + both[m:])

    nt = ((1,), (1,))
    refs = [(r0_ref, v0_ref, a0_ref, lw0_ref, k0_ref, b0_ref), (r1_ref, v1_ref, a1_ref, lw1_ref, k1_ref, b1_ref)]
    lw, r_t, a_t, b_t, k_t, v = [], [], [], [], [], []
    for d, (r_ref, v_ref, a_ref, lw_ref, k_ref, b_ref) in enumerate(refs):
        lwd = lw_ref[0]
        lw_hi, lw_lo = _split_bf16(lwd)
        cum = (jnp.dot(tri[d], lw_hi, preferred_element_type=f32) + jnp.dot(tri[d], lw_lo, preferred_element_type=f32))
        w_inv = jnp.exp(-cum)
        lw.append(lwd)
        r_t.append(r_ref[...].astype(f32) * jnp.exp(cum))
        a_t.append(a_ref[...].astype(f32) * jnp.exp(cum - lwd))
        b_t.append(b_ref[0].astype(f32) * w_inv)
        k_t.append(k_ref[0].astype(f32) * w_inv)
        v.append(v_ref[...].astype(f32))
    lane_head = lax.broadcasted_iota(jnp.int32, (N, GL), 1) // N

    chains = [(d, slice(g * GL, (g + 1) * GL)) for d in range(2) for g in range(RWKV_HEADS // GH)]
    each = lambda fn: [fn(j, d, sl) for j, (d, sl) in enumerate(chains)]
    bg = each(lambda j, d, sl: b_t[d][:, sl])
    kg = each(lambda j, d, sl: k_t[d][:, sl])
    vg = each(lambda j, d, sl: v[d][:, sl])
    h0 = each(lambda j, d, sl: h_ref[d, :, sl])
    ar = each(lambda j, d, sl: jnp.concatenate([a_t[d][:, sl], r_t[d][:, sl]], axis=0))
    m_b = each(lambda j, d, sl: jnp.where(mask2[d], mm(ar[j], bg[j], p_oth, nt), 0.0))
    m_k = each(lambda j, d, sl: jnp.where(mask2[d], mm(ar[j], kg[j], p_oth, nt), 0.0))
    p = each(lambda j, d, sl: mm(m_b[j][:C], m_b[j][:C], p_inv[0]))
    tmat = each(lambda j, d, sl: eye + m_b[j][:C])
    for lv in range(int(math.log2(C)) - 2):
        pt = each(lambda j, d, sl: mm(jnp.concatenate([p[j], tmat[j]], axis=0), p[j], p_inv[1 + lv]))
        p = each(lambda j, d, sl: pt[j][:C])
        tmat = each(lambda j, d, sl: tmat[j] + pt[j][C:])
    tmat = each(lambda j, d, sl: tmat[j] + mm(tmat[j], p[j], p_inv[-1]))
    ar_h = each(lambda j, d, sl: mm(ar[j], h0[j], p_oth, nt))
    mk_v = each(lambda j, d, sl: mm(m_k[j], vg[j], p_oth))
    u = each(lambda j, d, sl: mm(tmat[j], ar_h[j][:C] + mk_v[j][:C], p_oth))
    y = each(lambda j, d, sl: ar_h[j][C:] + mm(m_b[j][C:], u[j], p_oth) + mk_v[j][C:])
    full = each(lambda j, d, sl: _dot(jnp.concatenate([u[j], vg[j]], axis=0), jnp.concatenate([bg[j], kg[j]], axis=0),
                                      ((0,), (0,)), p_oth))
    y_refs = (y0_ref, y1_ref)
    for j, (d, sl) in enumerate(chains):
        y_refs[d][:, sl] = y[j].astype(y_refs[d].dtype)
        z = jnp.zeros((N, GL), f32)
        for hh in range(GH):
            z = z + jnp.where(lane_head == hh, full[j][hh * N:(hh + 1) * N], 0.0)
        w_tot = jnp.exp(jnp.sum(lw[d][:, sl], axis=0, keepdims=True))
        h_ref[d, :, sl] = w_tot * (h0[j] + z)

    @pl.when(tbl_ref[4, i] == 1)
    def _():
        sf_ref[0] = h_ref[...]


def _rwkv_steps(n_ctx, n_lat, l_lat):
    C = RWKV_C
    rows = []
    seqs = [(s, s * BLK, BLK) for s in range(n_ctx)] + [(n_ctx + s, n_ctx * BLK + s * l_lat, l_lat) for s in range(n_lat)]
    for sid, row0, length in seqs:
        nc = length // C
        for j in range(nc):
            rows.append((row0 // C + j, row0 // C + nc - 1 - j, sid, int(j == 0), int(j == nc - 1)))
    return [list(col) for col in zip(*rows)]


def rwkv_scan_pallas(r, v, a, lw, k, b, s0, n_zero, steps, p_inv=RWKV_DOUBLING_PASSES, p_oth=1):
    R, HN = r.shape
    C, N = RWKV_C, RWKV_N
    tbl = jnp.asarray(steps, jnp.int32)
    fwd = pl.BlockSpec((C, HN), lambda i, t: (t[0, i], 0))
    bwd = pl.BlockSpec((C, HN), lambda i, t: (t[1, i], 0))
    fwd_d = pl.BlockSpec((1, C, HN), lambda i, t: (0, t[0, i], 0))
    bwd_d = pl.BlockSpec((1, C, HN), lambda i, t: (1, t[1, i], 0))
    n_seq = max(steps[2]) + 1
    st_in = pl.BlockSpec((1, 2, N, HN), lambda i, t: (jnp.maximum(t[2, i] - n_zero, 0), 0, 0, 0))
    st = pl.BlockSpec((1, 2, N, HN), lambda i, t: (t[2, i], 0, 0, 0))
    return pl.pallas_call(
        functools.partial(_rwkv_chunk_kernel, p_inv=p_inv, p_oth=p_oth, n_zero=n_zero),
        grid_spec=pltpu.PrefetchScalarGridSpec(
            num_scalar_prefetch=1,
            grid=(len(steps[0]),),
            in_specs=[fwd, fwd, fwd, bwd, bwd, bwd, fwd_d, fwd_d, fwd_d, bwd_d, bwd_d, bwd_d, st_in],
            out_specs=[fwd, bwd, st],
            scratch_shapes=[pltpu.VMEM((2, N, HN), jnp.float32)]),
        out_shape=[jax.ShapeDtypeStruct((R, HN), ACT_DTYPE), jax.ShapeDtypeStruct((R, HN), ACT_DTYPE),
                   jax.ShapeDtypeStruct((n_seq, 2, N, HN), jnp.float32)],
        compiler_params=pltpu.CompilerParams(dimension_semantics=("arbitrary",)),
        name="rwkv_scan",
    )(tbl, r, v, a, r, v, a, lw, k, b, lw, k, b, s0)


BLK = 256
FF_TILE = 768
SELECT_TILE = 512
SCATTER_WINDOW = 96
SLOT_ALIGN = 16
SELECT_MIN_EXP = -1100.0
SELECT_BINADE_STEPS = 11
SELECT_MANTISSA_STEPS = 40
MOE_FFN_VMEM_BYTES = 48 * 1024 * 1024


def _moe_select_kernel(aff_ref, slot_ref, *, cap):
    a = aff_ref[...]
    E, T = a.shape
    f32 = jnp.float32

    def enough(piv):
        return jnp.sum(jnp.where(a >= piv, 1.0, 0.0), axis=1, keepdims=True) >= cap

    def binade(_, lohi):
        e_lo, e_hi = lohi
        mid = jnp.floor((e_lo + e_hi) * 0.5)
        ok = enough(jnp.exp2(mid))
        return jnp.where(ok, mid, e_lo), jnp.where(ok, e_hi, mid)

    e_lo, e_hi = lax.fori_loop(0, SELECT_BINADE_STEPS, binade,
                               (jnp.full((E, 1), SELECT_MIN_EXP, f32), jnp.full((E, 1), 1.0, f32)))

    def inside(_, lohi):
        lo, hi = lohi
        mid = lo + (hi - lo) * 0.5
        ok = enough(mid)
        return jnp.where(ok, mid, lo), jnp.where(ok, hi, mid)

    thr, _ = lax.fori_loop(0, SELECT_MANTISSA_STEPS, inside, (jnp.exp2(e_lo), jnp.exp2(e_hi)))
    gt = a > thr
    eq = a == thr
    need = cap - jnp.sum(jnp.where(gt, 1.0, 0.0), axis=1, keepdims=True)
    tw = min(T, SELECT_TILE)

    def prefix_count(mask):
        m = jnp.where(mask, 1.0, 0.0).astype(jnp.bfloat16)
        outs = []
        for j in range(T // tw):
            s_i = lax.broadcasted_iota(jnp.int32, (T, tw), 0)
            t_i = lax.broadcasted_iota(jnp.int32, (T, tw), 1) + j * tw
            before = jnp.where(s_i < t_i, 1.0, 0.0).astype(jnp.bfloat16)
            outs.append(jnp.dot(m, before, preferred_element_type=f32))
        return outs[0] if len(outs) == 1 else jnp.concatenate(outs, axis=1)

    sel = gt | (eq & (prefix_count(eq) < need))
    slot_ref[...] = jnp.where(sel, prefix_count(sel).astype(jnp.int32), -1)


def _moe_select(affT, row0, n_seq, t):
    E = affT.shape[0]
    rows = jnp.transpose(affT[:, row0:row0 + n_seq * t].reshape(E, n_seq, t), (1, 0, 2)).reshape(n_seq * E, t)
    slot = pl.pallas_call(
        functools.partial(_moe_select_kernel, cap=EC_CAPACITY * t // N_EXPERTS),
        grid=(1,),
        in_specs=[pl.BlockSpec((n_seq * E, t), lambda s: (0, 0))],
        out_specs=pl.BlockSpec((n_seq * E, t), lambda s: (0, 0)),
        out_shape=jax.ShapeDtypeStruct((n_seq * E, t), jnp.int32),
        name="moe_select",
    )(rows)
    return jnp.transpose(slot.reshape(n_seq, E, t), (1, 0, 2)).reshape(E, n_seq * t)


def _moe_gather_kernel(tbl_ref, slot_ref, aff_ref, hn_ref, xe_ref, gate_ref, *, cap, win, nb):
    E = slot_ref.shape[0]
    s = pl.program_id(0)
    w_i = lax.broadcasted_iota(jnp.int32, (win, BLK), 0)
    xe_ref[...] = jnp.zeros_like(xe_ref)
    gate_ref[...] = jnp.zeros_like(gate_ref)
    for j in range(nb):
        blk = s * nb + j
        toks = slice(j * BLK, (j + 1) * BLK)
        hn_blk = hn_ref[toks, :]

        def window(w, carry):
            hits, starts = [], []
            for e in range(E):
                lo = tbl_ref[e, blk] + w * win
                start = pl.multiple_of(jnp.minimum(lo, cap - win), SLOT_ALIGN)
                slot = slot_ref[e, :, toks]
                slot = jnp.where(slot >= lo, slot, -1)
                hits.append(slot == w_i + start)
                starts.append(start)
            onehot = jnp.concatenate([jnp.where(h, 1.0, 0.0) for h in hits], axis=0).astype(jnp.bfloat16)
            rows = jnp.dot(onehot, hn_blk, preferred_element_type=jnp.float32)
            for e in range(E):
                dst = pl.ds(starts[e], win)
                xe_ref[e, dst, :] += rows[e * win:(e + 1) * win].astype(xe_ref.dtype)
                g = jnp.sum(jnp.where(hits[e], aff_ref[e, :, toks], 0.0), axis=1, keepdims=True)
                gate_ref[e, dst, :] += jnp.broadcast_to(g, (win, 128))
            return carry

        lax.fori_loop(0, tbl_ref[E, blk], window, 0)


def _moe_gather(slot, slot3, aff3, hn, row0, n_seq, t):
    E = slot3.shape[0]
    D = hn.shape[1]
    cap = EC_CAPACITY * t // N_EXPERTS
    win = min(cap, SCATTER_WINDOW)
    nb = t // BLK
    b0 = row0 // t
    tbl = _scatter_windows(slot, row0, n_seq, t, win)
    return pl.pallas_call(
        functools.partial(_moe_gather_kernel, cap=cap, win=win, nb=nb),
        grid_spec=pltpu.PrefetchScalarGridSpec(
            num_scalar_prefetch=1, grid=(n_seq,),
            in_specs=[pl.BlockSpec((E, 1, t), lambda s, tb_: (0, 0, b0 + s)),
                      pl.BlockSpec((E, 1, t), lambda s, tb_: (0, 0, b0 + s)),
                      pl.BlockSpec((t, D), lambda s, tb_: (b0 + s, 0))],
            out_specs=[pl.BlockSpec((E, cap, D), lambda s, tb_: (0, s, 0)),
                       pl.BlockSpec((E, cap, 128), lambda s, tb_: (0, s, 0))]),
        out_shape=[jax.ShapeDtypeStruct((E, n_seq * cap, D), jnp.bfloat16),
                   jax.ShapeDtypeStruct((E, n_seq * cap, 128), jnp.float32)],
        compiler_params=pltpu.CompilerParams(dimension_semantics=("arbitrary",),
                                             vmem_limit_bytes=MOE_FFN_VMEM_BYTES),
        name="moe_gather",
    )(tbl, slot3, aff3, hn)


def _moe_ffn_kernel(xc_ref, xl_ref, gc_ref, gl_ref, wg_ref, wu_ref, wd_ref, yc_ref, yl_ref, acc_ref, *, nf):
    f = pl.program_id(1)
    bf16 = jnp.bfloat16
    wg = wg_ref[0, 0].astype(bf16)
    wu = wu_ref[0, 0].astype(bf16)
    wd = wd_ref[0, 0].astype(bf16)
    nc = xc_ref.shape[1]
    x = jnp.concatenate([xc_ref[0], xl_ref[0]], axis=0)
    g = jnp.dot(x, wg, preferred_element_type=jnp.float32)
    u = jnp.dot(x, wu, preferred_element_type=jnp.float32)
    h = (g * jax.nn.sigmoid(g) * u).astype(bf16)
    y = jnp.dot(h, wd, preferred_element_type=jnp.float32)

    if nf > 1:
        @pl.when(f == 0)
        def _():
            acc_ref[...] = y

        @pl.when((f != 0) & (f != nf - 1))
        def _():
            acc_ref[...] += y

    @pl.when(f == nf - 1)
    def _():
        total = y if nf == 1 else acc_ref[...] + y
        gate = jnp.concatenate([gc_ref[0], gl_ref[0]], axis=0)
        out = total * jnp.concatenate([gate] * (total.shape[1] // 128), axis=1)
        yc_ref[0] = out[:nc].astype(yc_ref.dtype)
        yl_ref[0] = out[nc:].astype(yl_ref.dtype)


def _moe_ffn(xc, xl, gc, gl, wg, wu, wd, layer):
    E, nc_rows, D = xc.shape
    nl_rows = xl.shape[1]
    F = wg.shape[3]
    nf = F // FF_TILE
    return pl.pallas_call(
        functools.partial(_moe_ffn_kernel, nf=nf),
        grid=(E, nf),
        in_specs=[pl.BlockSpec((1, nc_rows, D), lambda e, f: (e, 0, 0)),
                  pl.BlockSpec((1, nl_rows, D), lambda e, f: (e, 0, 0)),
                  pl.BlockSpec((1, nc_rows, 128), lambda e, f: (e, 0, 0)),
                  pl.BlockSpec((1, nl_rows, 128), lambda e, f: (e, 0, 0)),
                  pl.BlockSpec((1, 1, D, FF_TILE), lambda e, f: (layer, e, 0, f)),
                  pl.BlockSpec((1, 1, D, FF_TILE), lambda e, f: (layer, e, 0, f)),
                  pl.BlockSpec((1, 1, FF_TILE, D), lambda e, f: (layer, e, f, 0))],
        out_specs=[pl.BlockSpec((1, nc_rows, D), lambda e, f: (e, 0, 0)),
                   pl.BlockSpec((1, nl_rows, D), lambda e, f: (e, 0, 0))],
        out_shape=[jax.ShapeDtypeStruct((E, nc_rows, D), jnp.bfloat16),
                   jax.ShapeDtypeStruct((E, nl_rows, D), jnp.bfloat16)],
        scratch_shapes=[pltpu.VMEM((nc_rows + nl_rows, D), jnp.float32)],
        compiler_params=pltpu.CompilerParams(dimension_semantics=("arbitrary", "arbitrary"),
                                             vmem_limit_bytes=MOE_FFN_VMEM_BYTES),
        name="moe_ffn",
    )(xc, xl, gc, gl, wg, wu, wd)


def _moe_scatter_kernel(tbl_ref, slot_ref, ye_ref, x_ref, g2_ref, *rest, cap, win, nb, final):
    if final:
        fg_ref, o_ref, acc_ref = rest
    else:
        o_ref, acc_ref = rest
    E = ye_ref.shape[0]
    tb = x_ref.shape[0]
    blk = pl.program_id(0) * nb + pl.program_id(1)
    w_i = lax.broadcasted_iota(jnp.int32, (win, tb), 0)
    acc_ref[...] = jnp.zeros_like(acc_ref)

    def window(w, carry):
        hot, rows = [], []
        for e in range(E):
            lo = tbl_ref[e, blk] + w * win
            start = pl.multiple_of(jnp.minimum(lo, cap - win), SLOT_ALIGN)
            slot = slot_ref[e]
            slot = jnp.where(slot >= lo, slot, -1)
            hot.append(jnp.where(slot == w_i + start, 1.0, 0.0))
            rows.append(ye_ref[e, pl.ds(start, win), :])
        onehot = jnp.concatenate(hot, axis=0).astype(jnp.bfloat16)
        acc_ref[...] += lax.dot_general(onehot, jnp.concatenate(rows, axis=0), (((0,), (0,)), ((), ())),
                                        preferred_element_type=jnp.float32)
        return carry

    lax.fori_loop(0, tbl_ref[E, blk], window, 0)
    x = x_ref[...] + g2_ref[0] * acc_ref[...]
    if final:
        x = x * lax.rsqrt(jnp.mean(x * x, -1, keepdims=True) + NORM_EPS) * fg_ref[...]
    o_ref[...] = x


def _scatter_windows(slot, row0, n_seq, t, win):
    E = slot.shape[0]
    cap = EC_CAPACITY * t // N_EXPERTS
    nb = t // BLK
    cnt = jnp.sum(slot[:, row0:row0 + n_seq * t].reshape(E, n_seq, nb, BLK) >= 0, axis=3)
    first = jnp.cumsum(cnt, axis=2) - cnt
    start = jnp.minimum(first // SLOT_ALIGN * SLOT_ALIGN, cap - win)
    n_win = jnp.maximum(jnp.max((first + cnt - start + win - 1) // win, axis=0), 1)
    return jnp.concatenate([start.reshape(E, n_seq * nb), n_win.reshape(1, n_seq * nb)]).astype(jnp.int32)


def _moe_scatter(slot, slot3, ye, x, g2blk, row0, n_seq, t, final_g=None):
    E, _, D = ye.shape
    cap = EC_CAPACITY * t // N_EXPERTS
    win = min(cap, SCATTER_WINDOW)
    nb = t // BLK
    b0 = row0 // BLK
    final = final_g is not None
    tbl = _scatter_windows(slot, row0, n_seq, t, win)
    in_specs = [pl.BlockSpec((E, 1, BLK), lambda s, j, tb_: (0, 0, b0 + s * nb + j)),
                pl.BlockSpec((E, cap, D), lambda s, j, tb_: (0, s, 0)),
                pl.BlockSpec((BLK, D), lambda s, j, tb_: (b0 + s * nb + j, 0)),
                pl.BlockSpec((1, 1, D), lambda s, j, tb_: (b0 + s * nb + j, 0, 0))]
    args = [tbl, slot3, ye, x, g2blk]
    if final:
        in_specs.append(pl.BlockSpec((1, D), lambda s, j, tb_: (0, 0)))
        args.append(final_g.reshape(1, D))
        out_specs = pl.BlockSpec((BLK, D), lambda s, j, tb_: (s * nb + j, 0))
        out_shape = jax.ShapeDtypeStruct((n_seq * t, D), jnp.float32)
        aliases = {}
    else:
        out_specs = pl.BlockSpec((BLK, D), lambda s, j, tb_: (b0 + s * nb + j, 0))
        out_shape = jax.ShapeDtypeStruct(x.shape, jnp.float32)
        aliases = {3: 0}
    return pl.pallas_call(
        functools.partial(_moe_scatter_kernel, cap=cap, win=win, nb=nb, final=final),
        grid_spec=pltpu.PrefetchScalarGridSpec(
            num_scalar_prefetch=1, grid=(n_seq, nb), in_specs=in_specs, out_specs=out_specs,
            scratch_shapes=[pltpu.VMEM((BLK, D), jnp.float32)]),
        out_shape=out_shape,
        input_output_aliases=aliases,
        compiler_params=pltpu.CompilerParams(dimension_semantics=("arbitrary", "arbitrary"),
                                             vmem_limit_bytes=MOE_FFN_VMEM_BYTES),
        name="moe_scatter",
    )(*args)


def moe_layer(x, hn, affT, g2blk, wg, wu, wd, layer, n_ctx, l_lat, final_g=None):
    R = x.shape[0]
    r_ctx = n_ctx * BLK
    n_lat = (R - r_ctx) // l_lat
    slot = jnp.concatenate([_moe_select(affT, 0, n_ctx, BLK), _moe_select(affT, r_ctx, n_lat, l_lat)], axis=1)
    slot3 = slot[:, None, :]
    aff3 = affT[:, None, :]
    xc, gc = _moe_gather(slot, slot3, aff3, hn, 0, n_ctx, BLK)
    xl, gl = _moe_gather(slot, slot3, aff3, hn, r_ctx, n_lat, l_lat)
    yc, yl = _moe_ffn(xc, xl, gc, gl, wg, wu, wd, layer)
    if final_g is None:
        x = _moe_scatter(slot, slot3, yc, x, g2blk, 0, n_ctx, BLK)
        return _moe_scatter(slot, slot3, yl, x, g2blk, r_ctx, n_lat, l_lat)
    return (_moe_scatter(slot, slot3, yc, x, g2blk, 0, n_ctx, BLK, final_g),
            _moe_scatter(slot, slot3, yl, x, g2blk, r_ctx, n_lat, l_lat, final_g))


L0_Z = (0, 1024)
L0_XBC = (1024, 2560)
L0_SHIFT = (2560, 6016)
L0_DT = (6016, 6144)
L0_COLS = 6144
L0_VMEM_BYTES = 56 * 1024 * 1024


def _sum_split(x, m, n_split):
    acc = None
    for _ in range(n_split):
        hi = x.astype(jnp.bfloat16)
        x = x - hi.astype(jnp.float32)
        t = jnp.dot(hi, m, preferred_element_type=jnp.float32)
        acc = t if acc is None else acc + t
    return acc


def _sum_split_left(m, x, n_split):
    acc = None
    for _ in range(n_split):
        hi = x.astype(jnp.bfloat16)
        x = x - hi.astype(jnp.float32)
        t = jnp.dot(m, hi, preferred_element_type=jnp.float32)
        acc = t if acc is None else acc + t
    return acc


def _head_sum(x, e_ref, et_ref):
    return _sum_split(_sum_split(x, e_ref[...], 2), et_ref[...], 2)


def _adaln(x, g, sc, sh):
    y = x * lax.rsqrt(jnp.mean(x * x, -1, keepdims=True) + NORM_EPS) * g
    return y * (1.0 + sc) + sh


def _softplus(x):
    return jnp.maximum(x, 0.0) + jnp.log(1.0 + jnp.exp(-jnp.abs(x)))


def _l0_in_kernel(tbl_ref, *refs, n_parts, n_a):
    x_refs, xp_refs, xn_refs = refs[:n_parts], refs[n_parts:2 * n_parts], refs[2 * n_parts:3 * n_parts]
    (g_ref, sc_ref, sh_ref, w_ref, mup_ref, mun_ref, rwp_ref, w2_ref, a2_ref, g2_ref, e_ref, et_ref,
     z_ref, xbc_ref, dt_ref, r_ref, v_ref, an_ref, lw_ref, kd_ref, bv_ref, gate_ref, bonus_ref) = refs[3 * n_parts:]
    i = pl.program_id(0)
    f32, bf16 = jnp.float32, jnp.bfloat16
    g, sc, sh = g_ref[...], sc_ref[0], sh_ref[0]
    hn = _adaln(_residual_rows(x_refs, n_a), g, sc, sh).astype(bf16)
    halo = _adaln(jnp.concatenate([_residual_rows(xp_refs, n_a), _residual_rows(xn_refs, n_a)], axis=0),
                  g, sc, sh).astype(bf16)
    hn_halo = jnp.concatenate([hn, halo], axis=0)

    keep_prev = (1 - tbl_ref[0, i]).astype(f32)
    keep_next = (1 - tbl_ref[1, i]).astype(f32)
    row = lax.broadcasted_iota(jnp.int32, (BLK, 1), 0)
    c = RWKV_DIM

    def plain(c0, c1):
        return jnp.dot(hn, w_ref[:, c0:c1], preferred_element_type=f32)

    def proj(c0, c1):
        return jnp.dot(hn_halo, w_ref[:, L0_SHIFT[0] + c0:L0_SHIFT[0] + c1], preferred_element_type=f32)

    def shift(both, c0, c1):
        cur = both[:BLK]
        prev = jnp.where(row == 0, both[BLK + 7:BLK + 8] * keep_prev, pltpu.roll(cur, 1, 0))
        nxt = jnp.where(row == BLK - 1, both[BLK + 8:BLK + 9] * keep_next, pltpu.roll(cur, BLK - 1, 0))
        return cur + mup_ref[:, c0:c1] * (prev - cur) + mun_ref[:, c0:c1] * (nxt - cur)

    lo = 3 * c + 2 * W_LORA
    p_wl = proj(3 * c, lo)
    p_ag = proj(lo, lo + 256)
    p_k = proj(c, 2 * c)
    wl = shift(p_wl, 3 * c, lo)
    ag = shift(p_ag, lo, lo + 256)
    p_r = proj(0, c)
    k = shift(p_k, c, 2 * c)
    k_k, k_a, r_k = rwp_ref[0:1], rwp_ref[1:2], rwp_ref[2:3]
    w_lin = jnp.dot(jnp.tanh(wl).astype(bf16), w2_ref[...], preferred_element_type=f32)
    a_lora = jnp.dot(ag.astype(bf16), a2_ref[...], preferred_element_type=f32)
    gate_ref[...] = jnp.dot(jax.nn.sigmoid(ag).astype(bf16), g2_ref[...],
                            preferred_element_type=f32).astype(gate_ref.dtype)
    p_v = proj(2 * c, 3 * c)
    r = shift(p_r, 0, c)
    r_ref[...] = r.astype(r_ref.dtype)
    kk = k * k_k
    kk = kk * lax.rsqrt(_head_sum(kk * kk, e_ref, et_ref) + 1e-12)
    an_ref[...] = (-kk).astype(an_ref.dtype)
    z_ref[...] = plain(L0_Z[0], L0_Z[1]).astype(z_ref.dtype)
    v = shift(p_v, 2 * c, 3 * c)
    v_ref[...] = v.astype(v_ref.dtype)
    kd_sum = None
    xbc_cols = (L0_XBC[0], (L0_XBC[0] + L0_XBC[1]) // 2, L0_XBC[1])
    for d in range(2):
        xbc_ref[:, xbc_cols[d] - L0_XBC[0]:xbc_cols[d + 1] - L0_XBC[0]] = plain(xbc_cols[d], xbc_cols[d + 1])
        w_log = -_softplus(-(rwp_ref[3 + d:4 + d] + w_lin[:, d * c:(d + 1) * c])) - 0.5
        lw_ref[d] = -jnp.exp(w_log)
        a = jax.nn.sigmoid(rwp_ref[5 + d:6 + d] + a_lora)
        kd = k * (1.0 + (a - 1.0) * k_a)
        kd_ref[d] = kd.astype(kd_ref.dtype)
        bv_ref[d] = (kk * a).astype(bv_ref.dtype)
        kd_sum = kd if kd_sum is None else kd_sum + kd
    dt_ref[...] = plain(L0_DT[0], L0_DT[1])
    bonus_ref[...] = (_head_sum(r * kd_sum * r_k, e_ref, et_ref) * v).astype(bonus_ref.dtype)


def _seq_tables(n_ctx, n_lat, l_lat):
    nb = l_lat // BLK
    cond = [0] * n_ctx + [1 + s for s in range(n_lat) for _ in range(nb)]
    first = [1] * n_ctx + [1 if j == 0 else 0 for _ in range(n_lat) for j in range(nb)]
    last = [1] * n_ctx + [1 if j == nb - 1 else 0 for _ in range(n_lat) for j in range(nb)]
    return cond, first, last


def l0_in(x, g1, scb, shb, w_packed, mup, mun, rwp, w2bd, a2p, g2p, e_ind, et_ind, first, last):
    xs = x if isinstance(x, (tuple, list)) else (x,)
    D = xs[0].shape[1]
    R = sum(a.shape[0] for a in xs)
    nblk = R // BLK
    n_a = xs[0].shape[0] // BLK
    tbl = jnp.asarray([first, last], jnp.int32)
    c = RWKV_DIM
    row = lambda i, t: (i, 0)
    full = lambda shape: pl.BlockSpec(shape, lambda i, t: (0,) * len(shape))
    rows = lambda n: pl.BlockSpec((BLK, n), row)
    rows2 = lambda n: pl.BlockSpec((2, BLK, n), lambda i, t: (0, i, 0))
    f32, act = jnp.float32, ACT_DTYPE
    sds = jax.ShapeDtypeStruct
    h8 = BLK // 8

    def part_specs(k):
        nb_k, b0 = xs[k].shape[0] // BLK, (0 if k == 0 else n_a)
        local = lambda i: jnp.clip(i - b0, 0, nb_k - 1)
        return (pl.BlockSpec((BLK, D), lambda i, t: (local(i), 0)),
                pl.BlockSpec((8, D), lambda i, t: (jnp.maximum(local(i) * h8 - 1, 0), 0)),
                pl.BlockSpec((8, D), lambda i, t: (jnp.minimum((local(i) + 1) * h8, nb_k * h8 - 1), 0)))

    specs = [part_specs(k) for k in range(len(xs))]
    x_specs = [s[j] for j in range(3) for s in specs]
    return pl.pallas_call(
        functools.partial(_l0_in_kernel, n_parts=len(xs), n_a=n_a),
        grid_spec=pltpu.PrefetchScalarGridSpec(
            num_scalar_prefetch=1,
            grid=(nblk,),
            in_specs=[*x_specs,
                      full((1, D)),
                      pl.BlockSpec((1, 1, D), lambda i, t: (i, 0, 0)),
                      pl.BlockSpec((1, 1, D), lambda i, t: (i, 0, 0)),
                      full((D, L0_COLS)), full(mup.shape), full(mun.shape), full(rwp.shape),
                      full(w2bd.shape), full(a2p.shape), full(g2p.shape), full(e_ind.shape), full(et_ind.shape)],
            out_specs=[rows(c), rows(SSD_XBC), rows(128), rows(c), rows(c), rows(c),
                       rows2(c), rows2(c), rows2(c), rows(c), rows(c)]),
        out_shape=[sds((R, c), act), sds((R, SSD_XBC), f32), sds((R, 128), f32), sds((R, c), act), sds((R, c), act),
                   sds((R, c), act), sds((2, R, c), f32), sds((2, R, c), act), sds((2, R, c), act),
                   sds((R, c), act), sds((R, c), act)],
        compiler_params=pltpu.CompilerParams(dimension_semantics=("arbitrary",), vmem_limit_bytes=L0_VMEM_BYTES),
        name="l0_in",
    )(tbl, *xs, *xs, *xs, g1, scb, shb, w_packed, mup, mun, rwp, w2bd, a2p, g2p, e_ind, et_ind)


def l0_pack_weights(p, e):
    bf16 = jnp.bfloat16
    w = p['ab_w_in'][e]
    D = w.shape[0]
    c = RWKV_DIM
    rw0 = SSD_IN
    ag0 = rw0 + 3 * c + 2 * W_LORA
    w_packed = jnp.concatenate([
        w[:, :SSD_INNER + SSD_XBC], w[:, rw0:ag0], w[:, ag0:ag0 + A_LORA + G_LORA],
        jnp.zeros((D, 256 - A_LORA - G_LORA), w.dtype),
        w[:, SSD_INNER + SSD_XBC:SSD_IN], jnp.zeros((D, 128 - SSD_HEADS), w.dtype)], axis=1).astype(bf16)

    def pack_mu(mu):
        return jnp.concatenate([mu, jnp.zeros((256 - A_LORA - G_LORA,), mu.dtype)])[None, :]

    rwp = jnp.stack([p['rwkv_k_k'][e], p['rwkv_k_a'][e], p['rwkv_r_k'][e].reshape(-1), p['rwkv_w0'][e, 0],
                     p['rwkv_w0'][e, 1], p['rwkv_a0'][e, 0], p['rwkv_a0'][e, 1], jnp.zeros((c,), jnp.float32)])
    zw = jnp.zeros((W_LORA, c), jnp.float32)
    w2bd = jnp.concatenate([jnp.concatenate([p['rwkv_w2'][e, 0], zw], axis=1),
                            jnp.concatenate([zw, p['rwkv_w2'][e, 1]], axis=1)], axis=0).astype(bf16)
    a2p = jnp.concatenate([p['rwkv_a2'][e], jnp.zeros((256 - A_LORA, c), jnp.float32)], axis=0).astype(bf16)
    g2p = jnp.concatenate([jnp.zeros((A_LORA, c), jnp.float32), p['rwkv_g2'][e],
                           jnp.zeros((256 - A_LORA - G_LORA, c), jnp.float32)], axis=0).astype(bf16)
    head = jnp.arange(c) // RWKV_N
    e_ind = (head[:, None] == jnp.arange(128)[None, :]).astype(bf16)
    return w_packed, pack_mu(p['rwkv_mu_prev'][e]), pack_mu(p['rwkv_mu_next'][e]), rwp, w2bd, a2p, g2p, e_ind, e_ind.T


SSD_QH = 4
SSD_VMEM_BYTES = 48 * 1024 * 1024
NEG_BIG = -1e30
LOG2E = 1.4426950408889634


def _conv_silu(cur, prev_row, next_row, w_ref, b_ref):
    row = lax.broadcasted_iota(jnp.int32, (BLK, 1), 0)
    prev = jnp.where(row == 0, prev_row, pltpu.roll(cur, 1, 0))
    nxt = jnp.where(row == BLK - 1, next_row, pltpu.roll(cur, BLK - 1, 0))
    y = w_ref[0:1] * prev + w_ref[1:2] * cur + w_ref[2:3] * nxt + b_ref[...]
    return y * jax.nn.sigmoid(y)


def _ssd_kernel(tbl_ref, xs_ref, b_ref, c_ref, dt_ref, cwx_ref, cwb_ref, cwc_ref, cbx_ref, cbb_ref, cbc_ref,
                sel_ref, hp_ref, s0_ref, y_ref, fs_ref, xa_ref, ba_ref, ca_ref, sfx_ref, ldb_ref, st_ref, *, n_zero, qpg):
    f32, bf16 = jnp.float32, jnp.bfloat16
    sb = pl.program_id(0)
    sbr, qw = xs_ref.shape
    nch = sbr // BLK
    P, QH = SSD_P, SSD_QH
    t_i = lax.broadcasted_iota(jnp.int32, (BLK, BLK), 0)
    s_i = lax.broadcasted_iota(jnp.int32, (BLK, BLK), 1)
    lower = s_i <= t_i
    upper = s_i >= t_i
    tri_lo = jnp.where(lower, 1.0, 0.0).astype(bf16)
    tri_up = jnp.where(upper, 1.0, 0.0).astype(bf16)
    hp = hp_ref[0]
    sel = sel_ref[0]
    ind = jnp.where(lax.broadcasted_iota(jnp.int32, (128, qw), 1) // P == lax.broadcasted_iota(jnp.int32, (128, qw), 0),
                    1.0, 0.0).astype(bf16)
    head_of_lane = lax.broadcasted_iota(jnp.int32, (1, qw), 1) // P

    def expand(cols):
        return _sum_split(cols, ind, 2)

    d_row = expand(jnp.broadcast_to(hp[4:5], (8, 128)))[0:1]

    def chunk_rows(c):
        return pl.ds(pl.multiple_of(c * BLK, BLK), BLK)

    def neighbours(ref, c, keep_prev, keep_next):
        lo = jnp.maximum(c * BLK - 1, 0)
        hi = jnp.minimum((c + 1) * BLK, sbr - 1)
        return ref[pl.ds(lo, 1), :] * keep_prev, ref[pl.ds(hi, 1), :] * keep_next

    def fwd(c, carry):
        blk = sb * nch + c
        first, last = tbl_ref[0, blk], tbl_ref[1, blk]
        kp, kn = (1 - first).astype(f32), (1 - last).astype(f32)
        rows = chunk_rows(c)

        @pl.when(first == 1)
        def _():
            st_ref[0] = jnp.where(sb >= n_zero, s0_ref[0, 0], 0.0)

        xa = _conv_silu(xs_ref[rows, :], *neighbours(xs_ref, c, kp, kn), cwx_ref, cbx_ref)
        xb = xa.astype(bf16)
        xa_ref[rows, :] = xb

        @pl.when(pl.program_id(1) % qpg == 0)
        def _():
            ba_ref[rows, :] = _conv_silu(b_ref[rows, :], *neighbours(b_ref, c, kp, kn), cwb_ref, cbb_ref).astype(bf16)
            ca_ref[rows, :] = _conv_silu(c_ref[rows, :], *neighbours(c_ref, c, kp, kn), cwc_ref, cbc_ref).astype(bf16)

        bmb, cmb = ba_ref[rows, :], ca_ref[rows, :]
        dtq = _sum_split(dt_ref[rows, :], sel, 2)
        dtf = _softplus(dtq + hp[0:1])
        dtb = _softplus(dtq + hp[1:2])
        acs = _sum_split_left(tri_lo, dtf * hp[2:3], 3)
        sfx = _sum_split_left(tri_up, dtb * hp[3:4], 3)
        ldf, ldb = jnp.log(dtf), jnp.log(dtb)
        sfx_ref[rows, :] = sfx
        ldb_ref[rows, :] = ldb
        a2, s2 = acs * LOG2E, sfx * LOG2E
        a2r = (a2 - ldf * LOG2E).T
        s2r = (s2 - ldb * LOG2E).T
        g = lax.dot_general(cmb, bmb, (((1,), (1,)), ((), ())), preferred_element_type=f32)
        y_diag = None
        for j in range(QH):
            m = (g * (jnp.exp2(jnp.where(lower, a2[:, j:j + 1] - a2r[j:j + 1, :], NEG_BIG))
                      + jnp.exp2(jnp.where(upper, s2[:, j:j + 1] - s2r[j:j + 1, :], NEG_BIG)))).astype(bf16)
            xh = jnp.where(head_of_lane == j, xb, jnp.zeros((), bf16))
            t = jnp.dot(m, xh, preferred_element_type=f32)
            y_diag = t if y_diag is None else y_diag + t
        ea = jnp.exp(expand(acs))
        wf = jnp.exp(expand(acs[BLK - 1:BLK] - acs + ldf))
        s_in = st_ref[0]
        y_ref[rows, :] = xa * d_row + y_diag + ea * jnp.dot(cmb, s_in.astype(bf16), preferred_element_type=f32)
        st_ref[0] = ea[BLK - 1:BLK] * s_in + lax.dot_general(
            bmb, (xa * wf).astype(bf16), (((0,), (0,)), ((), ())), preferred_element_type=f32)
        fs_ref[c, 0] = st_ref[0]
        return carry

    lax.fori_loop(0, nch, fwd, 0)

    def bwd(k, carry):
        c = nch - 1 - k
        blk = sb * nch + c
        rows = chunk_rows(c)

        @pl.when(tbl_ref[1, blk] == 1)
        def _():
            st_ref[1] = jnp.where(sb >= n_zero, s0_ref[0, 1], 0.0)

        sfx = sfx_ref[rows, :]
        eb = jnp.exp(expand(sfx))
        wb = jnp.exp(expand(sfx[0:1] - sfx + ldb_ref[rows, :]))
        s_in = st_ref[1]
        y_ref[rows, :] += eb * jnp.dot(ca_ref[rows, :], s_in.astype(bf16), preferred_element_type=f32)
        st_ref[1] = eb[0:1] * s_in + lax.dot_general(
            ba_ref[rows, :], (xa_ref[rows, :].astype(f32) * wb).astype(bf16), (((0,), (0,)), ((), ())),
            preferred_element_type=f32)
        fs_ref[c, 1] = st_ref[1]
        return carry

    lax.fori_loop(0, nch, bwd, 0)


def ssd_scan(xbc, dt, conv_w, conv_b, sel, hp, s0, n_zero, first, last, sb_rows):
    R = xbc.shape[0]
    n_sb = R // sb_rows
    nch = sb_rows // BLK
    nq = SSD_HEADS // SSD_QH
    qw = SSD_QH * SSD_P
    qpg = SSD_HPG // SSD_QH
    b_blk = SSD_INNER // SSD_N
    c_blk = b_blk + SSD_GROUPS
    tbl = jnp.asarray([first, last], jnp.int32)
    cw = conv_w
    cb = conv_b.reshape(1, -1)
    f32, bf16 = jnp.float32, jnp.bfloat16
    return pl.pallas_call(
        functools.partial(_ssd_kernel, n_zero=n_zero, qpg=qpg),
        grid_spec=pltpu.PrefetchScalarGridSpec(
            num_scalar_prefetch=1,
            grid=(n_sb, nq),
            in_specs=[pl.BlockSpec((sb_rows, qw), lambda s, q, t: (s, q)),
                      pl.BlockSpec((sb_rows, SSD_N), lambda s, q, t: (s, b_blk + q // qpg)),
                      pl.BlockSpec((sb_rows, SSD_N), lambda s, q, t: (s, c_blk + q // qpg)),
                      pl.BlockSpec((sb_rows, 128), lambda s, q, t: (s, 0)),
                      pl.BlockSpec((3, qw), lambda s, q, t: (0, q)),
                      pl.BlockSpec((3, SSD_N), lambda s, q, t: (0, b_blk + q // qpg)),
                      pl.BlockSpec((3, SSD_N), lambda s, q, t: (0, c_blk + q // qpg)),
                      pl.BlockSpec((1, qw), lambda s, q, t: (0, q)),
                      pl.BlockSpec((1, SSD_N), lambda s, q, t: (0, b_blk + q // qpg)),
                      pl.BlockSpec((1, SSD_N), lambda s, q, t: (0, c_blk + q // qpg)),
                      pl.BlockSpec((1, 128, 128), lambda s, q, t: (q, 0, 0)),
                      pl.BlockSpec((1, 8, 128), lambda s, q, t: (q, 0, 0)),
                      pl.BlockSpec((1, 2, SSD_N, qw), lambda s, q, t: (jnp.maximum(s - n_zero, 0), 0, 0, q))],
            out_specs=[pl.BlockSpec((sb_rows, qw), lambda s, q, t: (s, q)),
                       pl.BlockSpec((nch, 2, SSD_N, qw), lambda s, q, t: (s, 0, 0, q))],
            scratch_shapes=[pltpu.VMEM((sb_rows, qw), bf16), pltpu.VMEM((sb_rows, SSD_N), bf16),
                            pltpu.VMEM((sb_rows, SSD_N), bf16), pltpu.VMEM((sb_rows, 128), f32),
                            pltpu.VMEM((sb_rows, 128), f32), pltpu.VMEM((2, SSD_N, qw), f32)]),
        out_shape=[jax.ShapeDtypeStruct((R, SSD_INNER), f32),
                   jax.ShapeDtypeStruct((R // BLK, 2, SSD_N, SSD_INNER), f32)],
        compiler_params=pltpu.CompilerParams(dimension_semantics=("arbitrary", "arbitrary"),
                                             vmem_limit_bytes=SSD_VMEM_BYTES),
        name="ssd_scan",
    )(tbl, xbc, xbc, xbc, dt, cw, cw, cw, cb, cb, cb, sel, hp, s0)


def ssd_tables(p, e):
    nq = SSD_HEADS // SSD_QH
    lane = jnp.arange(128)
    sel = jnp.stack([(lane[:, None] == (q * SSD_QH + lane[None, :])) & (lane[None, :] < SSD_QH)
                     for q in range(nq)]).astype(jnp.bfloat16)
    a_neg = -jnp.exp(p['ssd_a_log'][e].astype(jnp.float32))
    rows = jnp.stack([p['ssd_dt_bias'][e, 0], p['ssd_dt_bias'][e, 1], a_neg[0], a_neg[1], p['ssd_d'][e]])
    hp = jnp.zeros((nq, 8, 128), jnp.float32)
    hp = hp.at[:, :5, :SSD_QH].set(jnp.transpose(rows.reshape(5, nq, SSD_QH), (1, 0, 2)))
    return sel, hp


MIX_VMEM_BYTES = 48 * 1024 * 1024
MIX_BLOCKS = 1
ROUTER_LANES = 128


def _residual_norm_router(x, out, g1, n2g, sc2, sh2, rw_ref, x_out_ref, hn_ref, aff_ref):
    rows, d = x.shape
    nb = g1.shape[0]
    x_new = x.reshape(nb, rows // nb, d) + g1 * out.reshape(nb, rows // nb, d)
    x_out_ref[...] = x_new.reshape(rows, d)
    hn = _adaln(x_new, n2g, sc2, sh2).reshape(rows, d)
    hn_ref[...] = hn.astype(hn_ref.dtype)
    logits = _dot(hn, rw_ref[...], ((1,), (0,)), 3)
    lane = lax.broadcasted_iota(jnp.int32, logits.shape, 1)
    logits = jnp.where(lane < N_EXPERTS, logits, NEG_BIG)
    ex = jnp.exp(logits - jnp.max(logits, axis=-1, keepdims=True))
    aff = ex / jnp.sum(ex, axis=-1, keepdims=True)
    aff_ref[...] = aff.T[:N_EXPERTS]


def _residual_rows(x_refs, n_a):
    if len(x_refs) == 1:
        return x_refs[0][...]
    return jnp.where(pl.program_id(0) < n_a, x_refs[0][...], x_refs[1][...])


def _l0_out_kernel(ys_ref, z_ref, yf_ref, yb_ref, bonus_ref, gate_ref, sg_ref, lnw_ref, lnb_ref, e_ref, et_ref, w_ref,
                   *rest, n_a):
    *x_refs, g1_ref, n2g_ref, sc2_ref, sh2_ref, rw_ref, x_out_ref, hn_ref, aff_ref = rest
    f32, bf16 = jnp.float32, jnp.bfloat16
    z = z_ref[...].astype(f32)
    ys = ys_ref[...] * (z * jax.nn.sigmoid(z))
    gw = SSD_INNER // SSD_GROUPS
    parts = []
    for gi in range(SSD_GROUPS):
        yg = ys[:, gi * gw:(gi + 1) * gw]
        parts.append(yg * lax.rsqrt(jnp.mean(yg * yg, -1, keepdims=True) + NORM_EPS))
    a1 = jnp.concatenate(parts, axis=1) * sg_ref[...]
    o = yf_ref[...].astype(f32) + yb_ref[...].astype(f32)
    mu = _head_sum(o, e_ref, et_ref) * (1.0 / RWKV_N)
    oc = o - mu
    var = _head_sum(oc * oc, e_ref, et_ref) * (1.0 / RWKV_N)
    o = oc * lax.rsqrt(var + RWKV_GN_EPS) * lnw_ref[...] + lnb_ref[...]
    o = (o + bonus_ref[...].astype(f32)) * gate_ref[...].astype(f32)
    out = (jnp.dot(a1.astype(bf16), w_ref[:SSD_INNER], preferred_element_type=f32)
           + jnp.dot(o.astype(bf16), w_ref[SSD_INNER:], preferred_element_type=f32))
    _residual_norm_router(_residual_rows(x_refs, n_a), out, g1_ref[...], n2g_ref[...], sc2_ref[...], sh2_ref[...],
                          rw_ref, x_out_ref, hn_ref, aff_ref)


def _l1_out_kernel(a_ref, w_ref, *rest, n_a):
    *x_refs, g1_ref, n2g_ref, sc2_ref, sh2_ref, rw_ref, x_out_ref, hn_ref, aff_ref = rest
    out = jnp.dot(a_ref[...].astype(jnp.bfloat16), w_ref[...], preferred_element_type=jnp.float32)
    _residual_norm_router(_residual_rows(x_refs, n_a), out, g1_ref[...], n2g_ref[...], sc2_ref[...], sh2_ref[...],
                          rw_ref, x_out_ref, hn_ref, aff_ref)


def _mix_out_call(kernel_fn, name, lead_args, lead_specs, w_out, x, g1b, n2g, sc2b, sh2b, router_w):
    mb = MIX_BLOCKS
    xs = x if isinstance(x, (tuple, list)) else (x,)
    D = xs[0].shape[1]
    R = sum(a.shape[0] for a in xs)
    n_a = xs[0].shape[0] // (mb * BLK)
    assert (R // BLK) % mb == 0 and xs[0].shape[0] % (mb * BLK) == 0
    full = lambda a: pl.BlockSpec(a.shape, lambda i: (0,) * a.ndim)
    blkrow = pl.BlockSpec((mb, 1, D), lambda i: (i, 0, 0))
    rw = jnp.zeros((D, ROUTER_LANES), jnp.float32).at[:, :N_EXPERTS].set(router_w)
    args = list(lead_args) + [w_out, *xs, g1b, n2g, sc2b, sh2b, rw]
    if len(xs) == 1:
        x_specs = [pl.BlockSpec((mb * BLK, D), lambda i: (i, 0))]
        aliases = {len(lead_args) + 1: 0}
    else:
        x_specs = [pl.BlockSpec((mb * BLK, D), lambda i: (jnp.minimum(i, n_a - 1), 0)),
                   pl.BlockSpec((mb * BLK, D), lambda i: (jnp.maximum(i - n_a, 0), 0))]
        aliases = {}
    in_specs = list(lead_specs) + [full(w_out), *x_specs, blkrow, full(n2g), blkrow, blkrow, full(rw)]
    return pl.pallas_call(
        functools.partial(kernel_fn, n_a=n_a),
        grid=(R // (mb * BLK),),
        in_specs=in_specs,
        out_specs=[pl.BlockSpec((mb * BLK, D), lambda i: (i, 0)), pl.BlockSpec((mb * BLK, D), lambda i: (i, 0)),
                   pl.BlockSpec((N_EXPERTS, mb * BLK), lambda i: (0, i))],
        out_shape=[jax.ShapeDtypeStruct((R, D), jnp.float32), jax.ShapeDtypeStruct((R, D), jnp.bfloat16),
                   jax.ShapeDtypeStruct((N_EXPERTS, R), jnp.float32)],
        input_output_aliases=aliases,
        compiler_params=pltpu.CompilerParams(dimension_semantics=("arbitrary",), vmem_limit_bytes=MIX_VMEM_BYTES),
        name=name,
    )(*args)


def l0_out(ys, z, yf, yb, bonus, gate, ssd_g, ln_w, ln_b, e_ind, et_ind, w_out, x, g1b, n2g, sc2b, sh2b, router_w):
    c = RWKV_DIM
    rows = lambda n: pl.BlockSpec((MIX_BLOCKS * BLK, n), lambda i: (i, 0))
    full = lambda a: pl.BlockSpec(a.shape, lambda i: (0,) * a.ndim)
    lead = [ys, z, yf, yb, bonus, gate, ssd_g, ln_w, ln_b, e_ind, et_ind]
    specs = [rows(SSD_INNER), rows(SSD_INNER), rows(c), rows(c), rows(c), rows(c),
             full(ssd_g), full(ln_w), full(ln_b), full(e_ind), full(et_ind)]
    return _mix_out_call(_l0_out_kernel, "l0_out", lead, specs, w_out, x, g1b, n2g, sc2b, sh2b, router_w)


def l1_out(a, w_out, x, g1b, n2g, sc2b, sh2b, router_w):
    specs = [pl.BlockSpec((MIX_BLOCKS * BLK, a.shape[1]), lambda i: (i, 0))]
    return _mix_out_call(_l1_out_kernel, "l1_out", [a], specs, w_out, x, g1b, n2g, sc2b, sh2b, router_w)


RET_HPS = 2


def _l1_in_kernel(tbl_ref, x_ref, g_ref, sc_ref, sh_ref, w_ref, cos_ref, sin_ref, q_ref, k_ref, v_ref, gg_ref):
    f32 = jnp.float32
    hn = _adaln(x_ref[...], g_ref[...], sc_ref[0], sh_ref[0]).astype(jnp.bfloat16)
    cosf, sinf = cos_ref[0], sin_ref[0]

    def rope(x):
        parts = []
        for h in range(RET_HEADS):
            xh = x[:, h * RET_DK:(h + 1) * RET_DK]
            parts.append(xh * cosf + pltpu.roll(xh, RET_DK // 2, 1) * sinf)
        return jnp.concatenate(parts, axis=1)

    q_ref[...] = rope(jnp.dot(hn, w_ref[:, :RET_QK], preferred_element_type=f32)).astype(q_ref.dtype)
    k_ref[...] = (rope(jnp.dot(hn, w_ref[:, RET_QK:2 * RET_QK], preferred_element_type=f32))
                  * (RET_DK ** -0.5)).astype(k_ref.dtype)
    v_ref[...] = jnp.dot(hn, w_ref[:, 2 * RET_QK:2 * RET_QK + RET_V], preferred_element_type=f32).astype(v_ref.dtype)
    gg_ref[...] = jnp.dot(hn, w_ref[:, 2 * RET_QK + RET_V:], preferred_element_type=f32).astype(gg_ref.dtype)


def l1_in(x, g1, scb, shb, w_bf16, cos_t, sin_t, rope_blk):
    R, D = x.shape
    nblk = R // BLK
    tbl = jnp.asarray([rope_blk], jnp.int32)
    f32 = jnp.float32
    row = lambda n: pl.BlockSpec((BLK, n), lambda i, t: (i, 0))
    full = lambda a: pl.BlockSpec(a.shape, lambda i, t: (0,) * a.ndim)
    blkrow = pl.BlockSpec((1, 1, D), lambda i, t: (i, 0, 0))
    ropespec = pl.BlockSpec((1, BLK, RET_DK), lambda i, t: (t[0, i], 0, 0))
    return pl.pallas_call(
        _l1_in_kernel,
        grid_spec=pltpu.PrefetchScalarGridSpec(
            num_scalar_prefetch=1, grid=(nblk,),
            in_specs=[row(D), full(g1), blkrow, blkrow, full(w_bf16), ropespec, ropespec],
            out_specs=[row(RET_QK), row(RET_QK), row(RET_V), row(RET_V)]),
        out_shape=[jax.ShapeDtypeStruct((R, RET_QK), ACT_DTYPE), jax.ShapeDtypeStruct((R, RET_QK), ACT_DTYPE),
                   jax.ShapeDtypeStruct((R, RET_V), ACT_DTYPE), jax.ShapeDtypeStruct((R, RET_V), ACT_DTYPE)],
        compiler_params=pltpu.CompilerParams(dimension_semantics=("arbitrary",), vmem_limit_bytes=L0_VMEM_BYTES),
        name="l1_in",
    )(tbl, x, g1, scb, shb, w_bf16, cos_t, sin_t)


def _ret_kernel(tbl_ref, q_ref, k_ref, v_ref, g_ref, lg_ref, nw_ref, nb_ref, s0_ref, a_ref, fs_ref, st_ref, *,
                n_zero):
    f32, bf16 = jnp.float32, jnp.bfloat16
    sb = pl.program_id(0)
    nch = q_ref.shape[0] // BLK
    heads = range(RET_HPS)
    ks = [slice(h * RET_DK, (h + 1) * RET_DK) for h in heads]
    vs = [slice(h * RET_DV, (h + 1) * RET_DV) for h in heads]
    lgf = [lg_ref[h, 0:1, 0:1] for h in heads]
    lgb = [lg_ref[h, 1:2, 0:1] for h in heads]
    t_i = lax.broadcasted_iota(jnp.int32, (BLK, BLK), 0)
    s_i = lax.broadcasted_iota(jnp.int32, (BLK, BLK), 1)
    dist = (t_i - s_i).astype(f32)
    dm = [jnp.exp(jnp.where(s_i <= t_i, dist * lgf[h], NEG_BIG)) + jnp.exp(jnp.where(s_i >= t_i, -dist * lgb[h], NEG_BIG))
          for h in heads]
    tk = lax.broadcasted_iota(jnp.int32, (BLK, RET_DK), 0).astype(f32)
    tv = lax.broadcasted_iota(jnp.int32, (BLK, RET_DV), 0).astype(f32)
    k_to_end_f = [jnp.exp((BLK - 1.0 - tk) * lgf[h]) for h in heads]
    k_to_end_b = [jnp.exp(tk * lgb[h]) for h in heads]
    from_start_f = [jnp.exp((tv + 1.0) * lgf[h]) for h in heads]
    from_start_b = [jnp.exp((BLK - tv) * lgb[h]) for h in heads]
    nt, tn = (((1,), (1,)), ((), ())), (((0,), (0,)), ((), ()))

    def chunk_rows(c):
        return pl.ds(pl.multiple_of(c * BLK, BLK), BLK)

    def fwd(c, carry):
        blk = sb * nch + c
        rows = chunk_rows(c)

        @pl.when(tbl_ref[0, blk] == 1)
        def _():
            st_ref[0] = jnp.where(sb >= n_zero, s0_ref[0, 0], 0.0)

        q = [q_ref[rows, ks[h]].astype(bf16) for h in heads]
        k = [k_ref[rows, ks[h]] for h in heads]
        v = [v_ref[rows, vs[h]].astype(bf16) for h in heads]
        s_in = [st_ref[0, h] for h in heads]
        g = [lax.dot_general(q[h], k[h].astype(bf16), nt, preferred_element_type=f32) for h in heads]
        y_diag = [jnp.dot((g[h] * dm[h]).astype(bf16), v[h], preferred_element_type=f32) for h in heads]
        y_off = [jnp.dot(q[h], s_in[h].astype(bf16), preferred_element_type=f32) for h in heads]
        kw = [(k[h].astype(f32) * k_to_end_f[h]).astype(bf16) for h in heads]
        upd = [lax.dot_general(kw[h], v[h], tn, preferred_element_type=f32) for h in heads]
        for h in heads:
            a_ref[rows, vs[h]] = y_diag[h] + from_start_f[h] * y_off[h]
            st_ref[0, h] = jnp.exp(BLK * lgf[h]) * s_in[h] + upd[h]
        fs_ref[c, 0] = st_ref[0]
        return carry

    lax.fori_loop(0, nch, fwd, 0)

    def bwd(j, carry):
        c = nch - 1 - j
        blk = sb * nch + c
        rows = chunk_rows(c)

        @pl.when(tbl_ref[1, blk] == 1)
        def _():
            st_ref[1] = jnp.where(sb >= n_zero, s0_ref[0, 1], 0.0)

        q = [q_ref[rows, ks[h]].astype(bf16) for h in heads]
        v = [v_ref[rows, vs[h]].astype(bf16) for h in heads]
        s_in = [st_ref[1, h] for h in heads]
        y_off = [jnp.dot(q[h], s_in[h].astype(bf16), preferred_element_type=f32) for h in heads]
        kw = [(k_ref[rows, ks[h]].astype(f32) * k_to_end_b[h]).astype(bf16) for h in heads]
        upd = [lax.dot_general(kw[h], v[h], tn, preferred_element_type=f32) for h in heads]
        for h in heads:
            st_ref[1, h] = jnp.exp(BLK * lgb[h]) * s_in[h] + upd[h]
            y = a_ref[rows, vs[h]] + from_start_b[h] * y_off[h]
            mu = jnp.mean(y, -1, keepdims=True)
            yc = y - mu
            var = jnp.mean(yc * yc, -1, keepdims=True)
            gg = g_ref[rows, vs[h]].astype(f32)
            a_ref[rows, vs[h]] = ((yc * lax.rsqrt(var + 1e-5) * nw_ref[:, vs[h]] + nb_ref[:, vs[h]])
                                  * (gg * jax.nn.sigmoid(gg)))
        fs_ref[c, 1] = st_ref[1]
        return carry

    lax.fori_loop(0, nch, bwd, 0)


def ret_scan(q, k, v, g, lg_tab, norm_w, norm_b, s0, n_zero, first, last, sb_rows):
    R = q.shape[0]
    n_sb = R // sb_rows
    nch = sb_rows // BLK
    tbl = jnp.asarray([first, last], jnp.int32)
    f32 = jnp.float32
    hps = RET_HPS
    return pl.pallas_call(
        functools.partial(_ret_kernel, n_zero=n_zero),
        grid_spec=pltpu.PrefetchScalarGridSpec(
            num_scalar_prefetch=1, grid=(n_sb, RET_HEADS // hps),
            in_specs=[pl.BlockSpec((sb_rows, hps * RET_DK), lambda s, h, t: (s, h)),
                      pl.BlockSpec((sb_rows, hps * RET_DK), lambda s, h, t: (s, h)),
                      pl.BlockSpec((sb_rows, hps * RET_DV), lambda s, h, t: (s, h)),
                      pl.BlockSpec((sb_rows, hps * RET_DV), lambda s, h, t: (s, h)),
                      pl.BlockSpec((hps, 8, 128), lambda s, h, t: (h, 0, 0)),
                      pl.BlockSpec((1, hps * RET_DV), lambda s, h, t: (0, h)),
                      pl.BlockSpec((1, hps * RET_DV), lambda s, h, t: (0, h)),
                      pl.BlockSpec((1, 2, hps, RET_DK, RET_DV), lambda s, h, t: (jnp.maximum(s - n_zero, 0), 0, h, 0, 0))],
            out_specs=[pl.BlockSpec((sb_rows, hps * RET_DV), lambda s, h, t: (s, h)),
                       pl.BlockSpec((nch, 2, hps, RET_DK, RET_DV), lambda s, h, t: (s, 0, h, 0, 0))],
            scratch_shapes=[pltpu.VMEM((2, hps, RET_DK, RET_DV), f32)]),
        out_shape=[jax.ShapeDtypeStruct((R, RET_V), f32),
                   jax.ShapeDtypeStruct((R // BLK, 2, RET_HEADS, RET_DK, RET_DV), f32)],
        compiler_params=pltpu.CompilerParams(dimension_semantics=("arbitrary", "arbitrary"),
                                             vmem_limit_bytes=SSD_VMEM_BYTES),
        name="ret_scan",
    )(tbl, q, k, v, g, lg_tab, norm_w.reshape(1, -1), norm_b.reshape(1, -1), s0)


def _mod_kernel(c_ref, w_ref, b_ref, o_ref):
    c = c_ref[...]
    act = c * jax.nn.sigmoid(c)
    o_ref[0] = _dot(act, w_ref[0], ((1,), (0,)), 3) + b_ref[0]


def mod_vectors(conds, mod_w, mod_b):
    depth, D, n6 = mod_w.shape
    tn = D
    return pl.pallas_call(
        _mod_kernel,
        grid=(depth, n6 // tn),
        in_specs=[pl.BlockSpec(conds.shape, lambda i, j: (0, 0)),
                  pl.BlockSpec((1, D, tn), lambda i, j: (i, 0, j)),
                  pl.BlockSpec((1, 1, tn), lambda i, j: (i, 0, j))],
        out_specs=pl.BlockSpec((1, conds.shape[0], tn), lambda i, j: (i, 0, j)),
        out_shape=jax.ShapeDtypeStruct((depth, conds.shape[0], n6), jnp.float32),
        name="mod_vectors",
    )(conds, mod_w, mod_b.reshape(depth, 1, n6))


def rope_tables(n_tokens):
    rows = n_tokens // GRID_W
    row = np.repeat(np.arange(rows), GRID_W).astype(np.float64)
    col = np.tile(np.arange(GRID_W), rows).astype(np.float64)
    n_f = RET_DK // 4
    inv = ROPE_BASE ** (-np.arange(n_f, dtype=np.float64) / n_f)
    ang = np.concatenate([row[:, None] * inv, col[:, None] * inv], -1)
    return np.cos(ang), np.sin(ang)


def _rope_block_tables(n_ctx, n_lat, l_lat):
    nb = l_lat // BLK
    cos, sin = rope_tables(l_lat)
    cosf = np.concatenate([cos, cos], -1).reshape(nb, BLK, RET_DK)
    sinf = np.concatenate([-sin, sin], -1).reshape(nb, BLK, RET_DK)
    cos_t = jnp.asarray(np.concatenate([np.ones((1, BLK, RET_DK)), cosf]), jnp.float32)
    sin_t = jnp.asarray(np.concatenate([np.zeros((1, BLK, RET_DK)), sinf]), jnp.float32)
    rope_blk = [0] * n_ctx + [1 + j for _ in range(n_lat) for j in range(nb)]
    return cos_t, sin_t, rope_blk


def kernel(x_prompt, x_sample, state_ssd, state_rwkv, state_ret, c, c_ctx, mod_w, mod_b, norm1_g, norm2_g,
           router_w, exp_w_gate, exp_w_up, exp_w_down, ab_w_in, ab_w_out, ssd_conv_w, ssd_conv_b, ssd_dt_bias,
           ssd_a_log, ssd_d, ssd_norm_g, rwkv_mu_prev, rwkv_mu_next, rwkv_w0, rwkv_w2, rwkv_a0, rwkv_a2, rwkv_g2,
           rwkv_k_k, rwkv_k_a, rwkv_r_k, rwkv_ln_w, rwkv_ln_b, ret_w_in, ret_w_out, ret_decay_logit, ret_norm_w,
           ret_norm_b, final_norm_g):
    p = dict(mod_w=mod_w, mod_b=mod_b, norm1_g=norm1_g, norm2_g=norm2_g, router_w=router_w,
             exp_w_gate=exp_w_gate, exp_w_up=exp_w_up, exp_w_down=exp_w_down, ab_w_in=ab_w_in, ab_w_out=ab_w_out,
             ssd_conv_w=ssd_conv_w, ssd_conv_b=ssd_conv_b, ssd_dt_bias=ssd_dt_bias, ssd_a_log=ssd_a_log,
             ssd_d=ssd_d, ssd_norm_g=ssd_norm_g, rwkv_mu_prev=rwkv_mu_prev, rwkv_mu_next=rwkv_mu_next,
             rwkv_w0=rwkv_w0, rwkv_w2=rwkv_w2, rwkv_a0=rwkv_a0, rwkv_a2=rwkv_a2, rwkv_g2=rwkv_g2,
             rwkv_k_k=rwkv_k_k, rwkv_k_a=rwkv_k_a, rwkv_r_k=rwkv_r_k, rwkv_ln_w=rwkv_ln_w, rwkv_ln_b=rwkv_ln_b,
             ret_w_in=ret_w_in, ret_w_out=ret_w_out, ret_decay_logit=ret_decay_logit, ret_norm_w=ret_norm_w,
             ret_norm_b=ret_norm_b, final_norm_g=final_norm_g)
    f32, bf16 = jnp.float32, jnp.bfloat16
    n_ctx, l_ctx, D = x_prompt.shape
    n_lat, l_lat, _ = x_sample.shape
    assert l_ctx == BLK and l_lat % BLK == 0 and (n_ctx * BLK) % l_lat == 0
    n_sb_ctx = n_ctx * BLK // l_lat
    cond_id, first, last = _seq_tables(n_ctx, n_lat, l_lat)
    x = (x_prompt.reshape(-1, D), x_sample.reshape(-1, D))

    conds = jnp.concatenate([c_ctx[None, :], c, jnp.zeros((8 - 1 - n_lat, D), f32)])
    mods = mod_vectors(conds, mod_w, mod_b)[:, jnp.asarray(cond_id)]
    mods = mods.reshape(DEPTH, len(cond_id), 6, 1, D)

    new_ssd, new_rwkv, new_ret = [], [], []
    out = None
    for i in range(DEPTH):
        sh1, sc1, g1, sh2, sc2, g2 = (mods[i, :, k] for k in range(6))
        e = i // 2
        if i % 2 == 0:
            w_packed, mup, mun, rwp, w2bd, a2p, g2p, e_ind, et_ind = l0_pack_weights(p, e)
            z, xbc, dt, r, v, an, lw, kd, bv, gate, bonus = l0_in(
                x, norm1_g[i][None], sc1, sh1, w_packed, mup, mun, rwp, w2bd, a2p, g2p, e_ind, et_ind, first, last)
            sel, hp = ssd_tables(p, e)
            s0_ssd = jnp.transpose(state_ssd[:, e], (0, 1, 3, 2, 4)).reshape(n_lat, 2, SSD_N, SSD_INNER)
            ys, fs_ssd = ssd_scan(xbc, dt, ssd_conv_w[e], ssd_conv_b[e], sel, hp,
                                  s0_ssd, n_sb_ctx, first, last, l_lat)
            new_ssd.append(jnp.transpose(fs_ssd[:n_ctx].reshape(n_ctx, 2, SSD_N, SSD_HEADS, SSD_P), (0, 1, 3, 2, 4)))
            s0_rwkv = jnp.transpose(state_rwkv[:, e], (0, 1, 3, 2, 4)).reshape(n_lat, 2, RWKV_N, RWKV_DIM)
            yf, yb, sf_rwkv = rwkv_scan_pallas(r, v, an, lw, kd, bv, s0_rwkv, n_ctx,
                                               _rwkv_steps(n_ctx, n_lat, l_lat))
            new_rwkv.append(jnp.transpose(sf_rwkv[:n_ctx].reshape(n_ctx, 2, RWKV_N, RWKV_HEADS, RWKV_N),
                                          (0, 1, 3, 2, 4)))
            x, hn2, affT = l0_out(ys, z, yf, yb, bonus, gate, ssd_norm_g[e][None], rwkv_ln_w[e][None], rwkv_ln_b[e][None],
                                  e_ind, et_ind, ab_w_out[e].astype(bf16), x, g1, norm2_g[i][None], sc2, sh2,
                                  router_w[i])
        else:
            cos_t, sin_t, rope_blk = _rope_block_tables(n_ctx, n_lat, l_lat)
            q, k, v, gg = l1_in(x, norm1_g[i][None], sc1, sh1, ret_w_in[e].astype(bf16), cos_t, sin_t, rope_blk)
            lg = jax.nn.log_sigmoid(ret_decay_logit[e].astype(f32))
            lg_tab = jnp.zeros((RET_HEADS, 8, 128), f32).at[:, :2, :].set(jnp.transpose(lg)[:, :, None])
            a, fs_ret = ret_scan(q, k, v, gg, lg_tab, ret_norm_w[e], ret_norm_b[e],
                                 state_ret[:, e], n_sb_ctx, first, last, l_lat)
            new_ret.append(fs_ret[:n_ctx])
            x, hn2, affT = l1_out(a, ret_w_out[e].astype(bf16), x, g1, norm2_g[i][None], sc2, sh2, router_w[i])
        fin = final_norm_g if i == DEPTH - 1 else None
        out = moe_layer(x, hn2, affT, g2, exp_w_gate, exp_w_up, exp_w_down, i, n_ctx, l_lat, final_g=fin)
        if fin is None:
            x = out
    y_ctx, y_lat = out
    return (y_ctx.reshape(n_ctx, l_ctx, D), y_lat.reshape(n_lat, l_lat, D),
            jnp.stack(new_ssd, 1), jnp.stack(new_rwkv, 1), jnp.stack(new_ret, 1))
```

```python
import functools
import math

import jax
import jax.numpy as jnp
import numpy as np
from jax import lax
from jax.experimental import pallas as pl
from jax.experimental.pallas import tpu as pltpu

DEPTH = 2
GRID_W = 64
NORM_EPS = 1e-6
SSD_HEADS = 16
SSD_P = 64
SSD_INNER = SSD_HEADS * SSD_P
SSD_GROUPS = 2
SSD_HPG = SSD_HEADS // SSD_GROUPS
SSD_N = 128
SSD_XBC = SSD_INNER + 2 * SSD_GROUPS * SSD_N
SSD_IN = SSD_INNER + SSD_XBC + SSD_HEADS
RWKV_HEADS = 16
RWKV_N = 64
RWKV_DIM = RWKV_HEADS * RWKV_N
W_LORA = 64
A_LORA = 64
G_LORA = 128
RWKV_GN_EPS = 64e-5
RET_HEADS = 8
RET_DK = 128
RET_DV = 256
RET_QK = RET_HEADS * RET_DK
RET_V = RET_HEADS * RET_DV
ROPE_BASE = 10000.0
N_EXPERTS = 16
EC_CAPACITY = 2

ACT_DTYPE = jnp.bfloat16

RWKV_C = 64
RWKV_GH = 4
RWKV_GL = RWKV_GH * RWKV_N
RWKV_DOUBLING_PASSES = (3, 3, 3, 3, 1, 1)


def _split_bf16(x):
    hi = x.astype(jnp.bfloat16)
    lo = (x - hi.astype(jnp.float32)).astype(jnp.bfloat16)
    return hi, lo


def _dot(a, b, dims, passes):
    f = functools.partial(lax.dot_general, dimension_numbers=(dims, ((), ())),
                          preferred_element_type=jnp.float32)
    if passes == 1:
        return f(a.astype(jnp.bfloat16), b.astype(jnp.bfloat16))
    ah, al = _split_bf16(a)
    bh, bl = _split_bf16(b)
    return f(ah, bh) + (f(ah, bl) + f(al, bh))


def _rwkv_chunk_kernel(tbl_ref, r0_ref, v0_ref, a0_ref, r1_ref, v1_ref, a1_ref, lw0_ref, k0_ref, b0_ref,
                       lw1_ref, k1_ref, b1_ref, s0_ref, y0_ref, y1_ref, sf_ref, h_ref, *, p_inv, p_oth, n_zero):
    C, N, GH, GL = RWKV_C, RWKV_N, RWKV_GH, RWKV_GL
    i = pl.program_id(0)
    f32, bf16 = jnp.float32, jnp.bfloat16

    @pl.when(tbl_ref[3, i] == 1)
    def _():
        h_ref[...] = jnp.where(tbl_ref[2, i] >= n_zero, s0_ref[0], 0.0)

    t_i = lax.broadcasted_iota(jnp.int32, (C, GL), 0)
    s_i = lax.broadcasted_iota(jnp.int32, (C, GL), 1) & (N - 1)
    eye = (s_i == t_i).astype(f32)
    row2 = lax.broadcasted_iota(jnp.int32, (2 * C, GL), 0)
    rel2 = (lax.broadcasted_iota(jnp.int32, (2 * C, GL), 1) & (N - 1)) - (row2 & (C - 1))
    incl2 = row2 // C
    mask2 = [rel2 - incl2 < 0, -rel2 - incl2 < 0]
    bh_r = lax.broadcasted_iota(jnp.int32, (GL, GL), 0) // N
    bh_c = lax.broadcasted_iota(jnp.int32, (GL, GL), 1) // N
    blk = bh_r == bh_c
    tt = lax.broadcasted_iota(jnp.int32, (C, C), 0)
    ss = lax.broadcasted_iota(jnp.int32, (C, C), 1)
    tri = [(ss <= tt).astype(bf16), (ss >= tt).astype(bf16)]

    def bd(x, passes):
        pieces = []
        for _ in range(2 if passes == 3 else 1):
            hi = x.astype(bf16)
            x = x - hi.astype(f32)
            pieces.append(jnp.where(blk, jnp.concatenate([hi] * GH, axis=0), jnp.zeros((), bf16)))
        return pieces

    def mm(l, x, passes, dims=((1,), (0,))):
        f = functools.partial(lax.dot_general, dimension_numbers=(dims, ((), ())), preferred_element_type=f32)
        xs = bd(x, passes)
        lh = l.astype(bf16)
        if passes == 1:
            return f(lh, xs[0])
        ll = (l - lh.astype(f32)).astype(bf16)
        m = l.shape[0]
        both = f(jnp.concatenate([lh, ll], axis=0), xs[0])
        if passes == 2:
            return both[:m] + both[m:]
        return both[:m] + (f(lh, xs[1]) + both[m:])

    nt = ((1,), (1,))
    refs = [(r0_ref, v0_ref, a0_ref, lw0_ref, k0_ref, b0_ref), (r1_ref, v1_ref, a1_ref, lw1_ref, k1_ref, b1_ref)]
    lw, r_t, a_t, b_t, k_t, v = [], [], [], [], [], []
    for d, (r_ref, v_ref, a_ref, lw_ref, k_ref, b_ref) in enumerate(refs):
        lwd = lw_ref[0]
        lw_hi, lw_lo = _split_bf16(lwd)
        cum = (jnp.dot(tri[d], lw_hi, preferred_element_type=f32) + jnp.dot(tri[d], lw_lo, preferred_element_type=f32))
        w_inv = jnp.exp(-cum)
        lw.append(lwd)
        r_t.append(r_ref[...].astype(f32) * jnp.exp(cum))
        a_t.append(a_ref[...].astype(f32) * jnp.exp(cum - lwd))
        b_t.append(b_ref[0].astype(f32) * w_inv)
        k_t.append(k_ref[0].astype(f32) * w_inv)
        v.append(v_ref[...].astype(f32))
    lane_head = lax.broadcasted_iota(jnp.int32, (N, GL), 1) // N

    chains = [(d, slice(g * GL, (g + 1) * GL)) for d in range(2) for g in range(RWKV_HEADS // GH)]
    each = lambda fn: [fn(j, d, sl) for j, (d, sl) in enumerate(chains)]
    bg = each(lambda j, d, sl: b_t[d][:, sl])
    kg = each(lambda j, d, sl: k_t[d][:, sl])
    vg = each(lambda j, d, sl: v[d][:, sl])
    h0 = each(lambda j, d, sl: h_ref[d, :, sl])
    ar = each(lambda j, d, sl: jnp.concatenate([a_t[d][:, sl], r_t[d][:, sl]], axis=0))
    m_b = each(lambda j, d, sl: jnp.where(mask2[d], mm(ar[j], bg[j], p_oth, nt), 0.0))
    m_k = each(lambda j, d, sl: jnp.where(mask2[d], mm(ar[j], kg[j], p_oth, nt), 0.0))
    p = each(lambda j, d, sl: mm(m_b[j][:C], m_b[j][:C], p_inv[0]))
    tmat = each(lambda j, d, sl: eye + m_b[j][:C])
    for lv in range(int(math.log2(C)) - 2):
        pt = each(lambda j, d, sl: mm(jnp.concatenate([p[j], tmat[j]], axis=0), p[j], p_inv[1 + lv]))
        p = each(lambda j, d, sl: pt[j][:C])
        tmat = each(lambda j, d, sl: tmat[j] + pt[j][C:])
    tmat = each(lambda j, d, sl: tmat[j] + mm(tmat[j], p[j], p_inv[-1]))
    ar_h = each(lambda j, d, sl: mm(ar[j], h0[j], p_oth, nt))
    mk_v = each(lambda j, d, sl: mm(m_k[j], vg[j], p_oth))
    u = each(lambda j, d, sl: mm(tmat[j], ar_h[j][:C] + mk_v[j][:C], p_oth))
    y = each(lambda j, d, sl: ar_h[j][C:] + mm(m_b[j][C:], u[j], p_oth) + mk_v[j][C:])
    full = each(lambda j, d, sl: _dot(jnp.concatenate([u[j], vg[j]], axis=0), jnp.concatenate([bg[j], kg[j]], axis=0),
                                      ((0,), (0,)), p_oth))
    y_refs = (y0_ref, y1_ref)
    for j, (d, sl) in enumerate(chains):
        y_refs[d][:, sl] = y[j].astype(y_refs[d].dtype)
        z = jnp.zeros((N, GL), f32)
        for hh in range(GH):
            z = z + jnp.where(lane_head == hh, full[j][hh * N:(hh + 1) * N], 0.0)
        w_tot = jnp.exp(jnp.sum(lw[d][:, sl], axis=0, keepdims=True))
        h_ref[d, :, sl] = w_tot * (h0[j] + z)

    @pl.when(tbl_ref[4, i] == 1)
    def _():
        sf_ref[0] = h_ref[...]


def _rwkv_steps(n_ctx, n_lat, l_lat):
    C = RWKV_C
    rows = []
    seqs = [(s, s * BLK, BLK) for s in range(n_ctx)] + [(n_ctx + s, n_ctx * BLK + s * l_lat, l_lat) for s in range(n_lat)]
    for sid, row0, length in seqs:
        nc = length // C
        for j in range(nc):
            rows.append((row0 // C + j, row0 // C + nc - 1 - j, sid, int(j == 0), int(j == nc - 1)))
    return [list(col) for col in zip(*rows)]


def rwkv_scan_pallas(r, v, a, lw, k, b, s0, n_zero, steps, p_inv=RWKV_DOUBLING_PASSES, p_oth=1):
    R, HN = r.shape
    C, N = RWKV_C, RWKV_N
    tbl = jnp.asarray(steps, jnp.int32)
    fwd = pl.BlockSpec((C, HN), lambda i, t: (t[0, i], 0))
    bwd = pl.BlockSpec((C, HN), lambda i, t: (t[1, i], 0))
    fwd_d = pl.BlockSpec((1, C, HN), lambda i, t: (0, t[0, i], 0))
    bwd_d = pl.BlockSpec((1, C, HN), lambda i, t: (1, t[1, i], 0))
    n_seq = max(steps[2]) + 1
    st_in = pl.BlockSpec((1, 2, N, HN), lambda i, t: (jnp.maximum(t[2, i] - n_zero, 0), 0, 0, 0))
    st = pl.BlockSpec((1, 2, N, HN), lambda i, t: (t[2, i], 0, 0, 0))
    return pl.pallas_call(
        functools.partial(_rwkv_chunk_kernel, p_inv=p_inv, p_oth=p_oth, n_zero=n_zero),
        grid_spec=pltpu.PrefetchScalarGridSpec(
            num_scalar_prefetch=1,
            grid=(len(steps[0]),),
            in_specs=[fwd, fwd, fwd, bwd, bwd, bwd, fwd_d, fwd_d, fwd_d, bwd_d, bwd_d, bwd_d, st_in],
            out_specs=[fwd, bwd, st],
            scratch_shapes=[pltpu.VMEM((2, N, HN), jnp.float32)]),
        out_shape=[jax.ShapeDtypeStruct((R, HN), ACT_DTYPE), jax.ShapeDtypeStruct((R, HN), ACT_DTYPE),
                   jax.ShapeDtypeStruct((n_seq, 2, N, HN), jnp.float32)],
        compiler_params=pltpu.CompilerParams(dimension_semantics=("arbitrary",)),
        name="rwkv_scan",
    )(tbl, r, v, a, r, v, a, lw, k, b, lw, k, b, s0)


BLK = 256
FF_TILE = 768
SELECT_TILE = 512
SCATTER_WINDOW = 64
SLOT_ALIGN = 16
SELECT_MIN_EXP = -1100.0
SELECT_BINADE_STEPS = 11
SELECT_MANTISSA_STEPS = 40
MOE_FFN_VMEM_BYTES = 48 * 1024 * 1024


def _moe_select_kernel(aff_ref, slot_ref, *, cap):
    a = aff_ref[...]
    E, T = a.shape
    f32 = jnp.float32

    def enough(piv):
        return jnp.sum(jnp.where(a >= piv, 1.0, 0.0), axis=1, keepdims=True) >= cap

    def binade(_, lohi):
        e_lo, e_hi = lohi
        mid = jnp.floor((e_lo + e_hi) * 0.5)
        ok = enough(jnp.exp2(mid))
        return jnp.where(ok, mid, e_lo), jnp.where(ok, e_hi, mid)

    e_lo, e_hi = lax.fori_loop(0, SELECT_BINADE_STEPS, binade,
                               (jnp.full((E, 1), SELECT_MIN_EXP, f32), jnp.full((E, 1), 1.0, f32)))

    def inside(_, lohi):
        lo, hi = lohi
        mid = lo + (hi - lo) * 0.5
        ok = enough(mid)
        return jnp.where(ok, mid, lo), jnp.where(ok, hi, mid)

    thr, _ = lax.fori_loop(0, SELECT_MANTISSA_STEPS, inside, (jnp.exp2(e_lo), jnp.exp2(e_hi)))
    gt = a > thr
    eq = a == thr
    need = cap - jnp.sum(jnp.where(gt, 1.0, 0.0), axis=1, keepdims=True)
    tw = min(T, SELECT_TILE)

    def prefix_count(mask):
        m = jnp.where(mask, 1.0, 0.0).astype(jnp.bfloat16)
        outs = []
        for j in range(T // tw):
            s_i = lax.broadcasted_iota(jnp.int32, (T, tw), 0)
            t_i = lax.broadcasted_iota(jnp.int32, (T, tw), 1) + j * tw
            before = jnp.where(s_i < t_i, 1.0, 0.0).astype(jnp.bfloat16)
            outs.append(jnp.dot(m, before, preferred_element_type=f32))
        return outs[0] if len(outs) == 1 else jnp.concatenate(outs, axis=1)

    sel = gt | (eq & (prefix_count(eq) < need))
    slot_ref[...] = jnp.where(sel, prefix_count(sel).astype(jnp.int32), -1)


def _moe_select(affT, row0, n_seq, t):
    E = affT.shape[0]
    rows = jnp.transpose(affT[:, row0:row0 + n_seq * t].reshape(E, n_seq, t), (1, 0, 2)).reshape(n_seq * E, t)
    slot = pl.pallas_call(
        functools.partial(_moe_select_kernel, cap=EC_CAPACITY * t // N_EXPERTS),
        grid=(1,),
        in_specs=[pl.BlockSpec((n_seq * E, t), lambda s: (0, 0))],
        out_specs=pl.BlockSpec((n_seq * E, t), lambda s: (0, 0)),
        out_shape=jax.ShapeDtypeStruct((n_seq * E, t), jnp.int32),
        name="moe_select",
    )(rows)
    return jnp.transpose(slot.reshape(n_seq, E, t), (1, 0, 2)).reshape(E, n_seq * t)


def _moe_gather_kernel(tbl_ref, slot_ref, aff_ref, hn_ref, xe_ref, gate_ref, *, cap, win, nb):
    E = slot_ref.shape[0]
    s = pl.program_id(0)
    w_i = lax.broadcasted_iota(jnp.int32, (win, BLK), 0)
    xe_ref[...] = jnp.zeros_like(xe_ref)
    gate_ref[...] = jnp.zeros_like(gate_ref)
    for j in range(nb):
        blk = s * nb + j
        toks = slice(j * BLK, (j + 1) * BLK)
        hn_blk = hn_ref[toks, :]

        def window(w, carry):
            hits, starts = [], []
            for e in range(E):
                lo = tbl_ref[e, blk] + w * win
                start = pl.multiple_of(jnp.minimum(lo, cap - win), SLOT_ALIGN)
                slot = slot_ref[e, :, toks]
                slot = jnp.where(slot >= lo, slot, -1)
                hits.append(slot == w_i + start)
                starts.append(start)
            onehot = jnp.concatenate([jnp.where(h, 1.0, 0.0) for h in hits], axis=0).astype(jnp.bfloat16)
            rows = jnp.dot(onehot, hn_blk, preferred_element_type=jnp.float32)
            for e in range(E):
                dst = pl.ds(starts[e], win)
                xe_ref[e, dst, :] += rows[e * win:(e + 1) * win].astype(xe_ref.dtype)
                g = jnp.sum(jnp.where(hits[e], aff_ref[e, :, toks], 0.0), axis=1, keepdims=True)
                gate_ref[e, dst, :] += jnp.broadcast_to(g, (win, 128))
            return carry

        lax.fori_loop(0, tbl_ref[E, blk], window, 0)


def _moe_gather(slot, slot3, aff3, hn, row0, n_seq, t):
    E = slot3.shape[0]
    D = hn.shape[1]
    cap = EC_CAPACITY * t // N_EXPERTS
    win = min(cap, SCATTER_WINDOW)
    nb = t // BLK
    b0 = row0 // t
    tbl = _scatter_windows(slot, row0, n_seq, t, win)
    return pl.pallas_call(
        functools.partial(_moe_gather_kernel, cap=cap, win=win, nb=nb),
        grid_spec=pltpu.PrefetchScalarGridSpec(
            num_scalar_prefetch=1, grid=(n_seq,),
            in_specs=[pl.BlockSpec((E, 1, t), lambda s, tb_: (0, 0, b0 + s)),
                      pl.BlockSpec((E, 1, t), lambda s, tb_: (0, 0, b0 + s)),
                      pl.BlockSpec((t, D), lambda s, tb_: (b0 + s, 0))],
            out_specs=[pl.BlockSpec((E, cap, D), lambda s, tb_: (0, s, 0)),
                       pl.BlockSpec((E, cap, 128), lambda s, tb_: (0, s, 0))]),
        out_shape=[jax.ShapeDtypeStruct((E, n_seq * cap, D), jnp.bfloat16),
                   jax.ShapeDtypeStruct((E, n_seq * cap, 128), jnp.float32)],
        compiler_params=pltpu.CompilerParams(dimension_semantics=("arbitrary",),
                                             vmem_limit_bytes=MOE_FFN_VMEM_BYTES),
        name="moe_gather",
    )(tbl, slot3, aff3, hn)


def _moe_ffn_kernel(xc_ref, xl_ref, gc_ref, gl_ref, wg_ref, wu_ref, wd_ref, yc_ref, yl_ref, acc_ref, *, nf):
    f = pl.program_id(1)
    bf16 = jnp.bfloat16
    wg = wg_ref[0, 0].astype(bf16)
    wu = wu_ref[0, 0].astype(bf16)
    wd = wd_ref[0, 0].astype(bf16)
    nc = xc_ref.shape[1]
    x = jnp.concatenate([xc_ref[0], xl_ref[0]], axis=0)
    g = jnp.dot(x, wg, preferred_element_type=jnp.float32)
    u = jnp.dot(x, wu, preferred_element_type=jnp.float32)
    h = (g * jax.nn.sigmoid(g) * u).astype(bf16)
    y = jnp.dot(h, wd, preferred_element_type=jnp.float32)

    if nf > 1:
        @pl.when(f == 0)
        def _():
            acc_ref[...] = y

        @pl.when((f != 0) & (f != nf - 1))
        def _():
            acc_ref[...] += y

    @pl.when(f == nf - 1)
    def _():
        total = y if nf == 1 else acc_ref[...] + y
        gate = jnp.concatenate([gc_ref[0], gl_ref[0]], axis=0)
        out = total * jnp.concatenate([gate] * (total.shape[1] // 128), axis=1)
        yc_ref[0] = out[:nc].astype(yc_ref.dtype)
        yl_ref[0] = out[nc:].astype(yl_ref.dtype)


def _moe_ffn(xc, xl, gc, gl, wg, wu, wd, layer):
    E, nc_rows, D = xc.shape
    nl_rows = xl.shape[1]
    F = wg.shape[3]
    nf = F // FF_TILE
    return pl.pallas_call(
        functools.partial(_moe_ffn_kernel, nf=nf),
        grid=(E, nf),
        in_specs=[pl.BlockSpec((1, nc_rows, D), lambda e, f: (e, 0, 0)),
                  pl.BlockSpec((1, nl_rows, D), lambda e, f: (e, 0, 0)),
                  pl.BlockSpec((1, nc_rows, 128), lambda e, f: (e, 0, 0)),
                  pl.BlockSpec((1, nl_rows, 128), lambda e, f: (e, 0, 0)),
                  pl.BlockSpec((1, 1, D, FF_TILE), lambda e, f: (layer, e, 0, f)),
                  pl.BlockSpec((1, 1, D, FF_TILE), lambda e, f: (layer, e, 0, f)),
                  pl.BlockSpec((1, 1, FF_TILE, D), lambda e, f: (layer, e, f, 0))],
        out_specs=[pl.BlockSpec((1, nc_rows, D), lambda e, f: (e, 0, 0)),
                   pl.BlockSpec((1, nl_rows, D), lambda e, f: (e, 0, 0))],
        out_shape=[jax.ShapeDtypeStruct((E, nc_rows, D), jnp.bfloat16),
                   jax.ShapeDtypeStruct((E, nl_rows, D), jnp.bfloat16)],
        scratch_shapes=[pltpu.VMEM((nc_rows + nl_rows, D), jnp.float32)],
        compiler_params=pltpu.CompilerParams(dimension_semantics=("arbitrary", "arbitrary"),
                                             vmem_limit_bytes=MOE_FFN_VMEM_BYTES),
        name="moe_ffn",
    )(xc, xl, gc, gl, wg, wu, wd)


def _moe_scatter_kernel(tbl_ref, slot_ref, ye_ref, x_ref, g2_ref, *rest, cap, win, nb, final):
    if final:
        fg_ref, o_ref, acc_ref = rest
    else:
        o_ref, acc_ref = rest
    E = ye_ref.shape[0]
    tb = x_ref.shape[0]
    blk = pl.program_id(0) * nb + pl.program_id(1)
    w_i = lax.broadcasted_iota(jnp.int32, (win, tb), 0)
    acc_ref[...] = jnp.zeros_like(acc_ref)

    def window(w, carry):
        hot, rows = [], []
        for e in range(E):
            lo = tbl_ref[e, blk] + w * win
            start = pl.multiple_of(jnp.minimum(lo, cap - win), SLOT_ALIGN)
            slot = slot_ref[e]
            slot = jnp.where(slot >= lo, slot, -1)
            hot.append(jnp.where(slot == w_i + start, 1.0, 0.0))
            rows.append(ye_ref[e, pl.ds(start, win), :])
        onehot = jnp.concatenate(hot, axis=0).astype(jnp.bfloat16)
        acc_ref[...] += lax.dot_general(onehot, jnp.concatenate(rows, axis=0), (((0,), (0,)), ((), ())),
                                        preferred_element_type=jnp.float32)
        return carry

    lax.fori_loop(0, tbl_ref[E, blk], window, 0)
    x = x_ref[...] + g2_ref[0] * acc_ref[...]
    if final:
        x = x * lax.rsqrt(jnp.mean(x * x, -1, keepdims=True) + NORM_EPS) * fg_ref[...]
    o_ref[...] = x


def _scatter_windows(slot, row0, n_seq, t, win):
    E = slot.shape[0]
    cap = EC_CAPACITY * t // N_EXPERTS
    nb = t // BLK
    cnt = jnp.sum(slot[:, row0:row0 + n_seq * t].reshape(E, n_seq, nb, BLK) >= 0, axis=3)
    first = jnp.cumsum(cnt, axis=2) - cnt
    start = jnp.minimum(first // SLOT_ALIGN * SLOT_ALIGN, cap - win)
    n_win = jnp.maximum(jnp.max((first + cnt - start + win - 1) // win, axis=0), 1)
    return jnp.concatenate([start.reshape(E, n_seq * nb), n_win.reshape(1, n_seq * nb)]).astype(jnp.int32)


def _moe_scatter(slot, slot3, ye, x, g2blk, row0, n_seq, t, final_g=None):
    E, _, D = ye.shape
    cap = EC_CAPACITY * t // N_EXPERTS
    win = min(cap, SCATTER_WINDOW)
    nb = t // BLK
    b0 = row0 // BLK
    final = final_g is not None
    tbl = _scatter_windows(slot, row0, n_seq, t, win)
    in_specs = [pl.BlockSpec((E, 1, BLK), lambda s, j, tb_: (0, 0, b0 + s * nb + j)),
                pl.BlockSpec((E, cap, D), lambda s, j, tb_: (0, s, 0)),
                pl.BlockSpec((BLK, D), lambda s, j, tb_: (b0 + s * nb + j, 0)),
                pl.BlockSpec((1, 1, D), lambda s, j, tb_: (b0 + s * nb + j, 0, 0))]
    args = [tbl, slot3, ye, x, g2blk]
    if final:
        in_specs.append(pl.BlockSpec((1, D), lambda s, j, tb_: (0, 0)))
        args.append(final_g.reshape(1, D))
        out_specs = pl.BlockSpec((BLK, D), lambda s, j, tb_: (s * nb + j, 0))
        out_shape = jax.ShapeDtypeStruct((n_seq * t, D), jnp.float32)
        aliases = {}
    else:
        out_specs = pl.BlockSpec((BLK, D), lambda s, j, tb_: (b0 + s * nb + j, 0))
        out_shape = jax.ShapeDtypeStruct(x.shape, jnp.float32)
        aliases = {3: 0}
    return pl.pallas_call(
        functools.partial(_moe_scatter_kernel, cap=cap, win=win, nb=nb, final=final),
        grid_spec=pltpu.PrefetchScalarGridSpec(
            num_scalar_prefetch=1, grid=(n_seq, nb), in_specs=in_specs, out_specs=out_specs,
            scratch_shapes=[pltpu.VMEM((BLK, D), jnp.float32)]),
        out_shape=out_shape,
        input_output_aliases=aliases,
        compiler_params=pltpu.CompilerParams(dimension_semantics=("arbitrary", "arbitrary"),
                                             vmem_limit_bytes=MOE_FFN_VMEM_BYTES),
        name="moe_scatter",
    )(*args)


def moe_layer(x, hn, affT, g2blk, wg, wu, wd, layer, n_ctx, l_lat, final_g=None):
    R = x.shape[0]
    r_ctx = n_ctx * BLK
    n_lat = (R - r_ctx) // l_lat
    slot = jnp.concatenate([_moe_select(affT, 0, n_ctx, BLK), _moe_select(affT, r_ctx, n_lat, l_lat)], axis=1)
    slot3 = slot[:, None, :]
    aff3 = affT[:, None, :]
    xc, gc = _moe_gather(slot, slot3, aff3, hn, 0, n_ctx, BLK)
    xl, gl = _moe_gather(slot, slot3, aff3, hn, r_ctx, n_lat, l_lat)
    yc, yl = _moe_ffn(xc, xl, gc, gl, wg, wu, wd, layer)
    if final_g is None:
        x = _moe_scatter(slot, slot3, yc, x, g2blk, 0, n_ctx, BLK)
        return _moe_scatter(slot, slot3, yl, x, g2blk, r_ctx, n_lat, l_lat)
    return (_moe_scatter(slot, slot3, yc, x, g2blk, 0, n_ctx, BLK, final_g),
            _moe_scatter(slot, slot3, yl, x, g2blk, r_ctx, n_lat, l_lat, final_g))


L0_Z = (0, 1024)
L0_XBC = (1024, 2560)
L0_SHIFT = (2560, 6016)
L0_DT = (6016, 6144)
L0_COLS = 6144
L0_VMEM_BYTES = 56 * 1024 * 1024


def _sum_split(x, m, n_split):
    acc = None
    for _ in range(n_split):
        hi = x.astype(jnp.bfloat16)
        x = x - hi.astype(jnp.float32)
        t = jnp.dot(hi, m, preferred_element_type=jnp.float32)
        acc = t if acc is None else acc + t
    return acc


def _sum_split_left(m, x, n_split):
    acc = None
    for _ in range(n_split):
        hi = x.astype(jnp.bfloat16)
        x = x - hi.astype(jnp.float32)
        t = jnp.dot(m, hi, preferred_element_type=jnp.float32)
        acc = t if acc is None else acc + t
    return acc


def _head_sum(x, e_ref, et_ref):
    return _sum_split(_sum_split(x, e_ref[...], 2), et_ref[...], 2)


def _adaln(x, g, sc, sh):
    y = x * lax.rsqrt(jnp.mean(x * x, -1, keepdims=True) + NORM_EPS) * g
    return y * (1.0 + sc) + sh


def _softplus(x):
    return jnp.maximum(x, 0.0) + jnp.log(1.0 + jnp.exp(-jnp.abs(x)))


def _l0_in_kernel(tbl_ref, *refs, n_parts, n_a):
    x_refs, xp_refs, xn_refs = refs[:n_parts], refs[n_parts:2 * n_parts], refs[2 * n_parts:3 * n_parts]
    (g_ref, sc_ref, sh_ref, w_ref, mup_ref, mun_ref, rwp_ref, w2_ref, a2_ref, g2_ref, e_ref, et_ref,
     z_ref, xbc_ref, dt_ref, r_ref, v_ref, an_ref, lw_ref, kd_ref, bv_ref, gate_ref, bonus_ref) = refs[3 * n_parts:]
    i = pl.program_id(0)
    f32, bf16 = jnp.float32, jnp.bfloat16
    g, sc, sh = g_ref[...], sc_ref[0], sh_ref[0]
    hn = _adaln(_residual_rows(x_refs, n_a), g, sc, sh).astype(bf16)
    halo = _adaln(jnp.concatenate([_residual_rows(xp_refs, n_a), _residual_rows(xn_refs, n_a)], axis=0),
                  g, sc, sh).astype(bf16)
    hn_halo = jnp.concatenate([hn, halo], axis=0)

    keep_prev = (1 - tbl_ref[0, i]).astype(f32)
    keep_next = (1 - tbl_ref[1, i]).astype(f32)
    row = lax.broadcasted_iota(jnp.int32, (BLK, 1), 0)
    c = RWKV_DIM

    def plain(c0, c1):
        return jnp.dot(hn, w_ref[:, c0:c1], preferred_element_type=f32)

    def proj(c0, c1):
        return jnp.dot(hn_halo, w_ref[:, L0_SHIFT[0] + c0:L0_SHIFT[0] + c1], preferred_element_type=f32)

    def shift(both, c0, c1):
        cur = both[:BLK]
        prev = jnp.where(row == 0, both[BLK + 7:BLK + 8] * keep_prev, pltpu.roll(cur, 1, 0))
        nxt = jnp.where(row == BLK - 1, both[BLK + 8:BLK + 9] * keep_next, pltpu.roll(cur, BLK - 1, 0))
        return cur + mup_ref[:, c0:c1] * (prev - cur) + mun_ref[:, c0:c1] * (nxt - cur)

    lo = 3 * c + 2 * W_LORA
    p_wl = proj(3 * c, lo)
    p_ag = proj(lo, lo + 256)
    p_k = proj(c, 2 * c)
    wl = shift(p_wl, 3 * c, lo)
    ag = shift(p_ag, lo, lo + 256)
    p_r = proj(0, c)
    k = shift(p_k, c, 2 * c)
    k_k, k_a, r_k = rwp_ref[0:1], rwp_ref[1:2], rwp_ref[2:3]
    w_lin = jnp.dot(jnp.tanh(wl).astype(bf16), w2_ref[...], preferred_element_type=f32)
    a_lora = jnp.dot(ag.astype(bf16), a2_ref[...], preferred_element_type=f32)
    gate_ref[...] = jnp.dot(jax.nn.sigmoid(ag).astype(bf16), g2_ref[...],
                            preferred_element_type=f32).astype(gate_ref.dtype)
    p_v = proj(2 * c, 3 * c)
    r = shift(p_r, 0, c)
    r_ref[...] = r.astype(r_ref.dtype)
    kk = k * k_k
    kk = kk * lax.rsqrt(_head_sum(kk * kk, e_ref, et_ref) + 1e-12)
    an_ref[...] = (-kk).astype(an_ref.dtype)
    z_ref[...] = plain(L0_Z[0], L0_Z[1]).astype(z_ref.dtype)
    v = shift(p_v, 2 * c, 3 * c)
    v_ref[...] = v.astype(v_ref.dtype)
    kd_sum = None
    xbc_cols = (L0_XBC[0], (L0_XBC[0] + L0_XBC[1]) // 2, L0_XBC[1])
    for d in range(2):
        xbc_ref[:, xbc_cols[d] - L0_XBC[0]:xbc_cols[d + 1] - L0_XBC[0]] = plain(xbc_cols[d], xbc_cols[d + 1])
        w_log = -_softplus(-(rwp_ref[3 + d:4 + d] + w_lin[:, d * c:(d + 1) * c])) - 0.5
        lw_ref[d] = -jnp.exp(w_log)
        a = jax.nn.sigmoid(rwp_ref[5 + d:6 + d] + a_lora)
        kd = k * (1.0 + (a - 1.0) * k_a)
        kd_ref[d] = kd.astype(kd_ref.dtype)
        bv_ref[d] = (kk * a).astype(bv_ref.dtype)
        kd_sum = kd if kd_sum is None else kd_sum + kd
    dt_ref[...] = plain(L0_DT[0], L0_DT[1])
    bonus_ref[...] = (_head_sum(r * kd_sum * r_k, e_ref, et_ref) * v).astype(bonus_ref.dtype)


def _seq_tables(n_ctx, n_lat, l_lat):
    nb = l_lat // BLK
    cond = [0] * n_ctx + [1 + s for s in range(n_lat) for _ in range(nb)]
    first = [1] * n_ctx + [1 if j == 0 else 0 for _ in range(n_lat) for j in range(nb)]
    last = [1] * n_ctx + [1 if j == nb - 1 else 0 for _ in range(n_lat) for j in range(nb)]
    return cond, first, last


def l0_in(x, g1, scb, shb, w_packed, mup, mun, rwp, w2bd, a2p, g2p, e_ind, et_ind, first, last):
    xs = x if isinstance(x, (tuple, list)) else (x,)
    D = xs[0].shape[1]
    R = sum(a.shape[0] for a in xs)
    nblk = R // BLK
    n_a = xs[0].shape[0] // BLK
    tbl = jnp.asarray([first, last], jnp.int32)
    c = RWKV_DIM
    row = lambda i, t: (i, 0)
    full = lambda shape: pl.BlockSpec(shape, lambda i, t: (0,) * len(shape))
    rows = lambda n: pl.BlockSpec((BLK, n), row)
    rows2 = lambda n: pl.BlockSpec((2, BLK, n), lambda i, t: (0, i, 0))
    f32, act = jnp.float32, ACT_DTYPE
    sds = jax.ShapeDtypeStruct
    h8 = BLK // 8

    def part_specs(k):
        nb_k, b0 = xs[k].shape[0] // BLK, (0 if k == 0 else n_a)
        local = lambda i: jnp.clip(i - b0, 0, nb_k - 1)
        return (pl.BlockSpec((BLK, D), lambda i, t: (local(i), 0)),
                pl.BlockSpec((8, D), lambda i, t: (jnp.maximum(local(i) * h8 - 1, 0), 0)),
                pl.BlockSpec((8, D), lambda i, t: (jnp.minimum((local(i) + 1) * h8, nb_k * h8 - 1), 0)))

    specs = [part_specs(k) for k in range(len(xs))]
    x_specs = [s[j] for j in range(3) for s in specs]
    return pl.pallas_call(
        functools.partial(_l0_in_kernel, n_parts=len(xs), n_a=n_a),
        grid_spec=pltpu.PrefetchScalarGridSpec(
            num_scalar_prefetch=1,
            grid=(nblk,),
            in_specs=[*x_specs,
                      full((1, D)),
                      pl.BlockSpec((1, 1, D), lambda i, t: (i, 0, 0)),
                      pl.BlockSpec((1, 1, D), lambda i, t: (i, 0, 0)),
                      full((D, L0_COLS)), full(mup.shape), full(mun.shape), full(rwp.shape),
                      full(w2bd.shape), full(a2p.shape), full(g2p.shape), full(e_ind.shape), full(et_ind.shape)],
            out_specs=[rows(c), rows(SSD_XBC), rows(128), rows(c), rows(c), rows(c),
                       rows2(c), rows2(c), rows2(c), rows(c), rows(c)]),
        out_shape=[sds((R, c), act), sds((R, SSD_XBC), f32), sds((R, 128), f32), sds((R, c), act), sds((R, c), act),
                   sds((R, c), act), sds((2, R, c), f32), sds((2, R, c), act), sds((2, R, c), act),
                   sds((R, c), act), sds((R, c), act)],
        compiler_params=pltpu.CompilerParams(dimension_semantics=("arbitrary",), vmem_limit_bytes=L0_VMEM_BYTES),
        name="l0_in",
    )(tbl, *xs, *xs, *xs, g1, scb, shb, w_packed, mup, mun, rwp, w2bd, a2p, g2p, e_ind, et_ind)


def l0_pack_weights(p, e):
    bf16 = jnp.bfloat16
    w = p['ab_w_in'][e]
    D = w.shape[0]
    c = RWKV_DIM
    rw0 = SSD_IN
    ag0 = rw0 + 3 * c + 2 * W_LORA
    w_packed = jnp.concatenate([
        w[:, :SSD_INNER + SSD_XBC], w[:, rw0:ag0], w[:, ag0:ag0 + A_LORA + G_LORA],
        jnp.zeros((D, 256 - A_LORA - G_LORA), w.dtype),
        w[:, SSD_INNER + SSD_XBC:SSD_IN], jnp.zeros((D, 128 - SSD_HEADS), w.dtype)], axis=1).astype(bf16)

    def pack_mu(mu):
        return jnp.concatenate([mu, jnp.zeros((256 - A_LORA - G_LORA,), mu.dtype)])[None, :]

    rwp = jnp.stack([p['rwkv_k_k'][e], p['rwkv_k_a'][e], p['rwkv_r_k'][e].reshape(-1), p['rwkv_w0'][e, 0],
                     p['rwkv_w0'][e, 1], p['rwkv_a0'][e, 0], p['rwkv_a0'][e, 1], jnp.zeros((c,), jnp.float32)])
    zw = jnp.zeros((W_LORA, c), jnp.float32)
    w2bd = jnp.concatenate([jnp.concatenate([p['rwkv_w2'][e, 0], zw], axis=1),
                            jnp.concatenate([zw, p['rwkv_w2'][e, 1]], axis=1)], axis=0).astype(bf16)
    a2p = jnp.concatenate([p['rwkv_a2'][e], jnp.zeros((256 - A_LORA, c), jnp.float32)], axis=0).astype(bf16)
    g2p = jnp.concatenate([jnp.zeros((A_LORA, c), jnp.float32), p['rwkv_g2'][e],
                           jnp.zeros((256 - A_LORA - G_LORA, c), jnp.float32)], axis=0).astype(bf16)
    head = jnp.arange(c) // RWKV_N
    e_ind = (head[:, None] == jnp.arange(128)[None, :]).astype(bf16)
    return w_packed, pack_mu(p['rwkv_mu_prev'][e]), pack_mu(p['rwkv_mu_next'][e]), rwp, w2bd, a2p, g2p, e_ind, e_ind.T


SSD_QH = 4
SSD_VMEM_BYTES = 48 * 1024 * 1024
NEG_BIG = -1e30
LOG2E = 1.4426950408889634


def _conv_silu(cur, prev_row, next_row, w_ref, b_ref):
    row = lax.broadcasted_iota(jnp.int32, (BLK, 1), 0)
    prev = jnp.where(row == 0, prev_row, pltpu.roll(cur, 1, 0))
    nxt = jnp.where(row == BLK - 1, next_row, pltpu.roll(cur, BLK - 1, 0))
    y = w_ref[0:1] * prev + w_ref[1:2] * cur + w_ref[2:3] * nxt + b_ref[...]
    return y * jax.nn.sigmoid(y)


def _ssd_kernel(tbl_ref, xs_ref, b_ref, c_ref, dt_ref, cwx_ref, cwb_ref, cwc_ref, cbx_ref, cbb_ref, cbc_ref,
                sel_ref, hp_ref, s0_ref, y_ref, fs_ref, xa_ref, ba_ref, ca_ref, sfx_ref, ldb_ref, st_ref, *, n_zero):
    f32, bf16 = jnp.float32, jnp.bfloat16
    sb = pl.program_id(0)
    sbr, qw = xs_ref.shape
    nch = sbr // BLK
    P, QH = SSD_P, SSD_QH
    t_i = lax.broadcasted_iota(jnp.int32, (BLK, BLK), 0)
    s_i = lax.broadcasted_iota(jnp.int32, (BLK, BLK), 1)
    lower = s_i <= t_i
    upper = s_i >= t_i
    tri_lo = jnp.where(lower, 1.0, 0.0).astype(bf16)
    tri_up = jnp.where(upper, 1.0, 0.0).astype(bf16)
    hp = hp_ref[0]
    sel = sel_ref[0]
    ind = jnp.where(lax.broadcasted_iota(jnp.int32, (128, qw), 1) // P == lax.broadcasted_iota(jnp.int32, (128, qw), 0),
                    1.0, 0.0).astype(bf16)
    head_of_lane = lax.broadcasted_iota(jnp.int32, (1, qw), 1) // P

    def expand(cols):
        return _sum_split(cols, ind, 2)

    d_row = expand(jnp.broadcast_to(hp[4:5], (8, 128)))[0:1]

    def chunk_rows(c):
        return pl.ds(pl.multiple_of(c * BLK, BLK), BLK)

    def neighbours(ref, c, keep_prev, keep_next):
        lo = jnp.maximum(c * BLK - 1, 0)
        hi = jnp.minimum((c + 1) * BLK, sbr - 1)
        return ref[pl.ds(lo, 1), :] * keep_prev, ref[pl.ds(hi, 1), :] * keep_next

    def fwd(c, carry):
        blk = sb * nch + c
        first, last = tbl_ref[0, blk], tbl_ref[1, blk]
        kp, kn = (1 - first).astype(f32), (1 - last).astype(f32)
        rows = chunk_rows(c)

        @pl.when(first == 1)
        def _():
            st_ref[0] = jnp.where(sb >= n_zero, s0_ref[0, 0], 0.0)

        xa = _conv_silu(xs_ref[rows, :], *neighbours(xs_ref, c, kp, kn), cwx_ref, cbx_ref)
        bm = _conv_silu(b_ref[rows, :], *neighbours(b_ref, c, kp, kn), cwb_ref, cbb_ref)
        cm = _conv_silu(c_ref[rows, :], *neighbours(c_ref, c, kp, kn), cwc_ref, cbc_ref)
        xb, bmb, cmb = xa.astype(bf16), bm.astype(bf16), cm.astype(bf16)
        xa_ref[rows, :] = xb
        ba_ref[rows, :] = bmb
        ca_ref[rows, :] = cmb
        dtq = _sum_split(dt_ref[rows, :], sel, 2)
        dtf = _softplus(dtq + hp[0:1])
        dtb = _softplus(dtq + hp[1:2])
        acs = _sum_split_left(tri_lo, dtf * hp[2:3], 3)
        sfx = _sum_split_left(tri_up, dtb * hp[3:4], 3)
        ldf, ldb = jnp.log(dtf), jnp.log(dtb)
        sfx_ref[rows, :] = sfx
        ldb_ref[rows, :] = ldb
        a2, s2 = acs * LOG2E, sfx * LOG2E
        a2r = (a2 - ldf * LOG2E).T
        s2r = (s2 - ldb * LOG2E).T
        g = lax.dot_general(cmb, bmb, (((1,), (1,)), ((), ())), preferred_element_type=f32)
        y_diag = None
        for j in range(QH):
            m = (g * (jnp.exp2(jnp.where(lower, a2[:, j:j + 1] - a2r[j:j + 1, :], NEG_BIG))
                      + jnp.exp2(jnp.where(upper, s2[:, j:j + 1] - s2r[j:j + 1, :], NEG_BIG)))).astype(bf16)
            xh = jnp.where(head_of_lane == j, xb, jnp.zeros((), bf16))
            t = jnp.dot(m, xh, preferred_element_type=f32)
            y_diag = t if y_diag is None else y_diag + t
        ea = jnp.exp(expand(acs))
        wf = jnp.exp(expand(acs[BLK - 1:BLK] - acs + ldf))
        s_in = st_ref[0]
        y_ref[rows, :] = xa * d_row + y_diag + ea * jnp.dot(cmb, s_in.astype(bf16), preferred_element_type=f32)
        st_ref[0] = ea[BLK - 1:BLK] * s_in + lax.dot_general(
            bmb, (xa * wf).astype(bf16), (((0,), (0,)), ((), ())), preferred_element_type=f32)
        fs_ref[c, 0] = st_ref[0]
        return carry

    lax.fori_loop(0, nch, fwd, 0)

    def bwd(k, carry):
        c = nch - 1 - k
        blk = sb * nch + c
        rows = chunk_rows(c)

        @pl.when(tbl_ref[1, blk] == 1)
        def _():
            st_ref[1] = jnp.where(sb >= n_zero, s0_ref[0, 1], 0.0)

        sfx = sfx_ref[rows, :]
        eb = jnp.exp(expand(sfx))
        wb = jnp.exp(expand(sfx[0:1] - sfx + ldb_ref[rows, :]))
        s_in = st_ref[1]
        y_ref[rows, :] += eb * jnp.dot(ca_ref[rows, :], s_in.astype(bf16), preferred_element_type=f32)
        st_ref[1] = eb[0:1] * s_in + lax.dot_general(
            ba_ref[rows, :], (xa_ref[rows, :].astype(f32) * wb).astype(bf16), (((0,), (0,)), ((), ())),
            preferred_element_type=f32)
        fs_ref[c, 1] = st_ref[1]
        return carry

    lax.fori_loop(0, nch, bwd, 0)


def ssd_scan(xbc, dt, conv_w, conv_b, sel, hp, s0, n_zero, first, last, sb_rows):
    R = xbc.shape[0]
    n_sb = R // sb_rows
    nch = sb_rows // BLK
    nq = SSD_HEADS // SSD_QH
    qw = SSD_QH * SSD_P
    qpg = SSD_HPG // SSD_QH
    b_blk = SSD_INNER // SSD_N
    c_blk = b_blk + SSD_GROUPS
    tbl = jnp.asarray([first, last], jnp.int32)
    cw = conv_w
    cb = conv_b.reshape(1, -1)
    f32, bf16 = jnp.float32, jnp.bfloat16
    return pl.pallas_call(
        functools.partial(_ssd_kernel, n_zero=n_zero),
        grid_spec=pltpu.PrefetchScalarGridSpec(
            num_scalar_prefetch=1,
            grid=(n_sb, nq),
            in_specs=[pl.BlockSpec((sb_rows, qw), lambda s, q, t: (s, q)),
                      pl.BlockSpec((sb_rows, SSD_N), lambda s, q, t: (s, b_blk + q // qpg)),
                      pl.BlockSpec((sb_rows, SSD_N), lambda s, q, t: (s, c_blk + q // qpg)),
                      pl.BlockSpec((sb_rows, 128), lambda s, q, t: (s, 0)),
                      pl.BlockSpec((3, qw), lambda s, q, t: (0, q)),
                      pl.BlockSpec((3, SSD_N), lambda s, q, t: (0, b_blk + q // qpg)),
                      pl.BlockSpec((3, SSD_N), lambda s, q, t: (0, c_blk + q // qpg)),
                      pl.BlockSpec((1, qw), lambda s, q, t: (0, q)),
                      pl.BlockSpec((1, SSD_N), lambda s, q, t: (0, b_blk + q // qpg)),
                      pl.BlockSpec((1, SSD_N), lambda s, q, t: (0, c_blk + q // qpg)),
                      pl.BlockSpec((1, 128, 128), lambda s, q, t: (q, 0, 0)),
                      pl.BlockSpec((1, 8, 128), lambda s, q, t: (q, 0, 0)),
                      pl.BlockSpec((1, 2, SSD_N, qw), lambda s, q, t: (jnp.maximum(s - n_zero, 0), 0, 0, q))],
            out_specs=[pl.BlockSpec((sb_rows, qw), lambda s, q, t: (s, q)),
                       pl.BlockSpec((nch, 2, SSD_N, qw), lambda s, q, t: (s, 0, 0, q))],
            scratch_shapes=[pltpu.VMEM((sb_rows, qw), bf16), pltpu.VMEM((sb_rows, SSD_N), bf16),
                            pltpu.VMEM((sb_rows, SSD_N), bf16), pltpu.VMEM((sb_rows, 128), f32),
                            pltpu.VMEM((sb_rows, 128), f32), pltpu.VMEM((2, SSD_N, qw), f32)]),
        out_shape=[jax.ShapeDtypeStruct((R, SSD_INNER), f32),
                   jax.ShapeDtypeStruct((R // BLK, 2, SSD_N, SSD_INNER), f32)],
        compiler_params=pltpu.CompilerParams(dimension_semantics=("arbitrary", "arbitrary"),
                                             vmem_limit_bytes=SSD_VMEM_BYTES),
        name="ssd_scan",
    )(tbl, xbc, xbc, xbc, dt, cw, cw, cw, cb, cb, cb, sel, hp, s0)


def ssd_tables(p, e):
    nq = SSD_HEADS // SSD_QH
    lane = jnp.arange(128)
    sel = jnp.stack([(lane[:, None] == (q * SSD_QH + lane[None, :])) & (lane[None, :] < SSD_QH)
                     for q in range(nq)]).astype(jnp.bfloat16)
    a_neg = -jnp.exp(p['ssd_a_log'][e].astype(jnp.float32))
    rows = jnp.stack([p['ssd_dt_bias'][e, 0], p['ssd_dt_bias'][e, 1], a_neg[0], a_neg[1], p['ssd_d'][e]])
    hp = jnp.zeros((nq, 8, 128), jnp.float32)
    hp = hp.at[:, :5, :SSD_QH].set(jnp.transpose(rows.reshape(5, nq, SSD_QH), (1, 0, 2)))
    return sel, hp


MIX_VMEM_BYTES = 48 * 1024 * 1024
MIX_BLOCKS = 1
ROUTER_LANES = 128


def _residual_norm_router(x, out, g1, n2g, sc2, sh2, rw_ref, x_out_ref, hn_ref, aff_ref):
    rows, d = x.shape
    nb = g1.shape[0]
    x_new = x.reshape(nb, rows // nb, d) + g1 * out.reshape(nb, rows // nb, d)
    x_out_ref[...] = x_new.reshape(rows, d)
    hn = _adaln(x_new, n2g, sc2, sh2).reshape(rows, d)
    hn_ref[...] = hn.astype(hn_ref.dtype)
    logits = _dot(hn, rw_ref[...], ((1,), (0,)), 3)
    lane = lax.broadcasted_iota(jnp.int32, logits.shape, 1)
    logits = jnp.where(lane < N_EXPERTS, logits, NEG_BIG)
    ex = jnp.exp(logits - jnp.max(logits, axis=-1, keepdims=True))
    aff = ex / jnp.sum(ex, axis=-1, keepdims=True)
    aff_ref[...] = aff.T[:N_EXPERTS]


def _residual_rows(x_refs, n_a):
    if len(x_refs) == 1:
        return x_refs[0][...]
    return jnp.where(pl.program_id(0) < n_a, x_refs[0][...], x_refs[1][...])


def _l0_out_kernel(ys_ref, z_ref, yf_ref, yb_ref, bonus_ref, gate_ref, sg_ref, lnw_ref, lnb_ref, e_ref, et_ref, w_ref,
                   *rest, n_a):
    *x_refs, g1_ref, n2g_ref, sc2_ref, sh2_ref, rw_ref, x_out_ref, hn_ref, aff_ref = rest
    f32, bf16 = jnp.float32, jnp.bfloat16
    z = z_ref[...].astype(f32)
    ys = ys_ref[...] * (z * jax.nn.sigmoid(z))
    gw = SSD_INNER // SSD_GROUPS
    parts = []
    for gi in range(SSD_GROUPS):
        yg = ys[:, gi * gw:(gi + 1) * gw]
        parts.append(yg * lax.rsqrt(jnp.mean(yg * yg, -1, keepdims=True) + NORM_EPS))
    a1 = jnp.concatenate(parts, axis=1) * sg_ref[...]
    o = yf_ref[...].astype(f32) + yb_ref[...].astype(f32)
    mu = _head_sum(o, e_ref, et_ref) * (1.0 / RWKV_N)
    oc = o - mu
    var = _head_sum(oc * oc, e_ref, et_ref) * (1.0 / RWKV_N)
    o = oc * lax.rsqrt(var + RWKV_GN_EPS) * lnw_ref[...] + lnb_ref[...]
    o = (o + bonus_ref[...].astype(f32)) * gate_ref[...].astype(f32)
    out = (jnp.dot(a1.astype(bf16), w_ref[:SSD_INNER], preferred_element_type=f32)
           + jnp.dot(o.astype(bf16), w_ref[SSD_INNER:], preferred_element_type=f32))
    _residual_norm_router(_residual_rows(x_refs, n_a), out, g1_ref[...], n2g_ref[...], sc2_ref[...], sh2_ref[...],
                          rw_ref, x_out_ref, hn_ref, aff_ref)


def _l1_out_kernel(a_ref, w_ref, *rest, n_a):
    *x_refs, g1_ref, n2g_ref, sc2_ref, sh2_ref, rw_ref, x_out_ref, hn_ref, aff_ref = rest
    out = jnp.dot(a_ref[...].astype(jnp.bfloat16), w_ref[...], preferred_element_type=jnp.float32)
    _residual_norm_router(_residual_rows(x_refs, n_a), out, g1_ref[...], n2g_ref[...], sc2_ref[...], sh2_ref[...],
                          rw_ref, x_out_ref, hn_ref, aff_ref)


def _mix_out_call(kernel_fn, name, lead_args, lead_specs, w_out, x, g1b, n2g, sc2b, sh2b, router_w):
    mb = MIX_BLOCKS
    xs = x if isinstance(x, (tuple, list)) else (x,)
    D = xs[0].shape[1]
    R = sum(a.shape[0] for a in xs)
    n_a = xs[0].shape[0] // (mb * BLK)
    assert (R // BLK) % mb == 0 and xs[0].shape[0] % (mb * BLK) == 0
    full = lambda a: pl.BlockSpec(a.shape, lambda i: (0,) * a.ndim)
    blkrow = pl.BlockSpec((mb, 1, D), lambda i: (i, 0, 0))
    rw = jnp.zeros((D, ROUTER_LANES), jnp.float32).at[:, :N_EXPERTS].set(router_w)
    args = list(lead_args) + [w_out, *xs, g1b, n2g, sc2b, sh2b, rw]
    if len(xs) == 1:
        x_specs = [pl.BlockSpec((mb * BLK, D), lambda i: (i, 0))]
        aliases = {len(lead_args) + 1: 0}
    else:
        x_specs = [pl.BlockSpec((mb * BLK, D), lambda i: (jnp.minimum(i, n_a - 1), 0)),
                   pl.BlockSpec((mb * BLK, D), lambda i: (jnp.maximum(i - n_a, 0), 0))]
        aliases = {}
    in_specs = list(lead_specs) + [full(w_out), *x_specs, blkrow, full(n2g), blkrow, blkrow, full(rw)]
    return pl.pallas_call(
        functools.partial(kernel_fn, n_a=n_a),
        grid=(R // (mb * BLK),),
        in_specs=in_specs,
        out_specs=[pl.BlockSpec((mb * BLK, D), lambda i: (i, 0)), pl.BlockSpec((mb * BLK, D), lambda i: (i, 0)),
                   pl.BlockSpec((N_EXPERTS, mb * BLK), lambda i: (0, i))],
        out_shape=[jax.ShapeDtypeStruct((R, D), jnp.float32), jax.ShapeDtypeStruct((R, D), jnp.bfloat16),
                   jax.ShapeDtypeStruct((N_EXPERTS, R), jnp.float32)],
        input_output_aliases=aliases,
        compiler_params=pltpu.CompilerParams(dimension_semantics=("arbitrary",), vmem_limit_bytes=MIX_VMEM_BYTES),
        name=name,
    )(*args)


def l0_out(ys, z, yf, yb, bonus, gate, ssd_g, ln_w, ln_b, e_ind, et_ind, w_out, x, g1b, n2g, sc2b, sh2b, router_w):
    c = RWKV_DIM
    rows = lambda n: pl.BlockSpec((MIX_BLOCKS * BLK, n), lambda i: (i, 0))
    full = lambda a: pl.BlockSpec(a.shape, lambda i: (0,) * a.ndim)
    lead = [ys, z, yf, yb, bonus, gate, ssd_g, ln_w, ln_b, e_ind, et_ind]
    specs = [rows(SSD_INNER), rows(SSD_INNER), rows(c), rows(c), rows(c), rows(c),
             full(ssd_g), full(ln_w), full(ln_b), full(e_ind), full(et_ind)]
    return _mix_out_call(_l0_out_kernel, "l0_out", lead, specs, w_out, x, g1b, n2g, sc2b, sh2b, router_w)


def l1_out(a, w_out, x, g1b, n2g, sc2b, sh2b, router_w):
    specs = [pl.BlockSpec((MIX_BLOCKS * BLK, a.shape[1]), lambda i: (i, 0))]
    return _mix_out_call(_l1_out_kernel, "l1_out", [a], specs, w_out, x, g1b, n2g, sc2b, sh2b, router_w)


RET_HPS = 2


def _l1_in_kernel(tbl_ref, x_ref, g_ref, sc_ref, sh_ref, w_ref, cos_ref, sin_ref, q_ref, k_ref, v_ref, gg_ref):
    f32 = jnp.float32
    hn = _adaln(x_ref[...], g_ref[...], sc_ref[0], sh_ref[0]).astype(jnp.bfloat16)
    cosf, sinf = cos_ref[0], sin_ref[0]

    def rope(x):
        parts = []
        for h in range(RET_HEADS):
            xh = x[:, h * RET_DK:(h + 1) * RET_DK]
            parts.append(xh * cosf + pltpu.roll(xh, RET_DK // 2, 1) * sinf)
        return jnp.concatenate(parts, axis=1)

    q_ref[...] = rope(jnp.dot(hn, w_ref[:, :RET_QK], preferred_element_type=f32)).astype(q_ref.dtype)
    k_ref[...] = (rope(jnp.dot(hn, w_ref[:, RET_QK:2 * RET_QK], preferred_element_type=f32))
                  * (RET_DK ** -0.5)).astype(k_ref.dtype)
    v_ref[...] = jnp.dot(hn, w_ref[:, 2 * RET_QK:2 * RET_QK + RET_V], preferred_element_type=f32).astype(v_ref.dtype)
    gg_ref[...] = jnp.dot(hn, w_ref[:, 2 * RET_QK + RET_V:], preferred_element_type=f32).astype(gg_ref.dtype)


def l1_in(x, g1, scb, shb, w_bf16, cos_t, sin_t, rope_blk):
    R, D = x.shape
    nblk = R // BLK
    tbl = jnp.asarray([rope_blk], jnp.int32)
    f32 = jnp.float32
    row = lambda n: pl.BlockSpec((BLK, n), lambda i, t: (i, 0))
    full = lambda a: pl.BlockSpec(a.shape, lambda i, t: (0,) * a.ndim)
    blkrow = pl.BlockSpec((1, 1, D), lambda i, t: (i, 0, 0))
    ropespec = pl.BlockSpec((1, BLK, RET_DK), lambda i, t: (t[0, i], 0, 0))
    return pl.pallas_call(
        _l1_in_kernel,
        grid_spec=pltpu.PrefetchScalarGridSpec(
            num_scalar_prefetch=1, grid=(nblk,),
            in_specs=[row(D), full(g1), blkrow, blkrow, full(w_bf16), ropespec, ropespec],
            out_specs=[row(RET_QK), row(RET_QK), row(RET_V), row(RET_V)]),
        out_shape=[jax.ShapeDtypeStruct((R, RET_QK), ACT_DTYPE), jax.ShapeDtypeStruct((R, RET_QK), ACT_DTYPE),
                   jax.ShapeDtypeStruct((R, RET_V), ACT_DTYPE), jax.ShapeDtypeStruct((R, RET_V), ACT_DTYPE)],
        compiler_params=pltpu.CompilerParams(dimension_semantics=("arbitrary",), vmem_limit_bytes=L0_VMEM_BYTES),
        name="l1_in",
    )(tbl, x, g1, scb, shb, w_bf16, cos_t, sin_t)


def _ret_kernel(tbl_ref, q_ref, k_ref, v_ref, g_ref, lg_ref, nw_ref, nb_ref, s0_ref, a_ref, fs_ref, st_ref, *,
                n_zero):
    f32, bf16 = jnp.float32, jnp.bfloat16
    sb = pl.program_id(0)
    nch = q_ref.shape[0] // BLK
    heads = range(RET_HPS)
    ks = [slice(h * RET_DK, (h + 1) * RET_DK) for h in heads]
    vs = [slice(h * RET_DV, (h + 1) * RET_DV) for h in heads]
    lgf = [lg_ref[h, 0:1, 0:1] for h in heads]
    lgb = [lg_ref[h, 1:2, 0:1] for h in heads]
    t_i = lax.broadcasted_iota(jnp.int32, (BLK, BLK), 0)
    s_i = lax.broadcasted_iota(jnp.int32, (BLK, BLK), 1)
    dist = (t_i - s_i).astype(f32)
    dm = [jnp.exp(jnp.where(s_i <= t_i, dist * lgf[h], NEG_BIG)) + jnp.exp(jnp.where(s_i >= t_i, -dist * lgb[h], NEG_BIG))
          for h in heads]
    tk = lax.broadcasted_iota(jnp.int32, (BLK, RET_DK), 0).astype(f32)
    tv = lax.broadcasted_iota(jnp.int32, (BLK, RET_DV), 0).astype(f32)
    k_to_end_f = [jnp.exp((BLK - 1.0 - tk) * lgf[h]) for h in heads]
    k_to_end_b = [jnp.exp(tk * lgb[h]) for h in heads]
    from_start_f = [jnp.exp((tv + 1.0) * lgf[h]) for h in heads]
    from_start_b = [jnp.exp((BLK - tv) * lgb[h]) for h in heads]
    nt, tn = (((1,), (1,)), ((), ())), (((0,), (0,)), ((), ()))

    def chunk_rows(c):
        return pl.ds(pl.multiple_of(c * BLK, BLK), BLK)

    def fwd(c, carry):
        blk = sb * nch + c
        rows = chunk_rows(c)

        @pl.when(tbl_ref[0, blk] == 1)
        def _():
            st_ref[0] = jnp.where(sb >= n_zero, s0_ref[0, 0], 0.0)

        q = [q_ref[rows, ks[h]].astype(bf16) for h in heads]
        k = [k_ref[rows, ks[h]] for h in heads]
        v = [v_ref[rows, vs[h]].astype(bf16) for h in heads]
        s_in = [st_ref[0, h] for h in heads]
        g = [lax.dot_general(q[h], k[h].astype(bf16), nt, preferred_element_type=f32) for h in heads]
        y_diag = [jnp.dot((g[h] * dm[h]).astype(bf16), v[h], preferred_element_type=f32) for h in heads]
        y_off = [jnp.dot(q[h], s_in[h].astype(bf16), preferred_element_type=f32) for h in heads]
        kw = [(k[h].astype(f32) * k_to_end_f[h]).astype(bf16) for h in heads]
        upd = [lax.dot_general(kw[h], v[h], tn, preferred_element_type=f32) for h in heads]
        for h in heads:
            a_ref[rows, vs[h]] = y_diag[h] + from_start_f[h] * y_off[h]
            st_ref[0, h] = jnp.exp(BLK * lgf[h]) * s_in[h] + upd[h]
        fs_ref[c, 0] = st_ref[0]
        return carry

    lax.fori_loop(0, nch, fwd, 0)

    def bwd(j, carry):
        c = nch - 1 - j
        blk = sb * nch + c
        rows = chunk_rows(c)

        @pl.when(tbl_ref[1, blk] == 1)
        def _():
            st_ref[1] = jnp.where(sb >= n_zero, s0_ref[0, 1], 0.0)

        q = [q_ref[rows, ks[h]].astype(bf16) for h in heads]
        v = [v_ref[rows, vs[h]].astype(bf16) for h in heads]
        s_in = [st_ref[1, h] for h in heads]
        y_off = [jnp.dot(q[h], s_in[h].astype(bf16), preferred_element_type=f32) for h in heads]
        kw = [(k_ref[rows, ks[h]].astype(f32) * k_to_end_b[h]).astype(bf16) for h in heads]
        upd = [lax.dot_general(kw[h], v[h], tn, preferred_element_type=f32) for h in heads]
        for h in heads:
            st_ref[1, h] = jnp.exp(BLK * lgb[h]) * s_in[h] + upd[h]
            y = a_ref[rows, vs[h]] + from_start_b[h] * y_off[h]
            mu = jnp.mean(y, -1, keepdims=True)
            yc = y - mu
            var = jnp.mean(yc * yc, -1, keepdims=True)
            gg = g_ref[rows, vs[h]].astype(f32)
            a_ref[rows, vs[h]] = ((yc * lax.rsqrt(var + 1e-5) * nw_ref[:, vs[h]] + nb_ref[:, vs[h]])
                                  * (gg * jax.nn.sigmoid(gg)))
        fs_ref[c, 1] = st_ref[1]
        return carry

    lax.fori_loop(0, nch, bwd, 0)


def ret_scan(q, k, v, g, lg_tab, norm_w, norm_b, s0, n_zero, first, last, sb_rows):
    R = q.shape[0]
    n_sb = R // sb_rows
    nch = sb_rows // BLK
    tbl = jnp.asarray([first, last], jnp.int32)
    f32 = jnp.float32
    hps = RET_HPS
    return pl.pallas_call(
        functools.partial(_ret_kernel, n_zero=n_zero),
        grid_spec=pltpu.PrefetchScalarGridSpec(
            num_scalar_prefetch=1, grid=(n_sb, RET_HEADS // hps),
            in_specs=[pl.BlockSpec((sb_rows, hps * RET_DK), lambda s, h, t: (s, h)),
                      pl.BlockSpec((sb_rows, hps * RET_DK), lambda s, h, t: (s, h)),
                      pl.BlockSpec((sb_rows, hps * RET_DV), lambda s, h, t: (s, h)),
                      pl.BlockSpec((sb_rows, hps * RET_DV), lambda s, h, t: (s, h)),
                      pl.BlockSpec((hps, 8, 128), lambda s, h, t: (h, 0, 0)),
                      pl.BlockSpec((1, hps * RET_DV), lambda s, h, t: (0, h)),
                      pl.BlockSpec((1, hps * RET_DV), lambda s, h, t: (0, h)),
                      pl.BlockSpec((1, 2, hps, RET_DK, RET_DV), lambda s, h, t: (jnp.maximum(s - n_zero, 0), 0, h, 0, 0))],
            out_specs=[pl.BlockSpec((sb_rows, hps * RET_DV), lambda s, h, t: (s, h)),
                       pl.BlockSpec((nch, 2, hps, RET_DK, RET_DV), lambda s, h, t: (s, 0, h, 0, 0))],
            scratch_shapes=[pltpu.VMEM((2, hps, RET_DK, RET_DV), f32)]),
        out_shape=[jax.ShapeDtypeStruct((R, RET_V), f32),
                   jax.ShapeDtypeStruct((R // BLK, 2, RET_HEADS, RET_DK, RET_DV), f32)],
        compiler_params=pltpu.CompilerParams(dimension_semantics=("arbitrary", "arbitrary"),
                                             vmem_limit_bytes=SSD_VMEM_BYTES),
        name="ret_scan",
    )(tbl, q, k, v, g, lg_tab, norm_w.reshape(1, -1), norm_b.reshape(1, -1), s0)


def _mod_kernel(c_ref, w_ref, b_ref, o_ref):
    c = c_ref[...]
    act = c * jax.nn.sigmoid(c)
    o_ref[0] = _dot(act, w_ref[0], ((1,), (0,)), 3) + b_ref[0]


def mod_vectors(conds, mod_w, mod_b):
    depth, D, n6 = mod_w.shape
    tn = D
    return pl.pallas_call(
        _mod_kernel,
        grid=(depth, n6 // tn),
        in_specs=[pl.BlockSpec(conds.shape, lambda i, j: (0, 0)),
                  pl.BlockSpec((1, D, tn), lambda i, j: (i, 0, j)),
                  pl.BlockSpec((1, 1, tn), lambda i, j: (i, 0, j))],
        out_specs=pl.BlockSpec((1, conds.shape[0], tn), lambda i, j: (i, 0, j)),
        out_shape=jax.ShapeDtypeStruct((depth, conds.shape[0], n6), jnp.float32),
        name="mod_vectors",
    )(conds, mod_w, mod_b.reshape(depth, 1, n6))


def rope_tables(n_tokens):
    rows = n_tokens // GRID_W
    row = np.repeat(np.arange(rows), GRID_W).astype(np.float64)
    col = np.tile(np.arange(GRID_W), rows).astype(np.float64)
    n_f = RET_DK // 4
    inv = ROPE_BASE ** (-np.arange(n_f, dtype=np.float64) / n_f)
    ang = np.concatenate([row[:, None] * inv, col[:, None] * inv], -1)
    return np.cos(ang), np.sin(ang)


def _rope_block_tables(n_ctx, n_lat, l_lat):
    nb = l_lat // BLK
    cos, sin = rope_tables(l_lat)
    cosf = np.concatenate([cos, cos], -1).reshape(nb, BLK, RET_DK)
    sinf = np.concatenate([-sin, sin], -1).reshape(nb, BLK, RET_DK)
    cos_t = jnp.asarray(np.concatenate([np.ones((1, BLK, RET_DK)), cosf]), jnp.float32)
    sin_t = jnp.asarray(np.concatenate([np.zeros((1, BLK, RET_DK)), sinf]), jnp.float32)
    rope_blk = [0] * n_ctx + [1 + j for _ in range(n_lat) for j in range(nb)]
    return cos_t, sin_t, rope_blk


def kernel(x_prompt, x_sample, state_ssd, state_rwkv, state_ret, c, c_ctx, mod_w, mod_b, norm1_g, norm2_g,
           router_w, exp_w_gate, exp_w_up, exp_w_down, ab_w_in, ab_w_out, ssd_conv_w, ssd_conv_b, ssd_dt_bias,
           ssd_a_log, ssd_d, ssd_norm_g, rwkv_mu_prev, rwkv_mu_next, rwkv_w0, rwkv_w2, rwkv_a0, rwkv_a2, rwkv_g2,
           rwkv_k_k, rwkv_k_a, rwkv_r_k, rwkv_ln_w, rwkv_ln_b, ret_w_in, ret_w_out, ret_decay_logit, ret_norm_w,
           ret_norm_b, final_norm_g):
    p = dict(mod_w=mod_w, mod_b=mod_b, norm1_g=norm1_g, norm2_g=norm2_g, router_w=router_w,
             exp_w_gate=exp_w_gate, exp_w_up=exp_w_up, exp_w_down=exp_w_down, ab_w_in=ab_w_in, ab_w_out=ab_w_out,
             ssd_conv_w=ssd_conv_w, ssd_conv_b=ssd_conv_b, ssd_dt_bias=ssd_dt_bias, ssd_a_log=ssd_a_log,
             ssd_d=ssd_d, ssd_norm_g=ssd_norm_g, rwkv_mu_prev=rwkv_mu_prev, rwkv_mu_next=rwkv_mu_next,
             rwkv_w0=rwkv_w0, rwkv_w2=rwkv_w2, rwkv_a0=rwkv_a0, rwkv_a2=rwkv_a2, rwkv_g2=rwkv_g2,
             rwkv_k_k=rwkv_k_k, rwkv_k_a=rwkv_k_a, rwkv_r_k=rwkv_r_k, rwkv_ln_w=rwkv_ln_w, rwkv_ln_b=rwkv_ln_b,
             ret_w_in=ret_w_in, ret_w_out=ret_w_out, ret_decay_logit=ret_decay_logit, ret_norm_w=ret_norm_w,
             ret_norm_b=ret_norm_b, final_norm_g=final_norm_g)
    f32, bf16 = jnp.float32, jnp.bfloat16
    n_ctx, l_ctx, D = x_prompt.shape
    n_lat, l_lat, _ = x_sample.shape
    assert l_ctx == BLK and l_lat % BLK == 0 and (n_ctx * BLK) % l_lat == 0
    n_sb_ctx = n_ctx * BLK // l_lat
    cond_id, first, last = _seq_tables(n_ctx, n_lat, l_lat)
    x = (x_prompt.reshape(-1, D), x_sample.reshape(-1, D))

    conds = jnp.concatenate([c_ctx[None, :], c, jnp.zeros((8 - 1 - n_lat, D), f32)])
    mods = mod_vectors(conds, mod_w, mod_b)[:, jnp.asarray(cond_id)]
    mods = mods.reshape(DEPTH, len(cond_id), 6, 1, D)

    new_ssd, new_rwkv, new_ret = [], [], []
    out = None
    for i in range(DEPTH):
        sh1, sc1, g1, sh2, sc2, g2 = (mods[i, :, k] for k in range(6))
        e = i // 2
        if i % 2 == 0:
            w_packed, mup, mun, rwp, w2bd, a2p, g2p, e_ind, et_ind = l0_pack_weights(p, e)
            z, xbc, dt, r, v, an, lw, kd, bv, gate, bonus = l0_in(
                x, norm1_g[i][None], sc1, sh1, w_packed, mup, mun, rwp, w2bd, a2p, g2p, e_ind, et_ind, first, last)
            sel, hp = ssd_tables(p, e)
            s0_ssd = jnp.transpose(state_ssd[:, e], (0, 1, 3, 2, 4)).reshape(n_lat, 2, SSD_N, SSD_INNER)
            ys, fs_ssd = ssd_scan(xbc, dt, ssd_conv_w[e], ssd_conv_b[e], sel, hp,
                                  s0_ssd, n_sb_ctx, first, last, l_lat)
            new_ssd.append(jnp.transpose(fs_ssd[:n_ctx].reshape(n_ctx, 2, SSD_N, SSD_HEADS, SSD_P), (0, 1, 3, 2, 4)))
            s0_rwkv = jnp.transpose(state_rwkv[:, e], (0, 1, 3, 2, 4)).reshape(n_lat, 2, RWKV_N, RWKV_DIM)
            yf, yb, sf_rwkv = rwkv_scan_pallas(r, v, an, lw, kd, bv, s0_rwkv, n_ctx,
                                               _rwkv_steps(n_ctx, n_lat, l_lat))
            new_rwkv.append(jnp.transpose(sf_rwkv[:n_ctx].reshape(n_ctx, 2, RWKV_N, RWKV_HEADS, RWKV_N),
                                          (0, 1, 3, 2, 4)))
            x, hn2, affT = l0_out(ys, z, yf, yb, bonus, gate, ssd_norm_g[e][None], rwkv_ln_w[e][None], rwkv_ln_b[e][None],
                                  e_ind, et_ind, ab_w_out[e].astype(bf16), x, g1, norm2_g[i][None], sc2, sh2,
                                  router_w[i])
        else:
            cos_t, sin_t, rope_blk = _rope_block_tables(n_ctx, n_lat, l_lat)
            q, k, v, gg = l1_in(x, norm1_g[i][None], sc1, sh1, ret_w_in[e].astype(bf16), cos_t, sin_t, rope_blk)
            lg = jax.nn.log_sigmoid(ret_decay_logit[e].astype(f32))
            lg_tab = jnp.zeros((RET_HEADS, 8, 128), f32).at[:, :2, :].set(jnp.transpose(lg)[:, :, None])
            a, fs_ret = ret_scan(q, k, v, gg, lg_tab, ret_norm_w[e], ret_norm_b[e],
                                 state_ret[:, e], n_sb_ctx, first, last, l_lat)
            new_ret.append(fs_ret[:n_ctx])
            x, hn2, affT = l1_out(a, ret_w_out[e].astype(bf16), x, g1, norm2_g[i][None], sc2, sh2, router_w[i])
        fin = final_norm_g if i == DEPTH - 1 else None
        out = moe_layer(x, hn2, affT, g2, exp_w_gate, exp_w_up, exp_w_down, i, n_ctx, l_lat, final_g=fin)
        if fin is None:
            x = out
    y_ctx, y_lat = out
    return (y_ctx.reshape(n_ctx, l_ctx, D), y_lat.reshape(n_lat, l_lat, D),
            jnp.stack(new_ssd, 1), jnp.stack(new_rwkv, 1), jnp.stack(new_ret, 1))
```

```python
import functools
import math

import jax
import jax.numpy as jnp
import numpy as np
from jax import lax
from jax.experimental import pallas as pl
from jax.experimental.pallas import tpu as pltpu

DEPTH = 2
GRID_W = 64
NORM_EPS = 1e-6
SSD_HEADS = 16
SSD_P = 64
SSD_INNER = SSD_HEADS * SSD_P
SSD_GROUPS = 2
SSD_HPG = SSD_HEADS // SSD_GROUPS
SSD_N = 128
SSD_XBC = SSD_INNER + 2 * SSD_GROUPS * SSD_N
SSD_IN = SSD_INNER + SSD_XBC + SSD_HEADS
RWKV_HEADS = 16
RWKV_N = 64
RWKV_DIM = RWKV_HEADS * RWKV_N
W_LORA = 64
A_LORA = 64
G_LORA = 128
RWKV_GN_EPS = 64e-5
RET_HEADS = 8
RET_DK = 128
RET_DV = 256
RET_QK = RET_HEADS * RET_DK
RET_V = RET_HEADS * RET_DV
ROPE_BASE = 10000.0
N_EXPERTS = 16
EC_CAPACITY = 2

ACT_DTYPE = jnp.bfloat16

RWKV_C = 64
RWKV_GH = 4
RWKV_GL = RWKV_GH * RWKV_N
RWKV_DOUBLING_PASSES = (3, 3, 3, 3, 1, 1)


def _split_bf16(x):
    hi = x.astype(jnp.bfloat16)
    lo = (x - hi.astype(jnp.float32)).astype(jnp.bfloat16)
    return hi, lo


def _dot(a, b, dims, passes):
    f = functools.partial(lax.dot_general, dimension_numbers=(dims, ((), ())),
                          preferred_element_type=jnp.float32)
    if passes == 1:
        return f(a.astype(jnp.bfloat16), b.astype(jnp.bfloat16))
    ah, al = _split_bf16(a)
    bh, bl = _split_bf16(b)
    return f(ah, bh) + (f(ah, bl) + f(al, bh))


def _rwkv_chunk_kernel(tbl_ref, r0_ref, v0_ref, a0_ref, r1_ref, v1_ref, a1_ref, lw0_ref, k0_ref, b0_ref,
                       lw1_ref, k1_ref, b1_ref, s0_ref, y0_ref, y1_ref, sf_ref, h_ref, *, p_inv, p_oth, n_zero):
    C, N, GH, GL = RWKV_C, RWKV_N, RWKV_GH, RWKV_GL
    i = pl.program_id(0)
    f32, bf16 = jnp.float32, jnp.bfloat16

    @pl.when(tbl_ref[3, i] == 1)
    def _():
        h_ref[...] = jnp.where(tbl_ref[2, i] >= n_zero, s0_ref[0], 0.0)

    t_i = lax.broadcasted_iota(jnp.int32, (C, GL), 0)
    s_i = lax.broadcasted_iota(jnp.int32, (C, GL), 1) & (N - 1)
    eye = (s_i == t_i).astype(f32)
    row2 = lax.broadcasted_iota(jnp.int32, (2 * C, GL), 0)
    rel2 = (lax.broadcasted_iota(jnp.int32, (2 * C, GL), 1) & (N - 1)) - (row2 & (C - 1))
    incl2 = row2 // C
    mask2 = [rel2 - incl2 < 0, -rel2 - incl2 < 0]
    bh_r = lax.broadcasted_iota(jnp.int32, (GL, GL), 0) // N
    bh_c = lax.broadcasted_iota(jnp.int32, (GL, GL), 1) // N
    blk = bh_r == bh_c
    tt = lax.broadcasted_iota(jnp.int32, (C, C), 0)
    ss = lax.broadcasted_iota(jnp.int32, (C, C), 1)
    tri = [(ss <= tt).astype(bf16), (ss >= tt).astype(bf16)]

    def bd(x, passes):
        pieces = []
        for _ in range(2 if passes == 3 else 1):
            hi = x.astype(bf16)
            x = x - hi.astype(f32)
            pieces.append(jnp.where(blk, jnp.concatenate([hi] * GH, axis=0), jnp.zeros((), bf16)))
        return pieces

    def mm(l, x, passes, dims=((1,), (0,))):
        f = functools.partial(lax.dot_general, dimension_numbers=(dims, ((), ())), preferred_element_type=f32)
        xs = bd(x, passes)
        lh = l.astype(bf16)
        if passes == 1:
            return f(lh, xs[0])
        ll = (l - lh.astype(f32)).astype(bf16)
        m = l.shape[0]
        both = f(jnp.concatenate([lh, ll], axis=0), xs[0])
        if passes == 2:
            return both[:m] + both[m:]
        return both[:m] + (f(lh, xs[1]) + both[m:])

    nt = ((1,), (1,))
    refs = [(r0_ref, v0_ref, a0_ref, lw0_ref, k0_ref, b0_ref), (r1_ref, v1_ref, a1_ref, lw1_ref, k1_ref, b1_ref)]
    lw, r_t, a_t, b_t, k_t, v = [], [], [], [], [], []
    for d, (r_ref, v_ref, a_ref, lw_ref, k_ref, b_ref) in enumerate(refs):
        lwd = lw_ref[0]
        lw_hi, lw_lo = _split_bf16(lwd)
        cum = (jnp.dot(tri[d], lw_hi, preferred_element_type=f32) + jnp.dot(tri[d], lw_lo, preferred_element_type=f32))
        w_inv = jnp.exp(-cum)
        lw.append(lwd)
        r_t.append(r_ref[...].astype(f32) * jnp.exp(cum))
        a_t.append(a_ref[...].astype(f32) * jnp.exp(cum - lwd))
        b_t.append(b_ref[0].astype(f32) * w_inv)
        k_t.append(k_ref[0].astype(f32) * w_inv)
        v.append(v_ref[...].astype(f32))
    lane_head = lax.broadcasted_iota(jnp.int32, (N, GL), 1) // N

    chains = [(d, slice(g * GL, (g + 1) * GL)) for d in range(2) for g in range(RWKV_HEADS // GH)]
    each = lambda fn: [fn(j, d, sl) for j, (d, sl) in enumerate(chains)]
    bg = each(lambda j, d, sl: b_t[d][:, sl])
    kg = each(lambda j, d, sl: k_t[d][:, sl])
    vg = each(lambda j, d, sl: v[d][:, sl])
    h0 = each(lambda j, d, sl: h_ref[d, :, sl])
    ar = each(lambda j, d, sl: jnp.concatenate([a_t[d][:, sl], r_t[d][:, sl]], axis=0))
    m_b = each(lambda j, d, sl: jnp.where(mask2[d], mm(ar[j], bg[j], p_oth, nt), 0.0))
    m_k = each(lambda j, d, sl: jnp.where(mask2[d], mm(ar[j], kg[j], p_oth, nt), 0.0))
    p = each(lambda j, d, sl: mm(m_b[j][:C], m_b[j][:C], p_inv[0]))
    tmat = each(lambda j, d, sl: eye + m_b[j][:C])
    for lv in range(int(math.log2(C)) - 2):
        pt = each(lambda j, d, sl: mm(jnp.concatenate([p[j], tmat[j]], axis=0), p[j], p_inv[1 + lv]))
        p = each(lambda j, d, sl: pt[j][:C])
        tmat = each(lambda j, d, sl: tmat[j] + pt[j][C:])
    tmat = each(lambda j, d, sl: tmat[j] + mm(tmat[j], p[j], p_inv[-1]))
    ar_h = each(lambda j, d, sl: mm(ar[j], h0[j], p_oth, nt))
    mk_v = each(lambda j, d, sl: mm(m_k[j], vg[j], p_oth))
    u = each(lambda j, d, sl: mm(tmat[j], ar_h[j][:C] + mk_v[j][:C], p_oth))
    y = each(lambda j, d, sl: ar_h[j][C:] + mm(m_b[j][C:], u[j], p_oth) + mk_v[j][C:])
    full = each(lambda j, d, sl: _dot(jnp.concatenate([u[j], vg[j]], axis=0), jnp.concatenate([bg[j], kg[j]], axis=0),
                                      ((0,), (0,)), p_oth))
    y_refs = (y0_ref, y1_ref)
    for j, (d, sl) in enumerate(chains):
        y_refs[d][:, sl] = y[j].astype(y_refs[d].dtype)
        z = jnp.zeros((N, GL), f32)
        for hh in range(GH):
            z = z + jnp.where(lane_head == hh, full[j][hh * N:(hh + 1) * N], 0.0)
        w_tot = jnp.exp(jnp.sum(lw[d][:, sl], axis=0, keepdims=True))
        h_ref[d, :, sl] = w_tot * (h0[j] + z)

    @pl.when(tbl_ref[4, i] == 1)
    def _():
        sf_ref[0] = h_ref[...]


def _rwkv_steps(n_ctx, n_lat, l_lat):
    C = RWKV_C
    rows = []
    seqs = [(s, s * BLK, BLK) for s in range(n_ctx)] + [(n_ctx + s, n_ctx * BLK + s * l_lat, l_lat) for s in range(n_lat)]
    for sid, row0, length in seqs:
        nc = length // C
        for j in range(nc):
            rows.append((row0 // C + j, row0 // C + nc - 1 - j, sid, int(j == 0), int(j == nc - 1)))
    return [list(col) for col in zip(*rows)]


def rwkv_scan_pallas(r, v, a, lw, k, b, s0, n_zero, steps, p_inv=RWKV_DOUBLING_PASSES, p_oth=1):
    R, HN = r.shape
    C, N = RWKV_C, RWKV_N
    tbl = jnp.asarray(steps, jnp.int32)
    fwd = pl.BlockSpec((C, HN), lambda i, t: (t[0, i], 0))
    bwd = pl.BlockSpec((C, HN), lambda i, t: (t[1, i], 0))
    fwd_d = pl.BlockSpec((1, C, HN), lambda i, t: (0, t[0, i], 0))
    bwd_d = pl.BlockSpec((1, C, HN), lambda i, t: (1, t[1, i], 0))
    n_seq = max(steps[2]) + 1
    st_in = pl.BlockSpec((1, 2, N, HN), lambda i, t: (jnp.maximum(t[2, i] - n_zero, 0), 0, 0, 0))
    st = pl.BlockSpec((1, 2, N, HN), lambda i, t: (t[2, i], 0, 0, 0))
    return pl.pallas_call(
        functools.partial(_rwkv_chunk_kernel, p_inv=p_inv, p_oth=p_oth, n_zero=n_zero),
        grid_spec=pltpu.PrefetchScalarGridSpec(
            num_scalar_prefetch=1,
            grid=(len(steps[0]),),
            in_specs=[fwd, fwd, fwd, bwd, bwd, bwd, fwd_d, fwd_d, fwd_d, bwd_d, bwd_d, bwd_d, st_in],
            out_specs=[fwd, bwd, st],
            scratch_shapes=[pltpu.VMEM((2, N, HN), jnp.float32)]),
        out_shape=[jax.ShapeDtypeStruct((R, HN), ACT_DTYPE), jax.ShapeDtypeStruct((R, HN), ACT_DTYPE),
                   jax.ShapeDtypeStruct((n_seq, 2, N, HN), jnp.float32)],
        compiler_params=pltpu.CompilerParams(dimension_semantics=("arbitrary",)),
        name="rwkv_scan",
    )(tbl, r, v, a, r, v, a, lw, k, b, lw, k, b, s0)


BLK = 256
FF_TILE = 768
SELECT_TILE = 512
SCATTER_WINDOW = 64
SLOT_ALIGN = 16
SELECT_MIN_EXP = -1100.0
SELECT_BINADE_STEPS = 11
SELECT_MANTISSA_STEPS = 40
MOE_FFN_VMEM_BYTES = 48 * 1024 * 1024


def _moe_select_kernel(aff_ref, slot_ref, *, cap):
    a = aff_ref[...]
    E, T = a.shape
    f32 = jnp.float32

    def enough(piv):
        return jnp.sum(jnp.where(a >= piv, 1.0, 0.0), axis=1, keepdims=True) >= cap

    def binade(_, lohi):
        e_lo, e_hi = lohi
        mid = jnp.floor((e_lo + e_hi) * 0.5)
        ok = enough(jnp.exp2(mid))
        return jnp.where(ok, mid, e_lo), jnp.where(ok, e_hi, mid)

    e_lo, e_hi = lax.fori_loop(0, SELECT_BINADE_STEPS, binade,
                               (jnp.full((E, 1), SELECT_MIN_EXP, f32), jnp.full((E, 1), 1.0, f32)))

    def inside(_, lohi):
        lo, hi = lohi
        mid = lo + (hi - lo) * 0.5
        ok = enough(mid)
        return jnp.where(ok, mid, lo), jnp.where(ok, hi, mid)

    thr, _ = lax.fori_loop(0, SELECT_MANTISSA_STEPS, inside, (jnp.exp2(e_lo), jnp.exp2(e_hi)))
    gt = a > thr
    eq = a == thr
    need = cap - jnp.sum(jnp.where(gt, 1.0, 0.0), axis=1, keepdims=True)
    tw = min(T, SELECT_TILE)

    def prefix_count(mask):
        m = jnp.where(mask, 1.0, 0.0).astype(jnp.bfloat16)
        outs = []
        for j in range(T // tw):
            s_i = lax.broadcasted_iota(jnp.int32, (T, tw), 0)
            t_i = lax.broadcasted_iota(jnp.int32, (T, tw), 1) + j * tw
            before = jnp.where(s_i < t_i, 1.0, 0.0).astype(jnp.bfloat16)
            outs.append(jnp.dot(m, before, preferred_element_type=f32))
        return outs[0] if len(outs) == 1 else jnp.concatenate(outs, axis=1)

    sel = gt | (eq & (prefix_count(eq) < need))
    slot_ref[...] = jnp.where(sel, prefix_count(sel).astype(jnp.int32), -1)


def _moe_select(affT, row0, n_seq, t):
    E = affT.shape[0]
    rows = jnp.transpose(affT[:, row0:row0 + n_seq * t].reshape(E, n_seq, t), (1, 0, 2)).reshape(n_seq * E, t)
    slot = pl.pallas_call(
        functools.partial(_moe_select_kernel, cap=EC_CAPACITY * t // N_EXPERTS),
        grid=(1,),
        in_specs=[pl.BlockSpec((n_seq * E, t), lambda s: (0, 0))],
        out_specs=pl.BlockSpec((n_seq * E, t), lambda s: (0, 0)),
        out_shape=jax.ShapeDtypeStruct((n_seq * E, t), jnp.int32),
        name="moe_select",
    )(rows)
    return jnp.transpose(slot.reshape(n_seq, E, t), (1, 0, 2)).reshape(E, n_seq * t)


def _moe_gather_kernel(tbl_ref, slot_ref, aff_ref, hn_ref, xe_ref, gate_ref, *, cap, win, nb):
    E = slot_ref.shape[0]
    s = pl.program_id(0)
    w_i = lax.broadcasted_iota(jnp.int32, (win, BLK), 0)
    xe_ref[...] = jnp.zeros_like(xe_ref)
    gate_ref[...] = jnp.zeros_like(gate_ref)
    for j in range(nb):
        blk = s * nb + j
        toks = slice(j * BLK, (j + 1) * BLK)
        hn_blk = hn_ref[toks, :]

        def window(w, carry):
            hits, starts = [], []
            for e in range(E):
                lo = tbl_ref[e, blk] + w * win
                start = pl.multiple_of(jnp.minimum(lo, cap - win), SLOT_ALIGN)
                slot = slot_ref[e, :, toks]
                slot = jnp.where(slot >= lo, slot, -1)
                hits.append(slot == w_i + start)
                starts.append(start)
            onehot = jnp.concatenate([jnp.where(h, 1.0, 0.0) for h in hits], axis=0).astype(jnp.bfloat16)
            rows = jnp.dot(onehot, hn_blk, preferred_element_type=jnp.float32)
            for e in range(E):
                dst = pl.ds(starts[e], win)
                xe_ref[e, dst, :] += rows[e * win:(e + 1) * win].astype(xe_ref.dtype)
                g = jnp.sum(jnp.where(hits[e], aff_ref[e, :, toks], 0.0), axis=1, keepdims=True)
                gate_ref[e, dst, :] += jnp.broadcast_to(g, (win, 128))
            return carry

        lax.fori_loop(0, tbl_ref[E, blk], window, 0)


def _moe_gather(slot, slot3, aff3, hn, row0, n_seq, t):
    E = slot3.shape[0]
    D = hn.shape[1]
    cap = EC_CAPACITY * t // N_EXPERTS
    win = min(cap, SCATTER_WINDOW)
    nb = t // BLK
    b0 = row0 // t
    tbl = _scatter_windows(slot, row0, n_seq, t, win)
    return pl.pallas_call(
        functools.partial(_moe_gather_kernel, cap=cap, win=win, nb=nb),
        grid_spec=pltpu.PrefetchScalarGridSpec(
            num_scalar_prefetch=1, grid=(n_seq,),
            in_specs=[pl.BlockSpec((E, 1, t), lambda s, tb_: (0, 0, b0 + s)),
                      pl.BlockSpec((E, 1, t), lambda s, tb_: (0, 0, b0 + s)),
                      pl.BlockSpec((t, D), lambda s, tb_: (b0 + s, 0))],
            out_specs=[pl.BlockSpec((E, cap, D), lambda s, tb_: (0, s, 0)),
                       pl.BlockSpec((E, cap, 128), lambda s, tb_: (0, s, 0))]),
        out_shape=[jax.ShapeDtypeStruct((E, n_seq * cap, D), jnp.bfloat16),
                   jax.ShapeDtypeStruct((E, n_seq * cap, 128), jnp.float32)],
        compiler_params=pltpu.CompilerParams(dimension_semantics=("arbitrary",),
                                             vmem_limit_bytes=MOE_FFN_VMEM_BYTES),
        name="moe_gather",
    )(tbl, slot3, aff3, hn)


def _moe_ffn_kernel(xc_ref, xl_ref, gc_ref, gl_ref, wg_ref, wu_ref, wd_ref, yc_ref, yl_ref, acc_ref, *, nf):
    f = pl.program_id(1)
    bf16 = jnp.bfloat16
    wg = wg_ref[0, 0].astype(bf16)
    wu = wu_ref[0, 0].astype(bf16)
    wd = wd_ref[0, 0].astype(bf16)
    nc = xc_ref.shape[1]
    x = jnp.concatenate([xc_ref[0], xl_ref[0]], axis=0)
    g = jnp.dot(x, wg, preferred_element_type=jnp.float32)
    u = jnp.dot(x, wu, preferred_element_type=jnp.float32)
    h = (g * jax.nn.sigmoid(g) * u).astype(bf16)
    y = jnp.dot(h, wd, preferred_element_type=jnp.float32)

    if nf > 1:
        @pl.when(f == 0)
        def _():
            acc_ref[...] = y

        @pl.when((f != 0) & (f != nf - 1))
        def _():
            acc_ref[...] += y

    @pl.when(f == nf - 1)
    def _():
        total = y if nf == 1 else acc_ref[...] + y
        gate = jnp.concatenate([gc_ref[0], gl_ref[0]], axis=0)
        out = total * jnp.concatenate([gate] * (total.shape[1] // 128), axis=1)
        yc_ref[0] = out[:nc].astype(yc_ref.dtype)
        yl_ref[0] = out[nc:].astype(yl_ref.dtype)


def _moe_ffn(xc, xl, gc, gl, wg, wu, wd, layer):
    E, nc_rows, D = xc.shape
    nl_rows = xl.shape[1]
    F = wg.shape[3]
    nf = F // FF_TILE
    return pl.pallas_call(
        functools.partial(_moe_ffn_kernel, nf=nf),
        grid=(E, nf),
        in_specs=[pl.BlockSpec((1, nc_rows, D), lambda e, f: (e, 0, 0)),
                  pl.BlockSpec((1, nl_rows, D), lambda e, f: (e, 0, 0)),
                  pl.BlockSpec((1, nc_rows, 128), lambda e, f: (e, 0, 0)),
                  pl.BlockSpec((1, nl_rows, 128), lambda e, f: (e, 0, 0)),
                  pl.BlockSpec((1, 1, D, FF_TILE), lambda e, f: (layer, e, 0, f)),
                  pl.BlockSpec((1, 1, D, FF_TILE), lambda e, f: (layer, e, 0, f)),
                  pl.BlockSpec((1, 1, FF_TILE, D), lambda e, f: (layer, e, f, 0))],
        out_specs=[pl.BlockSpec((1, nc_rows, D), lambda e, f: (e, 0, 0)),
                   pl.BlockSpec((1, nl_rows, D), lambda e, f: (e, 0, 0))],
        out_shape=[jax.ShapeDtypeStruct((E, nc_rows, D), jnp.bfloat16),
                   jax.ShapeDtypeStruct((E, nl_rows, D), jnp.bfloat16)],
        scratch_shapes=[pltpu.VMEM((nc_rows + nl_rows, D), jnp.float32)],
        compiler_params=pltpu.CompilerParams(dimension_semantics=("arbitrary", "arbitrary"),
                                             vmem_limit_bytes=MOE_FFN_VMEM_BYTES),
        name="moe_ffn",
    )(xc, xl, gc, gl, wg, wu, wd)


def _moe_scatter_kernel(tbl_ref, slot_ref, ye_ref, x_ref, g2_ref, *rest, cap, win, nb, final):
    if final:
        fg_ref, o_ref, acc_ref = rest
    else:
        o_ref, acc_ref = rest
    E = ye_ref.shape[0]
    tb = x_ref.shape[0]
    blk = pl.program_id(0) * nb + pl.program_id(1)
    w_i = lax.broadcasted_iota(jnp.int32, (win, tb), 0)
    acc_ref[...] = jnp.zeros_like(acc_ref)

    def window(w, carry):
        hot, rows = [], []
        for e in range(E):
            lo = tbl_ref[e, blk] + w * win
            start = pl.multiple_of(jnp.minimum(lo, cap - win), SLOT_ALIGN)
            slot = slot_ref[e]
            slot = jnp.where(slot >= lo, slot, -1)
            hot.append(jnp.where(slot == w_i + start, 1.0, 0.0))
            rows.append(ye_ref[e, pl.ds(start, win), :])
        onehot = jnp.concatenate(hot, axis=0).astype(jnp.bfloat16)
        acc_ref[...] += lax.dot_general(onehot, jnp.concatenate(rows, axis=0), (((0,), (0,)), ((), ())),
                                        preferred_element_type=jnp.float32)
        return carry

    lax.fori_loop(0, tbl_ref[E, blk], window, 0)
    x = x_ref[...] + g2_ref[0] * acc_ref[...]
    if final:
        x = x * lax.rsqrt(jnp.mean(x * x, -1, keepdims=True) + NORM_EPS) * fg_ref[...]
    o_ref[...] = x


def _scatter_windows(slot, row0, n_seq, t, win):
    E = slot.shape[0]
    cap = EC_CAPACITY * t // N_EXPERTS
    nb = t // BLK
    cnt = jnp.sum(slot[:, row0:row0 + n_seq * t].reshape(E, n_seq, nb, BLK) >= 0, axis=3)
    first = jnp.cumsum(cnt, axis=2) - cnt
    start = jnp.minimum(first // SLOT_ALIGN * SLOT_ALIGN, cap - win)
    n_win = jnp.maximum(jnp.max((first + cnt - start + win - 1) // win, axis=0), 1)
    return jnp.concatenate([start.reshape(E, n_seq * nb), n_win.reshape(1, n_seq * nb)]).astype(jnp.int32)


def _moe_scatter(slot, slot3, ye, x, g2blk, row0, n_seq, t, final_g=None):
    E, _, D = ye.shape
    cap = EC_CAPACITY * t // N_EXPERTS
    win = min(cap, SCATTER_WINDOW)
    nb = t // BLK
    b0 = row0 // BLK
    final = final_g is not None
    tbl = _scatter_windows(slot, row0, n_seq, t, win)
    in_specs = [pl.BlockSpec((E, 1, BLK), lambda s, j, tb_: (0, 0, b0 + s * nb + j)),
                pl.BlockSpec((E, cap, D), lambda s, j, tb_: (0, s, 0)),
                pl.BlockSpec((BLK, D), lambda s, j, tb_: (b0 + s * nb + j, 0)),
                pl.BlockSpec((1, 1, D), lambda s, j, tb_: (b0 + s * nb + j, 0, 0))]
    args = [tbl, slot3, ye, x, g2blk]
    if final:
        in_specs.append(pl.BlockSpec((1, D), lambda s, j, tb_: (0, 0)))
        args.append(final_g.reshape(1, D))
        out_specs = pl.BlockSpec((BLK, D), lambda s, j, tb_: (s * nb + j, 0))
        out_shape = jax.ShapeDtypeStruct((n_seq * t, D), jnp.float32)
        aliases = {}
    else:
        out_specs = pl.BlockSpec((BLK, D), lambda s, j, tb_: (b0 + s * nb + j, 0))
        out_shape = jax.ShapeDtypeStruct(x.shape, jnp.float32)
        aliases = {3: 0}
    return pl.pallas_call(
        functools.partial(_moe_scatter_kernel, cap=cap, win=win, nb=nb, final=final),
        grid_spec=pltpu.PrefetchScalarGridSpec(
            num_scalar_prefetch=1, grid=(n_seq, nb), in_specs=in_specs, out_specs=out_specs,
            scratch_shapes=[pltpu.VMEM((BLK, D), jnp.float32)]),
        out_shape=out_shape,
        input_output_aliases=aliases,
        compiler_params=pltpu.CompilerParams(dimension_semantics=("arbitrary", "arbitrary"),
                                             vmem_limit_bytes=MOE_FFN_VMEM_BYTES),
        name="moe_scatter",
    )(*args)


def moe_layer(x, hn, affT, g2blk, wg, wu, wd, layer, n_ctx, l_lat, final_g=None):
    R = x.shape[0]
    r_ctx = n_ctx * BLK
    n_lat = (R - r_ctx) // l_lat
    slot = jnp.concatenate([_moe_select(affT, 0, n_ctx, BLK), _moe_select(affT, r_ctx, n_lat, l_lat)], axis=1)
    slot3 = slot[:, None, :]
    aff3 = affT[:, None, :]
    xc, gc = _moe_gather(slot, slot3, aff3, hn, 0, n_ctx, BLK)
    xl, gl = _moe_gather(slot, slot3, aff3, hn, r_ctx, n_lat, l_lat)
    yc, yl = _moe_ffn(xc, xl, gc, gl, wg, wu, wd, layer)
    if final_g is None:
        x = _moe_scatter(slot, slot3, yc, x, g2blk, 0, n_ctx, BLK)
        return _moe_scatter(slot, slot3, yl, x, g2blk, r_ctx, n_lat, l_lat)
    return (_moe_scatter(slot, slot3, yc, x, g2blk, 0, n_ctx, BLK, final_g),
            _moe_scatter(slot, slot3, yl, x, g2blk, r_ctx, n_lat, l_lat, final_g))


L0_Z = (0, 1024)
L0_XBC = (1024, 2560)
L0_SHIFT = (2560, 6016)
L0_DT = (6016, 6144)
L0_COLS = 6144
L0_VMEM_BYTES = 56 * 1024 * 1024


def _sum_split(x, m, n_split):
    acc = None
    for _ in range(n_split):
        hi = x.astype(jnp.bfloat16)
        x = x - hi.astype(jnp.float32)
        t = jnp.dot(hi, m, preferred_element_type=jnp.float32)
        acc = t if acc is None else acc + t
    return acc


def _sum_split_left(m, x, n_split):
    acc = None
    for _ in range(n_split):
        hi = x.astype(jnp.bfloat16)
        x = x - hi.astype(jnp.float32)
        t = jnp.dot(m, hi, preferred_element_type=jnp.float32)
        acc = t if acc is None else acc + t
    return acc


def _head_sum(x, e_ref, et_ref):
    return _sum_split(_sum_split(x, e_ref[...], 2), et_ref[...], 2)


def _adaln(x, g, sc, sh):
    y = x * lax.rsqrt(jnp.mean(x * x, -1, keepdims=True) + NORM_EPS) * g
    return y * (1.0 + sc) + sh


def _softplus(x):
    return jnp.maximum(x, 0.0) + jnp.log(1.0 + jnp.exp(-jnp.abs(x)))


def _l0_in_kernel(tbl_ref, *refs, n_parts, n_a):
    x_refs, xp_refs, xn_refs = refs[:n_parts], refs[n_parts:2 * n_parts], refs[2 * n_parts:3 * n_parts]
    (g_ref, sc_ref, sh_ref, w_ref, mup_ref, mun_ref, rwp_ref, w2_ref, a2_ref, g2_ref, e_ref, et_ref,
     z_ref, xbc_ref, dt_ref, r_ref, v_ref, an_ref, lw_ref, kd_ref, bv_ref, gate_ref, bonus_ref) = refs[3 * n_parts:]
    i = pl.program_id(0)
    f32, bf16 = jnp.float32, jnp.bfloat16
    g, sc, sh = g_ref[...], sc_ref[0], sh_ref[0]
    hn = _adaln(_residual_rows(x_refs, n_a), g, sc, sh).astype(bf16)
    halo = _adaln(jnp.concatenate([_residual_rows(xp_refs, n_a), _residual_rows(xn_refs, n_a)], axis=0),
                  g, sc, sh).astype(bf16)
    hn_halo = jnp.concatenate([hn, halo], axis=0)

    keep_prev = (1 - tbl_ref[0, i]).astype(f32)
    keep_next = (1 - tbl_ref[1, i]).astype(f32)
    row = lax.broadcasted_iota(jnp.int32, (BLK, 1), 0)
    c = RWKV_DIM

    def plain(c0, c1):
        return jnp.dot(hn, w_ref[:, c0:c1], preferred_element_type=f32)

    def proj(c0, c1):
        return jnp.dot(hn_halo, w_ref[:, L0_SHIFT[0] + c0:L0_SHIFT[0] + c1], preferred_element_type=f32)

    def shift(both, c0, c1):
        cur = both[:BLK]
        prev = jnp.where(row == 0, both[BLK + 7:BLK + 8] * keep_prev, pltpu.roll(cur, 1, 0))
        nxt = jnp.where(row == BLK - 1, both[BLK + 8:BLK + 9] * keep_next, pltpu.roll(cur, BLK - 1, 0))
        return cur + mup_ref[:, c0:c1] * (prev - cur) + mun_ref[:, c0:c1] * (nxt - cur)

    lo = 3 * c + 2 * W_LORA
    p_wl = proj(3 * c, lo)
    p_ag = proj(lo, lo + 256)
    p_k = proj(c, 2 * c)
    wl = shift(p_wl, 3 * c, lo)
    ag = shift(p_ag, lo, lo + 256)
    p_r = proj(0, c)
    k = shift(p_k, c, 2 * c)
    k_k, k_a, r_k = rwp_ref[0:1], rwp_ref[1:2], rwp_ref[2:3]
    w_lin = jnp.dot(jnp.tanh(wl).astype(bf16), w2_ref[...], preferred_element_type=f32)
    a_lora = jnp.dot(ag.astype(bf16), a2_ref[...], preferred_element_type=f32)
    gate_ref[...] = jnp.dot(jax.nn.sigmoid(ag).astype(bf16), g2_ref[...],
                            preferred_element_type=f32).astype(gate_ref.dtype)
    p_v = proj(2 * c, 3 * c)
    r = shift(p_r, 0, c)
    r_ref[...] = r.astype(r_ref.dtype)
    kk = k * k_k
    kk = kk * lax.rsqrt(_head_sum(kk * kk, e_ref, et_ref) + 1e-12)
    an_ref[...] = (-kk).astype(an_ref.dtype)
    z_ref[...] = plain(L0_Z[0], L0_Z[1]).astype(z_ref.dtype)
    v = shift(p_v, 2 * c, 3 * c)
    v_ref[...] = v.astype(v_ref.dtype)
    kd_sum = None
    xbc_cols = (L0_XBC[0], (L0_XBC[0] + L0_XBC[1]) // 2, L0_XBC[1])
    for d in range(2):
        xbc_ref[:, xbc_cols[d] - L0_XBC[0]:xbc_cols[d + 1] - L0_XBC[0]] = plain(xbc_cols[d], xbc_cols[d + 1])
        w_log = -_softplus(-(rwp_ref[3 + d:4 + d] + w_lin[:, d * c:(d + 1) * c])) - 0.5
        lw_ref[d] = -jnp.exp(w_log)
        a = jax.nn.sigmoid(rwp_ref[5 + d:6 + d] + a_lora)
        kd = k * (1.0 + (a - 1.0) * k_a)
        kd_ref[d] = kd.astype(kd_ref.dtype)
        bv_ref[d] = (kk * a).astype(bv_ref.dtype)
        kd_sum = kd if kd_sum is None else kd_sum + kd
    dt_ref[...] = plain(L0_DT[0], L0_DT[1])
    bonus_ref[...] = (_head_sum(r * kd_sum * r_k, e_ref, et_ref) * v).astype(bonus_ref.dtype)


def _seq_tables(n_ctx, n_lat, l_lat):
    nb = l_lat // BLK
    cond = [0] * n_ctx + [1 + s for s in range(n_lat) for _ in range(nb)]
    first = [1] * n_ctx + [1 if j == 0 else 0 for _ in range(n_lat) for j in range(nb)]
    last = [1] * n_ctx + [1 if j == nb - 1 else 0 for _ in range(n_lat) for j in range(nb)]
    return cond, first, last


def l0_in(x, g1, scb, shb, w_packed, mup, mun, rwp, w2bd, a2p, g2p, e_ind, et_ind, first, last):
    xs = x if isinstance(x, (tuple, list)) else (x,)
    D = xs[0].shape[1]
    R = sum(a.shape[0] for a in xs)
    nblk = R // BLK
    n_a = xs[0].shape[0] // BLK
    tbl = jnp.asarray([first, last], jnp.int32)
    c = RWKV_DIM
    row = lambda i, t: (i, 0)
    full = lambda shape: pl.BlockSpec(shape, lambda i, t: (0,) * len(shape))
    rows = lambda n: pl.BlockSpec((BLK, n), row)
    rows2 = lambda n: pl.BlockSpec((2, BLK, n), lambda i, t: (0, i, 0))
    f32, act = jnp.float32, ACT_DTYPE
    sds = jax.ShapeDtypeStruct
    h8 = BLK // 8

    def part_specs(k):
        nb_k, b0 = xs[k].shape[0] // BLK, (0 if k == 0 else n_a)
        local = lambda i: jnp.clip(i - b0, 0, nb_k - 1)
        return (pl.BlockSpec((BLK, D), lambda i, t: (local(i), 0)),
                pl.BlockSpec((8, D), lambda i, t: (jnp.maximum(local(i) * h8 - 1, 0), 0)),
                pl.BlockSpec((8, D), lambda i, t: (jnp.minimum((local(i) + 1) * h8, nb_k * h8 - 1), 0)))

    specs = [part_specs(k) for k in range(len(xs))]
    x_specs = [s[j] for j in range(3) for s in specs]
    return pl.pallas_call(
        functools.partial(_l0_in_kernel, n_parts=len(xs), n_a=n_a),
        grid_spec=pltpu.PrefetchScalarGridSpec(
            num_scalar_prefetch=1,
            grid=(nblk,),
            in_specs=[*x_specs,
                      full((1, D)),
                      pl.BlockSpec((1, 1, D), lambda i, t: (i, 0, 0)),
                      pl.BlockSpec((1, 1, D), lambda i, t: (i, 0, 0)),
                      full((D, L0_COLS)), full(mup.shape), full(mun.shape), full(rwp.shape),
                      full(w2bd.shape), full(a2p.shape), full(g2p.shape), full(e_ind.shape), full(et_ind.shape)],
            out_specs=[rows(c), rows(SSD_XBC), rows(128), rows(c), rows(c), rows(c),
                       rows2(c), rows2(c), rows2(c), rows(c), rows(c)]),
        out_shape=[sds((R, c), act), sds((R, SSD_XBC), f32), sds((R, 128), f32), sds((R, c), act), sds((R, c), act),
                   sds((R, c), act), sds((2, R, c), f32), sds((2, R, c), act), sds((2, R, c), act),
                   sds((R, c), act), sds((R, c), act)],
        compiler_params=pltpu.CompilerParams(dimension_semantics=("arbitrary",), vmem_limit_bytes=L0_VMEM_BYTES),
        name="l0_in",
    )(tbl, *xs, *xs, *xs, g1, scb, shb, w_packed, mup, mun, rwp, w2bd, a2p, g2p, e_ind, et_ind)


def l0_pack_weights(p, e):
    bf16 = jnp.bfloat16
    w = p['ab_w_in'][e]
    D = w.shape[0]
    c = RWKV_DIM
    rw0 = SSD_IN
    ag0 = rw0 + 3 * c + 2 * W_LORA
    w_packed = jnp.concatenate([
        w[:, :SSD_INNER + SSD_XBC], w[:, rw0:ag0], w[:, ag0:ag0 + A_LORA + G_LORA],
        jnp.zeros((D, 256 - A_LORA - G_LORA), w.dtype),
        w[:, SSD_INNER + SSD_XBC:SSD_IN], jnp.zeros((D, 128 - SSD_HEADS), w.dtype)], axis=1).astype(bf16)

    def pack_mu(mu):
        return jnp.concatenate([mu, jnp.zeros((256 - A_LORA - G_LORA,), mu.dtype)])[None, :]

    rwp = jnp.stack([p['rwkv_k_k'][e], p['rwkv_k_a'][e], p['rwkv_r_k'][e].reshape(-1), p['rwkv_w0'][e, 0],
                     p['rwkv_w0'][e, 1], p['rwkv_a0'][e, 0], p['rwkv_a0'][e, 1], jnp.zeros((c,), jnp.float32)])
    zw = jnp.zeros((W_LORA, c), jnp.float32)
    w2bd = jnp.concatenate([jnp.concatenate([p['rwkv_w2'][e, 0], zw], axis=1),
                            jnp.concatenate([zw, p['rwkv_w2'][e, 1]], axis=1)], axis=0).astype(bf16)
    a2p = jnp.concatenate([p['rwkv_a2'][e], jnp.zeros((256 - A_LORA, c), jnp.float32)], axis=0).astype(bf16)
    g2p = jnp.concatenate([jnp.zeros((A_LORA, c), jnp.float32), p['rwkv_g2'][e],
                           jnp.zeros((256 - A_LORA - G_LORA, c), jnp.float32)], axis=0).astype(bf16)
    head = jnp.arange(c) // RWKV_N
    e_ind = (head[:, None] == jnp.arange(128)[None, :]).astype(bf16)
    return w_packed, pack_mu(p['rwkv_mu_prev'][e]), pack_mu(p['rwkv_mu_next'][e]), rwp, w2bd, a2p, g2p, e_ind, e_ind.T


SSD_QH = 8
SSD_VMEM_BYTES = 48 * 1024 * 1024
NEG_BIG = -1e30
LOG2E = 1.4426950408889634


def _conv_silu(cur, prev_row, next_row, w_ref, b_ref):
    row = lax.broadcasted_iota(jnp.int32, (BLK, 1), 0)
    prev = jnp.where(row == 0, prev_row, pltpu.roll(cur, 1, 0))
    nxt = jnp.where(row == BLK - 1, next_row, pltpu.roll(cur, BLK - 1, 0))
    y = w_ref[0:1] * prev + w_ref[1:2] * cur + w_ref[2:3] * nxt + b_ref[...]
    return y * jax.nn.sigmoid(y)


def _ssd_kernel(tbl_ref, xs_ref, b_ref, c_ref, dt_ref, cwx_ref, cwb_ref, cwc_ref, cbx_ref, cbb_ref, cbc_ref,
                sel_ref, hp_ref, s0_ref, y_ref, fs_ref, xa_ref, ba_ref, ca_ref, sfx_ref, ldb_ref, st_ref, *, n_zero):
    f32, bf16 = jnp.float32, jnp.bfloat16
    sb = pl.program_id(0)
    sbr, qw = xs_ref.shape
    nch = sbr // BLK
    P, QH = SSD_P, SSD_QH
    t_i = lax.broadcasted_iota(jnp.int32, (BLK, BLK), 0)
    s_i = lax.broadcasted_iota(jnp.int32, (BLK, BLK), 1)
    lower = s_i <= t_i
    upper = s_i >= t_i
    tri_lo = jnp.where(lower, 1.0, 0.0).astype(bf16)
    tri_up = jnp.where(upper, 1.0, 0.0).astype(bf16)
    hp = hp_ref[0]
    sel = sel_ref[0]
    ind = jnp.where(lax.broadcasted_iota(jnp.int32, (128, qw), 1) // P == lax.broadcasted_iota(jnp.int32, (128, qw), 0),
                    1.0, 0.0).astype(bf16)
    head_of_lane = lax.broadcasted_iota(jnp.int32, (1, qw), 1) // P

    def expand(cols):
        return _sum_split(cols, ind, 2)

    d_row = expand(jnp.broadcast_to(hp[4:5], (8, 128)))[0:1]

    def chunk_rows(c):
        return pl.ds(pl.multiple_of(c * BLK, BLK), BLK)

    def neighbours(ref, c, keep_prev, keep_next):
        lo = jnp.maximum(c * BLK - 1, 0)
        hi = jnp.minimum((c + 1) * BLK, sbr - 1)
        return ref[pl.ds(lo, 1), :] * keep_prev, ref[pl.ds(hi, 1), :] * keep_next

    def fwd(c, carry):
        blk = sb * nch + c
        first, last = tbl_ref[0, blk], tbl_ref[1, blk]
        kp, kn = (1 - first).astype(f32), (1 - last).astype(f32)
        rows = chunk_rows(c)

        @pl.when(first == 1)
        def _():
            st_ref[0] = jnp.where(sb >= n_zero, s0_ref[0, 0], 0.0)

        xa = _conv_silu(xs_ref[rows, :], *neighbours(xs_ref, c, kp, kn), cwx_ref, cbx_ref)
        bm = _conv_silu(b_ref[rows, :], *neighbours(b_ref, c, kp, kn), cwb_ref, cbb_ref)
        cm = _conv_silu(c_ref[rows, :], *neighbours(c_ref, c, kp, kn), cwc_ref, cbc_ref)
        xb, bmb, cmb = xa.astype(bf16), bm.astype(bf16), cm.astype(bf16)
        xa_ref[rows, :] = xb
        ba_ref[rows, :] = bmb
        ca_ref[rows, :] = cmb
        dtq = _sum_split(dt_ref[rows, :], sel, 2)
        dtf = _softplus(dtq + hp[0:1])
        dtb = _softplus(dtq + hp[1:2])
        acs = _sum_split_left(tri_lo, dtf * hp[2:3], 3)
        sfx = _sum_split_left(tri_up, dtb * hp[3:4], 3)
        ldf, ldb = jnp.log(dtf), jnp.log(dtb)
        sfx_ref[rows, :] = sfx
        ldb_ref[rows, :] = ldb
        a2, s2 = acs * LOG2E, sfx * LOG2E
        a2r = (a2 - ldf * LOG2E).T
        s2r = (s2 - ldb * LOG2E).T
        g = lax.dot_general(cmb, bmb, (((1,), (1,)), ((), ())), preferred_element_type=f32)
        y_diag = None
        for j in range(QH):
            m = (g * (jnp.exp2(jnp.where(lower, a2[:, j:j + 1] - a2r[j:j + 1, :], NEG_BIG))
                      + jnp.exp2(jnp.where(upper, s2[:, j:j + 1] - s2r[j:j + 1, :], NEG_BIG)))).astype(bf16)
            xh = jnp.where(head_of_lane == j, xb, jnp.zeros((), bf16))
            t = jnp.dot(m, xh, preferred_element_type=f32)
            y_diag = t if y_diag is None else y_diag + t
        ea = jnp.exp(expand(acs))
        wf = jnp.exp(expand(acs[BLK - 1:BLK] - acs + ldf))
        s_in = st_ref[0]
        y_ref[rows, :] = xa * d_row + y_diag + ea * jnp.dot(cmb, s_in.astype(bf16), preferred_element_type=f32)
        st_ref[0] = ea[BLK - 1:BLK] * s_in + lax.dot_general(
            bmb, (xa * wf).astype(bf16), (((0,), (0,)), ((), ())), preferred_element_type=f32)
        fs_ref[c, 0] = st_ref[0]
        return carry

    lax.fori_loop(0, nch, fwd, 0)

    def bwd(k, carry):
        c = nch - 1 - k
        blk = sb * nch + c
        rows = chunk_rows(c)

        @pl.when(tbl_ref[1, blk] == 1)
        def _():
            st_ref[1] = jnp.where(sb >= n_zero, s0_ref[0, 1], 0.0)

        sfx = sfx_ref[rows, :]
        eb = jnp.exp(expand(sfx))
        wb = jnp.exp(expand(sfx[0:1] - sfx + ldb_ref[rows, :]))
        s_in = st_ref[1]
        y_ref[rows, :] += eb * jnp.dot(ca_ref[rows, :], s_in.astype(bf16), preferred_element_type=f32)
        st_ref[1] = eb[0:1] * s_in + lax.dot_general(
            ba_ref[rows, :], (xa_ref[rows, :].astype(f32) * wb).astype(bf16), (((0,), (0,)), ((), ())),
            preferred_element_type=f32)
        fs_ref[c, 1] = st_ref[1]
        return carry

    lax.fori_loop(0, nch, bwd, 0)


def ssd_scan(xbc, dt, conv_w, conv_b, sel, hp, s0, n_zero, first, last, sb_rows):
    R = xbc.shape[0]
    n_sb = R // sb_rows
    nch = sb_rows // BLK
    nq = SSD_HEADS // SSD_QH
    qw = SSD_QH * SSD_P
    qpg = SSD_HPG // SSD_QH
    b_blk = SSD_INNER // SSD_N
    c_blk = b_blk + SSD_GROUPS
    tbl = jnp.asarray([first, last], jnp.int32)
    cw = conv_w
    cb = conv_b.reshape(1, -1)
    f32, bf16 = jnp.float32, jnp.bfloat16
    return pl.pallas_call(
        functools.partial(_ssd_kernel, n_zero=n_zero),
        grid_spec=pltpu.PrefetchScalarGridSpec(
            num_scalar_prefetch=1,
            grid=(n_sb, nq),
            in_specs=[pl.BlockSpec((sb_rows, qw), lambda s, q, t: (s, q)),
                      pl.BlockSpec((sb_rows, SSD_N), lambda s, q, t: (s, b_blk + q // qpg)),
                      pl.BlockSpec((sb_rows, SSD_N), lambda s, q, t: (s, c_blk + q // qpg)),
                      pl.BlockSpec((sb_rows, 128), lambda s, q, t: (s, 0)),
                      pl.BlockSpec((3, qw), lambda s, q, t: (0, q)),
                      pl.BlockSpec((3, SSD_N), lambda s, q, t: (0, b_blk + q // qpg)),
                      pl.BlockSpec((3, SSD_N), lambda s, q, t: (0, c_blk + q // qpg)),
                      pl.BlockSpec((1, qw), lambda s, q, t: (0, q)),
                      pl.BlockSpec((1, SSD_N), lambda s, q, t: (0, b_blk + q // qpg)),
                      pl.BlockSpec((1, SSD_N), lambda s, q, t: (0, c_blk + q // qpg)),
                      pl.BlockSpec((1, 128, 128), lambda s, q, t: (q, 0, 0)),
                      pl.BlockSpec((1, 8, 128), lambda s, q, t: (q, 0, 0)),
                      pl.BlockSpec((1, 2, SSD_N, qw), lambda s, q, t: (jnp.maximum(s - n_zero, 0), 0, 0, q))],
            out_specs=[pl.BlockSpec((sb_rows, qw), lambda s, q, t: (s, q)),
                       pl.BlockSpec((nch, 2, SSD_N, qw), lambda s, q, t: (s, 0, 0, q))],
            scratch_shapes=[pltpu.VMEM((sb_rows, qw), bf16), pltpu.VMEM((sb_rows, SSD_N), bf16),
                            pltpu.VMEM((sb_rows, SSD_N), bf16), pltpu.VMEM((sb_rows, 128), f32),
                            pltpu.VMEM((sb_rows, 128), f32), pltpu.VMEM((2, SSD_N, qw), f32)]),
        out_shape=[jax.ShapeDtypeStruct((R, SSD_INNER), f32),
                   jax.ShapeDtypeStruct((R // BLK, 2, SSD_N, SSD_INNER), f32)],
        compiler_params=pltpu.CompilerParams(dimension_semantics=("arbitrary", "arbitrary"),
                                             vmem_limit_bytes=SSD_VMEM_BYTES),
        name="ssd_scan",
    )(tbl, xbc, xbc, xbc, dt, cw, cw, cw, cb, cb, cb, sel, hp, s0)


def ssd_tables(p, e):
    nq = SSD_HEADS // SSD_QH
    lane = jnp.arange(128)
    sel = jnp.stack([(lane[:, None] == (q * SSD_QH + lane[None, :])) & (lane[None, :] < SSD_QH)
                     for q in range(nq)]).astype(jnp.bfloat16)
    a_neg = -jnp.exp(p['ssd_a_log'][e].astype(jnp.float32))
    rows = jnp.stack([p['ssd_dt_bias'][e, 0], p['ssd_dt_bias'][e, 1], a_neg[0], a_neg[1], p['ssd_d'][e]])
    hp = jnp.zeros((nq, 8, 128), jnp.float32)
    hp = hp.at[:, :5, :SSD_QH].set(jnp.transpose(rows.reshape(5, nq, SSD_QH), (1, 0, 2)))
    return sel, hp


MIX_VMEM_BYTES = 48 * 1024 * 1024
MIX_BLOCKS = 1
ROUTER_LANES = 128


def _residual_norm_router(x, out, g1, n2g, sc2, sh2, rw_ref, x_out_ref, hn_ref, aff_ref):
    rows, d = x.shape
    nb = g1.shape[0]
    x_new = x.reshape(nb, rows // nb, d) + g1 * out.reshape(nb, rows // nb, d)
    x_out_ref[...] = x_new.reshape(rows, d)
    hn = _adaln(x_new, n2g, sc2, sh2).reshape(rows, d)
    hn_ref[...] = hn.astype(hn_ref.dtype)
    logits = _dot(hn, rw_ref[...], ((1,), (0,)), 3)
    lane = lax.broadcasted_iota(jnp.int32, logits.shape, 1)
    logits = jnp.where(lane < N_EXPERTS, logits, NEG_BIG)
    ex = jnp.exp(logits - jnp.max(logits, axis=-1, keepdims=True))
    aff = ex / jnp.sum(ex, axis=-1, keepdims=True)
    aff_ref[...] = aff.T[:N_EXPERTS]


def _residual_rows(x_refs, n_a):
    if len(x_refs) == 1:
        return x_refs[0][...]
    return jnp.where(pl.program_id(0) < n_a, x_refs[0][...], x_refs[1][...])


def _l0_out_kernel(ys_ref, z_ref, yf_ref, yb_ref, bonus_ref, gate_ref, sg_ref, lnw_ref, lnb_ref, e_ref, et_ref, w_ref,
                   *rest, n_a):
    *x_refs, g1_ref, n2g_ref, sc2_ref, sh2_ref, rw_ref, x_out_ref, hn_ref, aff_ref = rest
    f32, bf16 = jnp.float32, jnp.bfloat16
    z = z_ref[...].astype(f32)
    ys = ys_ref[...] * (z * jax.nn.sigmoid(z))
    gw = SSD_INNER // SSD_GROUPS
    parts = []
    for gi in range(SSD_GROUPS):
        yg = ys[:, gi * gw:(gi + 1) * gw]
        parts.append(yg * lax.rsqrt(jnp.mean(yg * yg, -1, keepdims=True) + NORM_EPS))
    a1 = jnp.concatenate(parts, axis=1) * sg_ref[...]
    o = yf_ref[...].astype(f32) + yb_ref[...].astype(f32)
    mu = _head_sum(o, e_ref, et_ref) * (1.0 / RWKV_N)
    oc = o - mu
    var = _head_sum(oc * oc, e_ref, et_ref) * (1.0 / RWKV_N)
    o = oc * lax.rsqrt(var + RWKV_GN_EPS) * lnw_ref[...] + lnb_ref[...]
    o = (o + bonus_ref[...].astype(f32)) * gate_ref[...].astype(f32)
    out = (jnp.dot(a1.astype(bf16), w_ref[:SSD_INNER], preferred_element_type=f32)
           + jnp.dot(o.astype(bf16), w_ref[SSD_INNER:], preferred_element_type=f32))
    _residual_norm_router(_residual_rows(x_refs, n_a), out, g1_ref[...], n2g_ref[...], sc2_ref[...], sh2_ref[...],
                          rw_ref, x_out_ref, hn_ref, aff_ref)


def _l1_out_kernel(a_ref, w_ref, *rest, n_a):
    *x_refs, g1_ref, n2g_ref, sc2_ref, sh2_ref, rw_ref, x_out_ref, hn_ref, aff_ref = rest
    out = jnp.dot(a_ref[...].astype(jnp.bfloat16), w_ref[...], preferred_element_type=jnp.float32)
    _residual_norm_router(_residual_rows(x_refs, n_a), out, g1_ref[...], n2g_ref[...], sc2_ref[...], sh2_ref[...],
                          rw_ref, x_out_ref, hn_ref, aff_ref)


def _mix_out_call(kernel_fn, name, lead_args, lead_specs, w_out, x, g1b, n2g, sc2b, sh2b, router_w):
    mb = MIX_BLOCKS
    xs = x if isinstance(x, (tuple, list)) else (x,)
    D = xs[0].shape[1]
    R = sum(a.shape[0] for a in xs)
    n_a = xs[0].shape[0] // (mb * BLK)
    assert (R // BLK) % mb == 0 and xs[0].shape[0] % (mb * BLK) == 0
    full = lambda a: pl.BlockSpec(a.shape, lambda i: (0,) * a.ndim)
    blkrow = pl.BlockSpec((mb, 1, D), lambda i: (i, 0, 0))
    rw = jnp.zeros((D, ROUTER_LANES), jnp.float32).at[:, :N_EXPERTS].set(router_w)
    args = list(lead_args) + [w_out, *xs, g1b, n2g, sc2b, sh2b, rw]
    if len(xs) == 1:
        x_specs = [pl.BlockSpec((mb * BLK, D), lambda i: (i, 0))]
        aliases = {len(lead_args) + 1: 0}
    else:
        x_specs = [pl.BlockSpec((mb * BLK, D), lambda i: (jnp.minimum(i, n_a - 1), 0)),
                   pl.BlockSpec((mb * BLK, D), lambda i: (jnp.maximum(i - n_a, 0), 0))]
        aliases = {}
    in_specs = list(lead_specs) + [full(w_out), *x_specs, blkrow, full(n2g), blkrow, blkrow, full(rw)]
    return pl.pallas_call(
        functools.partial(kernel_fn, n_a=n_a),
        grid=(R // (mb * BLK),),
        in_specs=in_specs,
        out_specs=[pl.BlockSpec((mb * BLK, D), lambda i: (i, 0)), pl.BlockSpec((mb * BLK, D), lambda i: (i, 0)),
                   pl.BlockSpec((N_EXPERTS, mb * BLK), lambda i: (0, i))],
        out_shape=[jax.ShapeDtypeStruct((R, D), jnp.float32), jax.ShapeDtypeStruct((R, D), jnp.bfloat16),
                   jax.ShapeDtypeStruct((N_EXPERTS, R), jnp.float32)],
        input_output_aliases=aliases,
        compiler_params=pltpu.CompilerParams(dimension_semantics=("arbitrary",), vmem_limit_bytes=MIX_VMEM_BYTES),
        name=name,
    )(*args)


def l0_out(ys, z, yf, yb, bonus, gate, ssd_g, ln_w, ln_b, e_ind, et_ind, w_out, x, g1b, n2g, sc2b, sh2b, router_w):
    c = RWKV_DIM
    rows = lambda n: pl.BlockSpec((MIX_BLOCKS * BLK, n), lambda i: (i, 0))
    full = lambda a: pl.BlockSpec(a.shape, lambda i: (0,) * a.ndim)
    lead = [ys, z, yf, yb, bonus, gate, ssd_g, ln_w, ln_b, e_ind, et_ind]
    specs = [rows(SSD_INNER), rows(SSD_INNER), rows(c), rows(c), rows(c), rows(c),
             full(ssd_g), full(ln_w), full(ln_b), full(e_ind), full(et_ind)]
    return _mix_out_call(_l0_out_kernel, "l0_out", lead, specs, w_out, x, g1b, n2g, sc2b, sh2b, router_w)


def l1_out(a, w_out, x, g1b, n2g, sc2b, sh2b, router_w):
    specs = [pl.BlockSpec((MIX_BLOCKS * BLK, a.shape[1]), lambda i: (i, 0))]
    return _mix_out_call(_l1_out_kernel, "l1_out", [a], specs, w_out, x, g1b, n2g, sc2b, sh2b, router_w)


RET_HPS = 2


def _l1_in_kernel(tbl_ref, x_ref, g_ref, sc_ref, sh_ref, w_ref, cos_ref, sin_ref, q_ref, k_ref, v_ref, gg_ref):
    f32 = jnp.float32
    hn = _adaln(x_ref[...], g_ref[...], sc_ref[0], sh_ref[0]).astype(jnp.bfloat16)
    cosf, sinf = cos_ref[0], sin_ref[0]

    def rope(x):
        parts = []
        for h in range(RET_HEADS):
            xh = x[:, h * RET_DK:(h + 1) * RET_DK]
            parts.append(xh * cosf + pltpu.roll(xh, RET_DK // 2, 1) * sinf)
        return jnp.concatenate(parts, axis=1)

    q_ref[...] = rope(jnp.dot(hn, w_ref[:, :RET_QK], preferred_element_type=f32)).astype(q_ref.dtype)
    k_ref[...] = (rope(jnp.dot(hn, w_ref[:, RET_QK:2 * RET_QK], preferred_element_type=f32))
                  * (RET_DK ** -0.5)).astype(k_ref.dtype)
    v_ref[...] = jnp.dot(hn, w_ref[:, 2 * RET_QK:2 * RET_QK + RET_V], preferred_element_type=f32).astype(v_ref.dtype)
    gg_ref[...] = jnp.dot(hn, w_ref[:, 2 * RET_QK + RET_V:], preferred_element_type=f32).astype(gg_ref.dtype)


def l1_in(x, g1, scb, shb, w_bf16, cos_t, sin_t, rope_blk):
    R, D = x.shape
    nblk = R // BLK
    tbl = jnp.asarray([rope_blk], jnp.int32)
    f32 = jnp.float32
    row = lambda n: pl.BlockSpec((BLK, n), lambda i, t: (i, 0))
    full = lambda a: pl.BlockSpec(a.shape, lambda i, t: (0,) * a.ndim)
    blkrow = pl.BlockSpec((1, 1, D), lambda i, t: (i, 0, 0))
    ropespec = pl.BlockSpec((1, BLK, RET_DK), lambda i, t: (t[0, i], 0, 0))
    return pl.pallas_call(
        _l1_in_kernel,
        grid_spec=pltpu.PrefetchScalarGridSpec(
            num_scalar_prefetch=1, grid=(nblk,),
            in_specs=[row(D), full(g1), blkrow, blkrow, full(w_bf16), ropespec, ropespec],
            out_specs=[row(RET_QK), row(RET_QK), row(RET_V), row(RET_V)]),
        out_shape=[jax.ShapeDtypeStruct((R, RET_QK), ACT_DTYPE), jax.ShapeDtypeStruct((R, RET_QK), ACT_DTYPE),
                   jax.ShapeDtypeStruct((R, RET_V), ACT_DTYPE), jax.ShapeDtypeStruct((R, RET_V), ACT_DTYPE)],
        compiler_params=pltpu.CompilerParams(dimension_semantics=("arbitrary",), vmem_limit_bytes=L0_VMEM_BYTES),
        name="l1_in",
    )(tbl, x, g1, scb, shb, w_bf16, cos_t, sin_t)


def _ret_kernel(tbl_ref, q_ref, k_ref, v_ref, g_ref, lg_ref, nw_ref, nb_ref, s0_ref, a_ref, fs_ref, st_ref, *,
                n_zero):
    f32, bf16 = jnp.float32, jnp.bfloat16
    sb = pl.program_id(0)
    nch = q_ref.shape[0] // BLK
    heads = range(RET_HPS)
    ks = [slice(h * RET_DK, (h + 1) * RET_DK) for h in heads]
    vs = [slice(h * RET_DV, (h + 1) * RET_DV) for h in heads]
    lgf = [lg_ref[h, 0:1, 0:1] for h in heads]
    lgb = [lg_ref[h, 1:2, 0:1] for h in heads]
    t_i = lax.broadcasted_iota(jnp.int32, (BLK, BLK), 0)
    s_i = lax.broadcasted_iota(jnp.int32, (BLK, BLK), 1)
    dist = (t_i - s_i).astype(f32)
    dm = [jnp.exp(jnp.where(s_i <= t_i, dist * lgf[h], NEG_BIG)) + jnp.exp(jnp.where(s_i >= t_i, -dist * lgb[h], NEG_BIG))
          for h in heads]
    tk = lax.broadcasted_iota(jnp.int32, (BLK, RET_DK), 0).astype(f32)
    tv = lax.broadcasted_iota(jnp.int32, (BLK, RET_DV), 0).astype(f32)
    k_to_end_f = [jnp.exp((BLK - 1.0 - tk) * lgf[h]) for h in heads]
    k_to_end_b = [jnp.exp(tk * lgb[h]) for h in heads]
    from_start_f = [jnp.exp((tv + 1.0) * lgf[h]) for h in heads]
    from_start_b = [jnp.exp((BLK - tv) * lgb[h]) for h in heads]
    nt, tn = (((1,), (1,)), ((), ())), (((0,), (0,)), ((), ()))

    def chunk_rows(c):
        return pl.ds(pl.multiple_of(c * BLK, BLK), BLK)

    def fwd(c, carry):
        blk = sb * nch + c
        rows = chunk_rows(c)

        @pl.when(tbl_ref[0, blk] == 1)
        def _():
            st_ref[0] = jnp.where(sb >= n_zero, s0_ref[0, 0], 0.0)

        q = [q_ref[rows, ks[h]].astype(bf16) for h in heads]
        k = [k_ref[rows, ks[h]] for h in heads]
        v = [v_ref[rows, vs[h]].astype(bf16) for h in heads]
        s_in = [st_ref[0, h] for h in heads]
        g = [lax.dot_general(q[h], k[h].astype(bf16), nt, preferred_element_type=f32) for h in heads]
        y_diag = [jnp.dot((g[h] * dm[h]).astype(bf16), v[h], preferred_element_type=f32) for h in heads]
        y_off = [jnp.dot(q[h], s_in[h].astype(bf16), preferred_element_type=f32) for h in heads]
        kw = [(k[h].astype(f32) * k_to_end_f[h]).astype(bf16) for h in heads]
        upd = [lax.dot_general(kw[h], v[h], tn, preferred_element_type=f32) for h in heads]
        for h in heads:
            a_ref[rows, vs[h]] = y_diag[h] + from_start_f[h] * y_off[h]
            st_ref[0, h] = jnp.exp(BLK * lgf[h]) * s_in[h] + upd[h]
        fs_ref[c, 0] = st_ref[0]
        return carry

    lax.fori_loop(0, nch, fwd, 0)

    def bwd(j, carry):
        c = nch - 1 - j
        blk = sb * nch + c
        rows = chunk_rows(c)

        @pl.when(tbl_ref[1, blk] == 1)
        def _():
            st_ref[1] = jnp.where(sb >= n_zero, s0_ref[0, 1], 0.0)

        q = [q_ref[rows, ks[h]].astype(bf16) for h in heads]
        v = [v_ref[rows, vs[h]].astype(bf16) for h in heads]
        s_in = [st_ref[1, h] for h in heads]
        y_off = [jnp.dot(q[h], s_in[h].astype(bf16), preferred_element_type=f32) for h in heads]
        kw = [(k_ref[rows, ks[h]].astype(f32) * k_to_end_b[h]).astype(bf16) for h in heads]
        upd = [lax.dot_general(kw[h], v[h], tn, preferred_element_type=f32) for h in heads]
        for h in heads:
            st_ref[1, h] = jnp.exp(BLK * lgb[h]) * s_in[h] + upd[h]
            y = a_ref[rows, vs[h]] + from_start_b[h] * y_off[h]
            mu = jnp.mean(y, -1, keepdims=True)
            yc = y - mu
            var = jnp.mean(yc * yc, -1, keepdims=True)
            gg = g_ref[rows, vs[h]].astype(f32)
            a_ref[rows, vs[h]] = ((yc * lax.rsqrt(var + 1e-5) * nw_ref[:, vs[h]] + nb_ref[:, vs[h]])
                                  * (gg * jax.nn.sigmoid(gg)))
        fs_ref[c, 1] = st_ref[1]
        return carry

    lax.fori_loop(0, nch, bwd, 0)


def ret_scan(q, k, v, g, lg_tab, norm_w, norm_b, s0, n_zero, first, last, sb_rows):
    R = q.shape[0]
    n_sb = R // sb_rows
    nch = sb_rows // BLK
    tbl = jnp.asarray([first, last], jnp.int32)
    f32 = jnp.float32
    hps = RET_HPS
    return pl.pallas_call(
        functools.partial(_ret_kernel, n_zero=n_zero),
        grid_spec=pltpu.PrefetchScalarGridSpec(
            num_scalar_prefetch=1, grid=(n_sb, RET_HEADS // hps),
            in_specs=[pl.BlockSpec((sb_rows, hps * RET_DK), lambda s, h, t: (s, h)),
                      pl.BlockSpec((sb_rows, hps * RET_DK), lambda s, h, t: (s, h)),
                      pl.BlockSpec((sb_rows, hps * RET_DV), lambda s, h, t: (s, h)),
                      pl.BlockSpec((sb_rows, hps * RET_DV), lambda s, h, t: (s, h)),
                      pl.BlockSpec((hps, 8, 128), lambda s, h, t: (h, 0, 0)),
                      pl.BlockSpec((1, hps * RET_DV), lambda s, h, t: (0, h)),
                      pl.BlockSpec((1, hps * RET_DV), lambda s, h, t: (0, h)),
                      pl.BlockSpec((1, 2, hps, RET_DK, RET_DV), lambda s, h, t: (jnp.maximum(s - n_zero, 0), 0, h, 0, 0))],
            out_specs=[pl.BlockSpec((sb_rows, hps * RET_DV), lambda s, h, t: (s, h)),
                       pl.BlockSpec((nch, 2, hps, RET_DK, RET_DV), lambda s, h, t: (s, 0, h, 0, 0))],
            scratch_shapes=[pltpu.VMEM((2, hps, RET_DK, RET_DV), f32)]),
        out_shape=[jax.ShapeDtypeStruct((R, RET_V), f32),
                   jax.ShapeDtypeStruct((R // BLK, 2, RET_HEADS, RET_DK, RET_DV), f32)],
        compiler_params=pltpu.CompilerParams(dimension_semantics=("arbitrary", "arbitrary"),
                                             vmem_limit_bytes=SSD_VMEM_BYTES),
        name="ret_scan",
    )(tbl, q, k, v, g, lg_tab, norm_w.reshape(1, -1), norm_b.reshape(1, -1), s0)


def _mod_kernel(c_ref, w_ref, b_ref, o_ref):
    c = c_ref[...]
    act = c * jax.nn.sigmoid(c)
    o_ref[0] = _dot(act, w_ref[0], ((1,), (0,)), 3) + b_ref[0]


def mod_vectors(conds, mod_w, mod_b):
    depth, D, n6 = mod_w.shape
    tn = D
    return pl.pallas_call(
        _mod_kernel,
        grid=(depth, n6 // tn),
        in_specs=[pl.BlockSpec(conds.shape, lambda i, j: (0, 0)),
                  pl.BlockSpec((1, D, tn), lambda i, j: (i, 0, j)),
                  pl.BlockSpec((1, 1, tn), lambda i, j: (i, 0, j))],
        out_specs=pl.BlockSpec((1, conds.shape[0], tn), lambda i, j: (i, 0, j)),
        out_shape=jax.ShapeDtypeStruct((depth, conds.shape[0], n6), jnp.float32),
        name="mod_vectors",
    )(conds, mod_w, mod_b.reshape(depth, 1, n6))


def rope_tables(n_tokens):
    rows = n_tokens // GRID_W
    row = np.repeat(np.arange(rows), GRID_W).astype(np.float64)
    col = np.tile(np.arange(GRID_W), rows).astype(np.float64)
    n_f = RET_DK // 4
    inv = ROPE_BASE ** (-np.arange(n_f, dtype=np.float64) / n_f)
    ang = np.concatenate([row[:, None] * inv, col[:, None] * inv], -1)
    return np.cos(ang), np.sin(ang)


def _rope_block_tables(n_ctx, n_lat, l_lat):
    nb = l_lat // BLK
    cos, sin = rope_tables(l_lat)
    cosf = np.concatenate([cos, cos], -1).reshape(nb, BLK, RET_DK)
    sinf = np.concatenate([-sin, sin], -1).reshape(nb, BLK, RET_DK)
    cos_t = jnp.asarray(np.concatenate([np.ones((1, BLK, RET_DK)), cosf]), jnp.float32)
    sin_t = jnp.asarray(np.concatenate([np.zeros((1, BLK, RET_DK)), sinf]), jnp.float32)
    rope_blk = [0] * n_ctx + [1 + j for _ in range(n_lat) for j in range(nb)]
    return cos_t, sin_t, rope_blk


def kernel(x_prompt, x_sample, state_ssd, state_rwkv, state_ret, c, c_ctx, mod_w, mod_b, norm1_g, norm2_g,
           router_w, exp_w_gate, exp_w_up, exp_w_down, ab_w_in, ab_w_out, ssd_conv_w, ssd_conv_b, ssd_dt_bias,
           ssd_a_log, ssd_d, ssd_norm_g, rwkv_mu_prev, rwkv_mu_next, rwkv_w0, rwkv_w2, rwkv_a0, rwkv_a2, rwkv_g2,
           rwkv_k_k, rwkv_k_a, rwkv_r_k, rwkv_ln_w, rwkv_ln_b, ret_w_in, ret_w_out, ret_decay_logit, ret_norm_w,
           ret_norm_b, final_norm_g):
    p = dict(mod_w=mod_w, mod_b=mod_b, norm1_g=norm1_g, norm2_g=norm2_g, router_w=router_w,
             exp_w_gate=exp_w_gate, exp_w_up=exp_w_up, exp_w_down=exp_w_down, ab_w_in=ab_w_in, ab_w_out=ab_w_out,
             ssd_conv_w=ssd_conv_w, ssd_conv_b=ssd_conv_b, ssd_dt_bias=ssd_dt_bias, ssd_a_log=ssd_a_log,
             ssd_d=ssd_d, ssd_norm_g=ssd_norm_g, rwkv_mu_prev=rwkv_mu_prev, rwkv_mu_next=rwkv_mu_next,
             rwkv_w0=rwkv_w0, rwkv_w2=rwkv_w2, rwkv_a0=rwkv_a0, rwkv_a2=rwkv_a2, rwkv_g2=rwkv_g2,
             rwkv_k_k=rwkv_k_k, rwkv_k_a=rwkv_k_a, rwkv_r_k=rwkv_r_k, rwkv_ln_w=rwkv_ln_w, rwkv_ln_b=rwkv_ln_b,
             ret_w_in=ret_w_in, ret_w_out=ret_w_out, ret_decay_logit=ret_decay_logit, ret_norm_w=ret_norm_w,
             ret_norm_b=ret_norm_b, final_norm_g=final_norm_g)
    f32, bf16 = jnp.float32, jnp.bfloat16
    n_ctx, l_ctx, D = x_prompt.shape
    n_lat, l_lat, _ = x_sample.shape
    assert l_ctx == BLK and l_lat % BLK == 0 and (n_ctx * BLK) % l_lat == 0
    n_sb_ctx = n_ctx * BLK // l_lat
    cond_id, first, last = _seq_tables(n_ctx, n_lat, l_lat)
    x = (x_prompt.reshape(-1, D), x_sample.reshape(-1, D))

    conds = jnp.concatenate([c_ctx[None, :], c, jnp.zeros((8 - 1 - n_lat, D), f32)])
    mods = mod_vectors(conds, mod_w, mod_b)[:, jnp.asarray(cond_id)]
    mods = mods.reshape(DEPTH, len(cond_id), 6, 1, D)

    new_ssd, new_rwkv, new_ret = [], [], []
    out = None
    for i in range(DEPTH):
        sh1, sc1, g1, sh2, sc2, g2 = (mods[i, :, k] for k in range(6))
        e = i // 2
        if i % 2 == 0:
            w_packed, mup, mun, rwp, w2bd, a2p, g2p, e_ind, et_ind = l0_pack_weights(p, e)
            z, xbc, dt, r, v, an, lw, kd, bv, gate, bonus = l0_in(
                x, norm1_g[i][None], sc1, sh1, w_packed, mup, mun, rwp, w2bd, a2p, g2p, e_ind, et_ind, first, last)
            sel, hp = ssd_tables(p, e)
            s0_ssd = jnp.transpose(state_ssd[:, e], (0, 1, 3, 2, 4)).reshape(n_lat, 2, SSD_N, SSD_INNER)
            ys, fs_ssd = ssd_scan(xbc, dt, ssd_conv_w[e], ssd_conv_b[e], sel, hp,
                                  s0_ssd, n_sb_ctx, first, last, l_lat)
            new_ssd.append(jnp.transpose(fs_ssd[:n_ctx].reshape(n_ctx, 2, SSD_N, SSD_HEADS, SSD_P), (0, 1, 3, 2, 4)))
            s0_rwkv = jnp.transpose(state_rwkv[:, e], (0, 1, 3, 2, 4)).reshape(n_lat, 2, RWKV_N, RWKV_DIM)
            yf, yb, sf_rwkv = rwkv_scan_pallas(r, v, an, lw, kd, bv, s0_rwkv, n_ctx,
                                               _rwkv_steps(n_ctx, n_lat, l_lat))
            new_rwkv.append(jnp.transpose(sf_rwkv[:n_ctx].reshape(n_ctx, 2, RWKV_N, RWKV_HEADS, RWKV_N),
                                          (0, 1, 3, 2, 4)))
            x, hn2, affT = l0_out(ys, z, yf, yb, bonus, gate, ssd_norm_g[e][None], rwkv_ln_w[e][None], rwkv_ln_b[e][None],
                                  e_ind, et_ind, ab_w_out[e].astype(bf16), x, g1, norm2_g[i][None], sc2, sh2,
                                  router_w[i])
        else:
            cos_t, sin_t, rope_blk = _rope_block_tables(n_ctx, n_lat, l_lat)
            q, k, v, gg = l1_in(x, norm1_g[i][None], sc1, sh1, ret_w_in[e].astype(bf16), cos_t, sin_t, rope_blk)
            lg = jax.nn.log_sigmoid(ret_decay_logit[e].astype(f32))
            lg_tab = jnp.zeros((RET_HEADS, 8, 128), f32).at[:, :2, :].set(jnp.transpose(lg)[:, :, None])
            a, fs_ret = ret_scan(q, k, v, gg, lg_tab, ret_norm_w[e], ret_norm_b[e],
                                 state_ret[:, e], n_sb_ctx, first, last, l_lat)
            new_ret.append(fs_ret[:n_ctx])
            x, hn2, affT = l1_out(a, ret_w_out[e].astype(bf16), x, g1, norm2_g[i][None], sc2, sh2, router_w[i])
        fin = final_norm_g if i == DEPTH - 1 else None
        out = moe_layer(x, hn2, affT, g2, exp_w_gate, exp_w_up, exp_w_down, i, n_ctx, l_lat, final_g=fin)
        if fin is None:
            x = out
    y_ctx, y_lat = out
    return (y_ctx.reshape(n_ctx, l_ctx, D), y_lat.reshape(n_lat, l_lat, D),
            jnp.stack(new_ssd, 1), jnp.stack(new_rwkv, 1), jnp.stack(new_ret, 1))
```
